```python
import math
import jax, jax.numpy as jnp
from jax import lax
import numpy as np

D_MODEL = 2048
BATCH = 8
SEQ = 8192
DEPTH = 1

GDN_HEADS = 8
GDN_HEAD_DIM = 128
GDN_CONV = 4
GDN_CHUNK = 64
GDN_QK = GDN_HEADS * GDN_HEAD_DIM
GDN_V = GDN_HEADS * GDN_HEAD_DIM
GDN_CONV_CH = 2 * GDN_QK + GDN_V
MLA_HEADS = 8
QK_NOPE = 128
QK_ROPE = 64
V_HEAD = 128
Q_LORA = 512
KV_LORA = 512
ROPE_THETA = 10000.0
Q_BLOCK = 128
MAX_POS_OFFSET = 1024
MIX_WIDTH = GDN_V + MLA_HEADS * V_HEAD
D_FF = ((8 * D_MODEL + 3 * 256 - 1) // (3 * 256)) * 256
EPS = 1e-6
IN_WIDTH = GDN_CONV_CH + GDN_V + 2 * GDN_HEADS + Q_LORA + KV_LORA + QK_ROPE
IN_SPLITS = (
    GDN_CONV_CH,
    GDN_CONV_CH + GDN_V,
    GDN_CONV_CH + GDN_V + GDN_HEADS,
    GDN_CONV_CH + GDN_V + 2 * GDN_HEADS,
    GDN_CONV_CH + GDN_V + 2 * GDN_HEADS + Q_LORA,
    GDN_CONV_CH + GDN_V + 2 * GDN_HEADS + Q_LORA + KV_LORA,
)

kernel_name = "hymba_gdn_mla_swiglu_layer"


def rms_norm(x, w):
    xf = x.astype(jnp.float32)
    y = xf * lax.rsqrt(jnp.mean(xf * xf, axis=-1, keepdims=True) + EPS)
    return (y * w.astype(jnp.float32)).astype(x.dtype)


def l2_normalize(x):
    return x * lax.rsqrt(jnp.sum(x * x, axis=-1, keepdims=True) + EPS)


def rotary(x, positions):
    half = x.shape[-1] // 2
    inv_freq = ROPE_THETA ** (-jnp.arange(half, dtype=jnp.float32) / half)
    ang = positions.astype(jnp.float32)[:, :, None, None] * inv_freq
    cos, sin = jnp.cos(ang), jnp.sin(ang)
    xf = x.astype(jnp.float32)
    x1, x2 = xf[..., :half], xf[..., half:]
    out = jnp.concatenate([x1 * cos - x2 * sin, x2 * cos + x1 * sin], axis=-1)
    return out.astype(x.dtype)


def causal_short_conv(u, w):
    k, c = w.shape
    out = lax.conv_general_dilated(
        u, w[:, None, :].astype(u.dtype), window_strides=(1,), padding=[(k - 1, 0)],
        dimension_numbers=("NWC", "WIO", "NWC"), feature_group_count=c)
    return jax.nn.silu(out)


def gated_delta_rule_chunked(q, k, v, g, beta):
    b, s, h, dk = q.shape
    dv = v.shape[-1]
    c = GDN_CHUNK
    n = s // c
    f32 = jnp.float32
    q = l2_normalize(q.astype(f32)) * (dk ** -0.5)
    k = l2_normalize(k.astype(f32))
    v = v.astype(f32)

    def chunks(t):
        return t.reshape(b, n, c, h, t.shape[-1]).transpose(0, 3, 1, 2, 4)

    q, k, v = chunks(q), chunks(k), chunks(v)
    g = g.astype(f32).reshape(b, n, c, h).transpose(0, 3, 1, 2)
    beta = beta.astype(f32).reshape(b, n, c, h).transpose(0, 3, 1, 2)
    gc = jnp.cumsum(g, axis=-1)

    idx = jnp.arange(c)
    tril = idx[:, None] >= idx[None, :]
    strict = idx[:, None] > idx[None, :]
    decay = jnp.exp(jnp.where(tril, gc[..., :, None] - gc[..., None, :], -jnp.inf))

    kb = k * beta[..., None]
    vb = v * beta[..., None]
    a_mat = jnp.where(strict, jnp.einsum("bhncd,bhnjd->bhncj", kb, k) * decay, 0.0)
    lhs = a_mat + jnp.eye(c, dtype=f32)
    rhs = jnp.concatenate([vb, kb * jnp.exp(gc)[..., None]], axis=-1)
    sol = lax.linalg.triangular_solve(lhs, rhs, left_side=True, lower=True, unit_diagonal=True)
    u, w = sol[..., :dv], sol[..., dv:]
    intra = jnp.einsum("bhncd,bhnjd->bhncj", q, k) * decay

    def step(state, inp):
        q_c, k_c, u_c, w_c, gc_c, intra_c = inp
        v_new = u_c - jnp.einsum("bhcd,bhdv->bhcv", w_c, state)
        o = (jnp.einsum("bhcd,bhdv->bhcv", q_c * jnp.exp(gc_c)[..., None], state)
             + jnp.einsum("bhcj,bhjv->bhcv", intra_c, v_new))
        g_last = gc_c[..., -1]
        k_dec = k_c * jnp.exp(g_last[..., None] - gc_c)[..., None]
        state = state * jnp.exp(g_last)[..., None, None] + jnp.einsum("bhcd,bhcv->bhdv", k_dec, v_new)
        return state, o

    xs = tuple(jnp.moveaxis(t, 2, 0) for t in (q, k, u, w, gc, intra))
    state0 = jnp.zeros((b, h, dk, dv), f32)
    _, o = lax.scan(step, state0, xs)
    return o.transpose(1, 0, 3, 2, 4).reshape(b, s, h, dv)


def blocked_causal_mla_attention(q_nope, q_rope, k_nope, k_rope, v):
    b, s, h, dn = q_nope.shape
    nb = s // Q_BLOCK
    scale = (QK_NOPE + QK_ROPE) ** -0.5
    qn = q_nope.reshape(b, nb, Q_BLOCK, h, dn).swapaxes(0, 1)
    qr = q_rope.reshape(b, nb, Q_BLOCK, h, QK_ROPE).swapaxes(0, 1)
    key_pos = jnp.arange(s)

    def block(args):
        qn_b, qr_b, start = args
        sc = (jnp.einsum("bqhd,bkhd->bhqk", qn_b, k_nope)
              + jnp.einsum("bqhr,bkr->bhqk", qr_b, k_rope)).astype(jnp.float32) * scale
        q_pos = start + jnp.arange(Q_BLOCK)
        sc = jnp.where(key_pos[None, :] <= q_pos[:, None], sc, -jnp.inf)
        p = jax.nn.softmax(sc, axis=-1).astype(v.dtype)
        return jnp.einsum("bhqk,bkhd->bqhd", p, v)

    out = lax.map(block, (qn, qr, jnp.arange(nb) * Q_BLOCK))
    return out.swapaxes(0, 1).reshape(b, s, h, v.shape[-1])


def _fwd_setup_inputs(seed: int = 0) -> dict:
    key = jax.random.key(seed)
    ks = jax.random.split(key, 20)
    L = DEPTH
    f32 = jnp.float32

    def nrm(k, shape, fan_in):
        return jax.random.normal(k, shape, f32) * fan_in ** -0.5

    def gain(k, shape):
        return 1.0 + 0.02 * jax.random.normal(k, shape, f32)

    x = jax.random.normal(ks[0], (BATCH, SEQ, D_MODEL), f32)
    offsets = jax.random.randint(ks[1], (BATCH, 1), 0, MAX_POS_OFFSET, dtype=jnp.int32)
    positions = offsets + jnp.arange(SEQ, dtype=jnp.int32)[None, :]
    attn_norm_w = gain(ks[2], (L, D_MODEL))
    w_in = nrm(ks[3], (L, D_MODEL, IN_WIDTH), D_MODEL)
    conv_w = nrm(ks[4], (L, GDN_CONV, GDN_CONV_CH), GDN_CONV)
    a_log = jnp.log(jax.random.uniform(ks[5], (L, GDN_HEADS), f32, 1.0, 16.0))
    dt = jnp.exp(jax.random.uniform(ks[6], (L, GDN_HEADS), f32, math.log(1e-3), math.log(1e-1)))
    dt_bias = dt + jnp.log(-jnp.expm1(-dt))
    gdn_norm_w = gain(ks[7], (L, GDN_HEAD_DIM))
    q_norm_w = gain(ks[8], (L, Q_LORA))
    w_uq = nrm(ks[9], (L, Q_LORA, MLA_HEADS * (QK_NOPE + QK_ROPE)), Q_LORA)
    kv_norm_w = gain(ks[10], (L, KV_LORA))
    w_ukv = nrm(ks[11], (L, KV_LORA, MLA_HEADS * (QK_NOPE + V_HEAD)), KV_LORA)
    mla_out_norm_w = gain(ks[12], (L, V_HEAD))
    w_out = nrm(ks[13], (L, MIX_WIDTH, D_MODEL), MIX_WIDTH)
    ffn_norm_w = gain(ks[14], (L, D_MODEL))
    w_gate = nrm(ks[15], (L, D_MODEL, D_FF), D_MODEL)
    w_up = nrm(ks[16], (L, D_MODEL, D_FF), D_MODEL)
    w_down = nrm(ks[17], (L, D_FF, D_MODEL), D_FF)
    final_norm_w = gain(ks[18], (D_MODEL,))
    return {"x": x, "positions": positions, "attn_norm_w": attn_norm_w, "w_in": w_in,
            "conv_w": conv_w, "a_log": a_log, "dt_bias": dt_bias, "gdn_norm_w": gdn_norm_w,
            "q_norm_w": q_norm_w, "w_uq": w_uq, "kv_norm_w": kv_norm_w, "w_ukv": w_ukv,
            "mla_out_norm_w": mla_out_norm_w, "w_out": w_out, "ffn_norm_w": ffn_norm_w,
            "w_gate": w_gate, "w_up": w_up, "w_down": w_down, "final_norm_w": final_norm_w}


def _fwd_reference(x, positions, attn_norm_w, w_in, conv_w, a_log, dt_bias, gdn_norm_w,
              q_norm_w, w_uq, kv_norm_w, w_ukv, mla_out_norm_w, w_out, ffn_norm_w,
              w_gate, w_up, w_down, final_norm_w):
    b, s, _ = x.shape
    for l in range(DEPTH):
        h = rms_norm(x, attn_norm_w[l])
        proj = h @ w_in[l]
        qkv_pre, z, b_raw, a_raw, cq, ckv, kr = jnp.split(proj, IN_SPLITS, axis=-1)

        qkv = causal_short_conv(qkv_pre, conv_w[l])
        gq, gk, gv = jnp.split(qkv, (GDN_QK, 2 * GDN_QK), axis=-1)
        gq = gq.reshape(b, s, GDN_HEADS, GDN_HEAD_DIM)
        gk = gk.reshape(b, s, GDN_HEADS, GDN_HEAD_DIM)
        gv = gv.reshape(b, s, GDN_HEADS, GDN_HEAD_DIM)
        beta = jax.nn.sigmoid(b_raw.astype(jnp.float32))
        g = -jnp.exp(a_log[l].astype(jnp.float32)) * jax.nn.softplus(
            a_raw.astype(jnp.float32) + dt_bias[l].astype(jnp.float32))
        o_gdn = gated_delta_rule_chunked(gq, gk, gv, g, beta).astype(x.dtype)
        o_gdn = rms_norm(o_gdn, gdn_norm_w[l]) * jax.nn.silu(z.reshape(b, s, GDN_HEADS, GDN_HEAD_DIM))

        q = (rms_norm(cq, q_norm_w[l]) @ w_uq[l]).reshape(b, s, MLA_HEADS, QK_NOPE + QK_ROPE)
        q_nope, q_rope = q[..., :QK_NOPE], rotary(q[..., QK_NOPE:], positions)
        kv = (rms_norm(ckv, kv_norm_w[l]) @ w_ukv[l]).reshape(b, s, MLA_HEADS, QK_NOPE + V_HEAD)
        k_nope, v = kv[..., :QK_NOPE], kv[..., QK_NOPE:]
        k_rope = rotary(kr[:, :, None, :], positions)[:, :, 0, :]
        o_mla = blocked_causal_mla_attention(q_nope, q_rope, k_nope, k_rope, v)
        o_mla = rms_norm(o_mla, mla_out_norm_w[l])

        mixed = jnp.concatenate([o_gdn.reshape(b, s, GDN_V), o_mla.reshape(b, s, MLA_HEADS * V_HEAD)], axis=-1)
        x = x + mixed @ w_out[l]

        h = rms_norm(x, ffn_norm_w[l])
        x = x + (jax.nn.silu(h @ w_gate[l]) * (h @ w_up[l])) @ w_down[l]
    return rms_norm(x, final_norm_w)


import jax as _jax
import jax.numpy as _jnp

TWIN_FORMAT = 'train_step'
FWD_PARAMS = ['x', 'positions', 'attn_norm_w', 'w_in', 'conv_w', 'a_log', 'dt_bias', 'gdn_norm_w', 'q_norm_w', 'w_uq', 'kv_norm_w', 'w_ukv', 'mla_out_norm_w', 'w_out', 'ffn_norm_w', 'w_gate', 'w_up', 'w_down', 'final_norm_w']
TWIN_WEIGHTS = ['attn_norm_w', 'w_in', 'conv_w', 'a_log', 'dt_bias', 'gdn_norm_w', 'q_norm_w', 'w_uq', 'kv_norm_w', 'w_ukv', 'mla_out_norm_w', 'w_out', 'ffn_norm_w', 'w_gate', 'w_up', 'w_down', 'final_norm_w']
TWIN_DIFF_INPUT = 'x'
TWIN_INPUTS = ['x', 'positions', 'attn_norm_w', 'w_in', 'conv_w', 'a_log', 'dt_bias', 'gdn_norm_w', 'q_norm_w', 'w_uq', 'kv_norm_w', 'w_ukv', 'mla_out_norm_w', 'w_out', 'ffn_norm_w', 'w_gate', 'w_up', 'w_down', 'final_norm_w', 'loss_target', 'm_attn_norm_w', 'm_w_in', 'm_conv_w', 'm_a_log', 'm_dt_bias', 'm_gdn_norm_w', 'm_q_norm_w', 'm_w_uq', 'm_kv_norm_w', 'm_w_ukv', 'm_mla_out_norm_w', 'm_w_out', 'm_ffn_norm_w', 'm_w_gate', 'm_w_up', 'm_w_down', 'm_final_norm_w', 'v_attn_norm_w', 'v_w_in', 'v_conv_w', 'v_a_log', 'v_dt_bias', 'v_gdn_norm_w', 'v_q_norm_w', 'v_w_uq', 'v_kv_norm_w', 'v_w_ukv', 'v_mla_out_norm_w', 'v_w_out', 'v_ffn_norm_w', 'v_w_gate', 'v_w_up', 'v_w_down', 'v_final_norm_w']
TWIN_OUTPUTS = ['loss', 'grad_x', 'grad_attn_norm_w', 'grad_w_in', 'grad_conv_w', 'grad_a_log', 'grad_dt_bias', 'grad_gdn_norm_w', 'grad_q_norm_w', 'grad_w_uq', 'grad_kv_norm_w', 'grad_w_ukv', 'grad_mla_out_norm_w', 'grad_w_out', 'grad_ffn_norm_w', 'grad_w_gate', 'grad_w_up', 'grad_w_down', 'grad_final_norm_w', 'delta_attn_norm_w', 'delta_w_in', 'delta_conv_w', 'delta_a_log', 'delta_dt_bias', 'delta_gdn_norm_w', 'delta_q_norm_w', 'delta_w_uq', 'delta_kv_norm_w', 'delta_w_ukv', 'delta_mla_out_norm_w', 'delta_w_out', 'delta_ffn_norm_w', 'delta_w_gate', 'delta_w_up', 'delta_w_down', 'delta_final_norm_w', 'new_m_attn_norm_w', 'new_m_w_in', 'new_m_conv_w', 'new_m_a_log', 'new_m_dt_bias', 'new_m_gdn_norm_w', 'new_m_q_norm_w', 'new_m_w_uq', 'new_m_kv_norm_w', 'new_m_w_ukv', 'new_m_mla_out_norm_w', 'new_m_w_out', 'new_m_ffn_norm_w', 'new_m_w_gate', 'new_m_w_up', 'new_m_w_down', 'new_m_final_norm_w', 'new_v_attn_norm_w', 'new_v_w_in', 'new_v_conv_w', 'new_v_a_log', 'new_v_dt_bias', 'new_v_gdn_norm_w', 'new_v_q_norm_w', 'new_v_w_uq', 'new_v_kv_norm_w', 'new_v_w_ukv', 'new_v_mla_out_norm_w', 'new_v_w_out', 'new_v_ffn_norm_w', 'new_v_w_gate', 'new_v_w_up', 'new_v_w_down', 'new_v_final_norm_w']
TWIN_LEAF_KINDS = {'loss': 'loss', 'grad_x': 'grad_x', 'grad_attn_norm_w': 'grad_w', 'grad_w_in': 'grad_w', 'grad_conv_w': 'grad_w', 'grad_a_log': 'grad_w', 'grad_dt_bias': 'grad_w', 'grad_gdn_norm_w': 'grad_w', 'grad_q_norm_w': 'grad_w', 'grad_w_uq': 'grad_w', 'grad_kv_norm_w': 'grad_w', 'grad_w_ukv': 'grad_w', 'grad_mla_out_norm_w': 'grad_w', 'grad_w_out': 'grad_w', 'grad_ffn_norm_w': 'grad_w', 'grad_w_gate': 'grad_w', 'grad_w_up': 'grad_w', 'grad_w_down': 'grad_w', 'grad_final_norm_w': 'grad_w', 'delta_attn_norm_w': 'delta_w', 'delta_w_in': 'delta_w', 'delta_conv_w': 'delta_w', 'delta_a_log': 'delta_w', 'delta_dt_bias': 'delta_w', 'delta_gdn_norm_w': 'delta_w', 'delta_q_norm_w': 'delta_w', 'delta_w_uq': 'delta_w', 'delta_kv_norm_w': 'delta_w', 'delta_w_ukv': 'delta_w', 'delta_mla_out_norm_w': 'delta_w', 'delta_w_out': 'delta_w', 'delta_ffn_norm_w': 'delta_w', 'delta_w_gate': 'delta_w', 'delta_w_up': 'delta_w', 'delta_w_down': 'delta_w', 'delta_final_norm_w': 'delta_w', 'new_m_attn_norm_w': 'new_m', 'new_m_w_in': 'new_m', 'new_m_conv_w': 'new_m', 'new_m_a_log': 'new_m', 'new_m_dt_bias': 'new_m', 'new_m_gdn_norm_w': 'new_m', 'new_m_q_norm_w': 'new_m', 'new_m_w_uq': 'new_m', 'new_m_kv_norm_w': 'new_m', 'new_m_w_ukv': 'new_m', 'new_m_mla_out_norm_w': 'new_m', 'new_m_w_out': 'new_m', 'new_m_ffn_norm_w': 'new_m', 'new_m_w_gate': 'new_m', 'new_m_w_up': 'new_m', 'new_m_w_down': 'new_m', 'new_m_final_norm_w': 'new_m', 'new_v_attn_norm_w': 'new_v', 'new_v_w_in': 'new_v', 'new_v_conv_w': 'new_v', 'new_v_a_log': 'new_v', 'new_v_dt_bias': 'new_v', 'new_v_gdn_norm_w': 'new_v', 'new_v_q_norm_w': 'new_v', 'new_v_w_uq': 'new_v', 'new_v_kv_norm_w': 'new_v', 'new_v_w_ukv': 'new_v', 'new_v_mla_out_norm_w': 'new_v', 'new_v_w_out': 'new_v', 'new_v_ffn_norm_w': 'new_v', 'new_v_w_gate': 'new_v', 'new_v_w_up': 'new_v', 'new_v_w_down': 'new_v', 'new_v_final_norm_w': 'new_v'}


def _forward(args):
    return _fwd_reference(*[args[k] for k in FWD_PARAMS])


def _output_shape():
    def fwd():
        inp = _fwd_setup_inputs(0)
        return _fwd_reference(*[inp[k] for k in FWD_PARAMS])
    out = _jax.eval_shape(fwd)
    return out.shape, out.dtype

N_MICROBATCH = 1
ADAM_LR = 0.001
ADAM_B1 = 0.9
ADAM_B2 = 0.999
ADAM_EPS = 1e-08
ADAM_WD = 0.01
ADAM_STEP = 10
PER_EXAMPLE_BATCH_AXIS = {'x': 0, 'positions': 0, 'loss_target': 0}
SHARED_INPUTS = []
_WEIGHT_DTYPES = {'attn_norm_w': _jnp.float32, 'w_in': _jnp.float32, 'conv_w': _jnp.float32, 'a_log': _jnp.float32, 'dt_bias': _jnp.float32, 'gdn_norm_w': _jnp.float32, 'q_norm_w': _jnp.float32, 'w_uq': _jnp.float32, 'kv_norm_w': _jnp.float32, 'w_ukv': _jnp.float32, 'mla_out_norm_w': _jnp.float32, 'w_out': _jnp.float32, 'ffn_norm_w': _jnp.float32, 'w_gate': _jnp.float32, 'w_up': _jnp.float32, 'w_down': _jnp.float32, 'final_norm_w': _jnp.float32}
MOMENT_SCALE = {'attn_norm_w': 1.496904e-01, 'w_in': 8.883678e-02, 'conv_w': 4.726361e-02, 'a_log': 2.319908e-01, 'dt_bias': 2.255771e-01, 'gdn_norm_w': 1.695164e-01, 'q_norm_w': 1.661611e-01, 'w_uq': 9.036752e-02, 'kv_norm_w': 2.166679e-01, 'w_ukv': 1.040458e-01, 'mla_out_norm_w': 3.141653e-01, 'w_out': 8.748866e-02, 'ffn_norm_w': 8.066898e-02, 'w_gate': 3.285687e-02, 'w_up': 3.174287e-02, 'w_down': 5.267111e-02, 'final_norm_w': 3.197732e+01}


def _to_microbatches(a, axis):
    t = _jnp.moveaxis(a, axis, 0)
    t = t.reshape((N_MICROBATCH, t.shape[0] // N_MICROBATCH) + t.shape[1:])
    return _jnp.moveaxis(t, 1, axis + 1)


def setup_inputs(seed: int = 0) -> dict:
    inp = _fwd_setup_inputs(seed)
    key = _jax.random.fold_in(_jax.random.key(seed), 7919)
    shape, _ = _output_shape()
    out = dict(inp)
    out["loss_target"] = _jax.random.normal(_jax.random.fold_in(key, 0), shape, _jnp.float32)
    for i, name in enumerate(TWIN_WEIGHTS):
        w = inp[name].astype(_jnp.float32)
        if MOMENT_SCALE is None:
            s = _jnp.sqrt(_jnp.mean(_jnp.square(w)) + 1e-30)
        else:
            s = MOMENT_SCALE[name]
        km, kv = _jax.random.split(_jax.random.fold_in(key, i + 1))
        out[name] = w
        out["m_" + name] = s * _jax.random.normal(km, w.shape, _jnp.float32)
        out["v_" + name] = (s * s) * _jax.random.uniform(kv, w.shape, _jnp.float32, 0.5, 1.5)
    if N_MICROBATCH > 1:
        for name, axis in PER_EXAMPLE_BATCH_AXIS.items():
            out[name] = _to_microbatches(out[name], axis)
    return {'x': out['x'], 'positions': out['positions'], 'attn_norm_w': out['attn_norm_w'], 'w_in': out['w_in'], 'conv_w': out['conv_w'], 'a_log': out['a_log'], 'dt_bias': out['dt_bias'], 'gdn_norm_w': out['gdn_norm_w'], 'q_norm_w': out['q_norm_w'], 'w_uq': out['w_uq'], 'kv_norm_w': out['kv_norm_w'], 'w_ukv': out['w_ukv'], 'mla_out_norm_w': out['mla_out_norm_w'], 'w_out': out['w_out'], 'ffn_norm_w': out['ffn_norm_w'], 'w_gate': out['w_gate'], 'w_up': out['w_up'], 'w_down': out['w_down'], 'final_norm_w': out['final_norm_w'], 'loss_target': out['loss_target'], 'm_attn_norm_w': out['m_attn_norm_w'], 'm_w_in': out['m_w_in'], 'm_conv_w': out['m_conv_w'], 'm_a_log': out['m_a_log'], 'm_dt_bias': out['m_dt_bias'], 'm_gdn_norm_w': out['m_gdn_norm_w'], 'm_q_norm_w': out['m_q_norm_w'], 'm_w_uq': out['m_w_uq'], 'm_kv_norm_w': out['m_kv_norm_w'], 'm_w_ukv': out['m_w_ukv'], 'm_mla_out_norm_w': out['m_mla_out_norm_w'], 'm_w_out': out['m_w_out'], 'm_ffn_norm_w': out['m_ffn_norm_w'], 'm_w_gate': out['m_w_gate'], 'm_w_up': out['m_w_up'], 'm_w_down': out['m_w_down'], 'm_final_norm_w': out['m_final_norm_w'], 'v_attn_norm_w': out['v_attn_norm_w'], 'v_w_in': out['v_w_in'], 'v_conv_w': out['v_conv_w'], 'v_a_log': out['v_a_log'], 'v_dt_bias': out['v_dt_bias'], 'v_gdn_norm_w': out['v_gdn_norm_w'], 'v_q_norm_w': out['v_q_norm_w'], 'v_w_uq': out['v_w_uq'], 'v_kv_norm_w': out['v_kv_norm_w'], 'v_w_ukv': out['v_w_ukv'], 'v_mla_out_norm_w': out['v_mla_out_norm_w'], 'v_w_out': out['v_w_out'], 'v_ffn_norm_w': out['v_ffn_norm_w'], 'v_w_gate': out['v_w_gate'], 'v_w_up': out['v_w_up'], 'v_w_down': out['v_w_down'], 'v_final_norm_w': out['v_final_norm_w']}


def _loss(weights, diff, rest, loss_target):
    with _jax.named_scope("forward"):
        args = {**rest, TWIN_DIFF_INPUT: diff, **{k: w.astype(_WEIGHT_DTYPES[k]) for k, w in weights.items()}}
        y = _forward(args)
    with _jax.named_scope("loss_head"):
        err = _jnp.square(y.astype(_jnp.float32) - loss_target)
        return 0.5 * _jnp.sum(_jnp.mean(err, axis=-1)) if err.ndim else 0.5 * err


def _adamw(w, g, m, v):
    m = ADAM_B1 * m + (1.0 - ADAM_B1) * g
    v = ADAM_B2 * v + (1.0 - ADAM_B2) * _jnp.square(g)
    m_hat = m / (1.0 - ADAM_B1 ** ADAM_STEP)
    v_hat = v / (1.0 - ADAM_B2 ** ADAM_STEP)
    delta = -ADAM_LR * (m_hat / (_jnp.sqrt(v_hat) + ADAM_EPS) + ADAM_WD * w)
    return delta, m, v


def reference(x, positions, attn_norm_w, w_in, conv_w, a_log, dt_bias, gdn_norm_w, q_norm_w, w_uq, kv_norm_w, w_ukv, mla_out_norm_w, w_out, ffn_norm_w, w_gate, w_up, w_down, final_norm_w, loss_target, m_attn_norm_w, m_w_in, m_conv_w, m_a_log, m_dt_bias, m_gdn_norm_w, m_q_norm_w, m_w_uq, m_kv_norm_w, m_w_ukv, m_mla_out_norm_w, m_w_out, m_ffn_norm_w, m_w_gate, m_w_up, m_w_down, m_final_norm_w, v_attn_norm_w, v_w_in, v_conv_w, v_a_log, v_dt_bias, v_gdn_norm_w, v_q_norm_w, v_w_uq, v_kv_norm_w, v_w_ukv, v_mla_out_norm_w, v_w_out, v_ffn_norm_w, v_w_gate, v_w_up, v_w_down, v_final_norm_w):
    given = dict(x=x, positions=positions, attn_norm_w=attn_norm_w, w_in=w_in, conv_w=conv_w, a_log=a_log, dt_bias=dt_bias, gdn_norm_w=gdn_norm_w, q_norm_w=q_norm_w, w_uq=w_uq, kv_norm_w=kv_norm_w, w_ukv=w_ukv, mla_out_norm_w=mla_out_norm_w, w_out=w_out, ffn_norm_w=ffn_norm_w, w_gate=w_gate, w_up=w_up, w_down=w_down, final_norm_w=final_norm_w, loss_target=loss_target, m_attn_norm_w=m_attn_norm_w, m_w_in=m_w_in, m_conv_w=m_conv_w, m_a_log=m_a_log, m_dt_bias=m_dt_bias, m_gdn_norm_w=m_gdn_norm_w, m_q_norm_w=m_q_norm_w, m_w_uq=m_w_uq, m_kv_norm_w=m_kv_norm_w, m_w_ukv=m_w_ukv, m_mla_out_norm_w=m_mla_out_norm_w, m_w_out=m_w_out, m_ffn_norm_w=m_ffn_norm_w, m_w_gate=m_w_gate, m_w_up=m_w_up, m_w_down=m_w_down, m_final_norm_w=m_final_norm_w, v_attn_norm_w=v_attn_norm_w, v_w_in=v_w_in, v_conv_w=v_conv_w, v_a_log=v_a_log, v_dt_bias=v_dt_bias, v_gdn_norm_w=v_gdn_norm_w, v_q_norm_w=v_q_norm_w, v_w_uq=v_w_uq, v_kv_norm_w=v_kv_norm_w, v_w_ukv=v_w_ukv, v_mla_out_norm_w=v_mla_out_norm_w, v_w_out=v_w_out, v_ffn_norm_w=v_ffn_norm_w, v_w_gate=v_w_gate, v_w_up=v_w_up, v_w_down=v_w_down, v_final_norm_w=v_final_norm_w)
    weights = {n: given[n] for n in TWIN_WEIGHTS}
    shared = {n: given[n] for n in SHARED_INPUTS}
    per_example = {n: given[n] for n in ['x', 'positions']}
    grad_fn = _jax.value_and_grad(_loss, argnums=(0, 1))

    def one_microbatch(ex, loss_target):
        ex = dict(ex)
        diff = ex.pop(TWIN_DIFF_INPUT)
        return grad_fn(weights, diff, {**shared, **ex}, loss_target)

    if N_MICROBATCH == 1:
        loss, (grad_w, grad_x) = one_microbatch(per_example, given["loss_target"])
    else:
        def body(carry, xs):
            loss_sum, grad_sum = carry
            l_k, (gw_k, gx_k) = one_microbatch(xs[0], xs[1])
            with _jax.named_scope("update"):
                return (loss_sum + l_k, _jax.tree.map(_jnp.add, grad_sum, gw_k)), gx_k

        init = (_jnp.zeros((), _jnp.float32), _jax.tree.map(_jnp.zeros_like, weights))
        (loss, grad_w), grad_x = _jax.lax.scan(body, init, (per_example, given["loss_target"]))
    with _jax.named_scope("update"):
        delta_w, new_m, new_v = {}, {}, {}
        for n in TWIN_WEIGHTS:
            delta_w[n], new_m[n], new_v[n] = _adamw(weights[n], grad_w[n], given["m_" + n], given["v_" + n])
    return (loss, grad_x, *[grad_w[n] for n in TWIN_WEIGHTS], *[delta_w[n] for n in TWIN_WEIGHTS],
            *[new_m[n] for n in TWIN_WEIGHTS], *[new_v[n] for n in TWIN_WEIGHTS])
```

```python
import functools

import jax
import jax.numpy as jnp
from jax import lax
from jax.experimental import pallas as pl
from jax.experimental.pallas import tpu as pltpu

F32, BF16 = jnp.float32, jnp.bfloat16
SDS = jax.ShapeDtypeStruct
MESH = pl.DeviceIdType.MESH

HEAD = 128
ROPE = 64
CHUNK = 64
PAIR = 2 * CHUNK
CONV = 4
EPS = 1e-6
ROPE_THETA = 10000.0
LANE = 128
B_LANE = 64
A_LANE = 72
VMEM_LIMIT = 48 * 1024 * 1024

ADAM_LR = 0.001
ADAM_B1 = 0.9
ADAM_B2 = 0.999
ADAM_EPS = 1e-08
ADAM_WD = 0.01
ADAM_STEP = 10


def _tile(n, pref, mult=LANE):
    if n <= pref:
        return n
    t = (pref // mult) * mult
    while t >= mult:
        if n % t == 0:
            return t
        t -= mult
    return n


def _pcall(body, *, name, grid, in_specs, out_specs, out_shape, scratch=()):
    return pl.pallas_call(
        body, name=name, grid=grid, in_specs=in_specs, out_specs=out_specs,
        out_shape=out_shape, scratch_shapes=list(scratch),
        compiler_params=pltpu.CompilerParams(
            dimension_semantics=("arbitrary",) * len(grid), vmem_limit_bytes=VMEM_LIMIT))


def _rows(ts, width, col=0):
    return pl.BlockSpec((ts, width), lambda i: (i, col))


def _full(shape):
    nd = len(shape)
    return pl.BlockSpec(shape, lambda i: (0,) * nd)


def _dot(a, b):
    return jnp.dot(a.astype(BF16), b.astype(BF16), preferred_element_type=F32)


def _dot_nt(a, b):
    return lax.dot_general(a.astype(BF16), b.astype(BF16), (((1,), (1,)), ((), ())),
                           preferred_element_type=F32)


def _dot_tn(a, b):
    return lax.dot_general(a.astype(BF16), b.astype(BF16), (((0,), (0,)), ((), ())),
                           preferred_element_type=F32)


def _sigmoid(x):
    return 1.0 / (1.0 + jnp.exp(-x))


def _silu(x):
    return x * _sigmoid(x)


def _dsilu(x):
    s = _sigmoid(x)
    return s * (1.0 + x * (1.0 - s))


def _lane_iota(shape):
    return lax.broadcasted_iota(jnp.int32, shape, len(shape) - 1)


def _col(block, idx):
    return jnp.sum(jnp.where(_lane_iota(block.shape) == idx, block, 0.0), axis=-1, keepdims=True)


def _mm(pairs, *, name, ta=False, tb=False, out_dtype=F32, res=None, a_off=0, a_k=None,
        tm=1024, tn=1024, tk=512):
    a0, b0 = pairs[0]
    if ta:
        kdim, m = a0.shape
    else:
        m = a0.shape[0]
        kdim = a_k if a_k is not None else a0.shape[1]
    n = b0.shape[0] if tb else b0.shape[1]
    tm, tn, tk = _tile(m, tm), _tile(n, tn), _tile(kdim, tk)
    assert m % tm == 0 and n % tn == 0 and kdim % tk == 0 and a_off % tk == 0
    nk, koff, npair = kdim // tk, a_off // tk, len(pairs)
    dims = (((0 if ta else 1,), (1 if tb else 0,)), ((), ()))

    def body(*refs):
        o_ref, acc = refs[-2], refs[-1]
        k = pl.program_id(2)

        @pl.when(k == 0)
        def _():
            acc[...] = jnp.zeros_like(acc)

        tot = None
        for p in range(npair):
            d = lax.dot_general(refs[2 * p][...].astype(BF16), refs[2 * p + 1][...].astype(BF16),
                                dims, preferred_element_type=F32)
            tot = d if tot is None else tot + d
        acc[...] += tot

        @pl.when(k == nk - 1)
        def _():
            r = acc[...]
            if res is not None:
                r = r + refs[2 * npair][...]
            o_ref[...] = r.astype(out_dtype)

    if ta:
        a_spec = pl.BlockSpec((tk, tm), lambda i, j, k: (k, i))
    else:
        a_spec = pl.BlockSpec((tm, tk), lambda i, j, k: (i, k + koff))
    if tb:
        b_spec = pl.BlockSpec((tn, tk), lambda i, j, k: (j, k))
    else:
        b_spec = pl.BlockSpec((tk, tn), lambda i, j, k: (k, j))
    o_spec = pl.BlockSpec((tm, tn), lambda i, j, k: (i, j))
    in_specs, args = [], []
    for a, b in pairs:
        in_specs += [a_spec, b_spec]
        args += [a, b]
    if res is not None:
        in_specs.append(o_spec)
        args.append(res)
    return _pcall(body, name=name, grid=(m // tm, n // tn, nk), in_specs=in_specs,
                  out_specs=o_spec, out_shape=SDS((m, n), out_dtype),
                  scratch=[pltpu.VMEM((tm, tn), F32)])(*args)


def _norm_fwd(x, w, name):
    s, d = x.shape
    ts = _tile(s, 512, 8)

    def body(x_ref, w_ref, h_ref):
        xv = x_ref[...]
        r = lax.rsqrt(jnp.mean(xv * xv, axis=-1, keepdims=True) + EPS)
        h_ref[...] = (xv * r * w_ref[...]).astype(BF16)

    return _pcall(body, name=name, grid=(s // ts,), in_specs=[_rows(ts, d), _full((1, d))],
                  out_specs=_rows(ts, d), out_shape=SDS((s, d), BF16))(x, w)


def _norm_bwd(dh, x, w, dres, name):
    s, d = x.shape
    ts = _tile(s, 512, 8)

    def body(dh_ref, x_ref, w_ref, dres_ref, dx_ref, dw_ref):
        @pl.when(pl.program_id(0) == 0)
        def _():
            dw_ref[...] = jnp.zeros_like(dw_ref)

        xv, dhv = x_ref[...], dh_ref[...]
        r = lax.rsqrt(jnp.mean(xv * xv, axis=-1, keepdims=True) + EPS)
        xh = xv * r
        dw_ref[...] += jnp.sum(dhv * xh, axis=0, keepdims=True)
        dxh = dhv * w_ref[...]
        dx_ref[...] = dres_ref[...] + r * (dxh - xh * jnp.mean(dxh * xh, axis=-1, keepdims=True))

    return _pcall(body, name=name, grid=(s // ts,),
                  in_specs=[_rows(ts, d), _rows(ts, d), _full((1, d)), _rows(ts, d)],
                  out_specs=[_rows(ts, d), _full((1, d))],
                  out_shape=[SDS((s, d), F32), SDS((1, d), F32)])(dh, x, w, dres)


def _final_loss(x3, tgt, w):
    s, d = x3.shape
    ts = _tile(s, 512, 8)

    def body(x_ref, t_ref, w_ref, dx_ref, dw_ref, loss_ref):
        @pl.when(pl.program_id(0) == 0)
        def _():
            dw_ref[...] = jnp.zeros_like(dw_ref)
            loss_ref[...] = jnp.zeros_like(loss_ref)

        xv, wv = x_ref[...], w_ref[...]
        r = lax.rsqrt(jnp.mean(xv * xv, axis=-1, keepdims=True) + EPS)
        xh = xv * r
        err = xh * wv - t_ref[...]
        row = jnp.mean(err * err, axis=-1, keepdims=True)
        loss_ref[...] += 0.5 * jnp.sum(row, axis=0, keepdims=True)
        dy = err * (1.0 / d)
        dw_ref[...] += jnp.sum(dy * xh, axis=0, keepdims=True)
        dxh = dy * wv
        dx_ref[...] = r * (dxh - xh * jnp.mean(dxh * xh, axis=-1, keepdims=True))

    return _pcall(body, name="final_loss", grid=(s // ts,),
                  in_specs=[_rows(ts, d), _rows(ts, d), _full((1, d))],
                  out_specs=[_rows(ts, d), _full((1, d)), _full((1, 1))],
                  out_shape=[SDS((s, d), F32), SDS((1, d), F32), SDS((1, 1), F32)])(x3, tgt, w)


def _shift_down(cur, halo, s):
    if s == 0:
        return cur
    row8 = lax.broadcasted_iota(jnp.int32, halo.shape, 0)
    r = pltpu.roll(cur, s, 0)
    top = jnp.where(row8 < s, pltpu.roll(halo, s, 0), r[0:8])
    return jnp.concatenate([top, r[8:]], axis=0)


def _shift_up(cur, halo, s):
    if s == 0:
        return cur
    ts = cur.shape[0]
    row8 = lax.broadcasted_iota(jnp.int32, halo.shape, 0)
    r = pltpu.roll(cur, ts - s, 0)
    bot = jnp.where(row8 >= 8 - s, pltpu.roll(halo, 8 - s, 0), r[ts - 8:ts])
    return jnp.concatenate([r[:ts - 8], bot], axis=0)


def _chunk_tri(ts, upper):
    i = lax.broadcasted_iota(jnp.int32, (ts, ts), 0)
    j = lax.broadcasted_iota(jnp.int32, (ts, ts), 1)
    same = jnp.right_shift(i, 6) == jnp.right_shift(j, 6)
    return jnp.where(same & ((j >= i) if upper else (j <= i)), 1.0, 0.0).astype(F32)


def _gate_values(m, alog, dtb):
    lane = _lane_iota(m.shape)
    beta = _sigmoid(m)
    xg = m + dtb
    sp = jnp.maximum(xg, 0.0) + jnp.log(1.0 + jnp.exp(-jnp.abs(xg)))
    ga = (lane >= A_LANE) & (lane < A_LANE + 8)
    g = jnp.where(ga, -jnp.exp(alog) * sp, 0.0)
    return beta, g, xg, ga


def _l2_heads(a, nh, scale):
    outs, rs = [], []
    for h in range(nh):
        ah = a[:, HEAD * h:HEAD * (h + 1)]
        r = lax.rsqrt(jnp.sum(ah * ah, axis=-1, keepdims=True) + EPS)
        outs.append(ah * (r * scale))
        rs.append(r)
    return jnp.concatenate(outs, axis=-1), rs


def _gdn_prep(proj, conv_w, alog_l, dtb_l, nh, misc_col):
    s = proj.shape[0]
    w = nh * HEAD
    ts = _tile(s, 256, PAIR)
    hb = ts // 8

    def body(cur_ref, halo_ref, misc_ref, cw_ref, al_ref, db_ref, q_ref, k_ref, v_ref, gb_ref, gbt_ref):
        first = pl.program_id(0) == 0
        outs = (q_ref, k_ref, v_ref)
        for sec in range(3):
            cs = slice(sec * w, (sec + 1) * w)
            cur = cur_ref[:, cs]
            halo = jnp.where(first, 0.0, halo_ref[:, cs])
            pre = None
            for j in range(CONV):
                term = cw_ref[j:j + 1, cs] * _shift_down(cur, halo, CONV - 1 - j)
                pre = term if pre is None else pre + term
            act = _silu(pre)
            if sec == 0:
                act, _ = _l2_heads(act, nh, HEAD ** -0.5)
            elif sec == 1:
                act, _ = _l2_heads(act, nh, 1.0)
            outs[sec][...] = act
        m = misc_ref[...]
        lane = _lane_iota(m.shape)
        beta, g, _, ga = _gate_values(m, al_ref[...], db_ref[...])
        gcc = jnp.dot(_chunk_tri(ts, False), g, precision=lax.Precision.HIGHEST,
                      preferred_element_type=F32)
        gb = jnp.where((lane >= B_LANE) & (lane < B_LANE + 8), beta, jnp.where(ga, gcc, 0.0))
        gb_ref[...] = gb
        gbt_ref[...] = gb.T

    return _pcall(
        body, name="gdn_prep", grid=(s // ts,),
        in_specs=[_rows(ts, 3 * w),
                  pl.BlockSpec((8, 3 * w), lambda i: (jnp.maximum(i * hb - 1, 0), 0)),
                  _rows(ts, LANE, misc_col), _full((CONV, 3 * w)), _full((1, LANE)), _full((1, LANE))],
        out_specs=[_rows(ts, w), _rows(ts, w), _rows(ts, w), _rows(ts, LANE),
                   pl.BlockSpec((LANE, ts), lambda i: (0, i))],
        out_shape=[SDS((s, w), F32), SDS((s, w), F32), SDS((s, w), F32), SDS((s, LANE), F32),
                   SDS((LANE, s), F32)])(proj, proj, proj, conv_w, alog_l, dtb_l)


def _gdn_prep_bwd(proj, conv_w, alog_l, dtb_l, dq, dk, dv, dgb, dkr, nh, misc_col):
    s = proj.shape[0]
    w = nh * HEAD
    ts = _tile(s, 256, PAIR)
    hb = ts // 8

    def body(cur_ref, halo_ref, misc_ref, cw_ref, al_ref, db_ref, dq_ref, dk_ref, dv_ref, dgb_ref,
             dkr_ref, dc_ref, dm_ref, dcw_ref, dal_ref, ddb_ref):
        first = pl.program_id(0) == 0

        @pl.when(first)
        def _():
            dcw_ref[...] = jnp.zeros_like(dcw_ref)
            dal_ref[...] = jnp.zeros_like(dal_ref)
            ddb_ref[...] = jnp.zeros_like(ddb_ref)

        dins = (dq_ref, dk_ref, dv_ref)
        for sec in range(3):
            cs = slice(sec * w, (sec + 1) * w)
            cur = cur_ref[:, cs]
            halo = jnp.where(first, 0.0, halo_ref[:, cs])
            us = [_shift_down(cur, halo, CONV - 1 - j) for j in range(CONV)]
            pre = None
            for j in range(CONV):
                term = cw_ref[j:j + 1, cs] * us[j]
                pre = term if pre is None else pre + term
            act = _silu(pre)
            dout = dins[sec][...]
            if sec < 2:
                scale = HEAD ** -0.5 if sec == 0 else 1.0
                parts = []
                for h in range(nh):
                    hs = slice(HEAD * h, HEAD * (h + 1))
                    ah = act[:, hs]
                    r = lax.rsqrt(jnp.sum(ah * ah, axis=-1, keepdims=True) + EPS)
                    ahat = ah * r
                    dy = dout[:, hs]
                    parts.append((scale * r) * (dy - ahat * jnp.sum(dy * ahat, axis=-1, keepdims=True)))
                dact = jnp.concatenate(parts, axis=-1)
            else:
                dact = dout
            dconv = dact * _dsilu(pre)
            dc_ref[:, cs] = dconv
            for j in range(CONV):
                dcw_ref[j:j + 1, cs] += jnp.sum(dconv * us[j], axis=0, keepdims=True)
        m = misc_ref[...]
        lane = _lane_iota(m.shape)
        al = al_ref[...]
        beta, g, xg, ga = _gate_values(m, al, db_ref[...])
        dgbv = dgb_ref[...]
        dg = jnp.dot(_chunk_tri(ts, True), jnp.where(ga, dgbv, 0.0), precision=lax.Precision.HIGHEST,
                     preferred_element_type=F32)
        da_raw = jnp.where(ga, dg * (-jnp.exp(al)) * _sigmoid(xg), 0.0)
        db_raw = jnp.where((lane >= B_LANE) & (lane < B_LANE + 8), dgbv * beta * (1.0 - beta), 0.0)
        dal_ref[...] += jnp.sum(dg * g, axis=0, keepdims=True)
        ddb_ref[...] += jnp.sum(da_raw, axis=0, keepdims=True)
        dm_ref[...] = (dkr_ref[...] + da_raw + db_raw).astype(BF16)

    return _pcall(
        body, name="gdn_prep_bwd", grid=(s // ts,),
        in_specs=[_rows(ts, 3 * w),
                  pl.BlockSpec((8, 3 * w), lambda i: (jnp.maximum(i * hb - 1, 0), 0)),
                  _rows(ts, LANE, misc_col), _full((CONV, 3 * w)), _full((1, LANE)), _full((1, LANE)),
                  _rows(ts, w), _rows(ts, w), _rows(ts, w), _rows(ts, LANE), _rows(ts, LANE)],
        out_specs=[_rows(ts, 3 * w), _rows(ts, LANE), _full((CONV, 3 * w)), _full((1, LANE)),
                   _full((1, LANE))],
        out_shape=[SDS((s, 3 * w), F32), SDS((s, LANE), BF16), SDS((CONV, 3 * w), F32),
                   SDS((1, LANE), F32), SDS((1, LANE), F32)])(
                       proj, proj, proj, conv_w, alog_l, dtb_l, dq, dk, dv, dgb, dkr)


def _conv_bwd_input(dconv, conv_w):
    s, c = dconv.shape
    ts = _tile(s, 256, 8)
    hb = ts // 8
    nblk8 = s // 8
    nt = s // ts

    def body(cur_ref, nxt_ref, cw_ref, o_ref):
        last = pl.program_id(0) == nt - 1
        cur = cur_ref[...]
        halo = jnp.where(last, 0.0, nxt_ref[...])
        acc = None
        for j in range(CONV):
            term = cw_ref[j:j + 1, :] * _shift_up(cur, halo, CONV - 1 - j)
            acc = term if acc is None else acc + term
        o_ref[...] = acc.astype(BF16)

    return _pcall(
        body, name="conv_bwd_input", grid=(nt,),
        in_specs=[_rows(ts, c),
                  pl.BlockSpec((8, c), lambda i: (jnp.minimum((i + 1) * hb, nblk8 - 1), 0)),
                  _full((CONV, c))],
        out_specs=_rows(ts, c), out_shape=SDS((s, c), BF16))(dconv, dconv, conv_w)


def _inv_unit_lower(a):
    n = a.shape[0]
    i = lax.broadcasted_iota(jnp.int32, (n, n), 0)
    j = lax.broadcasted_iota(jnp.int32, (n, n), 1)
    t = jnp.where(i == j, 1.0, 0.0) - a
    x = a
    for _ in range(5):
        x = _dot(x, x)
        t = t + _dot(t, x)
    return t


def _pair_common(q, k, gcol, grow, bcol):
    i = lax.broadcasted_iota(jnp.int32, (PAIR, PAIR), 0)
    j = lax.broadcasted_iota(jnp.int32, (PAIR, PAIR), 1)
    same = jnp.right_shift(i, 6) == jnp.right_shift(j, 6)
    tril = same & (i >= j)
    strict = same & (i > j)
    dec = jnp.where(tril, jnp.exp(jnp.minimum(gcol - grow, 0.0)), 0.0)
    kk = _dot_nt(k, k)
    a = jnp.where(strict, bcol * kk * dec, 0.0)
    t = _inv_unit_lower(a)
    p = _dot_nt(q, k) * dec
    return dec, kk, a, t, p, tril, strict


def _ext(v, a):
    z = jnp.zeros_like(v)
    return jnp.concatenate([v, z] if a == 0 else [z, v], axis=0)


def _gdn_fwd(q, k, v, gb, gbt, nh):
    s = q.shape[0]
    w = nh * HEAD
    npair = s // PAIR

    def body(q_ref, k_ref, v_ref, gb_ref, gbt_ref, o_ref, st_ref, s_ref):
        @pl.when(pl.program_id(0) == 0)
        def _():
            s_ref[...] = jnp.zeros_like(s_ref)

        for h in range(nh):
            hs = slice(HEAD * h, HEAD * (h + 1))
            qh, kh, vh = q_ref[:, hs], k_ref[:, hs], v_ref[:, hs]
            gcol = _col(gb_ref[...], A_LANE + h)
            bcol = _col(gb_ref[...], B_LANE + h)
            grow = gbt_ref[A_LANE + h:A_LANE + h + 1, :]
            _, _, _, t, p, _, _ = _pair_common(qh, kh, gcol, grow, bcol)
            eg = jnp.exp(gcol)
            qg, kg = qh * eg, kh * eg
            outs = []
            for a in range(2):
                sl = slice(CHUNK * a, CHUNK * (a + 1))
                st = s_ref[h]
                st_ref[a, h] = st
                r = vh[sl] - _dot(kg[sl], st)
                vn = _dot(t[sl], _ext(bcol[sl] * r, a))
                outs.append(_dot(qg[sl], st) + _dot(p[sl], _ext(vn, a)))
                gl = _col(grow, CHUNK * (a + 1) - 1)
                kd = kh[sl] * jnp.exp(gl - gcol[sl])
                s_ref[h] = jnp.exp(gl) * st + _dot_tn(kd, vn)
            o_ref[:, hs] = jnp.concatenate(outs, axis=0)

    return _pcall(
        body, name="gdn_fwd", grid=(npair,),
        in_specs=[_rows(PAIR, w), _rows(PAIR, w), _rows(PAIR, w), _rows(PAIR, LANE),
                  pl.BlockSpec((LANE, PAIR), lambda i: (0, i))],
        out_specs=[_rows(PAIR, w), pl.BlockSpec((2, nh, HEAD, HEAD), lambda i: (i, 0, 0, 0))],
        out_shape=[SDS((s, w), F32), SDS((2 * npair, nh, HEAD, HEAD), F32)],
        scratch=[pltpu.VMEM((nh, HEAD, HEAD), F32)])(q, k, v, gb, gbt)


def _gdn_bwd(q, k, v, gb, gbt, states, do, nh):
    s = q.shape[0]
    w = nh * HEAD
    npair = s // PAIR
    rev = lambda i: (npair - 1 - i, 0)

    def body(q_ref, k_ref, v_ref, gb_ref, gbt_ref, st_ref, do_ref, dq_ref, dk_ref, dv_ref, dgb_ref,
             ds_ref):
        @pl.when(pl.program_id(0) == 0)
        def _():
            ds_ref[...] = jnp.zeros_like(ds_ref)

        lane = _lane_iota((PAIR, LANE))
        row = lax.broadcasted_iota(jnp.int32, (CHUNK, 1), 0)
        dgb = jnp.zeros((PAIR, LANE), F32)
        for h in range(nh):
            hs = slice(HEAD * h, HEAD * (h + 1))
            qh, kh, vh, doh = q_ref[:, hs], k_ref[:, hs], v_ref[:, hs], do_ref[:, hs]
            gcol = _col(gb_ref[...], A_LANE + h)
            bcol = _col(gb_ref[...], B_LANE + h)
            grow = gbt_ref[A_LANE + h:A_LANE + h + 1, :]
            dec, kk, amat, t, p, tril, strict = _pair_common(qh, kh, gcol, grow, bcol)
            tt, pt = t.T, p.T
            eg = jnp.exp(gcol)
            qg, kg = qh * eg, kh * eg
            rs, vns = [], []
            for a in range(2):
                sl = slice(CHUNK * a, CHUNK * (a + 1))
                r = vh[sl] - _dot(kg[sl], st_ref[a, h])
                rs.append(r)
                vns.append(_dot(t[sl], _ext(bcol[sl] * r, a)))
            dsn = ds_ref[h]
            dqs, dks, dvs, dgcs, dbs, drbs = [None] * 2, [None] * 2, [None] * 2, [None] * 2, [None] * 2, [None] * 2
            for a in (1, 0):
                sl = slice(CHUNK * a, CHUNK * (a + 1))
                st = st_ref[a, h]
                gl = _col(grow, CHUNK * (a + 1) - 1)
                egl = jnp.exp(gl)
                dk_dec = jnp.exp(gl - gcol[sl])
                kd = kh[sl] * dk_dec
                d_vn = _dot(pt[sl], _ext(doh[sl], a)) + _dot(kd, dsn)
                d_qg = _dot_nt(doh[sl], st)
                d_rb = _dot(tt[sl], _ext(d_vn, a))
                dbs[a] = jnp.sum(d_rb * rs[a], axis=-1, keepdims=True)
                d_r = bcol[sl] * d_rb
                d_kg = -_dot_nt(d_r, st)
                d_kd = _dot_nt(vns[a], dsn)
                dgl = egl * jnp.sum(dsn * st, keepdims=True) + jnp.sum(d_kd * kd, keepdims=True)
                dgc = (jnp.sum(d_qg * qg[sl], axis=-1, keepdims=True)
                       + jnp.sum(d_kg * kg[sl], axis=-1, keepdims=True)
                       - jnp.sum(d_kd * kd, axis=-1, keepdims=True))
                dgcs[a] = dgc + jnp.where(row == CHUNK - 1, dgl, 0.0)
                dqs[a] = d_qg * eg[sl]
                dks[a] = d_kg * eg[sl] + d_kd * dk_dec
                dvs[a] = d_r
                drbs[a] = d_rb
                dsn = _dot_tn(qg[sl], doh[sl]) + egl * dsn - _dot_tn(kg[sl], d_r)
            ds_ref[h] = dsn
            cat = lambda xs: jnp.concatenate(xs, axis=0)
            vn, d_rb = cat(vns), cat(drbs)
            dp = jnp.where(tril, _dot_nt(doh, vn), 0.0)
            dam = jnp.where(strict, -_dot_nt(d_rb, vn), 0.0)
            g_p = dp * dec
            g_a = dam * dec
            gbk = bcol * g_a
            dq_ref[:, hs] = cat(dqs) + _dot(g_p, kh)
            dk_ref[:, hs] = cat(dks) + _dot_tn(g_p, qh) + _dot(gbk, kh) + _dot_tn(gbk, kh)
            dv_ref[:, hs] = cat(dvs)
            dbeta = cat(dbs) + jnp.sum(g_a * kk, axis=-1, keepdims=True)
            mm = dp * p + dam * amat
            dgc = cat(dgcs) + jnp.sum(mm, axis=-1, keepdims=True) - jnp.sum(mm.T, axis=-1, keepdims=True)
            dgb = dgb + jnp.where(lane == A_LANE + h, dgc, 0.0) + jnp.where(lane == B_LANE + h, dbeta, 0.0)
        dgb_ref[...] = dgb

    return _pcall(
        body, name="gdn_bwd", grid=(npair,),
        in_specs=[pl.BlockSpec((PAIR, w), rev), pl.BlockSpec((PAIR, w), rev), pl.BlockSpec((PAIR, w), rev),
                  pl.BlockSpec((PAIR, LANE), rev),
                  pl.BlockSpec((LANE, PAIR), lambda i: (0, npair - 1 - i)),
                  pl.BlockSpec((2, nh, HEAD, HEAD), lambda i: (npair - 1 - i, 0, 0, 0)),
                  pl.BlockSpec((PAIR, w), rev)],
        out_specs=[pl.BlockSpec((PAIR, w), rev), pl.BlockSpec((PAIR, w), rev), pl.BlockSpec((PAIR, w), rev),
                   pl.BlockSpec((PAIR, LANE), rev)],
        out_shape=[SDS((s, w), F32), SDS((s, w), F32), SDS((s, w), F32), SDS((s, LANE), F32)],
        scratch=[pltpu.VMEM((nh, HEAD, HEAD), F32)])(q, k, v, gb, gbt, states, do)


def _mla_norm(proj, qw, kvw, col_q, col_kv):
    s = proj.shape[0]
    lr = qw.shape[1]
    ts = _tile(s, 512, 8)

    def body(cq_ref, ckv_ref, qw_ref, kvw_ref, oq_ref, okv_ref):
        for x_ref, w_ref, o_ref in ((cq_ref, qw_ref, oq_ref), (ckv_ref, kvw_ref, okv_ref)):
            xv = x_ref[...]
            r = lax.rsqrt(jnp.mean(xv * xv, axis=-1, keepdims=True) + EPS)
            o_ref[...] = (xv * r * w_ref[...]).astype(BF16)

    return _pcall(body, name="mla_norm", grid=(s // ts,),
                  in_specs=[_rows(ts, lr, col_q), _rows(ts, lr, col_kv), _full((1, lr)), _full((1, lr))],
                  out_specs=[_rows(ts, lr), _rows(ts, lr)],
                  out_shape=[SDS((s, lr), BF16), SDS((s, lr), BF16)])(proj, proj, qw, kvw)


def _mla_norm_bwd(proj, qw, kvw, dq, dkv, col_q, col_kv):
    s = proj.shape[0]
    lr = qw.shape[1]
    ts = _tile(s, 512, 8)

    def body(cq_ref, ckv_ref, qw_ref, kvw_ref, dq_ref, dkv_ref, oq_ref, okv_ref, dqw_ref, dkvw_ref):
        @pl.when(pl.program_id(0) == 0)
        def _():
            dqw_ref[...] = jnp.zeros_like(dqw_ref)
            dkvw_ref[...] = jnp.zeros_like(dkvw_ref)

        for x_ref, w_ref, d_ref, o_ref, dw_ref in ((cq_ref, qw_ref, dq_ref, oq_ref, dqw_ref),
                                                    (ckv_ref, kvw_ref, dkv_ref, okv_ref, dkvw_ref)):
            xv, dh = x_ref[...], d_ref[...]
            r = lax.rsqrt(jnp.mean(xv * xv, axis=-1, keepdims=True) + EPS)
            xh = xv * r
            dw_ref[...] += jnp.sum(dh * xh, axis=0, keepdims=True)
            dxh = dh * w_ref[...]
            o_ref[...] = (r * (dxh - xh * jnp.mean(dxh * xh, axis=-1, keepdims=True))).astype(BF16)

    return _pcall(body, name="mla_norm_bwd", grid=(s // ts,),
                  in_specs=[_rows(ts, lr, col_q), _rows(ts, lr, col_kv), _full((1, lr)), _full((1, lr)),
                            _rows(ts, lr), _rows(ts, lr)],
                  out_specs=[_rows(ts, lr), _rows(ts, lr), _full((1, lr)), _full((1, lr))],
                  out_shape=[SDS((s, lr), BF16), SDS((s, lr), BF16), SDS((1, lr), F32),
                             SDS((1, lr), F32)])(proj, proj, qw, kvw, dq, dkv)


def _rope_tables(pos, invf, sgn):
    ang = pos * invf
    return jnp.cos(ang), jnp.sin(ang) * sgn


def _swap_halves_lanes(y):
    lane = _lane_iota(y.shape)
    return jnp.where(lane < ROPE // 2, pltpu.roll(y, LANE - ROPE // 2, 1), pltpu.roll(y, ROPE // 2, 1))


def _rope_consts():
    half = ROPE // 2
    inv = ROPE_THETA ** (-jnp.arange(half, dtype=F32) / half)
    invf = jnp.concatenate([inv, inv, jnp.zeros((LANE - ROPE,), F32)])[None, :]
    sgn = jnp.concatenate([-jnp.ones((half,), F32), jnp.ones((half,), F32),
                           jnp.zeros((LANE - ROPE,), F32)])[None, :]
    return invf, sgn


def _mla_rope(qraw, kvraw, proj, pos, nh, misc_col):
    s = qraw.shape[0]
    ts = _tile(s, 256, 8)
    wq = nh * 2 * HEAD
    invf, sgn = _rope_consts()

    def body(q_ref, kv_ref, misc_ref, pos_ref, if_ref, sg_ref, qc_ref, kc_ref, v_ref):
        c, sn = _rope_tables(pos_ref[...], if_ref[...], sg_ref[...])
        lane = _lane_iota(c.shape)
        rot = lambda xb: xb * c + _swap_halves_lanes(xb) * sn
        krot = jnp.where(lane < ROPE, rot(misc_ref[...]), 0.0).astype(BF16)
        for h in range(nh):
            b0 = 2 * HEAD * h
            qc_ref[:, b0:b0 + HEAD] = q_ref[:, b0:b0 + HEAD].astype(BF16)
            qc_ref[:, b0 + HEAD:b0 + 2 * HEAD] = rot(q_ref[:, b0 + HEAD:b0 + 2 * HEAD]).astype(BF16)
            kc_ref[:, b0:b0 + HEAD] = kv_ref[:, b0:b0 + HEAD].astype(BF16)
            kc_ref[:, b0 + HEAD:b0 + 2 * HEAD] = krot
        v_ref[...] = kv_ref[:, wq:].astype(BF16)

    return _pcall(body, name="mla_rope", grid=(s // ts,),
                  in_specs=[_rows(ts, wq), _rows(ts, wq + nh * HEAD), _rows(ts, LANE, misc_col),
                            _rows(ts, 1), _full((1, LANE)), _full((1, LANE))],
                  out_specs=[_rows(ts, wq), _rows(ts, wq), _rows(ts, nh * HEAD)],
                  out_shape=[SDS((s, wq), BF16), SDS((s, wq), BF16), SDS((s, nh * HEAD), BF16)])(
                      qraw, kvraw, proj, pos, invf, sgn)


def _mla_rope_bwd(dqc, dkc, dv, pos, nh):
    s = dqc.shape[0]
    ts = _tile(s, 256, 8)
    wq = nh * 2 * HEAD
    invf, sgn = _rope_consts()

    def body(dq_ref, dk_ref, dv_ref, pos_ref, if_ref, sg_ref, oq_ref, okv_ref, okr_ref):
        c, sn = _rope_tables(pos_ref[...], if_ref[...], sg_ref[...])
        lane = _lane_iota(c.shape)
        unrot = lambda d: d * c + _swap_halves_lanes(d * sn)
        dkr = jnp.zeros(c.shape, F32)
        for h in range(nh):
            b0 = 2 * HEAD * h
            oq_ref[:, b0:b0 + HEAD] = dq_ref[:, b0:b0 + HEAD].astype(BF16)
            oq_ref[:, b0 + HEAD:b0 + 2 * HEAD] = unrot(dq_ref[:, b0 + HEAD:b0 + 2 * HEAD]).astype(BF16)
            okv_ref[:, b0:b0 + HEAD] = dk_ref[:, b0:b0 + HEAD].astype(BF16)
            okv_ref[:, b0 + HEAD:b0 + 2 * HEAD] = jnp.zeros((ts, HEAD), BF16)
            dkr = dkr + dk_ref[:, b0 + HEAD:b0 + 2 * HEAD]
        okv_ref[:, wq:] = dv_ref[...].astype(BF16)
        okr_ref[...] = jnp.where(lane < ROPE, unrot(jnp.where(lane < ROPE, dkr, 0.0)), 0.0)

    return _pcall(body, name="mla_rope_bwd", grid=(s // ts,),
                  in_specs=[_rows(ts, wq), _rows(ts, wq), _rows(ts, nh * HEAD), _rows(ts, 1),
                            _full((1, LANE)), _full((1, LANE))],
                  out_specs=[_rows(ts, wq), _rows(ts, wq + nh * HEAD), _rows(ts, LANE)],
                  out_shape=[SDS((s, wq), BF16), SDS((s, wq + nh * HEAD), BF16), SDS((s, LANE), F32)])(
                      dqc, dkc, dv, pos, invf, sgn)


def _causal_mask(blk):
    i = lax.broadcasted_iota(jnp.int32, (blk, blk), 0)
    j = lax.broadcasted_iota(jnp.int32, (blk, blk), 1)
    return j <= i


def _mla_fwd(qc, kc, v, nh):
    s = qc.shape[0]
    blk = _tile(s, 512)
    nb = s // blk
    scale = (HEAD + ROPE) ** -0.5

    def body(q_ref, k_ref, v_ref, o_ref, lse_ref, m_sc, l_sc, acc):
        h, j, i = pl.program_id(1), pl.program_id(2), pl.program_id(0)

        @pl.when((h == 0) & (j == 0))
        def _():
            lse_ref[...] = jnp.zeros_like(lse_ref)

        @pl.when(j == 0)
        def _():
            m_sc[...] = jnp.full_like(m_sc, -1e30)
            l_sc[...] = jnp.zeros_like(l_sc)
            acc[...] = jnp.zeros_like(acc)

        @pl.when(j <= i)
        def _():
            sc = _dot_nt(q_ref[...], k_ref[...]) * scale
            sc = jnp.where((j < i) | _causal_mask(blk), sc, -1e30)
            m_new = jnp.maximum(m_sc[...], jnp.max(sc, axis=-1, keepdims=True))
            alpha = jnp.exp(m_sc[...] - m_new)
            p = jnp.exp(sc - m_new)
            l_sc[...] = alpha * l_sc[...] + jnp.sum(p, axis=-1, keepdims=True)
            acc[...] = alpha * acc[...] + _dot(p, v_ref[...])
            m_sc[...] = m_new

        @pl.when(j == i)
        def _():
            o_ref[...] = acc[...] / l_sc[...]
            lse = m_sc[...] + jnp.log(l_sc[...])
            lane = _lane_iota((blk, LANE))
            lse_ref[...] = jnp.where(lane == h, lse, lse_ref[...])

    return pl.pallas_call(
        body, name="mla_fwd", grid=(nb, nh, nb),
        in_specs=[pl.BlockSpec((blk, 2 * HEAD), lambda i, h, j: (i, h)),
                  pl.BlockSpec((blk, 2 * HEAD), lambda i, h, j: (jnp.minimum(j, i), h)),
                  pl.BlockSpec((blk, HEAD), lambda i, h, j: (jnp.minimum(j, i), h))],
        out_specs=[pl.BlockSpec((blk, HEAD), lambda i, h, j: (i, h)),
                   pl.BlockSpec((blk, LANE), lambda i, h, j: (i, 0))],
        out_shape=[SDS((s, nh * HEAD), F32), SDS((s, LANE), F32)],
        scratch_shapes=[pltpu.VMEM((blk, 1), F32), pltpu.VMEM((blk, 1), F32), pltpu.VMEM((blk, HEAD), F32)],
        compiler_params=pltpu.CompilerParams(dimension_semantics=("arbitrary",) * 3,
                                             vmem_limit_bytes=VMEM_LIMIT))(qc, kc, v)


def _head_lse(lse_blk, h):
    lane = _lane_iota(lse_blk.shape)
    return jnp.sum(jnp.where(lane == h, lse_blk, 0.0), axis=-1, keepdims=True)


def _mla_bwd_dq(qc, kc, v, o, do, lse, nh):
    s = qc.shape[0]
    blk = _tile(s, 512)
    nb = s // blk
    scale = (HEAD + ROPE) ** -0.5

    def body(q_ref, k_ref, v_ref, o_ref, do_ref, lse_ref, dq_ref, acc):
        h, j, i = pl.program_id(1), pl.program_id(2), pl.program_id(0)

        @pl.when(j == 0)
        def _():
            acc[...] = jnp.zeros_like(acc)

        @pl.when(j <= i)
        def _():
            dov = do_ref[...]
            delta = jnp.sum(dov * o_ref[...], axis=-1, keepdims=True)
            sc = _dot_nt(q_ref[...], k_ref[...]) * scale
            sc = jnp.where((j < i) | _causal_mask(blk), sc, -1e30)
            p = jnp.exp(sc - _head_lse(lse_ref[...], h))
            dp = _dot_nt(dov, v_ref[...])
            ds = p * (dp - delta) * scale
            acc[...] += _dot(ds, k_ref[...])

        @pl.when(j == i)
        def _():
            dq_ref[...] = acc[...]

    return pl.pallas_call(
        body, name="mla_bwd_dq", grid=(nb, nh, nb),
        in_specs=[pl.BlockSpec((blk, 2 * HEAD), lambda i, h, j: (i, h)),
                  pl.BlockSpec((blk, 2 * HEAD), lambda i, h, j: (jnp.minimum(j, i), h)),
                  pl.BlockSpec((blk, HEAD), lambda i, h, j: (jnp.minimum(j, i), h)),
                  pl.BlockSpec((blk, HEAD), lambda i, h, j: (i, h)),
                  pl.BlockSpec((blk, HEAD), lambda i, h, j: (i, h)),
                  pl.BlockSpec((blk, LANE), lambda i, h, j: (i, 0))],
        out_specs=pl.BlockSpec((blk, 2 * HEAD), lambda i, h, j: (i, h)),
        out_shape=SDS((s, nh * 2 * HEAD), F32),
        scratch_shapes=[pltpu.VMEM((blk, 2 * HEAD), F32)],
        compiler_params=pltpu.CompilerParams(dimension_semantics=("arbitrary",) * 3,
                                             vmem_limit_bytes=VMEM_LIMIT))(qc, kc, v, o, do, lse)


def _mla_bwd_dkv(qc, kc, v, o, do, lse, nh):
    s = qc.shape[0]
    blk = _tile(s, 512)
    nb = s // blk
    scale = (HEAD + ROPE) ** -0.5

    def body(q_ref, k_ref, v_ref, o_ref, do_ref, lse_ref, dk_ref, dv_ref, dk_acc, dv_acc):
        h, i, j = pl.program_id(1), pl.program_id(2), pl.program_id(0)

        @pl.when(i == 0)
        def _():
            dk_acc[...] = jnp.zeros_like(dk_acc)
            dv_acc[...] = jnp.zeros_like(dv_acc)

        @pl.when(i >= j)
        def _():
            dov = do_ref[...]
            delta = jnp.sum(dov * o_ref[...], axis=-1, keepdims=True)
            sc = _dot_nt(q_ref[...], k_ref[...]) * scale
            sc = jnp.where((j < i) | _causal_mask(blk), sc, -1e30)
            p = jnp.exp(sc - _head_lse(lse_ref[...], h))
            dv_acc[...] += _dot_tn(p, dov)
            dp = _dot_nt(dov, v_ref[...])
            ds = p * (dp - delta) * scale
            dk_acc[...] += _dot_tn(ds, q_ref[...])

        @pl.when(i == nb - 1)
        def _():
            dk_ref[...] = dk_acc[...]
            dv_ref[...] = dv_acc[...]

    qi = lambda j, h, i: (jnp.maximum(i, j), h)
    return pl.pallas_call(
        body, name="mla_bwd_dkv", grid=(nb, nh, nb),
        in_specs=[pl.BlockSpec((blk, 2 * HEAD), qi),
                  pl.BlockSpec((blk, 2 * HEAD), lambda j, h, i: (j, h)),
                  pl.BlockSpec((blk, HEAD), lambda j, h, i: (j, h)),
                  pl.BlockSpec((blk, HEAD), qi),
                  pl.BlockSpec((blk, HEAD), qi),
                  pl.BlockSpec((blk, LANE), lambda j, h, i: (jnp.maximum(i, j), 0))],
        out_specs=[pl.BlockSpec((blk, 2 * HEAD), lambda j, h, i: (j, h)),
                   pl.BlockSpec((blk, HEAD), lambda j, h, i: (j, h))],
        out_shape=[SDS((s, nh * 2 * HEAD), F32), SDS((s, nh * HEAD), F32)],
        scratch_shapes=[pltpu.VMEM((blk, 2 * HEAD), F32), pltpu.VMEM((blk, HEAD), F32)],
        compiler_params=pltpu.CompilerParams(dimension_semantics=("arbitrary",) * 3,
                                             vmem_limit_bytes=VMEM_LIMIT))(qc, kc, v, o, do, lse)


def _mix_fwd(og, proj, om, gw, mw, nh, z_col):
    s = og.shape[0]
    w = nh * HEAD
    ts = _tile(s, 256, 8)

    def body(og_ref, z_ref, om_ref, gw_ref, mw_ref, o_ref):
        for h in range(nh):
            hs = slice(HEAD * h, HEAD * (h + 1))
            a = og_ref[:, hs]
            r = lax.rsqrt(jnp.mean(a * a, axis=-1, keepdims=True) + EPS)
            o_ref[:, hs] = (a * r * gw_ref[...] * _silu(z_ref[:, hs])).astype(BF16)
            b = om_ref[:, hs]
            r = lax.rsqrt(jnp.mean(b * b, axis=-1, keepdims=True) + EPS)
            o_ref[:, w + HEAD * h:w + HEAD * (h + 1)] = (b * r * mw_ref[...]).astype(BF16)

    return _pcall(body, name="mix_fwd", grid=(s // ts,),
                  in_specs=[_rows(ts, w), _rows(ts, w, z_col), _rows(ts, w), _full((1, HEAD)),
                            _full((1, HEAD))],
                  out_specs=_rows(ts, 2 * w), out_shape=SDS((s, 2 * w), BF16))(og, proj, om, gw, mw)


def _mix_bwd(dmix, og, proj, om, gw, mw, nh, z_col):
    s = og.shape[0]
    w = nh * HEAD
    ts = _tile(s, 256, 8)

    def body(d_ref, og_ref, z_ref, om_ref, gw_ref, mw_ref, dog_ref, dz_ref, dom_ref, dgw_ref, dmw_ref):
        @pl.when(pl.program_id(0) == 0)
        def _():
            dgw_ref[...] = jnp.zeros_like(dgw_ref)
            dmw_ref[...] = jnp.zeros_like(dmw_ref)

        dgw = jnp.zeros((1, HEAD), F32)
        dmw = jnp.zeros((1, HEAD), F32)
        for h in range(nh):
            hs = slice(HEAD * h, HEAD * (h + 1))
            a, z, dy = og_ref[:, hs], z_ref[:, hs], d_ref[:, hs]
            r = lax.rsqrt(jnp.mean(a * a, axis=-1, keepdims=True) + EPS)
            ah = a * r
            sz = _silu(z)
            dz_ref[:, hs] = (dy * (ah * gw_ref[...]) * _dsilu(z)).astype(BF16)
            dn = dy * sz
            dgw = dgw + jnp.sum(dn * ah, axis=0, keepdims=True)
            dah = dn * gw_ref[...]
            dog_ref[:, hs] = r * (dah - ah * jnp.mean(dah * ah, axis=-1, keepdims=True))
            b, dyb = om_ref[:, hs], d_ref[:, w + HEAD * h:w + HEAD * (h + 1)]
            r = lax.rsqrt(jnp.mean(b * b, axis=-1, keepdims=True) + EPS)
            bh = b * r
            dmw = dmw + jnp.sum(dyb * bh, axis=0, keepdims=True)
            dbh = dyb * mw_ref[...]
            dom_ref[:, hs] = r * (dbh - bh * jnp.mean(dbh * bh, axis=-1, keepdims=True))
        dgw_ref[...] += dgw
        dmw_ref[...] += dmw

    return _pcall(body, name="mix_bwd", grid=(s // ts,),
                  in_specs=[_rows(ts, 2 * w), _rows(ts, w), _rows(ts, w, z_col), _rows(ts, w),
                            _full((1, HEAD)), _full((1, HEAD))],
                  out_specs=[_rows(ts, w), _rows(ts, w), _rows(ts, w), _full((1, HEAD)), _full((1, HEAD))],
                  out_shape=[SDS((s, w), F32), SDS((s, w), BF16), SDS((s, w), F32), SDS((1, HEAD), F32),
                             SDS((1, HEAD), F32)])(dmix, og, proj, om, gw, mw)


def _swiglu_fwd(h2, wg, wu):
    m, kdim = h2.shape
    n = wg.shape[1]
    tm, tn, tk = _tile(m, 1024), _tile(n, 512), _tile(kdim, 512)
    nk = kdim // tk

    def body(a_ref, g_ref, u_ref, act_ref, go_ref, uo_ref, gacc, uacc):
        k = pl.program_id(2)

        @pl.when(k == 0)
        def _():
            gacc[...] = jnp.zeros_like(gacc)
            uacc[...] = jnp.zeros_like(uacc)

        a = a_ref[...]
        gacc[...] += _dot(a, g_ref[...])
        uacc[...] += _dot(a, u_ref[...])

        @pl.when(k == nk - 1)
        def _():
            g, u = gacc[...], uacc[...]
            act_ref[...] = (_silu(g) * u).astype(BF16)
            go_ref[...] = g.astype(BF16)
            uo_ref[...] = u.astype(BF16)

    a_spec = pl.BlockSpec((tm, tk), lambda i, j, k: (i, k))
    b_spec = pl.BlockSpec((tk, tn), lambda i, j, k: (k, j))
    o_spec = pl.BlockSpec((tm, tn), lambda i, j, k: (i, j))
    return _pcall(body, name="swiglu_fwd", grid=(m // tm, n // tn, nk),
                  in_specs=[a_spec, b_spec, b_spec], out_specs=[o_spec] * 3,
                  out_shape=[SDS((m, n), BF16)] * 3,
                  scratch=[pltpu.VMEM((tm, tn), F32), pltpu.VMEM((tm, tn), F32)])(h2, wg, wu)


def _swiglu_bwd(dx3, wd, g, u):
    m, kdim = dx3.shape
    n = wd.shape[0]
    tm, tn, tk = _tile(m, 1024), _tile(n, 512), _tile(kdim, 512)
    nk = kdim // tk

    def body(a_ref, b_ref, g_ref, u_ref, dg_ref, du_ref, acc):
        k = pl.program_id(2)

        @pl.when(k == 0)
        def _():
            acc[...] = jnp.zeros_like(acc)

        acc[...] += _dot_nt(a_ref[...], b_ref[...])

        @pl.when(k == nk - 1)
        def _():
            da = acc[...]
            gv, uv = g_ref[...].astype(F32), u_ref[...].astype(F32)
            dg_ref[...] = (da * uv * _dsilu(gv)).astype(BF16)
            du_ref[...] = (da * _silu(gv)).astype(BF16)

    a_spec = pl.BlockSpec((tm, tk), lambda i, j, k: (i, k))
    b_spec = pl.BlockSpec((tn, tk), lambda i, j, k: (j, k))
    o_spec = pl.BlockSpec((tm, tn), lambda i, j, k: (i, j))
    return _pcall(body, name="swiglu_bwd", grid=(m // tm, n // tn, nk),
                  in_specs=[a_spec, b_spec, o_spec, o_spec], out_specs=[o_spec] * 2,
                  out_shape=[SDS((m, n), BF16)] * 2,
                  scratch=[pltpu.VMEM((tm, tn), F32)])(dx3, wd, g, u)


def _flat_tile(r):
    for t in (5008, 4096, 2048, 1024, 512, 256, 128, 64, 32, 16):
        if r % t == 0:
            return t
    return r


def _sum_pair(a, b):
    t, r, _ = a.shape
    tr = _flat_tile(r)

    def body(a_ref, b_ref, o32_ref, o16_ref):
        sm = a_ref[...].astype(F32) + b_ref[...].astype(F32)
        o32_ref[...] = sm
        o16_ref[...] = sm.astype(BF16)

    spec = pl.BlockSpec((1, tr, LANE), lambda i, j: (i, j, 0))
    return _pcall(body, name="sum_pair", grid=(t, r // tr), in_specs=[spec, spec], out_specs=[spec, spec],
                  out_shape=[SDS(a.shape, F32), SDS(a.shape, BF16)])(a, b)


def _sum_chips(own, recv):
    r = own.shape[0]
    tr = _flat_tile(r)

    def body(o_ref, r_ref, out_ref):
        acc = o_ref[...]
        for j in range(3):
            acc = acc + r_ref[j].astype(F32)
        out_ref[...] = acc

    return _pcall(body, name="sum_chips", grid=(r // tr,),
                  in_specs=[_rows(tr, LANE), pl.BlockSpec((3, tr, LANE), lambda i: (0, i, 0))],
                  out_specs=_rows(tr, LANE), out_shape=SDS(own.shape, F32))(own, recv)


def _adamw(w, g, m, v, name):
    r, c = w.shape
    tr = _tile(r, 256, 8)

    def body(w_ref, g_ref, m_ref, v_ref, d_ref, mo_ref, vo_ref):
        gv = g_ref[...]
        mn = ADAM_B1 * m_ref[...] + (1.0 - ADAM_B1) * gv
        vn = ADAM_B2 * v_ref[...] + (1.0 - ADAM_B2) * (gv * gv)
        m_hat = mn / (1.0 - ADAM_B1 ** ADAM_STEP)
        v_hat = vn / (1.0 - ADAM_B2 ** ADAM_STEP)
        d_ref[...] = -ADAM_LR * (m_hat / (jnp.sqrt(v_hat) + ADAM_EPS) + ADAM_WD * w_ref[...])
        mo_ref[...] = mn
        vo_ref[...] = vn

    spec = _rows(tr, c)
    return _pcall(body, name=name, grid=(r // tr,), in_specs=[spec] * 4, out_specs=[spec] * 3,
                  out_shape=[SDS(w.shape, F32)] * 3)(w, g, m, v)


def _place():
    x, y, c = lax.axis_index("x"), lax.axis_index("y"), lax.axis_index("c")
    chips = [(1 - x, y), (x, 1 - y), (1 - x, 1 - y)]
    return x, y, c, chips


_ANY = pl.BlockSpec(memory_space=pl.ANY)


def _comm_call(body, *, name, out_shape, n_in, scratch):
    return pl.pallas_call(body, name=name, out_shape=out_shape, in_specs=[_ANY] * n_in,
                          out_specs=_ANY, scratch_shapes=scratch)


def _all_gather_weights(wsh):
    _, r, _ = wsh.shape

    def body(w_ref, o_ref, send_sems, recv_sems, local_sem):
        x, y, c, chips = _place()
        s = 2 * x + y
        sib = (x, y, 1 - c)

        def cp(k, src, dst, to):
            return pltpu.make_async_remote_copy(src_ref=src, dst_ref=dst, send_sem=send_sems.at[k],
                                                recv_sem=recv_sems.at[k], device_id=to, device_id_type=MESH)

        mine = pltpu.make_async_copy(w_ref, o_ref.at[s], local_sem)
        mine.start()
        first = [cp(j, w_ref.at[c], o_ref.at[s, c], (tx, ty, c)) for j, (tx, ty) in enumerate(chips)]
        for f in first:
            f.start()
        passed = []
        for j, (tx, ty) in enumerate(chips):
            t = 2 * tx + ty
            cp(j, w_ref.at[c], o_ref.at[t, c], (x, y, c)).wait_recv()
            p = cp(3 + j, o_ref.at[t, c], o_ref.at[t, c], sib)
            p.start()
            passed.append(p)
        for j, (tx, ty) in enumerate(chips):
            t = 2 * tx + ty
            cp(3 + j, o_ref.at[t, 1 - c], o_ref.at[t, 1 - c], (x, y, c)).wait_recv()
        for f in first + passed:
            f.wait_send()
        mine.wait()

    return _comm_call(body, name="all_gather_weights", out_shape=SDS((4, 2, r, LANE), wsh.dtype), n_in=1,
                      scratch=[pltpu.SemaphoreType.DMA((6,)), pltpu.SemaphoreType.DMA((6,)),
                               pltpu.SemaphoreType.DMA])(wsh)


def _swap_with_sibling(theirs):
    def body(g_ref, o_ref, send_sems, recv_sems):
        x, y, c, _ = _place()
        cps = [pltpu.make_async_remote_copy(src_ref=g_ref.at[t], dst_ref=o_ref.at[t], send_sem=send_sems.at[t],
                                            recv_sem=recv_sems.at[t], device_id=(x, y, 1 - c),
                                            device_id_type=MESH) for t in range(4)]
        for cpy in cps:
            cpy.start()
        for cpy in cps:
            cpy.wait()

    return _comm_call(body, name="swap_with_sibling", out_shape=SDS(theirs.shape, theirs.dtype), n_in=1,
                      scratch=[pltpu.SemaphoreType.DMA((4,)), pltpu.SemaphoreType.DMA((4,))])(theirs)


def _exchange_chips(c16):
    _, r, _ = c16.shape

    def body(g_ref, o_ref, send_sems, recv_sems):
        x, y, c, chips = _place()
        cps = [pltpu.make_async_remote_copy(src_ref=g_ref.at[2 * tx + ty], dst_ref=o_ref.at[j],
                                            send_sem=send_sems.at[j], recv_sem=recv_sems.at[j],
                                            device_id=(tx, ty, c), device_id_type=MESH)
               for j, (tx, ty) in enumerate(chips)]
        for cpy in cps:
            cpy.start()
        for cpy in cps:
            cpy.wait()

    return _comm_call(body, name="exchange_chips", out_shape=SDS((3, r, LANE), c16.dtype), n_in=1,
                      scratch=[pltpu.SemaphoreType.DMA((3,)), pltpu.SemaphoreType.DMA((3,))])(c16)


def _share_half(tot):
    r = tot.shape[0]

    def body(t_ref, o_ref, send_sem, recv_sem, local_sem):
        x, y, c, _ = _place()
        mine = pltpu.make_async_copy(t_ref, o_ref.at[c], local_sem)
        mine.start()
        cpy = pltpu.make_async_remote_copy(src_ref=t_ref, dst_ref=o_ref.at[c], send_sem=send_sem,
                                           recv_sem=recv_sem, device_id=(x, y, 1 - c), device_id_type=MESH)
        cpy.start()
        cpy.wait()
        mine.wait()

    return _comm_call(body, name="share_half", out_shape=SDS((2, r, LANE), tot.dtype), n_in=1,
                      scratch=[pltpu.SemaphoreType.DMA, pltpu.SemaphoreType.DMA,
                               pltpu.SemaphoreType.DMA])(tot)


def _small_allreduce(pk, name):
    r = pk.shape[0]
    rels = [(dx, dy, dc) for dx in (0, 1) for dy in (0, 1) for dc in (0, 1) if dx or dy or dc]

    def body(p_ref, o_ref, buf, send_sems, recv_sems):
        x, y, c, _ = _place()
        me = 4 * x + 2 * y + c
        buf[me] = p_ref[...]
        cps = []
        for k, (dx, dy, dc) in enumerate(rels):
            to = (1 - x if dx else x, 1 - y if dy else y, 1 - c if dc else c)
            cps.append(pltpu.make_async_remote_copy(src_ref=p_ref, dst_ref=buf.at[me], send_sem=send_sems.at[k],
                                                    recv_sem=recv_sems.at[k], device_id=to,
                                                    device_id_type=MESH))
        for cpy in cps:
            cpy.start()
        for cpy in cps:
            cpy.wait()
        acc = buf[0]
        for d in range(1, 8):
            acc = acc + buf[d]
        o_ref[...] = acc

    vm = pl.BlockSpec(memory_space=pltpu.VMEM)
    return pl.pallas_call(body, name=name, out_shape=SDS(pk.shape, F32), in_specs=[vm], out_specs=vm,
                          scratch_shapes=[pltpu.VMEM((8, r, LANE), F32), pltpu.SemaphoreType.DMA((7,)),
                                          pltpu.SemaphoreType.DMA((7,))])(pk)


BIG = ("w_in", "w_uq", "w_ukv", "w_out", "w_gate", "w_up", "w_down")
ROW_SHARDED = ("w_out", "w_down")
PACK_ALIGN = 2 * 16 * LANE


def _pack_flat(parts, dtype):
    flat = jnp.concatenate([p.astype(dtype).reshape(-1) for p in parts])
    n = flat.shape[0]
    pad = (-n) % PACK_ALIGN
    if pad:
        flat = jnp.concatenate([flat, jnp.zeros((pad,), dtype)])
    return flat.reshape(2, -1, LANE)


def _unpack_flat(flat, shapes):
    out, off = [], 0
    for shp in shapes:
        n = shp[0] * shp[1]
        out.append(flat[off:off + n].reshape(shp))
        off += n
    return out


def _permute_w_in(w, nh):
    d = w.shape[0]
    g = 4 * nh * HEAD
    lr = (w.shape[1] - g - 2 * nh - ROPE) // 2
    o = g + 2 * nh
    pad = jnp.zeros((d, LANE - ROPE - 8 - nh), w.dtype)
    pad8 = jnp.zeros((d, 8 - nh), w.dtype)
    return jnp.concatenate([w[:, :g], w[:, o:o + 2 * lr], w[:, o + 2 * lr:], w[:, g:g + nh], pad8,
                            w[:, g + nh:g + 2 * nh], pad, jnp.zeros((d, LANE), w.dtype)], axis=1)


def _unpermute_w_in(wp, nh, lr):
    g = 4 * nh * HEAD
    mc = g + 2 * lr
    return jnp.concatenate([wp[:, :g], wp[:, mc + B_LANE:mc + B_LANE + nh], wp[:, mc + A_LANE:mc + A_LANE + nh],
                            wp[:, g:g + 2 * lr], wp[:, mc:mc + ROPE]], axis=1)


def _permute_w_uq(w, nh):
    lr = w.shape[0]
    w3 = w.reshape(lr, nh, HEAD + ROPE)
    return jnp.concatenate([w3, jnp.zeros((lr, nh, HEAD - ROPE), w.dtype)], axis=2).reshape(lr, nh * 2 * HEAD)


def _unpermute_w_uq(wp, nh):
    lr = wp.shape[0]
    return wp.reshape(lr, nh, 2 * HEAD)[:, :, :HEAD + ROPE].reshape(lr, nh * (HEAD + ROPE))


def _permute_w_ukv(w, nh):
    lr = w.shape[0]
    w3 = w.reshape(lr, nh, 2 * HEAD)
    kp = jnp.concatenate([w3[:, :, :HEAD], jnp.zeros((lr, nh, HEAD), w.dtype)], axis=2)
    return jnp.concatenate([kp.reshape(lr, nh * 2 * HEAD), w3[:, :, HEAD:].reshape(lr, nh * HEAD)], axis=1)


def _unpermute_w_ukv(wp, nh):
    lr = wp.shape[0]
    kp = wp[:, :nh * 2 * HEAD].reshape(lr, nh, 2 * HEAD)[:, :, :HEAD]
    vp = wp[:, nh * 2 * HEAD:].reshape(lr, nh, HEAD)
    return jnp.concatenate([kp, vp], axis=2).reshape(lr, nh * 2 * HEAD)


def _local_step(x, pos, tgt, wt, small):
    nh = small["a_log"].shape[1]
    lr = small["q_norm_w"].shape[1]
    w = nh * HEAD
    z_col, col_q, col_kv = 3, 4 * w // lr, 4 * w // lr + 1
    misc_c = 4 * w + 2 * lr
    misc_col = misc_c // LANE
    assert (4 * w) % lr == 0 and small["kv_norm_w"].shape[1] == lr

    win_p = _permute_w_in(wt["w_in"], nh)
    wuq_p = _permute_w_uq(wt["w_uq"], nh)
    wukv_p = _permute_w_ukv(wt["w_ukv"], nh)
    zl = jnp.zeros((1, LANE), F32)
    alog_l = zl.at[:, A_LANE:A_LANE + nh].set(small["a_log"])
    dtb_l = zl.at[:, A_LANE:A_LANE + nh].set(small["dt_bias"])
    conv_w = small["conv_w"]

    h1 = _norm_fwd(x, small["attn_norm_w"], "norm1")
    proj = _mm([(h1, win_p)], name="proj_in")
    gq, gk, gv, gb, gbt = _gdn_prep(proj, conv_w, alog_l, dtb_l, nh, misc_col)
    o_gdn, states = _gdn_fwd(gq, gk, gv, gb, gbt, nh)
    cqn, ckvn = _mla_norm(proj, small["q_norm_w"], small["kv_norm_w"], col_q, col_kv)
    qraw = _mm([(cqn, wuq_p)], name="proj_uq")
    kvraw = _mm([(ckvn, wukv_p)], name="proj_ukv")
    qc, kc, vv = _mla_rope(qraw, kvraw, proj, pos, nh, misc_col)
    o_mla, lse = _mla_fwd(qc, kc, vv, nh)
    mixed = _mix_fwd(o_gdn, proj, o_mla, small["gdn_norm_w"], small["mla_out_norm_w"], nh, z_col)
    x2 = _mm([(mixed, wt["w_out"])], name="proj_out", res=x)
    h2 = _norm_fwd(x2, small["ffn_norm_w"], "norm2")
    act, gpre, upre = _swiglu_fwd(h2, wt["w_gate"], wt["w_up"])
    x3 = _mm([(act, wt["w_down"])], name="proj_down", res=x2)
    dx3, d_final, loss = _final_loss(x3, tgt, small["final_norm_w"])

    gm, gs = {}, {"final_norm_w": d_final}
    dgate, dup = _swiglu_bwd(dx3, wt["w_down"], gpre, upre)
    gm["w_down"] = _mm([(act, dx3)], name="dw_down", ta=True)
    dh2 = _mm([(dgate, wt["w_gate"]), (dup, wt["w_up"])], name="dh2", tb=True)
    gm["w_gate"] = _mm([(h2, dgate)], name="dw_gate", ta=True)
    gm["w_up"] = _mm([(h2, dup)], name="dw_up", ta=True)
    dx2, gs["ffn_norm_w"] = _norm_bwd(dh2, x2, small["ffn_norm_w"], dx3, "norm2_bwd")
    dmix = _mm([(dx2, wt["w_out"])], name="dmix", tb=True)
    gm["w_out"] = _mm([(mixed, dx2)], name="dw_out", ta=True)
    d_ogdn, dz, d_omla, gs["gdn_norm_w"], gs["mla_out_norm_w"] = _mix_bwd(
        dmix, o_gdn, proj, o_mla, small["gdn_norm_w"], small["mla_out_norm_w"], nh, z_col)
    dqc = _mla_bwd_dq(qc, kc, vv, o_mla, d_omla, lse, nh)
    dkc, dvv = _mla_bwd_dkv(qc, kc, vv, o_mla, d_omla, lse, nh)
    dqraw, dkvraw, dkr = _mla_rope_bwd(dqc, dkc, dvv, pos, nh)
    dcqn = _mm([(dqraw, wuq_p)], name="dcqn", tb=True)
    dckvn = _mm([(dkvraw, wukv_p)], name="dckvn", tb=True)
    gm["w_uq"] = _unpermute_w_uq(_mm([(cqn, dqraw)], name="dw_uq", ta=True), nh)
    gm["w_ukv"] = _unpermute_w_ukv(_mm([(ckvn, dkvraw)], name="dw_ukv", ta=True), nh)
    dcq, dckv, gs["q_norm_w"], gs["kv_norm_w"] = _mla_norm_bwd(
        proj, small["q_norm_w"], small["kv_norm_w"], dcqn, dckvn, col_q, col_kv)
    dgq, dgk, dgv, dgb = _gdn_bwd(gq, gk, gv, gb, gbt, states, d_ogdn, nh)
    dconv, dmisc, gs["conv_w"], dal, ddb = _gdn_prep_bwd(
        proj, conv_w, alog_l, dtb_l, dgq, dgk, dgv, dgb, dkr, nh, misc_col)
    gs["a_log"] = dal[:, A_LANE:A_LANE + nh]
    gs["dt_bias"] = ddb[:, A_LANE:A_LANE + nh]
    dqkv = _conv_bwd_input(dconv, conv_w)
    dproj = jnp.concatenate([dqkv, dz, dcq, dckv, dmisc, jnp.zeros((x.shape[0], LANE), BF16)], axis=1)
    dh1 = _mm([(dproj, win_p)], name="dh1", tb=True)
    gm["w_in"] = _unpermute_w_in(_mm([(h1, dproj)], name="dw_in", ta=True), nh, lr)
    grad_x, gs["attn_norm_w"] = _norm_bwd(dh1, x, small["attn_norm_w"], dx2, "norm1_bwd")
    return loss, grad_x, gm, gs


SMALL = ("attn_norm_w", "ffn_norm_w", "final_norm_w", "q_norm_w", "kv_norm_w", "gdn_norm_w",
         "mla_out_norm_w", "a_log", "dt_bias")
WEIGHTS = ("attn_norm_w", "w_in", "conv_w", "a_log", "dt_bias", "gdn_norm_w", "q_norm_w", "w_uq",
           "kv_norm_w", "w_ukv", "mla_out_norm_w", "w_out", "ffn_norm_w", "w_gate", "w_up", "w_down",
           "final_norm_w")


def _pack_small(vecs):
    flat = jnp.concatenate([v.astype(F32).reshape(-1) for v in vecs])
    pad = (-flat.shape[0]) % (8 * LANE)
    return jnp.concatenate([flat, jnp.zeros((pad,), F32)]).reshape(-1, LANE)


def kernel(x, positions, attn_norm_w, w_in, conv_w, a_log, dt_bias, gdn_norm_w, q_norm_w, w_uq, kv_norm_w, w_ukv, mla_out_norm_w, w_out, ffn_norm_w, w_gate, w_up, w_down, final_norm_w, loss_target, m_attn_norm_w, m_w_in, m_conv_w, m_a_log, m_dt_bias, m_gdn_norm_w, m_q_norm_w, m_w_uq, m_kv_norm_w, m_w_ukv, m_mla_out_norm_w, m_w_out, m_ffn_norm_w, m_w_gate, m_w_up, m_w_down, m_final_norm_w, v_attn_norm_w, v_w_in, v_conv_w, v_a_log, v_dt_bias, v_gdn_norm_w, v_q_norm_w, v_w_uq, v_kv_norm_w, v_w_ukv, v_mla_out_norm_w, v_w_out, v_ffn_norm_w, v_w_gate, v_w_up, v_w_down, v_final_norm_w):
    args = dict(locals())
    xi, yi, ci = lax.axis_index("x"), lax.axis_index("y"), lax.axis_index("c")
    chip = 2 * xi + yi

    def two_d(a):
        return a.reshape(a.shape[-2:]) if a.ndim >= 2 else a.reshape(1, -1)

    wloc = {n: two_d(args[n]) for n in WEIGHTS}
    mloc = {n: two_d(args["m_" + n]) for n in WEIGHTS}
    vloc = {n: two_d(args["v_" + n]) for n in WEIGHTS}

    shard_shapes = [wloc[n].shape for n in BIG]
    gath = _all_gather_weights(_pack_flat([wloc[n] for n in BIG], BF16))
    nflat = sum(a * b for a, b in shard_shapes)
    gflat = gath.reshape(4, -1)[:, :nflat]
    per_chip = [_unpack_flat(gflat[t], shard_shapes) for t in range(4)]
    wt = {}
    for k, n in enumerate(BIG):
        wt[n] = jnp.concatenate([per_chip[t][k] for t in range(4)], axis=0 if n in ROW_SHARDED else 1)
    cw = wloc["conv_w"]
    cshard = cw.shape[1]
    cfull = jnp.zeros((CONV, 4 * cshard), F32)
    cfull = lax.dynamic_update_slice(cfull, jnp.where(ci == 0, cw, 0.0), (0, chip * cshard))
    conv_rows = _pack_small([cfull]).shape[0]
    conv_full = _small_allreduce(_pack_small([cfull]), "gather_conv_w").reshape(-1)[:CONV * 4 * cshard]
    conv_full = conv_full.reshape(CONV, 4 * cshard)
    del conv_rows

    small = {n: wloc[n] for n in SMALL}
    small["conv_w"] = conv_full

    pos = positions.reshape(-1, 1).astype(F32)
    loss, grad_x, gm, gs = _local_step(two_d(x), pos, two_d(loss_target), wt, small)

    def dest_pieces(t):
        out = []
        for n, shp in zip(BIG, shard_shapes):
            g = gm[n]
            if n in ROW_SHARDED:
                out.append(g[t * shp[0]:(t + 1) * shp[0], :])
            else:
                out.append(g[:, t * shp[1]:(t + 1) * shp[1]])
        return out

    gsend = jnp.stack([_pack_flat(dest_pieces(t), BF16) for t in range(4)])
    g_mine = lax.dynamic_index_in_dim(gsend, ci, 1, keepdims=False)
    g_theirs = lax.dynamic_index_in_dim(gsend, 1 - ci, 1, keepdims=False)
    from_sib = _swap_with_sibling(g_theirs)
    c32, c16 = _sum_pair(g_mine, from_sib)
    recv = _exchange_chips(c16)
    own = lax.dynamic_index_in_dim(c32, chip, 0, keepdims=False)
    total = _sum_chips(own, recv)
    gfull = _share_half(total).reshape(-1)[:nflat]
    gbig = dict(zip(BIG, _unpack_flat(gfull, shard_shapes)))

    small_names = SMALL + ("conv_w",)
    pk = _pack_small([gs[n] for n in small_names] + [loss])
    red = _small_allreduce(pk, "reduce_small").reshape(-1)
    gsm, off = {}, 0
    for n in small_names:
        shp = gs[n].shape
        gsm[n] = red[off:off + shp[0] * shp[1]].reshape(shp)
        off += shp[0] * shp[1]
    loss_out = red[off]
    gsm["conv_w"] = lax.dynamic_slice(gsm["conv_w"], (0, chip * cshard), (CONV, cshard))

    grads, deltas, new_m, new_v = {}, {}, {}, {}
    for n in BIG + ("conv_w",):
        grads[n] = gbig[n] if n in gbig else gsm[n]
        deltas[n], new_m[n], new_v[n] = _adamw(wloc[n], grads[n], mloc[n], vloc[n], "adamw_" + n)
    sm_shapes = [wloc[n].shape for n in SMALL]
    pd, pm, pv = _adamw(_pack_small([wloc[n] for n in SMALL]), _pack_small([gsm[n] for n in SMALL]),
                        _pack_small([mloc[n] for n in SMALL]), _pack_small([vloc[n] for n in SMALL]),
                        "adamw_small")
    for dst, packed in ((deltas, pd), (new_m, pm), (new_v, pv)):
        flat, off = packed.reshape(-1), 0
        for n, shp in zip(SMALL, sm_shapes):
            dst[n] = flat[off:off + shp[0] * shp[1]].reshape(shp)
            off += shp[0] * shp[1]
    for n in SMALL:
        grads[n] = gsm[n]

    def like(n, a):
        return a.reshape(args[n].shape)

    outs = [loss_out.reshape(()), grad_x.reshape(x.shape)]
    for group in (grads, deltas, new_m, new_v):
        outs += [like(n, group[n]) for n in WEIGHTS]
    return tuple(outs)
```

```python
import functools

import jax
import jax.numpy as jnp
from jax import lax
from jax.experimental import pallas as pl
from jax.experimental.pallas import tpu as pltpu

F32, BF16 = jnp.float32, jnp.bfloat16
SDS = jax.ShapeDtypeStruct
MESH = pl.DeviceIdType.MESH

HEAD = 128
ROPE = 64
CHUNK = 64
PAIR = 2 * CHUNK
CONV = 4
EPS = 1e-6
ROPE_THETA = 10000.0
LANE = 128
B_LANE = 64
A_LANE = 72
VMEM_LIMIT = 48 * 1024 * 1024
MLA_BLOCK = 512
LOG2E = 1.4426950408889634
LN2 = 0.6931471805599453
SM_SCALE = (HEAD + ROPE) ** -0.5

ADAM_LR = 0.001
ADAM_B1 = 0.9
ADAM_B2 = 0.999
ADAM_EPS = 1e-08
ADAM_WD = 0.01
ADAM_STEP = 10


def _tile(n, pref, mult=LANE):
    if n <= pref:
        return n
    t = (pref // mult) * mult
    while t >= mult:
        if n % t == 0:
            return t
        t -= mult
    return n


def _pcall(body, *, name, grid, in_specs, out_specs, out_shape, scratch=()):
    return pl.pallas_call(
        body, name=name, grid=grid, in_specs=in_specs, out_specs=out_specs,
        out_shape=out_shape, scratch_shapes=list(scratch),
        compiler_params=pltpu.CompilerParams(
            dimension_semantics=("arbitrary",) * len(grid), vmem_limit_bytes=VMEM_LIMIT))


def _rows(ts, width, col=0):
    return pl.BlockSpec((ts, width), lambda i: (i, col))


def _full(shape):
    nd = len(shape)
    return pl.BlockSpec(shape, lambda i: (0,) * nd)


def _dot(a, b):
    return jnp.dot(a.astype(BF16), b.astype(BF16), preferred_element_type=F32)


def _dot_nt(a, b):
    return lax.dot_general(a.astype(BF16), b.astype(BF16), (((1,), (1,)), ((), ())),
                           preferred_element_type=F32)


def _dot_tn(a, b):
    return lax.dot_general(a.astype(BF16), b.astype(BF16), (((0,), (0,)), ((), ())),
                           preferred_element_type=F32)


def _sigmoid(x):
    return 1.0 / (1.0 + jnp.exp(-x))


def _silu(x):
    return x * _sigmoid(x)


def _dsilu(x):
    s = _sigmoid(x)
    return s * (1.0 + x * (1.0 - s))


def _lane_iota(shape):
    return lax.broadcasted_iota(jnp.int32, shape, len(shape) - 1)


def _col(block, idx):
    return jnp.sum(jnp.where(_lane_iota(block.shape) == idx, block, 0.0), axis=-1, keepdims=True)


def _mm(pairs, *, name, ta=False, tb=False, out_dtype=F32, res=None, a_off=0, a_k=None,
        tm=1024, tn=1024, tk=2048):
    a0, b0 = pairs[0]
    if ta:
        kdim, m = a0.shape
    else:
        m = a0.shape[0]
        kdim = a_k if a_k is not None else a0.shape[1]
    n = b0.shape[0] if tb else b0.shape[1]
    tm, tn, tk = _tile(m, tm), _tile(n, tn), _tile(kdim, tk)
    assert m % tm == 0 and n % tn == 0 and kdim % tk == 0 and a_off % tk == 0
    nk, koff, npair = kdim // tk, a_off // tk, len(pairs)
    dims = (((0 if ta else 1,), (1 if tb else 0,)), ((), ()))

    def body(*refs):
        o_ref, acc = refs[-2], refs[-1]
        k = pl.program_id(2)

        @pl.when(k == 0)
        def _():
            acc[...] = jnp.zeros_like(acc)

        tot = None
        for p in range(npair):
            d = lax.dot_general(refs[2 * p][...].astype(BF16), refs[2 * p + 1][...].astype(BF16),
                                dims, preferred_element_type=F32)
            tot = d if tot is None else tot + d
        acc[...] += tot

        @pl.when(k == nk - 1)
        def _():
            r = acc[...]
            if res is not None:
                r = r + refs[2 * npair][...]
            o_ref[...] = r.astype(out_dtype)

    if ta:
        a_spec = pl.BlockSpec((tk, tm), lambda i, j, k: (k, i))
    else:
        a_spec = pl.BlockSpec((tm, tk), lambda i, j, k: (i, k + koff))
    if tb:
        b_spec = pl.BlockSpec((tn, tk), lambda i, j, k: (j, k))
    else:
        b_spec = pl.BlockSpec((tk, tn), lambda i, j, k: (k, j))
    o_spec = pl.BlockSpec((tm, tn), lambda i, j, k: (i, j))
    in_specs, args = [], []
    for a, b in pairs:
        in_specs += [a_spec, b_spec]
        args += [a, b]
    if res is not None:
        in_specs.append(o_spec)
        args.append(res)
    return _pcall(body, name=name, grid=(m // tm, n // tn, nk), in_specs=in_specs,
                  out_specs=o_spec, out_shape=SDS((m, n), out_dtype),
                  scratch=[pltpu.VMEM((tm, tn), F32)])(*args)


def _norm_fwd(x, w, name):
    s, d = x.shape
    ts = _tile(s, 512, 8)

    def body(x_ref, w_ref, h_ref):
        xv = x_ref[...]
        r = lax.rsqrt(jnp.mean(xv * xv, axis=-1, keepdims=True) + EPS)
        h_ref[...] = (xv * r * w_ref[...]).astype(BF16)

    return _pcall(body, name=name, grid=(s // ts,), in_specs=[_rows(ts, d), _full((1, d))],
                  out_specs=_rows(ts, d), out_shape=SDS((s, d), BF16))(x, w)


def _norm_bwd(dh, x, w, dres, name, with_bf16):
    s, d = x.shape
    ts = _tile(s, 256, 8)

    def body(dh_ref, x_ref, w_ref, dres_ref, dx_ref, dw_ref, *dx16_ref):
        @pl.when(pl.program_id(0) == 0)
        def _():
            dw_ref[...] = jnp.zeros_like(dw_ref)

        xv, dhv = x_ref[...], dh_ref[...]
        r = lax.rsqrt(jnp.mean(xv * xv, axis=-1, keepdims=True) + EPS)
        xh = xv * r
        dw_ref[...] += jnp.sum(dhv * xh, axis=0, keepdims=True)
        dxh = dhv * w_ref[...]
        dx = dres_ref[...] + r * (dxh - xh * jnp.mean(dxh * xh, axis=-1, keepdims=True))
        dx_ref[...] = dx
        for ref in dx16_ref:
            ref[...] = dx.astype(BF16)

    extra = 1 if with_bf16 else 0
    return _pcall(body, name=name, grid=(s // ts,),
                  in_specs=[_rows(ts, d), _rows(ts, d), _full((1, d)), _rows(ts, d)],
                  out_specs=[_rows(ts, d), _full((1, d))] + [_rows(ts, d)] * extra,
                  out_shape=[SDS((s, d), F32), SDS((1, d), F32)] + [SDS((s, d), BF16)] * extra)(
                      dh, x, w, dres)


def _final_loss(x3, tgt, w):
    s, d = x3.shape
    ts = _tile(s, 256, 8)

    def body(x_ref, t_ref, w_ref, dx_ref, dw_ref, loss_ref, dx16_ref):
        @pl.when(pl.program_id(0) == 0)
        def _():
            dw_ref[...] = jnp.zeros_like(dw_ref)
            loss_ref[...] = jnp.zeros_like(loss_ref)

        xv, wv = x_ref[...], w_ref[...]
        r = lax.rsqrt(jnp.mean(xv * xv, axis=-1, keepdims=True) + EPS)
        xh = xv * r
        err = xh * wv - t_ref[...]
        row = jnp.mean(err * err, axis=-1, keepdims=True)
        loss_ref[...] += 0.5 * jnp.sum(row, axis=0, keepdims=True)
        dy = err * (1.0 / d)
        dw_ref[...] += jnp.sum(dy * xh, axis=0, keepdims=True)
        dxh = dy * wv
        dx = r * (dxh - xh * jnp.mean(dxh * xh, axis=-1, keepdims=True))
        dx_ref[...] = dx
        dx16_ref[...] = dx.astype(BF16)

    return _pcall(body, name="final_loss", grid=(s // ts,),
                  in_specs=[_rows(ts, d), _rows(ts, d), _full((1, d))],
                  out_specs=[_rows(ts, d), _full((1, d)), _full((1, 1)), _rows(ts, d)],
                  out_shape=[SDS((s, d), F32), SDS((1, d), F32), SDS((1, 1), F32), SDS((s, d), BF16)])(
                      x3, tgt, w)


def _shift_down(cur, halo, s):
    if s == 0:
        return cur
    row8 = lax.broadcasted_iota(jnp.int32, halo.shape, 0)
    r = pltpu.roll(cur, s, 0)
    top = jnp.where(row8 < s, pltpu.roll(halo, s, 0), r[0:8])
    return jnp.concatenate([top, r[8:]], axis=0)


def _shift_up(cur, halo, s):
    if s == 0:
        return cur
    ts = cur.shape[0]
    row8 = lax.broadcasted_iota(jnp.int32, halo.shape, 0)
    r = pltpu.roll(cur, ts - s, 0)
    bot = jnp.where(row8 >= 8 - s, pltpu.roll(halo, 8 - s, 0), r[ts - 8:ts])
    return jnp.concatenate([r[:ts - 8], bot], axis=0)


def _chunk_tri(ts, upper):
    i = lax.broadcasted_iota(jnp.int32, (ts, ts), 0)
    j = lax.broadcasted_iota(jnp.int32, (ts, ts), 1)
    same = jnp.right_shift(i, 6) == jnp.right_shift(j, 6)
    return jnp.where(same & ((j >= i) if upper else (j <= i)), 1.0, 0.0).astype(F32)


def _gate_values(m, alog, dtb):
    lane = _lane_iota(m.shape)
    beta = _sigmoid(m)
    xg = m + dtb
    sp = jnp.maximum(xg, 0.0) + jnp.log(1.0 + jnp.exp(-jnp.abs(xg)))
    ga = (lane >= A_LANE) & (lane < A_LANE + 8)
    g = jnp.where(ga, -jnp.exp(alog) * sp, 0.0)
    return beta, g, xg, ga


def _l2_heads(a, nh, scale):
    outs, rs = [], []
    for h in range(nh):
        ah = a[:, HEAD * h:HEAD * (h + 1)]
        r = lax.rsqrt(jnp.sum(ah * ah, axis=-1, keepdims=True) + EPS)
        outs.append(ah * (r * scale))
        rs.append(r)
    return jnp.concatenate(outs, axis=-1), rs


def _gdn_prep(proj, conv_w, alog_l, dtb_l, nh, misc_col):
    s = proj.shape[0]
    w = nh * HEAD
    ts = _tile(s, 256, PAIR)
    hb = ts // 8

    def body(cur_ref, halo_ref, misc_ref, cw_ref, al_ref, db_ref, q_ref, k_ref, v_ref, gb_ref, gbt_ref):
        first = pl.program_id(0) == 0
        outs = (q_ref, k_ref, v_ref)
        for sec in range(3):
            cs = slice(sec * w, (sec + 1) * w)
            cur = cur_ref[:, cs]
            halo = jnp.where(first, 0.0, halo_ref[:, cs])
            pre = None
            for j in range(CONV):
                term = cw_ref[j:j + 1, cs] * _shift_down(cur, halo, CONV - 1 - j)
                pre = term if pre is None else pre + term
            act = _silu(pre)
            if sec == 0:
                act, _ = _l2_heads(act, nh, HEAD ** -0.5)
            elif sec == 1:
                act, _ = _l2_heads(act, nh, 1.0)
            outs[sec][...] = act
        m = misc_ref[...]
        lane = _lane_iota(m.shape)
        beta, g, _, ga = _gate_values(m, al_ref[...], db_ref[...])
        gcc = jnp.dot(_chunk_tri(ts, False), g, precision=lax.Precision.HIGHEST,
                      preferred_element_type=F32)
        gb = jnp.where((lane >= B_LANE) & (lane < B_LANE + 8), beta, jnp.where(ga, gcc, 0.0))
        gb_ref[...] = gb
        gbt_ref[...] = gb.T

    return _pcall(
        body, name="gdn_prep", grid=(s // ts,),
        in_specs=[_rows(ts, 3 * w),
                  pl.BlockSpec((8, 3 * w), lambda i: (jnp.maximum(i * hb - 1, 0), 0)),
                  _rows(ts, LANE, misc_col), _full((CONV, 3 * w)), _full((1, LANE)), _full((1, LANE))],
        out_specs=[_rows(ts, w), _rows(ts, w), _rows(ts, w), _rows(ts, LANE),
                   pl.BlockSpec((LANE, ts), lambda i: (0, i))],
        out_shape=[SDS((s, w), F32), SDS((s, w), F32), SDS((s, w), F32), SDS((s, LANE), F32),
                   SDS((LANE, s), F32)])(proj, proj, proj, conv_w, alog_l, dtb_l)


def _gdn_prep_bwd(proj, conv_w, alog_l, dtb_l, dq, dk, dv, dgb, dkr, nh, misc_col):
    s = proj.shape[0]
    w = nh * HEAD
    ts = _tile(s, 256, PAIR)
    hb = ts // 8

    def body(cur_ref, halo_ref, misc_ref, cw_ref, al_ref, db_ref, dq_ref, dk_ref, dv_ref, dgb_ref,
             dkr_ref, dc_ref, dm_ref, dcw_ref, dal_ref, ddb_ref):
        first = pl.program_id(0) == 0

        @pl.when(first)
        def _():
            dcw_ref[...] = jnp.zeros_like(dcw_ref)
            dal_ref[...] = jnp.zeros_like(dal_ref)
            ddb_ref[...] = jnp.zeros_like(ddb_ref)

        dins = (dq_ref, dk_ref, dv_ref)
        for sec in range(3):
            cs = slice(sec * w, (sec + 1) * w)
            cur = cur_ref[:, cs]
            halo = jnp.where(first, 0.0, halo_ref[:, cs])
            us = [_shift_down(cur, halo, CONV - 1 - j) for j in range(CONV)]
            pre = None
            for j in range(CONV):
                term = cw_ref[j:j + 1, cs] * us[j]
                pre = term if pre is None else pre + term
            act = _silu(pre)
            dout = dins[sec][...]
            if sec < 2:
                scale = HEAD ** -0.5 if sec == 0 else 1.0
                parts = []
                for h in range(nh):
                    hs = slice(HEAD * h, HEAD * (h + 1))
                    ah = act[:, hs]
                    r = lax.rsqrt(jnp.sum(ah * ah, axis=-1, keepdims=True) + EPS)
                    ahat = ah * r
                    dy = dout[:, hs]
                    parts.append((scale * r) * (dy - ahat * jnp.sum(dy * ahat, axis=-1, keepdims=True)))
                dact = jnp.concatenate(parts, axis=-1)
            else:
                dact = dout
            dconv = dact * _dsilu(pre)
            dc_ref[:, cs] = dconv
            for j in range(CONV):
                dcw_ref[j:j + 1, cs] += jnp.sum(dconv * us[j], axis=0, keepdims=True)
        m = misc_ref[...]
        lane = _lane_iota(m.shape)
        al = al_ref[...]
        beta, g, xg, ga = _gate_values(m, al, db_ref[...])
        dgbv = dgb_ref[...]
        dg = jnp.dot(_chunk_tri(ts, True), jnp.where(ga, dgbv, 0.0), precision=lax.Precision.HIGHEST,
                     preferred_element_type=F32)
        da_raw = jnp.where(ga, dg * (-jnp.exp(al)) * _sigmoid(xg), 0.0)
        db_raw = jnp.where((lane >= B_LANE) & (lane < B_LANE + 8), dgbv * beta * (1.0 - beta), 0.0)
        dal_ref[...] += jnp.sum(dg * g, axis=0, keepdims=True)
        ddb_ref[...] += jnp.sum(da_raw, axis=0, keepdims=True)
        dm_ref[...] = (dkr_ref[...] + da_raw + db_raw).astype(BF16)

    return _pcall(
        body, name="gdn_prep_bwd", grid=(s // ts,),
        in_specs=[_rows(ts, 3 * w),
                  pl.BlockSpec((8, 3 * w), lambda i: (jnp.maximum(i * hb - 1, 0), 0)),
                  _rows(ts, LANE, misc_col), _full((CONV, 3 * w)), _full((1, LANE)), _full((1, LANE)),
                  _rows(ts, w), _rows(ts, w), _rows(ts, w), _rows(ts, LANE), _rows(ts, LANE)],
        out_specs=[_rows(ts, 3 * w), _rows(ts, LANE), _full((CONV, 3 * w)), _full((1, LANE)),
                   _full((1, LANE))],
        out_shape=[SDS((s, 3 * w), F32), SDS((s, LANE), BF16), SDS((CONV, 3 * w), F32),
                   SDS((1, LANE), F32), SDS((1, LANE), F32)])(
                       proj, proj, proj, conv_w, alog_l, dtb_l, dq, dk, dv, dgb, dkr)


def _conv_bwd_input(dconv, conv_w):
    s, c = dconv.shape
    ts = _tile(s, 256, 8)
    hb = ts // 8
    nblk8 = s // 8
    nt = s // ts

    def body(cur_ref, nxt_ref, cw_ref, o_ref):
        last = pl.program_id(0) == nt - 1
        cur = cur_ref[...]
        halo = jnp.where(last, 0.0, nxt_ref[...])
        acc = None
        for j in range(CONV):
            term = cw_ref[j:j + 1, :] * _shift_up(cur, halo, CONV - 1 - j)
            acc = term if acc is None else acc + term
        o_ref[...] = acc.astype(BF16)

    return _pcall(
        body, name="conv_bwd_input", grid=(nt,),
        in_specs=[_rows(ts, c),
                  pl.BlockSpec((8, c), lambda i: (jnp.minimum((i + 1) * hb, nblk8 - 1), 0)),
                  _full((CONV, c))],
        out_specs=_rows(ts, c), out_shape=SDS((s, c), BF16))(dconv, dconv, conv_w)


def _inv_unit_lower(a):
    n = a.shape[0]
    i = lax.broadcasted_iota(jnp.int32, (n, n), 0)
    j = lax.broadcasted_iota(jnp.int32, (n, n), 1)
    t = jnp.where(i == j, 1.0, 0.0) - a
    x = a
    for _ in range(5):
        x = _dot(x, x)
        t = t + _dot(t, x)
    return t


def _pair_common(q, k, gcol, grow, bcol):
    i = lax.broadcasted_iota(jnp.int32, (PAIR, PAIR), 0)
    j = lax.broadcasted_iota(jnp.int32, (PAIR, PAIR), 1)
    same = jnp.right_shift(i, 6) == jnp.right_shift(j, 6)
    tril = same & (i >= j)
    strict = same & (i > j)
    dec = jnp.where(tril, jnp.exp(jnp.minimum(gcol - grow, 0.0)), 0.0)
    kk = _dot_nt(k, k)
    a = jnp.where(strict, bcol * kk * dec, 0.0)
    t = _inv_unit_lower(a)
    p = _dot_nt(q, k) * dec
    return dec, kk, a, t, p, tril, strict


def _ext(v, a):
    z = jnp.zeros_like(v)
    return jnp.concatenate([v, z] if a == 0 else [z, v], axis=0)


def _gdn_fwd(q, k, v, gb, gbt, nh):
    s = q.shape[0]
    w = nh * HEAD
    npair = s // PAIR

    def body(q_ref, k_ref, v_ref, gb_ref, gbt_ref, o_ref, st_ref, s_ref):
        @pl.when(pl.program_id(0) == 0)
        def _():
            s_ref[...] = jnp.zeros_like(s_ref)

        for h in range(nh):
            hs = slice(HEAD * h, HEAD * (h + 1))
            qh, kh, vh = q_ref[:, hs], k_ref[:, hs], v_ref[:, hs]
            gcol = _col(gb_ref[...], A_LANE + h)
            bcol = _col(gb_ref[...], B_LANE + h)
            grow = gbt_ref[A_LANE + h:A_LANE + h + 1, :]
            _, _, _, t, p, _, _ = _pair_common(qh, kh, gcol, grow, bcol)
            eg = jnp.exp(gcol)
            qg, kg = qh * eg, kh * eg
            outs = []
            for a in range(2):
                sl = slice(CHUNK * a, CHUNK * (a + 1))
                st = s_ref[h]
                st_ref[a, h] = st
                r = vh[sl] - _dot(kg[sl], st)
                vn = _dot(t[sl], _ext(bcol[sl] * r, a))
                outs.append(_dot(qg[sl], st) + _dot(p[sl], _ext(vn, a)))
                gl = _col(grow, CHUNK * (a + 1) - 1)
                kd = kh[sl] * jnp.exp(gl - gcol[sl])
                s_ref[h] = jnp.exp(gl) * st + _dot_tn(kd, vn)
            o_ref[:, hs] = jnp.concatenate(outs, axis=0)

    return _pcall(
        body, name="gdn_fwd", grid=(npair,),
        in_specs=[_rows(PAIR, w), _rows(PAIR, w), _rows(PAIR, w), _rows(PAIR, LANE),
                  pl.BlockSpec((LANE, PAIR), lambda i: (0, i))],
        out_specs=[_rows(PAIR, w), pl.BlockSpec((2, nh, HEAD, HEAD), lambda i: (i, 0, 0, 0))],
        out_shape=[SDS((s, w), F32), SDS((2 * npair, nh, HEAD, HEAD), F32)],
        scratch=[pltpu.VMEM((nh, HEAD, HEAD), F32)])(q, k, v, gb, gbt)


def _gdn_bwd(q, k, v, gb, gbt, states, do, nh):
    s = q.shape[0]
    w = nh * HEAD
    npair = s // PAIR
    rev = lambda i: (npair - 1 - i, 0)

    def body(q_ref, k_ref, v_ref, gb_ref, gbt_ref, st_ref, do_ref, dq_ref, dk_ref, dv_ref, dgb_ref,
             ds_ref):
        @pl.when(pl.program_id(0) == 0)
        def _():
            ds_ref[...] = jnp.zeros_like(ds_ref)

        lane = _lane_iota((PAIR, LANE))
        row = lax.broadcasted_iota(jnp.int32, (CHUNK, 1), 0)
        dgb = jnp.zeros((PAIR, LANE), F32)
        for h in range(nh):
            hs = slice(HEAD * h, HEAD * (h + 1))
            qh, kh, vh, doh = q_ref[:, hs], k_ref[:, hs], v_ref[:, hs], do_ref[:, hs]
            gcol = _col(gb_ref[...], A_LANE + h)
            bcol = _col(gb_ref[...], B_LANE + h)
            grow = gbt_ref[A_LANE + h:A_LANE + h + 1, :]
            dec, kk, amat, t, p, tril, strict = _pair_common(qh, kh, gcol, grow, bcol)
            tt, pt = t.T, p.T
            eg = jnp.exp(gcol)
            qg, kg = qh * eg, kh * eg
            rs, vns = [], []
            for a in range(2):
                sl = slice(CHUNK * a, CHUNK * (a + 1))
                r = vh[sl] - _dot(kg[sl], st_ref[a, h])
                rs.append(r)
                vns.append(_dot(t[sl], _ext(bcol[sl] * r, a)))
            dsn = ds_ref[h]
            dqs, dks, dvs, dgcs, dbs, drbs = [None] * 2, [None] * 2, [None] * 2, [None] * 2, [None] * 2, [None] * 2
            for a in (1, 0):
                sl = slice(CHUNK * a, CHUNK * (a + 1))
                st = st_ref[a, h]
                gl = _col(grow, CHUNK * (a + 1) - 1)
                egl = jnp.exp(gl)
                dk_dec = jnp.exp(gl - gcol[sl])
                kd = kh[sl] * dk_dec
                d_vn = _dot(pt[sl], _ext(doh[sl], a)) + _dot(kd, dsn)
                d_qg = _dot_nt(doh[sl], st)
                d_rb = _dot(tt[sl], _ext(d_vn, a))
                dbs[a] = jnp.sum(d_rb * rs[a], axis=-1, keepdims=True)
                d_r = bcol[sl] * d_rb
                d_kg = -_dot_nt(d_r, st)
                d_kd = _dot_nt(vns[a], dsn)
                dgl = egl * jnp.sum(dsn * st, keepdims=True) + jnp.sum(d_kd * kd, keepdims=True)
                dgc = (jnp.sum(d_qg * qg[sl], axis=-1, keepdims=True)
                       + jnp.sum(d_kg * kg[sl], axis=-1, keepdims=True)
                       - jnp.sum(d_kd * kd, axis=-1, keepdims=True))
                dgcs[a] = dgc + jnp.where(row == CHUNK - 1, dgl, 0.0)
                dqs[a] = d_qg * eg[sl]
                dks[a] = d_kg * eg[sl] + d_kd * dk_dec
                dvs[a] = d_r
                drbs[a] = d_rb
                dsn = _dot_tn(qg[sl], doh[sl]) + egl * dsn - _dot_tn(kg[sl], d_r)
            ds_ref[h] = dsn
            cat = lambda xs: jnp.concatenate(xs, axis=0)
            vn, d_rb = cat(vns), cat(drbs)
            dp = jnp.where(tril, _dot_nt(doh, vn), 0.0)
            dam = jnp.where(strict, -_dot_nt(d_rb, vn), 0.0)
            g_p = dp * dec
            g_a = dam * dec
            gbk = bcol * g_a
            dq_ref[:, hs] = cat(dqs) + _dot(g_p, kh)
            dk_ref[:, hs] = cat(dks) + _dot_tn(g_p, qh) + _dot(gbk, kh) + _dot_tn(gbk, kh)
            dv_ref[:, hs] = cat(dvs)
            dbeta = cat(dbs) + jnp.sum(g_a * kk, axis=-1, keepdims=True)
            mm = dp * p + dam * amat
            dgc = cat(dgcs) + jnp.sum(mm, axis=-1, keepdims=True) - jnp.sum(mm.T, axis=-1, keepdims=True)
            dgb = dgb + jnp.where(lane == A_LANE + h, dgc, 0.0) + jnp.where(lane == B_LANE + h, dbeta, 0.0)
        dgb_ref[...] = dgb

    return _pcall(
        body, name="gdn_bwd", grid=(npair,),
        in_specs=[pl.BlockSpec((PAIR, w), rev), pl.BlockSpec((PAIR, w), rev), pl.BlockSpec((PAIR, w), rev),
                  pl.BlockSpec((PAIR, LANE), rev),
                  pl.BlockSpec((LANE, PAIR), lambda i: (0, npair - 1 - i)),
                  pl.BlockSpec((2, nh, HEAD, HEAD), lambda i: (npair - 1 - i, 0, 0, 0)),
                  pl.BlockSpec((PAIR, w), rev)],
        out_specs=[pl.BlockSpec((PAIR, w), rev), pl.BlockSpec((PAIR, w), rev), pl.BlockSpec((PAIR, w), rev),
                   pl.BlockSpec((PAIR, LANE), rev)],
        out_shape=[SDS((s, w), F32), SDS((s, w), F32), SDS((s, w), F32), SDS((s, LANE), F32)],
        scratch=[pltpu.VMEM((nh, HEAD, HEAD), F32)])(q, k, v, gb, gbt, states, do)


def _mla_norm(proj, qw, kvw, col_q, col_kv):
    s = proj.shape[0]
    lr = qw.shape[1]
    ts = _tile(s, 512, 8)

    def body(cq_ref, ckv_ref, qw_ref, kvw_ref, oq_ref, okv_ref):
        for x_ref, w_ref, o_ref in ((cq_ref, qw_ref, oq_ref), (ckv_ref, kvw_ref, okv_ref)):
            xv = x_ref[...]
            r = lax.rsqrt(jnp.mean(xv * xv, axis=-1, keepdims=True) + EPS)
            o_ref[...] = (xv * r * w_ref[...]).astype(BF16)

    return _pcall(body, name="mla_norm", grid=(s // ts,),
                  in_specs=[_rows(ts, lr, col_q), _rows(ts, lr, col_kv), _full((1, lr)), _full((1, lr))],
                  out_specs=[_rows(ts, lr), _rows(ts, lr)],
                  out_shape=[SDS((s, lr), BF16), SDS((s, lr), BF16)])(proj, proj, qw, kvw)


def _mla_norm_bwd(proj, qw, kvw, dq, dkv, col_q, col_kv):
    s = proj.shape[0]
    lr = qw.shape[1]
    ts = _tile(s, 512, 8)

    def body(cq_ref, ckv_ref, qw_ref, kvw_ref, dq_ref, dkv_ref, oq_ref, okv_ref, dqw_ref, dkvw_ref):
        @pl.when(pl.program_id(0) == 0)
        def _():
            dqw_ref[...] = jnp.zeros_like(dqw_ref)
            dkvw_ref[...] = jnp.zeros_like(dkvw_ref)

        for x_ref, w_ref, d_ref, o_ref, dw_ref in ((cq_ref, qw_ref, dq_ref, oq_ref, dqw_ref),
                                                    (ckv_ref, kvw_ref, dkv_ref, okv_ref, dkvw_ref)):
            xv, dh = x_ref[...], d_ref[...]
            r = lax.rsqrt(jnp.mean(xv * xv, axis=-1, keepdims=True) + EPS)
            xh = xv * r
            dw_ref[...] += jnp.sum(dh * xh, axis=0, keepdims=True)
            dxh = dh * w_ref[...]
            o_ref[...] = (r * (dxh - xh * jnp.mean(dxh * xh, axis=-1, keepdims=True))).astype(BF16)

    return _pcall(body, name="mla_norm_bwd", grid=(s // ts,),
                  in_specs=[_rows(ts, lr, col_q), _rows(ts, lr, col_kv), _full((1, lr)), _full((1, lr)),
                            _rows(ts, lr), _rows(ts, lr)],
                  out_specs=[_rows(ts, lr), _rows(ts, lr), _full((1, lr)), _full((1, lr))],
                  out_shape=[SDS((s, lr), BF16), SDS((s, lr), BF16), SDS((1, lr), F32),
                             SDS((1, lr), F32)])(proj, proj, qw, kvw, dq, dkv)


def _rope_tables(pos, invf, sgn):
    ang = pos * invf
    return jnp.cos(ang), jnp.sin(ang) * sgn


def _swap_halves_lanes(y):
    lane = _lane_iota(y.shape)
    return jnp.where(lane < ROPE // 2, pltpu.roll(y, LANE - ROPE // 2, 1), pltpu.roll(y, ROPE // 2, 1))


def _rope_consts():
    half = ROPE // 2
    inv = ROPE_THETA ** (-jnp.arange(half, dtype=F32) / half)
    invf = jnp.concatenate([inv, inv, jnp.zeros((LANE - ROPE,), F32)])[None, :]
    sgn = jnp.concatenate([-jnp.ones((half,), F32), jnp.ones((half,), F32),
                           jnp.zeros((LANE - ROPE,), F32)])[None, :]
    return invf, sgn


def _mla_rope(qraw, kvraw, proj, pos, nh, misc_col):
    s = qraw.shape[0]
    ts = _tile(s, 256, 8)
    wq = nh * 2 * HEAD
    invf, sgn = _rope_consts()

    def body(q_ref, kv_ref, misc_ref, pos_ref, if_ref, sg_ref, qc_ref, kc_ref, v_ref):
        c, sn = _rope_tables(pos_ref[...], if_ref[...], sg_ref[...])
        lane = _lane_iota(c.shape)
        rot = lambda xb: xb * c + _swap_halves_lanes(xb) * sn
        qs = SM_SCALE * LOG2E
        krot = jnp.where(lane < ROPE, rot(misc_ref[...]), 0.0).astype(BF16)
        for h in range(nh):
            b0 = 2 * HEAD * h
            qc_ref[:, b0:b0 + HEAD] = (q_ref[:, b0:b0 + HEAD] * qs).astype(BF16)
            qc_ref[:, b0 + HEAD:b0 + 2 * HEAD] = (rot(q_ref[:, b0 + HEAD:b0 + 2 * HEAD]) * qs).astype(BF16)
            kc_ref[:, b0:b0 + HEAD] = kv_ref[:, b0:b0 + HEAD].astype(BF16)
            kc_ref[:, b0 + HEAD:b0 + 2 * HEAD] = krot
        v_ref[...] = kv_ref[:, wq:].astype(BF16)

    return _pcall(body, name="mla_rope", grid=(s // ts,),
                  in_specs=[_rows(ts, wq), _rows(ts, wq + nh * HEAD), _rows(ts, LANE, misc_col),
                            _rows(ts, 1), _full((1, LANE)), _full((1, LANE))],
                  out_specs=[_rows(ts, wq), _rows(ts, wq), _rows(ts, nh * HEAD)],
                  out_shape=[SDS((s, wq), BF16), SDS((s, wq), BF16), SDS((s, nh * HEAD), BF16)])(
                      qraw, kvraw, proj, pos, invf, sgn)


def _mla_rope_bwd(dqc, dkc, dv, pos, nh):
    s = dqc.shape[0]
    ts = _tile(s, 256, 8)
    wq = nh * 2 * HEAD
    invf, sgn = _rope_consts()

    def body(dq_ref, dk_ref, dv_ref, pos_ref, if_ref, sg_ref, oq_ref, okv_ref, okr_ref):
        c, sn = _rope_tables(pos_ref[...], if_ref[...], sg_ref[...])
        lane = _lane_iota(c.shape)
        unrot = lambda d: d * c + _swap_halves_lanes(d * sn)
        dkr = jnp.zeros(c.shape, F32)
        for h in range(nh):
            b0 = 2 * HEAD * h
            oq_ref[:, b0:b0 + HEAD] = (dq_ref[:, b0:b0 + HEAD] * SM_SCALE).astype(BF16)
            oq_ref[:, b0 + HEAD:b0 + 2 * HEAD] = (
                unrot(dq_ref[:, b0 + HEAD:b0 + 2 * HEAD]) * SM_SCALE).astype(BF16)
            okv_ref[:, b0:b0 + HEAD] = (dk_ref[:, b0:b0 + HEAD] * LN2).astype(BF16)
            okv_ref[:, b0 + HEAD:b0 + 2 * HEAD] = jnp.zeros((ts, HEAD), BF16)
            dkr = dkr + dk_ref[:, b0 + HEAD:b0 + 2 * HEAD]
        okv_ref[:, wq:] = dv_ref[...].astype(BF16)
        okr_ref[...] = jnp.where(lane < ROPE, unrot(jnp.where(lane < ROPE, dkr * LN2, 0.0)), 0.0)

    return _pcall(body, name="mla_rope_bwd", grid=(s // ts,),
                  in_specs=[_rows(ts, wq), _rows(ts, wq), _rows(ts, nh * HEAD), _rows(ts, 1),
                            _full((1, LANE)), _full((1, LANE))],
                  out_specs=[_rows(ts, wq), _rows(ts, wq + nh * HEAD), _rows(ts, LANE)],
                  out_shape=[SDS((s, wq), BF16), SDS((s, wq + nh * HEAD), BF16), SDS((s, LANE), F32)])(
                      dqc, dkc, dv, pos, invf, sgn)


def _causal_mask(blk):
    i = lax.broadcasted_iota(jnp.int32, (blk, blk), 0)
    j = lax.broadcasted_iota(jnp.int32, (blk, blk), 1)
    return j <= i


def _mla_fwd(qc, kc, v, nh):
    s = qc.shape[0]
    blk = _tile(s, MLA_BLOCK)
    nb = s // blk
    rep = blk // LANE

    def body(q_ref, k_ref, v_ref, o_ref, lse_ref, m_sc, l_sc, acc):
        i = pl.program_id(1)
        q = q_ref[...]
        m_sc[...] = jnp.full_like(m_sc, -1e30)
        l_sc[...] = jnp.zeros_like(l_sc)
        acc[...] = jnp.zeros_like(acc)

        def step(j, masked):
            off = pl.multiple_of(j * blk, blk)
            sc = _dot_nt(q, k_ref[pl.ds(off, blk), :])
            if masked:
                sc = jnp.where(_causal_mask(blk), sc, -1e30)
            m_prev = m_sc[...]
            m_new = jnp.maximum(m_prev, jnp.max(sc, axis=-1, keepdims=True))
            p = jnp.exp2(sc - jnp.tile(m_new, (1, rep)))
            alpha = jnp.exp2(m_prev - m_new)
            l_sc[...] = alpha * l_sc[...] + jnp.sum(p, axis=-1, keepdims=True)
            acc[...] = alpha * acc[...] + _dot(p, v_ref[pl.ds(off, blk), :])
            m_sc[...] = m_new

        def loop_body(j, carry):
            step(j, False)
            return carry

        lax.fori_loop(0, i, loop_body, 0)
        step(i, True)
        o_ref[...] = acc[...] / l_sc[...]
        lse_ref[...] = m_sc[...] + jnp.log(l_sc[...]) * LOG2E

    return pl.pallas_call(
        body, name="mla_fwd", grid=(nh, nb),
        in_specs=[pl.BlockSpec((blk, 2 * HEAD), lambda h, i: (i, h)),
                  pl.BlockSpec((s, 2 * HEAD), lambda h, i: (0, h)),
                  pl.BlockSpec((s, HEAD), lambda h, i: (0, h))],
        out_specs=[pl.BlockSpec((blk, HEAD), lambda h, i: (i, h)),
                   pl.BlockSpec((None, blk, LANE), lambda h, i: (h, i, 0))],
        out_shape=[SDS((s, nh * HEAD), F32), SDS((nh, s, LANE), F32)],
        scratch_shapes=[pltpu.VMEM((blk, LANE), F32), pltpu.VMEM((blk, LANE), F32), pltpu.VMEM((blk, HEAD), F32)],
        compiler_params=pltpu.CompilerParams(dimension_semantics=("arbitrary",) * 2,
                                             vmem_limit_bytes=VMEM_LIMIT))(qc, kc, v)


def _mla_bwd(qc, kc, v, do, lse, delta, nh):
    s = qc.shape[0]
    blk = _tile(s, MLA_BLOCK)
    nb = s // blk
    rep = blk // LANE
    once = pl.Buffered(1)

    def body(q_ref, do_ref, lse_ref, dl_ref, k_ref, v_ref, dq_ref, dk_ref, dv_ref, dk_acc, dv_acc):
        j = pl.program_id(1)

        @pl.when(j == 0)
        def _():
            dq_ref[...] = jnp.zeros_like(dq_ref)

        dk_acc[...] = jnp.zeros_like(dk_acc)
        dv_acc[...] = jnp.zeros_like(dv_acc)
        kj, vj = k_ref[...], v_ref[...]

        def step(i, masked):
            rows = pl.ds(pl.multiple_of(i * blk, blk), blk)
            qi, doi = q_ref[rows, :], do_ref[rows, :]
            sc = _dot_nt(qi, kj)
            if masked:
                sc = jnp.where(_causal_mask(blk), sc, -1e30)
            p = jnp.exp2(sc - jnp.tile(lse_ref[rows, :], (1, rep)))
            dp = _dot_nt(doi, vj)
            ds = p * (dp - jnp.tile(dl_ref[rows, :], (1, rep)))
            dv_acc[...] += _dot_tn(p, doi)
            dk_acc[...] += _dot_tn(ds, qi)
            dq_ref[rows, :] += _dot(ds, kj)

        def loop_body(i, carry):
            step(i, False)
            return carry

        step(j, True)
        lax.fori_loop(j + 1, nb, loop_body, 0)
        dk_ref[...] = dk_acc[...]
        dv_ref[...] = dv_acc[...]

    return pl.pallas_call(
        body, name="mla_bwd", grid=(nh, nb),
        in_specs=[pl.BlockSpec((s, 2 * HEAD), lambda h, j: (0, h), pipeline_mode=once),
                  pl.BlockSpec((s, HEAD), lambda h, j: (0, h), pipeline_mode=once),
                  pl.BlockSpec((None, s, LANE), lambda h, j: (h, 0, 0), pipeline_mode=once),
                  pl.BlockSpec((None, s, LANE), lambda h, j: (h, 0, 0), pipeline_mode=once),
                  pl.BlockSpec((blk, 2 * HEAD), lambda h, j: (j, h)),
                  pl.BlockSpec((blk, HEAD), lambda h, j: (j, h))],
        out_specs=[pl.BlockSpec((s, 2 * HEAD), lambda h, j: (0, h), pipeline_mode=once),
                   pl.BlockSpec((blk, 2 * HEAD), lambda h, j: (j, h)),
                   pl.BlockSpec((blk, HEAD), lambda h, j: (j, h))],
        out_shape=[SDS((s, nh * 2 * HEAD), F32), SDS((s, nh * 2 * HEAD), F32), SDS((s, nh * HEAD), F32)],
        scratch_shapes=[pltpu.VMEM((blk, 2 * HEAD), F32), pltpu.VMEM((blk, HEAD), F32)],
        compiler_params=pltpu.CompilerParams(dimension_semantics=("arbitrary",) * 2,
                                             vmem_limit_bytes=VMEM_LIMIT))(qc, do, lse, delta, kc, v)


def _mix_fwd(og, proj, om, gw, mw, nh, z_col):
    s = og.shape[0]
    w = nh * HEAD
    ts = _tile(s, 256, 8)

    def body(og_ref, z_ref, om_ref, gw_ref, mw_ref, o_ref):
        for h in range(nh):
            hs = slice(HEAD * h, HEAD * (h + 1))
            a = og_ref[:, hs]
            r = lax.rsqrt(jnp.mean(a * a, axis=-1, keepdims=True) + EPS)
            o_ref[:, hs] = (a * r * gw_ref[...] * _silu(z_ref[:, hs])).astype(BF16)
            b = om_ref[:, hs]
            r = lax.rsqrt(jnp.mean(b * b, axis=-1, keepdims=True) + EPS)
            o_ref[:, w + HEAD * h:w + HEAD * (h + 1)] = (b * r * mw_ref[...]).astype(BF16)

    return _pcall(body, name="mix_fwd", grid=(s // ts,),
                  in_specs=[_rows(ts, w), _rows(ts, w, z_col), _rows(ts, w), _full((1, HEAD)),
                            _full((1, HEAD))],
                  out_specs=_rows(ts, 2 * w), out_shape=SDS((s, 2 * w), BF16))(og, proj, om, gw, mw)


def _mix_bwd(dmix, og, proj, om, gw, mw, nh, z_col):
    s = og.shape[0]
    w = nh * HEAD
    ts = _tile(s, 256, 8)

    def body(d_ref, og_ref, z_ref, om_ref, gw_ref, mw_ref, dog_ref, dz_ref, dom_ref, dgw_ref, dmw_ref,
             dl_ref):
        @pl.when(pl.program_id(0) == 0)
        def _():
            dgw_ref[...] = jnp.zeros_like(dgw_ref)
            dmw_ref[...] = jnp.zeros_like(dmw_ref)

        dgw = jnp.zeros((1, HEAD), F32)
        dmw = jnp.zeros((1, HEAD), F32)
        for h in range(nh):
            hs = slice(HEAD * h, HEAD * (h + 1))
            a, z, dy = og_ref[:, hs], z_ref[:, hs], d_ref[:, hs]
            r = lax.rsqrt(jnp.mean(a * a, axis=-1, keepdims=True) + EPS)
            ah = a * r
            sz = _silu(z)
            dz_ref[:, hs] = (dy * (ah * gw_ref[...]) * _dsilu(z)).astype(BF16)
            dn = dy * sz
            dgw = dgw + jnp.sum(dn * ah, axis=0, keepdims=True)
            dah = dn * gw_ref[...]
            dog_ref[:, hs] = r * (dah - ah * jnp.mean(dah * ah, axis=-1, keepdims=True))
            b, dyb = om_ref[:, hs], d_ref[:, w + HEAD * h:w + HEAD * (h + 1)]
            r = lax.rsqrt(jnp.mean(b * b, axis=-1, keepdims=True) + EPS)
            bh = b * r
            dmw = dmw + jnp.sum(dyb * bh, axis=0, keepdims=True)
            dbh = dyb * mw_ref[...]
            dom = r * (dbh - bh * jnp.mean(dbh * bh, axis=-1, keepdims=True))
            dom_ref[:, hs] = dom.astype(BF16)
            dl_ref[h] = jnp.broadcast_to(jnp.sum(dom * b, axis=-1, keepdims=True), (ts, LANE))
        dgw_ref[...] += dgw
        dmw_ref[...] += dmw

    return _pcall(body, name="mix_bwd", grid=(s // ts,),
                  in_specs=[_rows(ts, 2 * w), _rows(ts, w), _rows(ts, w, z_col), _rows(ts, w),
                            _full((1, HEAD)), _full((1, HEAD))],
                  out_specs=[_rows(ts, w), _rows(ts, w), _rows(ts, w), _full((1, HEAD)), _full((1, HEAD)),
                             pl.BlockSpec((nh, ts, LANE), lambda i: (0, i, 0))],
                  out_shape=[SDS((s, w), F32), SDS((s, w), BF16), SDS((s, w), BF16), SDS((1, HEAD), F32),
                             SDS((1, HEAD), F32), SDS((nh, s, LANE), F32)])(dmix, og, proj, om, gw, mw)


def _swiglu_fwd(h2, wg, wu):
    m, kdim = h2.shape
    n = wg.shape[1]
    tm, tn, tk = _tile(m, 1024), _tile(n, 512), _tile(kdim, 2048)
    nk = kdim // tk

    def body(a_ref, g_ref, u_ref, act_ref, go_ref, uo_ref, gacc, uacc):
        k = pl.program_id(2)

        @pl.when(k == 0)
        def _():
            gacc[...] = jnp.zeros_like(gacc)
            uacc[...] = jnp.zeros_like(uacc)

        a = a_ref[...]
        gacc[...] += _dot(a, g_ref[...])
        uacc[...] += _dot(a, u_ref[...])

        @pl.when(k == nk - 1)
        def _():
            g, u = gacc[...], uacc[...]
            act_ref[...] = (_silu(g) * u).astype(BF16)
            go_ref[...] = g.astype(BF16)
            uo_ref[...] = u.astype(BF16)

    a_spec = pl.BlockSpec((tm, tk), lambda i, j, k: (i, k))
    b_spec = pl.BlockSpec((tk, tn), lambda i, j, k: (k, j))
    o_spec = pl.BlockSpec((tm, tn), lambda i, j, k: (i, j))
    return _pcall(body, name="swiglu_fwd", grid=(m // tm, n // tn, nk),
                  in_specs=[a_spec, b_spec, b_spec], out_specs=[o_spec] * 3,
                  out_shape=[SDS((m, n), BF16)] * 3,
                  scratch=[pltpu.VMEM((tm, tn), F32), pltpu.VMEM((tm, tn), F32)])(h2, wg, wu)


def _swiglu_bwd(dx3, wd, g, u):
    m, kdim = dx3.shape
    n = wd.shape[0]
    tm, tn, tk = _tile(m, 1024), _tile(n, 512), _tile(kdim, 2048)
    nk = kdim // tk

    def body(a_ref, b_ref, g_ref, u_ref, dg_ref, du_ref, acc):
        k = pl.program_id(2)

        @pl.when(k == 0)
        def _():
            acc[...] = jnp.zeros_like(acc)

        acc[...] += _dot_nt(a_ref[...], b_ref[...])

        @pl.when(k == nk - 1)
        def _():
            da = acc[...]
            gv, uv = g_ref[...].astype(F32), u_ref[...].astype(F32)
            dg_ref[...] = (da * uv * _dsilu(gv)).astype(BF16)
            du_ref[...] = (da * _silu(gv)).astype(BF16)

    a_spec = pl.BlockSpec((tm, tk), lambda i, j, k: (i, k))
    b_spec = pl.BlockSpec((tn, tk), lambda i, j, k: (j, k))
    o_spec = pl.BlockSpec((tm, tn), lambda i, j, k: (i, j))
    return _pcall(body, name="swiglu_bwd", grid=(m // tm, n // tn, nk),
                  in_specs=[a_spec, b_spec, o_spec, o_spec], out_specs=[o_spec] * 2,
                  out_shape=[SDS((m, n), BF16)] * 2,
                  scratch=[pltpu.VMEM((tm, tn), F32)])(dx3, wd, g, u)


def _flat_tile(r):
    for t in (5008, 4096, 2048, 1024, 512, 256, 128, 64, 32, 16):
        if r % t == 0:
            return t
    return r


def _sum_pair(a, b):
    t, r, _ = a.shape
    tr = _flat_tile(r)

    def body(a_ref, b_ref, o32_ref, o16_ref):
        sm = a_ref[...].astype(F32) + b_ref[...].astype(F32)
        o32_ref[...] = sm
        o16_ref[...] = sm.astype(BF16)

    spec = pl.BlockSpec((1, tr, LANE), lambda i, j: (i, j, 0))
    return _pcall(body, name="sum_pair", grid=(t, r // tr), in_specs=[spec, spec], out_specs=[spec, spec],
                  out_shape=[SDS(a.shape, F32), SDS(a.shape, BF16)])(a, b)


def _sum_chips(own, recv):
    r = own.shape[0]
    tr = _flat_tile(r)

    def body(o_ref, r_ref, out_ref):
        acc = o_ref[...]
        for j in range(3):
            acc = acc + r_ref[j].astype(F32)
        out_ref[...] = acc

    return _pcall(body, name="sum_chips", grid=(r // tr,),
                  in_specs=[_rows(tr, LANE), pl.BlockSpec((3, tr, LANE), lambda i: (0, i, 0))],
                  out_specs=_rows(tr, LANE), out_shape=SDS(own.shape, F32))(own, recv)


def _adamw(w, g, m, v, name):
    r, c = w.shape
    tr = _tile(r, 256, 8)

    def body(w_ref, g_ref, m_ref, v_ref, d_ref, mo_ref, vo_ref):
        gv = g_ref[...]
        mn = ADAM_B1 * m_ref[...] + (1.0 - ADAM_B1) * gv
        vn = ADAM_B2 * v_ref[...] + (1.0 - ADAM_B2) * (gv * gv)
        m_hat = mn / (1.0 - ADAM_B1 ** ADAM_STEP)
        v_hat = vn / (1.0 - ADAM_B2 ** ADAM_STEP)
        d_ref[...] = -ADAM_LR * (m_hat / (jnp.sqrt(v_hat) + ADAM_EPS) + ADAM_WD * w_ref[...])
        mo_ref[...] = mn
        vo_ref[...] = vn

    spec = _rows(tr, c)
    return _pcall(body, name=name, grid=(r // tr,), in_specs=[spec] * 4, out_specs=[spec] * 3,
                  out_shape=[SDS(w.shape, F32)] * 3)(w, g, m, v)


def _place():
    x, y, c = lax.axis_index("x"), lax.axis_index("y"), lax.axis_index("c")
    chips = [(1 - x, y), (x, 1 - y), (1 - x, 1 - y)]
    return x, y, c, chips


_ANY = pl.BlockSpec(memory_space=pl.ANY)


def _comm_call(body, *, name, out_shape, n_in, scratch):
    return pl.pallas_call(body, name=name, out_shape=out_shape, in_specs=[_ANY] * n_in,
                          out_specs=_ANY, scratch_shapes=scratch)


def _all_gather_weights(wsh):
    _, r, _ = wsh.shape

    def body(w_ref, o_ref, send_sems, recv_sems, local_sem):
        x, y, c, chips = _place()
        s = 2 * x + y
        sib = (x, y, 1 - c)

        def cp(k, src, dst, to):
            return pltpu.make_async_remote_copy(src_ref=src, dst_ref=dst, send_sem=send_sems.at[k],
                                                recv_sem=recv_sems.at[k], device_id=to, device_id_type=MESH)

        mine = pltpu.make_async_copy(w_ref, o_ref.at[s], local_sem)
        mine.start()
        first = [cp(j, w_ref.at[c], o_ref.at[s, c], (tx, ty, c)) for j, (tx, ty) in enumerate(chips)]
        for f in first:
            f.start()
        passed = []
        for j, (tx, ty) in enumerate(chips):
            t = 2 * tx + ty
            cp(j, w_ref.at[c], o_ref.at[t, c], (x, y, c)).wait_recv()
            p = cp(3 + j, o_ref.at[t, c], o_ref.at[t, c], sib)
            p.start()
            passed.append(p)
        for j, (tx, ty) in enumerate(chips):
            t = 2 * tx + ty
            cp(3 + j, o_ref.at[t, 1 - c], o_ref.at[t, 1 - c], (x, y, c)).wait_recv()
        for f in first + passed:
            f.wait_send()
        mine.wait()

    return _comm_call(body, name="all_gather_weights", out_shape=SDS((4, 2, r, LANE), wsh.dtype), n_in=1,
                      scratch=[pltpu.SemaphoreType.DMA((6,)), pltpu.SemaphoreType.DMA((6,)),
                               pltpu.SemaphoreType.DMA])(wsh)


def _swap_with_sibling(theirs):
    def body(g_ref, o_ref, send_sems, recv_sems):
        x, y, c, _ = _place()
        cps = [pltpu.make_async_remote_copy(src_ref=g_ref.at[t], dst_ref=o_ref.at[t], send_sem=send_sems.at[t],
                                            recv_sem=recv_sems.at[t], device_id=(x, y, 1 - c),
                                            device_id_type=MESH) for t in range(4)]
        for cpy in cps:
            cpy.start()
        for cpy in cps:
            cpy.wait()

    return _comm_call(body, name="swap_with_sibling", out_shape=SDS(theirs.shape, theirs.dtype), n_in=1,
                      scratch=[pltpu.SemaphoreType.DMA((4,)), pltpu.SemaphoreType.DMA((4,))])(theirs)


def _exchange_chips(c16):
    _, r, _ = c16.shape

    def body(g_ref, o_ref, send_sems, recv_sems):
        x, y, c, chips = _place()
        cps = [pltpu.make_async_remote_copy(src_ref=g_ref.at[2 * tx + ty], dst_ref=o_ref.at[j],
                                            send_sem=send_sems.at[j], recv_sem=recv_sems.at[j],
                                            device_id=(tx, ty, c), device_id_type=MESH)
               for j, (tx, ty) in enumerate(chips)]
        for cpy in cps:
            cpy.start()
        for cpy in cps:
            cpy.wait()

    return _comm_call(body, name="exchange_chips", out_shape=SDS((3, r, LANE), c16.dtype), n_in=1,
                      scratch=[pltpu.SemaphoreType.DMA((3,)), pltpu.SemaphoreType.DMA((3,))])(c16)


def _share_half(tot):
    r = tot.shape[0]

    def body(t_ref, o_ref, send_sem, recv_sem, local_sem):
        x, y, c, _ = _place()
        mine = pltpu.make_async_copy(t_ref, o_ref.at[c], local_sem)
        mine.start()
        cpy = pltpu.make_async_remote_copy(src_ref=t_ref, dst_ref=o_ref.at[c], send_sem=send_sem,
                                           recv_sem=recv_sem, device_id=(x, y, 1 - c), device_id_type=MESH)
        cpy.start()
        cpy.wait()
        mine.wait()

    return _comm_call(body, name="share_half", out_shape=SDS((2, r, LANE), tot.dtype), n_in=1,
                      scratch=[pltpu.SemaphoreType.DMA, pltpu.SemaphoreType.DMA,
                               pltpu.SemaphoreType.DMA])(tot)


def _small_allreduce(pk, name):
    r = pk.shape[0]
    rels = [(dx, dy, dc) for dx in (0, 1) for dy in (0, 1) for dc in (0, 1) if dx or dy or dc]

    def body(p_ref, o_ref, buf, send_sems, recv_sems):
        x, y, c, _ = _place()
        me = 4 * x + 2 * y + c
        buf[me] = p_ref[...]
        cps = []
        for k, (dx, dy, dc) in enumerate(rels):
            to = (1 - x if dx else x, 1 - y if dy else y, 1 - c if dc else c)
            cps.append(pltpu.make_async_remote_copy(src_ref=p_ref, dst_ref=buf.at[me], send_sem=send_sems.at[k],
                                                    recv_sem=recv_sems.at[k], device_id=to,
                                                    device_id_type=MESH))
        for cpy in cps:
            cpy.start()
        for cpy in cps:
            cpy.wait()
        acc = buf[0]
        for d in range(1, 8):
            acc = acc + buf[d]
        o_ref[...] = acc

    vm = pl.BlockSpec(memory_space=pltpu.VMEM)
    return pl.pallas_call(body, name=name, out_shape=SDS(pk.shape, F32), in_specs=[vm], out_specs=vm,
                          scratch_shapes=[pltpu.VMEM((8, r, LANE), F32), pltpu.SemaphoreType.DMA((7,)),
                                          pltpu.SemaphoreType.DMA((7,))])(pk)


BIG = ("w_in", "w_uq", "w_ukv", "w_out", "w_gate", "w_up", "w_down")
ROW_SHARDED = ("w_out", "w_down")
PACK_ALIGN = 2 * 16 * LANE


def _pack_flat(parts, dtype):
    flat = jnp.concatenate([p.astype(dtype).reshape(-1) for p in parts])
    n = flat.shape[0]
    pad = (-n) % PACK_ALIGN
    if pad:
        flat = jnp.concatenate([flat, jnp.zeros((pad,), dtype)])
    return flat.reshape(2, -1, LANE)


def _unpack_flat(flat, shapes):
    out, off = [], 0
    for shp in shapes:
        n = shp[0] * shp[1]
        out.append(flat[off:off + n].reshape(shp))
        off += n
    return out


def _permute_w_in(w, nh):
    d = w.shape[0]
    g = 4 * nh * HEAD
    lr = (w.shape[1] - g - 2 * nh - ROPE) // 2
    o = g + 2 * nh
    pad = jnp.zeros((d, LANE - ROPE - 8 - nh), w.dtype)
    pad8 = jnp.zeros((d, 8 - nh), w.dtype)
    return jnp.concatenate([w[:, :g], w[:, o:o + 2 * lr], w[:, o + 2 * lr:], w[:, g:g + nh], pad8,
                            w[:, g + nh:g + 2 * nh], pad, jnp.zeros((d, LANE), w.dtype)], axis=1)


def _unpermute_w_in(wp, nh, lr):
    g = 4 * nh * HEAD
    mc = g + 2 * lr
    return jnp.concatenate([wp[:, :g], wp[:, mc + B_LANE:mc + B_LANE + nh], wp[:, mc + A_LANE:mc + A_LANE + nh],
                            wp[:, g:g + 2 * lr], wp[:, mc:mc + ROPE]], axis=1)


def _permute_w_uq(w, nh):
    lr = w.shape[0]
    w3 = w.reshape(lr, nh, HEAD + ROPE)
    return jnp.concatenate([w3, jnp.zeros((lr, nh, HEAD - ROPE), w.dtype)], axis=2).reshape(lr, nh * 2 * HEAD)


def _unpermute_w_uq(wp, nh):
    lr = wp.shape[0]
    return wp.reshape(lr, nh, 2 * HEAD)[:, :, :HEAD + ROPE].reshape(lr, nh * (HEAD + ROPE))


def _permute_w_ukv(w, nh):
    lr = w.shape[0]
    w3 = w.reshape(lr, nh, 2 * HEAD)
    kp = jnp.concatenate([w3[:, :, :HEAD], jnp.zeros((lr, nh, HEAD), w.dtype)], axis=2)
    return jnp.concatenate([kp.reshape(lr, nh * 2 * HEAD), w3[:, :, HEAD:].reshape(lr, nh * HEAD)], axis=1)


def _unpermute_w_ukv(wp, nh):
    lr = wp.shape[0]
    kp = wp[:, :nh * 2 * HEAD].reshape(lr, nh, 2 * HEAD)[:, :, :HEAD]
    vp = wp[:, nh * 2 * HEAD:].reshape(lr, nh, HEAD)
    return jnp.concatenate([kp, vp], axis=2).reshape(lr, nh * 2 * HEAD)


def _local_step(x, pos, tgt, wt, small):
    nh = small["a_log"].shape[1]
    lr = small["q_norm_w"].shape[1]
    w = nh * HEAD
    z_col, col_q, col_kv = 3, 4 * w // lr, 4 * w // lr + 1
    misc_c = 4 * w + 2 * lr
    misc_col = misc_c // LANE
    assert (4 * w) % lr == 0 and small["kv_norm_w"].shape[1] == lr

    win_p = _permute_w_in(wt["w_in"], nh)
    wuq_p = _permute_w_uq(wt["w_uq"], nh)
    wukv_p = _permute_w_ukv(wt["w_ukv"], nh)
    zl = jnp.zeros((1, LANE), F32)
    alog_l = zl.at[:, A_LANE:A_LANE + nh].set(small["a_log"])
    dtb_l = zl.at[:, A_LANE:A_LANE + nh].set(small["dt_bias"])
    conv_w = small["conv_w"]

    h1 = _norm_fwd(x, small["attn_norm_w"], "norm1")
    proj = _mm([(h1, win_p)], name="proj_in")
    gq, gk, gv, gb, gbt = _gdn_prep(proj, conv_w, alog_l, dtb_l, nh, misc_col)
    o_gdn, states = _gdn_fwd(gq, gk, gv, gb, gbt, nh)
    cqn, ckvn = _mla_norm(proj, small["q_norm_w"], small["kv_norm_w"], col_q, col_kv)
    qraw = _mm([(cqn, wuq_p)], name="proj_uq")
    kvraw = _mm([(ckvn, wukv_p)], name="proj_ukv")
    qc, kc, vv = _mla_rope(qraw, kvraw, proj, pos, nh, misc_col)
    o_mla, lse = _mla_fwd(qc, kc, vv, nh)
    mixed = _mix_fwd(o_gdn, proj, o_mla, small["gdn_norm_w"], small["mla_out_norm_w"], nh, z_col)
    x2 = _mm([(mixed, wt["w_out"])], name="proj_out", res=x)
    h2 = _norm_fwd(x2, small["ffn_norm_w"], "norm2")
    act, gpre, upre = _swiglu_fwd(h2, wt["w_gate"], wt["w_up"])
    x3 = _mm([(act, wt["w_down"])], name="proj_down", res=x2)
    dx3, d_final, loss, dx3h = _final_loss(x3, tgt, small["final_norm_w"])

    gm, gs = {}, {"final_norm_w": d_final}
    dgate, dup = _swiglu_bwd(dx3h, wt["w_down"], gpre, upre)
    gm["w_down"] = _mm([(act, dx3h)], name="dw_down", ta=True)
    dh2 = _mm([(dgate, wt["w_gate"]), (dup, wt["w_up"])], name="dh2", tb=True)
    gm["w_gate"] = _mm([(h2, dgate)], name="dw_gate", ta=True)
    gm["w_up"] = _mm([(h2, dup)], name="dw_up", ta=True)
    dx2, gs["ffn_norm_w"], dx2h = _norm_bwd(dh2, x2, small["ffn_norm_w"], dx3, "norm2_bwd", True)
    dmix = _mm([(dx2h, wt["w_out"])], name="dmix", tb=True)
    gm["w_out"] = _mm([(mixed, dx2h)], name="dw_out", ta=True)
    d_ogdn, dz, d_omla, gs["gdn_norm_w"], gs["mla_out_norm_w"], delta = _mix_bwd(
        dmix, o_gdn, proj, o_mla, small["gdn_norm_w"], small["mla_out_norm_w"], nh, z_col)
    dqc, dkc, dvv = _mla_bwd(qc, kc, vv, d_omla, lse, delta, nh)
    dqraw, dkvraw, dkr = _mla_rope_bwd(dqc, dkc, dvv, pos, nh)
    dcqn = _mm([(dqraw, wuq_p)], name="dcqn", tb=True)
    dckvn = _mm([(dkvraw, wukv_p)], name="dckvn", tb=True)
    gm["w_uq"] = _unpermute_w_uq(_mm([(cqn, dqraw)], name="dw_uq", ta=True), nh)
    gm["w_ukv"] = _unpermute_w_ukv(_mm([(ckvn, dkvraw)], name="dw_ukv", ta=True), nh)
    dcq, dckv, gs["q_norm_w"], gs["kv_norm_w"] = _mla_norm_bwd(
        proj, small["q_norm_w"], small["kv_norm_w"], dcqn, dckvn, col_q, col_kv)
    dgq, dgk, dgv, dgb = _gdn_bwd(gq, gk, gv, gb, gbt, states, d_ogdn, nh)
    dconv, dmisc, gs["conv_w"], dal, ddb = _gdn_prep_bwd(
        proj, conv_w, alog_l, dtb_l, dgq, dgk, dgv, dgb, dkr, nh, misc_col)
    gs["a_log"] = dal[:, A_LANE:A_LANE + nh]
    gs["dt_bias"] = ddb[:, A_LANE:A_LANE + nh]
    dqkv = _conv_bwd_input(dconv, conv_w)
    dproj = jnp.concatenate([dqkv, dz, dcq, dckv, dmisc, jnp.zeros((x.shape[0], LANE), BF16)], axis=1)
    dh1 = _mm([(dproj, win_p)], name="dh1", tb=True)
    gm["w_in"] = _unpermute_w_in(_mm([(h1, dproj)], name="dw_in", ta=True), nh, lr)
    grad_x, gs["attn_norm_w"] = _norm_bwd(dh1, x, small["attn_norm_w"], dx2, "norm1_bwd", False)
    return loss, grad_x, gm, gs


SMALL = ("attn_norm_w", "ffn_norm_w", "final_norm_w", "q_norm_w", "kv_norm_w", "gdn_norm_w",
         "mla_out_norm_w", "a_log", "dt_bias")
WEIGHTS = ("attn_norm_w", "w_in", "conv_w", "a_log", "dt_bias", "gdn_norm_w", "q_norm_w", "w_uq",
           "kv_norm_w", "w_ukv", "mla_out_norm_w", "w_out", "ffn_norm_w", "w_gate", "w_up", "w_down",
           "final_norm_w")


def _pack_small(vecs):
    flat = jnp.concatenate([v.astype(F32).reshape(-1) for v in vecs])
    pad = (-flat.shape[0]) % (8 * LANE)
    return jnp.concatenate([flat, jnp.zeros((pad,), F32)]).reshape(-1, LANE)


def kernel(x, positions, attn_norm_w, w_in, conv_w, a_log, dt_bias, gdn_norm_w, q_norm_w, w_uq, kv_norm_w, w_ukv, mla_out_norm_w, w_out, ffn_norm_w, w_gate, w_up, w_down, final_norm_w, loss_target, m_attn_norm_w, m_w_in, m_conv_w, m_a_log, m_dt_bias, m_gdn_norm_w, m_q_norm_w, m_w_uq, m_kv_norm_w, m_w_ukv, m_mla_out_norm_w, m_w_out, m_ffn_norm_w, m_w_gate, m_w_up, m_w_down, m_final_norm_w, v_attn_norm_w, v_w_in, v_conv_w, v_a_log, v_dt_bias, v_gdn_norm_w, v_q_norm_w, v_w_uq, v_kv_norm_w, v_w_ukv, v_mla_out_norm_w, v_w_out, v_ffn_norm_w, v_w_gate, v_w_up, v_w_down, v_final_norm_w):
    args = dict(locals())
    xi, yi, ci = lax.axis_index("x"), lax.axis_index("y"), lax.axis_index("c")
    chip = 2 * xi + yi

    def two_d(a):
        return a.reshape(a.shape[-2:]) if a.ndim >= 2 else a.reshape(1, -1)

    wloc = {n: two_d(args[n]) for n in WEIGHTS}
    mloc = {n: two_d(args["m_" + n]) for n in WEIGHTS}
    vloc = {n: two_d(args["v_" + n]) for n in WEIGHTS}

    shard_shapes = [wloc[n].shape for n in BIG]
    gath = _all_gather_weights(_pack_flat([wloc[n] for n in BIG], BF16))
    nflat = sum(a * b for a, b in shard_shapes)
    gflat = gath.reshape(4, -1)[:, :nflat]
    per_chip = [_unpack_flat(gflat[t], shard_shapes) for t in range(4)]
    wt = {}
    for k, n in enumerate(BIG):
        wt[n] = jnp.concatenate([per_chip[t][k] for t in range(4)], axis=0 if n in ROW_SHARDED else 1)
    cw = wloc["conv_w"]
    cshard = cw.shape[1]
    cfull = jnp.zeros((CONV, 4 * cshard), F32)
    cfull = lax.dynamic_update_slice(cfull, jnp.where(ci == 0, cw, 0.0), (0, chip * cshard))
    conv_rows = _pack_small([cfull]).shape[0]
    conv_full = _small_allreduce(_pack_small([cfull]), "gather_conv_w").reshape(-1)[:CONV * 4 * cshard]
    conv_full = conv_full.reshape(CONV, 4 * cshard)
    del conv_rows

    small = {n: wloc[n] for n in SMALL}
    small["conv_w"] = conv_full

    pos = positions.reshape(-1, 1).astype(F32)
    loss, grad_x, gm, gs = _local_step(two_d(x), pos, two_d(loss_target), wt, small)

    def dest_pieces(t):
        out = []
        for n, shp in zip(BIG, shard_shapes):
            g = gm[n]
            if n in ROW_SHARDED:
                out.append(g[t * shp[0]:(t + 1) * shp[0], :])
            else:
                out.append(g[:, t * shp[1]:(t + 1) * shp[1]])
        return out

    gsend = jnp.stack([_pack_flat(dest_pieces(t), BF16) for t in range(4)])
    g_mine = lax.dynamic_index_in_dim(gsend, ci, 1, keepdims=False)
    g_theirs = lax.dynamic_index_in_dim(gsend, 1 - ci, 1, keepdims=False)
    from_sib = _swap_with_sibling(g_theirs)
    c32, c16 = _sum_pair(g_mine, from_sib)
    recv = _exchange_chips(c16)
    own = lax.dynamic_index_in_dim(c32, chip, 0, keepdims=False)
    total = _sum_chips(own, recv)
    gfull = _share_half(total).reshape(-1)[:nflat]
    gbig = dict(zip(BIG, _unpack_flat(gfull, shard_shapes)))

    small_names = SMALL + ("conv_w",)
    pk = _pack_small([gs[n] for n in small_names] + [loss])
    red = _small_allreduce(pk, "reduce_small").reshape(-1)
    gsm, off = {}, 0
    for n in small_names:
        shp = gs[n].shape
        gsm[n] = red[off:off + shp[0] * shp[1]].reshape(shp)
        off += shp[0] * shp[1]
    loss_out = red[off]
    gsm["conv_w"] = lax.dynamic_slice(gsm["conv_w"], (0, chip * cshard), (CONV, cshard))

    grads, deltas, new_m, new_v = {}, {}, {}, {}
    for n in BIG + ("conv_w",):
        grads[n] = gbig[n] if n in gbig else gsm[n]
        deltas[n], new_m[n], new_v[n] = _adamw(wloc[n], grads[n], mloc[n], vloc[n], "adamw_" + n)
    sm_shapes = [wloc[n].shape for n in SMALL]
    pd, pm, pv = _adamw(_pack_small([wloc[n] for n in SMALL]), _pack_small([gsm[n] for n in SMALL]),
                        _pack_small([mloc[n] for n in SMALL]), _pack_small([vloc[n] for n in SMALL]),
                        "adamw_small")
    for dst, packed in ((deltas, pd), (new_m, pm), (new_v, pv)):
        flat, off = packed.reshape(-1), 0
        for n, shp in zip(SMALL, sm_shapes):
            dst[n] = flat[off:off + shp[0] * shp[1]].reshape(shp)
            off += shp[0] * shp[1]
    for n in SMALL:
        grads[n] = gsm[n]

    def like(n, a):
        return a.reshape(args[n].shape)

    outs = [loss_out.reshape(()), grad_x.reshape(x.shape)]
    for group in (grads, deltas, new_m, new_v):
        outs += [like(n, group[n]) for n in WEIGHTS]
    return tuple(outs)
```

```python
import functools

import jax
import jax.numpy as jnp
from jax import lax
from jax.experimental import pallas as pl
from jax.experimental.pallas import tpu as pltpu

F32, BF16 = jnp.float32, jnp.bfloat16
SDS = jax.ShapeDtypeStruct
MESH = pl.DeviceIdType.MESH

HEAD = 128
ROPE = 64
CHUNK = 64
PAIR = 2 * CHUNK
CONV = 4
EPS = 1e-6
ROPE_THETA = 10000.0
LANE = 128
B_LANE = 64
A_LANE = 72
VMEM_LIMIT = 48 * 1024 * 1024
MLA_BLOCK = 512
LOG2E = 1.4426950408889634
LN2 = 0.6931471805599453
SM_SCALE = (HEAD + ROPE) ** -0.5

ADAM_LR = 0.001
ADAM_B1 = 0.9
ADAM_B2 = 0.999
ADAM_EPS = 1e-08
ADAM_WD = 0.01
ADAM_STEP = 10


def _tile(n, pref, mult=LANE):
    if n <= pref:
        return n
    t = (pref // mult) * mult
    while t >= mult:
        if n % t == 0:
            return t
        t -= mult
    return n


def _pcall(body, *, name, grid, in_specs, out_specs, out_shape, scratch=()):
    return pl.pallas_call(
        body, name=name, grid=grid, in_specs=in_specs, out_specs=out_specs,
        out_shape=out_shape, scratch_shapes=list(scratch),
        compiler_params=pltpu.CompilerParams(
            dimension_semantics=("arbitrary",) * len(grid), vmem_limit_bytes=VMEM_LIMIT))


def _rows(ts, width, col=0):
    return pl.BlockSpec((ts, width), lambda i: (i, col))


def _full(shape):
    nd = len(shape)
    return pl.BlockSpec(shape, lambda i: (0,) * nd)


def _dot(a, b):
    return jnp.dot(a.astype(BF16), b.astype(BF16), preferred_element_type=F32)


def _dot_nt(a, b):
    return lax.dot_general(a.astype(BF16), b.astype(BF16), (((1,), (1,)), ((), ())),
                           preferred_element_type=F32)


def _dot_tn(a, b):
    return lax.dot_general(a.astype(BF16), b.astype(BF16), (((0,), (0,)), ((), ())),
                           preferred_element_type=F32)


def _sigmoid(x):
    return 1.0 / (1.0 + jnp.exp(-x))


def _silu(x):
    return x * _sigmoid(x)


def _dsilu(x):
    s = _sigmoid(x)
    return s * (1.0 + x * (1.0 - s))


def _lane_iota(shape):
    return lax.broadcasted_iota(jnp.int32, shape, len(shape) - 1)


def _col(block, idx):
    return jnp.sum(jnp.where(_lane_iota(block.shape) == idx, block, 0.0), axis=-1, keepdims=True)


def _mm(pairs, *, name, ta=False, tb=False, out_dtype=F32, res=None, tm=1024, tn=1024, tk=2048,
        b_chips=False, out_chips=False, rider=None):
    a0, b0 = pairs[0]
    if ta:
        kdim, m = a0.shape
    else:
        m, kdim = a0.shape
    if b_chips and tb:
        n, tk = b0.shape[1], b0.shape[2]
        assert kdim == 4 * tk
    elif b_chips:
        n, tn = 4 * b0.shape[2], b0.shape[2]
        assert kdim == b0.shape[1]
    else:
        n = b0.shape[0] if tb else b0.shape[1]
    if out_chips:
        tn = n // 4
    tm = _tile(m, tm)
    tn = tn if (out_chips or (b_chips and not tb)) else _tile(n, tn)
    tk = tk if (b_chips and tb) else _tile(kdim, tk)
    assert m % tm == 0 and n % tn == 0 and kdim % tk == 0
    nk, npair = kdim // tk, len(pairs)
    grid = (m // tm, n // tn, nk)
    dims = (((0 if ta else 1,), (1 if tb else 0,)), ((), ()))
    n_in = 2 * npair + (res is not None)
    r_in, r_out = (len(rider.arrays), len(rider.out_shapes)) if rider else (0, 0)

    def body(*refs):
        o_ref = refs[n_in + r_in]
        acc = refs[n_in + r_in + 1 + r_out]
        k = pl.program_id(2)
        if rider:
            r_refs = (refs[n_in:n_in + r_in], refs[n_in + r_in + 1:n_in + r_in + 1 + r_out],
                      refs[n_in + r_in + 2 + r_out:])
            step = (pl.program_id(0) * grid[1] + pl.program_id(1)) * nk + k
            _ride_begin(rider, r_refs, step)

        @pl.when(k == 0)
        def _():
            acc[...] = jnp.zeros_like(acc)

        tot = None
        for p in range(npair):
            d = lax.dot_general(refs[2 * p][...].astype(BF16), refs[2 * p + 1][...].astype(BF16),
                                dims, preferred_element_type=F32)
            tot = d if tot is None else tot + d
        acc[...] += tot

        @pl.when(k == nk - 1)
        def _():
            r = acc[...]
            if res is not None:
                r = r + refs[2 * npair][...]
            o_ref[...] = r.astype(out_dtype)

        if rider:
            _ride_end(rider, r_refs, step, grid[0] * grid[1] * nk)

    if ta:
        a_spec = pl.BlockSpec((tk, tm), lambda i, j, k: (k, i))
    else:
        a_spec = pl.BlockSpec((tm, tk), lambda i, j, k: (i, k))
    if b_chips and tb:
        b_spec = pl.BlockSpec((None, tn, tk), lambda i, j, k: (k, j, 0))
    elif b_chips:
        b_spec = pl.BlockSpec((None, tk, tn), lambda i, j, k: (j, k, 0))
    elif tb:
        b_spec = pl.BlockSpec((tn, tk), lambda i, j, k: (j, k))
    else:
        b_spec = pl.BlockSpec((tk, tn), lambda i, j, k: (k, j))
    if out_chips:
        o_spec = pl.BlockSpec((None, tm, tn), lambda i, j, k: (j, i, 0))
        o_shape = SDS((4, m, tn), out_dtype)
    else:
        o_spec = pl.BlockSpec((tm, tn), lambda i, j, k: (i, j))
        o_shape = SDS((m, n), out_dtype)
    in_specs, args = [], []
    for a, b in pairs:
        in_specs += [a_spec, b_spec]
        args += [a, b]
    if res is not None:
        in_specs.append(o_spec)
        args.append(res)
    out_specs, out_shapes, scratch = [o_spec], [o_shape], [pltpu.VMEM((tm, tn), F32)]
    if rider:
        in_specs += [_ANY] * r_in
        args += rider.arrays
        out_specs += [_ANY] * r_out
        out_shapes += rider.out_shapes
        scratch += rider.scratch
    outs = _pcall(body, name=name, grid=grid, in_specs=in_specs, out_specs=out_specs, out_shape=out_shapes,
                  scratch=scratch)(*args)
    return (outs[0], outs[1:]) if rider else outs[0]


def _norm_fwd(x, w, name):
    s, d = x.shape
    ts = _tile(s, 512, 8)

    def body(x_ref, w_ref, h_ref):
        xv = x_ref[...]
        r = lax.rsqrt(jnp.mean(xv * xv, axis=-1, keepdims=True) + EPS)
        h_ref[...] = (xv * r * w_ref[...]).astype(BF16)

    return _pcall(body, name=name, grid=(s // ts,), in_specs=[_rows(ts, d), _full((1, d))],
                  out_specs=_rows(ts, d), out_shape=SDS((s, d), BF16))(x, w)


def _norm_bwd(dh, x, w, dres, name, with_bf16):
    s, d = x.shape
    ts = _tile(s, 256, 8)

    def body(dh_ref, x_ref, w_ref, dres_ref, dx_ref, dw_ref, *dx16_ref):
        @pl.when(pl.program_id(0) == 0)
        def _():
            dw_ref[...] = jnp.zeros_like(dw_ref)

        xv, dhv = x_ref[...], dh_ref[...]
        r = lax.rsqrt(jnp.mean(xv * xv, axis=-1, keepdims=True) + EPS)
        xh = xv * r
        dw_ref[...] += jnp.sum(dhv * xh, axis=0, keepdims=True)
        dxh = dhv * w_ref[...]
        dx = dres_ref[...] + r * (dxh - xh * jnp.mean(dxh * xh, axis=-1, keepdims=True))
        dx_ref[...] = dx
        for ref in dx16_ref:
            ref[...] = dx.astype(BF16)

    extra = 1 if with_bf16 else 0
    return _pcall(body, name=name, grid=(s // ts,),
                  in_specs=[_rows(ts, d), _rows(ts, d), _full((1, d)), _rows(ts, d)],
                  out_specs=[_rows(ts, d), _full((1, d))] + [_rows(ts, d)] * extra,
                  out_shape=[SDS((s, d), F32), SDS((1, d), F32)] + [SDS((s, d), BF16)] * extra)(
                      dh, x, w, dres)


def _final_loss(x3, tgt, w):
    s, d = x3.shape
    ts = _tile(s, 256, 8)

    def body(x_ref, t_ref, w_ref, dx_ref, dw_ref, loss_ref, dx16_ref):
        @pl.when(pl.program_id(0) == 0)
        def _():
            dw_ref[...] = jnp.zeros_like(dw_ref)
            loss_ref[...] = jnp.zeros_like(loss_ref)

        xv, wv = x_ref[...], w_ref[...]
        r = lax.rsqrt(jnp.mean(xv * xv, axis=-1, keepdims=True) + EPS)
        xh = xv * r
        err = xh * wv - t_ref[...]
        row = jnp.mean(err * err, axis=-1, keepdims=True)
        loss_ref[...] += 0.5 * jnp.sum(row, axis=0, keepdims=True)
        dy = err * (1.0 / d)
        dw_ref[...] += jnp.sum(dy * xh, axis=0, keepdims=True)
        dxh = dy * wv
        dx = r * (dxh - xh * jnp.mean(dxh * xh, axis=-1, keepdims=True))
        dx_ref[...] = dx
        dx16_ref[...] = dx.astype(BF16)

    return _pcall(body, name="final_loss", grid=(s // ts,),
                  in_specs=[_rows(ts, d), _rows(ts, d), _full((1, d))],
                  out_specs=[_rows(ts, d), _full((1, d)), _full((1, 1)), _rows(ts, d)],
                  out_shape=[SDS((s, d), F32), SDS((1, d), F32), SDS((1, 1), F32), SDS((s, d), BF16)])(
                      x3, tgt, w)


def _shift_down(cur, halo, s):
    if s == 0:
        return cur
    row8 = lax.broadcasted_iota(jnp.int32, halo.shape, 0)
    r = pltpu.roll(cur, s, 0)
    top = jnp.where(row8 < s, pltpu.roll(halo, s, 0), r[0:8])
    return jnp.concatenate([top, r[8:]], axis=0)


def _shift_up(cur, halo, s):
    if s == 0:
        return cur
    ts = cur.shape[0]
    row8 = lax.broadcasted_iota(jnp.int32, halo.shape, 0)
    r = pltpu.roll(cur, ts - s, 0)
    bot = jnp.where(row8 >= 8 - s, pltpu.roll(halo, 8 - s, 0), r[ts - 8:ts])
    return jnp.concatenate([r[:ts - 8], bot], axis=0)


def _chunk_tri(ts, upper):
    i = lax.broadcasted_iota(jnp.int32, (ts, ts), 0)
    j = lax.broadcasted_iota(jnp.int32, (ts, ts), 1)
    same = jnp.right_shift(i, 6) == jnp.right_shift(j, 6)
    return jnp.where(same & ((j >= i) if upper else (j <= i)), 1.0, 0.0).astype(F32)


def _gate_values(m, alog, dtb):
    lane = _lane_iota(m.shape)
    beta = _sigmoid(m)
    xg = m + dtb
    sp = jnp.maximum(xg, 0.0) + jnp.log(1.0 + jnp.exp(-jnp.abs(xg)))
    ga = (lane >= A_LANE) & (lane < A_LANE + 8)
    g = jnp.where(ga, -jnp.exp(alog) * sp, 0.0)
    return beta, g, xg, ga


def _l2_heads(a, nh, scale):
    outs, rs = [], []
    for h in range(nh):
        ah = a[:, HEAD * h:HEAD * (h + 1)]
        r = lax.rsqrt(jnp.sum(ah * ah, axis=-1, keepdims=True) + EPS)
        outs.append(ah * (r * scale))
        rs.append(r)
    return jnp.concatenate(outs, axis=-1), rs


def _gdn_prep(proj, conv_w, alog_l, dtb_l, nh, misc_col):
    s = proj.shape[0]
    w = nh * HEAD
    ts = _tile(s, 256, PAIR)
    hb = ts // 8

    def body(cur_ref, halo_ref, misc_ref, cw_ref, al_ref, db_ref, q_ref, k_ref, v_ref, gb_ref, gbt_ref):
        first = pl.program_id(0) == 0
        outs = (q_ref, k_ref, v_ref)
        for sec in range(3):
            cs = slice(sec * w, (sec + 1) * w)
            cur = cur_ref[:, cs]
            halo = jnp.where(first, 0.0, halo_ref[:, cs])
            pre = None
            for j in range(CONV):
                term = cw_ref[j:j + 1, cs] * _shift_down(cur, halo, CONV - 1 - j)
                pre = term if pre is None else pre + term
            act = _silu(pre)
            if sec == 0:
                act, _ = _l2_heads(act, nh, HEAD ** -0.5)
            elif sec == 1:
                act, _ = _l2_heads(act, nh, 1.0)
            outs[sec][...] = act
        m = misc_ref[...]
        lane = _lane_iota(m.shape)
        beta, g, _, ga = _gate_values(m, al_ref[...], db_ref[...])
        gcc = jnp.dot(_chunk_tri(ts, False), g, precision=lax.Precision.HIGHEST,
                      preferred_element_type=F32)
        gb = jnp.where((lane >= B_LANE) & (lane < B_LANE + 8), beta, jnp.where(ga, gcc, 0.0))
        gb_ref[...] = gb
        gbt_ref[...] = gb.T

    return _pcall(
        body, name="gdn_prep", grid=(s // ts,),
        in_specs=[_rows(ts, 3 * w),
                  pl.BlockSpec((8, 3 * w), lambda i: (jnp.maximum(i * hb - 1, 0), 0)),
                  _rows(ts, LANE, misc_col), _full((CONV, 3 * w)), _full((1, LANE)), _full((1, LANE))],
        out_specs=[_rows(ts, w), _rows(ts, w), _rows(ts, w), _rows(ts, LANE),
                   pl.BlockSpec((LANE, ts), lambda i: (0, i))],
        out_shape=[SDS((s, w), F32), SDS((s, w), F32), SDS((s, w), F32), SDS((s, LANE), F32),
                   SDS((LANE, s), F32)])(proj, proj, proj, conv_w, alog_l, dtb_l)


def _gdn_prep_bwd(proj, conv_w, alog_l, dtb_l, dq, dk, dv, dgb, dkr, nh, misc_col):
    s = proj.shape[0]
    w = nh * HEAD
    ts = _tile(s, 256, PAIR)
    hb = ts // 8

    def body(cur_ref, halo_ref, misc_ref, cw_ref, al_ref, db_ref, dq_ref, dk_ref, dv_ref, dgb_ref,
             dkr_ref, dc_ref, dm_ref, dcw_ref, dal_ref, ddb_ref):
        first = pl.program_id(0) == 0

        @pl.when(first)
        def _():
            dcw_ref[...] = jnp.zeros_like(dcw_ref)
            dal_ref[...] = jnp.zeros_like(dal_ref)
            ddb_ref[...] = jnp.zeros_like(ddb_ref)

        dins = (dq_ref, dk_ref, dv_ref)
        for sec in range(3):
            cs = slice(sec * w, (sec + 1) * w)
            cur = cur_ref[:, cs]
            halo = jnp.where(first, 0.0, halo_ref[:, cs])
            us = [_shift_down(cur, halo, CONV - 1 - j) for j in range(CONV)]
            pre = None
            for j in range(CONV):
                term = cw_ref[j:j + 1, cs] * us[j]
                pre = term if pre is None else pre + term
            act = _silu(pre)
            dout = dins[sec][...]
            if sec < 2:
                scale = HEAD ** -0.5 if sec == 0 else 1.0
                parts = []
                for h in range(nh):
                    hs = slice(HEAD * h, HEAD * (h + 1))
                    ah = act[:, hs]
                    r = lax.rsqrt(jnp.sum(ah * ah, axis=-1, keepdims=True) + EPS)
                    ahat = ah * r
                    dy = dout[:, hs]
                    parts.append((scale * r) * (dy - ahat * jnp.sum(dy * ahat, axis=-1, keepdims=True)))
                dact = jnp.concatenate(parts, axis=-1)
            else:
                dact = dout
            dconv = dact * _dsilu(pre)
            dc_ref[:, cs] = dconv
            for j in range(CONV):
                dcw_ref[j:j + 1, cs] += jnp.sum(dconv * us[j], axis=0, keepdims=True)
        m = misc_ref[...]
        lane = _lane_iota(m.shape)
        al = al_ref[...]
        beta, g, xg, ga = _gate_values(m, al, db_ref[...])
        dgbv = dgb_ref[...]
        dg = jnp.dot(_chunk_tri(ts, True), jnp.where(ga, dgbv, 0.0), precision=lax.Precision.HIGHEST,
                     preferred_element_type=F32)
        da_raw = jnp.where(ga, dg * (-jnp.exp(al)) * _sigmoid(xg), 0.0)
        db_raw = jnp.where((lane >= B_LANE) & (lane < B_LANE + 8), dgbv * beta * (1.0 - beta), 0.0)
        dal_ref[...] += jnp.sum(dg * g, axis=0, keepdims=True)
        ddb_ref[...] += jnp.sum(da_raw, axis=0, keepdims=True)
        dm_ref[...] = (dkr_ref[...] + da_raw + db_raw).astype(BF16)

    return _pcall(
        body, name="gdn_prep_bwd", grid=(s // ts,),
        in_specs=[_rows(ts, 3 * w),
                  pl.BlockSpec((8, 3 * w), lambda i: (jnp.maximum(i * hb - 1, 0), 0)),
                  _rows(ts, LANE, misc_col), _full((CONV, 3 * w)), _full((1, LANE)), _full((1, LANE)),
                  _rows(ts, w), _rows(ts, w), _rows(ts, w), _rows(ts, LANE), _rows(ts, LANE)],
        out_specs=[_rows(ts, 3 * w), _rows(ts, LANE), _full((CONV, 3 * w)), _full((1, LANE)),
                   _full((1, LANE))],
        out_shape=[SDS((s, 3 * w), F32), SDS((s, LANE), BF16), SDS((CONV, 3 * w), F32),
                   SDS((1, LANE), F32), SDS((1, LANE), F32)])(
                       proj, proj, proj, conv_w, alog_l, dtb_l, dq, dk, dv, dgb, dkr)


def _conv_bwd_input(dconv, conv_w):
    s, c = dconv.shape
    ts = _tile(s, 256, 8)
    hb = ts // 8
    nblk8 = s // 8
    nt = s // ts

    def body(cur_ref, nxt_ref, cw_ref, o_ref):
        last = pl.program_id(0) == nt - 1
        cur = cur_ref[...]
        halo = jnp.where(last, 0.0, nxt_ref[...])
        acc = None
        for j in range(CONV):
            term = cw_ref[j:j + 1, :] * _shift_up(cur, halo, CONV - 1 - j)
            acc = term if acc is None else acc + term
        o_ref[...] = acc.astype(BF16)

    return _pcall(
        body, name="conv_bwd_input", grid=(nt,),
        in_specs=[_rows(ts, c),
                  pl.BlockSpec((8, c), lambda i: (jnp.minimum((i + 1) * hb, nblk8 - 1), 0)),
                  _full((CONV, c))],
        out_specs=_rows(ts, c), out_shape=SDS((s, c), BF16))(dconv, dconv, conv_w)


def _inv_unit_lower(a):
    n = a.shape[0]
    i = lax.broadcasted_iota(jnp.int32, (n, n), 0)
    j = lax.broadcasted_iota(jnp.int32, (n, n), 1)
    t = jnp.where(i == j, 1.0, 0.0) - a
    x = a
    for _ in range(5):
        x = _dot(x, x)
        t = t + _dot(t, x)
    return t


def _pair_common(q, k, gcol, grow, bcol):
    i = lax.broadcasted_iota(jnp.int32, (PAIR, PAIR), 0)
    j = lax.broadcasted_iota(jnp.int32, (PAIR, PAIR), 1)
    same = jnp.right_shift(i, 6) == jnp.right_shift(j, 6)
    tril = same & (i >= j)
    strict = same & (i > j)
    dec = jnp.where(tril, jnp.exp(jnp.minimum(gcol - grow, 0.0)), 0.0)
    kk = _dot_nt(k, k)
    a = jnp.where(strict, bcol * kk * dec, 0.0)
    t = _inv_unit_lower(a)
    p = _dot_nt(q, k) * dec
    return dec, kk, a, t, p, tril, strict


def _ext(v, a):
    z = jnp.zeros_like(v)
    return jnp.concatenate([v, z] if a == 0 else [z, v], axis=0)


def _gdn_fwd(q, k, v, gb, gbt, nh, rider):
    s = q.shape[0]
    w = nh * HEAD
    npair = s // PAIR
    r_in, r_out = len(rider.arrays), len(rider.out_shapes)

    def body(*refs):
        q_ref, k_ref, v_ref, gb_ref, gbt_ref = refs[:5]
        o_ref, st_ref = refs[5 + r_in:7 + r_in]
        s_ref = refs[7 + r_in + r_out]
        r_refs = (refs[5:5 + r_in], refs[7 + r_in:7 + r_in + r_out], refs[8 + r_in + r_out:])
        _ride_begin(rider, r_refs, pl.program_id(0))

        @pl.when(pl.program_id(0) == 0)
        def _():
            s_ref[...] = jnp.zeros_like(s_ref)

        for h in range(nh):
            hs = slice(HEAD * h, HEAD * (h + 1))
            qh, kh, vh = q_ref[:, hs], k_ref[:, hs], v_ref[:, hs]
            gcol = _col(gb_ref[...], A_LANE + h)
            bcol = _col(gb_ref[...], B_LANE + h)
            grow = gbt_ref[A_LANE + h:A_LANE + h + 1, :]
            _, _, _, t, p, _, _ = _pair_common(qh, kh, gcol, grow, bcol)
            eg = jnp.exp(gcol)
            qg, kg = qh * eg, kh * eg
            outs = []
            for a in range(2):
                sl = slice(CHUNK * a, CHUNK * (a + 1))
                st = s_ref[h]
                st_ref[a, h] = st
                r = vh[sl] - _dot(kg[sl], st)
                vn = _dot(t[sl], _ext(bcol[sl] * r, a))
                outs.append(_dot(qg[sl], st) + _dot(p[sl], _ext(vn, a)))
                gl = _col(grow, CHUNK * (a + 1) - 1)
                kd = kh[sl] * jnp.exp(gl - gcol[sl])
                s_ref[h] = jnp.exp(gl) * st + _dot_tn(kd, vn)
            o_ref[:, hs] = jnp.concatenate(outs, axis=0)
        _ride_end(rider, r_refs, pl.program_id(0), npair)

    outs = _pcall(
        body, name="gdn_fwd", grid=(npair,),
        in_specs=[_rows(PAIR, w), _rows(PAIR, w), _rows(PAIR, w), _rows(PAIR, LANE),
                  pl.BlockSpec((LANE, PAIR), lambda i: (0, i))] + [_ANY] * r_in,
        out_specs=[_rows(PAIR, w), pl.BlockSpec((2, nh, HEAD, HEAD), lambda i: (i, 0, 0, 0))] + [_ANY] * r_out,
        out_shape=[SDS((s, w), F32), SDS((2 * npair, nh, HEAD, HEAD), F32)] + rider.out_shapes,
        scratch=[pltpu.VMEM((nh, HEAD, HEAD), F32)] + rider.scratch)(q, k, v, gb, gbt, *rider.arrays)
    return outs[0], outs[1], outs[2:]


def _gdn_bwd(q, k, v, gb, gbt, states, do, nh):
    s = q.shape[0]
    w = nh * HEAD
    npair = s // PAIR
    rev = lambda i: (npair - 1 - i, 0)

    def body(q_ref, k_ref, v_ref, gb_ref, gbt_ref, st_ref, do_ref, dq_ref, dk_ref, dv_ref, dgb_ref,
             ds_ref):
        @pl.when(pl.program_id(0) == 0)
        def _():
            ds_ref[...] = jnp.zeros_like(ds_ref)

        lane = _lane_iota((PAIR, LANE))
        row = lax.broadcasted_iota(jnp.int32, (CHUNK, 1), 0)
        dgb = jnp.zeros((PAIR, LANE), F32)
        for h in range(nh):
            hs = slice(HEAD * h, HEAD * (h + 1))
            qh, kh, vh, doh = q_ref[:, hs], k_ref[:, hs], v_ref[:, hs], do_ref[:, hs]
            gcol = _col(gb_ref[...], A_LANE + h)
            bcol = _col(gb_ref[...], B_LANE + h)
            grow = gbt_ref[A_LANE + h:A_LANE + h + 1, :]
            dec, kk, amat, t, p, tril, strict = _pair_common(qh, kh, gcol, grow, bcol)
            tt, pt = t.T, p.T
            eg = jnp.exp(gcol)
            qg, kg = qh * eg, kh * eg
            rs, vns = [], []
            for a in range(2):
                sl = slice(CHUNK * a, CHUNK * (a + 1))
                r = vh[sl] - _dot(kg[sl], st_ref[a, h])
                rs.append(r)
                vns.append(_dot(t[sl], _ext(bcol[sl] * r, a)))
            dsn = ds_ref[h]
            dqs, dks, dvs, dgcs, dbs, drbs = [None] * 2, [None] * 2, [None] * 2, [None] * 2, [None] * 2, [None] * 2
            for a in (1, 0):
                sl = slice(CHUNK * a, CHUNK * (a + 1))
                st = st_ref[a, h]
                gl = _col(grow, CHUNK * (a + 1) - 1)
                egl = jnp.exp(gl)
                dk_dec = jnp.exp(gl - gcol[sl])
                kd = kh[sl] * dk_dec
                d_vn = _dot(pt[sl], _ext(doh[sl], a)) + _dot(kd, dsn)
                d_qg = _dot_nt(doh[sl], st)
                d_rb = _dot(tt[sl], _ext(d_vn, a))
                dbs[a] = jnp.sum(d_rb * rs[a], axis=-1, keepdims=True)
                d_r = bcol[sl] * d_rb
                d_kg = -_dot_nt(d_r, st)
                d_kd = _dot_nt(vns[a], dsn)
                dgl = egl * jnp.sum(dsn * st, keepdims=True) + jnp.sum(d_kd * kd, keepdims=True)
                dgc = (jnp.sum(d_qg * qg[sl], axis=-1, keepdims=True)
                       + jnp.sum(d_kg * kg[sl], axis=-1, keepdims=True)
                       - jnp.sum(d_kd * kd, axis=-1, keepdims=True))
                dgcs[a] = dgc + jnp.where(row == CHUNK - 1, dgl, 0.0)
                dqs[a] = d_qg * eg[sl]
                dks[a] = d_kg * eg[sl] + d_kd * dk_dec
                dvs[a] = d_r
                drbs[a] = d_rb
                dsn = _dot_tn(qg[sl], doh[sl]) + egl * dsn - _dot_tn(kg[sl], d_r)
            ds_ref[h] = dsn
            cat = lambda xs: jnp.concatenate(xs, axis=0)
            vn, d_rb = cat(vns), cat(drbs)
            dp = jnp.where(tril, _dot_nt(doh, vn), 0.0)
            dam = jnp.where(strict, -_dot_nt(d_rb, vn), 0.0)
            g_p = dp * dec
            g_a = dam * dec
            gbk = bcol * g_a
            dq_ref[:, hs] = cat(dqs) + _dot(g_p, kh)
            dk_ref[:, hs] = cat(dks) + _dot_tn(g_p, qh) + _dot(gbk, kh) + _dot_tn(gbk, kh)
            dv_ref[:, hs] = cat(dvs)
            dbeta = cat(dbs) + jnp.sum(g_a * kk, axis=-1, keepdims=True)
            mm = dp * p + dam * amat
            dgc = cat(dgcs) + jnp.sum(mm, axis=-1, keepdims=True) - jnp.sum(mm.T, axis=-1, keepdims=True)
            dgb = dgb + jnp.where(lane == A_LANE + h, dgc, 0.0) + jnp.where(lane == B_LANE + h, dbeta, 0.0)
        dgb_ref[...] = dgb

    return _pcall(
        body, name="gdn_bwd", grid=(npair,),
        in_specs=[pl.BlockSpec((PAIR, w), rev), pl.BlockSpec((PAIR, w), rev), pl.BlockSpec((PAIR, w), rev),
                  pl.BlockSpec((PAIR, LANE), rev),
                  pl.BlockSpec((LANE, PAIR), lambda i: (0, npair - 1 - i)),
                  pl.BlockSpec((2, nh, HEAD, HEAD), lambda i: (npair - 1 - i, 0, 0, 0)),
                  pl.BlockSpec((PAIR, w), rev)],
        out_specs=[pl.BlockSpec((PAIR, w), rev), pl.BlockSpec((PAIR, w), rev), pl.BlockSpec((PAIR, w), rev),
                   pl.BlockSpec((PAIR, LANE), rev)],
        out_shape=[SDS((s, w), F32), SDS((s, w), F32), SDS((s, w), F32), SDS((s, LANE), F32)],
        scratch=[pltpu.VMEM((nh, HEAD, HEAD), F32)])(q, k, v, gb, gbt, states, do)


def _mla_norm(proj, qw, kvw, col_q, col_kv):
    s = proj.shape[0]
    lr = qw.shape[1]
    ts = _tile(s, 512, 8)

    def body(cq_ref, ckv_ref, qw_ref, kvw_ref, oq_ref, okv_ref):
        for x_ref, w_ref, o_ref in ((cq_ref, qw_ref, oq_ref), (ckv_ref, kvw_ref, okv_ref)):
            xv = x_ref[...]
            r = lax.rsqrt(jnp.mean(xv * xv, axis=-1, keepdims=True) + EPS)
            o_ref[...] = (xv * r * w_ref[...]).astype(BF16)

    return _pcall(body, name="mla_norm", grid=(s // ts,),
                  in_specs=[_rows(ts, lr, col_q), _rows(ts, lr, col_kv), _full((1, lr)), _full((1, lr))],
                  out_specs=[_rows(ts, lr), _rows(ts, lr)],
                  out_shape=[SDS((s, lr), BF16), SDS((s, lr), BF16)])(proj, proj, qw, kvw)


def _mla_norm_bwd(proj, qw, kvw, dq, dkv, col_q, col_kv):
    s = proj.shape[0]
    lr = qw.shape[1]
    ts = _tile(s, 512, 8)

    def body(cq_ref, ckv_ref, qw_ref, kvw_ref, dq_ref, dkv_ref, oq_ref, okv_ref, dqw_ref, dkvw_ref):
        @pl.when(pl.program_id(0) == 0)
        def _():
            dqw_ref[...] = jnp.zeros_like(dqw_ref)
            dkvw_ref[...] = jnp.zeros_like(dkvw_ref)

        for x_ref, w_ref, d_ref, o_ref, dw_ref in ((cq_ref, qw_ref, dq_ref, oq_ref, dqw_ref),
                                                    (ckv_ref, kvw_ref, dkv_ref, okv_ref, dkvw_ref)):
            xv, dh = x_ref[...], d_ref[...]
            r = lax.rsqrt(jnp.mean(xv * xv, axis=-1, keepdims=True) + EPS)
            xh = xv * r
            dw_ref[...] += jnp.sum(dh * xh, axis=0, keepdims=True)
            dxh = dh * w_ref[...]
            o_ref[...] = (r * (dxh - xh * jnp.mean(dxh * xh, axis=-1, keepdims=True))).astype(BF16)

    return _pcall(body, name="mla_norm_bwd", grid=(s // ts,),
                  in_specs=[_rows(ts, lr, col_q), _rows(ts, lr, col_kv), _full((1, lr)), _full((1, lr)),
                            _rows(ts, lr), _rows(ts, lr)],
                  out_specs=[_rows(ts, lr), _rows(ts, lr), _full((1, lr)), _full((1, lr))],
                  out_shape=[SDS((s, lr), BF16), SDS((s, lr), BF16), SDS((1, lr), F32),
                             SDS((1, lr), F32)])(proj, proj, qw, kvw, dq, dkv)


def _rope_tables(pos, invf, sgn):
    ang = pos * invf
    return jnp.cos(ang), jnp.sin(ang) * sgn


def _swap_halves_lanes(y):
    lane = _lane_iota(y.shape)
    return jnp.where(lane < ROPE // 2, pltpu.roll(y, LANE - ROPE // 2, 1), pltpu.roll(y, ROPE // 2, 1))


def _rope_consts():
    half = ROPE // 2
    inv = ROPE_THETA ** (-jnp.arange(half, dtype=F32) / half)
    invf = jnp.concatenate([inv, inv, jnp.zeros((LANE - ROPE,), F32)])[None, :]
    sgn = jnp.concatenate([-jnp.ones((half,), F32), jnp.ones((half,), F32),
                           jnp.zeros((LANE - ROPE,), F32)])[None, :]
    return invf, sgn


def _mla_rope(qraw, kvraw, proj, pos, nh, misc_col):
    s = qraw.shape[0]
    ts = _tile(s, 256, 8)
    wq = nh * 2 * HEAD
    invf, sgn = _rope_consts()

    def body(q_ref, kv_ref, misc_ref, pos_ref, if_ref, sg_ref, qc_ref, kc_ref, v_ref):
        c, sn = _rope_tables(pos_ref[...], if_ref[...], sg_ref[...])
        lane = _lane_iota(c.shape)
        rot = lambda xb: xb * c + _swap_halves_lanes(xb) * sn
        qs = SM_SCALE * LOG2E
        krot = jnp.where(lane < ROPE, rot(misc_ref[...]), 0.0).astype(BF16)
        for h in range(nh):
            b0 = 2 * HEAD * h
            qc_ref[:, b0:b0 + HEAD] = (q_ref[:, b0:b0 + HEAD] * qs).astype(BF16)
            qc_ref[:, b0 + HEAD:b0 + 2 * HEAD] = (rot(q_ref[:, b0 + HEAD:b0 + 2 * HEAD]) * qs).astype(BF16)
            kc_ref[:, b0:b0 + HEAD] = kv_ref[:, b0:b0 + HEAD].astype(BF16)
            kc_ref[:, b0 + HEAD:b0 + 2 * HEAD] = krot
        v_ref[...] = kv_ref[:, wq:].astype(BF16)

    return _pcall(body, name="mla_rope", grid=(s // ts,),
                  in_specs=[_rows(ts, wq), _rows(ts, wq + nh * HEAD), _rows(ts, LANE, misc_col),
                            _rows(ts, 1), _full((1, LANE)), _full((1, LANE))],
                  out_specs=[_rows(ts, wq), _rows(ts, wq), _rows(ts, nh * HEAD)],
                  out_shape=[SDS((s, wq), BF16), SDS((s, wq), BF16), SDS((s, nh * HEAD), BF16)])(
                      qraw, kvraw, proj, pos, invf, sgn)


def _mla_rope_bwd(dqc, dkc, dv, pos, nh):
    s = dqc.shape[0]
    ts = _tile(s, 256, 8)
    wq = nh * 2 * HEAD
    invf, sgn = _rope_consts()

    def body(dq_ref, dk_ref, dv_ref, pos_ref, if_ref, sg_ref, oq_ref, okv_ref, okr_ref):
        c, sn = _rope_tables(pos_ref[...], if_ref[...], sg_ref[...])
        lane = _lane_iota(c.shape)
        unrot = lambda d: d * c + _swap_halves_lanes(d * sn)
        dkr = jnp.zeros(c.shape, F32)
        for h in range(nh):
            b0 = 2 * HEAD * h
            oq_ref[:, b0:b0 + HEAD] = (dq_ref[:, b0:b0 + HEAD] * SM_SCALE).astype(BF16)
            oq_ref[:, b0 + HEAD:b0 + 2 * HEAD] = (
                unrot(dq_ref[:, b0 + HEAD:b0 + 2 * HEAD]) * SM_SCALE).astype(BF16)
            okv_ref[:, b0:b0 + HEAD] = (dk_ref[:, b0:b0 + HEAD] * LN2).astype(BF16)
            okv_ref[:, b0 + HEAD:b0 + 2 * HEAD] = jnp.zeros((ts, HEAD), BF16)
            dkr = dkr + dk_ref[:, b0 + HEAD:b0 + 2 * HEAD]
        okv_ref[:, wq:] = dv_ref[...].astype(BF16)
        okr_ref[...] = jnp.where(lane < ROPE, unrot(jnp.where(lane < ROPE, dkr * LN2, 0.0)), 0.0)

    return _pcall(body, name="mla_rope_bwd", grid=(s // ts,),
                  in_specs=[_rows(ts, wq), _rows(ts, wq), _rows(ts, nh * HEAD), _rows(ts, 1),
                            _full((1, LANE)), _full((1, LANE))],
                  out_specs=[_rows(ts, wq), _rows(ts, wq + nh * HEAD), _rows(ts, LANE)],
                  out_shape=[SDS((s, wq), BF16), SDS((s, wq + nh * HEAD), BF16), SDS((s, LANE), F32)])(
                      dqc, dkc, dv, pos, invf, sgn)


def _causal_mask(blk):
    i = lax.broadcasted_iota(jnp.int32, (blk, blk), 0)
    j = lax.broadcasted_iota(jnp.int32, (blk, blk), 1)
    return j <= i


def _mla_fwd(qc, kc, v, nh):
    s = qc.shape[0]
    blk = _tile(s, MLA_BLOCK)
    nb = s // blk
    rep = blk // LANE

    def body(q_ref, k_ref, v_ref, o_ref, lse_ref, m_sc, l_sc, acc):
        i = pl.program_id(1)
        q = q_ref[...]
        m_sc[...] = jnp.full_like(m_sc, -1e30)
        l_sc[...] = jnp.zeros_like(l_sc)
        acc[...] = jnp.zeros_like(acc)

        def step(j, masked):
            off = pl.multiple_of(j * blk, blk)
            sc = _dot_nt(q, k_ref[pl.ds(off, blk), :])
            if masked:
                sc = jnp.where(_causal_mask(blk), sc, -1e30)
            m_prev = m_sc[...]
            m_new = jnp.maximum(m_prev, jnp.max(sc, axis=-1, keepdims=True))
            p = jnp.exp2(sc - jnp.tile(m_new, (1, rep)))
            alpha = jnp.exp2(m_prev - m_new)
            l_sc[...] = alpha * l_sc[...] + jnp.sum(p, axis=-1, keepdims=True)
            acc[...] = alpha * acc[...] + _dot(p, v_ref[pl.ds(off, blk), :])
            m_sc[...] = m_new

        def loop_body(j, carry):
            step(j, False)
            return carry

        lax.fori_loop(0, i, loop_body, 0)
        step(i, True)
        o_ref[...] = acc[...] / l_sc[...]
        lse_ref[...] = m_sc[...] + jnp.log(l_sc[...]) * LOG2E

    return pl.pallas_call(
        body, name="mla_fwd", grid=(nh, nb),
        in_specs=[pl.BlockSpec((blk, 2 * HEAD), lambda h, i: (i, h)),
                  pl.BlockSpec((s, 2 * HEAD), lambda h, i: (0, h)),
                  pl.BlockSpec((s, HEAD), lambda h, i: (0, h))],
        out_specs=[pl.BlockSpec((blk, HEAD), lambda h, i: (i, h)),
                   pl.BlockSpec((None, blk, LANE), lambda h, i: (h, i, 0))],
        out_shape=[SDS((s, nh * HEAD), F32), SDS((nh, s, LANE), F32)],
        scratch_shapes=[pltpu.VMEM((blk, LANE), F32), pltpu.VMEM((blk, LANE), F32), pltpu.VMEM((blk, HEAD), F32)],
        compiler_params=pltpu.CompilerParams(dimension_semantics=("arbitrary",) * 2,
                                             vmem_limit_bytes=VMEM_LIMIT))(qc, kc, v)


def _mla_bwd(qc, kc, v, do, lse, delta, nh, rider):
    s = qc.shape[0]
    blk = _tile(s, MLA_BLOCK)
    nb = s // blk
    rep = blk // LANE
    once = pl.Buffered(1)
    r_in, r_out = len(rider.arrays), len(rider.out_shapes)

    def body(*refs):
        q_ref, do_ref, lse_ref, dl_ref, k_ref, v_ref = refs[:6]
        dq_ref, dk_ref, dv_ref = refs[6 + r_in:9 + r_in]
        dk_acc, dv_acc = refs[9 + r_in + r_out:11 + r_in + r_out]
        r_refs = (refs[6:6 + r_in], refs[9 + r_in:9 + r_in + r_out], refs[11 + r_in + r_out:])
        j = pl.program_id(1)
        grid_step = pl.program_id(0) * nb + j
        _ride_begin(rider, r_refs, grid_step)

        @pl.when(j == 0)
        def _():
            dq_ref[...] = jnp.zeros_like(dq_ref)

        dk_acc[...] = jnp.zeros_like(dk_acc)
        dv_acc[...] = jnp.zeros_like(dv_acc)
        kj, vj = k_ref[...], v_ref[...]

        def step(i, masked):
            rows = pl.ds(pl.multiple_of(i * blk, blk), blk)
            qi, doi = q_ref[rows, :], do_ref[rows, :]
            sc = _dot_nt(qi, kj)
            if masked:
                sc = jnp.where(_causal_mask(blk), sc, -1e30)
            p = jnp.exp2(sc - jnp.tile(lse_ref[rows, :], (1, rep)))
            dp = _dot_nt(doi, vj)
            ds = p * (dp - jnp.tile(dl_ref[rows, :], (1, rep)))
            dv_acc[...] += _dot_tn(p, doi)
            dk_acc[...] += _dot_tn(ds, qi)
            dq_ref[rows, :] += _dot(ds, kj)

        def loop_body(i, carry):
            step(i, False)
            return carry

        step(j, True)
        lax.fori_loop(j + 1, nb, loop_body, 0)
        dk_ref[...] = dk_acc[...]
        dv_ref[...] = dv_acc[...]
        _ride_end(rider, r_refs, grid_step, nh * nb)

    outs = _pcall(
        body, name="mla_bwd", grid=(nh, nb),
        in_specs=[pl.BlockSpec((s, 2 * HEAD), lambda h, j: (0, h), pipeline_mode=once),
                  pl.BlockSpec((s, HEAD), lambda h, j: (0, h), pipeline_mode=once),
                  pl.BlockSpec((None, s, LANE), lambda h, j: (h, 0, 0), pipeline_mode=once),
                  pl.BlockSpec((None, s, LANE), lambda h, j: (h, 0, 0), pipeline_mode=once),
                  pl.BlockSpec((blk, 2 * HEAD), lambda h, j: (j, h)),
                  pl.BlockSpec((blk, HEAD), lambda h, j: (j, h))] + [_ANY] * r_in,
        out_specs=[pl.BlockSpec((s, 2 * HEAD), lambda h, j: (0, h), pipeline_mode=once),
                   pl.BlockSpec((blk, 2 * HEAD), lambda h, j: (j, h)),
                   pl.BlockSpec((blk, HEAD), lambda h, j: (j, h))] + [_ANY] * r_out,
        out_shape=[SDS((s, nh * 2 * HEAD), F32), SDS((s, nh * 2 * HEAD), F32),
                   SDS((s, nh * HEAD), F32)] + rider.out_shapes,
        scratch=[pltpu.VMEM((blk, 2 * HEAD), F32), pltpu.VMEM((blk, HEAD), F32)] + rider.scratch)(
            qc, do, lse, delta, kc, v, *rider.arrays)
    return outs[0], outs[1], outs[2], outs[3:]


def _mix_fwd(og, proj, om, gw, mw, nh, z_col):
    s = og.shape[0]
    w = nh * HEAD
    ts = _tile(s, 256, 8)

    def body(og_ref, z_ref, om_ref, gw_ref, mw_ref, o_ref):
        for h in range(nh):
            hs = slice(HEAD * h, HEAD * (h + 1))
            a = og_ref[:, hs]
            r = lax.rsqrt(jnp.mean(a * a, axis=-1, keepdims=True) + EPS)
            o_ref[:, hs] = (a * r * gw_ref[...] * _silu(z_ref[:, hs])).astype(BF16)
            b = om_ref[:, hs]
            r = lax.rsqrt(jnp.mean(b * b, axis=-1, keepdims=True) + EPS)
            o_ref[:, w + HEAD * h:w + HEAD * (h + 1)] = (b * r * mw_ref[...]).astype(BF16)

    return _pcall(body, name="mix_fwd", grid=(s // ts,),
                  in_specs=[_rows(ts, w), _rows(ts, w, z_col), _rows(ts, w), _full((1, HEAD)),
                            _full((1, HEAD))],
                  out_specs=_rows(ts, 2 * w), out_shape=SDS((s, 2 * w), BF16))(og, proj, om, gw, mw)


def _mix_bwd(dmix, og, proj, om, gw, mw, nh, z_col):
    s = og.shape[0]
    w = nh * HEAD
    ts = _tile(s, 256, 8)

    def body(d_ref, og_ref, z_ref, om_ref, gw_ref, mw_ref, dog_ref, dz_ref, dom_ref, dgw_ref, dmw_ref,
             dl_ref):
        @pl.when(pl.program_id(0) == 0)
        def _():
            dgw_ref[...] = jnp.zeros_like(dgw_ref)
            dmw_ref[...] = jnp.zeros_like(dmw_ref)

        dgw = jnp.zeros((1, HEAD), F32)
        dmw = jnp.zeros((1, HEAD), F32)
        for h in range(nh):
            hs = slice(HEAD * h, HEAD * (h + 1))
            a, z, dy = og_ref[:, hs], z_ref[:, hs], d_ref[:, hs]
            r = lax.rsqrt(jnp.mean(a * a, axis=-1, keepdims=True) + EPS)
            ah = a * r
            sz = _silu(z)
            dz_ref[:, hs] = (dy * (ah * gw_ref[...]) * _dsilu(z)).astype(BF16)
            dn = dy * sz
            dgw = dgw + jnp.sum(dn * ah, axis=0, keepdims=True)
            dah = dn * gw_ref[...]
            dog_ref[:, hs] = r * (dah - ah * jnp.mean(dah * ah, axis=-1, keepdims=True))
            b, dyb = om_ref[:, hs], d_ref[:, w + HEAD * h:w + HEAD * (h + 1)]
            r = lax.rsqrt(jnp.mean(b * b, axis=-1, keepdims=True) + EPS)
            bh = b * r
            dmw = dmw + jnp.sum(dyb * bh, axis=0, keepdims=True)
            dbh = dyb * mw_ref[...]
            dom = r * (dbh - bh * jnp.mean(dbh * bh, axis=-1, keepdims=True))
            dom_ref[:, hs] = dom.astype(BF16)
            dl_ref[h] = jnp.broadcast_to(jnp.sum(dom * b, axis=-1, keepdims=True), (ts, LANE))
        dgw_ref[...] += dgw
        dmw_ref[...] += dmw

    return _pcall(body, name="mix_bwd", grid=(s // ts,),
                  in_specs=[_rows(ts, 2 * w), _rows(ts, w), _rows(ts, w, z_col), _rows(ts, w),
                            _full((1, HEAD)), _full((1, HEAD))],
                  out_specs=[_rows(ts, w), _rows(ts, w), _rows(ts, w), _full((1, HEAD)), _full((1, HEAD)),
                             pl.BlockSpec((nh, ts, LANE), lambda i: (0, i, 0))],
                  out_shape=[SDS((s, w), F32), SDS((s, w), BF16), SDS((s, w), BF16), SDS((1, HEAD), F32),
                             SDS((1, HEAD), F32), SDS((nh, s, LANE), F32)])(dmix, og, proj, om, gw, mw)


def _swiglu_fwd(h2, wg, wu):
    m, kdim = h2.shape
    tn = wg.shape[2]
    n = 4 * tn
    tm, tk = _tile(m, 512), _tile(kdim, 2048)
    nk = kdim // tk

    def body(a_ref, g_ref, u_ref, act_ref, go_ref, uo_ref, gacc, uacc):
        k = pl.program_id(2)

        @pl.when(k == 0)
        def _():
            gacc[...] = jnp.zeros_like(gacc)
            uacc[...] = jnp.zeros_like(uacc)

        a = a_ref[...]
        gacc[...] += _dot(a, g_ref[...])
        uacc[...] += _dot(a, u_ref[...])

        @pl.when(k == nk - 1)
        def _():
            g, u = gacc[...], uacc[...]
            act_ref[...] = (_silu(g) * u).astype(BF16)
            go_ref[...] = g.astype(BF16)
            uo_ref[...] = u.astype(BF16)

    a_spec = pl.BlockSpec((tm, tk), lambda i, j, k: (i, k))
    b_spec = pl.BlockSpec((None, tk, tn), lambda i, j, k: (j, k, 0))
    o_spec = pl.BlockSpec((tm, tn), lambda i, j, k: (i, j))
    return _pcall(body, name="swiglu_fwd", grid=(m // tm, n // tn, nk),
                  in_specs=[a_spec, b_spec, b_spec], out_specs=[o_spec] * 3,
                  out_shape=[SDS((m, n), BF16)] * 3,
                  scratch=[pltpu.VMEM((tm, tn), F32), pltpu.VMEM((tm, tn), F32)])(h2, wg, wu)


def _swiglu_bwd(dx3, wd, g, u):
    m, kdim = dx3.shape
    n = wd.shape[0]
    tm, tn, tk = _tile(m, 1024), _tile(n, 512), _tile(kdim, 2048)
    nk = kdim // tk

    def body(a_ref, b_ref, g_ref, u_ref, dg_ref, du_ref, acc):
        k = pl.program_id(2)

        @pl.when(k == 0)
        def _():
            acc[...] = jnp.zeros_like(acc)

        acc[...] += _dot_nt(a_ref[...], b_ref[...])

        @pl.when(k == nk - 1)
        def _():
            da = acc[...]
            gv, uv = g_ref[...].astype(F32), u_ref[...].astype(F32)
            dg_ref[...] = (da * uv * _dsilu(gv)).astype(BF16)
            du_ref[...] = (da * _silu(gv)).astype(BF16)

    a_spec = pl.BlockSpec((tm, tk), lambda i, j, k: (i, k))
    b_spec = pl.BlockSpec((tn, tk), lambda i, j, k: (j, k))
    o_spec = pl.BlockSpec((tm, tn), lambda i, j, k: (i, j))
    return _pcall(body, name="swiglu_bwd", grid=(m // tm, n // tn, nk),
                  in_specs=[a_spec, b_spec, o_spec, o_spec], out_specs=[o_spec] * 2,
                  out_shape=[SDS((m, n), BF16)] * 2,
                  scratch=[pltpu.VMEM((tm, tn), F32)])(dx3, wd, g, u)


def _sum_pair(g, recv, place, name):
    _, _, rh, c = g.shape
    tr = _tile(rh, 256, 16)

    def body(pl_ref, g_ref, r_ref, o16_ref, own_ref):
        sm = g_ref[...].astype(F32) + r_ref[...].astype(F32)
        o16_ref[...] = sm.astype(BF16)

        @pl.when(pl.program_id(1) == pl_ref[1])
        def _():
            own_ref[...] = sm

    grid_spec = pltpu.PrefetchScalarGridSpec(
        num_scalar_prefetch=1, grid=(rh // tr, 4),
        in_specs=[pl.BlockSpec((None, None, tr, c), lambda i, t, p: (t, p[0], i, 0)),
                  pl.BlockSpec((None, tr, c), lambda i, t, p: (t, i, 0))],
        out_specs=[pl.BlockSpec((None, tr, c), lambda i, t, p: (t, i, 0)),
                   pl.BlockSpec((tr, c), lambda i, t, p: (i, 0))])
    return pl.pallas_call(
        body, name=name, grid_spec=grid_spec,
        out_shape=[SDS((4, rh, c), BF16), SDS((rh, c), F32)],
        compiler_params=pltpu.CompilerParams(dimension_semantics=("arbitrary",) * 2,
                                             vmem_limit_bytes=VMEM_LIMIT))(place, g, recv)


def _sum_chips(own, recv, name):
    rh, c = own.shape
    tr = _tile(rh, 256, 16)

    def body(o_ref, r_ref, out_ref):
        acc = o_ref[...]
        for j in range(3):
            acc = acc + r_ref[j].astype(F32)
        out_ref[...] = acc

    return _pcall(body, name=name, grid=(rh // tr,),
                  in_specs=[_rows(tr, c), pl.BlockSpec((3, tr, c), lambda i: (0, i, 0))],
                  out_specs=_rows(tr, c), out_shape=SDS(own.shape, F32))(own, recv)


def _adamw(w, g, m, v, name):
    r, c = w.shape
    tr = _tile(r, 256, 8)

    def body(w_ref, g_ref, m_ref, v_ref, d_ref, mo_ref, vo_ref):
        gv = g_ref[...]
        mn = ADAM_B1 * m_ref[...] + (1.0 - ADAM_B1) * gv
        vn = ADAM_B2 * v_ref[...] + (1.0 - ADAM_B2) * (gv * gv)
        m_hat = mn / (1.0 - ADAM_B1 ** ADAM_STEP)
        v_hat = vn / (1.0 - ADAM_B2 ** ADAM_STEP)
        d_ref[...] = -ADAM_LR * (m_hat / (jnp.sqrt(v_hat) + ADAM_EPS) + ADAM_WD * w_ref[...])
        mo_ref[...] = mn
        vo_ref[...] = vn

    spec = _rows(tr, c)
    return _pcall(body, name=name, grid=(r // tr,), in_specs=[spec] * 4, out_specs=[spec] * 3,
                  out_shape=[SDS(w.shape, F32)] * 3)(w, g, m, v)


def _place():
    x, y, c = lax.axis_index("x"), lax.axis_index("y"), lax.axis_index("c")
    chips = [(1 - x, y), (x, 1 - y), (1 - x, 1 - y)]
    return x, y, c, chips


_ANY = pl.BlockSpec(memory_space=pl.ANY)


def _remote(src, dst, sems, k, to):
    return pltpu.make_async_remote_copy(src_ref=src, dst_ref=dst, send_sem=sems[0].at[k], recv_sem=sems[1].at[k],
                                        device_id=to, device_id_type=MESH)


class _Gather:
    def __init__(self, shards):
        n = len(shards)
        self.arrays = list(shards)
        self.out_shapes = [SDS((4,) + a.shape, a.dtype) for a in shards]
        self.scratch = [pltpu.SemaphoreType.DMA((6 * n,)), pltpu.SemaphoreType.DMA((6 * n,)),
                        pltpu.SemaphoreType.DMA((n,))]

    def _plan(self, ins, outs, sems):
        x, y, c, chips = _place()
        own = 2 * x + y
        plan = []
        for wi, (w, o) in enumerate(zip(ins, outs)):
            rh = w.shape[0] // 2
            mine, theirs = pl.ds(c * rh, rh), pl.ds((1 - c) * rh, rh)
            local = pltpu.make_async_copy(w, o.at[own], sems[2].at[wi])
            ici, d2d, d2d_in = [], [], []
            for j, (tx, ty) in enumerate(chips):
                t = 2 * tx + ty
                ici.append(_remote(w.at[mine], o.at[own, mine], sems, 6 * wi + j, (tx, ty, c)))
                d2d.append(_remote(o.at[t, mine], o.at[t, mine], sems, 6 * wi + 3 + j, (x, y, 1 - c)))
                d2d_in.append(_remote(o.at[t, theirs], o.at[t, theirs], sems, 6 * wi + 3 + j, (x, y, 1 - c)))
            plan.append((local, ici, d2d, d2d_in))
        return plan

    def begin(self, ins, outs, sems):
        for local, ici, _, _ in self._plan(ins, outs, sems):
            local.start()
            for cp in ici:
                cp.start()

    def middle(self, ins, outs, sems):
        for _, ici, d2d, _ in self._plan(ins, outs, sems):
            for cp_in, cp_on in zip(ici, d2d):
                cp_in.wait_recv()
                cp_on.start()

    def finish(self, ins, outs, sems):
        for local, ici, d2d, d2d_in in self._plan(ins, outs, sems):
            for cp in d2d_in:
                cp.wait_recv()
            for cp in ici + d2d:
                cp.wait_send()
            local.wait()


class _Swap:
    def __init__(self, grads):
        n = len(grads)
        self.arrays = list(grads)
        self.out_shapes = [SDS((4,) + g.shape[2:], g.dtype) for g in grads]
        self.scratch = [pltpu.SemaphoreType.DMA((4 * n,)), pltpu.SemaphoreType.DMA((4 * n,))]

    def _plan(self, ins, outs, sems):
        x, y, c, _ = _place()
        return [_remote(g.at[t, 1 - c], o.at[t], sems, 4 * wi + t, (x, y, 1 - c))
                for wi, (g, o) in enumerate(zip(ins, outs)) for t in range(4)]

    def begin(self, ins, outs, sems):
        for cp in self._plan(ins, outs, sems):
            cp.start()

    def middle(self, ins, outs, sems):
        pass

    def finish(self, ins, outs, sems):
        for cp in self._plan(ins, outs, sems):
            cp.wait()


class _Exchange:
    def __init__(self, pieces):
        n = len(pieces)
        self.arrays = list(pieces)
        self.out_shapes = [SDS((3,) + p.shape[1:], p.dtype) for p in pieces]
        self.scratch = [pltpu.SemaphoreType.DMA((3 * n,)), pltpu.SemaphoreType.DMA((3 * n,))]

    def _plan(self, ins, outs, sems):
        x, y, c, chips = _place()
        return [_remote(g.at[2 * tx + ty], o.at[j], sems, 3 * wi + j, (tx, ty, c))
                for wi, (g, o) in enumerate(zip(ins, outs)) for j, (tx, ty) in enumerate(chips)]

    def begin(self, ins, outs, sems):
        for cp in self._plan(ins, outs, sems):
            cp.start()

    def middle(self, ins, outs, sems):
        pass

    def finish(self, ins, outs, sems):
        for cp in self._plan(ins, outs, sems):
            cp.wait()


class _Share:
    def __init__(self, totals):
        n = len(totals)
        self.arrays = list(totals)
        self.out_shapes = [SDS((2,) + t.shape, t.dtype) for t in totals]
        self.scratch = [pltpu.SemaphoreType.DMA((n,)), pltpu.SemaphoreType.DMA((n,)), pltpu.SemaphoreType.DMA((n,))]

    def _plan(self, ins, outs, sems):
        x, y, c, _ = _place()
        return [(pltpu.make_async_copy(t, o.at[c], sems[2].at[wi]), _remote(t, o.at[c], sems, wi, (x, y, 1 - c)))
                for wi, (t, o) in enumerate(zip(ins, outs))]

    def begin(self, ins, outs, sems):
        for local, cp in self._plan(ins, outs, sems):
            local.start()
            cp.start()

    def middle(self, ins, outs, sems):
        pass

    def finish(self, ins, outs, sems):
        for local, cp in self._plan(ins, outs, sems):
            cp.wait()
            local.wait()


def _ride_begin(rider, r_refs, step):
    @pl.when(step == 0)
    def _():
        rider.begin(*r_refs)


def _ride_end(rider, r_refs, step, nsteps):
    @pl.when(step == min(3 * nsteps // 4, nsteps - 1))
    def _():
        rider.middle(*r_refs)

    @pl.when(step == nsteps - 1)
    def _():
        rider.finish(*r_refs)


def _comm(rider, name):
    n_in, n_out = len(rider.arrays), len(rider.out_shapes)

    def body(*refs):
        r_refs = (refs[:n_in], refs[n_in:n_in + n_out], refs[n_in + n_out:])
        rider.begin(*r_refs)
        rider.middle(*r_refs)
        rider.finish(*r_refs)

    return pl.pallas_call(body, name=name, out_shape=rider.out_shapes, in_specs=[_ANY] * n_in,
                          out_specs=[_ANY] * n_out, scratch_shapes=rider.scratch)(*rider.arrays)


def _small_allreduce(pk, name):
    r = pk.shape[0]
    rels = [(dx, dy, dc) for dx in (0, 1) for dy in (0, 1) for dc in (0, 1) if dx or dy or dc]

    def body(p_ref, o_ref, buf, send_sems, recv_sems):
        x, y, c, _ = _place()
        me = 4 * x + 2 * y + c
        buf[me] = p_ref[...]
        cps = []
        for k, (dx, dy, dc) in enumerate(rels):
            to = (1 - x if dx else x, 1 - y if dy else y, 1 - c if dc else c)
            cps.append(pltpu.make_async_remote_copy(src_ref=p_ref, dst_ref=buf.at[me], send_sem=send_sems.at[k],
                                                    recv_sem=recv_sems.at[k], device_id=to,
                                                    device_id_type=MESH))
        for cpy in cps:
            cpy.start()
        for cpy in cps:
            cpy.wait()
        acc = buf[0]
        for d in range(1, 8):
            acc = acc + buf[d]
        o_ref[...] = acc

    vm = pl.BlockSpec(memory_space=pltpu.VMEM)
    return pl.pallas_call(body, name=name, out_shape=SDS(pk.shape, F32), in_specs=[vm], out_specs=vm,
                          scratch_shapes=[pltpu.VMEM((8, r, LANE), F32), pltpu.SemaphoreType.DMA((7,)),
                                          pltpu.SemaphoreType.DMA((7,))])(pk)


ATTN_W = ("w_in", "w_uq", "w_ukv", "w_out")
FFN_W = ("w_gate", "w_up", "w_down")
BIG = ATTN_W + FFN_W


def _cols_from_chips(g):
    return jnp.concatenate([g[t] for t in range(4)], axis=1)


def _cols_to_chips(full):
    r, n = full.shape
    return full.reshape(r, 4, n // 4).transpose(1, 0, 2).reshape(4, 2, r // 2, n // 4)


def _rows_to_chips(full):
    n, c = full.shape
    return full.reshape(4, 2, n // 8, c)


def _permute_w_in(w, nh):
    d = w.shape[0]
    g = 4 * nh * HEAD
    lr = (w.shape[1] - g - 2 * nh - ROPE) // 2
    o = g + 2 * nh
    pad = jnp.zeros((d, LANE - ROPE - 8 - nh), w.dtype)
    pad8 = jnp.zeros((d, 8 - nh), w.dtype)
    return jnp.concatenate([w[:, :g], w[:, o:o + 2 * lr], w[:, o + 2 * lr:], w[:, g:g + nh], pad8,
                            w[:, g + nh:g + 2 * nh], pad, jnp.zeros((d, LANE), w.dtype)], axis=1)


def _unpermute_w_in(wp, nh, lr):
    g = 4 * nh * HEAD
    mc = g + 2 * lr
    return jnp.concatenate([wp[:, :g], wp[:, mc + B_LANE:mc + B_LANE + nh], wp[:, mc + A_LANE:mc + A_LANE + nh],
                            wp[:, g:g + 2 * lr], wp[:, mc:mc + ROPE]], axis=1)


def _permute_w_uq(w, nh):
    lr = w.shape[0]
    w3 = w.reshape(lr, nh, HEAD + ROPE)
    return jnp.concatenate([w3, jnp.zeros((lr, nh, HEAD - ROPE), w.dtype)], axis=2).reshape(lr, nh * 2 * HEAD)


def _unpermute_w_uq(wp, nh):
    lr = wp.shape[0]
    return wp.reshape(lr, nh, 2 * HEAD)[:, :, :HEAD + ROPE].reshape(lr, nh * (HEAD + ROPE))


def _permute_w_ukv(w, nh):
    lr = w.shape[0]
    w3 = w.reshape(lr, nh, 2 * HEAD)
    kp = jnp.concatenate([w3[:, :, :HEAD], jnp.zeros((lr, nh, HEAD), w.dtype)], axis=2)
    return jnp.concatenate([kp.reshape(lr, nh * 2 * HEAD), w3[:, :, HEAD:].reshape(lr, nh * HEAD)], axis=1)


def _unpermute_w_ukv(wp, nh):
    lr = wp.shape[0]
    kp = wp[:, :nh * 2 * HEAD].reshape(lr, nh, 2 * HEAD)[:, :, :HEAD]
    vp = wp[:, nh * 2 * HEAD:].reshape(lr, nh, HEAD)
    return jnp.concatenate([kp, vp], axis=2).reshape(lr, nh * 2 * HEAD)


def _reduce_begin(grads, place, tag):
    recv = _comm(_Swap(grads), "swap_" + tag)
    sums = [_sum_pair(g, r, place, "sum_pair_%s%d" % (tag, k)) for k, (g, r) in enumerate(zip(grads, recv))]
    return [s[0] for s in sums], [s[1] for s in sums]


def _reduce_end(own, recv, tag):
    return [_sum_chips(o, r, "sum_chips_%s%d" % (tag, k)) for k, (o, r) in enumerate(zip(own, recv))]


def _step(x, pos, tgt, wt, ffn_shards, small, place):
    nh = small["a_log"].shape[1]
    lr = small["q_norm_w"].shape[1]
    w = nh * HEAD
    z_col, col_q, col_kv = 3, 4 * w // lr, 4 * w // lr + 1
    misc_c = 4 * w + 2 * lr
    misc_col = misc_c // LANE
    assert (4 * w) % lr == 0 and small["kv_norm_w"].shape[1] == lr

    win_p = _permute_w_in(wt["w_in"], nh)
    wuq_p = _permute_w_uq(wt["w_uq"], nh)
    wukv_p = _permute_w_ukv(wt["w_ukv"], nh)
    zl = jnp.zeros((1, LANE), F32)
    alog_l = zl.at[:, A_LANE:A_LANE + nh].set(small["a_log"])
    dtb_l = zl.at[:, A_LANE:A_LANE + nh].set(small["dt_bias"])
    conv_w = small["conv_w"]

    h1 = _norm_fwd(x, small["attn_norm_w"], "norm1")
    proj = _mm([(h1, win_p)], name="proj_in")
    gq, gk, gv, gb, gbt = _gdn_prep(proj, conv_w, alog_l, dtb_l, nh, misc_col)
    o_gdn, states, (wg4, wu4, wd4) = _gdn_fwd(gq, gk, gv, gb, gbt, nh, _Gather(ffn_shards))
    w_down = wd4.reshape(-1, wd4.shape[2])
    cqn, ckvn = _mla_norm(proj, small["q_norm_w"], small["kv_norm_w"], col_q, col_kv)
    qraw = _mm([(cqn, wuq_p)], name="proj_uq")
    kvraw = _mm([(ckvn, wukv_p)], name="proj_ukv")
    qc, kc, vv = _mla_rope(qraw, kvraw, proj, pos, nh, misc_col)
    o_mla, lse = _mla_fwd(qc, kc, vv, nh)
    mixed = _mix_fwd(o_gdn, proj, o_mla, small["gdn_norm_w"], small["mla_out_norm_w"], nh, z_col)
    x2 = _mm([(mixed, wt["w_out"])], name="proj_out", res=x)
    h2 = _norm_fwd(x2, small["ffn_norm_w"], "norm2")
    act, gpre, upre = _swiglu_fwd(h2, wg4, wu4)
    x3 = _mm([(act, w_down)], name="proj_down", res=x2)
    dx3, d_final, loss, dx3h = _final_loss(x3, tgt, small["final_norm_w"])

    gs = {"final_norm_w": d_final}
    dgate, dup = _swiglu_bwd(dx3h, w_down, gpre, upre)
    g_down = _rows_to_chips(_mm([(act, dx3h)], name="dw_down", ta=True, out_dtype=BF16))
    dh2 = _mm([(dgate, wg4), (dup, wu4)], name="dh2", tb=True, b_chips=True)
    g_gate = _mm([(h2, dgate)], name="dw_gate", ta=True, out_dtype=BF16, out_chips=True)
    g_up = _mm([(h2, dup)], name="dw_up", ta=True, out_dtype=BF16, out_chips=True)
    halves = lambda g: g.reshape(4, 2, g.shape[1] // 2, g.shape[2])
    ffn16, ffn_own = _reduce_begin([halves(g_gate), halves(g_up), g_down], place, "ffn")
    dx2, gs["ffn_norm_w"], dx2h = _norm_bwd(dh2, x2, small["ffn_norm_w"], dx3, "norm2_bwd", True)
    dmix = _mm([(dx2h, wt["w_out"])], name="dmix", tb=True)
    g_out = _rows_to_chips(_mm([(mixed, dx2h)], name="dw_out", ta=True, out_dtype=BF16))
    d_ogdn, dz, d_omla, gs["gdn_norm_w"], gs["mla_out_norm_w"], delta = _mix_bwd(
        dmix, o_gdn, proj, o_mla, small["gdn_norm_w"], small["mla_out_norm_w"], nh, z_col)
    dqc, dkc, dvv, ffn_recv = _mla_bwd(qc, kc, vv, d_omla, lse, delta, nh, _Exchange(ffn16))
    ffn_tot = _reduce_end(ffn_own, ffn_recv, "ffn")
    dqraw, dkvraw, dkr = _mla_rope_bwd(dqc, dkc, dvv, pos, nh)
    dcqn = _mm([(dqraw, wuq_p)], name="dcqn", tb=True)
    dckvn = _mm([(dkvraw, wukv_p)], name="dckvn", tb=True)
    g_uq = _cols_to_chips(_unpermute_w_uq(_mm([(cqn, dqraw)], name="dw_uq", ta=True, out_dtype=BF16), nh))
    g_ukv = _cols_to_chips(_unpermute_w_ukv(_mm([(ckvn, dkvraw)], name="dw_ukv", ta=True, out_dtype=BF16), nh))
    dcq, dckv, gs["q_norm_w"], gs["kv_norm_w"] = _mla_norm_bwd(
        proj, small["q_norm_w"], small["kv_norm_w"], dcqn, dckvn, col_q, col_kv)
    dgq, dgk, dgv, dgb = _gdn_bwd(gq, gk, gv, gb, gbt, states, d_ogdn, nh)
    dconv, dmisc, gs["conv_w"], dal, ddb = _gdn_prep_bwd(
        proj, conv_w, alog_l, dtb_l, dgq, dgk, dgv, dgb, dkr, nh, misc_col)
    gs["a_log"] = dal[:, A_LANE:A_LANE + nh]
    gs["dt_bias"] = ddb[:, A_LANE:A_LANE + nh]
    dqkv = _conv_bwd_input(dconv, conv_w)
    dproj = jnp.concatenate([dqkv, dz, dcq, dckv, dmisc, jnp.zeros((x.shape[0], LANE), BF16)], axis=1)
    g_in = _cols_to_chips(_unpermute_w_in(_mm([(h1, dproj)], name="dw_in", ta=True, out_dtype=BF16), nh, lr))
    att16, att_own = _reduce_begin([g_in, g_uq, g_ukv, g_out], place, "att")
    dh1, att_recv = _mm([(dproj, win_p)], name="dh1", tb=True, rider=_Exchange(att16))
    att_tot = _reduce_end(att_own, att_recv, "att")
    grad_x, gs["attn_norm_w"] = _norm_bwd(dh1, x, small["attn_norm_w"], dx2, "norm1_bwd", False)
    return loss, grad_x, att_tot + ffn_tot, gs


SMALL = ("attn_norm_w", "ffn_norm_w", "final_norm_w", "q_norm_w", "kv_norm_w", "gdn_norm_w",
         "mla_out_norm_w", "a_log", "dt_bias")
WEIGHTS = ("attn_norm_w", "w_in", "conv_w", "a_log", "dt_bias", "gdn_norm_w", "q_norm_w", "w_uq",
           "kv_norm_w", "w_ukv", "mla_out_norm_w", "w_out", "ffn_norm_w", "w_gate", "w_up", "w_down",
           "final_norm_w")


def _pack_small(vecs):
    flat = jnp.concatenate([v.astype(F32).reshape(-1) for v in vecs])
    pad = (-flat.shape[0]) % (8 * LANE)
    return jnp.concatenate([flat, jnp.zeros((pad,), F32)]).reshape(-1, LANE)


def kernel(x, positions, attn_norm_w, w_in, conv_w, a_log, dt_bias, gdn_norm_w, q_norm_w, w_uq, kv_norm_w, w_ukv, mla_out_norm_w, w_out, ffn_norm_w, w_gate, w_up, w_down, final_norm_w, loss_target, m_attn_norm_w, m_w_in, m_conv_w, m_a_log, m_dt_bias, m_gdn_norm_w, m_q_norm_w, m_w_uq, m_kv_norm_w, m_w_ukv, m_mla_out_norm_w, m_w_out, m_ffn_norm_w, m_w_gate, m_w_up, m_w_down, m_final_norm_w, v_attn_norm_w, v_w_in, v_conv_w, v_a_log, v_dt_bias, v_gdn_norm_w, v_q_norm_w, v_w_uq, v_kv_norm_w, v_w_ukv, v_mla_out_norm_w, v_w_out, v_ffn_norm_w, v_w_gate, v_w_up, v_w_down, v_final_norm_w):
    args = dict(locals())
    xi, yi, ci = lax.axis_index("x"), lax.axis_index("y"), lax.axis_index("c")
    chip = 2 * xi + yi

    def two_d(a):
        return a.reshape(a.shape[-2:]) if a.ndim >= 2 else a.reshape(1, -1)

    wloc = {n: two_d(args[n]) for n in WEIGHTS}
    mloc = {n: two_d(args["m_" + n]) for n in WEIGHTS}
    vloc = {n: two_d(args["v_" + n]) for n in WEIGHTS}

    ga = _comm(_Gather([wloc[n].astype(BF16) for n in ATTN_W]), "gather_attn")
    wt = {"w_in": _cols_from_chips(ga[0]), "w_uq": _cols_from_chips(ga[1]), "w_ukv": _cols_from_chips(ga[2]),
          "w_out": ga[3].reshape(-1, ga[3].shape[2])}
    cw = wloc["conv_w"]
    cshard = cw.shape[1]
    cfull = jnp.zeros((CONV, 4 * cshard), F32)
    cfull = lax.dynamic_update_slice(cfull, jnp.where(ci == 0, cw, 0.0), (0, chip * cshard))
    conv_full = _small_allreduce(_pack_small([cfull]), "gather_conv_w").reshape(-1)[:CONV * 4 * cshard]
    conv_full = conv_full.reshape(CONV, 4 * cshard)

    small = {n: wloc[n] for n in SMALL}
    small["conv_w"] = conv_full

    pos = positions.reshape(-1, 1).astype(F32)
    place = jnp.stack([ci, chip]).astype(jnp.int32)
    loss, grad_x, totals, gs = _step(two_d(x), pos, two_d(loss_target), wt,
                                     [wloc[n].astype(BF16) for n in FFN_W], small, place)
    shared = _comm(_Share(totals), "share_halves")
    gbig = {n: g.reshape(wloc[n].shape) for n, g in zip(BIG, shared)}

    small_names = SMALL + ("conv_w",)
    pk = _pack_small([gs[n] for n in small_names] + [loss])
    red = _small_allreduce(pk, "reduce_small").reshape(-1)
    gsm, off = {}, 0
    for n in small_names:
        shp = gs[n].shape
        gsm[n] = red[off:off + shp[0] * shp[1]].reshape(shp)
        off += shp[0] * shp[1]
    loss_out = red[off]
    gsm["conv_w"] = lax.dynamic_slice(gsm["conv_w"], (0, chip * cshard), (CONV, cshard))

    grads, deltas, new_m, new_v = {}, {}, {}, {}
    for n in BIG + ("conv_w",):
        grads[n] = gbig[n] if n in gbig else gsm[n]
        deltas[n], new_m[n], new_v[n] = _adamw(wloc[n], grads[n], mloc[n], vloc[n], "adamw_" + n)
    sm_shapes = [wloc[n].shape for n in SMALL]
    pd, pm, pv = _adamw(_pack_small([wloc[n] for n in SMALL]), _pack_small([gsm[n] for n in SMALL]),
                        _pack_small([mloc[n] for n in SMALL]), _pack_small([vloc[n] for n in SMALL]),
                        "adamw_small")
    for dst, packed in ((deltas, pd), (new_m, pm), (new_v, pv)):
        flat, off = packed.reshape(-1), 0
        for n, shp in zip(SMALL, sm_shapes):
            dst[n] = flat[off:off + shp[0] * shp[1]].reshape(shp)
            off += shp[0] * shp[1]
    for n in SMALL:
        grads[n] = gsm[n]

    def like(n, a):
        return a.reshape(args[n].shape)

    outs = [loss_out.reshape(()), grad_x.reshape(x.shape)]
    for group in (grads, deltas, new_m, new_v):
        outs += [like(n, group[n]) for n in WEIGHTS]
    return tuple(outs)
```

```python
import functools

import jax
import jax.numpy as jnp
from jax import lax
from jax.experimental import pallas as pl
from jax.experimental.pallas import tpu as pltpu

F32, BF16 = jnp.float32, jnp.bfloat16
SDS = jax.ShapeDtypeStruct
MESH = pl.DeviceIdType.MESH

HEAD = 128
ROPE = 64
CHUNK = 64
PAIR = 2 * CHUNK
CONV = 4
EPS = 1e-6
ROPE_THETA = 10000.0
LANE = 128
B_LANE = 64
A_LANE = 72
VMEM_LIMIT = 48 * 1024 * 1024
MLA_BLOCK = 512
LOG2E = 1.4426950408889634
LN2 = 0.6931471805599453
SM_SCALE = (HEAD + ROPE) ** -0.5

ADAM_LR = 0.001
ADAM_B1 = 0.9
ADAM_B2 = 0.999
ADAM_EPS = 1e-08
ADAM_WD = 0.01
ADAM_STEP = 10


def _tile(n, pref, mult=LANE):
    if n <= pref:
        return n
    t = (pref // mult) * mult
    while t >= mult:
        if n % t == 0:
            return t
        t -= mult
    return n


def _pcall(body, *, name, grid, in_specs, out_specs, out_shape, scratch=()):
    return pl.pallas_call(
        body, name=name, grid=grid, in_specs=in_specs, out_specs=out_specs,
        out_shape=out_shape, scratch_shapes=list(scratch),
        compiler_params=pltpu.CompilerParams(
            dimension_semantics=("arbitrary",) * len(grid), vmem_limit_bytes=VMEM_LIMIT))


def _rows(ts, width, col=0):
    return pl.BlockSpec((ts, width), lambda i: (i, col))


def _full(shape):
    nd = len(shape)
    return pl.BlockSpec(shape, lambda i: (0,) * nd)


def _dot(a, b):
    return jnp.dot(a.astype(BF16), b.astype(BF16), preferred_element_type=F32)


def _dot_nt(a, b):
    return lax.dot_general(a.astype(BF16), b.astype(BF16), (((1,), (1,)), ((), ())),
                           preferred_element_type=F32)


def _dot_tn(a, b):
    return lax.dot_general(a.astype(BF16), b.astype(BF16), (((0,), (0,)), ((), ())),
                           preferred_element_type=F32)


def _sigmoid(x):
    return 1.0 / (1.0 + jnp.exp(-x))


def _silu(x):
    return x * _sigmoid(x)


def _dsilu(x):
    s = _sigmoid(x)
    return s * (1.0 + x * (1.0 - s))


def _lane_iota(shape):
    return lax.broadcasted_iota(jnp.int32, shape, len(shape) - 1)


def _col(block, idx):
    return jnp.sum(jnp.where(_lane_iota(block.shape) == idx, block, 0.0), axis=-1, keepdims=True)


def _mm(pairs, *, name, ta=False, tb=False, out_dtype=F32, res=None, tm=1024, tn=1024, tk=2048,
        b_chips=False, out_chips=False, rider=None):
    a0, b0 = pairs[0]
    if ta:
        kdim, m = a0.shape
    else:
        m, kdim = a0.shape
    if b_chips and tb:
        n, tk = b0.shape[1], b0.shape[2]
        assert kdim == 4 * tk
    elif b_chips:
        n, tn = 4 * b0.shape[2], b0.shape[2]
        assert kdim == b0.shape[1]
    else:
        n = b0.shape[0] if tb else b0.shape[1]
    if out_chips:
        tn = n // 4
    tm = _tile(m, tm)
    tn = tn if (out_chips or (b_chips and not tb)) else _tile(n, tn)
    tk = tk if (b_chips and tb) else _tile(kdim, tk)
    assert m % tm == 0 and n % tn == 0 and kdim % tk == 0
    nk, npair = kdim // tk, len(pairs)
    grid = (m // tm, n // tn, nk)
    dims = (((0 if ta else 1,), (1 if tb else 0,)), ((), ()))
    n_in = 2 * npair + (res is not None)
    r_in, r_out = (len(rider.arrays), len(rider.out_shapes)) if rider else (0, 0)

    def body(*refs):
        o_ref = refs[n_in + r_in]
        acc = refs[n_in + r_in + 1 + r_out]
        k = pl.program_id(2)
        if rider:
            r_refs = (refs[n_in:n_in + r_in], refs[n_in + r_in + 1:n_in + r_in + 1 + r_out],
                      refs[n_in + r_in + 2 + r_out:])
            step = (pl.program_id(0) * grid[1] + pl.program_id(1)) * nk + k
            _ride_begin(rider, r_refs, step)

        @pl.when(k == 0)
        def _():
            acc[...] = jnp.zeros_like(acc)

        tot = None
        for p in range(npair):
            d = lax.dot_general(refs[2 * p][...].astype(BF16), refs[2 * p + 1][...].astype(BF16),
                                dims, preferred_element_type=F32)
            tot = d if tot is None else tot + d
        acc[...] += tot

        @pl.when(k == nk - 1)
        def _():
            r = acc[...]
            if res is not None:
                r = r + refs[2 * npair][...]
            o_ref[...] = r.astype(out_dtype)

        if rider:
            _ride_end(rider, r_refs, step, grid[0] * grid[1] * nk)

    if ta:
        a_spec = pl.BlockSpec((tk, tm), lambda i, j, k: (k, i))
    else:
        a_spec = pl.BlockSpec((tm, tk), lambda i, j, k: (i, k))
    if b_chips and tb:
        b_spec = pl.BlockSpec((None, tn, tk), lambda i, j, k: (k, j, 0))
    elif b_chips:
        b_spec = pl.BlockSpec((None, tk, tn), lambda i, j, k: (j, k, 0))
    elif tb:
        b_spec = pl.BlockSpec((tn, tk), lambda i, j, k: (j, k))
    else:
        b_spec = pl.BlockSpec((tk, tn), lambda i, j, k: (k, j))
    if out_chips:
        o_spec = pl.BlockSpec((None, tm, tn), lambda i, j, k: (j, i, 0))
        o_shape = SDS((4, m, tn), out_dtype)
    else:
        o_spec = pl.BlockSpec((tm, tn), lambda i, j, k: (i, j))
        o_shape = SDS((m, n), out_dtype)
    in_specs, args = [], []
    for a, b in pairs:
        in_specs += [a_spec, b_spec]
        args += [a, b]
    if res is not None:
        in_specs.append(o_spec)
        args.append(res)
    out_specs, out_shapes, scratch = [o_spec], [o_shape], [pltpu.VMEM((tm, tn), F32)]
    if rider:
        in_specs += [_ANY] * r_in
        args += rider.arrays
        out_specs += [_ANY] * r_out
        out_shapes += rider.out_shapes
        scratch += rider.scratch
    outs = _pcall(body, name=name, grid=grid, in_specs=in_specs, out_specs=out_specs, out_shape=out_shapes,
                  scratch=scratch)(*args)
    return (outs[0], outs[1:]) if rider else outs[0]


def _norm_fwd(x, w, name):
    s, d = x.shape
    ts = _tile(s, 512, 8)

    def body(x_ref, w_ref, h_ref):
        xv = x_ref[...]
        r = lax.rsqrt(jnp.mean(xv * xv, axis=-1, keepdims=True) + EPS)
        h_ref[...] = (xv * r * w_ref[...]).astype(BF16)

    return _pcall(body, name=name, grid=(s // ts,), in_specs=[_rows(ts, d), _full((1, d))],
                  out_specs=_rows(ts, d), out_shape=SDS((s, d), BF16))(x, w)


def _norm_bwd(dh, x, w, dres, name, with_bf16):
    s, d = x.shape
    ts = _tile(s, 256, 8)

    def body(dh_ref, x_ref, w_ref, dres_ref, dx_ref, dw_ref, *dx16_ref):
        @pl.when(pl.program_id(0) == 0)
        def _():
            dw_ref[...] = jnp.zeros_like(dw_ref)

        xv, dhv = x_ref[...], dh_ref[...]
        r = lax.rsqrt(jnp.mean(xv * xv, axis=-1, keepdims=True) + EPS)
        xh = xv * r
        dw_ref[...] += jnp.sum(dhv * xh, axis=0, keepdims=True)
        dxh = dhv * w_ref[...]
        dx = dres_ref[...] + r * (dxh - xh * jnp.mean(dxh * xh, axis=-1, keepdims=True))
        dx_ref[...] = dx
        for ref in dx16_ref:
            ref[...] = dx.astype(BF16)

    extra = 1 if with_bf16 else 0
    return _pcall(body, name=name, grid=(s // ts,),
                  in_specs=[_rows(ts, d), _rows(ts, d), _full((1, d)), _rows(ts, d)],
                  out_specs=[_rows(ts, d), _full((1, d))] + [_rows(ts, d)] * extra,
                  out_shape=[SDS((s, d), F32), SDS((1, d), F32)] + [SDS((s, d), BF16)] * extra)(
                      dh, x, w, dres)


def _final_loss(x3, tgt, w):
    s, d = x3.shape
    ts = _tile(s, 256, 8)

    def body(x_ref, t_ref, w_ref, dx_ref, dw_ref, loss_ref, dx16_ref):
        @pl.when(pl.program_id(0) == 0)
        def _():
            dw_ref[...] = jnp.zeros_like(dw_ref)
            loss_ref[...] = jnp.zeros_like(loss_ref)

        xv, wv = x_ref[...], w_ref[...]
        r = lax.rsqrt(jnp.mean(xv * xv, axis=-1, keepdims=True) + EPS)
        xh = xv * r
        err = xh * wv - t_ref[...]
        row = jnp.mean(err * err, axis=-1, keepdims=True)
        loss_ref[...] += 0.5 * jnp.sum(row, axis=0, keepdims=True)
        dy = err * (1.0 / d)
        dw_ref[...] += jnp.sum(dy * xh, axis=0, keepdims=True)
        dxh = dy * wv
        dx = r * (dxh - xh * jnp.mean(dxh * xh, axis=-1, keepdims=True))
        dx_ref[...] = dx
        dx16_ref[...] = dx.astype(BF16)

    return _pcall(body, name="final_loss", grid=(s // ts,),
                  in_specs=[_rows(ts, d), _rows(ts, d), _full((1, d))],
                  out_specs=[_rows(ts, d), _full((1, d)), _full((1, 1)), _rows(ts, d)],
                  out_shape=[SDS((s, d), F32), SDS((1, d), F32), SDS((1, 1), F32), SDS((s, d), BF16)])(
                      x3, tgt, w)


def _shift_down(cur, halo, s):
    if s == 0:
        return cur
    row8 = lax.broadcasted_iota(jnp.int32, halo.shape, 0)
    r = pltpu.roll(cur, s, 0)
    top = jnp.where(row8 < s, pltpu.roll(halo, s, 0), r[0:8])
    return jnp.concatenate([top, r[8:]], axis=0)


def _shift_up(cur, halo, s):
    if s == 0:
        return cur
    ts = cur.shape[0]
    row8 = lax.broadcasted_iota(jnp.int32, halo.shape, 0)
    r = pltpu.roll(cur, ts - s, 0)
    bot = jnp.where(row8 >= 8 - s, pltpu.roll(halo, 8 - s, 0), r[ts - 8:ts])
    return jnp.concatenate([r[:ts - 8], bot], axis=0)


def _chunk_tri(ts, upper):
    i = lax.broadcasted_iota(jnp.int32, (ts, ts), 0)
    j = lax.broadcasted_iota(jnp.int32, (ts, ts), 1)
    same = jnp.right_shift(i, 6) == jnp.right_shift(j, 6)
    return jnp.where(same & ((j >= i) if upper else (j <= i)), 1.0, 0.0).astype(F32)


def _gate_values(m, alog, dtb):
    lane = _lane_iota(m.shape)
    beta = _sigmoid(m)
    xg = m + dtb
    sp = jnp.maximum(xg, 0.0) + jnp.log(1.0 + jnp.exp(-jnp.abs(xg)))
    ga = (lane >= A_LANE) & (lane < A_LANE + 8)
    g = jnp.where(ga, -jnp.exp(alog) * sp, 0.0)
    return beta, g, xg, ga


def _l2_heads(a, nh, scale):
    outs, rs = [], []
    for h in range(nh):
        ah = a[:, HEAD * h:HEAD * (h + 1)]
        r = lax.rsqrt(jnp.sum(ah * ah, axis=-1, keepdims=True) + EPS)
        outs.append(ah * (r * scale))
        rs.append(r)
    return jnp.concatenate(outs, axis=-1), rs


def _gdn_prep(proj, conv_w, alog_l, dtb_l, nh, misc_col):
    s = proj.shape[0]
    w = nh * HEAD
    ts = _tile(s, 256, PAIR)
    hb = ts // 8

    def body(cur_ref, halo_ref, misc_ref, cw_ref, al_ref, db_ref, q_ref, k_ref, v_ref, gb_ref, gbt_ref):
        first = pl.program_id(0) == 0
        outs = (q_ref, k_ref, v_ref)
        for sec in range(3):
            cs = slice(sec * w, (sec + 1) * w)
            cur = cur_ref[:, cs]
            halo = jnp.where(first, 0.0, halo_ref[:, cs])
            pre = None
            for j in range(CONV):
                term = cw_ref[j:j + 1, cs] * _shift_down(cur, halo, CONV - 1 - j)
                pre = term if pre is None else pre + term
            act = _silu(pre)
            if sec == 0:
                act, _ = _l2_heads(act, nh, HEAD ** -0.5)
            elif sec == 1:
                act, _ = _l2_heads(act, nh, 1.0)
            outs[sec][...] = act
        m = misc_ref[...]
        lane = _lane_iota(m.shape)
        beta, g, _, ga = _gate_values(m, al_ref[...], db_ref[...])
        gcc = jnp.dot(_chunk_tri(ts, False), g, precision=lax.Precision.HIGHEST,
                      preferred_element_type=F32)
        gb = jnp.where((lane >= B_LANE) & (lane < B_LANE + 8), beta, jnp.where(ga, gcc, 0.0))
        gb_ref[...] = gb
        gbt_ref[...] = gb.T

    return _pcall(
        body, name="gdn_prep", grid=(s // ts,),
        in_specs=[_rows(ts, 3 * w),
                  pl.BlockSpec((8, 3 * w), lambda i: (jnp.maximum(i * hb - 1, 0), 0)),
                  _rows(ts, LANE, misc_col), _full((CONV, 3 * w)), _full((1, LANE)), _full((1, LANE))],
        out_specs=[_rows(ts, w), _rows(ts, w), _rows(ts, w), _rows(ts, LANE),
                   pl.BlockSpec((LANE, ts), lambda i: (0, i))],
        out_shape=[SDS((s, w), F32), SDS((s, w), F32), SDS((s, w), F32), SDS((s, LANE), F32),
                   SDS((LANE, s), F32)])(proj, proj, proj, conv_w, alog_l, dtb_l)


def _gdn_prep_bwd(proj, conv_w, alog_l, dtb_l, dq, dk, dv, dgb, dkr, nh, misc_col):
    s = proj.shape[0]
    w = nh * HEAD
    ts = _tile(s, 256, PAIR)
    hb = ts // 8

    def body(cur_ref, halo_ref, misc_ref, cw_ref, al_ref, db_ref, dq_ref, dk_ref, dv_ref, dgb_ref,
             dkr_ref, dc_ref, dm_ref, dcw_ref, dal_ref, ddb_ref):
        first = pl.program_id(0) == 0

        @pl.when(first)
        def _():
            dcw_ref[...] = jnp.zeros_like(dcw_ref)
            dal_ref[...] = jnp.zeros_like(dal_ref)
            ddb_ref[...] = jnp.zeros_like(ddb_ref)

        dins = (dq_ref, dk_ref, dv_ref)
        for sec in range(3):
            cs = slice(sec * w, (sec + 1) * w)
            cur = cur_ref[:, cs]
            halo = jnp.where(first, 0.0, halo_ref[:, cs])
            us = [_shift_down(cur, halo, CONV - 1 - j) for j in range(CONV)]
            pre = None
            for j in range(CONV):
                term = cw_ref[j:j + 1, cs] * us[j]
                pre = term if pre is None else pre + term
            act = _silu(pre)
            dout = dins[sec][...]
            if sec < 2:
                scale = HEAD ** -0.5 if sec == 0 else 1.0
                parts = []
                for h in range(nh):
                    hs = slice(HEAD * h, HEAD * (h + 1))
                    ah = act[:, hs]
                    r = lax.rsqrt(jnp.sum(ah * ah, axis=-1, keepdims=True) + EPS)
                    ahat = ah * r
                    dy = dout[:, hs]
                    parts.append((scale * r) * (dy - ahat * jnp.sum(dy * ahat, axis=-1, keepdims=True)))
                dact = jnp.concatenate(parts, axis=-1)
            else:
                dact = dout
            dconv = dact * _dsilu(pre)
            dc_ref[:, cs] = dconv
            for j in range(CONV):
                dcw_ref[j:j + 1, cs] += jnp.sum(dconv * us[j], axis=0, keepdims=True)
        m = misc_ref[...]
        lane = _lane_iota(m.shape)
        al = al_ref[...]
        beta, g, xg, ga = _gate_values(m, al, db_ref[...])
        dgbv = dgb_ref[...]
        dg = jnp.dot(_chunk_tri(ts, True), jnp.where(ga, dgbv, 0.0), precision=lax.Precision.HIGHEST,
                     preferred_element_type=F32)
        da_raw = jnp.where(ga, dg * (-jnp.exp(al)) * _sigmoid(xg), 0.0)
        db_raw = jnp.where((lane >= B_LANE) & (lane < B_LANE + 8), dgbv * beta * (1.0 - beta), 0.0)
        dal_ref[...] += jnp.sum(dg * g, axis=0, keepdims=True)
        ddb_ref[...] += jnp.sum(da_raw, axis=0, keepdims=True)
        dm_ref[...] = (dkr_ref[...] + da_raw + db_raw).astype(BF16)

    return _pcall(
        body, name="gdn_prep_bwd", grid=(s // ts,),
        in_specs=[_rows(ts, 3 * w),
                  pl.BlockSpec((8, 3 * w), lambda i: (jnp.maximum(i * hb - 1, 0), 0)),
                  _rows(ts, LANE, misc_col), _full((CONV, 3 * w)), _full((1, LANE)), _full((1, LANE)),
                  _rows(ts, w), _rows(ts, w), _rows(ts, w), _rows(ts, LANE), _rows(ts, LANE)],
        out_specs=[_rows(ts, 3 * w), _rows(ts, LANE), _full((CONV, 3 * w)), _full((1, LANE)),
                   _full((1, LANE))],
        out_shape=[SDS((s, 3 * w), F32), SDS((s, LANE), BF16), SDS((CONV, 3 * w), F32),
                   SDS((1, LANE), F32), SDS((1, LANE), F32)])(
                       proj, proj, proj, conv_w, alog_l, dtb_l, dq, dk, dv, dgb, dkr)


def _conv_bwd_input(dconv, conv_w):
    s, c = dconv.shape
    ts = _tile(s, 256, 8)
    hb = ts // 8
    nblk8 = s // 8
    nt = s // ts

    def body(cur_ref, nxt_ref, cw_ref, o_ref):
        last = pl.program_id(0) == nt - 1
        cur = cur_ref[...]
        halo = jnp.where(last, 0.0, nxt_ref[...])
        acc = None
        for j in range(CONV):
            term = cw_ref[j:j + 1, :] * _shift_up(cur, halo, CONV - 1 - j)
            acc = term if acc is None else acc + term
        o_ref[...] = acc.astype(BF16)

    return _pcall(
        body, name="conv_bwd_input", grid=(nt,),
        in_specs=[_rows(ts, c),
                  pl.BlockSpec((8, c), lambda i: (jnp.minimum((i + 1) * hb, nblk8 - 1), 0)),
                  _full((CONV, c))],
        out_specs=_rows(ts, c), out_shape=SDS((s, c), BF16))(dconv, dconv, conv_w)


def _inv_unit_lower(a):
    n = a.shape[0]
    i = lax.broadcasted_iota(jnp.int32, (n, n), 0)
    j = lax.broadcasted_iota(jnp.int32, (n, n), 1)
    t = jnp.where(i == j, 1.0, 0.0) - a
    x = a
    for _ in range(5):
        x = _dot(x, x)
        t = t + _dot(t, x)
    return t


def _pair_common(q, k, gcol, grow, bcol):
    i = lax.broadcasted_iota(jnp.int32, (PAIR, PAIR), 0)
    j = lax.broadcasted_iota(jnp.int32, (PAIR, PAIR), 1)
    same = jnp.right_shift(i, 6) == jnp.right_shift(j, 6)
    tril = same & (i >= j)
    strict = same & (i > j)
    dec = jnp.where(tril, jnp.exp(jnp.minimum(gcol - grow, 0.0)), 0.0)
    kk = _dot_nt(k, k)
    a = jnp.where(strict, bcol * kk * dec, 0.0)
    t = _inv_unit_lower(a)
    p = _dot_nt(q, k) * dec
    return dec, kk, a, t, p, tril, strict


def _ext(v, a):
    z = jnp.zeros_like(v)
    return jnp.concatenate([v, z] if a == 0 else [z, v], axis=0)


def _gdn_fwd(q, k, v, gb, gbt, nh, rider):
    s = q.shape[0]
    w = nh * HEAD
    npair = s // PAIR
    r_in, r_out = len(rider.arrays), len(rider.out_shapes)

    def body(*refs):
        q_ref, k_ref, v_ref, gb_ref, gbt_ref = refs[:5]
        o_ref, st_ref = refs[5 + r_in:7 + r_in]
        s_ref = refs[7 + r_in + r_out]
        r_refs = (refs[5:5 + r_in], refs[7 + r_in:7 + r_in + r_out], refs[8 + r_in + r_out:])
        _ride_begin(rider, r_refs, pl.program_id(0))

        @pl.when(pl.program_id(0) == 0)
        def _():
            s_ref[...] = jnp.zeros_like(s_ref)

        for h in range(nh):
            hs = slice(HEAD * h, HEAD * (h + 1))
            qh, kh, vh = q_ref[:, hs], k_ref[:, hs], v_ref[:, hs]
            gcol = _col(gb_ref[...], A_LANE + h)
            bcol = _col(gb_ref[...], B_LANE + h)
            grow = gbt_ref[A_LANE + h:A_LANE + h + 1, :]
            _, _, _, t, p, _, _ = _pair_common(qh, kh, gcol, grow, bcol)
            eg = jnp.exp(gcol)
            qg, kg = qh * eg, kh * eg
            outs = []
            for a in range(2):
                sl = slice(CHUNK * a, CHUNK * (a + 1))
                st = s_ref[h]
                st_ref[a, h] = st
                r = vh[sl] - _dot(kg[sl], st)
                vn = _dot(t[sl], _ext(bcol[sl] * r, a))
                outs.append(_dot(qg[sl], st) + _dot(p[sl], _ext(vn, a)))
                gl = _col(grow, CHUNK * (a + 1) - 1)
                kd = kh[sl] * jnp.exp(gl - gcol[sl])
                s_ref[h] = jnp.exp(gl) * st + _dot_tn(kd, vn)
            o_ref[:, hs] = jnp.concatenate(outs, axis=0)
        _ride_end(rider, r_refs, pl.program_id(0), npair)

    outs = _pcall(
        body, name="gdn_fwd", grid=(npair,),
        in_specs=[_rows(PAIR, w), _rows(PAIR, w), _rows(PAIR, w), _rows(PAIR, LANE),
                  pl.BlockSpec((LANE, PAIR), lambda i: (0, i))] + [_ANY] * r_in,
        out_specs=[_rows(PAIR, w), pl.BlockSpec((2, nh, HEAD, HEAD), lambda i: (i, 0, 0, 0))] + [_ANY] * r_out,
        out_shape=[SDS((s, w), F32), SDS((2 * npair, nh, HEAD, HEAD), F32)] + rider.out_shapes,
        scratch=[pltpu.VMEM((nh, HEAD, HEAD), F32)] + rider.scratch)(q, k, v, gb, gbt, *rider.arrays)
    return outs[0], outs[1], outs[2:]


def _gdn_bwd(q, k, v, gb, gbt, states, do, nh):
    s = q.shape[0]
    w = nh * HEAD
    npair = s // PAIR
    rev = lambda i: (npair - 1 - i, 0)

    def body(q_ref, k_ref, v_ref, gb_ref, gbt_ref, st_ref, do_ref, dq_ref, dk_ref, dv_ref, dgb_ref,
             ds_ref):
        @pl.when(pl.program_id(0) == 0)
        def _():
            ds_ref[...] = jnp.zeros_like(ds_ref)

        lane = _lane_iota((PAIR, LANE))
        row = lax.broadcasted_iota(jnp.int32, (CHUNK, 1), 0)
        dgb = jnp.zeros((PAIR, LANE), F32)
        for h in range(nh):
            hs = slice(HEAD * h, HEAD * (h + 1))
            qh, kh, vh, doh = q_ref[:, hs], k_ref[:, hs], v_ref[:, hs], do_ref[:, hs]
            gcol = _col(gb_ref[...], A_LANE + h)
            bcol = _col(gb_ref[...], B_LANE + h)
            grow = gbt_ref[A_LANE + h:A_LANE + h + 1, :]
            dec, kk, amat, t, p, tril, strict = _pair_common(qh, kh, gcol, grow, bcol)
            tt, pt = t.T, p.T
            eg = jnp.exp(gcol)
            qg, kg = qh * eg, kh * eg
            rs, vns = [], []
            for a in range(2):
                sl = slice(CHUNK * a, CHUNK * (a + 1))
                r = vh[sl] - _dot(kg[sl], st_ref[a, h])
                rs.append(r)
                vns.append(_dot(t[sl], _ext(bcol[sl] * r, a)))
            dsn = ds_ref[h]
            dqs, dks, dvs, dgcs, dbs, drbs = [None] * 2, [None] * 2, [None] * 2, [None] * 2, [None] * 2, [None] * 2
            for a in (1, 0):
                sl = slice(CHUNK * a, CHUNK * (a + 1))
                st = st_ref[a, h]
                gl = _col(grow, CHUNK * (a + 1) - 1)
                egl = jnp.exp(gl)
                dk_dec = jnp.exp(gl - gcol[sl])
                kd = kh[sl] * dk_dec
                d_vn = _dot(pt[sl], _ext(doh[sl], a)) + _dot(kd, dsn)
                d_qg = _dot_nt(doh[sl], st)
                d_rb = _dot(tt[sl], _ext(d_vn, a))
                dbs[a] = jnp.sum(d_rb * rs[a], axis=-1, keepdims=True)
                d_r = bcol[sl] * d_rb
                d_kg = -_dot_nt(d_r, st)
                d_kd = _dot_nt(vns[a], dsn)
                dgl = egl * jnp.sum(dsn * st, keepdims=True) + jnp.sum(d_kd * kd, keepdims=True)
                dgc = (jnp.sum(d_qg * qg[sl], axis=-1, keepdims=True)
                       + jnp.sum(d_kg * kg[sl], axis=-1, keepdims=True)
                       - jnp.sum(d_kd * kd, axis=-1, keepdims=True))
                dgcs[a] = dgc + jnp.where(row == CHUNK - 1, dgl, 0.0)
                dqs[a] = d_qg * eg[sl]
                dks[a] = d_kg * eg[sl] + d_kd * dk_dec
                dvs[a] = d_r
                drbs[a] = d_rb
                dsn = _dot_tn(qg[sl], doh[sl]) + egl * dsn - _dot_tn(kg[sl], d_r)
            ds_ref[h] = dsn
            cat = lambda xs: jnp.concatenate(xs, axis=0)
            vn, d_rb = cat(vns), cat(drbs)
            dp = jnp.where(tril, _dot_nt(doh, vn), 0.0)
            dam = jnp.where(strict, -_dot_nt(d_rb, vn), 0.0)
            g_p = dp * dec
            g_a = dam * dec
            gbk = bcol * g_a
            dq_ref[:, hs] = cat(dqs) + _dot(g_p, kh)
            dk_ref[:, hs] = cat(dks) + _dot_tn(g_p, qh) + _dot(gbk, kh) + _dot_tn(gbk, kh)
            dv_ref[:, hs] = cat(dvs)
            dbeta = cat(dbs) + jnp.sum(g_a * kk, axis=-1, keepdims=True)
            mm = dp * p + dam * amat
            dgc = cat(dgcs) + jnp.sum(mm, axis=-1, keepdims=True) - jnp.sum(mm.T, axis=-1, keepdims=True)
            dgb = dgb + jnp.where(lane == A_LANE + h, dgc, 0.0) + jnp.where(lane == B_LANE + h, dbeta, 0.0)
        dgb_ref[...] = dgb

    return _pcall(
        body, name="gdn_bwd", grid=(npair,),
        in_specs=[pl.BlockSpec((PAIR, w), rev), pl.BlockSpec((PAIR, w), rev), pl.BlockSpec((PAIR, w), rev),
                  pl.BlockSpec((PAIR, LANE), rev),
                  pl.BlockSpec((LANE, PAIR), lambda i: (0, npair - 1 - i)),
                  pl.BlockSpec((2, nh, HEAD, HEAD), lambda i: (npair - 1 - i, 0, 0, 0)),
                  pl.BlockSpec((PAIR, w), rev)],
        out_specs=[pl.BlockSpec((PAIR, w), rev), pl.BlockSpec((PAIR, w), rev), pl.BlockSpec((PAIR, w), rev),
                   pl.BlockSpec((PAIR, LANE), rev)],
        out_shape=[SDS((s, w), F32), SDS((s, w), F32), SDS((s, w), F32), SDS((s, LANE), F32)],
        scratch=[pltpu.VMEM((nh, HEAD, HEAD), F32)])(q, k, v, gb, gbt, states, do)


def _mla_norm(proj, qw, kvw, col_q, col_kv):
    s = proj.shape[0]
    lr = qw.shape[1]
    ts = _tile(s, 512, 8)

    def body(cq_ref, ckv_ref, qw_ref, kvw_ref, oq_ref, okv_ref):
        for x_ref, w_ref, o_ref in ((cq_ref, qw_ref, oq_ref), (ckv_ref, kvw_ref, okv_ref)):
            xv = x_ref[...]
            r = lax.rsqrt(jnp.mean(xv * xv, axis=-1, keepdims=True) + EPS)
            o_ref[...] = (xv * r * w_ref[...]).astype(BF16)

    return _pcall(body, name="mla_norm", grid=(s // ts,),
                  in_specs=[_rows(ts, lr, col_q), _rows(ts, lr, col_kv), _full((1, lr)), _full((1, lr))],
                  out_specs=[_rows(ts, lr), _rows(ts, lr)],
                  out_shape=[SDS((s, lr), BF16), SDS((s, lr), BF16)])(proj, proj, qw, kvw)


def _mla_norm_bwd(proj, qw, kvw, dq, dkv, col_q, col_kv):
    s = proj.shape[0]
    lr = qw.shape[1]
    ts = _tile(s, 512, 8)

    def body(cq_ref, ckv_ref, qw_ref, kvw_ref, dq_ref, dkv_ref, oq_ref, okv_ref, dqw_ref, dkvw_ref):
        @pl.when(pl.program_id(0) == 0)
        def _():
            dqw_ref[...] = jnp.zeros_like(dqw_ref)
            dkvw_ref[...] = jnp.zeros_like(dkvw_ref)

        for x_ref, w_ref, d_ref, o_ref, dw_ref in ((cq_ref, qw_ref, dq_ref, oq_ref, dqw_ref),
                                                    (ckv_ref, kvw_ref, dkv_ref, okv_ref, dkvw_ref)):
            xv, dh = x_ref[...], d_ref[...]
            r = lax.rsqrt(jnp.mean(xv * xv, axis=-1, keepdims=True) + EPS)
            xh = xv * r
            dw_ref[...] += jnp.sum(dh * xh, axis=0, keepdims=True)
            dxh = dh * w_ref[...]
            o_ref[...] = (r * (dxh - xh * jnp.mean(dxh * xh, axis=-1, keepdims=True))).astype(BF16)

    return _pcall(body, name="mla_norm_bwd", grid=(s // ts,),
                  in_specs=[_rows(ts, lr, col_q), _rows(ts, lr, col_kv), _full((1, lr)), _full((1, lr)),
                            _rows(ts, lr), _rows(ts, lr)],
                  out_specs=[_rows(ts, lr), _rows(ts, lr), _full((1, lr)), _full((1, lr))],
                  out_shape=[SDS((s, lr), BF16), SDS((s, lr), BF16), SDS((1, lr), F32),
                             SDS((1, lr), F32)])(proj, proj, qw, kvw, dq, dkv)


def _rope_tables(pos, invf, sgn):
    ang = pos * invf
    return jnp.cos(ang), jnp.sin(ang) * sgn


def _swap_halves_lanes(y):
    lane = _lane_iota(y.shape)
    return jnp.where(lane < ROPE // 2, pltpu.roll(y, LANE - ROPE // 2, 1), pltpu.roll(y, ROPE // 2, 1))


def _rope_consts():
    half = ROPE // 2
    inv = ROPE_THETA ** (-jnp.arange(half, dtype=F32) / half)
    invf = jnp.concatenate([inv, inv, jnp.zeros((LANE - ROPE,), F32)])[None, :]
    sgn = jnp.concatenate([-jnp.ones((half,), F32), jnp.ones((half,), F32),
                           jnp.zeros((LANE - ROPE,), F32)])[None, :]
    return invf, sgn


def _mla_rope(qraw, kvraw, proj, pos, nh, misc_col):
    s = qraw.shape[0]
    ts = _tile(s, 256, 8)
    wq = nh * 2 * HEAD
    invf, sgn = _rope_consts()

    def body(q_ref, kv_ref, misc_ref, pos_ref, if_ref, sg_ref, qc_ref, kc_ref, v_ref):
        c, sn = _rope_tables(pos_ref[...], if_ref[...], sg_ref[...])
        lane = _lane_iota(c.shape)
        rot = lambda xb: xb * c + _swap_halves_lanes(xb) * sn
        qs = SM_SCALE * LOG2E
        krot = jnp.where(lane < ROPE, rot(misc_ref[...]), 0.0).astype(BF16)
        for h in range(nh):
            b0 = 2 * HEAD * h
            qc_ref[:, b0:b0 + HEAD] = (q_ref[:, b0:b0 + HEAD] * qs).astype(BF16)
            qc_ref[:, b0 + HEAD:b0 + 2 * HEAD] = (rot(q_ref[:, b0 + HEAD:b0 + 2 * HEAD]) * qs).astype(BF16)
            kc_ref[:, b0:b0 + HEAD] = kv_ref[:, b0:b0 + HEAD].astype(BF16)
            kc_ref[:, b0 + HEAD:b0 + 2 * HEAD] = krot
        v_ref[...] = kv_ref[:, wq:].astype(BF16)

    return _pcall(body, name="mla_rope", grid=(s // ts,),
                  in_specs=[_rows(ts, wq), _rows(ts, wq + nh * HEAD), _rows(ts, LANE, misc_col),
                            _rows(ts, 1), _full((1, LANE)), _full((1, LANE))],
                  out_specs=[_rows(ts, wq), _rows(ts, wq), _rows(ts, nh * HEAD)],
                  out_shape=[SDS((s, wq), BF16), SDS((s, wq), BF16), SDS((s, nh * HEAD), BF16)])(
                      qraw, kvraw, proj, pos, invf, sgn)


def _mla_rope_bwd(dqc, dkc, dv, pos, nh):
    s = dqc.shape[0]
    ts = _tile(s, 256, 8)
    wq = nh * 2 * HEAD
    invf, sgn = _rope_consts()

    def body(dq_ref, dk_ref, dv_ref, pos_ref, if_ref, sg_ref, oq_ref, okv_ref, okr_ref):
        c, sn = _rope_tables(pos_ref[...], if_ref[...], sg_ref[...])
        lane = _lane_iota(c.shape)
        unrot = lambda d: d * c + _swap_halves_lanes(d * sn)
        dkr = jnp.zeros(c.shape, F32)
        for h in range(nh):
            b0 = 2 * HEAD * h
            oq_ref[:, b0:b0 + HEAD] = (dq_ref[:, b0:b0 + HEAD] * SM_SCALE).astype(BF16)
            oq_ref[:, b0 + HEAD:b0 + 2 * HEAD] = (
                unrot(dq_ref[:, b0 + HEAD:b0 + 2 * HEAD]) * SM_SCALE).astype(BF16)
            okv_ref[:, b0:b0 + HEAD] = (dk_ref[:, b0:b0 + HEAD] * LN2).astype(BF16)
            okv_ref[:, b0 + HEAD:b0 + 2 * HEAD] = jnp.zeros((ts, HEAD), BF16)
            dkr = dkr + dk_ref[:, b0 + HEAD:b0 + 2 * HEAD]
        okv_ref[:, wq:] = dv_ref[...].astype(BF16)
        okr_ref[...] = jnp.where(lane < ROPE, unrot(jnp.where(lane < ROPE, dkr * LN2, 0.0)), 0.0)

    return _pcall(body, name="mla_rope_bwd", grid=(s // ts,),
                  in_specs=[_rows(ts, wq), _rows(ts, wq), _rows(ts, nh * HEAD), _rows(ts, 1),
                            _full((1, LANE)), _full((1, LANE))],
                  out_specs=[_rows(ts, wq), _rows(ts, wq + nh * HEAD), _rows(ts, LANE)],
                  out_shape=[SDS((s, wq), BF16), SDS((s, wq + nh * HEAD), BF16), SDS((s, LANE), F32)])(
                      dqc, dkc, dv, pos, invf, sgn)


def _causal_mask(blk):
    i = lax.broadcasted_iota(jnp.int32, (blk, blk), 0)
    j = lax.broadcasted_iota(jnp.int32, (blk, blk), 1)
    return j <= i


def _mla_fwd(qc, kc, v, nh):
    s = qc.shape[0]
    blk = _tile(s, MLA_BLOCK)
    nb = s // blk
    rep = blk // LANE

    def body(q_ref, k_ref, v_ref, o_ref, lse_ref, m_sc, l_sc, acc):
        i = pl.program_id(1)
        q = q_ref[...]
        m_sc[...] = jnp.full_like(m_sc, -1e30)
        l_sc[...] = jnp.zeros_like(l_sc)
        acc[...] = jnp.zeros_like(acc)

        def step(j, masked):
            off = pl.multiple_of(j * blk, blk)
            sc = _dot_nt(q, k_ref[pl.ds(off, blk), :])
            if masked:
                sc = jnp.where(_causal_mask(blk), sc, -1e30)
            m_prev = m_sc[...]
            m_new = jnp.maximum(m_prev, jnp.max(sc, axis=-1, keepdims=True))
            p = jnp.exp2(sc - jnp.tile(m_new, (1, rep)))
            alpha = jnp.exp2(m_prev - m_new)
            l_sc[...] = alpha * l_sc[...] + jnp.sum(p, axis=-1, keepdims=True)
            acc[...] = alpha * acc[...] + _dot(p, v_ref[pl.ds(off, blk), :])
            m_sc[...] = m_new

        def loop_body(j, carry):
            step(j, False)
            return carry

        lax.fori_loop(0, i, loop_body, 0)
        step(i, True)
        o_ref[...] = acc[...] / l_sc[...]
        lse_ref[...] = m_sc[...] + jnp.log(l_sc[...]) * LOG2E

    return pl.pallas_call(
        body, name="mla_fwd", grid=(nh, nb),
        in_specs=[pl.BlockSpec((blk, 2 * HEAD), lambda h, i: (i, h)),
                  pl.BlockSpec((s, 2 * HEAD), lambda h, i: (0, h)),
                  pl.BlockSpec((s, HEAD), lambda h, i: (0, h))],
        out_specs=[pl.BlockSpec((blk, HEAD), lambda h, i: (i, h)),
                   pl.BlockSpec((None, blk, LANE), lambda h, i: (h, i, 0))],
        out_shape=[SDS((s, nh * HEAD), F32), SDS((nh, s, LANE), F32)],
        scratch_shapes=[pltpu.VMEM((blk, LANE), F32), pltpu.VMEM((blk, LANE), F32), pltpu.VMEM((blk, HEAD), F32)],
        compiler_params=pltpu.CompilerParams(dimension_semantics=("arbitrary",) * 2,
                                             vmem_limit_bytes=VMEM_LIMIT))(qc, kc, v)


def _mla_bwd(qc, kc, v, do, lse, delta, nh, rider):
    s = qc.shape[0]
    blk = _tile(s, MLA_BLOCK)
    nb = s // blk
    rep = blk // LANE
    once = pl.Buffered(1)
    r_in, r_out = len(rider.arrays), len(rider.out_shapes)

    def body(*refs):
        q_ref, do_ref, lse_ref, dl_ref, k_ref, v_ref = refs[:6]
        dq_ref, dk_ref, dv_ref = refs[6 + r_in:9 + r_in]
        dk_acc, dv_acc = refs[9 + r_in + r_out:11 + r_in + r_out]
        r_refs = (refs[6:6 + r_in], refs[9 + r_in:9 + r_in + r_out], refs[11 + r_in + r_out:])
        j = pl.program_id(1)
        grid_step = pl.program_id(0) * nb + j
        _ride_begin(rider, r_refs, grid_step)

        @pl.when(j == 0)
        def _():
            dq_ref[...] = jnp.zeros_like(dq_ref)

        dk_acc[...] = jnp.zeros_like(dk_acc)
        dv_acc[...] = jnp.zeros_like(dv_acc)
        kj, vj = k_ref[...], v_ref[...]

        def step(i, masked):
            rows = pl.ds(pl.multiple_of(i * blk, blk), blk)
            qi, doi = q_ref[rows, :], do_ref[rows, :]
            sc = _dot_nt(qi, kj)
            if masked:
                sc = jnp.where(_causal_mask(blk), sc, -1e30)
            p = jnp.exp2(sc - jnp.tile(lse_ref[rows, :], (1, rep)))
            dp = _dot_nt(doi, vj)
            ds = p * (dp - jnp.tile(dl_ref[rows, :], (1, rep)))
            dv_acc[...] += _dot_tn(p, doi)
            dk_acc[...] += _dot_tn(ds, qi)
            dq_ref[rows, :] += _dot(ds, kj)

        def loop_body(i, carry):
            step(i, False)
            return carry

        step(j, True)
        lax.fori_loop(j + 1, nb, loop_body, 0)
        dk_ref[...] = dk_acc[...]
        dv_ref[...] = dv_acc[...]
        _ride_end(rider, r_refs, grid_step, nh * nb)

    outs = _pcall(
        body, name="mla_bwd", grid=(nh, nb),
        in_specs=[pl.BlockSpec((s, 2 * HEAD), lambda h, j: (0, h), pipeline_mode=once),
                  pl.BlockSpec((s, HEAD), lambda h, j: (0, h), pipeline_mode=once),
                  pl.BlockSpec((None, s, LANE), lambda h, j: (h, 0, 0), pipeline_mode=once),
                  pl.BlockSpec((None, s, LANE), lambda h, j: (h, 0, 0), pipeline_mode=once),
                  pl.BlockSpec((blk, 2 * HEAD), lambda h, j: (j, h)),
                  pl.BlockSpec((blk, HEAD), lambda h, j: (j, h))] + [_ANY] * r_in,
        out_specs=[pl.BlockSpec((s, 2 * HEAD), lambda h, j: (0, h), pipeline_mode=once),
                   pl.BlockSpec((blk, 2 * HEAD), lambda h, j: (j, h)),
                   pl.BlockSpec((blk, HEAD), lambda h, j: (j, h))] + [_ANY] * r_out,
        out_shape=[SDS((s, nh * 2 * HEAD), F32), SDS((s, nh * 2 * HEAD), F32),
                   SDS((s, nh * HEAD), F32)] + rider.out_shapes,
        scratch=[pltpu.VMEM((blk, 2 * HEAD), F32), pltpu.VMEM((blk, HEAD), F32)] + rider.scratch)(
            qc, do, lse, delta, kc, v, *rider.arrays)
    return outs[0], outs[1], outs[2], outs[3:]


def _mix_fwd(og, proj, om, gw, mw, nh, z_col):
    s = og.shape[0]
    w = nh * HEAD
    ts = _tile(s, 256, 8)

    def body(og_ref, z_ref, om_ref, gw_ref, mw_ref, o_ref):
        for h in range(nh):
            hs = slice(HEAD * h, HEAD * (h + 1))
            a = og_ref[:, hs]
            r = lax.rsqrt(jnp.mean(a * a, axis=-1, keepdims=True) + EPS)
            o_ref[:, hs] = (a * r * gw_ref[...] * _silu(z_ref[:, hs])).astype(BF16)
            b = om_ref[:, hs]
            r = lax.rsqrt(jnp.mean(b * b, axis=-1, keepdims=True) + EPS)
            o_ref[:, w + HEAD * h:w + HEAD * (h + 1)] = (b * r * mw_ref[...]).astype(BF16)

    return _pcall(body, name="mix_fwd", grid=(s // ts,),
                  in_specs=[_rows(ts, w), _rows(ts, w, z_col), _rows(ts, w), _full((1, HEAD)),
                            _full((1, HEAD))],
                  out_specs=_rows(ts, 2 * w), out_shape=SDS((s, 2 * w), BF16))(og, proj, om, gw, mw)


def _mix_bwd(dmix, og, proj, om, gw, mw, nh, z_col):
    s = og.shape[0]
    w = nh * HEAD
    ts = _tile(s, 256, 8)

    def body(d_ref, og_ref, z_ref, om_ref, gw_ref, mw_ref, dog_ref, dz_ref, dom_ref, dgw_ref, dmw_ref,
             dl_ref):
        @pl.when(pl.program_id(0) == 0)
        def _():
            dgw_ref[...] = jnp.zeros_like(dgw_ref)
            dmw_ref[...] = jnp.zeros_like(dmw_ref)

        dgw = jnp.zeros((1, HEAD), F32)
        dmw = jnp.zeros((1, HEAD), F32)
        for h in range(nh):
            hs = slice(HEAD * h, HEAD * (h + 1))
            a, z, dy = og_ref[:, hs], z_ref[:, hs], d_ref[:, hs]
            r = lax.rsqrt(jnp.mean(a * a, axis=-1, keepdims=True) + EPS)
            ah = a * r
            sz = _silu(z)
            dz_ref[:, hs] = (dy * (ah * gw_ref[...]) * _dsilu(z)).astype(BF16)
            dn = dy * sz
            dgw = dgw + jnp.sum(dn * ah, axis=0, keepdims=True)
            dah = dn * gw_ref[...]
            dog_ref[:, hs] = r * (dah - ah * jnp.mean(dah * ah, axis=-1, keepdims=True))
            b, dyb = om_ref[:, hs], d_ref[:, w + HEAD * h:w + HEAD * (h + 1)]
            r = lax.rsqrt(jnp.mean(b * b, axis=-1, keepdims=True) + EPS)
            bh = b * r
            dmw = dmw + jnp.sum(dyb * bh, axis=0, keepdims=True)
            dbh = dyb * mw_ref[...]
            dom = r * (dbh - bh * jnp.mean(dbh * bh, axis=-1, keepdims=True))
            dom_ref[:, hs] = dom.astype(BF16)
            dl_ref[h] = jnp.broadcast_to(jnp.sum(dom * b, axis=-1, keepdims=True), (ts, LANE))
        dgw_ref[...] += dgw
        dmw_ref[...] += dmw

    return _pcall(body, name="mix_bwd", grid=(s // ts,),
                  in_specs=[_rows(ts, 2 * w), _rows(ts, w), _rows(ts, w, z_col), _rows(ts, w),
                            _full((1, HEAD)), _full((1, HEAD))],
                  out_specs=[_rows(ts, w), _rows(ts, w), _rows(ts, w), _full((1, HEAD)), _full((1, HEAD)),
                             pl.BlockSpec((nh, ts, LANE), lambda i: (0, i, 0))],
                  out_shape=[SDS((s, w), F32), SDS((s, w), BF16), SDS((s, w), BF16), SDS((1, HEAD), F32),
                             SDS((1, HEAD), F32), SDS((nh, s, LANE), F32)])(dmix, og, proj, om, gw, mw)


def _swiglu_fwd(h2, wg, wu):
    m, kdim = h2.shape
    tn = wg.shape[2]
    n = 4 * tn
    tm, tk = _tile(m, 512), _tile(kdim, 2048)
    nk = kdim // tk

    def body(a_ref, g_ref, u_ref, act_ref, go_ref, uo_ref, gacc, uacc):
        k = pl.program_id(2)

        @pl.when(k == 0)
        def _():
            gacc[...] = jnp.zeros_like(gacc)
            uacc[...] = jnp.zeros_like(uacc)

        a = a_ref[...]
        gacc[...] += _dot(a, g_ref[...])
        uacc[...] += _dot(a, u_ref[...])

        @pl.when(k == nk - 1)
        def _():
            g, u = gacc[...], uacc[...]
            act_ref[...] = (_silu(g) * u).astype(BF16)
            go_ref[...] = g.astype(BF16)
            uo_ref[...] = u.astype(BF16)

    a_spec = pl.BlockSpec((tm, tk), lambda i, j, k: (i, k))
    b_spec = pl.BlockSpec((None, tk, tn), lambda i, j, k: (j, k, 0))
    o_spec = pl.BlockSpec((tm, tn), lambda i, j, k: (i, j))
    return _pcall(body, name="swiglu_fwd", grid=(m // tm, n // tn, nk),
                  in_specs=[a_spec, b_spec, b_spec], out_specs=[o_spec] * 3,
                  out_shape=[SDS((m, n), BF16)] * 3,
                  scratch=[pltpu.VMEM((tm, tn), F32), pltpu.VMEM((tm, tn), F32)])(h2, wg, wu)


def _swiglu_bwd(dx3, wd, g, u):
    m, kdim = dx3.shape
    n = wd.shape[0]
    tm, tn, tk = _tile(m, 1024), _tile(n, 512), _tile(kdim, 2048)
    nk = kdim // tk

    def body(a_ref, b_ref, g_ref, u_ref, dg_ref, du_ref, acc):
        k = pl.program_id(2)

        @pl.when(k == 0)
        def _():
            acc[...] = jnp.zeros_like(acc)

        acc[...] += _dot_nt(a_ref[...], b_ref[...])

        @pl.when(k == nk - 1)
        def _():
            da = acc[...]
            gv, uv = g_ref[...].astype(F32), u_ref[...].astype(F32)
            dg_ref[...] = (da * uv * _dsilu(gv)).astype(BF16)
            du_ref[...] = (da * _silu(gv)).astype(BF16)

    a_spec = pl.BlockSpec((tm, tk), lambda i, j, k: (i, k))
    b_spec = pl.BlockSpec((tn, tk), lambda i, j, k: (j, k))
    o_spec = pl.BlockSpec((tm, tn), lambda i, j, k: (i, j))
    return _pcall(body, name="swiglu_bwd", grid=(m // tm, n // tn, nk),
                  in_specs=[a_spec, b_spec, o_spec, o_spec], out_specs=[o_spec] * 2,
                  out_shape=[SDS((m, n), BF16)] * 2,
                  scratch=[pltpu.VMEM((tm, tn), F32)])(dx3, wd, g, u)


def _sum_pair(g, recv, place, name):
    _, _, rh, c = g.shape
    tr = _tile(rh, 256, 16)

    def body(pl_ref, g_ref, r_ref, o16_ref, own_ref):
        sm = g_ref[...].astype(F32) + r_ref[...].astype(F32)
        o16_ref[...] = sm.astype(BF16)

        @pl.when(pl.program_id(1) == pl_ref[1])
        def _():
            own_ref[...] = sm

    grid_spec = pltpu.PrefetchScalarGridSpec(
        num_scalar_prefetch=1, grid=(rh // tr, 4),
        in_specs=[pl.BlockSpec((None, None, tr, c), lambda i, t, p: (t, p[0], i, 0)),
                  pl.BlockSpec((None, tr, c), lambda i, t, p: (t, i, 0))],
        out_specs=[pl.BlockSpec((None, tr, c), lambda i, t, p: (t, i, 0)),
                   pl.BlockSpec((tr, c), lambda i, t, p: (i, 0))])
    return pl.pallas_call(
        body, name=name, grid_spec=grid_spec,
        out_shape=[SDS((4, rh, c), BF16), SDS((rh, c), F32)],
        compiler_params=pltpu.CompilerParams(dimension_semantics=("arbitrary",) * 2,
                                             vmem_limit_bytes=VMEM_LIMIT))(place, g, recv)


def _sum_chips(own, recv, name):
    rh, c = own.shape
    tr = _tile(rh, 256, 16)

    def body(o_ref, r_ref, out_ref):
        acc = o_ref[...]
        for j in range(3):
            acc = acc + r_ref[j].astype(F32)
        out_ref[...] = acc

    return _pcall(body, name=name, grid=(rh // tr,),
                  in_specs=[_rows(tr, c), pl.BlockSpec((3, tr, c), lambda i: (0, i, 0))],
                  out_specs=_rows(tr, c), out_shape=SDS(own.shape, F32))(own, recv)


def _adamw_update(wv, gv, mv, vv):
    mn = ADAM_B1 * mv + (1.0 - ADAM_B1) * gv
    vn = ADAM_B2 * vv + (1.0 - ADAM_B2) * (gv * gv)
    m_hat = mn / (1.0 - ADAM_B1 ** ADAM_STEP)
    v_hat = vn / (1.0 - ADAM_B2 ** ADAM_STEP)
    return -ADAM_LR * (m_hat / (jnp.sqrt(v_hat) + ADAM_EPS) + ADAM_WD * wv), mn, vn


def _adamw(w, g, m, v, name):
    r, c = w.shape
    tr = _tile(r, 256, 8)

    def body(w_ref, g_ref, m_ref, v_ref, d_ref, mo_ref, vo_ref):
        d_ref[...], mo_ref[...], vo_ref[...] = _adamw_update(w_ref[...], g_ref[...], m_ref[...], v_ref[...])

    spec = _rows(tr, c)
    return _pcall(body, name=name, grid=(r // tr,), in_specs=[spec] * 4, out_specs=[spec] * 3,
                  out_shape=[SDS(w.shape, F32)] * 3)(w, g, m, v)


def _adamw_halves(w, mine, theirs, m, v, place, name):
    r, c = w.shape
    rh = r // 2
    tr = _tile(rh, 256, 8)
    nt = rh // tr

    def body(p_ref, w_ref, a_ref, b_ref, m_ref, v_ref, g_ref, d_ref, mo_ref, vo_ref):
        gv = jnp.where(pl.program_id(0) // nt == p_ref[0], a_ref[...], b_ref[...])
        g_ref[...] = gv
        d_ref[...], mo_ref[...], vo_ref[...] = _adamw_update(w_ref[...], gv, m_ref[...], v_ref[...])

    full = pl.BlockSpec((tr, c), lambda i, p: (i, 0))
    half = pl.BlockSpec((tr, c), lambda i, p: (i % nt, 0))
    grid_spec = pltpu.PrefetchScalarGridSpec(num_scalar_prefetch=1, grid=(2 * nt,),
                                             in_specs=[full, half, half, full, full], out_specs=[full] * 4)
    return pl.pallas_call(
        body, name=name, grid_spec=grid_spec, out_shape=[SDS(w.shape, F32)] * 4,
        compiler_params=pltpu.CompilerParams(dimension_semantics=("arbitrary",),
                                             vmem_limit_bytes=VMEM_LIMIT))(place, w, mine, theirs, m, v)


def _place():
    x, y, c = lax.axis_index("x"), lax.axis_index("y"), lax.axis_index("c")
    chips = [(1 - x, y), (x, 1 - y), (1 - x, 1 - y)]
    return x, y, c, chips


_ANY = pl.BlockSpec(memory_space=pl.ANY)


def _remote(src, dst, sems, k, to):
    return pltpu.make_async_remote_copy(src_ref=src, dst_ref=dst, send_sem=sems[0].at[k], recv_sem=sems[1].at[k],
                                        device_id=to, device_id_type=MESH)


class _Gather:
    def __init__(self, shards):
        n = len(shards)
        self.arrays = list(shards)
        self.out_shapes = [SDS((4,) + a.shape, a.dtype) for a in shards]
        self.scratch = [pltpu.SemaphoreType.DMA((7 * n,)), pltpu.SemaphoreType.DMA((7 * n,))]

    def _plan(self, ins, outs, sems):
        x, y, c, chips = _place()
        own, sib = 2 * x + y, (x, y, 1 - c)
        plan = []
        for wi, (w, o) in enumerate(zip(ins, outs)):
            rh = w.shape[0] // 2
            mine, theirs = pl.ds(c * rh, rh), pl.ds((1 - c) * rh, rh)
            whole = _remote(w, o.at[own], sems, 7 * wi + 6, sib)
            ici, d2d, d2d_in = [], [], []
            for j, (tx, ty) in enumerate(chips):
                t = 2 * tx + ty
                ici.append(_remote(w.at[mine], o.at[own, mine], sems, 7 * wi + j, (tx, ty, c)))
                d2d.append(_remote(o.at[t, mine], o.at[t, mine], sems, 7 * wi + 3 + j, sib))
                d2d_in.append(_remote(o.at[t, theirs], o.at[t, theirs], sems, 7 * wi + 3 + j, sib))
            plan.append((whole, ici, d2d, d2d_in))
        return plan

    def begin(self, ins, outs, sems):
        for whole, ici, _, _ in self._plan(ins, outs, sems):
            whole.start()
            for cp in ici:
                cp.start()

    def middle(self, ins, outs, sems):
        for _, ici, d2d, _ in self._plan(ins, outs, sems):
            for cp_in, cp_on in zip(ici, d2d):
                cp_in.wait_recv()
                cp_on.start()

    def finish(self, ins, outs, sems):
        for whole, ici, d2d, d2d_in in self._plan(ins, outs, sems):
            for cp in d2d_in:
                cp.wait_recv()
            for cp in ici + d2d:
                cp.wait_send()
            whole.wait()


class _Swap:
    def __init__(self, grads):
        n = len(grads)
        self.arrays = list(grads)
        self.out_shapes = [SDS((4,) + g.shape[2:], g.dtype) for g in grads]
        self.scratch = [pltpu.SemaphoreType.DMA((4 * n,)), pltpu.SemaphoreType.DMA((4 * n,))]

    def _plan(self, ins, outs, sems):
        x, y, c, _ = _place()
        return [_remote(g.at[t, 1 - c], o.at[t], sems, 4 * wi + t, (x, y, 1 - c))
                for wi, (g, o) in enumerate(zip(ins, outs)) for t in range(4)]

    def begin(self, ins, outs, sems):
        for cp in self._plan(ins, outs, sems):
            cp.start()

    def middle(self, ins, outs, sems):
        pass

    def finish(self, ins, outs, sems):
        for cp in self._plan(ins, outs, sems):
            cp.wait()


class _Exchange:
    def __init__(self, pieces):
        n = len(pieces)
        self.arrays = list(pieces)
        self.out_shapes = [SDS((3,) + p.shape[1:], p.dtype) for p in pieces]
        self.scratch = [pltpu.SemaphoreType.DMA((3 * n,)), pltpu.SemaphoreType.DMA((3 * n,))]

    def _plan(self, ins, outs, sems):
        x, y, c, chips = _place()
        return [_remote(g.at[2 * tx + ty], o.at[j], sems, 3 * wi + j, (tx, ty, c))
                for wi, (g, o) in enumerate(zip(ins, outs)) for j, (tx, ty) in enumerate(chips)]

    def begin(self, ins, outs, sems):
        for cp in self._plan(ins, outs, sems):
            cp.start()

    def middle(self, ins, outs, sems):
        pass

    def finish(self, ins, outs, sems):
        for cp in self._plan(ins, outs, sems):
            cp.wait()


class _Share:
    def __init__(self, totals):
        n = len(totals)
        self.arrays = list(totals)
        self.out_shapes = [SDS(t.shape, t.dtype) for t in totals]
        self.scratch = [pltpu.SemaphoreType.DMA((n,)), pltpu.SemaphoreType.DMA((n,))]

    def _plan(self, ins, outs, sems):
        x, y, c, _ = _place()
        return [_remote(t, o, sems, wi, (x, y, 1 - c)) for wi, (t, o) in enumerate(zip(ins, outs))]

    def begin(self, ins, outs, sems):
        for cp in self._plan(ins, outs, sems):
            cp.start()

    def middle(self, ins, outs, sems):
        pass

    def finish(self, ins, outs, sems):
        for cp in self._plan(ins, outs, sems):
            cp.wait()


def _ride_begin(rider, r_refs, step):
    @pl.when(step == 0)
    def _():
        rider.begin(*r_refs)


def _ride_end(rider, r_refs, step, nsteps):
    @pl.when(step == min(3 * nsteps // 4, nsteps - 1))
    def _():
        rider.middle(*r_refs)

    @pl.when(step == nsteps - 1)
    def _():
        rider.finish(*r_refs)


def _comm(rider, name):
    n_in, n_out = len(rider.arrays), len(rider.out_shapes)

    def body(*refs):
        r_refs = (refs[:n_in], refs[n_in:n_in + n_out], refs[n_in + n_out:])
        rider.begin(*r_refs)
        rider.middle(*r_refs)
        rider.finish(*r_refs)

    return pl.pallas_call(body, name=name, out_shape=rider.out_shapes, in_specs=[_ANY] * n_in,
                          out_specs=[_ANY] * n_out, scratch_shapes=rider.scratch)(*rider.arrays)


def _small_allreduce(pk, name):
    r = pk.shape[0]
    rels = [(dx, dy, dc) for dx in (0, 1) for dy in (0, 1) for dc in (0, 1) if dx or dy or dc]

    def body(p_ref, o_ref, buf, send_sems, recv_sems):
        x, y, c, _ = _place()
        me = 4 * x + 2 * y + c
        buf[me] = p_ref[...]
        cps = []
        for k, (dx, dy, dc) in enumerate(rels):
            to = (1 - x if dx else x, 1 - y if dy else y, 1 - c if dc else c)
            cps.append(pltpu.make_async_remote_copy(src_ref=p_ref, dst_ref=buf.at[me], send_sem=send_sems.at[k],
                                                    recv_sem=recv_sems.at[k], device_id=to,
                                                    device_id_type=MESH))
        for cpy in cps:
            cpy.start()
        for cpy in cps:
            cpy.wait()
        acc = buf[0]
        for d in range(1, 8):
            acc = acc + buf[d]
        o_ref[...] = acc

    vm = pl.BlockSpec(memory_space=pltpu.VMEM)
    return pl.pallas_call(body, name=name, out_shape=SDS(pk.shape, F32), in_specs=[vm], out_specs=vm,
                          scratch_shapes=[pltpu.VMEM((8, r, LANE), F32), pltpu.SemaphoreType.DMA((7,)),
                                          pltpu.SemaphoreType.DMA((7,))])(pk)


ATTN_W = ("w_in", "w_uq", "w_ukv", "w_out")
FFN_W = ("w_gate", "w_up", "w_down")
BIG = ATTN_W + FFN_W


def _cols_from_chips(g):
    return jnp.concatenate([g[t] for t in range(4)], axis=1)


def _cols_to_chips(full):
    r, n = full.shape
    return full.reshape(r, 4, n // 4).transpose(1, 0, 2).reshape(4, 2, r // 2, n // 4)


def _rows_to_chips(full):
    n, c = full.shape
    return full.reshape(4, 2, n // 8, c)


def _permute_w_in(w, nh):
    d = w.shape[0]
    g = 4 * nh * HEAD
    lr = (w.shape[1] - g - 2 * nh - ROPE) // 2
    o = g + 2 * nh
    pad = jnp.zeros((d, LANE - ROPE - 8 - nh), w.dtype)
    pad8 = jnp.zeros((d, 8 - nh), w.dtype)
    return jnp.concatenate([w[:, :g], w[:, o:o + 2 * lr], w[:, o + 2 * lr:], w[:, g:g + nh], pad8,
                            w[:, g + nh:g + 2 * nh], pad, jnp.zeros((d, LANE), w.dtype)], axis=1)


def _unpermute_w_in(wp, nh, lr):
    g = 4 * nh * HEAD
    mc = g + 2 * lr
    return jnp.concatenate([wp[:, :g], wp[:, mc + B_LANE:mc + B_LANE + nh], wp[:, mc + A_LANE:mc + A_LANE + nh],
                            wp[:, g:g + 2 * lr], wp[:, mc:mc + ROPE]], axis=1)


def _permute_w_uq(w, nh):
    lr = w.shape[0]
    w3 = w.reshape(lr, nh, HEAD + ROPE)
    return jnp.concatenate([w3, jnp.zeros((lr, nh, HEAD - ROPE), w.dtype)], axis=2).reshape(lr, nh * 2 * HEAD)


def _unpermute_w_uq(wp, nh):
    lr = wp.shape[0]
    return wp.reshape(lr, nh, 2 * HEAD)[:, :, :HEAD + ROPE].reshape(lr, nh * (HEAD + ROPE))


def _permute_w_ukv(w, nh):
    lr = w.shape[0]
    w3 = w.reshape(lr, nh, 2 * HEAD)
    kp = jnp.concatenate([w3[:, :, :HEAD], jnp.zeros((lr, nh, HEAD), w.dtype)], axis=2)
    return jnp.concatenate([kp.reshape(lr, nh * 2 * HEAD), w3[:, :, HEAD:].reshape(lr, nh * HEAD)], axis=1)


def _unpermute_w_ukv(wp, nh):
    lr = wp.shape[0]
    kp = wp[:, :nh * 2 * HEAD].reshape(lr, nh, 2 * HEAD)[:, :, :HEAD]
    vp = wp[:, nh * 2 * HEAD:].reshape(lr, nh, HEAD)
    return jnp.concatenate([kp, vp], axis=2).reshape(lr, nh * 2 * HEAD)


def _reduce_begin(grads, place, tag):
    recv = _comm(_Swap(grads), "swap_" + tag)
    sums = [_sum_pair(g, r, place, "sum_pair_%s%d" % (tag, k)) for k, (g, r) in enumerate(zip(grads, recv))]
    return [s[0] for s in sums], [s[1] for s in sums]


def _reduce_end(own, recv, tag):
    return [_sum_chips(o, r, "sum_chips_%s%d" % (tag, k)) for k, (o, r) in enumerate(zip(own, recv))]


def _step(x, pos, tgt, wt, ffn_shards, small, place):
    nh = small["a_log"].shape[1]
    lr = small["q_norm_w"].shape[1]
    w = nh * HEAD
    z_col, col_q, col_kv = 3, 4 * w // lr, 4 * w // lr + 1
    misc_c = 4 * w + 2 * lr
    misc_col = misc_c // LANE
    assert (4 * w) % lr == 0 and small["kv_norm_w"].shape[1] == lr

    win_p = _permute_w_in(wt["w_in"], nh)
    wuq_p = _permute_w_uq(wt["w_uq"], nh)
    wukv_p = _permute_w_ukv(wt["w_ukv"], nh)
    zl = jnp.zeros((1, LANE), F32)
    alog_l = zl.at[:, A_LANE:A_LANE + nh].set(small["a_log"])
    dtb_l = zl.at[:, A_LANE:A_LANE + nh].set(small["dt_bias"])
    conv_w = small["conv_w"]

    h1 = _norm_fwd(x, small["attn_norm_w"], "norm1")
    proj = _mm([(h1, win_p)], name="proj_in")
    gq, gk, gv, gb, gbt = _gdn_prep(proj, conv_w, alog_l, dtb_l, nh, misc_col)
    o_gdn, states, (wg4, wu4, wd4) = _gdn_fwd(gq, gk, gv, gb, gbt, nh, _Gather(ffn_shards))
    w_down = wd4.reshape(-1, wd4.shape[2])
    cqn, ckvn = _mla_norm(proj, small["q_norm_w"], small["kv_norm_w"], col_q, col_kv)
    qraw = _mm([(cqn, wuq_p)], name="proj_uq")
    kvraw = _mm([(ckvn, wukv_p)], name="proj_ukv")
    qc, kc, vv = _mla_rope(qraw, kvraw, proj, pos, nh, misc_col)
    o_mla, lse = _mla_fwd(qc, kc, vv, nh)
    mixed = _mix_fwd(o_gdn, proj, o_mla, small["gdn_norm_w"], small["mla_out_norm_w"], nh, z_col)
    x2 = _mm([(mixed, wt["w_out"])], name="proj_out", res=x)
    h2 = _norm_fwd(x2, small["ffn_norm_w"], "norm2")
    act, gpre, upre = _swiglu_fwd(h2, wg4, wu4)
    x3 = _mm([(act, w_down)], name="proj_down", res=x2)
    dx3, d_final, loss, dx3h = _final_loss(x3, tgt, small["final_norm_w"])

    gs = {"final_norm_w": d_final}
    dgate, dup = _swiglu_bwd(dx3h, w_down, gpre, upre)
    g_down = _rows_to_chips(_mm([(act, dx3h)], name="dw_down", ta=True, out_dtype=BF16))
    dh2 = _mm([(dgate, wg4), (dup, wu4)], name="dh2", tb=True, b_chips=True)
    g_gate = _mm([(h2, dgate)], name="dw_gate", ta=True, out_dtype=BF16, out_chips=True)
    g_up = _mm([(h2, dup)], name="dw_up", ta=True, out_dtype=BF16, out_chips=True)
    halves = lambda g: g.reshape(4, 2, g.shape[1] // 2, g.shape[2])
    ffn16, ffn_own = _reduce_begin([halves(g_gate), halves(g_up), g_down], place, "ffn")
    dx2, gs["ffn_norm_w"], dx2h = _norm_bwd(dh2, x2, small["ffn_norm_w"], dx3, "norm2_bwd", True)
    dmix = _mm([(dx2h, wt["w_out"])], name="dmix", tb=True)
    g_out = _rows_to_chips(_mm([(mixed, dx2h)], name="dw_out", ta=True, out_dtype=BF16))
    d_ogdn, dz, d_omla, gs["gdn_norm_w"], gs["mla_out_norm_w"], delta = _mix_bwd(
        dmix, o_gdn, proj, o_mla, small["gdn_norm_w"], small["mla_out_norm_w"], nh, z_col)
    dqc, dkc, dvv, ffn_recv = _mla_bwd(qc, kc, vv, d_omla, lse, delta, nh, _Exchange(ffn16))
    ffn_tot = _reduce_end(ffn_own, ffn_recv, "ffn")
    dqraw, dkvraw, dkr = _mla_rope_bwd(dqc, dkc, dvv, pos, nh)
    dcqn = _mm([(dqraw, wuq_p)], name="dcqn", tb=True)
    dckvn = _mm([(dkvraw, wukv_p)], name="dckvn", tb=True)
    g_uq = _cols_to_chips(_unpermute_w_uq(_mm([(cqn, dqraw)], name="dw_uq", ta=True, out_dtype=BF16), nh))
    g_ukv = _cols_to_chips(_unpermute_w_ukv(_mm([(ckvn, dkvraw)], name="dw_ukv", ta=True, out_dtype=BF16), nh))
    dcq, dckv, gs["q_norm_w"], gs["kv_norm_w"] = _mla_norm_bwd(
        proj, small["q_norm_w"], small["kv_norm_w"], dcqn, dckvn, col_q, col_kv)
    dgq, dgk, dgv, dgb = _gdn_bwd(gq, gk, gv, gb, gbt, states, d_ogdn, nh)
    dconv, dmisc, gs["conv_w"], dal, ddb = _gdn_prep_bwd(
        proj, conv_w, alog_l, dtb_l, dgq, dgk, dgv, dgb, dkr, nh, misc_col)
    gs["a_log"] = dal[:, A_LANE:A_LANE + nh]
    gs["dt_bias"] = ddb[:, A_LANE:A_LANE + nh]
    dqkv = _conv_bwd_input(dconv, conv_w)
    dproj = jnp.concatenate([dqkv, dz, dcq, dckv, dmisc, jnp.zeros((x.shape[0], LANE), BF16)], axis=1)
    g_in = _cols_to_chips(_unpermute_w_in(_mm([(h1, dproj)], name="dw_in", ta=True, out_dtype=BF16), nh, lr))
    att16, att_own = _reduce_begin([g_in, g_uq, g_ukv, g_out], place, "att")
    dh1, att_recv = _mm([(dproj, win_p)], name="dh1", tb=True, rider=_Exchange(att16))
    att_tot = _reduce_end(att_own, att_recv, "att")
    grad_x, gs["attn_norm_w"] = _norm_bwd(dh1, x, small["attn_norm_w"], dx2, "norm1_bwd", False)
    return loss, grad_x, att_tot + ffn_tot, gs


SMALL = ("attn_norm_w", "ffn_norm_w", "final_norm_w", "q_norm_w", "kv_norm_w", "gdn_norm_w",
         "mla_out_norm_w", "a_log", "dt_bias")
WEIGHTS = ("attn_norm_w", "w_in", "conv_w", "a_log", "dt_bias", "gdn_norm_w", "q_norm_w", "w_uq",
           "kv_norm_w", "w_ukv", "mla_out_norm_w", "w_out", "ffn_norm_w", "w_gate", "w_up", "w_down",
           "final_norm_w")


def _pack_small(vecs):
    flat = jnp.concatenate([v.astype(F32).reshape(-1) for v in vecs])
    pad = (-flat.shape[0]) % (8 * LANE)
    return jnp.concatenate([flat, jnp.zeros((pad,), F32)]).reshape(-1, LANE)


def kernel(x, positions, attn_norm_w, w_in, conv_w, a_log, dt_bias, gdn_norm_w, q_norm_w, w_uq, kv_norm_w, w_ukv, mla_out_norm_w, w_out, ffn_norm_w, w_gate, w_up, w_down, final_norm_w, loss_target, m_attn_norm_w, m_w_in, m_conv_w, m_a_log, m_dt_bias, m_gdn_norm_w, m_q_norm_w, m_w_uq, m_kv_norm_w, m_w_ukv, m_mla_out_norm_w, m_w_out, m_ffn_norm_w, m_w_gate, m_w_up, m_w_down, m_final_norm_w, v_attn_norm_w, v_w_in, v_conv_w, v_a_log, v_dt_bias, v_gdn_norm_w, v_q_norm_w, v_w_uq, v_kv_norm_w, v_w_ukv, v_mla_out_norm_w, v_w_out, v_ffn_norm_w, v_w_gate, v_w_up, v_w_down, v_final_norm_w):
    args = dict(locals())
    xi, yi, ci = lax.axis_index("x"), lax.axis_index("y"), lax.axis_index("c")
    chip = 2 * xi + yi

    def two_d(a):
        return a.reshape(a.shape[-2:]) if a.ndim >= 2 else a.reshape(1, -1)

    wloc = {n: two_d(args[n]) for n in WEIGHTS}
    mloc = {n: two_d(args["m_" + n]) for n in WEIGHTS}
    vloc = {n: two_d(args["v_" + n]) for n in WEIGHTS}

    ga = _comm(_Gather([wloc[n].astype(BF16) for n in ATTN_W]), "gather_attn")
    wt = {"w_in": _cols_from_chips(ga[0]), "w_uq": _cols_from_chips(ga[1]), "w_ukv": _cols_from_chips(ga[2]),
          "w_out": ga[3].reshape(-1, ga[3].shape[2])}
    cw = wloc["conv_w"]
    cshard = cw.shape[1]
    cfull = jnp.zeros((CONV, 4 * cshard), F32)
    cfull = lax.dynamic_update_slice(cfull, jnp.where(ci == 0, cw, 0.0), (0, chip * cshard))
    conv_full = _small_allreduce(_pack_small([cfull]), "gather_conv_w").reshape(-1)[:CONV * 4 * cshard]
    conv_full = conv_full.reshape(CONV, 4 * cshard)

    small = {n: wloc[n] for n in SMALL}
    small["conv_w"] = conv_full

    pos = positions.reshape(-1, 1).astype(F32)
    place = jnp.stack([ci, chip]).astype(jnp.int32)
    loss, grad_x, totals, gs = _step(two_d(x), pos, two_d(loss_target), wt,
                                     [wloc[n].astype(BF16) for n in FFN_W], small, place)
    from_sib = _comm(_Share(totals), "share_halves")

    small_names = SMALL + ("conv_w",)
    pk = _pack_small([gs[n] for n in small_names] + [loss])
    red = _small_allreduce(pk, "reduce_small").reshape(-1)
    gsm, off = {}, 0
    for n in small_names:
        shp = gs[n].shape
        gsm[n] = red[off:off + shp[0] * shp[1]].reshape(shp)
        off += shp[0] * shp[1]
    loss_out = red[off]
    gsm["conv_w"] = lax.dynamic_slice(gsm["conv_w"], (0, chip * cshard), (CONV, cshard))

    grads, deltas, new_m, new_v = {}, {}, {}, {}
    for n, mine, theirs in zip(BIG, totals, from_sib):
        grads[n], deltas[n], new_m[n], new_v[n] = _adamw_halves(wloc[n], mine, theirs, mloc[n], vloc[n], place,
                                                                "adamw_" + n)
    grads["conv_w"] = gsm["conv_w"]
    deltas["conv_w"], new_m["conv_w"], new_v["conv_w"] = _adamw(wloc["conv_w"], gsm["conv_w"], mloc["conv_w"],
                                                                vloc["conv_w"], "adamw_conv_w")
    sm_shapes = [wloc[n].shape for n in SMALL]
    pd, pm, pv = _adamw(_pack_small([wloc[n] for n in SMALL]), _pack_small([gsm[n] for n in SMALL]),
                        _pack_small([mloc[n] for n in SMALL]), _pack_small([vloc[n] for n in SMALL]),
                        "adamw_small")
    for dst, packed in ((deltas, pd), (new_m, pm), (new_v, pv)):
        flat, off = packed.reshape(-1), 0
        for n, shp in zip(SMALL, sm_shapes):
            dst[n] = flat[off:off + shp[0] * shp[1]].reshape(shp)
            off += shp[0] * shp[1]
    for n in SMALL:
        grads[n] = gsm[n]

    def like(n, a):
        return a.reshape(args[n].shape)

    outs = [loss_out.reshape(()), grad_x.reshape(x.shape)]
    for group in (grads, deltas, new_m, new_v):
        outs += [like(n, group[n]) for n in WEIGHTS]
    return tuple(outs)
```

```python
import functools

import jax
import jax.numpy as jnp
from jax import lax
from jax.experimental import pallas as pl
from jax.experimental.pallas import tpu as pltpu

F32, BF16 = jnp.float32, jnp.bfloat16
SDS = jax.ShapeDtypeStruct
MESH = pl.DeviceIdType.MESH

HEAD = 128
ROPE = 64
CHUNK = 64
PAIR = 2 * CHUNK
CONV = 4
EPS = 1e-6
ROPE_THETA = 10000.0
LANE = 128
B_LANE = 64
A_LANE = 72
VMEM_LIMIT = 48 * 1024 * 1024
MLA_BLOCK = 512
LOG2E = 1.4426950408889634
LN2 = 0.6931471805599453
SM_SCALE = (HEAD + ROPE) ** -0.5

ADAM_LR = 0.001
ADAM_B1 = 0.9
ADAM_B2 = 0.999
ADAM_EPS = 1e-08
ADAM_WD = 0.01
ADAM_STEP = 10


def _tile(n, pref, mult=LANE):
    if n <= pref:
        return n
    t = (pref // mult) * mult
    while t >= mult:
        if n % t == 0:
            return t
        t -= mult
    return n


def _pcall(body, *, name, grid, in_specs, out_specs, out_shape, scratch=()):
    return pl.pallas_call(
        body, name=name, grid=grid, in_specs=in_specs, out_specs=out_specs,
        out_shape=out_shape, scratch_shapes=list(scratch),
        compiler_params=pltpu.CompilerParams(
            dimension_semantics=("arbitrary",) * len(grid), vmem_limit_bytes=VMEM_LIMIT))


def _rows(ts, width, col=0):
    return pl.BlockSpec((ts, width), lambda i: (i, col))


def _full(shape):
    nd = len(shape)
    return pl.BlockSpec(shape, lambda i: (0,) * nd)


def _dot(a, b):
    return jnp.dot(a.astype(BF16), b.astype(BF16), preferred_element_type=F32)


def _dot_nt(a, b):
    return lax.dot_general(a.astype(BF16), b.astype(BF16), (((1,), (1,)), ((), ())),
                           preferred_element_type=F32)


def _dot_tn(a, b):
    return lax.dot_general(a.astype(BF16), b.astype(BF16), (((0,), (0,)), ((), ())),
                           preferred_element_type=F32)


def _sigmoid(x):
    return 1.0 / (1.0 + jnp.exp(-x))


def _silu(x):
    return x * _sigmoid(x)


def _dsilu(x):
    s = _sigmoid(x)
    return s * (1.0 + x * (1.0 - s))


def _lane_iota(shape):
    return lax.broadcasted_iota(jnp.int32, shape, len(shape) - 1)


def _col(block, idx):
    return jnp.sum(jnp.where(_lane_iota(block.shape) == idx, block, 0.0), axis=-1, keepdims=True)


def _mm(pairs, *, name, ta=False, tb=False, out_dtype=F32, res=None, tm=1024, tn=1024, tk=2048,
        b_chips=False, out_chips=False, rider=None):
    a0, b0 = pairs[0]
    if ta:
        kdim, m = a0.shape
    else:
        m, kdim = a0.shape
    if b_chips and tb:
        n, tk = b0.shape[1], b0.shape[2]
        assert kdim == 4 * tk
    elif b_chips:
        n, tn = 4 * b0.shape[2], b0.shape[2]
        assert kdim == b0.shape[1]
    else:
        n = b0.shape[0] if tb else b0.shape[1]
    if out_chips:
        tn = n // 4
    tm = _tile(m, tm)
    tn = tn if (out_chips or (b_chips and not tb)) else _tile(n, tn)
    tk = tk if (b_chips and tb) else _tile(kdim, tk)
    assert m % tm == 0 and n % tn == 0 and kdim % tk == 0
    nk, npair = kdim // tk, len(pairs)
    grid = (m // tm, n // tn, nk)
    dims = (((0 if ta else 1,), (1 if tb else 0,)), ((), ()))
    n_in = 2 * npair + (res is not None)
    r_in, r_out = (len(rider.arrays), len(rider.out_shapes)) if rider else (0, 0)

    def body(*refs):
        o_ref = refs[n_in + r_in]
        acc = refs[n_in + r_in + 1 + r_out]
        k = pl.program_id(2)
        if rider:
            r_refs = (refs[n_in:n_in + r_in], refs[n_in + r_in + 1:n_in + r_in + 1 + r_out],
                      refs[n_in + r_in + 2 + r_out:])
            step = (pl.program_id(0) * grid[1] + pl.program_id(1)) * nk + k
            _ride_begin(rider, r_refs, step)

        @pl.when(k == 0)
        def _():
            acc[...] = jnp.zeros_like(acc)

        tot = None
        for p in range(npair):
            d = lax.dot_general(refs[2 * p][...].astype(BF16), refs[2 * p + 1][...].astype(BF16),
                                dims, preferred_element_type=F32)
            tot = d if tot is None else tot + d
        acc[...] += tot

        @pl.when(k == nk - 1)
        def _():
            r = acc[...]
            if res is not None:
                r = r + refs[2 * npair][...]
            o_ref[...] = r.astype(out_dtype)

        if rider:
            _ride_end(rider, r_refs, step, grid[0] * grid[1] * nk)

    if ta:
        a_spec = pl.BlockSpec((tk, tm), lambda i, j, k: (k, i))
    else:
        a_spec = pl.BlockSpec((tm, tk), lambda i, j, k: (i, k))
    if b_chips and tb:
        b_spec = pl.BlockSpec((None, tn, tk), lambda i, j, k: (k, j, 0))
    elif b_chips:
        b_spec = pl.BlockSpec((None, tk, tn), lambda i, j, k: (j, k, 0))
    elif tb:
        b_spec = pl.BlockSpec((tn, tk), lambda i, j, k: (j, k))
    else:
        b_spec = pl.BlockSpec((tk, tn), lambda i, j, k: (k, j))
    if out_chips:
        o_spec = pl.BlockSpec((None, tm, tn), lambda i, j, k: (j, i, 0))
        o_shape = SDS((4, m, tn), out_dtype)
    else:
        o_spec = pl.BlockSpec((tm, tn), lambda i, j, k: (i, j))
        o_shape = SDS((m, n), out_dtype)
    in_specs, args = [], []
    for a, b in pairs:
        in_specs += [a_spec, b_spec]
        args += [a, b]
    if res is not None:
        in_specs.append(o_spec)
        args.append(res)
    out_specs, out_shapes, scratch = [o_spec], [o_shape], [pltpu.VMEM((tm, tn), F32)]
    if rider:
        in_specs += [_ANY] * r_in
        args += rider.arrays
        out_specs += [_ANY] * r_out
        out_shapes += rider.out_shapes
        scratch += rider.scratch
    outs = _pcall(body, name=name, grid=grid, in_specs=in_specs, out_specs=out_specs, out_shape=out_shapes,
                  scratch=scratch)(*args)
    return (outs[0], outs[1:]) if rider else outs[0]


def _norm_fwd(x, w, name):
    s, d = x.shape
    ts = _tile(s, 512, 8)

    def body(x_ref, w_ref, h_ref):
        xv = x_ref[...]
        r = lax.rsqrt(jnp.mean(xv * xv, axis=-1, keepdims=True) + EPS)
        h_ref[...] = (xv * r * w_ref[...]).astype(BF16)

    return _pcall(body, name=name, grid=(s // ts,), in_specs=[_rows(ts, d), _full((1, d))],
                  out_specs=_rows(ts, d), out_shape=SDS((s, d), BF16))(x, w)


def _norm_bwd(dh, x, w, dres, name, with_bf16):
    s, d = x.shape
    ts = _tile(s, 256, 8)

    def body(dh_ref, x_ref, w_ref, dres_ref, dx_ref, dw_ref, *dx16_ref):
        @pl.when(pl.program_id(0) == 0)
        def _():
            dw_ref[...] = jnp.zeros_like(dw_ref)

        xv, dhv = x_ref[...], dh_ref[...]
        r = lax.rsqrt(jnp.mean(xv * xv, axis=-1, keepdims=True) + EPS)
        xh = xv * r
        dw_ref[...] += jnp.sum(dhv * xh, axis=0, keepdims=True)
        dxh = dhv * w_ref[...]
        dx = dres_ref[...] + r * (dxh - xh * jnp.mean(dxh * xh, axis=-1, keepdims=True))
        dx_ref[...] = dx
        for ref in dx16_ref:
            ref[...] = dx.astype(BF16)

    extra = 1 if with_bf16 else 0
    return _pcall(body, name=name, grid=(s // ts,),
                  in_specs=[_rows(ts, d), _rows(ts, d), _full((1, d)), _rows(ts, d)],
                  out_specs=[_rows(ts, d), _full((1, d))] + [_rows(ts, d)] * extra,
                  out_shape=[SDS((s, d), F32), SDS((1, d), F32)] + [SDS((s, d), BF16)] * extra)(
                      dh, x, w, dres)


def _final_loss(x3, tgt, w):
    s, d = x3.shape
    ts = _tile(s, 256, 8)

    def body(x_ref, t_ref, w_ref, dx_ref, dw_ref, loss_ref, dx16_ref):
        @pl.when(pl.program_id(0) == 0)
        def _():
            dw_ref[...] = jnp.zeros_like(dw_ref)
            loss_ref[...] = jnp.zeros_like(loss_ref)

        xv, wv = x_ref[...], w_ref[...]
        r = lax.rsqrt(jnp.mean(xv * xv, axis=-1, keepdims=True) + EPS)
        xh = xv * r
        err = xh * wv - t_ref[...]
        row = jnp.mean(err * err, axis=-1, keepdims=True)
        loss_ref[...] += 0.5 * jnp.sum(row, axis=0, keepdims=True)
        dy = err * (1.0 / d)
        dw_ref[...] += jnp.sum(dy * xh, axis=0, keepdims=True)
        dxh = dy * wv
        dx = r * (dxh - xh * jnp.mean(dxh * xh, axis=-1, keepdims=True))
        dx_ref[...] = dx
        dx16_ref[...] = dx.astype(BF16)

    return _pcall(body, name="final_loss", grid=(s // ts,),
                  in_specs=[_rows(ts, d), _rows(ts, d), _full((1, d))],
                  out_specs=[_rows(ts, d), _full((1, d)), _full((1, 1)), _rows(ts, d)],
                  out_shape=[SDS((s, d), F32), SDS((1, d), F32), SDS((1, 1), F32), SDS((s, d), BF16)])(
                      x3, tgt, w)


def _shift_down(cur, halo, s):
    if s == 0:
        return cur
    row8 = lax.broadcasted_iota(jnp.int32, halo.shape, 0)
    r = pltpu.roll(cur, s, 0)
    top = jnp.where(row8 < s, pltpu.roll(halo, s, 0), r[0:8])
    return jnp.concatenate([top, r[8:]], axis=0)


def _shift_up(cur, halo, s):
    if s == 0:
        return cur
    ts = cur.shape[0]
    row8 = lax.broadcasted_iota(jnp.int32, halo.shape, 0)
    r = pltpu.roll(cur, ts - s, 0)
    bot = jnp.where(row8 >= 8 - s, pltpu.roll(halo, 8 - s, 0), r[ts - 8:ts])
    return jnp.concatenate([r[:ts - 8], bot], axis=0)


def _chunk_tri(ts, upper):
    i = lax.broadcasted_iota(jnp.int32, (ts, ts), 0)
    j = lax.broadcasted_iota(jnp.int32, (ts, ts), 1)
    same = jnp.right_shift(i, 6) == jnp.right_shift(j, 6)
    return jnp.where(same & ((j >= i) if upper else (j <= i)), 1.0, 0.0).astype(F32)


def _gate_values(m, alog, dtb):
    lane = _lane_iota(m.shape)
    beta = _sigmoid(m)
    xg = m + dtb
    sp = jnp.maximum(xg, 0.0) + jnp.log(1.0 + jnp.exp(-jnp.abs(xg)))
    ga = (lane >= A_LANE) & (lane < A_LANE + 8)
    g = jnp.where(ga, -jnp.exp(alog) * sp, 0.0)
    return beta, g, xg, ga


def _l2_heads(a, nh, scale):
    outs, rs = [], []
    for h in range(nh):
        ah = a[:, HEAD * h:HEAD * (h + 1)]
        r = lax.rsqrt(jnp.sum(ah * ah, axis=-1, keepdims=True) + EPS)
        outs.append(ah * (r * scale))
        rs.append(r)
    return jnp.concatenate(outs, axis=-1), rs


def _gdn_prep(proj, conv_w, alog_l, dtb_l, nh, misc_col):
    s = proj.shape[0]
    w = nh * HEAD
    ts = _tile(s, 256, PAIR)
    hb = ts // 8

    def body(cur_ref, halo_ref, misc_ref, cw_ref, al_ref, db_ref, q_ref, k_ref, v_ref, gb_ref, gbt_ref):
        first = pl.program_id(0) == 0
        outs = (q_ref, k_ref, v_ref)
        for sec in range(3):
            cs = slice(sec * w, (sec + 1) * w)
            cur = cur_ref[:, cs]
            halo = jnp.where(first, 0.0, halo_ref[:, cs])
            pre = None
            for j in range(CONV):
                term = cw_ref[j:j + 1, cs] * _shift_down(cur, halo, CONV - 1 - j)
                pre = term if pre is None else pre + term
            act = _silu(pre)
            if sec == 0:
                act, _ = _l2_heads(act, nh, HEAD ** -0.5)
            elif sec == 1:
                act, _ = _l2_heads(act, nh, 1.0)
            outs[sec][...] = act
        m = misc_ref[...]
        lane = _lane_iota(m.shape)
        beta, g, _, ga = _gate_values(m, al_ref[...], db_ref[...])
        gcc = jnp.dot(_chunk_tri(ts, False), g, precision=lax.Precision.HIGHEST,
                      preferred_element_type=F32)
        gb = jnp.where((lane >= B_LANE) & (lane < B_LANE + 8), beta, jnp.where(ga, gcc, 0.0))
        gb_ref[...] = gb
        gbt_ref[...] = gb.T

    return _pcall(
        body, name="gdn_prep", grid=(s // ts,),
        in_specs=[_rows(ts, 3 * w),
                  pl.BlockSpec((8, 3 * w), lambda i: (jnp.maximum(i * hb - 1, 0), 0)),
                  _rows(ts, LANE, misc_col), _full((CONV, 3 * w)), _full((1, LANE)), _full((1, LANE))],
        out_specs=[_rows(ts, w), _rows(ts, w), _rows(ts, w), _rows(ts, LANE),
                   pl.BlockSpec((LANE, ts), lambda i: (0, i))],
        out_shape=[SDS((s, w), F32), SDS((s, w), F32), SDS((s, w), F32), SDS((s, LANE), F32),
                   SDS((LANE, s), F32)])(proj, proj, proj, conv_w, alog_l, dtb_l)


def _gdn_prep_bwd(proj, conv_w, alog_l, dtb_l, dq, dk, dv, dgb, dkr, nh, misc_col):
    s = proj.shape[0]
    w = nh * HEAD
    ts = _tile(s, 256, PAIR)
    hb = ts // 8

    def body(cur_ref, halo_ref, misc_ref, cw_ref, al_ref, db_ref, dq_ref, dk_ref, dv_ref, dgb_ref,
             dkr_ref, dc_ref, dm_ref, dcw_ref, dal_ref, ddb_ref):
        first = pl.program_id(0) == 0

        @pl.when(first)
        def _():
            dcw_ref[...] = jnp.zeros_like(dcw_ref)
            dal_ref[...] = jnp.zeros_like(dal_ref)
            ddb_ref[...] = jnp.zeros_like(ddb_ref)

        dins = (dq_ref, dk_ref, dv_ref)
        for sec in range(3):
            cs = slice(sec * w, (sec + 1) * w)
            cur = cur_ref[:, cs]
            halo = jnp.where(first, 0.0, halo_ref[:, cs])
            us = [_shift_down(cur, halo, CONV - 1 - j) for j in range(CONV)]
            pre = None
            for j in range(CONV):
                term = cw_ref[j:j + 1, cs] * us[j]
                pre = term if pre is None else pre + term
            act = _silu(pre)
            dout = dins[sec][...]
            if sec < 2:
                scale = HEAD ** -0.5 if sec == 0 else 1.0
                parts = []
                for h in range(nh):
                    hs = slice(HEAD * h, HEAD * (h + 1))
                    ah = act[:, hs]
                    r = lax.rsqrt(jnp.sum(ah * ah, axis=-1, keepdims=True) + EPS)
                    ahat = ah * r
                    dy = dout[:, hs]
                    parts.append((scale * r) * (dy - ahat * jnp.sum(dy * ahat, axis=-1, keepdims=True)))
                dact = jnp.concatenate(parts, axis=-1)
            else:
                dact = dout
            dconv = dact * _dsilu(pre)
            dc_ref[:, cs] = dconv
            for j in range(CONV):
                dcw_ref[j:j + 1, cs] += jnp.sum(dconv * us[j], axis=0, keepdims=True)
        m = misc_ref[...]
        lane = _lane_iota(m.shape)
        al = al_ref[...]
        beta, g, xg, ga = _gate_values(m, al, db_ref[...])
        dgbv = dgb_ref[...]
        dg = jnp.dot(_chunk_tri(ts, True), jnp.where(ga, dgbv, 0.0), precision=lax.Precision.HIGHEST,
                     preferred_element_type=F32)
        da_raw = jnp.where(ga, dg * (-jnp.exp(al)) * _sigmoid(xg), 0.0)
        db_raw = jnp.where((lane >= B_LANE) & (lane < B_LANE + 8), dgbv * beta * (1.0 - beta), 0.0)
        dal_ref[...] += jnp.sum(dg * g, axis=0, keepdims=True)
        ddb_ref[...] += jnp.sum(da_raw, axis=0, keepdims=True)
        dm_ref[...] = (dkr_ref[...] + da_raw + db_raw).astype(BF16)

    return _pcall(
        body, name="gdn_prep_bwd", grid=(s // ts,),
        in_specs=[_rows(ts, 3 * w),
                  pl.BlockSpec((8, 3 * w), lambda i: (jnp.maximum(i * hb - 1, 0), 0)),
                  _rows(ts, LANE, misc_col), _full((CONV, 3 * w)), _full((1, LANE)), _full((1, LANE)),
                  _rows(ts, w), _rows(ts, w), _rows(ts, w), _rows(ts, LANE), _rows(ts, LANE)],
        out_specs=[_rows(ts, 3 * w), _rows(ts, LANE), _full((CONV, 3 * w)), _full((1, LANE)),
                   _full((1, LANE))],
        out_shape=[SDS((s, 3 * w), F32), SDS((s, LANE), BF16), SDS((CONV, 3 * w), F32),
                   SDS((1, LANE), F32), SDS((1, LANE), F32)])(
                       proj, proj, proj, conv_w, alog_l, dtb_l, dq, dk, dv, dgb, dkr)


def _conv_bwd_input(dconv, conv_w):
    s, c = dconv.shape
    ts = _tile(s, 256, 8)
    hb = ts // 8
    nblk8 = s // 8
    nt = s // ts

    def body(cur_ref, nxt_ref, cw_ref, o_ref):
        last = pl.program_id(0) == nt - 1
        cur = cur_ref[...]
        halo = jnp.where(last, 0.0, nxt_ref[...])
        acc = None
        for j in range(CONV):
            term = cw_ref[j:j + 1, :] * _shift_up(cur, halo, CONV - 1 - j)
            acc = term if acc is None else acc + term
        o_ref[...] = acc.astype(BF16)

    return _pcall(
        body, name="conv_bwd_input", grid=(nt,),
        in_specs=[_rows(ts, c),
                  pl.BlockSpec((8, c), lambda i: (jnp.minimum((i + 1) * hb, nblk8 - 1), 0)),
                  _full((CONV, c))],
        out_specs=_rows(ts, c), out_shape=SDS((s, c), BF16))(dconv, dconv, conv_w)


def _inv_unit_lower(a):
    n = a[0].shape[0]
    i = lax.broadcasted_iota(jnp.int32, (n, n), 0)
    j = lax.broadcasted_iota(jnp.int32, (n, n), 1)
    eye = jnp.where(i == j, 1.0, 0.0)
    t = [eye - ah for ah in a]
    x = a
    for _ in range(5):
        x = [_dot(xh, xh) for xh in x]
        t = [th + _dot(th, xh) for th, xh in zip(t, x)]
    return t


def _pair_common(q, k, gcol, grow, bcol):
    i = lax.broadcasted_iota(jnp.int32, (PAIR, PAIR), 0)
    j = lax.broadcasted_iota(jnp.int32, (PAIR, PAIR), 1)
    same = jnp.right_shift(i, 6) == jnp.right_shift(j, 6)
    tril = same & (i >= j)
    strict = same & (i > j)
    dec = [jnp.where(tril, jnp.exp(jnp.minimum(gc - gr, 0.0)), 0.0) for gc, gr in zip(gcol, grow)]
    kk = [_dot_nt(kh, kh) for kh in k]
    qk = [_dot_nt(qh, kh) for qh, kh in zip(q, k)]
    a = [jnp.where(strict, b * kkh * d, 0.0) for b, kkh, d in zip(bcol, kk, dec)]
    t = _inv_unit_lower(a)
    p = [qkh * d for qkh, d in zip(qk, dec)]
    return dec, kk, a, t, p, tril, strict


def _ext(v, a):
    z = jnp.zeros_like(v)
    return jnp.concatenate([v, z] if a == 0 else [z, v], axis=0)


def _gdn_fwd(q, k, v, gb, gbt, nh, rider):
    s = q.shape[0]
    w = nh * HEAD
    npair = s // PAIR
    r_in, r_out = len(rider.arrays), len(rider.out_shapes)

    def body(*refs):
        q_ref, k_ref, v_ref, gb_ref, gbt_ref = refs[:5]
        o_ref, st_ref = refs[5 + r_in:7 + r_in]
        s_ref = refs[7 + r_in + r_out]
        r_refs = (refs[5:5 + r_in], refs[7 + r_in:7 + r_in + r_out], refs[8 + r_in + r_out:])
        _ride_begin(rider, r_refs, pl.program_id(0))

        @pl.when(pl.program_id(0) == 0)
        def _():
            s_ref[...] = jnp.zeros_like(s_ref)

        heads = range(nh)
        hs = [slice(HEAD * h, HEAD * (h + 1)) for h in heads]
        gbv = gb_ref[...]
        q, k, v = [q_ref[:, s_] for s_ in hs], [k_ref[:, s_] for s_ in hs], [v_ref[:, s_] for s_ in hs]
        gcol = [_col(gbv, A_LANE + h) for h in heads]
        bcol = [_col(gbv, B_LANE + h) for h in heads]
        grow = [gbt_ref[A_LANE + h:A_LANE + h + 1, :] for h in heads]
        _, _, _, t, p, _, _ = _pair_common(q, k, gcol, grow, bcol)
        eg = [jnp.exp(gc) for gc in gcol]
        qg = [x * e for x, e in zip(q, eg)]
        kg = [x * e for x, e in zip(k, eg)]
        outs = []
        for a in range(2):
            sl = slice(CHUNK * a, CHUNK * (a + 1))
            st = [s_ref[h] for h in heads]
            for h in heads:
                st_ref[a, h] = st[h]
            r = [v[h][sl] - _dot(kg[h][sl], st[h]) for h in heads]
            vn = [_dot(t[h][sl], _ext(bcol[h][sl] * r[h], a)) for h in heads]
            outs.append([_dot(qg[h][sl], st[h]) + _dot(p[h][sl], _ext(vn[h], a)) for h in heads])
            gl = [_col(gr, CHUNK * (a + 1) - 1) for gr in grow]
            kd = [k[h][sl] * jnp.exp(gl[h] - gcol[h][sl]) for h in heads]
            upd = [_dot_tn(kd[h], vn[h]) for h in heads]
            for h in heads:
                s_ref[h] = jnp.exp(gl[h]) * st[h] + upd[h]
        for h in heads:
            o_ref[:, hs[h]] = jnp.concatenate([outs[0][h], outs[1][h]], axis=0)
        _ride_end(rider, r_refs, pl.program_id(0), npair)

    outs = _pcall(
        body, name="gdn_fwd", grid=(npair,),
        in_specs=[_rows(PAIR, w), _rows(PAIR, w), _rows(PAIR, w), _rows(PAIR, LANE),
                  pl.BlockSpec((LANE, PAIR), lambda i: (0, i))] + [_ANY] * r_in,
        out_specs=[_rows(PAIR, w), pl.BlockSpec((2, nh, HEAD, HEAD), lambda i: (i, 0, 0, 0))] + [_ANY] * r_out,
        out_shape=[SDS((s, w), F32), SDS((2 * npair, nh, HEAD, HEAD), F32)] + rider.out_shapes,
        scratch=[pltpu.VMEM((nh, HEAD, HEAD), F32)] + rider.scratch)(q, k, v, gb, gbt, *rider.arrays)
    return outs[0], outs[1], outs[2:]


def _gdn_bwd(q, k, v, gb, gbt, states, do, nh):
    s = q.shape[0]
    w = nh * HEAD
    npair = s // PAIR
    rev = lambda i: (npair - 1 - i, 0)

    def body(q_ref, k_ref, v_ref, gb_ref, gbt_ref, st_ref, do_ref, dq_ref, dk_ref, dv_ref, dgb_ref,
             ds_ref):
        @pl.when(pl.program_id(0) == 0)
        def _():
            ds_ref[...] = jnp.zeros_like(ds_ref)

        lane = _lane_iota((PAIR, LANE))
        row = lax.broadcasted_iota(jnp.int32, (CHUNK, 1), 0)
        heads = range(nh)
        hs = [slice(HEAD * h, HEAD * (h + 1)) for h in heads]
        gbv = gb_ref[...]
        q, k, v = [q_ref[:, s_] for s_ in hs], [k_ref[:, s_] for s_ in hs], [v_ref[:, s_] for s_ in hs]
        do = [do_ref[:, s_] for s_ in hs]
        gcol = [_col(gbv, A_LANE + h) for h in heads]
        bcol = [_col(gbv, B_LANE + h) for h in heads]
        grow = [gbt_ref[A_LANE + h:A_LANE + h + 1, :] for h in heads]
        dec, kk, amat, t, p, tril, strict = _pair_common(q, k, gcol, grow, bcol)
        tt, pt = [x.T for x in t], [x.T for x in p]
        eg = [jnp.exp(gc) for gc in gcol]
        qg = [x * e for x, e in zip(q, eg)]
        kg = [x * e for x, e in zip(k, eg)]
        sums = lambda x: jnp.sum(x, axis=-1, keepdims=True)
        rs, vns = [None, None], [None, None]
        for a in range(2):
            sl = slice(CHUNK * a, CHUNK * (a + 1))
            rs[a] = [v[h][sl] - _dot(kg[h][sl], st_ref[a, h]) for h in heads]
            vns[a] = [_dot(t[h][sl], _ext(bcol[h][sl] * rs[a][h], a)) for h in heads]
        dsn = [ds_ref[h] for h in heads]
        dqs, dks, dvs, dgcs, dbs, drbs = ([None, None] for _ in range(6))
        for a in (1, 0):
            sl = slice(CHUNK * a, CHUNK * (a + 1))
            st = [st_ref[a, h] for h in heads]
            gl = [_col(gr, CHUNK * (a + 1) - 1) for gr in grow]
            egl = [jnp.exp(x) for x in gl]
            dk_dec = [jnp.exp(gl[h] - gcol[h][sl]) for h in heads]
            kd = [k[h][sl] * dk_dec[h] for h in heads]
            d_vn = [_dot(pt[h][sl], _ext(do[h][sl], a)) + _dot(kd[h], dsn[h]) for h in heads]
            d_qg = [_dot_nt(do[h][sl], st[h]) for h in heads]
            d_rb = [_dot(tt[h][sl], _ext(d_vn[h], a)) for h in heads]
            d_r = [bcol[h][sl] * d_rb[h] for h in heads]
            d_kg = [-_dot_nt(d_r[h], st[h]) for h in heads]
            d_kd = [_dot_nt(vns[a][h], dsn[h]) for h in heads]
            dsn_new = [_dot_tn(qg[h][sl], do[h][sl]) - _dot_tn(kg[h][sl], d_r[h]) for h in heads]
            dbs[a] = [sums(d_rb[h] * rs[a][h]) for h in heads]
            dgl = [egl[h] * jnp.sum(dsn[h] * st[h], keepdims=True) + jnp.sum(d_kd[h] * kd[h], keepdims=True)
                   for h in heads]
            dgcs[a] = [sums(d_qg[h] * qg[h][sl]) + sums(d_kg[h] * kg[h][sl]) - sums(d_kd[h] * kd[h])
                       + jnp.where(row == CHUNK - 1, dgl[h], 0.0) for h in heads]
            dqs[a] = [d_qg[h] * eg[h][sl] for h in heads]
            dks[a] = [d_kg[h] * eg[h][sl] + d_kd[h] * dk_dec[h] for h in heads]
            dvs[a] = d_r
            drbs[a] = d_rb
            dsn = [dsn_new[h] + egl[h] * dsn[h] for h in heads]
        for h in heads:
            ds_ref[h] = dsn[h]
        cat = lambda xs, h: jnp.concatenate([xs[0][h], xs[1][h]], axis=0)
        vn = [cat(vns, h) for h in heads]
        d_rb = [cat(drbs, h) for h in heads]
        dp = [jnp.where(tril, _dot_nt(do[h], vn[h]), 0.0) for h in heads]
        dam = [jnp.where(strict, -_dot_nt(d_rb[h], vn[h]), 0.0) for h in heads]
        g_p = [dp[h] * dec[h] for h in heads]
        g_a = [dam[h] * dec[h] for h in heads]
        gbk = [bcol[h] * g_a[h] for h in heads]
        dq2 = [_dot(g_p[h], k[h]) for h in heads]
        dk2 = [_dot_tn(g_p[h], q[h]) + _dot(gbk[h], k[h]) + _dot_tn(gbk[h], k[h]) for h in heads]
        dgb = jnp.zeros((PAIR, LANE), F32)
        for h in heads:
            dq_ref[:, hs[h]] = cat(dqs, h) + dq2[h]
            dk_ref[:, hs[h]] = cat(dks, h) + dk2[h]
            dv_ref[:, hs[h]] = cat(dvs, h)
            dbeta = cat(dbs, h) + sums(g_a[h] * kk[h])
            mm = dp[h] * p[h] + dam[h] * amat[h]
            dgc = cat(dgcs, h) + sums(mm) - sums(mm.T)
            dgb = dgb + jnp.where(lane == A_LANE + h, dgc, 0.0) + jnp.where(lane == B_LANE + h, dbeta, 0.0)
        dgb_ref[...] = dgb

    return _pcall(
        body, name="gdn_bwd", grid=(npair,),
        in_specs=[pl.BlockSpec((PAIR, w), rev), pl.BlockSpec((PAIR, w), rev), pl.BlockSpec((PAIR, w), rev),
                  pl.BlockSpec((PAIR, LANE), rev),
                  pl.BlockSpec((LANE, PAIR), lambda i: (0, npair - 1 - i)),
                  pl.BlockSpec((2, nh, HEAD, HEAD), lambda i: (npair - 1 - i, 0, 0, 0)),
                  pl.BlockSpec((PAIR, w), rev)],
        out_specs=[pl.BlockSpec((PAIR, w), rev), pl.BlockSpec((PAIR, w), rev), pl.BlockSpec((PAIR, w), rev),
                   pl.BlockSpec((PAIR, LANE), rev)],
        out_shape=[SDS((s, w), F32), SDS((s, w), F32), SDS((s, w), F32), SDS((s, LANE), F32)],
        scratch=[pltpu.VMEM((nh, HEAD, HEAD), F32)])(q, k, v, gb, gbt, states, do)


def _mla_norm(proj, qw, kvw, col_q, col_kv):
    s = proj.shape[0]
    lr = qw.shape[1]
    ts = _tile(s, 512, 8)

    def body(cq_ref, ckv_ref, qw_ref, kvw_ref, oq_ref, okv_ref):
        for x_ref, w_ref, o_ref in ((cq_ref, qw_ref, oq_ref), (ckv_ref, kvw_ref, okv_ref)):
            xv = x_ref[...]
            r = lax.rsqrt(jnp.mean(xv * xv, axis=-1, keepdims=True) + EPS)
            o_ref[...] = (xv * r * w_ref[...]).astype(BF16)

    return _pcall(body, name="mla_norm", grid=(s // ts,),
                  in_specs=[_rows(ts, lr, col_q), _rows(ts, lr, col_kv), _full((1, lr)), _full((1, lr))],
                  out_specs=[_rows(ts, lr), _rows(ts, lr)],
                  out_shape=[SDS((s, lr), BF16), SDS((s, lr), BF16)])(proj, proj, qw, kvw)


def _mla_norm_bwd(proj, qw, kvw, dq, dkv, col_q, col_kv):
    s = proj.shape[0]
    lr = qw.shape[1]
    ts = _tile(s, 512, 8)

    def body(cq_ref, ckv_ref, qw_ref, kvw_ref, dq_ref, dkv_ref, oq_ref, okv_ref, dqw_ref, dkvw_ref):
        @pl.when(pl.program_id(0) == 0)
        def _():
            dqw_ref[...] = jnp.zeros_like(dqw_ref)
            dkvw_ref[...] = jnp.zeros_like(dkvw_ref)

        for x_ref, w_ref, d_ref, o_ref, dw_ref in ((cq_ref, qw_ref, dq_ref, oq_ref, dqw_ref),
                                                    (ckv_ref, kvw_ref, dkv_ref, okv_ref, dkvw_ref)):
            xv, dh = x_ref[...], d_ref[...]
            r = lax.rsqrt(jnp.mean(xv * xv, axis=-1, keepdims=True) + EPS)
            xh = xv * r
            dw_ref[...] += jnp.sum(dh * xh, axis=0, keepdims=True)
            dxh = dh * w_ref[...]
            o_ref[...] = (r * (dxh - xh * jnp.mean(dxh * xh, axis=-1, keepdims=True))).astype(BF16)

    return _pcall(body, name="mla_norm_bwd", grid=(s // ts,),
                  in_specs=[_rows(ts, lr, col_q), _rows(ts, lr, col_kv), _full((1, lr)), _full((1, lr)),
                            _rows(ts, lr), _rows(ts, lr)],
                  out_specs=[_rows(ts, lr), _rows(ts, lr), _full((1, lr)), _full((1, lr))],
                  out_shape=[SDS((s, lr), BF16), SDS((s, lr), BF16), SDS((1, lr), F32),
                             SDS((1, lr), F32)])(proj, proj, qw, kvw, dq, dkv)


def _rope_tables(pos, invf, sgn):
    ang = pos * invf
    return jnp.cos(ang), jnp.sin(ang) * sgn


def _swap_halves_lanes(y):
    lane = _lane_iota(y.shape)
    return jnp.where(lane < ROPE // 2, pltpu.roll(y, LANE - ROPE // 2, 1), pltpu.roll(y, ROPE // 2, 1))


def _rope_consts():
    half = ROPE // 2
    inv = ROPE_THETA ** (-jnp.arange(half, dtype=F32) / half)
    invf = jnp.concatenate([inv, inv, jnp.zeros((LANE - ROPE,), F32)])[None, :]
    sgn = jnp.concatenate([-jnp.ones((half,), F32), jnp.ones((half,), F32),
                           jnp.zeros((LANE - ROPE,), F32)])[None, :]
    return invf, sgn


def _mla_rope(qraw, kvraw, proj, pos, nh, misc_col):
    s = qraw.shape[0]
    ts = _tile(s, 256, 8)
    wq = nh * 2 * HEAD
    invf, sgn = _rope_consts()

    def body(q_ref, kv_ref, misc_ref, pos_ref, if_ref, sg_ref, qc_ref, kc_ref, v_ref):
        c, sn = _rope_tables(pos_ref[...], if_ref[...], sg_ref[...])
        lane = _lane_iota(c.shape)
        rot = lambda xb: xb * c + _swap_halves_lanes(xb) * sn
        qs = SM_SCALE * LOG2E
        krot = jnp.where(lane < ROPE, rot(misc_ref[...]), 0.0).astype(BF16)
        for h in range(nh):
            b0 = 2 * HEAD * h
            qc_ref[:, b0:b0 + HEAD] = (q_ref[:, b0:b0 + HEAD] * qs).astype(BF16)
            qc_ref[:, b0 + HEAD:b0 + 2 * HEAD] = (rot(q_ref[:, b0 + HEAD:b0 + 2 * HEAD]) * qs).astype(BF16)
            kc_ref[:, b0:b0 + HEAD] = kv_ref[:, b0:b0 + HEAD].astype(BF16)
            kc_ref[:, b0 + HEAD:b0 + 2 * HEAD] = krot
        v_ref[...] = kv_ref[:, wq:].astype(BF16)

    return _pcall(body, name="mla_rope", grid=(s // ts,),
                  in_specs=[_rows(ts, wq), _rows(ts, wq + nh * HEAD), _rows(ts, LANE, misc_col),
                            _rows(ts, 1), _full((1, LANE)), _full((1, LANE))],
                  out_specs=[_rows(ts, wq), _rows(ts, wq), _rows(ts, nh * HEAD)],
                  out_shape=[SDS((s, wq), BF16), SDS((s, wq), BF16), SDS((s, nh * HEAD), BF16)])(
                      qraw, kvraw, proj, pos, invf, sgn)


def _mla_rope_bwd(dqc, dkc, dv, pos, nh):
    s = dqc.shape[0]
    ts = _tile(s, 256, 8)
    wq = nh * 2 * HEAD
    invf, sgn = _rope_consts()

    def body(dq_ref, dk_ref, dv_ref, pos_ref, if_ref, sg_ref, oq_ref, okv_ref, okr_ref):
        c, sn = _rope_tables(pos_ref[...], if_ref[...], sg_ref[...])
        lane = _lane_iota(c.shape)
        unrot = lambda d: d * c + _swap_halves_lanes(d * sn)
        dkr = jnp.zeros(c.shape, F32)
        for h in range(nh):
            b0 = 2 * HEAD * h
            oq_ref[:, b0:b0 + HEAD] = (dq_ref[:, b0:b0 + HEAD] * SM_SCALE).astype(BF16)
            oq_ref[:, b0 + HEAD:b0 + 2 * HEAD] = (
                unrot(dq_ref[:, b0 + HEAD:b0 + 2 * HEAD]) * SM_SCALE).astype(BF16)
            okv_ref[:, b0:b0 + HEAD] = (dk_ref[:, b0:b0 + HEAD] * LN2).astype(BF16)
            okv_ref[:, b0 + HEAD:b0 + 2 * HEAD] = jnp.zeros((ts, HEAD), BF16)
            dkr = dkr + dk_ref[:, b0 + HEAD:b0 + 2 * HEAD]
        okv_ref[:, wq:] = dv_ref[...].astype(BF16)
        okr_ref[...] = jnp.where(lane < ROPE, unrot(jnp.where(lane < ROPE, dkr * LN2, 0.0)), 0.0)

    return _pcall(body, name="mla_rope_bwd", grid=(s // ts,),
                  in_specs=[_rows(ts, wq), _rows(ts, wq), _rows(ts, nh * HEAD), _rows(ts, 1),
                            _full((1, LANE)), _full((1, LANE))],
                  out_specs=[_rows(ts, wq), _rows(ts, wq + nh * HEAD), _rows(ts, LANE)],
                  out_shape=[SDS((s, wq), BF16), SDS((s, wq + nh * HEAD), BF16), SDS((s, LANE), F32)])(
                      dqc, dkc, dv, pos, invf, sgn)


def _causal_mask(blk):
    i = lax.broadcasted_iota(jnp.int32, (blk, blk), 0)
    j = lax.broadcasted_iota(jnp.int32, (blk, blk), 1)
    return j <= i


def _mla_fwd(qc, kc, v, nh):
    s = qc.shape[0]
    blk = _tile(s, MLA_BLOCK)
    nb = s // blk
    rep = blk // LANE

    hp = 2
    assert nh % hp == 0
    once = pl.Buffered(1)

    def body(q_ref, k_ref, v_ref, o_ref, lse_ref, m_sc, l_sc, acc):
        i = pl.program_id(1)
        m_sc[...] = jnp.full_like(m_sc, -1e30)
        l_sc[...] = jnp.zeros_like(l_sc)
        acc[...] = jnp.zeros_like(acc)

        def step(j, masked):
            rows = pl.ds(pl.multiple_of(j * blk, blk), blk)
            es = range(hp)
            sc = [_dot_nt(q_ref[:, 2 * HEAD * e:2 * HEAD * (e + 1)], k_ref[rows, 2 * HEAD * e:2 * HEAD * (e + 1)])
                  for e in es]
            if masked:
                sc = [jnp.where(_causal_mask(blk), x, -1e30) for x in sc]
            m_prev = [m_sc[e] for e in es]
            m_new = [jnp.maximum(m_prev[e], jnp.max(sc[e], axis=-1, keepdims=True)) for e in es]
            p = [jnp.exp2(sc[e] - jnp.tile(m_new[e], (1, rep))) for e in es]
            alpha = [jnp.exp2(m_prev[e] - m_new[e]) for e in es]
            pv = [_dot(p[e], v_ref[rows, HEAD * e:HEAD * (e + 1)]) for e in es]
            for e in es:
                l_sc[e] = alpha[e] * l_sc[e] + jnp.sum(p[e], axis=-1, keepdims=True)
                acc[e] = alpha[e] * acc[e] + pv[e]
                m_sc[e] = m_new[e]

        def loop_body(j, carry):
            step(j, False)
            return carry

        lax.fori_loop(0, i, loop_body, 0)
        step(i, True)
        for e in range(hp):
            o_ref[:, HEAD * e:HEAD * (e + 1)] = acc[e] / l_sc[e]
            lse_ref[e] = m_sc[e] + jnp.log(l_sc[e]) * LOG2E

    return pl.pallas_call(
        body, name="mla_fwd", grid=(nh // hp, nb),
        in_specs=[pl.BlockSpec((blk, hp * 2 * HEAD), lambda g, i: (i, g)),
                  pl.BlockSpec((s, hp * 2 * HEAD), lambda g, i: (0, g), pipeline_mode=once),
                  pl.BlockSpec((s, hp * HEAD), lambda g, i: (0, g), pipeline_mode=once)],
        out_specs=[pl.BlockSpec((blk, hp * HEAD), lambda g, i: (i, g)),
                   pl.BlockSpec((hp, blk, LANE), lambda g, i: (g, i, 0))],
        out_shape=[SDS((s, nh * HEAD), F32), SDS((nh, s, LANE), F32)],
        scratch_shapes=[pltpu.VMEM((hp, blk, LANE), F32), pltpu.VMEM((hp, blk, LANE), F32),
                        pltpu.VMEM((hp, blk, HEAD), F32)],
        compiler_params=pltpu.CompilerParams(dimension_semantics=("arbitrary",) * 2,
                                             vmem_limit_bytes=VMEM_LIMIT))(qc, kc, v)


def _mla_bwd(qc, kc, v, do, lse, delta, nh, rider):
    s = qc.shape[0]
    blk = _tile(s, MLA_BLOCK)
    nb = s // blk
    rep = blk // LANE
    once = pl.Buffered(1)
    r_in, r_out = len(rider.arrays), len(rider.out_shapes)

    def body(*refs):
        q_ref, do_ref, lse_ref, dl_ref, k_ref, v_ref = refs[:6]
        dq_ref, dk_ref, dv_ref = refs[6 + r_in:9 + r_in]
        dk_acc, dv_acc = refs[9 + r_in + r_out:11 + r_in + r_out]
        r_refs = (refs[6:6 + r_in], refs[9 + r_in:9 + r_in + r_out], refs[11 + r_in + r_out:])
        j = pl.program_id(1)
        grid_step = pl.program_id(0) * nb + j
        _ride_begin(rider, r_refs, grid_step)

        @pl.when(j == 0)
        def _():
            dq_ref[...] = jnp.zeros_like(dq_ref)

        dk_acc[...] = jnp.zeros_like(dk_acc)
        dv_acc[...] = jnp.zeros_like(dv_acc)
        kj, vj = k_ref[...], v_ref[...]

        def step(i, masked):
            rows = pl.ds(pl.multiple_of(i * blk, blk), blk)
            qi, doi = q_ref[rows, :], do_ref[rows, :]
            sc = _dot_nt(qi, kj)
            if masked:
                sc = jnp.where(_causal_mask(blk), sc, -1e30)
            p = jnp.exp2(sc - jnp.tile(lse_ref[rows, :], (1, rep)))
            dp = _dot_nt(doi, vj)
            ds = p * (dp - jnp.tile(dl_ref[rows, :], (1, rep)))
            dv_acc[...] += _dot_tn(p, doi)
            dk_acc[...] += _dot_tn(ds, qi)
            dq_ref[rows, :] += _dot(ds, kj)

        def loop_body(i, carry):
            step(i, False)
            return carry

        step(j, True)
        lax.fori_loop(j + 1, nb, loop_body, 0)
        dk_ref[...] = dk_acc[...]
        dv_ref[...] = dv_acc[...]
        _ride_end(rider, r_refs, grid_step, nh * nb)

    outs = _pcall(
        body, name="mla_bwd", grid=(nh, nb),
        in_specs=[pl.BlockSpec((s, 2 * HEAD), lambda h, j: (0, h), pipeline_mode=once),
                  pl.BlockSpec((s, HEAD), lambda h, j: (0, h), pipeline_mode=once),
                  pl.BlockSpec((None, s, LANE), lambda h, j: (h, 0, 0), pipeline_mode=once),
                  pl.BlockSpec((None, s, LANE), lambda h, j: (h, 0, 0), pipeline_mode=once),
                  pl.BlockSpec((blk, 2 * HEAD), lambda h, j: (j, h)),
                  pl.BlockSpec((blk, HEAD), lambda h, j: (j, h))] + [_ANY] * r_in,
        out_specs=[pl.BlockSpec((s, 2 * HEAD), lambda h, j: (0, h), pipeline_mode=once),
                   pl.BlockSpec((blk, 2 * HEAD), lambda h, j: (j, h)),
                   pl.BlockSpec((blk, HEAD), lambda h, j: (j, h))] + [_ANY] * r_out,
        out_shape=[SDS((s, nh * 2 * HEAD), F32), SDS((s, nh * 2 * HEAD), F32),
                   SDS((s, nh * HEAD), F32)] + rider.out_shapes,
        scratch=[pltpu.VMEM((blk, 2 * HEAD), F32), pltpu.VMEM((blk, HEAD), F32)] + rider.scratch)(
            qc, do, lse, delta, kc, v, *rider.arrays)
    return outs[0], outs[1], outs[2], outs[3:]


def _mix_fwd(og, proj, om, gw, mw, nh, z_col):
    s = og.shape[0]
    w = nh * HEAD
    ts = _tile(s, 256, 8)

    def body(og_ref, z_ref, om_ref, gw_ref, mw_ref, o_ref):
        for h in range(nh):
            hs = slice(HEAD * h, HEAD * (h + 1))
            a = og_ref[:, hs]
            r = lax.rsqrt(jnp.mean(a * a, axis=-1, keepdims=True) + EPS)
            o_ref[:, hs] = (a * r * gw_ref[...] * _silu(z_ref[:, hs])).astype(BF16)
            b = om_ref[:, hs]
            r = lax.rsqrt(jnp.mean(b * b, axis=-1, keepdims=True) + EPS)
            o_ref[:, w + HEAD * h:w + HEAD * (h + 1)] = (b * r * mw_ref[...]).astype(BF16)

    return _pcall(body, name="mix_fwd", grid=(s // ts,),
                  in_specs=[_rows(ts, w), _rows(ts, w, z_col), _rows(ts, w), _full((1, HEAD)),
                            _full((1, HEAD))],
                  out_specs=_rows(ts, 2 * w), out_shape=SDS((s, 2 * w), BF16))(og, proj, om, gw, mw)


def _mix_bwd(dmix, og, proj, om, gw, mw, nh, z_col):
    s = og.shape[0]
    w = nh * HEAD
    ts = _tile(s, 256, 8)

    def body(d_ref, og_ref, z_ref, om_ref, gw_ref, mw_ref, dog_ref, dz_ref, dom_ref, dgw_ref, dmw_ref,
             dl_ref):
        @pl.when(pl.program_id(0) == 0)
        def _():
            dgw_ref[...] = jnp.zeros_like(dgw_ref)
            dmw_ref[...] = jnp.zeros_like(dmw_ref)

        dgw = jnp.zeros((1, HEAD), F32)
        dmw = jnp.zeros((1, HEAD), F32)
        for h in range(nh):
            hs = slice(HEAD * h, HEAD * (h + 1))
            a, z, dy = og_ref[:, hs], z_ref[:, hs], d_ref[:, hs]
            r = lax.rsqrt(jnp.mean(a * a, axis=-1, keepdims=True) + EPS)
            ah = a * r
            sz = _silu(z)
            dz_ref[:, hs] = (dy * (ah * gw_ref[...]) * _dsilu(z)).astype(BF16)
            dn = dy * sz
            dgw = dgw + jnp.sum(dn * ah, axis=0, keepdims=True)
            dah = dn * gw_ref[...]
            dog_ref[:, hs] = r * (dah - ah * jnp.mean(dah * ah, axis=-1, keepdims=True))
            b, dyb = om_ref[:, hs], d_ref[:, w + HEAD * h:w + HEAD * (h + 1)]
            r = lax.rsqrt(jnp.mean(b * b, axis=-1, keepdims=True) + EPS)
            bh = b * r
            dmw = dmw + jnp.sum(dyb * bh, axis=0, keepdims=True)
            dbh = dyb * mw_ref[...]
            dom = r * (dbh - bh * jnp.mean(dbh * bh, axis=-1, keepdims=True))
            dom_ref[:, hs] = dom.astype(BF16)
            dl_ref[h] = jnp.broadcast_to(jnp.sum(dom * b, axis=-1, keepdims=True), (ts, LANE))
        dgw_ref[...] += dgw
        dmw_ref[...] += dmw

    return _pcall(body, name="mix_bwd", grid=(s // ts,),
                  in_specs=[_rows(ts, 2 * w), _rows(ts, w), _rows(ts, w, z_col), _rows(ts, w),
                            _full((1, HEAD)), _full((1, HEAD))],
                  out_specs=[_rows(ts, w), _rows(ts, w), _rows(ts, w), _full((1, HEAD)), _full((1, HEAD)),
                             pl.BlockSpec((nh, ts, LANE), lambda i: (0, i, 0))],
                  out_shape=[SDS((s, w), F32), SDS((s, w), BF16), SDS((s, w), BF16), SDS((1, HEAD), F32),
                             SDS((1, HEAD), F32), SDS((nh, s, LANE), F32)])(dmix, og, proj, om, gw, mw)


def _swiglu_fwd(h2, wg, wu):
    m, kdim = h2.shape
    tn = wg.shape[2]
    n = 4 * tn
    tm, tk = _tile(m, 512), _tile(kdim, 2048)
    nk = kdim // tk

    def body(a_ref, g_ref, u_ref, act_ref, go_ref, uo_ref, gacc, uacc):
        k = pl.program_id(2)

        @pl.when(k == 0)
        def _():
            gacc[...] = jnp.zeros_like(gacc)
            uacc[...] = jnp.zeros_like(uacc)

        a = a_ref[...]
        gacc[...] += _dot(a, g_ref[...])
        uacc[...] += _dot(a, u_ref[...])

        @pl.when(k == nk - 1)
        def _():
            g, u = gacc[...], uacc[...]
            act_ref[...] = (_silu(g) * u).astype(BF16)
            go_ref[...] = g.astype(BF16)
            uo_ref[...] = u.astype(BF16)

    a_spec = pl.BlockSpec((tm, tk), lambda i, j, k: (i, k))
    b_spec = pl.BlockSpec((None, tk, tn), lambda i, j, k: (j, k, 0))
    o_spec = pl.BlockSpec((tm, tn), lambda i, j, k: (i, j))
    return _pcall(body, name="swiglu_fwd", grid=(m // tm, n // tn, nk),
                  in_specs=[a_spec, b_spec, b_spec], out_specs=[o_spec] * 3,
                  out_shape=[SDS((m, n), BF16)] * 3,
                  scratch=[pltpu.VMEM((tm, tn), F32), pltpu.VMEM((tm, tn), F32)])(h2, wg, wu)


def _swiglu_bwd(dx3, wd, g, u):
    m, kdim = dx3.shape
    n = wd.shape[0]
    tm, tn, tk = _tile(m, 1024), _tile(n, 512), _tile(kdim, 2048)
    nk = kdim // tk

    def body(a_ref, b_ref, g_ref, u_ref, dg_ref, du_ref, acc):
        k = pl.program_id(2)

        @pl.when(k == 0)
        def _():
            acc[...] = jnp.zeros_like(acc)

        acc[...] += _dot_nt(a_ref[...], b_ref[...])

        @pl.when(k == nk - 1)
        def _():
            da = acc[...]
            gv, uv = g_ref[...].astype(F32), u_ref[...].astype(F32)
            dg_ref[...] = (da * uv * _dsilu(gv)).astype(BF16)
            du_ref[...] = (da * _silu(gv)).astype(BF16)

    a_spec = pl.BlockSpec((tm, tk), lambda i, j, k: (i, k))
    b_spec = pl.BlockSpec((tn, tk), lambda i, j, k: (j, k))
    o_spec = pl.BlockSpec((tm, tn), lambda i, j, k: (i, j))
    return _pcall(body, name="swiglu_bwd", grid=(m // tm, n // tn, nk),
                  in_specs=[a_spec, b_spec, o_spec, o_spec], out_specs=[o_spec] * 2,
                  out_shape=[SDS((m, n), BF16)] * 2,
                  scratch=[pltpu.VMEM((tm, tn), F32)])(dx3, wd, g, u)


def _sum_pair(g, recv, place, name):
    _, _, rh, c = g.shape
    tr = _tile(rh, 256, 16)

    def body(pl_ref, g_ref, r_ref, o16_ref, own_ref):
        sm = g_ref[...].astype(F32) + r_ref[...].astype(F32)
        o16_ref[...] = sm.astype(BF16)

        @pl.when(pl.program_id(1) == pl_ref[1])
        def _():
            own_ref[...] = sm

    grid_spec = pltpu.PrefetchScalarGridSpec(
        num_scalar_prefetch=1, grid=(rh // tr, 4),
        in_specs=[pl.BlockSpec((None, None, tr, c), lambda i, t, p: (t, p[0], i, 0)),
                  pl.BlockSpec((None, tr, c), lambda i, t, p: (t, i, 0))],
        out_specs=[pl.BlockSpec((None, tr, c), lambda i, t, p: (t, i, 0)),
                   pl.BlockSpec((tr, c), lambda i, t, p: (i, 0))])
    return pl.pallas_call(
        body, name=name, grid_spec=grid_spec,
        out_shape=[SDS((4, rh, c), BF16), SDS((rh, c), F32)],
        compiler_params=pltpu.CompilerParams(dimension_semantics=("arbitrary",) * 2,
                                             vmem_limit_bytes=VMEM_LIMIT))(place, g, recv)


def _sum_chips(own, recv, name):
    rh, c = own.shape
    tr = _tile(rh, 256, 16)

    def body(o_ref, r_ref, out_ref):
        acc = o_ref[...]
        for j in range(3):
            acc = acc + r_ref[j].astype(F32)
        out_ref[...] = acc

    return _pcall(body, name=name, grid=(rh // tr,),
                  in_specs=[_rows(tr, c), pl.BlockSpec((3, tr, c), lambda i: (0, i, 0))],
                  out_specs=_rows(tr, c), out_shape=SDS(own.shape, F32))(own, recv)


def _adamw_update(wv, gv, mv, vv):
    mn = ADAM_B1 * mv + (1.0 - ADAM_B1) * gv
    vn = ADAM_B2 * vv + (1.0 - ADAM_B2) * (gv * gv)
    m_hat = mn / (1.0 - ADAM_B1 ** ADAM_STEP)
    v_hat = vn / (1.0 - ADAM_B2 ** ADAM_STEP)
    return -ADAM_LR * (m_hat / (jnp.sqrt(v_hat) + ADAM_EPS) + ADAM_WD * wv), mn, vn


def _adamw(w, g, m, v, name):
    r, c = w.shape
    tr = _tile(r, 256, 8)

    def body(w_ref, g_ref, m_ref, v_ref, d_ref, mo_ref, vo_ref):
        d_ref[...], mo_ref[...], vo_ref[...] = _adamw_update(w_ref[...], g_ref[...], m_ref[...], v_ref[...])

    spec = _rows(tr, c)
    return _pcall(body, name=name, grid=(r // tr,), in_specs=[spec] * 4, out_specs=[spec] * 3,
                  out_shape=[SDS(w.shape, F32)] * 3)(w, g, m, v)


def _adamw_halves(w, mine, theirs, m, v, place, name):
    r, c = w.shape
    rh = r // 2
    tr = _tile(rh, 256, 8)
    nt = rh // tr

    def body(p_ref, w_ref, a_ref, b_ref, m_ref, v_ref, g_ref, d_ref, mo_ref, vo_ref):
        gv = jnp.where(pl.program_id(0) // nt == p_ref[0], a_ref[...], b_ref[...])
        g_ref[...] = gv
        d_ref[...], mo_ref[...], vo_ref[...] = _adamw_update(w_ref[...], gv, m_ref[...], v_ref[...])

    full = pl.BlockSpec((tr, c), lambda i, p: (i, 0))
    half = pl.BlockSpec((tr, c), lambda i, p: (i % nt, 0))
    grid_spec = pltpu.PrefetchScalarGridSpec(num_scalar_prefetch=1, grid=(2 * nt,),
                                             in_specs=[full, half, half, full, full], out_specs=[full] * 4)
    return pl.pallas_call(
        body, name=name, grid_spec=grid_spec, out_shape=[SDS(w.shape, F32)] * 4,
        compiler_params=pltpu.CompilerParams(dimension_semantics=("arbitrary",),
                                             vmem_limit_bytes=VMEM_LIMIT))(place, w, mine, theirs, m, v)


def _place():
    x, y, c = lax.axis_index("x"), lax.axis_index("y"), lax.axis_index("c")
    chips = [(1 - x, y), (x, 1 - y), (1 - x, 1 - y)]
    return x, y, c, chips


_ANY = pl.BlockSpec(memory_space=pl.ANY)


def _remote(src, dst, sems, k, to):
    return pltpu.make_async_remote_copy(src_ref=src, dst_ref=dst, send_sem=sems[0].at[k], recv_sem=sems[1].at[k],
                                        device_id=to, device_id_type=MESH)


class _Gather:
    def __init__(self, shards):
        n = len(shards)
        self.arrays = list(shards)
        self.out_shapes = [SDS((4,) + a.shape, a.dtype) for a in shards]
        self.scratch = [pltpu.SemaphoreType.DMA((7 * n,)), pltpu.SemaphoreType.DMA((7 * n,))]

    def _plan(self, ins, outs, sems):
        x, y, c, chips = _place()
        own, sib = 2 * x + y, (x, y, 1 - c)
        plan = []
        for wi, (w, o) in enumerate(zip(ins, outs)):
            rh = w.shape[0] // 2
            mine, theirs = pl.ds(c * rh, rh), pl.ds((1 - c) * rh, rh)
            whole = _remote(w, o.at[own], sems, 7 * wi + 6, sib)
            ici, d2d, d2d_in = [], [], []
            for j, (tx, ty) in enumerate(chips):
                t = 2 * tx + ty
                ici.append(_remote(w.at[mine], o.at[own, mine], sems, 7 * wi + j, (tx, ty, c)))
                d2d.append(_remote(o.at[t, mine], o.at[t, mine], sems, 7 * wi + 3 + j, sib))
                d2d_in.append(_remote(o.at[t, theirs], o.at[t, theirs], sems, 7 * wi + 3 + j, sib))
            plan.append((whole, ici, d2d, d2d_in))
        return plan

    def begin(self, ins, outs, sems):
        for whole, ici, _, _ in self._plan(ins, outs, sems):
            whole.start()
            for cp in ici:
                cp.start()

    def middle(self, ins, outs, sems):
        for _, ici, d2d, _ in self._plan(ins, outs, sems):
            for cp_in, cp_on in zip(ici, d2d):
                cp_in.wait_recv()
                cp_on.start()

    def finish(self, ins, outs, sems):
        for whole, ici, d2d, d2d_in in self._plan(ins, outs, sems):
            for cp in d2d_in:
                cp.wait_recv()
            for cp in ici + d2d:
                cp.wait_send()
            whole.wait()


class _Swap:
    def __init__(self, grads):
        n = len(grads)
        self.arrays = list(grads)
        self.out_shapes = [SDS((4,) + g.shape[2:], g.dtype) for g in grads]
        self.scratch = [pltpu.SemaphoreType.DMA((4 * n,)), pltpu.SemaphoreType.DMA((4 * n,))]

    def _plan(self, ins, outs, sems):
        x, y, c, _ = _place()
        return [_remote(g.at[t, 1 - c], o.at[t], sems, 4 * wi + t, (x, y, 1 - c))
                for wi, (g, o) in enumerate(zip(ins, outs)) for t in range(4)]

    def begin(self, ins, outs, sems):
        for cp in self._plan(ins, outs, sems):
            cp.start()

    def middle(self, ins, outs, sems):
        pass

    def finish(self, ins, outs, sems):
        for cp in self._plan(ins, outs, sems):
            cp.wait()


class _Exchange:
    def __init__(self, pieces):
        n = len(pieces)
        self.arrays = list(pieces)
        self.out_shapes = [SDS((3,) + p.shape[1:], p.dtype) for p in pieces]
        self.scratch = [pltpu.SemaphoreType.DMA((3 * n,)), pltpu.SemaphoreType.DMA((3 * n,))]

    def _plan(self, ins, outs, sems):
        x, y, c, chips = _place()
        return [_remote(g.at[2 * tx + ty], o.at[j], sems, 3 * wi + j, (tx, ty, c))
                for wi, (g, o) in enumerate(zip(ins, outs)) for j, (tx, ty) in enumerate(chips)]

    def begin(self, ins, outs, sems):
        for cp in self._plan(ins, outs, sems):
            cp.start()

    def middle(self, ins, outs, sems):
        pass

    def finish(self, ins, outs, sems):
        for cp in self._plan(ins, outs, sems):
            cp.wait()


class _Share:
    def __init__(self, totals):
        n = len(totals)
        self.arrays = list(totals)
        self.out_shapes = [SDS(t.shape, t.dtype) for t in totals]
        self.scratch = [pltpu.SemaphoreType.DMA((n,)), pltpu.SemaphoreType.DMA((n,))]

    def _plan(self, ins, outs, sems):
        x, y, c, _ = _place()
        return [_remote(t, o, sems, wi, (x, y, 1 - c)) for wi, (t, o) in enumerate(zip(ins, outs))]

    def begin(self, ins, outs, sems):
        for cp in self._plan(ins, outs, sems):
            cp.start()

    def middle(self, ins, outs, sems):
        pass

    def finish(self, ins, outs, sems):
        for cp in self._plan(ins, outs, sems):
            cp.wait()


def _ride_begin(rider, r_refs, step):
    @pl.when(step == 0)
    def _():
        rider.begin(*r_refs)


def _ride_end(rider, r_refs, step, nsteps):
    @pl.when(step == min(3 * nsteps // 4, nsteps - 1))
    def _():
        rider.middle(*r_refs)

    @pl.when(step == nsteps - 1)
    def _():
        rider.finish(*r_refs)


def _comm(rider, name):
    n_in, n_out = len(rider.arrays), len(rider.out_shapes)

    def body(*refs):
        r_refs = (refs[:n_in], refs[n_in:n_in + n_out], refs[n_in + n_out:])
        rider.begin(*r_refs)
        rider.middle(*r_refs)
        rider.finish(*r_refs)

    return pl.pallas_call(body, name=name, out_shape=rider.out_shapes, in_specs=[_ANY] * n_in,
                          out_specs=[_ANY] * n_out, scratch_shapes=rider.scratch)(*rider.arrays)


def _small_allreduce(pk, name):
    r = pk.shape[0]
    rels = [(dx, dy, dc) for dx in (0, 1) for dy in (0, 1) for dc in (0, 1) if dx or dy or dc]

    def body(p_ref, o_ref, buf, send_sems, recv_sems):
        x, y, c, _ = _place()
        me = 4 * x + 2 * y + c
        buf[me] = p_ref[...]
        cps = []
        for k, (dx, dy, dc) in enumerate(rels):
            to = (1 - x if dx else x, 1 - y if dy else y, 1 - c if dc else c)
            cps.append(pltpu.make_async_remote_copy(src_ref=p_ref, dst_ref=buf.at[me], send_sem=send_sems.at[k],
                                                    recv_sem=recv_sems.at[k], device_id=to,
                                                    device_id_type=MESH))
        for cpy in cps:
            cpy.start()
        for cpy in cps:
            cpy.wait()
        acc = buf[0]
        for d in range(1, 8):
            acc = acc + buf[d]
        o_ref[...] = acc

    vm = pl.BlockSpec(memory_space=pltpu.VMEM)
    return pl.pallas_call(body, name=name, out_shape=SDS(pk.shape, F32), in_specs=[vm], out_specs=vm,
                          scratch_shapes=[pltpu.VMEM((8, r, LANE), F32), pltpu.SemaphoreType.DMA((7,)),
                                          pltpu.SemaphoreType.DMA((7,))])(pk)


ATTN_W = ("w_in", "w_uq", "w_ukv", "w_out")
FFN_W = ("w_gate", "w_up", "w_down")
BIG = ATTN_W + FFN_W


def _cols_from_chips(g):
    return jnp.concatenate([g[t] for t in range(4)], axis=1)


def _cols_to_chips(full):
    r, n = full.shape
    return full.reshape(r, 4, n // 4).transpose(1, 0, 2).reshape(4, 2, r // 2, n // 4)


def _rows_to_chips(full):
    n, c = full.shape
    return full.reshape(4, 2, n // 8, c)


def _permute_w_in(w, nh):
    d = w.shape[0]
    g = 4 * nh * HEAD
    lr = (w.shape[1] - g - 2 * nh - ROPE) // 2
    o = g + 2 * nh
    pad = jnp.zeros((d, LANE - ROPE - 8 - nh), w.dtype)
    pad8 = jnp.zeros((d, 8 - nh), w.dtype)
    return jnp.concatenate([w[:, :g], w[:, o:o + 2 * lr], w[:, o + 2 * lr:], w[:, g:g + nh], pad8,
                            w[:, g + nh:g + 2 * nh], pad, jnp.zeros((d, LANE), w.dtype)], axis=1)


def _unpermute_w_in(wp, nh, lr):
    g = 4 * nh * HEAD
    mc = g + 2 * lr
    return jnp.concatenate([wp[:, :g], wp[:, mc + B_LANE:mc + B_LANE + nh], wp[:, mc + A_LANE:mc + A_LANE + nh],
                            wp[:, g:g + 2 * lr], wp[:, mc:mc + ROPE]], axis=1)


def _permute_w_uq(w, nh):
    lr = w.shape[0]
    w3 = w.reshape(lr, nh, HEAD + ROPE)
    return jnp.concatenate([w3, jnp.zeros((lr, nh, HEAD - ROPE), w.dtype)], axis=2).reshape(lr, nh * 2 * HEAD)


def _unpermute_w_uq(wp, nh):
    lr = wp.shape[0]
    return wp.reshape(lr, nh, 2 * HEAD)[:, :, :HEAD + ROPE].reshape(lr, nh * (HEAD + ROPE))


def _permute_w_ukv(w, nh):
    lr = w.shape[0]
    w3 = w.reshape(lr, nh, 2 * HEAD)
    kp = jnp.concatenate([w3[:, :, :HEAD], jnp.zeros((lr, nh, HEAD), w.dtype)], axis=2)
    return jnp.concatenate([kp.reshape(lr, nh * 2 * HEAD), w3[:, :, HEAD:].reshape(lr, nh * HEAD)], axis=1)


def _unpermute_w_ukv(wp, nh):
    lr = wp.shape[0]
    kp = wp[:, :nh * 2 * HEAD].reshape(lr, nh, 2 * HEAD)[:, :, :HEAD]
    vp = wp[:, nh * 2 * HEAD:].reshape(lr, nh, HEAD)
    return jnp.concatenate([kp, vp], axis=2).reshape(lr, nh * 2 * HEAD)


def _reduce_begin(grads, place, tag):
    recv = _comm(_Swap(grads), "swap_" + tag)
    sums = [_sum_pair(g, r, place, "sum_pair_%s%d" % (tag, k)) for k, (g, r) in enumerate(zip(grads, recv))]
    return [s[0] for s in sums], [s[1] for s in sums]


def _reduce_end(own, recv, tag):
    return [_sum_chips(o, r, "sum_chips_%s%d" % (tag, k)) for k, (o, r) in enumerate(zip(own, recv))]


def _step(x, pos, tgt, wt, ffn_shards, small, place):
    nh = small["a_log"].shape[1]
    lr = small["q_norm_w"].shape[1]
    w = nh * HEAD
    z_col, col_q, col_kv = 3, 4 * w // lr, 4 * w // lr + 1
    misc_c = 4 * w + 2 * lr
    misc_col = misc_c // LANE
    assert (4 * w) % lr == 0 and small["kv_norm_w"].shape[1] == lr

    win_p = _permute_w_in(wt["w_in"], nh)
    wuq_p = _permute_w_uq(wt["w_uq"], nh)
    wukv_p = _permute_w_ukv(wt["w_ukv"], nh)
    zl = jnp.zeros((1, LANE), F32)
    alog_l = zl.at[:, A_LANE:A_LANE + nh].set(small["a_log"])
    dtb_l = zl.at[:, A_LANE:A_LANE + nh].set(small["dt_bias"])
    conv_w = small["conv_w"]

    h1 = _norm_fwd(x, small["attn_norm_w"], "norm1")
    proj = _mm([(h1, win_p)], name="proj_in")
    gq, gk, gv, gb, gbt = _gdn_prep(proj, conv_w, alog_l, dtb_l, nh, misc_col)
    o_gdn, states, (wg4, wu4, wd4) = _gdn_fwd(gq, gk, gv, gb, gbt, nh, _Gather(ffn_shards))
    w_down = wd4.reshape(-1, wd4.shape[2])
    cqn, ckvn = _mla_norm(proj, small["q_norm_w"], small["kv_norm_w"], col_q, col_kv)
    qraw = _mm([(cqn, wuq_p)], name="proj_uq")
    kvraw = _mm([(ckvn, wukv_p)], name="proj_ukv")
    qc, kc, vv = _mla_rope(qraw, kvraw, proj, pos, nh, misc_col)
    o_mla, lse = _mla_fwd(qc, kc, vv, nh)
    mixed = _mix_fwd(o_gdn, proj, o_mla, small["gdn_norm_w"], small["mla_out_norm_w"], nh, z_col)
    x2 = _mm([(mixed, wt["w_out"])], name="proj_out", res=x)
    h2 = _norm_fwd(x2, small["ffn_norm_w"], "norm2")
    act, gpre, upre = _swiglu_fwd(h2, wg4, wu4)
    x3 = _mm([(act, w_down)], name="proj_down", res=x2)
    dx3, d_final, loss, dx3h = _final_loss(x3, tgt, small["final_norm_w"])

    gs = {"final_norm_w": d_final}
    dgate, dup = _swiglu_bwd(dx3h, w_down, gpre, upre)
    g_down = _rows_to_chips(_mm([(act, dx3h)], name="dw_down", ta=True, out_dtype=BF16))
    dh2 = _mm([(dgate, wg4), (dup, wu4)], name="dh2", tb=True, b_chips=True)
    g_gate = _mm([(h2, dgate)], name="dw_gate", ta=True, out_dtype=BF16, out_chips=True)
    g_up = _mm([(h2, dup)], name="dw_up", ta=True, out_dtype=BF16, out_chips=True)
    halves = lambda g: g.reshape(4, 2, g.shape[1] // 2, g.shape[2])
    ffn16, ffn_own = _reduce_begin([halves(g_gate), halves(g_up), g_down], place, "ffn")
    dx2, gs["ffn_norm_w"], dx2h = _norm_bwd(dh2, x2, small["ffn_norm_w"], dx3, "norm2_bwd", True)
    dmix = _mm([(dx2h, wt["w_out"])], name="dmix", tb=True)
    g_out = _rows_to_chips(_mm([(mixed, dx2h)], name="dw_out", ta=True, out_dtype=BF16))
    d_ogdn, dz, d_omla, gs["gdn_norm_w"], gs["mla_out_norm_w"], delta = _mix_bwd(
        dmix, o_gdn, proj, o_mla, small["gdn_norm_w"], small["mla_out_norm_w"], nh, z_col)
    dqc, dkc, dvv, ffn_recv = _mla_bwd(qc, kc, vv, d_omla, lse, delta, nh, _Exchange(ffn16))
    ffn_tot = _reduce_end(ffn_own, ffn_recv, "ffn")
    dqraw, dkvraw, dkr = _mla_rope_bwd(dqc, dkc, dvv, pos, nh)
    dcqn = _mm([(dqraw, wuq_p)], name="dcqn", tb=True)
    dckvn = _mm([(dkvraw, wukv_p)], name="dckvn", tb=True)
    g_uq = _cols_to_chips(_unpermute_w_uq(_mm([(cqn, dqraw)], name="dw_uq", ta=True, out_dtype=BF16), nh))
    g_ukv = _cols_to_chips(_unpermute_w_ukv(_mm([(ckvn, dkvraw)], name="dw_ukv", ta=True, out_dtype=BF16), nh))
    dcq, dckv, gs["q_norm_w"], gs["kv_norm_w"] = _mla_norm_bwd(
        proj, small["q_norm_w"], small["kv_norm_w"], dcqn, dckvn, col_q, col_kv)
    dgq, dgk, dgv, dgb = _gdn_bwd(gq, gk, gv, gb, gbt, states, d_ogdn, nh)
    dconv, dmisc, gs["conv_w"], dal, ddb = _gdn_prep_bwd(
        proj, conv_w, alog_l, dtb_l, dgq, dgk, dgv, dgb, dkr, nh, misc_col)
    gs["a_log"] = dal[:, A_LANE:A_LANE + nh]
    gs["dt_bias"] = ddb[:, A_LANE:A_LANE + nh]
    dqkv = _conv_bwd_input(dconv, conv_w)
    dproj = jnp.concatenate([dqkv, dz, dcq, dckv, dmisc, jnp.zeros((x.shape[0], LANE), BF16)], axis=1)
    g_in = _cols_to_chips(_unpermute_w_in(_mm([(h1, dproj)], name="dw_in", ta=True, out_dtype=BF16), nh, lr))
    att16, att_own = _reduce_begin([g_in, g_uq, g_ukv, g_out], place, "att")
    dh1, att_recv = _mm([(dproj, win_p)], name="dh1", tb=True, rider=_Exchange(att16))
    att_tot = _reduce_end(att_own, att_recv, "att")
    grad_x, gs["attn_norm_w"] = _norm_bwd(dh1, x, small["attn_norm_w"], dx2, "norm1_bwd", False)
    return loss, grad_x, att_tot + ffn_tot, gs


SMALL = ("attn_norm_w", "ffn_norm_w", "final_norm_w", "q_norm_w", "kv_norm_w", "gdn_norm_w",
         "mla_out_norm_w", "a_log", "dt_bias")
WEIGHTS = ("attn_norm_w", "w_in", "conv_w", "a_log", "dt_bias", "gdn_norm_w", "q_norm_w", "w_uq",
           "kv_norm_w", "w_ukv", "mla_out_norm_w", "w_out", "ffn_norm_w", "w_gate", "w_up", "w_down",
           "final_norm_w")


def _pack_small(vecs):
    flat = jnp.concatenate([v.astype(F32).reshape(-1) for v in vecs])
    pad = (-flat.shape[0]) % (8 * LANE)
    return jnp.concatenate([flat, jnp.zeros((pad,), F32)]).reshape(-1, LANE)


def kernel(x, positions, attn_norm_w, w_in, conv_w, a_log, dt_bias, gdn_norm_w, q_norm_w, w_uq, kv_norm_w, w_ukv, mla_out_norm_w, w_out, ffn_norm_w, w_gate, w_up, w_down, final_norm_w, loss_target, m_attn_norm_w, m_w_in, m_conv_w, m_a_log, m_dt_bias, m_gdn_norm_w, m_q_norm_w, m_w_uq, m_kv_norm_w, m_w_ukv, m_mla_out_norm_w, m_w_out, m_ffn_norm_w, m_w_gate, m_w_up, m_w_down, m_final_norm_w, v_attn_norm_w, v_w_in, v_conv_w, v_a_log, v_dt_bias, v_gdn_norm_w, v_q_norm_w, v_w_uq, v_kv_norm_w, v_w_ukv, v_mla_out_norm_w, v_w_out, v_ffn_norm_w, v_w_gate, v_w_up, v_w_down, v_final_norm_w):
    args = dict(locals())
    xi, yi, ci = lax.axis_index("x"), lax.axis_index("y"), lax.axis_index("c")
    chip = 2 * xi + yi

    def two_d(a):
        return a.reshape(a.shape[-2:]) if a.ndim >= 2 else a.reshape(1, -1)

    wloc = {n: two_d(args[n]) for n in WEIGHTS}
    mloc = {n: two_d(args["m_" + n]) for n in WEIGHTS}
    vloc = {n: two_d(args["v_" + n]) for n in WEIGHTS}

    ga = _comm(_Gather([wloc[n].astype(BF16) for n in ATTN_W]), "gather_attn")
    wt = {"w_in": _cols_from_chips(ga[0]), "w_uq": _cols_from_chips(ga[1]), "w_ukv": _cols_from_chips(ga[2]),
          "w_out": ga[3].reshape(-1, ga[3].shape[2])}
    cw = wloc["conv_w"]
    cshard = cw.shape[1]
    cfull = jnp.zeros((CONV, 4 * cshard), F32)
    cfull = lax.dynamic_update_slice(cfull, jnp.where(ci == 0, cw, 0.0), (0, chip * cshard))
    conv_full = _small_allreduce(_pack_small([cfull]), "gather_conv_w").reshape(-1)[:CONV * 4 * cshard]
    conv_full = conv_full.reshape(CONV, 4 * cshard)

    small = {n: wloc[n] for n in SMALL}
    small["conv_w"] = conv_full

    pos = positions.reshape(-1, 1).astype(F32)
    place = jnp.stack([ci, chip]).astype(jnp.int32)
    loss, grad_x, totals, gs = _step(two_d(x), pos, two_d(loss_target), wt,
                                     [wloc[n].astype(BF16) for n in FFN_W], small, place)
    from_sib = _comm(_Share(totals), "share_halves")

    small_names = SMALL + ("conv_w",)
    pk = _pack_small([gs[n] for n in small_names] + [loss])
    red = _small_allreduce(pk, "reduce_small").reshape(-1)
    gsm, off = {}, 0
    for n in small_names:
        shp = gs[n].shape
        gsm[n] = red[off:off + shp[0] * shp[1]].reshape(shp)
        off += shp[0] * shp[1]
    loss_out = red[off]
    gsm["conv_w"] = lax.dynamic_slice(gsm["conv_w"], (0, chip * cshard), (CONV, cshard))

    grads, deltas, new_m, new_v = {}, {}, {}, {}
    for n, mine, theirs in zip(BIG, totals, from_sib):
        grads[n], deltas[n], new_m[n], new_v[n] = _adamw_halves(wloc[n], mine, theirs, mloc[n], vloc[n], place,
                                                                "adamw_" + n)
    grads["conv_w"] = gsm["conv_w"]
    deltas["conv_w"], new_m["conv_w"], new_v["conv_w"] = _adamw(wloc["conv_w"], gsm["conv_w"], mloc["conv_w"],
                                                                vloc["conv_w"], "adamw_conv_w")
    sm_shapes = [wloc[n].shape for n in SMALL]
    pd, pm, pv = _adamw(_pack_small([wloc[n] for n in SMALL]), _pack_small([gsm[n] for n in SMALL]),
                        _pack_small([mloc[n] for n in SMALL]), _pack_small([vloc[n] for n in SMALL]),
                        "adamw_small")
    for dst, packed in ((deltas, pd), (new_m, pm), (new_v, pv)):
        flat, off = packed.reshape(-1), 0
        for n, shp in zip(SMALL, sm_shapes):
            dst[n] = flat[off:off + shp[0] * shp[1]].reshape(shp)
            off += shp[0] * shp[1]
    for n in SMALL:
        grads[n] = gsm[n]

    def like(n, a):
        return a.reshape(args[n].shape)

    outs = [loss_out.reshape(()), grad_x.reshape(x.shape)]
    for group in (grads, deltas, new_m, new_v):
        outs += [like(n, group[n]) for n in WEIGHTS]
    return tuple(outs)
```

```python
import functools

import jax
import jax.numpy as jnp
from jax import lax
from jax.experimental import pallas as pl
from jax.experimental.pallas import tpu as pltpu

F32, BF16 = jnp.float32, jnp.bfloat16
SDS = jax.ShapeDtypeStruct
MESH = pl.DeviceIdType.MESH

HEAD = 128
ROPE = 64
CHUNK = 64
PAIR = 2 * CHUNK
CONV = 4
EPS = 1e-6
ROPE_THETA = 10000.0
LANE = 128
B_LANE = 64
A_LANE = 72
VMEM_LIMIT = 48 * 1024 * 1024
VMEM_LIMIT_WIDE = 56 * 1024 * 1024
MLA_BLOCK = 512
LOG2E = 1.4426950408889634
LN2 = 0.6931471805599453
SM_SCALE = (HEAD + ROPE) ** -0.5

ADAM_LR = 0.001
ADAM_B1 = 0.9
ADAM_B2 = 0.999
ADAM_EPS = 1e-08
ADAM_WD = 0.01
ADAM_STEP = 10


def _tile(n, pref, mult=LANE):
    if n <= pref:
        return n
    t = (pref // mult) * mult
    while t >= mult:
        if n % t == 0:
            return t
        t -= mult
    return n


def _pcall(body, *, name, grid, in_specs, out_specs, out_shape, scratch=(), vmem=VMEM_LIMIT):
    return pl.pallas_call(
        body, name=name, grid=grid, in_specs=in_specs, out_specs=out_specs,
        out_shape=out_shape, scratch_shapes=list(scratch),
        compiler_params=pltpu.CompilerParams(
            dimension_semantics=("arbitrary",) * len(grid), vmem_limit_bytes=vmem))


def _rows(ts, width, col=0):
    return pl.BlockSpec((ts, width), lambda i: (i, col))


def _full(shape):
    nd = len(shape)
    return pl.BlockSpec(shape, lambda i: (0,) * nd)


def _dot(a, b):
    return jnp.dot(a.astype(BF16), b.astype(BF16), preferred_element_type=F32)


def _dot_nt(a, b):
    return lax.dot_general(a.astype(BF16), b.astype(BF16), (((1,), (1,)), ((), ())),
                           preferred_element_type=F32)


def _dot_tn(a, b):
    return lax.dot_general(a.astype(BF16), b.astype(BF16), (((0,), (0,)), ((), ())),
                           preferred_element_type=F32)


def _sigmoid(x):
    return 1.0 / (1.0 + jnp.exp(-x))


def _silu(x):
    return x * _sigmoid(x)


def _dsilu(x):
    s = _sigmoid(x)
    return s * (1.0 + x * (1.0 - s))


def _lane_iota(shape):
    return lax.broadcasted_iota(jnp.int32, shape, len(shape) - 1)


def _col(block, idx):
    return jnp.sum(jnp.where(_lane_iota(block.shape) == idx, block, 0.0), axis=-1, keepdims=True)


def _mm(pairs, *, name, ta=False, tb=False, out_dtype=F32, res=None, tm=1024, tn=1024, tk=2048,
        b_chips=False, out_chips=False, rider=None):
    a0, b0 = pairs[0]
    if ta:
        kdim, m = a0.shape
    else:
        m, kdim = a0.shape
    if b_chips and tb:
        n, tk = b0.shape[1], b0.shape[2]
        assert kdim == 4 * tk
    elif b_chips:
        n, tn = 4 * b0.shape[2], b0.shape[2]
        assert kdim == b0.shape[1]
    else:
        n = b0.shape[0] if tb else b0.shape[1]
    if out_chips:
        tn = n // 4
    tm = _tile(m, tm)
    tn = tn if (out_chips or (b_chips and not tb)) else _tile(n, tn)
    tk = tk if (b_chips and tb) else _tile(kdim, tk)
    assert m % tm == 0 and n % tn == 0 and kdim % tk == 0
    nk, npair = kdim // tk, len(pairs)
    grid = (m // tm, n // tn, nk)
    dims = (((0 if ta else 1,), (1 if tb else 0,)), ((), ()))
    n_in = 2 * npair + (res is not None)
    r_in, r_out = (len(rider.arrays), len(rider.out_shapes)) if rider else (0, 0)

    def body(*refs):
        o_ref = refs[n_in + r_in]
        acc = refs[n_in + r_in + 1 + r_out]
        k = pl.program_id(2)
        if rider:
            r_refs = (refs[n_in:n_in + r_in], refs[n_in + r_in + 1:n_in + r_in + 1 + r_out],
                      refs[n_in + r_in + 2 + r_out:])
            step = (pl.program_id(0) * grid[1] + pl.program_id(1)) * nk + k
            _ride_begin(rider, r_refs, step)

        @pl.when(k == 0)
        def _():
            acc[...] = jnp.zeros_like(acc)

        tot = None
        for p in range(npair):
            d = lax.dot_general(refs[2 * p][...].astype(BF16), refs[2 * p + 1][...].astype(BF16),
                                dims, preferred_element_type=F32)
            tot = d if tot is None else tot + d
        acc[...] += tot

        @pl.when(k == nk - 1)
        def _():
            r = acc[...]
            if res is not None:
                r = r + refs[2 * npair][...]
            o_ref[...] = r.astype(out_dtype)

        if rider:
            _ride_end(rider, r_refs, step, grid[0] * grid[1] * nk)

    if ta:
        a_spec = pl.BlockSpec((tk, tm), lambda i, j, k: (k, i))
    else:
        a_spec = pl.BlockSpec((tm, tk), lambda i, j, k: (i, k))
    if b_chips and tb:
        b_spec = pl.BlockSpec((None, tn, tk), lambda i, j, k: (k, j, 0))
    elif b_chips:
        b_spec = pl.BlockSpec((None, tk, tn), lambda i, j, k: (j, k, 0))
    elif tb:
        b_spec = pl.BlockSpec((tn, tk), lambda i, j, k: (j, k))
    else:
        b_spec = pl.BlockSpec((tk, tn), lambda i, j, k: (k, j))
    if out_chips:
        o_spec = pl.BlockSpec((None, tm, tn), lambda i, j, k: (j, i, 0))
        o_shape = SDS((4, m, tn), out_dtype)
    else:
        o_spec = pl.BlockSpec((tm, tn), lambda i, j, k: (i, j))
        o_shape = SDS((m, n), out_dtype)
    in_specs, args = [], []
    for a, b in pairs:
        in_specs += [a_spec, b_spec]
        args += [a, b]
    if res is not None:
        in_specs.append(o_spec)
        args.append(res)
    out_specs, out_shapes, scratch = [o_spec], [o_shape], [pltpu.VMEM((tm, tn), F32)]
    if rider:
        in_specs += [_ANY] * r_in
        args += rider.arrays
        out_specs += [_ANY] * r_out
        out_shapes += rider.out_shapes
        scratch += rider.scratch
    outs = _pcall(body, name=name, grid=grid, in_specs=in_specs, out_specs=out_specs, out_shape=out_shapes,
                  scratch=scratch)(*args)
    return (outs[0], outs[1:]) if rider else outs[0]


def _norm_fwd(x, w, name):
    s, d = x.shape
    ts = _tile(s, 512, 8)

    def body(x_ref, w_ref, h_ref):
        xv = x_ref[...]
        r = lax.rsqrt(jnp.mean(xv * xv, axis=-1, keepdims=True) + EPS)
        h_ref[...] = (xv * r * w_ref[...]).astype(BF16)

    return _pcall(body, name=name, grid=(s // ts,), in_specs=[_rows(ts, d), _full((1, d))],
                  out_specs=_rows(ts, d), out_shape=SDS((s, d), BF16))(x, w)


def _norm_bwd(dh, x, w, dres, name, with_bf16):
    s, d = x.shape
    ts = _tile(s, 256, 8)

    def body(dh_ref, x_ref, w_ref, dres_ref, dx_ref, dw_ref, *dx16_ref):
        @pl.when(pl.program_id(0) == 0)
        def _():
            dw_ref[...] = jnp.zeros_like(dw_ref)

        xv, dhv = x_ref[...], dh_ref[...]
        r = lax.rsqrt(jnp.mean(xv * xv, axis=-1, keepdims=True) + EPS)
        xh = xv * r
        dw_ref[...] += jnp.sum(dhv * xh, axis=0, keepdims=True)
        dxh = dhv * w_ref[...]
        dx = dres_ref[...] + r * (dxh - xh * jnp.mean(dxh * xh, axis=-1, keepdims=True))
        dx_ref[...] = dx
        for ref in dx16_ref:
            ref[...] = dx.astype(BF16)

    extra = 1 if with_bf16 else 0
    return _pcall(body, name=name, grid=(s // ts,),
                  in_specs=[_rows(ts, d), _rows(ts, d), _full((1, d)), _rows(ts, d)],
                  out_specs=[_rows(ts, d), _full((1, d))] + [_rows(ts, d)] * extra,
                  out_shape=[SDS((s, d), F32), SDS((1, d), F32)] + [SDS((s, d), BF16)] * extra)(
                      dh, x, w, dres)


def _final_loss(x3, tgt, w):
    s, d = x3.shape
    ts = _tile(s, 256, 8)

    def body(x_ref, t_ref, w_ref, dx_ref, dw_ref, loss_ref, dx16_ref):
        @pl.when(pl.program_id(0) == 0)
        def _():
            dw_ref[...] = jnp.zeros_like(dw_ref)
            loss_ref[...] = jnp.zeros_like(loss_ref)

        xv, wv = x_ref[...], w_ref[...]
        r = lax.rsqrt(jnp.mean(xv * xv, axis=-1, keepdims=True) + EPS)
        xh = xv * r
        err = xh * wv - t_ref[...]
        row = jnp.mean(err * err, axis=-1, keepdims=True)
        loss_ref[...] += 0.5 * jnp.sum(row, axis=0, keepdims=True)
        dy = err * (1.0 / d)
        dw_ref[...] += jnp.sum(dy * xh, axis=0, keepdims=True)
        dxh = dy * wv
        dx = r * (dxh - xh * jnp.mean(dxh * xh, axis=-1, keepdims=True))
        dx_ref[...] = dx
        dx16_ref[...] = dx.astype(BF16)

    return _pcall(body, name="final_loss", grid=(s // ts,),
                  in_specs=[_rows(ts, d), _rows(ts, d), _full((1, d))],
                  out_specs=[_rows(ts, d), _full((1, d)), _full((1, 1)), _rows(ts, d)],
                  out_shape=[SDS((s, d), F32), SDS((1, d), F32), SDS((1, 1), F32), SDS((s, d), BF16)])(
                      x3, tgt, w)


def _shift_down(cur, halo, s):
    if s == 0:
        return cur
    row8 = lax.broadcasted_iota(jnp.int32, halo.shape, 0)
    r = pltpu.roll(cur, s, 0)
    top = jnp.where(row8 < s, pltpu.roll(halo, s, 0), r[0:8])
    return jnp.concatenate([top, r[8:]], axis=0)


def _shift_up(cur, halo, s):
    if s == 0:
        return cur
    ts = cur.shape[0]
    row8 = lax.broadcasted_iota(jnp.int32, halo.shape, 0)
    r = pltpu.roll(cur, ts - s, 0)
    bot = jnp.where(row8 >= 8 - s, pltpu.roll(halo, 8 - s, 0), r[ts - 8:ts])
    return jnp.concatenate([r[:ts - 8], bot], axis=0)


def _chunk_tri(ts, upper):
    i = lax.broadcasted_iota(jnp.int32, (ts, ts), 0)
    j = lax.broadcasted_iota(jnp.int32, (ts, ts), 1)
    same = jnp.right_shift(i, 6) == jnp.right_shift(j, 6)
    return jnp.where(same & ((j >= i) if upper else (j <= i)), 1.0, 0.0).astype(F32)


def _gate_values(m, alog, dtb):
    lane = _lane_iota(m.shape)
    beta = _sigmoid(m)
    xg = m + dtb
    sp = jnp.maximum(xg, 0.0) + jnp.log(1.0 + jnp.exp(-jnp.abs(xg)))
    ga = (lane >= A_LANE) & (lane < A_LANE + 8)
    g = jnp.where(ga, -jnp.exp(alog) * sp, 0.0)
    return beta, g, xg, ga


def _l2_heads(a, nh, scale):
    outs, rs = [], []
    for h in range(nh):
        ah = a[:, HEAD * h:HEAD * (h + 1)]
        r = lax.rsqrt(jnp.sum(ah * ah, axis=-1, keepdims=True) + EPS)
        outs.append(ah * (r * scale))
        rs.append(r)
    return jnp.concatenate(outs, axis=-1), rs


def _gdn_prep(proj, conv_w, alog_l, dtb_l, nh, misc_col):
    s = proj.shape[0]
    w = nh * HEAD
    ts = _tile(s, 256, PAIR)
    hb = ts // 8

    def body(cur_ref, halo_ref, misc_ref, cw_ref, al_ref, db_ref, q_ref, k_ref, v_ref, gb_ref, gbt_ref):
        first = pl.program_id(0) == 0
        outs = (q_ref, k_ref, v_ref)
        for sec in range(3):
            cs = slice(sec * w, (sec + 1) * w)
            cur = cur_ref[:, cs]
            halo = jnp.where(first, 0.0, halo_ref[:, cs])
            pre = None
            for j in range(CONV):
                term = cw_ref[j:j + 1, cs] * _shift_down(cur, halo, CONV - 1 - j)
                pre = term if pre is None else pre + term
            act = _silu(pre)
            if sec == 0:
                act, _ = _l2_heads(act, nh, HEAD ** -0.5)
            elif sec == 1:
                act, _ = _l2_heads(act, nh, 1.0)
            outs[sec][...] = act
        m = misc_ref[...]
        lane = _lane_iota(m.shape)
        beta, g, _, ga = _gate_values(m, al_ref[...], db_ref[...])
        gcc = jnp.dot(_chunk_tri(ts, False), g, precision=lax.Precision.HIGHEST,
                      preferred_element_type=F32)
        gb = jnp.where((lane >= B_LANE) & (lane < B_LANE + 8), beta, jnp.where(ga, gcc, 0.0))
        gb_ref[...] = gb
        gbt_ref[...] = gb.T

    return _pcall(
        body, name="gdn_prep", grid=(s // ts,),
        in_specs=[_rows(ts, 3 * w),
                  pl.BlockSpec((8, 3 * w), lambda i: (jnp.maximum(i * hb - 1, 0), 0)),
                  _rows(ts, LANE, misc_col), _full((CONV, 3 * w)), _full((1, LANE)), _full((1, LANE))],
        out_specs=[_rows(ts, w), _rows(ts, w), _rows(ts, w), _rows(ts, LANE),
                   pl.BlockSpec((LANE, ts), lambda i: (0, i))],
        out_shape=[SDS((s, w), F32), SDS((s, w), F32), SDS((s, w), F32), SDS((s, LANE), F32),
                   SDS((LANE, s), F32)])(proj, proj, proj, conv_w, alog_l, dtb_l)


def _gdn_prep_bwd(proj, conv_w, alog_l, dtb_l, dq, dk, dv, dgb, dkr, nh, misc_col):
    s = proj.shape[0]
    w = nh * HEAD
    ts = _tile(s, 256, PAIR)
    hb = ts // 8

    def body(cur_ref, halo_ref, misc_ref, cw_ref, al_ref, db_ref, dq_ref, dk_ref, dv_ref, dgb_ref,
             dkr_ref, dc_ref, dm_ref, dcw_ref, dal_ref, ddb_ref):
        first = pl.program_id(0) == 0

        @pl.when(first)
        def _():
            dcw_ref[...] = jnp.zeros_like(dcw_ref)
            dal_ref[...] = jnp.zeros_like(dal_ref)
            ddb_ref[...] = jnp.zeros_like(ddb_ref)

        dins = (dq_ref, dk_ref, dv_ref)
        for sec in range(3):
            cs = slice(sec * w, (sec + 1) * w)
            cur = cur_ref[:, cs]
            halo = jnp.where(first, 0.0, halo_ref[:, cs])
            us = [_shift_down(cur, halo, CONV - 1 - j) for j in range(CONV)]
            pre = None
            for j in range(CONV):
                term = cw_ref[j:j + 1, cs] * us[j]
                pre = term if pre is None else pre + term
            act = _silu(pre)
            dout = dins[sec][...]
            if sec < 2:
                scale = HEAD ** -0.5 if sec == 0 else 1.0
                parts = []
                for h in range(nh):
                    hs = slice(HEAD * h, HEAD * (h + 1))
                    ah = act[:, hs]
                    r = lax.rsqrt(jnp.sum(ah * ah, axis=-1, keepdims=True) + EPS)
                    ahat = ah * r
                    dy = dout[:, hs]
                    parts.append((scale * r) * (dy - ahat * jnp.sum(dy * ahat, axis=-1, keepdims=True)))
                dact = jnp.concatenate(parts, axis=-1)
            else:
                dact = dout
            dconv = dact * _dsilu(pre)
            dc_ref[:, cs] = dconv
            for j in range(CONV):
                dcw_ref[j:j + 1, cs] += jnp.sum(dconv * us[j], axis=0, keepdims=True)
        m = misc_ref[...]
        lane = _lane_iota(m.shape)
        al = al_ref[...]
        beta, g, xg, ga = _gate_values(m, al, db_ref[...])
        dgbv = dgb_ref[...]
        dg = jnp.dot(_chunk_tri(ts, True), jnp.where(ga, dgbv, 0.0), precision=lax.Precision.HIGHEST,
                     preferred_element_type=F32)
        da_raw = jnp.where(ga, dg * (-jnp.exp(al)) * _sigmoid(xg), 0.0)
        db_raw = jnp.where((lane >= B_LANE) & (lane < B_LANE + 8), dgbv * beta * (1.0 - beta), 0.0)
        dal_ref[...] += jnp.sum(dg * g, axis=0, keepdims=True)
        ddb_ref[...] += jnp.sum(da_raw, axis=0, keepdims=True)
        dm_ref[...] = (dkr_ref[...] + da_raw + db_raw).astype(BF16)

    return _pcall(
        body, name="gdn_prep_bwd", grid=(s // ts,),
        in_specs=[_rows(ts, 3 * w),
                  pl.BlockSpec((8, 3 * w), lambda i: (jnp.maximum(i * hb - 1, 0), 0)),
                  _rows(ts, LANE, misc_col), _full((CONV, 3 * w)), _full((1, LANE)), _full((1, LANE)),
                  _rows(ts, w), _rows(ts, w), _rows(ts, w), _rows(ts, LANE), _rows(ts, LANE)],
        out_specs=[_rows(ts, 3 * w), _rows(ts, LANE), _full((CONV, 3 * w)), _full((1, LANE)),
                   _full((1, LANE))],
        out_shape=[SDS((s, 3 * w), F32), SDS((s, LANE), BF16), SDS((CONV, 3 * w), F32),
                   SDS((1, LANE), F32), SDS((1, LANE), F32)])(
                       proj, proj, proj, conv_w, alog_l, dtb_l, dq, dk, dv, dgb, dkr)


def _conv_bwd_input(dconv, conv_w):
    s, c = dconv.shape
    ts = _tile(s, 256, 8)
    hb = ts // 8
    nblk8 = s // 8
    nt = s // ts

    def body(cur_ref, nxt_ref, cw_ref, o_ref):
        last = pl.program_id(0) == nt - 1
        cur = cur_ref[...]
        halo = jnp.where(last, 0.0, nxt_ref[...])
        acc = None
        for j in range(CONV):
            term = cw_ref[j:j + 1, :] * _shift_up(cur, halo, CONV - 1 - j)
            acc = term if acc is None else acc + term
        o_ref[...] = acc.astype(BF16)

    return _pcall(
        body, name="conv_bwd_input", grid=(nt,),
        in_specs=[_rows(ts, c),
                  pl.BlockSpec((8, c), lambda i: (jnp.minimum((i + 1) * hb, nblk8 - 1), 0)),
                  _full((CONV, c))],
        out_specs=_rows(ts, c), out_shape=SDS((s, c), BF16))(dconv, dconv, conv_w)


def _inv_unit_lower(a):
    n = a[0].shape[0]
    i = lax.broadcasted_iota(jnp.int32, (n, n), 0)
    j = lax.broadcasted_iota(jnp.int32, (n, n), 1)
    eye = jnp.where(i == j, 1.0, 0.0)
    t = [eye - ah for ah in a]
    x = a
    for _ in range(5):
        x = [_dot(xh, xh) for xh in x]
        t = [th + _dot(th, xh) for th, xh in zip(t, x)]
    return t


def _pair_common(q, k, gcol, grow, bcol):
    i = lax.broadcasted_iota(jnp.int32, (PAIR, PAIR), 0)
    j = lax.broadcasted_iota(jnp.int32, (PAIR, PAIR), 1)
    same = jnp.right_shift(i, 6) == jnp.right_shift(j, 6)
    tril = same & (i >= j)
    strict = same & (i > j)
    dec = [jnp.where(tril, jnp.exp(jnp.minimum(gc - gr, 0.0)), 0.0) for gc, gr in zip(gcol, grow)]
    kk = [_dot_nt(kh, kh) for kh in k]
    qk = [_dot_nt(qh, kh) for qh, kh in zip(q, k)]
    a = [jnp.where(strict, b * kkh * d, 0.0) for b, kkh, d in zip(bcol, kk, dec)]
    t = _inv_unit_lower(a)
    p = [qkh * d for qkh, d in zip(qk, dec)]
    return dec, kk, a, t, p, tril, strict


def _ext(v, a):
    z = jnp.zeros_like(v)
    return jnp.concatenate([v, z] if a == 0 else [z, v], axis=0)


def _gdn_fwd(q, k, v, gb, gbt, nh):
    s = q.shape[0]
    w = nh * HEAD
    npair = s // PAIR

    def body(q_ref, k_ref, v_ref, gb_ref, gbt_ref, o_ref, st_ref, s_ref):
        @pl.when(pl.program_id(0) == 0)
        def _():
            s_ref[...] = jnp.zeros_like(s_ref)

        heads = range(nh)
        hs = [slice(HEAD * h, HEAD * (h + 1)) for h in heads]
        gbv = gb_ref[...]
        q, k, v = [q_ref[:, s_] for s_ in hs], [k_ref[:, s_] for s_ in hs], [v_ref[:, s_] for s_ in hs]
        gcol = [_col(gbv, A_LANE + h) for h in heads]
        bcol = [_col(gbv, B_LANE + h) for h in heads]
        grow = [gbt_ref[A_LANE + h:A_LANE + h + 1, :] for h in heads]
        _, _, _, t, p, _, _ = _pair_common(q, k, gcol, grow, bcol)
        eg = [jnp.exp(gc) for gc in gcol]
        qg = [x * e for x, e in zip(q, eg)]
        kg = [x * e for x, e in zip(k, eg)]
        outs = []
        for a in range(2):
            sl = slice(CHUNK * a, CHUNK * (a + 1))
            st = [s_ref[h] for h in heads]
            for h in heads:
                st_ref[a, h] = st[h]
            r = [v[h][sl] - _dot(kg[h][sl], st[h]) for h in heads]
            vn = [_dot(t[h][sl], _ext(bcol[h][sl] * r[h], a)) for h in heads]
            outs.append([_dot(qg[h][sl], st[h]) + _dot(p[h][sl], _ext(vn[h], a)) for h in heads])
            gl = [_col(gr, CHUNK * (a + 1) - 1) for gr in grow]
            kd = [k[h][sl] * jnp.exp(gl[h] - gcol[h][sl]) for h in heads]
            upd = [_dot_tn(kd[h], vn[h]) for h in heads]
            for h in heads:
                s_ref[h] = jnp.exp(gl[h]) * st[h] + upd[h]
        for h in heads:
            o_ref[:, hs[h]] = jnp.concatenate([outs[0][h], outs[1][h]], axis=0)

    return _pcall(
        body, name="gdn_fwd", grid=(npair,),
        in_specs=[_rows(PAIR, w), _rows(PAIR, w), _rows(PAIR, w), _rows(PAIR, LANE),
                  pl.BlockSpec((LANE, PAIR), lambda i: (0, i))],
        out_specs=[_rows(PAIR, w), pl.BlockSpec((2, nh, HEAD, HEAD), lambda i: (i, 0, 0, 0))],
        out_shape=[SDS((s, w), F32), SDS((2 * npair, nh, HEAD, HEAD), F32)],
        scratch=[pltpu.VMEM((nh, HEAD, HEAD), F32)])(q, k, v, gb, gbt)


def _gdn_bwd(q, k, v, gb, gbt, states, do, nh):
    s = q.shape[0]
    w = nh * HEAD
    npair = s // PAIR
    rev = lambda i: (npair - 1 - i, 0)

    def body(q_ref, k_ref, v_ref, gb_ref, gbt_ref, st_ref, do_ref, dq_ref, dk_ref, dv_ref, dgb_ref,
             ds_ref):
        @pl.when(pl.program_id(0) == 0)
        def _():
            ds_ref[...] = jnp.zeros_like(ds_ref)

        lane = _lane_iota((PAIR, LANE))
        row = lax.broadcasted_iota(jnp.int32, (CHUNK, 1), 0)
        heads = range(nh)
        hs = [slice(HEAD * h, HEAD * (h + 1)) for h in heads]
        gbv = gb_ref[...]
        q, k, v = [q_ref[:, s_] for s_ in hs], [k_ref[:, s_] for s_ in hs], [v_ref[:, s_] for s_ in hs]
        do = [do_ref[:, s_] for s_ in hs]
        gcol = [_col(gbv, A_LANE + h) for h in heads]
        bcol = [_col(gbv, B_LANE + h) for h in heads]
        grow = [gbt_ref[A_LANE + h:A_LANE + h + 1, :] for h in heads]
        dec, kk, amat, t, p, tril, strict = _pair_common(q, k, gcol, grow, bcol)
        tt, pt = [x.T for x in t], [x.T for x in p]
        eg = [jnp.exp(gc) for gc in gcol]
        qg = [x * e for x, e in zip(q, eg)]
        kg = [x * e for x, e in zip(k, eg)]
        sums = lambda x: jnp.sum(x, axis=-1, keepdims=True)
        rs, vns = [None, None], [None, None]
        for a in range(2):
            sl = slice(CHUNK * a, CHUNK * (a + 1))
            rs[a] = [v[h][sl] - _dot(kg[h][sl], st_ref[a, h]) for h in heads]
            vns[a] = [_dot(t[h][sl], _ext(bcol[h][sl] * rs[a][h], a)) for h in heads]
        dsn = [ds_ref[h] for h in heads]
        dqs, dks, dvs, dgcs, dbs, drbs = ([None, None] for _ in range(6))
        for a in (1, 0):
            sl = slice(CHUNK * a, CHUNK * (a + 1))
            st = [st_ref[a, h] for h in heads]
            gl = [_col(gr, CHUNK * (a + 1) - 1) for gr in grow]
            egl = [jnp.exp(x) for x in gl]
            dk_dec = [jnp.exp(gl[h] - gcol[h][sl]) for h in heads]
            kd = [k[h][sl] * dk_dec[h] for h in heads]
            d_vn = [_dot(pt[h][sl], _ext(do[h][sl], a)) + _dot(kd[h], dsn[h]) for h in heads]
            d_qg = [_dot_nt(do[h][sl], st[h]) for h in heads]
            d_rb = [_dot(tt[h][sl], _ext(d_vn[h], a)) for h in heads]
            d_r = [bcol[h][sl] * d_rb[h] for h in heads]
            d_kg = [-_dot_nt(d_r[h], st[h]) for h in heads]
            d_kd = [_dot_nt(vns[a][h], dsn[h]) for h in heads]
            dsn_new = [_dot_tn(qg[h][sl], do[h][sl]) - _dot_tn(kg[h][sl], d_r[h]) for h in heads]
            dbs[a] = [sums(d_rb[h] * rs[a][h]) for h in heads]
            dgl = [egl[h] * jnp.sum(dsn[h] * st[h], keepdims=True) + jnp.sum(d_kd[h] * kd[h], keepdims=True)
                   for h in heads]
            dgcs[a] = [sums(d_qg[h] * qg[h][sl]) + sums(d_kg[h] * kg[h][sl]) - sums(d_kd[h] * kd[h])
                       + jnp.where(row == CHUNK - 1, dgl[h], 0.0) for h in heads]
            dqs[a] = [d_qg[h] * eg[h][sl] for h in heads]
            dks[a] = [d_kg[h] * eg[h][sl] + d_kd[h] * dk_dec[h] for h in heads]
            dvs[a] = d_r
            drbs[a] = d_rb
            dsn = [dsn_new[h] + egl[h] * dsn[h] for h in heads]
        for h in heads:
            ds_ref[h] = dsn[h]
        cat = lambda xs, h: jnp.concatenate([xs[0][h], xs[1][h]], axis=0)
        vn = [cat(vns, h) for h in heads]
        d_rb = [cat(drbs, h) for h in heads]
        dp = [jnp.where(tril, _dot_nt(do[h], vn[h]), 0.0) for h in heads]
        dam = [jnp.where(strict, -_dot_nt(d_rb[h], vn[h]), 0.0) for h in heads]
        g_p = [dp[h] * dec[h] for h in heads]
        g_a = [dam[h] * dec[h] for h in heads]
        gbk = [bcol[h] * g_a[h] for h in heads]
        dq2 = [_dot(g_p[h], k[h]) for h in heads]
        dk2 = [_dot_tn(g_p[h], q[h]) + _dot(gbk[h], k[h]) + _dot_tn(gbk[h], k[h]) for h in heads]
        dgb = jnp.zeros((PAIR, LANE), F32)
        for h in heads:
            dq_ref[:, hs[h]] = cat(dqs, h) + dq2[h]
            dk_ref[:, hs[h]] = cat(dks, h) + dk2[h]
            dv_ref[:, hs[h]] = cat(dvs, h)
            dbeta = cat(dbs, h) + sums(g_a[h] * kk[h])
            mm = dp[h] * p[h] + dam[h] * amat[h]
            dgc = cat(dgcs, h) + sums(mm) - sums(mm.T)
            dgb = dgb + jnp.where(lane == A_LANE + h, dgc, 0.0) + jnp.where(lane == B_LANE + h, dbeta, 0.0)
        dgb_ref[...] = dgb

    return _pcall(
        body, name="gdn_bwd", grid=(npair,),
        in_specs=[pl.BlockSpec((PAIR, w), rev), pl.BlockSpec((PAIR, w), rev), pl.BlockSpec((PAIR, w), rev),
                  pl.BlockSpec((PAIR, LANE), rev),
                  pl.BlockSpec((LANE, PAIR), lambda i: (0, npair - 1 - i)),
                  pl.BlockSpec((2, nh, HEAD, HEAD), lambda i: (npair - 1 - i, 0, 0, 0)),
                  pl.BlockSpec((PAIR, w), rev)],
        out_specs=[pl.BlockSpec((PAIR, w), rev), pl.BlockSpec((PAIR, w), rev), pl.BlockSpec((PAIR, w), rev),
                   pl.BlockSpec((PAIR, LANE), rev)],
        out_shape=[SDS((s, w), F32), SDS((s, w), F32), SDS((s, w), F32), SDS((s, LANE), F32)],
        scratch=[pltpu.VMEM((nh, HEAD, HEAD), F32)])(q, k, v, gb, gbt, states, do)


def _mla_norm(proj, qw, kvw, col_q, col_kv):
    s = proj.shape[0]
    lr = qw.shape[1]
    ts = _tile(s, 512, 8)

    def body(cq_ref, ckv_ref, qw_ref, kvw_ref, oq_ref, okv_ref):
        for x_ref, w_ref, o_ref in ((cq_ref, qw_ref, oq_ref), (ckv_ref, kvw_ref, okv_ref)):
            xv = x_ref[...]
            r = lax.rsqrt(jnp.mean(xv * xv, axis=-1, keepdims=True) + EPS)
            o_ref[...] = (xv * r * w_ref[...]).astype(BF16)

    return _pcall(body, name="mla_norm", grid=(s // ts,),
                  in_specs=[_rows(ts, lr, col_q), _rows(ts, lr, col_kv), _full((1, lr)), _full((1, lr))],
                  out_specs=[_rows(ts, lr), _rows(ts, lr)],
                  out_shape=[SDS((s, lr), BF16), SDS((s, lr), BF16)])(proj, proj, qw, kvw)


def _mla_norm_bwd(proj, qw, kvw, dq, dkv, col_q, col_kv):
    s = proj.shape[0]
    lr = qw.shape[1]
    ts = _tile(s, 512, 8)

    def body(cq_ref, ckv_ref, qw_ref, kvw_ref, dq_ref, dkv_ref, oq_ref, okv_ref, dqw_ref, dkvw_ref):
        @pl.when(pl.program_id(0) == 0)
        def _():
            dqw_ref[...] = jnp.zeros_like(dqw_ref)
            dkvw_ref[...] = jnp.zeros_like(dkvw_ref)

        for x_ref, w_ref, d_ref, o_ref, dw_ref in ((cq_ref, qw_ref, dq_ref, oq_ref, dqw_ref),
                                                    (ckv_ref, kvw_ref, dkv_ref, okv_ref, dkvw_ref)):
            xv, dh = x_ref[...], d_ref[...]
            r = lax.rsqrt(jnp.mean(xv * xv, axis=-1, keepdims=True) + EPS)
            xh = xv * r
            dw_ref[...] += jnp.sum(dh * xh, axis=0, keepdims=True)
            dxh = dh * w_ref[...]
            o_ref[...] = (r * (dxh - xh * jnp.mean(dxh * xh, axis=-1, keepdims=True))).astype(BF16)

    return _pcall(body, name="mla_norm_bwd", grid=(s // ts,),
                  in_specs=[_rows(ts, lr, col_q), _rows(ts, lr, col_kv), _full((1, lr)), _full((1, lr)),
                            _rows(ts, lr), _rows(ts, lr)],
                  out_specs=[_rows(ts, lr), _rows(ts, lr), _full((1, lr)), _full((1, lr))],
                  out_shape=[SDS((s, lr), BF16), SDS((s, lr), BF16), SDS((1, lr), F32),
                             SDS((1, lr), F32)])(proj, proj, qw, kvw, dq, dkv)


def _rope_tables(pos, invf, sgn):
    ang = pos * invf
    return jnp.cos(ang), jnp.sin(ang) * sgn


def _swap_halves_lanes(y):
    lane = _lane_iota(y.shape)
    return jnp.where(lane < ROPE // 2, pltpu.roll(y, LANE - ROPE // 2, 1), pltpu.roll(y, ROPE // 2, 1))


def _rope_consts():
    half = ROPE // 2
    inv = ROPE_THETA ** (-jnp.arange(half, dtype=F32) / half)
    invf = jnp.concatenate([inv, inv, jnp.zeros((LANE - ROPE,), F32)])[None, :]
    sgn = jnp.concatenate([-jnp.ones((half,), F32), jnp.ones((half,), F32),
                           jnp.zeros((LANE - ROPE,), F32)])[None, :]
    return invf, sgn


def _mla_rope(qraw, kvraw, proj, pos, nh, misc_col):
    s = qraw.shape[0]
    ts = _tile(s, 256, 8)
    wq = nh * 2 * HEAD
    invf, sgn = _rope_consts()

    def body(q_ref, kv_ref, misc_ref, pos_ref, if_ref, sg_ref, qc_ref, kc_ref, v_ref):
        c, sn = _rope_tables(pos_ref[...], if_ref[...], sg_ref[...])
        lane = _lane_iota(c.shape)
        rot = lambda xb: xb * c + _swap_halves_lanes(xb) * sn
        qs = SM_SCALE * LOG2E
        krot = jnp.where(lane < ROPE, rot(misc_ref[...]), 0.0).astype(BF16)
        for h in range(nh):
            b0 = 2 * HEAD * h
            qc_ref[:, b0:b0 + HEAD] = (q_ref[:, b0:b0 + HEAD] * qs).astype(BF16)
            qc_ref[:, b0 + HEAD:b0 + 2 * HEAD] = (rot(q_ref[:, b0 + HEAD:b0 + 2 * HEAD]) * qs).astype(BF16)
            kc_ref[:, b0:b0 + HEAD] = kv_ref[:, b0:b0 + HEAD].astype(BF16)
            kc_ref[:, b0 + HEAD:b0 + 2 * HEAD] = krot
        v_ref[...] = kv_ref[:, wq:].astype(BF16)

    return _pcall(body, name="mla_rope", grid=(s // ts,),
                  in_specs=[_rows(ts, wq), _rows(ts, wq + nh * HEAD), _rows(ts, LANE, misc_col),
                            _rows(ts, 1), _full((1, LANE)), _full((1, LANE))],
                  out_specs=[_rows(ts, wq), _rows(ts, wq), _rows(ts, nh * HEAD)],
                  out_shape=[SDS((s, wq), BF16), SDS((s, wq), BF16), SDS((s, nh * HEAD), BF16)])(
                      qraw, kvraw, proj, pos, invf, sgn)


def _mla_rope_bwd(dqc, dkc, dv, pos, nh):
    s = dqc.shape[0]
    ts = _tile(s, 256, 8)
    wq = nh * 2 * HEAD
    invf, sgn = _rope_consts()

    def body(dq_ref, dk_ref, dv_ref, pos_ref, if_ref, sg_ref, oq_ref, okv_ref, okr_ref):
        c, sn = _rope_tables(pos_ref[...], if_ref[...], sg_ref[...])
        lane = _lane_iota(c.shape)
        unrot = lambda d: d * c + _swap_halves_lanes(d * sn)
        dkr = jnp.zeros(c.shape, F32)
        for h in range(nh):
            b0 = 2 * HEAD * h
            oq_ref[:, b0:b0 + HEAD] = (dq_ref[:, b0:b0 + HEAD] * SM_SCALE).astype(BF16)
            oq_ref[:, b0 + HEAD:b0 + 2 * HEAD] = (
                unrot(dq_ref[:, b0 + HEAD:b0 + 2 * HEAD]) * SM_SCALE).astype(BF16)
            okv_ref[:, b0:b0 + HEAD] = (dk_ref[:, b0:b0 + HEAD] * LN2).astype(BF16)
            okv_ref[:, b0 + HEAD:b0 + 2 * HEAD] = jnp.zeros((ts, HEAD), BF16)
            dkr = dkr + dk_ref[:, b0 + HEAD:b0 + 2 * HEAD]
        okv_ref[:, wq:] = dv_ref[...].astype(BF16)
        okr_ref[...] = jnp.where(lane < ROPE, unrot(jnp.where(lane < ROPE, dkr * LN2, 0.0)), 0.0)

    return _pcall(body, name="mla_rope_bwd", grid=(s // ts,),
                  in_specs=[_rows(ts, wq), _rows(ts, wq), _rows(ts, nh * HEAD), _rows(ts, 1),
                            _full((1, LANE)), _full((1, LANE))],
                  out_specs=[_rows(ts, wq), _rows(ts, wq + nh * HEAD), _rows(ts, LANE)],
                  out_shape=[SDS((s, wq), BF16), SDS((s, wq + nh * HEAD), BF16), SDS((s, LANE), F32)])(
                      dqc, dkc, dv, pos, invf, sgn)


def _causal_mask(blk):
    i = lax.broadcasted_iota(jnp.int32, (blk, blk), 0)
    j = lax.broadcasted_iota(jnp.int32, (blk, blk), 1)
    return j <= i


MLA_HP = 2


def _pair_pack(a, b):
    return jnp.where(_lane_iota(a.shape) < LANE // 2, a, b)


def _pair_unpack(x, e):
    lane = _lane_iota(x.shape)
    keep = (lane < LANE // 2) if e == 0 else (lane >= LANE // 2)
    return jnp.where(keep, x, pltpu.roll(x, LANE // 2, 1))


def _mla_fwd(qc, kc, v, nh, rider):
    s = qc.shape[0]
    blk = _tile(s, MLA_BLOCK)
    nb = s // blk
    rep = blk // LANE
    hp = MLA_HP
    assert nh % hp == 0
    once = pl.Buffered(1)
    r_in, r_out = len(rider.arrays), len(rider.out_shapes)

    def body(*refs):
        q_ref, k_ref, v_ref = refs[:3]
        o_ref, lse_ref = refs[3 + r_in:5 + r_in]
        m_sc, l_sc, acc = refs[5 + r_in + r_out:8 + r_in + r_out]
        r_refs = (refs[3:3 + r_in], refs[5 + r_in:5 + r_in + r_out], refs[8 + r_in + r_out:])
        i = pl.program_id(1)
        grid_step = pl.program_id(0) * nb + i
        _ride_begin(rider, r_refs, grid_step)
        m_sc[...] = jnp.full_like(m_sc, -1e30)
        l_sc[...] = jnp.zeros_like(l_sc)
        acc[...] = jnp.zeros_like(acc)

        def step(j, masked):
            rows = pl.ds(pl.multiple_of(j * blk, blk), blk)
            es = range(hp)
            sc = [_dot_nt(q_ref[:, 2 * HEAD * e:2 * HEAD * (e + 1)], k_ref[rows, 2 * HEAD * e:2 * HEAD * (e + 1)])
                  for e in es]
            if masked:
                sc = [jnp.where(_causal_mask(blk), x, -1e30) for x in sc]
            m_prev = [m_sc[e] for e in es]
            m_new = [jnp.maximum(m_prev[e], jnp.max(sc[e], axis=-1, keepdims=True)) for e in es]
            p = [jnp.exp2(sc[e] - jnp.tile(m_new[e], (1, rep))) for e in es]
            alpha = [jnp.exp2(m_prev[e] - m_new[e]) for e in es]
            pv = [_dot(p[e], v_ref[rows, HEAD * e:HEAD * (e + 1)]) for e in es]
            for e in es:
                l_sc[e] = alpha[e] * l_sc[e] + jnp.sum(p[e], axis=-1, keepdims=True)
                acc[e] = alpha[e] * acc[e] + pv[e]
                m_sc[e] = m_new[e]

        def loop_body(j, carry):
            step(j, False)
            return carry

        lax.fori_loop(0, i, loop_body, 0)
        step(i, True)
        for e in range(hp):
            o_ref[:, HEAD * e:HEAD * (e + 1)] = acc[e] / l_sc[e]
        lse = [m_sc[e] + jnp.log(l_sc[e]) * LOG2E for e in range(hp)]
        lse_ref[...] = _pair_pack(lse[0], lse[1])
        _ride_end(rider, r_refs, grid_step, (nh // hp) * nb)

    outs = _pcall(
        body, name="mla_fwd", grid=(nh // hp, nb),
        in_specs=[pl.BlockSpec((blk, hp * 2 * HEAD), lambda g, i: (i, g)),
                  pl.BlockSpec((s, hp * 2 * HEAD), lambda g, i: (0, g), pipeline_mode=once),
                  pl.BlockSpec((s, hp * HEAD), lambda g, i: (0, g), pipeline_mode=once)] + [_ANY] * r_in,
        out_specs=[pl.BlockSpec((blk, hp * HEAD), lambda g, i: (i, g)),
                   pl.BlockSpec((None, blk, LANE), lambda g, i: (g, i, 0))] + [_ANY] * r_out,
        out_shape=[SDS((s, nh * HEAD), F32), SDS((nh // hp, s, LANE), F32)] + rider.out_shapes,
        scratch=[pltpu.VMEM((hp, blk, LANE), F32), pltpu.VMEM((hp, blk, LANE), F32),
                 pltpu.VMEM((hp, blk, HEAD), F32)] + rider.scratch)(qc, kc, v, *rider.arrays)
    return outs[0], outs[1], outs[2:]


def _mla_bwd(qc, kc, v, do, lse, delta, nh, rider):
    s = qc.shape[0]
    blk = _tile(s, MLA_BLOCK)
    nb = s // blk
    rep = blk // LANE
    hp = MLA_HP
    once = pl.Buffered(1)
    r_in, r_out = len(rider.arrays), len(rider.out_shapes)
    qs = [slice(2 * HEAD * e, 2 * HEAD * (e + 1)) for e in range(hp)]
    vs = [slice(HEAD * e, HEAD * (e + 1)) for e in range(hp)]

    def body(*refs):
        q_ref, do_ref, lse_ref, dl_ref, k_ref, v_ref = refs[:6]
        dq_ref, dk_ref, dv_ref = refs[6 + r_in:9 + r_in]
        dk_acc, dv_acc = refs[9 + r_in + r_out:11 + r_in + r_out]
        r_refs = (refs[6:6 + r_in], refs[9 + r_in:9 + r_in + r_out], refs[11 + r_in + r_out:])
        j = pl.program_id(1)
        grid_step = pl.program_id(0) * nb + j
        _ride_begin(rider, r_refs, grid_step)

        @pl.when(j == 0)
        def _():
            dq_ref[...] = jnp.zeros_like(dq_ref)

        dk_acc[...] = jnp.zeros_like(dk_acc)
        dv_acc[...] = jnp.zeros_like(dv_acc)
        es = range(hp)
        kj = [k_ref[:, qs[e]] for e in es]
        vj = [v_ref[:, vs[e]] for e in es]

        def step(i, masked):
            rows = pl.ds(pl.multiple_of(i * blk, blk), blk)
            qi = [q_ref[rows, qs[e]] for e in es]
            doi = [do_ref[rows, vs[e]] for e in es]
            lse, dl = lse_ref[rows, :], dl_ref[rows, :]
            sc = [_dot_nt(qi[e], kj[e]) for e in es]
            dp = [_dot_nt(doi[e], vj[e]) for e in es]
            if masked:
                sc = [jnp.where(_causal_mask(blk), x, -1e30) for x in sc]
            p = [jnp.exp2(sc[e] - jnp.tile(_pair_unpack(lse, e), (1, rep))) for e in es]
            ds = [p[e] * (dp[e] - jnp.tile(_pair_unpack(dl, e), (1, rep))) for e in es]
            dv = [_dot_tn(p[e], doi[e]) for e in es]
            dk = [_dot_tn(ds[e], qi[e]) for e in es]
            dq = [_dot(ds[e], kj[e]) for e in es]
            for e in es:
                dv_acc[:, vs[e]] += dv[e]
                dk_acc[:, qs[e]] += dk[e]
                dq_ref[rows, qs[e]] += dq[e]

        def loop_body(i, carry):
            step(i, False)
            return carry

        step(j, True)
        lax.fori_loop(j + 1, nb, loop_body, 0)
        dk_ref[...] = dk_acc[...]
        dv_ref[...] = dv_acc[...]
        _ride_end(rider, r_refs, grid_step, (nh // hp) * nb)

    outs = _pcall(
        body, name="mla_bwd", grid=(nh // hp, nb),
        in_specs=[pl.BlockSpec((s, hp * 2 * HEAD), lambda g, j: (0, g), pipeline_mode=once),
                  pl.BlockSpec((s, hp * HEAD), lambda g, j: (0, g), pipeline_mode=once),
                  pl.BlockSpec((None, s, LANE), lambda g, j: (g, 0, 0), pipeline_mode=once),
                  pl.BlockSpec((None, s, LANE), lambda g, j: (g, 0, 0), pipeline_mode=once),
                  pl.BlockSpec((blk, hp * 2 * HEAD), lambda g, j: (j, g)),
                  pl.BlockSpec((blk, hp * HEAD), lambda g, j: (j, g))] + [_ANY] * r_in,
        out_specs=[pl.BlockSpec((s, hp * 2 * HEAD), lambda g, j: (0, g), pipeline_mode=once),
                   pl.BlockSpec((blk, hp * 2 * HEAD), lambda g, j: (j, g)),
                   pl.BlockSpec((blk, hp * HEAD), lambda g, j: (j, g))] + [_ANY] * r_out,
        out_shape=[SDS((s, nh * 2 * HEAD), F32), SDS((s, nh * 2 * HEAD), F32),
                   SDS((s, nh * HEAD), F32)] + rider.out_shapes,
        scratch=[pltpu.VMEM((blk, hp * 2 * HEAD), F32), pltpu.VMEM((blk, hp * HEAD), F32)] + rider.scratch,
        vmem=VMEM_LIMIT_WIDE)(qc, do, lse, delta, kc, v, *rider.arrays)
    return outs[0], outs[1], outs[2], outs[3:]


def _mix_fwd(og, proj, om, gw, mw, nh, z_col):
    s = og.shape[0]
    w = nh * HEAD
    ts = _tile(s, 256, 8)

    def body(og_ref, z_ref, om_ref, gw_ref, mw_ref, o_ref):
        for h in range(nh):
            hs = slice(HEAD * h, HEAD * (h + 1))
            a = og_ref[:, hs]
            r = lax.rsqrt(jnp.mean(a * a, axis=-1, keepdims=True) + EPS)
            o_ref[:, hs] = (a * r * gw_ref[...] * _silu(z_ref[:, hs])).astype(BF16)
            b = om_ref[:, hs]
            r = lax.rsqrt(jnp.mean(b * b, axis=-1, keepdims=True) + EPS)
            o_ref[:, w + HEAD * h:w + HEAD * (h + 1)] = (b * r * mw_ref[...]).astype(BF16)

    return _pcall(body, name="mix_fwd", grid=(s // ts,),
                  in_specs=[_rows(ts, w), _rows(ts, w, z_col), _rows(ts, w), _full((1, HEAD)),
                            _full((1, HEAD))],
                  out_specs=_rows(ts, 2 * w), out_shape=SDS((s, 2 * w), BF16))(og, proj, om, gw, mw)


def _mix_bwd(dmix, og, proj, om, gw, mw, nh, z_col):
    s = og.shape[0]
    w = nh * HEAD
    ts = _tile(s, 256, 8)

    def body(d_ref, og_ref, z_ref, om_ref, gw_ref, mw_ref, dog_ref, dz_ref, dom_ref, dgw_ref, dmw_ref,
             dl_ref):
        @pl.when(pl.program_id(0) == 0)
        def _():
            dgw_ref[...] = jnp.zeros_like(dgw_ref)
            dmw_ref[...] = jnp.zeros_like(dmw_ref)

        dgw = jnp.zeros((1, HEAD), F32)
        dmw = jnp.zeros((1, HEAD), F32)
        deltas = []
        for h in range(nh):
            hs = slice(HEAD * h, HEAD * (h + 1))
            a, z, dy = og_ref[:, hs], z_ref[:, hs], d_ref[:, hs]
            r = lax.rsqrt(jnp.mean(a * a, axis=-1, keepdims=True) + EPS)
            ah = a * r
            sz = _silu(z)
            dz_ref[:, hs] = (dy * (ah * gw_ref[...]) * _dsilu(z)).astype(BF16)
            dn = dy * sz
            dgw = dgw + jnp.sum(dn * ah, axis=0, keepdims=True)
            dah = dn * gw_ref[...]
            dog_ref[:, hs] = r * (dah - ah * jnp.mean(dah * ah, axis=-1, keepdims=True))
            b, dyb = om_ref[:, hs], d_ref[:, w + HEAD * h:w + HEAD * (h + 1)]
            r = lax.rsqrt(jnp.mean(b * b, axis=-1, keepdims=True) + EPS)
            bh = b * r
            dmw = dmw + jnp.sum(dyb * bh, axis=0, keepdims=True)
            dbh = dyb * mw_ref[...]
            dom = r * (dbh - bh * jnp.mean(dbh * bh, axis=-1, keepdims=True))
            dom_ref[:, hs] = dom.astype(BF16)
            deltas.append(jnp.broadcast_to(jnp.sum(dom * b, axis=-1, keepdims=True), (ts, LANE)))
        for g in range(nh // MLA_HP):
            dl_ref[g] = _pair_pack(deltas[2 * g], deltas[2 * g + 1])
        dgw_ref[...] += dgw
        dmw_ref[...] += dmw

    return _pcall(body, name="mix_bwd", grid=(s // ts,),
                  in_specs=[_rows(ts, 2 * w), _rows(ts, w), _rows(ts, w, z_col), _rows(ts, w),
                            _full((1, HEAD)), _full((1, HEAD))],
                  out_specs=[_rows(ts, w), _rows(ts, w), _rows(ts, w), _full((1, HEAD)), _full((1, HEAD)),
                             pl.BlockSpec((nh // MLA_HP, ts, LANE), lambda i: (0, i, 0))],
                  out_shape=[SDS((s, w), F32), SDS((s, w), BF16), SDS((s, w), BF16), SDS((1, HEAD), F32),
                             SDS((1, HEAD), F32), SDS((nh // MLA_HP, s, LANE), F32)])(dmix, og, proj, om, gw, mw)


def _swiglu_fwd(h2, wg, wu):
    m, kdim = h2.shape
    tn = wg.shape[2]
    n = 4 * tn
    tm, tk = _tile(m, 512), _tile(kdim, 2048)
    nk = kdim // tk

    def body(a_ref, g_ref, u_ref, act_ref, go_ref, uo_ref, gacc, uacc):
        k = pl.program_id(2)

        @pl.when(k == 0)
        def _():
            gacc[...] = jnp.zeros_like(gacc)
            uacc[...] = jnp.zeros_like(uacc)

        a = a_ref[...]
        gacc[...] += _dot(a, g_ref[...])
        uacc[...] += _dot(a, u_ref[...])

        @pl.when(k == nk - 1)
        def _():
            g, u = gacc[...], uacc[...]
            act_ref[...] = (_silu(g) * u).astype(BF16)
            go_ref[...] = g.astype(BF16)
            uo_ref[...] = u.astype(BF16)

    a_spec = pl.BlockSpec((tm, tk), lambda i, j, k: (i, k))
    b_spec = pl.BlockSpec((None, tk, tn), lambda i, j, k: (j, k, 0))
    o_spec = pl.BlockSpec((tm, tn), lambda i, j, k: (i, j))
    return _pcall(body, name="swiglu_fwd", grid=(m // tm, n // tn, nk),
                  in_specs=[a_spec, b_spec, b_spec], out_specs=[o_spec] * 3,
                  out_shape=[SDS((m, n), BF16)] * 3,
                  scratch=[pltpu.VMEM((tm, tn), F32), pltpu.VMEM((tm, tn), F32)])(h2, wg, wu)


def _swiglu_bwd(dx3, wd, g, u):
    m, kdim = dx3.shape
    n = wd.shape[0]
    tm, tn, tk = _tile(m, 1024), _tile(n, 512), _tile(kdim, 2048)
    nk = kdim // tk

    def body(a_ref, b_ref, g_ref, u_ref, dg_ref, du_ref, acc):
        k = pl.program_id(2)

        @pl.when(k == 0)
        def _():
            acc[...] = jnp.zeros_like(acc)

        acc[...] += _dot_nt(a_ref[...], b_ref[...])

        @pl.when(k == nk - 1)
        def _():
            da = acc[...]
            gv, uv = g_ref[...].astype(F32), u_ref[...].astype(F32)
            dg_ref[...] = (da * uv * _dsilu(gv)).astype(BF16)
            du_ref[...] = (da * _silu(gv)).astype(BF16)

    a_spec = pl.BlockSpec((tm, tk), lambda i, j, k: (i, k))
    b_spec = pl.BlockSpec((tn, tk), lambda i, j, k: (j, k))
    o_spec = pl.BlockSpec((tm, tn), lambda i, j, k: (i, j))
    return _pcall(body, name="swiglu_bwd", grid=(m // tm, n // tn, nk),
                  in_specs=[a_spec, b_spec, o_spec, o_spec], out_specs=[o_spec] * 2,
                  out_shape=[SDS((m, n), BF16)] * 2,
                  scratch=[pltpu.VMEM((tm, tn), F32)])(dx3, wd, g, u)


def _sum_pair(g, recv, place, name):
    _, _, rh, c = g.shape
    tr = _tile(rh, 256, 16)

    def body(pl_ref, g_ref, r_ref, o16_ref, own_ref):
        sm = g_ref[...].astype(F32) + r_ref[...].astype(F32)
        o16_ref[...] = sm.astype(BF16)

        @pl.when(pl.program_id(1) == pl_ref[1])
        def _():
            own_ref[...] = sm

    grid_spec = pltpu.PrefetchScalarGridSpec(
        num_scalar_prefetch=1, grid=(rh // tr, 4),
        in_specs=[pl.BlockSpec((None, None, tr, c), lambda i, t, p: (t, p[0], i, 0)),
                  pl.BlockSpec((None, tr, c), lambda i, t, p: (t, i, 0))],
        out_specs=[pl.BlockSpec((None, tr, c), lambda i, t, p: (t, i, 0)),
                   pl.BlockSpec((tr, c), lambda i, t, p: (i, 0))])
    return pl.pallas_call(
        body, name=name, grid_spec=grid_spec,
        out_shape=[SDS((4, rh, c), BF16), SDS((rh, c), F32)],
        compiler_params=pltpu.CompilerParams(dimension_semantics=("arbitrary",) * 2,
                                             vmem_limit_bytes=VMEM_LIMIT))(place, g, recv)


def _sum_chips(own, recv, name):
    rh, c = own.shape
    tr = _tile(rh, 256, 16)

    def body(o_ref, r_ref, out_ref):
        acc = o_ref[...]
        for j in range(3):
            acc = acc + r_ref[j].astype(F32)
        out_ref[...] = acc

    return _pcall(body, name=name, grid=(rh // tr,),
                  in_specs=[_rows(tr, c), pl.BlockSpec((3, tr, c), lambda i: (0, i, 0))],
                  out_specs=_rows(tr, c), out_shape=SDS(own.shape, F32))(own, recv)


def _adamw_update(wv, gv, mv, vv):
    mn = ADAM_B1 * mv + (1.0 - ADAM_B1) * gv
    vn = ADAM_B2 * vv + (1.0 - ADAM_B2) * (gv * gv)
    m_hat = mn / (1.0 - ADAM_B1 ** ADAM_STEP)
    v_hat = vn / (1.0 - ADAM_B2 ** ADAM_STEP)
    return -ADAM_LR * (m_hat / (jnp.sqrt(v_hat) + ADAM_EPS) + ADAM_WD * wv), mn, vn


def _adamw(w, g, m, v, name):
    r, c = w.shape
    tr = _tile(r, 256, 8)

    def body(w_ref, g_ref, m_ref, v_ref, d_ref, mo_ref, vo_ref):
        d_ref[...], mo_ref[...], vo_ref[...] = _adamw_update(w_ref[...], g_ref[...], m_ref[...], v_ref[...])

    spec = _rows(tr, c)
    return _pcall(body, name=name, grid=(r // tr,), in_specs=[spec] * 4, out_specs=[spec] * 3,
                  out_shape=[SDS(w.shape, F32)] * 3)(w, g, m, v)


def _adamw_halves(w, mine, theirs, m, v, place, name):
    r, c = w.shape
    rh = r // 2
    tr = _tile(rh, 256, 8)
    nt = rh // tr

    def body(p_ref, w_ref, a_ref, b_ref, m_ref, v_ref, g_ref, d_ref, mo_ref, vo_ref):
        gv = jnp.where(pl.program_id(0) // nt == p_ref[0], a_ref[...], b_ref[...])
        g_ref[...] = gv
        d_ref[...], mo_ref[...], vo_ref[...] = _adamw_update(w_ref[...], gv, m_ref[...], v_ref[...])

    full = pl.BlockSpec((tr, c), lambda i, p: (i, 0))
    half = pl.BlockSpec((tr, c), lambda i, p: (i % nt, 0))
    grid_spec = pltpu.PrefetchScalarGridSpec(num_scalar_prefetch=1, grid=(2 * nt,),
                                             in_specs=[full, half, half, full, full], out_specs=[full] * 4)
    return pl.pallas_call(
        body, name=name, grid_spec=grid_spec, out_shape=[SDS(w.shape, F32)] * 4,
        compiler_params=pltpu.CompilerParams(dimension_semantics=("arbitrary",),
                                             vmem_limit_bytes=VMEM_LIMIT))(place, w, mine, theirs, m, v)


def _place():
    x, y, c = lax.axis_index("x"), lax.axis_index("y"), lax.axis_index("c")
    chips = [(1 - x, y), (x, 1 - y), (1 - x, 1 - y)]
    return x, y, c, chips


_ANY = pl.BlockSpec(memory_space=pl.ANY)


def _remote(src, dst, sems, k, to):
    return pltpu.make_async_remote_copy(src_ref=src, dst_ref=dst, send_sem=sems[0].at[k], recv_sem=sems[1].at[k],
                                        device_id=to, device_id_type=MESH)


class _Gather:
    def __init__(self, shards):
        n = len(shards)
        self.arrays = list(shards)
        self.out_shapes = [SDS((4,) + a.shape, a.dtype) for a in shards]
        self.scratch = [pltpu.SemaphoreType.DMA((7 * n,)), pltpu.SemaphoreType.DMA((7 * n,))]

    def _plan(self, ins, outs, sems):
        x, y, c, chips = _place()
        own, sib = 2 * x + y, (x, y, 1 - c)
        plan = []
        for wi, (w, o) in enumerate(zip(ins, outs)):
            rh = w.shape[0] // 2
            mine, theirs = pl.ds(c * rh, rh), pl.ds((1 - c) * rh, rh)
            whole = _remote(w, o.at[own], sems, 7 * wi + 6, sib)
            ici, d2d, d2d_in = [], [], []
            for j, (tx, ty) in enumerate(chips):
                t = 2 * tx + ty
                ici.append(_remote(w.at[mine], o.at[own, mine], sems, 7 * wi + j, (tx, ty, c)))
                d2d.append(_remote(o.at[t, mine], o.at[t, mine], sems, 7 * wi + 3 + j, sib))
                d2d_in.append(_remote(o.at[t, theirs], o.at[t, theirs], sems, 7 * wi + 3 + j, sib))
            plan.append((whole, ici, d2d, d2d_in))
        return plan

    def begin(self, ins, outs, sems):
        for whole, ici, _, _ in self._plan(ins, outs, sems):
            whole.start()
            for cp in ici:
                cp.start()

    def middle(self, ins, outs, sems):
        for _, ici, d2d, _ in self._plan(ins, outs, sems):
            for cp_in, cp_on in zip(ici, d2d):
                cp_in.wait_recv()
                cp_on.start()

    def finish(self, ins, outs, sems):
        for whole, ici, d2d, d2d_in in self._plan(ins, outs, sems):
            for cp in d2d_in:
                cp.wait_recv()
            for cp in ici + d2d:
                cp.wait_send()
            whole.wait()


class _Swap:
    def __init__(self, grads):
        n = len(grads)
        self.arrays = list(grads)
        self.out_shapes = [SDS((4,) + g.shape[2:], g.dtype) for g in grads]
        self.scratch = [pltpu.SemaphoreType.DMA((4 * n,)), pltpu.SemaphoreType.DMA((4 * n,))]

    def _plan(self, ins, outs, sems):
        x, y, c, _ = _place()
        return [_remote(g.at[t, 1 - c], o.at[t], sems, 4 * wi + t, (x, y, 1 - c))
                for wi, (g, o) in enumerate(zip(ins, outs)) for t in range(4)]

    def begin(self, ins, outs, sems):
        for cp in self._plan(ins, outs, sems):
            cp.start()

    def middle(self, ins, outs, sems):
        pass

    def finish(self, ins, outs, sems):
        for cp in self._plan(ins, outs, sems):
            cp.wait()


class _Exchange:
    def __init__(self, pieces):
        n = len(pieces)
        self.arrays = list(pieces)
        self.out_shapes = [SDS((3,) + p.shape[1:], p.dtype) for p in pieces]
        self.scratch = [pltpu.SemaphoreType.DMA((3 * n,)), pltpu.SemaphoreType.DMA((3 * n,))]

    def _plan(self, ins, outs, sems):
        x, y, c, chips = _place()
        return [_remote(g.at[2 * tx + ty], o.at[j], sems, 3 * wi + j, (tx, ty, c))
                for wi, (g, o) in enumerate(zip(ins, outs)) for j, (tx, ty) in enumerate(chips)]

    def begin(self, ins, outs, sems):
        for cp in self._plan(ins, outs, sems):
            cp.start()

    def middle(self, ins, outs, sems):
        pass

    def finish(self, ins, outs, sems):
        for cp in self._plan(ins, outs, sems):
            cp.wait()


class _Share:
    def __init__(self, totals):
        n = len(totals)
        self.arrays = list(totals)
        self.out_shapes = [SDS(t.shape, t.dtype) for t in totals]
        self.scratch = [pltpu.SemaphoreType.DMA((n,)), pltpu.SemaphoreType.DMA((n,))]

    def _plan(self, ins, outs, sems):
        x, y, c, _ = _place()
        return [_remote(t, o, sems, wi, (x, y, 1 - c)) for wi, (t, o) in enumerate(zip(ins, outs))]

    def begin(self, ins, outs, sems):
        for cp in self._plan(ins, outs, sems):
            cp.start()

    def middle(self, ins, outs, sems):
        pass

    def finish(self, ins, outs, sems):
        for cp in self._plan(ins, outs, sems):
            cp.wait()


def _ride_begin(rider, r_refs, step):
    @pl.when(step == 0)
    def _():
        rider.begin(*r_refs)


def _ride_end(rider, r_refs, step, nsteps):
    @pl.when(step == min(3 * nsteps // 4, nsteps - 1))
    def _():
        rider.middle(*r_refs)

    @pl.when(step == nsteps - 1)
    def _():
        rider.finish(*r_refs)


def _comm(rider, name):
    n_in, n_out = len(rider.arrays), len(rider.out_shapes)

    def body(*refs):
        r_refs = (refs[:n_in], refs[n_in:n_in + n_out], refs[n_in + n_out:])
        rider.begin(*r_refs)
        rider.middle(*r_refs)
        rider.finish(*r_refs)

    return pl.pallas_call(body, name=name, out_shape=rider.out_shapes, in_specs=[_ANY] * n_in,
                          out_specs=[_ANY] * n_out, scratch_shapes=rider.scratch)(*rider.arrays)


def _small_allreduce(pk, name):
    r = pk.shape[0]
    rels = [(dx, dy, dc) for dx in (0, 1) for dy in (0, 1) for dc in (0, 1) if dx or dy or dc]

    def body(p_ref, o_ref, buf, send_sems, recv_sems):
        x, y, c, _ = _place()
        me = 4 * x + 2 * y + c
        buf[me] = p_ref[...]
        cps = []
        for k, (dx, dy, dc) in enumerate(rels):
            to = (1 - x if dx else x, 1 - y if dy else y, 1 - c if dc else c)
            cps.append(pltpu.make_async_remote_copy(src_ref=p_ref, dst_ref=buf.at[me], send_sem=send_sems.at[k],
                                                    recv_sem=recv_sems.at[k], device_id=to,
                                                    device_id_type=MESH))
        for cpy in cps:
            cpy.start()
        for cpy in cps:
            cpy.wait()
        acc = buf[0]
        for d in range(1, 8):
            acc = acc + buf[d]
        o_ref[...] = acc

    vm = pl.BlockSpec(memory_space=pltpu.VMEM)
    return pl.pallas_call(body, name=name, out_shape=SDS(pk.shape, F32), in_specs=[vm], out_specs=vm,
                          scratch_shapes=[pltpu.VMEM((8, r, LANE), F32), pltpu.SemaphoreType.DMA((7,)),
                                          pltpu.SemaphoreType.DMA((7,))])(pk)


ATTN_W = ("w_in", "w_uq", "w_ukv", "w_out")
FFN_W = ("w_gate", "w_up", "w_down")
BIG = ATTN_W + FFN_W


def _cols_from_chips(g):
    return jnp.concatenate([g[t] for t in range(4)], axis=1)


def _cols_to_chips(full):
    r, n = full.shape
    return full.reshape(r, 4, n // 4).transpose(1, 0, 2).reshape(4, 2, r // 2, n // 4)


def _rows_to_chips(full):
    n, c = full.shape
    return full.reshape(4, 2, n // 8, c)


def _permute_w_in(w, nh):
    d = w.shape[0]
    g = 4 * nh * HEAD
    lr = (w.shape[1] - g - 2 * nh - ROPE) // 2
    o = g + 2 * nh
    pad = jnp.zeros((d, LANE - ROPE - 8 - nh), w.dtype)
    pad8 = jnp.zeros((d, 8 - nh), w.dtype)
    return jnp.concatenate([w[:, :g], w[:, o:o + 2 * lr], w[:, o + 2 * lr:], w[:, g:g + nh], pad8,
                            w[:, g + nh:g + 2 * nh], pad, jnp.zeros((d, LANE), w.dtype)], axis=1)


def _unpermute_w_in(wp, nh, lr):
    g = 4 * nh * HEAD
    mc = g + 2 * lr
    return jnp.concatenate([wp[:, :g], wp[:, mc + B_LANE:mc + B_LANE + nh], wp[:, mc + A_LANE:mc + A_LANE + nh],
                            wp[:, g:g + 2 * lr], wp[:, mc:mc + ROPE]], axis=1)


def _permute_w_uq(w, nh):
    lr = w.shape[0]
    w3 = w.reshape(lr, nh, HEAD + ROPE)
    return jnp.concatenate([w3, jnp.zeros((lr, nh, HEAD - ROPE), w.dtype)], axis=2).reshape(lr, nh * 2 * HEAD)


def _unpermute_w_uq(wp, nh):
    lr = wp.shape[0]
    return wp.reshape(lr, nh, 2 * HEAD)[:, :, :HEAD + ROPE].reshape(lr, nh * (HEAD + ROPE))


def _permute_w_ukv(w, nh):
    lr = w.shape[0]
    w3 = w.reshape(lr, nh, 2 * HEAD)
    kp = jnp.concatenate([w3[:, :, :HEAD], jnp.zeros((lr, nh, HEAD), w.dtype)], axis=2)
    return jnp.concatenate([kp.reshape(lr, nh * 2 * HEAD), w3[:, :, HEAD:].reshape(lr, nh * HEAD)], axis=1)


def _unpermute_w_ukv(wp, nh):
    lr = wp.shape[0]
    kp = wp[:, :nh * 2 * HEAD].reshape(lr, nh, 2 * HEAD)[:, :, :HEAD]
    vp = wp[:, nh * 2 * HEAD:].reshape(lr, nh, HEAD)
    return jnp.concatenate([kp, vp], axis=2).reshape(lr, nh * 2 * HEAD)


def _reduce_begin(grads, place, tag):
    recv = _comm(_Swap(grads), "swap_" + tag)
    sums = [_sum_pair(g, r, place, "sum_pair_%s%d" % (tag, k)) for k, (g, r) in enumerate(zip(grads, recv))]
    return [s[0] for s in sums], [s[1] for s in sums]


def _reduce_end(own, recv, tag):
    return [_sum_chips(o, r, "sum_chips_%s%d" % (tag, k)) for k, (o, r) in enumerate(zip(own, recv))]


def _step(x, pos, tgt, wt, ffn_shards, small, place):
    nh = small["a_log"].shape[1]
    lr = small["q_norm_w"].shape[1]
    w = nh * HEAD
    z_col, col_q, col_kv = 3, 4 * w // lr, 4 * w // lr + 1
    misc_c = 4 * w + 2 * lr
    misc_col = misc_c // LANE
    assert (4 * w) % lr == 0 and small["kv_norm_w"].shape[1] == lr

    win_p = _permute_w_in(wt["w_in"], nh)
    wuq_p = _permute_w_uq(wt["w_uq"], nh)
    wukv_p = _permute_w_ukv(wt["w_ukv"], nh)
    zl = jnp.zeros((1, LANE), F32)
    alog_l = zl.at[:, A_LANE:A_LANE + nh].set(small["a_log"])
    dtb_l = zl.at[:, A_LANE:A_LANE + nh].set(small["dt_bias"])
    conv_w = small["conv_w"]

    h1 = _norm_fwd(x, small["attn_norm_w"], "norm1")
    proj = _mm([(h1, win_p)], name="proj_in")
    gq, gk, gv, gb, gbt = _gdn_prep(proj, conv_w, alog_l, dtb_l, nh, misc_col)
    o_gdn, states = _gdn_fwd(gq, gk, gv, gb, gbt, nh)
    cqn, ckvn = _mla_norm(proj, small["q_norm_w"], small["kv_norm_w"], col_q, col_kv)
    qraw = _mm([(cqn, wuq_p)], name="proj_uq")
    kvraw = _mm([(ckvn, wukv_p)], name="proj_ukv")
    qc, kc, vv = _mla_rope(qraw, kvraw, proj, pos, nh, misc_col)
    o_mla, lse, (wg4, wu4, wd4) = _mla_fwd(qc, kc, vv, nh, _Gather(ffn_shards))
    w_down = wd4.reshape(-1, wd4.shape[2])
    mixed = _mix_fwd(o_gdn, proj, o_mla, small["gdn_norm_w"], small["mla_out_norm_w"], nh, z_col)
    x2 = _mm([(mixed, wt["w_out"])], name="proj_out", res=x)
    h2 = _norm_fwd(x2, small["ffn_norm_w"], "norm2")
    act, gpre, upre = _swiglu_fwd(h2, wg4, wu4)
    x3 = _mm([(act, w_down)], name="proj_down", res=x2)
    dx3, d_final, loss, dx3h = _final_loss(x3, tgt, small["final_norm_w"])

    gs = {"final_norm_w": d_final}
    dgate, dup = _swiglu_bwd(dx3h, w_down, gpre, upre)
    g_down = _rows_to_chips(_mm([(act, dx3h)], name="dw_down", ta=True, out_dtype=BF16))
    dh2 = _mm([(dgate, wg4), (dup, wu4)], name="dh2", tb=True, b_chips=True)
    g_gate = _mm([(h2, dgate)], name="dw_gate", ta=True, out_dtype=BF16, out_chips=True)
    g_up = _mm([(h2, dup)], name="dw_up", ta=True, out_dtype=BF16, out_chips=True)
    halves = lambda g: g.reshape(4, 2, g.shape[1] // 2, g.shape[2])
    ffn16, ffn_own = _reduce_begin([halves(g_gate), halves(g_up), g_down], place, "ffn")
    dx2, gs["ffn_norm_w"], dx2h = _norm_bwd(dh2, x2, small["ffn_norm_w"], dx3, "norm2_bwd", True)
    dmix = _mm([(dx2h, wt["w_out"])], name="dmix", tb=True)
    g_out = _rows_to_chips(_mm([(mixed, dx2h)], name="dw_out", ta=True, out_dtype=BF16))
    d_ogdn, dz, d_omla, gs["gdn_norm_w"], gs["mla_out_norm_w"], delta = _mix_bwd(
        dmix, o_gdn, proj, o_mla, small["gdn_norm_w"], small["mla_out_norm_w"], nh, z_col)
    dqc, dkc, dvv, ffn_recv = _mla_bwd(qc, kc, vv, d_omla, lse, delta, nh, _Exchange(ffn16))
    ffn_tot = _reduce_end(ffn_own, ffn_recv, "ffn")
    dqraw, dkvraw, dkr = _mla_rope_bwd(dqc, dkc, dvv, pos, nh)
    dcqn = _mm([(dqraw, wuq_p)], name="dcqn", tb=True)
    dckvn = _mm([(dkvraw, wukv_p)], name="dckvn", tb=True)
    g_uq = _cols_to_chips(_unpermute_w_uq(_mm([(cqn, dqraw)], name="dw_uq", ta=True, out_dtype=BF16), nh))
    g_ukv = _cols_to_chips(_unpermute_w_ukv(_mm([(ckvn, dkvraw)], name="dw_ukv", ta=True, out_dtype=BF16), nh))
    dcq, dckv, gs["q_norm_w"], gs["kv_norm_w"] = _mla_norm_bwd(
        proj, small["q_norm_w"], small["kv_norm_w"], dcqn, dckvn, col_q, col_kv)
    dgq, dgk, dgv, dgb = _gdn_bwd(gq, gk, gv, gb, gbt, states, d_ogdn, nh)
    dconv, dmisc, gs["conv_w"], dal, ddb = _gdn_prep_bwd(
        proj, conv_w, alog_l, dtb_l, dgq, dgk, dgv, dgb, dkr, nh, misc_col)
    gs["a_log"] = dal[:, A_LANE:A_LANE + nh]
    gs["dt_bias"] = ddb[:, A_LANE:A_LANE + nh]
    dqkv = _conv_bwd_input(dconv, conv_w)
    dproj = jnp.concatenate([dqkv, dz, dcq, dckv, dmisc, jnp.zeros((x.shape[0], LANE), BF16)], axis=1)
    g_in = _cols_to_chips(_unpermute_w_in(_mm([(h1, dproj)], name="dw_in", ta=True, out_dtype=BF16), nh, lr))
    att16, att_own = _reduce_begin([g_in, g_uq, g_ukv, g_out], place, "att")
    dh1, att_recv = _mm([(dproj, win_p)], name="dh1", tb=True, rider=_Exchange(att16))
    att_tot = _reduce_end(att_own, att_recv, "att")
    grad_x, gs["attn_norm_w"] = _norm_bwd(dh1, x, small["attn_norm_w"], dx2, "norm1_bwd", False)
    return loss, grad_x, att_tot + ffn_tot, gs


SMALL = ("attn_norm_w", "ffn_norm_w", "final_norm_w", "q_norm_w", "kv_norm_w", "gdn_norm_w",
         "mla_out_norm_w", "a_log", "dt_bias")
WEIGHTS = ("attn_norm_w", "w_in", "conv_w", "a_log", "dt_bias", "gdn_norm_w", "q_norm_w", "w_uq",
           "kv_norm_w", "w_ukv", "mla_out_norm_w", "w_out", "ffn_norm_w", "w_gate", "w_up", "w_down",
           "final_norm_w")


def _pack_small(vecs):
    flat = jnp.concatenate([v.astype(F32).reshape(-1) for v in vecs])
    pad = (-flat.shape[0]) % (8 * LANE)
    return jnp.concatenate([flat, jnp.zeros((pad,), F32)]).reshape(-1, LANE)


def kernel(x, positions, attn_norm_w, w_in, conv_w, a_log, dt_bias, gdn_norm_w, q_norm_w, w_uq, kv_norm_w, w_ukv, mla_out_norm_w, w_out, ffn_norm_w, w_gate, w_up, w_down, final_norm_w, loss_target, m_attn_norm_w, m_w_in, m_conv_w, m_a_log, m_dt_bias, m_gdn_norm_w, m_q_norm_w, m_w_uq, m_kv_norm_w, m_w_ukv, m_mla_out_norm_w, m_w_out, m_ffn_norm_w, m_w_gate, m_w_up, m_w_down, m_final_norm_w, v_attn_norm_w, v_w_in, v_conv_w, v_a_log, v_dt_bias, v_gdn_norm_w, v_q_norm_w, v_w_uq, v_kv_norm_w, v_w_ukv, v_mla_out_norm_w, v_w_out, v_ffn_norm_w, v_w_gate, v_w_up, v_w_down, v_final_norm_w):
    args = dict(locals())
    xi, yi, ci = lax.axis_index("x"), lax.axis_index("y"), lax.axis_index("c")
    chip = 2 * xi + yi

    def two_d(a):
        return a.reshape(a.shape[-2:]) if a.ndim >= 2 else a.reshape(1, -1)

    wloc = {n: two_d(args[n]) for n in WEIGHTS}
    mloc = {n: two_d(args["m_" + n]) for n in WEIGHTS}
    vloc = {n: two_d(args["v_" + n]) for n in WEIGHTS}

    ga = _comm(_Gather([wloc[n].astype(BF16) for n in ATTN_W]), "gather_attn")
    wt = {"w_in": _cols_from_chips(ga[0]), "w_uq": _cols_from_chips(ga[1]), "w_ukv": _cols_from_chips(ga[2]),
          "w_out": ga[3].reshape(-1, ga[3].shape[2])}
    cw = wloc["conv_w"]
    cshard = cw.shape[1]
    cfull = jnp.zeros((CONV, 4 * cshard), F32)
    cfull = lax.dynamic_update_slice(cfull, jnp.where(ci == 0, cw, 0.0), (0, chip * cshard))
    conv_full = _small_allreduce(_pack_small([cfull]), "gather_conv_w").reshape(-1)[:CONV * 4 * cshard]
    conv_full = conv_full.reshape(CONV, 4 * cshard)

    small = {n: wloc[n] for n in SMALL}
    small["conv_w"] = conv_full

    pos = positions.reshape(-1, 1).astype(F32)
    place = jnp.stack([ci, chip]).astype(jnp.int32)
    loss, grad_x, totals, gs = _step(two_d(x), pos, two_d(loss_target), wt,
                                     [wloc[n].astype(BF16) for n in FFN_W], small, place)
    from_sib = _comm(_Share(totals), "share_halves")

    small_names = SMALL + ("conv_w",)
    pk = _pack_small([gs[n] for n in small_names] + [loss])
    red = _small_allreduce(pk, "reduce_small").reshape(-1)
    gsm, off = {}, 0
    for n in small_names:
        shp = gs[n].shape
        gsm[n] = red[off:off + shp[0] * shp[1]].reshape(shp)
        off += shp[0] * shp[1]
    loss_out = red[off]
    gsm["conv_w"] = lax.dynamic_slice(gsm["conv_w"], (0, chip * cshard), (CONV, cshard))

    grads, deltas, new_m, new_v = {}, {}, {}, {}
    for n, mine, theirs in zip(BIG, totals, from_sib):
        grads[n], deltas[n], new_m[n], new_v[n] = _adamw_halves(wloc[n], mine, theirs, mloc[n], vloc[n], place,
                                                                "adamw_" + n)
    grads["conv_w"] = gsm["conv_w"]
    deltas["conv_w"], new_m["conv_w"], new_v["conv_w"] = _adamw(wloc["conv_w"], gsm["conv_w"], mloc["conv_w"],
                                                                vloc["conv_w"], "adamw_conv_w")
    sm_shapes = [wloc[n].shape for n in SMALL]
    pd, pm, pv = _adamw(_pack_small([wloc[n] for n in SMALL]), _pack_small([gsm[n] for n in SMALL]),
                        _pack_small([mloc[n] for n in SMALL]), _pack_small([vloc[n] for n in SMALL]),
                        "adamw_small")
    for dst, packed in ((deltas, pd), (new_m, pm), (new_v, pv)):
        flat, off = packed.reshape(-1), 0
        for n, shp in zip(SMALL, sm_shapes):
            dst[n] = flat[off:off + shp[0] * shp[1]].reshape(shp)
            off += shp[0] * shp[1]
    for n in SMALL:
        grads[n] = gsm[n]

    def like(n, a):
        return a.reshape(args[n].shape)

    outs = [loss_out.reshape(()), grad_x.reshape(x.shape)]
    for group in (grads, deltas, new_m, new_v):
        outs += [like(n, group[n]) for n in WEIGHTS]
    return tuple(outs)
```

```python
import functools

import jax
import jax.numpy as jnp
from jax import lax
from jax.experimental import pallas as pl
from jax.experimental.pallas import tpu as pltpu

F32, BF16 = jnp.float32, jnp.bfloat16
SDS = jax.ShapeDtypeStruct
MESH = pl.DeviceIdType.MESH

HEAD = 128
ROPE = 64
CHUNK = 64
PAIR = 2 * CHUNK
CONV = 4
EPS = 1e-6
ROPE_THETA = 10000.0
LANE = 128
B_LANE = 64
A_LANE = 72
VMEM_LIMIT = 48 * 1024 * 1024
VMEM_LIMIT_WIDE = 56 * 1024 * 1024
MLA_BLOCK = 512
LOG2E = 1.4426950408889634
LN2 = 0.6931471805599453
SM_SCALE = (HEAD + ROPE) ** -0.5

ADAM_LR = 0.001
ADAM_B1 = 0.9
ADAM_B2 = 0.999
ADAM_EPS = 1e-08
ADAM_WD = 0.01
ADAM_STEP = 10


def _tile(n, pref, mult=LANE):
    if n <= pref:
        return n
    t = (pref // mult) * mult
    while t >= mult:
        if n % t == 0:
            return t
        t -= mult
    return n


def _pcall(body, *, name, grid, in_specs, out_specs, out_shape, scratch=(), vmem=VMEM_LIMIT):
    return pl.pallas_call(
        body, name=name, grid=grid, in_specs=in_specs, out_specs=out_specs,
        out_shape=out_shape, scratch_shapes=list(scratch),
        compiler_params=pltpu.CompilerParams(
            dimension_semantics=("arbitrary",) * len(grid), vmem_limit_bytes=vmem))


def _pcall_riding(core, rider, *, name, grid, in_specs, out_specs, out_shape, args, scratch=()):
    n_in, n_out, n_scr = len(in_specs), len(out_specs), len(scratch)
    r_in, r_out = len(rider.arrays), len(rider.out_shapes)

    def body(*refs):
        ins, refs = refs[:n_in], refs[n_in:]
        r_ins, refs = refs[:r_in], refs[r_in:]
        outs, refs = refs[:n_out], refs[n_out:]
        r_outs, refs = refs[:r_out], refs[r_out:]
        scr, sems = refs[:n_scr], refs[n_scr:]
        r_refs = (r_ins, r_outs, sems)
        _ride_begin(rider, r_refs, pl.program_id(0))
        core(*ins, *outs, *scr)
        _ride_end(rider, r_refs, pl.program_id(0), grid[0])

    res = _pcall(body, name=name, grid=grid, in_specs=list(in_specs) + [_ANY] * r_in,
                 out_specs=list(out_specs) + [_ANY] * r_out, out_shape=list(out_shape) + rider.out_shapes,
                 scratch=list(scratch) + rider.scratch)(*args, *rider.arrays)
    return res[:n_out], res[n_out:]


def _rows(ts, width, col=0):
    return pl.BlockSpec((ts, width), lambda i: (i, col))


def _full(shape):
    nd = len(shape)
    return pl.BlockSpec(shape, lambda i: (0,) * nd)


def _dot(a, b):
    return jnp.dot(a.astype(BF16), b.astype(BF16), preferred_element_type=F32)


def _dot_nt(a, b):
    return lax.dot_general(a.astype(BF16), b.astype(BF16), (((1,), (1,)), ((), ())),
                           preferred_element_type=F32)


def _dot_tn(a, b):
    return lax.dot_general(a.astype(BF16), b.astype(BF16), (((0,), (0,)), ((), ())),
                           preferred_element_type=F32)


def _sigmoid(x):
    return 1.0 / (1.0 + jnp.exp(-x))


def _silu(x):
    return x * _sigmoid(x)


def _dsilu(x):
    s = _sigmoid(x)
    return s * (1.0 + x * (1.0 - s))


def _lane_iota(shape):
    return lax.broadcasted_iota(jnp.int32, shape, len(shape) - 1)


def _col(block, idx):
    return jnp.sum(jnp.where(_lane_iota(block.shape) == idx, block, 0.0), axis=-1, keepdims=True)


def _mm(pairs, *, name, ta=False, tb=False, out_dtype=F32, res=None, tm=1024, tn=1024, tk=2048,
        b_chips=False, out_chips=False, rider=None):
    a0, b0 = pairs[0]
    if ta:
        kdim, m = a0.shape
    else:
        m, kdim = a0.shape
    if b_chips and tb:
        n, tk = b0.shape[1], b0.shape[2]
        assert kdim == 4 * tk
    elif b_chips:
        n, tn = 4 * b0.shape[2], b0.shape[2]
        assert kdim == b0.shape[1]
    else:
        n = b0.shape[0] if tb else b0.shape[1]
    if out_chips:
        tn = n // 4
    tm = _tile(m, tm)
    tn = tn if (out_chips or (b_chips and not tb)) else _tile(n, tn)
    tk = tk if (b_chips and tb) else _tile(kdim, tk)
    assert m % tm == 0 and n % tn == 0 and kdim % tk == 0
    nk, npair = kdim // tk, len(pairs)
    grid = (m // tm, n // tn, nk)
    dims = (((0 if ta else 1,), (1 if tb else 0,)), ((), ()))
    n_in = 2 * npair + (res is not None)
    r_in, r_out = (len(rider.arrays), len(rider.out_shapes)) if rider else (0, 0)

    def body(*refs):
        o_ref = refs[n_in + r_in]
        acc = refs[n_in + r_in + 1 + r_out]
        k = pl.program_id(2)
        if rider:
            r_refs = (refs[n_in:n_in + r_in], refs[n_in + r_in + 1:n_in + r_in + 1 + r_out],
                      refs[n_in + r_in + 2 + r_out:])
            step = (pl.program_id(0) * grid[1] + pl.program_id(1)) * nk + k
            _ride_begin(rider, r_refs, step)

        @pl.when(k == 0)
        def _():
            acc[...] = jnp.zeros_like(acc)

        tot = None
        for p in range(npair):
            d = lax.dot_general(refs[2 * p][...].astype(BF16), refs[2 * p + 1][...].astype(BF16),
                                dims, preferred_element_type=F32)
            tot = d if tot is None else tot + d
        acc[...] += tot

        @pl.when(k == nk - 1)
        def _():
            r = acc[...]
            if res is not None:
                r = r + refs[2 * npair][...]
            o_ref[...] = r.astype(out_dtype)

        if rider:
            _ride_end(rider, r_refs, step, grid[0] * grid[1] * nk)

    if ta:
        a_spec = pl.BlockSpec((tk, tm), lambda i, j, k: (k, i))
    else:
        a_spec = pl.BlockSpec((tm, tk), lambda i, j, k: (i, k))
    if b_chips and tb:
        b_spec = pl.BlockSpec((None, tn, tk), lambda i, j, k: (k, j, 0))
    elif b_chips:
        b_spec = pl.BlockSpec((None, tk, tn), lambda i, j, k: (j, k, 0))
    elif tb:
        b_spec = pl.BlockSpec((tn, tk), lambda i, j, k: (j, k))
    else:
        b_spec = pl.BlockSpec((tk, tn), lambda i, j, k: (k, j))
    if out_chips:
        o_spec = pl.BlockSpec((None, tm, tn), lambda i, j, k: (j, i, 0))
        o_shape = SDS((4, m, tn), out_dtype)
    else:
        o_spec = pl.BlockSpec((tm, tn), lambda i, j, k: (i, j))
        o_shape = SDS((m, n), out_dtype)
    in_specs, args = [], []
    for a, b in pairs:
        in_specs += [a_spec, b_spec]
        args += [a, b]
    if res is not None:
        in_specs.append(o_spec)
        args.append(res)
    out_specs, out_shapes, scratch = [o_spec], [o_shape], [pltpu.VMEM((tm, tn), F32)]
    if rider:
        in_specs += [_ANY] * r_in
        args += rider.arrays
        out_specs += [_ANY] * r_out
        out_shapes += rider.out_shapes
        scratch += rider.scratch
    outs = _pcall(body, name=name, grid=grid, in_specs=in_specs, out_specs=out_specs, out_shape=out_shapes,
                  scratch=scratch)(*args)
    return (outs[0], outs[1:]) if rider else outs[0]


def _norm_fwd(x, w, name):
    s, d = x.shape
    ts = _tile(s, 512, 8)

    def body(x_ref, w_ref, h_ref):
        xv = x_ref[...]
        r = lax.rsqrt(jnp.mean(xv * xv, axis=-1, keepdims=True) + EPS)
        h_ref[...] = (xv * r * w_ref[...]).astype(BF16)

    return _pcall(body, name=name, grid=(s // ts,), in_specs=[_rows(ts, d), _full((1, d))],
                  out_specs=_rows(ts, d), out_shape=SDS((s, d), BF16))(x, w)


def _norm_bwd(dh, x, w, dres, name, with_bf16, rider=None):
    s, d = x.shape
    ts = _tile(s, 256, 8)

    def body(dh_ref, x_ref, w_ref, dres_ref, dx_ref, dw_ref, *dx16_ref):
        @pl.when(pl.program_id(0) == 0)
        def _():
            dw_ref[...] = jnp.zeros_like(dw_ref)

        xv, dhv = x_ref[...], dh_ref[...]
        r = lax.rsqrt(jnp.mean(xv * xv, axis=-1, keepdims=True) + EPS)
        xh = xv * r
        dw_ref[...] += jnp.sum(dhv * xh, axis=0, keepdims=True)
        dxh = dhv * w_ref[...]
        dx = dres_ref[...] + r * (dxh - xh * jnp.mean(dxh * xh, axis=-1, keepdims=True))
        dx_ref[...] = dx
        for ref in dx16_ref:
            ref[...] = dx.astype(BF16)

    extra = 1 if with_bf16 else 0
    spec = dict(name=name, grid=(s // ts,),
                in_specs=[_rows(ts, d), _rows(ts, d), _full((1, d)), _rows(ts, d)],
                out_specs=[_rows(ts, d), _full((1, d))] + [_rows(ts, d)] * extra,
                out_shape=[SDS((s, d), F32), SDS((1, d), F32)] + [SDS((s, d), BF16)] * extra)
    if rider is None:
        return _pcall(body, **spec)(dh, x, w, dres)
    outs, r_outs = _pcall_riding(body, rider, args=(dh, x, w, dres), **spec)
    return (*outs, r_outs)


def _final_loss(x3, tgt, w):
    s, d = x3.shape
    ts = _tile(s, 256, 8)

    def body(x_ref, t_ref, w_ref, dx_ref, dw_ref, loss_ref, dx16_ref):
        @pl.when(pl.program_id(0) == 0)
        def _():
            dw_ref[...] = jnp.zeros_like(dw_ref)
            loss_ref[...] = jnp.zeros_like(loss_ref)

        xv, wv = x_ref[...], w_ref[...]
        r = lax.rsqrt(jnp.mean(xv * xv, axis=-1, keepdims=True) + EPS)
        xh = xv * r
        err = xh * wv - t_ref[...]
        row = jnp.mean(err * err, axis=-1, keepdims=True)
        loss_ref[...] += 0.5 * jnp.sum(row, axis=0, keepdims=True)
        dy = err * (1.0 / d)
        dw_ref[...] += jnp.sum(dy * xh, axis=0, keepdims=True)
        dxh = dy * wv
        dx = r * (dxh - xh * jnp.mean(dxh * xh, axis=-1, keepdims=True))
        dx_ref[...] = dx
        dx16_ref[...] = dx.astype(BF16)

    return _pcall(body, name="final_loss", grid=(s // ts,),
                  in_specs=[_rows(ts, d), _rows(ts, d), _full((1, d))],
                  out_specs=[_rows(ts, d), _full((1, d)), _full((1, 1)), _rows(ts, d)],
                  out_shape=[SDS((s, d), F32), SDS((1, d), F32), SDS((1, 1), F32), SDS((s, d), BF16)])(
                      x3, tgt, w)


def _shift_down(cur, halo, s):
    if s == 0:
        return cur
    row8 = lax.broadcasted_iota(jnp.int32, halo.shape, 0)
    r = pltpu.roll(cur, s, 0)
    top = jnp.where(row8 < s, pltpu.roll(halo, s, 0), r[0:8])
    return jnp.concatenate([top, r[8:]], axis=0)


def _shift_up(cur, halo, s):
    if s == 0:
        return cur
    ts = cur.shape[0]
    row8 = lax.broadcasted_iota(jnp.int32, halo.shape, 0)
    r = pltpu.roll(cur, ts - s, 0)
    bot = jnp.where(row8 >= 8 - s, pltpu.roll(halo, 8 - s, 0), r[ts - 8:ts])
    return jnp.concatenate([r[:ts - 8], bot], axis=0)


def _chunk_tri(ts, upper):
    i = lax.broadcasted_iota(jnp.int32, (ts, ts), 0)
    j = lax.broadcasted_iota(jnp.int32, (ts, ts), 1)
    same = jnp.right_shift(i, 6) == jnp.right_shift(j, 6)
    return jnp.where(same & ((j >= i) if upper else (j <= i)), 1.0, 0.0).astype(F32)


def _gate_values(m, alog, dtb):
    lane = _lane_iota(m.shape)
    beta = _sigmoid(m)
    xg = m + dtb
    sp = jnp.maximum(xg, 0.0) + jnp.log(1.0 + jnp.exp(-jnp.abs(xg)))
    ga = (lane >= A_LANE) & (lane < A_LANE + 8)
    g = jnp.where(ga, -jnp.exp(alog) * sp, 0.0)
    return beta, g, xg, ga


def _l2_heads(a, nh, scale):
    outs, rs = [], []
    for h in range(nh):
        ah = a[:, HEAD * h:HEAD * (h + 1)]
        r = lax.rsqrt(jnp.sum(ah * ah, axis=-1, keepdims=True) + EPS)
        outs.append(ah * (r * scale))
        rs.append(r)
    return jnp.concatenate(outs, axis=-1), rs


def _gdn_prep(proj, conv_w, alog_l, dtb_l, nh, misc_col):
    s = proj.shape[0]
    w = nh * HEAD
    ts = _tile(s, 256, PAIR)
    hb = ts // 8

    def body(cur_ref, halo_ref, misc_ref, cw_ref, al_ref, db_ref, q_ref, k_ref, v_ref, gb_ref, gbt_ref):
        first = pl.program_id(0) == 0
        outs = (q_ref, k_ref, v_ref)
        for sec in range(3):
            cs = slice(sec * w, (sec + 1) * w)
            cur = cur_ref[:, cs]
            halo = jnp.where(first, 0.0, halo_ref[:, cs])
            pre = None
            for j in range(CONV):
                term = cw_ref[j:j + 1, cs] * _shift_down(cur, halo, CONV - 1 - j)
                pre = term if pre is None else pre + term
            act = _silu(pre)
            if sec == 0:
                act, _ = _l2_heads(act, nh, HEAD ** -0.5)
            elif sec == 1:
                act, _ = _l2_heads(act, nh, 1.0)
            outs[sec][...] = act
        m = misc_ref[...]
        lane = _lane_iota(m.shape)
        beta, g, _, ga = _gate_values(m, al_ref[...], db_ref[...])
        gcc = jnp.dot(_chunk_tri(ts, False), g, precision=lax.Precision.HIGHEST,
                      preferred_element_type=F32)
        gb = jnp.where((lane >= B_LANE) & (lane < B_LANE + 8), beta, jnp.where(ga, gcc, 0.0))
        gb_ref[...] = gb
        gbt_ref[...] = gb.T

    return _pcall(
        body, name="gdn_prep", grid=(s // ts,),
        in_specs=[_rows(ts, 3 * w),
                  pl.BlockSpec((8, 3 * w), lambda i: (jnp.maximum(i * hb - 1, 0), 0)),
                  _rows(ts, LANE, misc_col), _full((CONV, 3 * w)), _full((1, LANE)), _full((1, LANE))],
        out_specs=[_rows(ts, w), _rows(ts, w), _rows(ts, w), _rows(ts, LANE),
                   pl.BlockSpec((LANE, ts), lambda i: (0, i))],
        out_shape=[SDS((s, w), F32), SDS((s, w), F32), SDS((s, w), F32), SDS((s, LANE), F32),
                   SDS((LANE, s), F32)])(proj, proj, proj, conv_w, alog_l, dtb_l)


def _gdn_prep_bwd(proj, conv_w, alog_l, dtb_l, dq, dk, dv, dgb, dkr, nh, misc_col):
    s = proj.shape[0]
    w = nh * HEAD
    ts = _tile(s, 256, PAIR)
    hb = ts // 8

    def body(cur_ref, halo_ref, misc_ref, cw_ref, al_ref, db_ref, dq_ref, dk_ref, dv_ref, dgb_ref,
             dkr_ref, dc_ref, dm_ref, dcw_ref, dal_ref, ddb_ref):
        first = pl.program_id(0) == 0

        @pl.when(first)
        def _():
            dcw_ref[...] = jnp.zeros_like(dcw_ref)
            dal_ref[...] = jnp.zeros_like(dal_ref)
            ddb_ref[...] = jnp.zeros_like(ddb_ref)

        dins = (dq_ref, dk_ref, dv_ref)
        for sec in range(3):
            cs = slice(sec * w, (sec + 1) * w)
            cur = cur_ref[:, cs]
            halo = jnp.where(first, 0.0, halo_ref[:, cs])
            us = [_shift_down(cur, halo, CONV - 1 - j) for j in range(CONV)]
            pre = None
            for j in range(CONV):
                term = cw_ref[j:j + 1, cs] * us[j]
                pre = term if pre is None else pre + term
            act = _silu(pre)
            dout = dins[sec][...]
            if sec < 2:
                scale = HEAD ** -0.5 if sec == 0 else 1.0
                parts = []
                for h in range(nh):
                    hs = slice(HEAD * h, HEAD * (h + 1))
                    ah = act[:, hs]
                    r = lax.rsqrt(jnp.sum(ah * ah, axis=-1, keepdims=True) + EPS)
                    ahat = ah * r
                    dy = dout[:, hs]
                    parts.append((scale * r) * (dy - ahat * jnp.sum(dy * ahat, axis=-1, keepdims=True)))
                dact = jnp.concatenate(parts, axis=-1)
            else:
                dact = dout
            dconv = dact * _dsilu(pre)
            dc_ref[:, cs] = dconv
            for j in range(CONV):
                dcw_ref[j:j + 1, cs] += jnp.sum(dconv * us[j], axis=0, keepdims=True)
        m = misc_ref[...]
        lane = _lane_iota(m.shape)
        al = al_ref[...]
        beta, g, xg, ga = _gate_values(m, al, db_ref[...])
        dgbv = dgb_ref[...]
        dg = jnp.dot(_chunk_tri(ts, True), jnp.where(ga, dgbv, 0.0), precision=lax.Precision.HIGHEST,
                     preferred_element_type=F32)
        da_raw = jnp.where(ga, dg * (-jnp.exp(al)) * _sigmoid(xg), 0.0)
        db_raw = jnp.where((lane >= B_LANE) & (lane < B_LANE + 8), dgbv * beta * (1.0 - beta), 0.0)
        dal_ref[...] += jnp.sum(dg * g, axis=0, keepdims=True)
        ddb_ref[...] += jnp.sum(da_raw, axis=0, keepdims=True)
        dm_ref[...] = (dkr_ref[...] + da_raw + db_raw).astype(BF16)

    return _pcall(
        body, name="gdn_prep_bwd", grid=(s // ts,),
        in_specs=[_rows(ts, 3 * w),
                  pl.BlockSpec((8, 3 * w), lambda i: (jnp.maximum(i * hb - 1, 0), 0)),
                  _rows(ts, LANE, misc_col), _full((CONV, 3 * w)), _full((1, LANE)), _full((1, LANE)),
                  _rows(ts, w), _rows(ts, w), _rows(ts, w), _rows(ts, LANE), _rows(ts, LANE)],
        out_specs=[_rows(ts, 3 * w), _rows(ts, LANE), _full((CONV, 3 * w)), _full((1, LANE)),
                   _full((1, LANE))],
        out_shape=[SDS((s, 3 * w), F32), SDS((s, LANE), BF16), SDS((CONV, 3 * w), F32),
                   SDS((1, LANE), F32), SDS((1, LANE), F32)])(
                       proj, proj, proj, conv_w, alog_l, dtb_l, dq, dk, dv, dgb, dkr)


def _conv_bwd_input(dconv, conv_w):
    s, c = dconv.shape
    ts = _tile(s, 256, 8)
    hb = ts // 8
    nblk8 = s // 8
    nt = s // ts

    def body(cur_ref, nxt_ref, cw_ref, o_ref):
        last = pl.program_id(0) == nt - 1
        cur = cur_ref[...]
        halo = jnp.where(last, 0.0, nxt_ref[...])
        acc = None
        for j in range(CONV):
            term = cw_ref[j:j + 1, :] * _shift_up(cur, halo, CONV - 1 - j)
            acc = term if acc is None else acc + term
        o_ref[...] = acc.astype(BF16)

    return _pcall(
        body, name="conv_bwd_input", grid=(nt,),
        in_specs=[_rows(ts, c),
                  pl.BlockSpec((8, c), lambda i: (jnp.minimum((i + 1) * hb, nblk8 - 1), 0)),
                  _full((CONV, c))],
        out_specs=_rows(ts, c), out_shape=SDS((s, c), BF16))(dconv, dconv, conv_w)


def _inv_unit_lower(a):
    n = a[0].shape[0]
    i = lax.broadcasted_iota(jnp.int32, (n, n), 0)
    j = lax.broadcasted_iota(jnp.int32, (n, n), 1)
    eye = jnp.where(i == j, 1.0, 0.0)
    t = [eye - ah for ah in a]
    x = a
    for _ in range(5):
        x = [_dot(xh, xh) for xh in x]
        t = [th + _dot(th, xh) for th, xh in zip(t, x)]
    return t


def _pair_common(q, k, gcol, grow, bcol):
    i = lax.broadcasted_iota(jnp.int32, (PAIR, PAIR), 0)
    j = lax.broadcasted_iota(jnp.int32, (PAIR, PAIR), 1)
    same = jnp.right_shift(i, 6) == jnp.right_shift(j, 6)
    tril = same & (i >= j)
    strict = same & (i > j)
    dec = [jnp.where(tril, jnp.exp(jnp.minimum(gc - gr, 0.0)), 0.0) for gc, gr in zip(gcol, grow)]
    kk = [_dot_nt(kh, kh) for kh in k]
    qk = [_dot_nt(qh, kh) for qh, kh in zip(q, k)]
    a = [jnp.where(strict, b * kkh * d, 0.0) for b, kkh, d in zip(bcol, kk, dec)]
    t = _inv_unit_lower(a)
    p = [qkh * d for qkh, d in zip(qk, dec)]
    return dec, kk, a, t, p, tril, strict


def _ext(v, a):
    z = jnp.zeros_like(v)
    return jnp.concatenate([v, z] if a == 0 else [z, v], axis=0)


def _gdn_fwd(q, k, v, gb, gbt, nh):
    s = q.shape[0]
    w = nh * HEAD
    npair = s // PAIR

    def body(q_ref, k_ref, v_ref, gb_ref, gbt_ref, o_ref, st_ref, s_ref):
        @pl.when(pl.program_id(0) == 0)
        def _():
            s_ref[...] = jnp.zeros_like(s_ref)

        heads = range(nh)
        hs = [slice(HEAD * h, HEAD * (h + 1)) for h in heads]
        gbv = gb_ref[...]
        q, k, v = [q_ref[:, s_] for s_ in hs], [k_ref[:, s_] for s_ in hs], [v_ref[:, s_] for s_ in hs]
        gcol = [_col(gbv, A_LANE + h) for h in heads]
        bcol = [_col(gbv, B_LANE + h) for h in heads]
        grow = [gbt_ref[A_LANE + h:A_LANE + h + 1, :] for h in heads]
        _, _, _, t, p, _, _ = _pair_common(q, k, gcol, grow, bcol)
        eg = [jnp.exp(gc) for gc in gcol]
        qg = [x * e for x, e in zip(q, eg)]
        kg = [x * e for x, e in zip(k, eg)]
        outs = []
        for a in range(2):
            sl = slice(CHUNK * a, CHUNK * (a + 1))
            st = [s_ref[h] for h in heads]
            for h in heads:
                st_ref[a, h] = st[h]
            r = [v[h][sl] - _dot(kg[h][sl], st[h]) for h in heads]
            vn = [_dot(t[h][sl], _ext(bcol[h][sl] * r[h], a)) for h in heads]
            outs.append([_dot(qg[h][sl], st[h]) + _dot(p[h][sl], _ext(vn[h], a)) for h in heads])
            gl = [_col(gr, CHUNK * (a + 1) - 1) for gr in grow]
            kd = [k[h][sl] * jnp.exp(gl[h] - gcol[h][sl]) for h in heads]
            upd = [_dot_tn(kd[h], vn[h]) for h in heads]
            for h in heads:
                s_ref[h] = jnp.exp(gl[h]) * st[h] + upd[h]
        for h in heads:
            o_ref[:, hs[h]] = jnp.concatenate([outs[0][h], outs[1][h]], axis=0)

    return _pcall(
        body, name="gdn_fwd", grid=(npair,),
        in_specs=[_rows(PAIR, w), _rows(PAIR, w), _rows(PAIR, w), _rows(PAIR, LANE),
                  pl.BlockSpec((LANE, PAIR), lambda i: (0, i))],
        out_specs=[_rows(PAIR, w), pl.BlockSpec((2, nh, HEAD, HEAD), lambda i: (i, 0, 0, 0))],
        out_shape=[SDS((s, w), F32), SDS((2 * npair, nh, HEAD, HEAD), F32)],
        scratch=[pltpu.VMEM((nh, HEAD, HEAD), F32)])(q, k, v, gb, gbt)


def _gdn_bwd(q, k, v, gb, gbt, states, do, nh, rider):
    s = q.shape[0]
    w = nh * HEAD
    npair = s // PAIR
    rev = lambda i: (npair - 1 - i, 0)

    def body(q_ref, k_ref, v_ref, gb_ref, gbt_ref, st_ref, do_ref, dq_ref, dk_ref, dv_ref, dgb_ref,
             ds_ref):
        @pl.when(pl.program_id(0) == 0)
        def _():
            ds_ref[...] = jnp.zeros_like(ds_ref)

        lane = _lane_iota((PAIR, LANE))
        row = lax.broadcasted_iota(jnp.int32, (CHUNK, 1), 0)
        heads = range(nh)
        hs = [slice(HEAD * h, HEAD * (h + 1)) for h in heads]
        gbv = gb_ref[...]
        q, k, v = [q_ref[:, s_] for s_ in hs], [k_ref[:, s_] for s_ in hs], [v_ref[:, s_] for s_ in hs]
        do = [do_ref[:, s_] for s_ in hs]
        gcol = [_col(gbv, A_LANE + h) for h in heads]
        bcol = [_col(gbv, B_LANE + h) for h in heads]
        grow = [gbt_ref[A_LANE + h:A_LANE + h + 1, :] for h in heads]
        dec, kk, amat, t, p, tril, strict = _pair_common(q, k, gcol, grow, bcol)
        tt, pt = [x.T for x in t], [x.T for x in p]
        eg = [jnp.exp(gc) for gc in gcol]
        qg = [x * e for x, e in zip(q, eg)]
        kg = [x * e for x, e in zip(k, eg)]
        sums = lambda x: jnp.sum(x, axis=-1, keepdims=True)
        rs, vns = [None, None], [None, None]
        for a in range(2):
            sl = slice(CHUNK * a, CHUNK * (a + 1))
            rs[a] = [v[h][sl] - _dot(kg[h][sl], st_ref[a, h]) for h in heads]
            vns[a] = [_dot(t[h][sl], _ext(bcol[h][sl] * rs[a][h], a)) for h in heads]
        dsn = [ds_ref[h] for h in heads]
        dqs, dks, dvs, dgcs, dbs, drbs = ([None, None] for _ in range(6))
        for a in (1, 0):
            sl = slice(CHUNK * a, CHUNK * (a + 1))
            st = [st_ref[a, h] for h in heads]
            gl = [_col(gr, CHUNK * (a + 1) - 1) for gr in grow]
            egl = [jnp.exp(x) for x in gl]
            dk_dec = [jnp.exp(gl[h] - gcol[h][sl]) for h in heads]
            kd = [k[h][sl] * dk_dec[h] for h in heads]
            d_vn = [_dot(pt[h][sl], _ext(do[h][sl], a)) + _dot(kd[h], dsn[h]) for h in heads]
            d_qg = [_dot_nt(do[h][sl], st[h]) for h in heads]
            d_rb = [_dot(tt[h][sl], _ext(d_vn[h], a)) for h in heads]
            d_r = [bcol[h][sl] * d_rb[h] for h in heads]
            d_kg = [-_dot_nt(d_r[h], st[h]) for h in heads]
            d_kd = [_dot_nt(vns[a][h], dsn[h]) for h in heads]
            dsn_new = [_dot_tn(qg[h][sl], do[h][sl]) - _dot_tn(kg[h][sl], d_r[h]) for h in heads]
            dbs[a] = [sums(d_rb[h] * rs[a][h]) for h in heads]
            dgl = [egl[h] * jnp.sum(dsn[h] * st[h], keepdims=True) + jnp.sum(d_kd[h] * kd[h], keepdims=True)
                   for h in heads]
            dgcs[a] = [sums(d_qg[h] * qg[h][sl]) + sums(d_kg[h] * kg[h][sl]) - sums(d_kd[h] * kd[h])
                       + jnp.where(row == CHUNK - 1, dgl[h], 0.0) for h in heads]
            dqs[a] = [d_qg[h] * eg[h][sl] for h in heads]
            dks[a] = [d_kg[h] * eg[h][sl] + d_kd[h] * dk_dec[h] for h in heads]
            dvs[a] = d_r
            drbs[a] = d_rb
            dsn = [dsn_new[h] + egl[h] * dsn[h] for h in heads]
        for h in heads:
            ds_ref[h] = dsn[h]
        cat = lambda xs, h: jnp.concatenate([xs[0][h], xs[1][h]], axis=0)
        vn = [cat(vns, h) for h in heads]
        d_rb = [cat(drbs, h) for h in heads]
        dp = [jnp.where(tril, _dot_nt(do[h], vn[h]), 0.0) for h in heads]
        dam = [jnp.where(strict, -_dot_nt(d_rb[h], vn[h]), 0.0) for h in heads]
        g_p = [dp[h] * dec[h] for h in heads]
        g_a = [dam[h] * dec[h] for h in heads]
        gbk = [bcol[h] * g_a[h] for h in heads]
        dq2 = [_dot(g_p[h], k[h]) for h in heads]
        dk2 = [_dot_tn(g_p[h], q[h]) + _dot(gbk[h], k[h]) + _dot_tn(gbk[h], k[h]) for h in heads]
        dgb = jnp.zeros((PAIR, LANE), F32)
        for h in heads:
            dq_ref[:, hs[h]] = cat(dqs, h) + dq2[h]
            dk_ref[:, hs[h]] = cat(dks, h) + dk2[h]
            dv_ref[:, hs[h]] = cat(dvs, h)
            dbeta = cat(dbs, h) + sums(g_a[h] * kk[h])
            mm = dp[h] * p[h] + dam[h] * amat[h]
            dgc = cat(dgcs, h) + sums(mm) - sums(mm.T)
            dgb = dgb + jnp.where(lane == A_LANE + h, dgc, 0.0) + jnp.where(lane == B_LANE + h, dbeta, 0.0)
        dgb_ref[...] = dgb

    return _pcall_riding(
        body, rider, name="gdn_bwd", grid=(npair,),
        in_specs=[pl.BlockSpec((PAIR, w), rev), pl.BlockSpec((PAIR, w), rev), pl.BlockSpec((PAIR, w), rev),
                  pl.BlockSpec((PAIR, LANE), rev),
                  pl.BlockSpec((LANE, PAIR), lambda i: (0, npair - 1 - i)),
                  pl.BlockSpec((2, nh, HEAD, HEAD), lambda i: (npair - 1 - i, 0, 0, 0)),
                  pl.BlockSpec((PAIR, w), rev)],
        out_specs=[pl.BlockSpec((PAIR, w), rev), pl.BlockSpec((PAIR, w), rev), pl.BlockSpec((PAIR, w), rev),
                   pl.BlockSpec((PAIR, LANE), rev)],
        out_shape=[SDS((s, w), F32), SDS((s, w), F32), SDS((s, w), F32), SDS((s, LANE), F32)],
        scratch=[pltpu.VMEM((nh, HEAD, HEAD), F32)], args=(q, k, v, gb, gbt, states, do))


def _mla_norm(proj, qw, kvw, col_q, col_kv):
    s = proj.shape[0]
    lr = qw.shape[1]
    ts = _tile(s, 512, 8)

    def body(cq_ref, ckv_ref, qw_ref, kvw_ref, oq_ref, okv_ref):
        for x_ref, w_ref, o_ref in ((cq_ref, qw_ref, oq_ref), (ckv_ref, kvw_ref, okv_ref)):
            xv = x_ref[...]
            r = lax.rsqrt(jnp.mean(xv * xv, axis=-1, keepdims=True) + EPS)
            o_ref[...] = (xv * r * w_ref[...]).astype(BF16)

    return _pcall(body, name="mla_norm", grid=(s // ts,),
                  in_specs=[_rows(ts, lr, col_q), _rows(ts, lr, col_kv), _full((1, lr)), _full((1, lr))],
                  out_specs=[_rows(ts, lr), _rows(ts, lr)],
                  out_shape=[SDS((s, lr), BF16), SDS((s, lr), BF16)])(proj, proj, qw, kvw)


def _mla_norm_bwd(proj, qw, kvw, dq, dkv, col_q, col_kv):
    s = proj.shape[0]
    lr = qw.shape[1]
    ts = _tile(s, 512, 8)

    def body(cq_ref, ckv_ref, qw_ref, kvw_ref, dq_ref, dkv_ref, oq_ref, okv_ref, dqw_ref, dkvw_ref):
        @pl.when(pl.program_id(0) == 0)
        def _():
            dqw_ref[...] = jnp.zeros_like(dqw_ref)
            dkvw_ref[...] = jnp.zeros_like(dkvw_ref)

        for x_ref, w_ref, d_ref, o_ref, dw_ref in ((cq_ref, qw_ref, dq_ref, oq_ref, dqw_ref),
                                                    (ckv_ref, kvw_ref, dkv_ref, okv_ref, dkvw_ref)):
            xv, dh = x_ref[...], d_ref[...]
            r = lax.rsqrt(jnp.mean(xv * xv, axis=-1, keepdims=True) + EPS)
            xh = xv * r
            dw_ref[...] += jnp.sum(dh * xh, axis=0, keepdims=True)
            dxh = dh * w_ref[...]
            o_ref[...] = (r * (dxh - xh * jnp.mean(dxh * xh, axis=-1, keepdims=True))).astype(BF16)

    return _pcall(body, name="mla_norm_bwd", grid=(s // ts,),
                  in_specs=[_rows(ts, lr, col_q), _rows(ts, lr, col_kv), _full((1, lr)), _full((1, lr)),
                            _rows(ts, lr), _rows(ts, lr)],
                  out_specs=[_rows(ts, lr), _rows(ts, lr), _full((1, lr)), _full((1, lr))],
                  out_shape=[SDS((s, lr), BF16), SDS((s, lr), BF16), SDS((1, lr), F32),
                             SDS((1, lr), F32)])(proj, proj, qw, kvw, dq, dkv)


def _rope_tables(pos, invf, sgn):
    ang = pos * invf
    return jnp.cos(ang), jnp.sin(ang) * sgn


def _swap_halves_lanes(y):
    lane = _lane_iota(y.shape)
    return jnp.where(lane < ROPE // 2, pltpu.roll(y, LANE - ROPE // 2, 1), pltpu.roll(y, ROPE // 2, 1))


def _rope_consts():
    half = ROPE // 2
    inv = ROPE_THETA ** (-jnp.arange(half, dtype=F32) / half)
    invf = jnp.concatenate([inv, inv, jnp.zeros((LANE - ROPE,), F32)])[None, :]
    sgn = jnp.concatenate([-jnp.ones((half,), F32), jnp.ones((half,), F32),
                           jnp.zeros((LANE - ROPE,), F32)])[None, :]
    return invf, sgn


def _mla_rope(qraw, kvraw, proj, pos, nh, misc_col):
    s = qraw.shape[0]
    ts = _tile(s, 256, 8)
    wq = nh * 2 * HEAD
    invf, sgn = _rope_consts()

    def body(q_ref, kv_ref, misc_ref, pos_ref, if_ref, sg_ref, qc_ref, kc_ref, v_ref):
        c, sn = _rope_tables(pos_ref[...], if_ref[...], sg_ref[...])
        lane = _lane_iota(c.shape)
        rot = lambda xb: xb * c + _swap_halves_lanes(xb) * sn
        qs = SM_SCALE * LOG2E
        krot = jnp.where(lane < ROPE, rot(misc_ref[...]), 0.0).astype(BF16)
        for h in range(nh):
            b0 = 2 * HEAD * h
            qc_ref[:, b0:b0 + HEAD] = (q_ref[:, b0:b0 + HEAD] * qs).astype(BF16)
            qc_ref[:, b0 + HEAD:b0 + 2 * HEAD] = (rot(q_ref[:, b0 + HEAD:b0 + 2 * HEAD]) * qs).astype(BF16)
            kc_ref[:, b0:b0 + HEAD] = kv_ref[:, b0:b0 + HEAD].astype(BF16)
            kc_ref[:, b0 + HEAD:b0 + 2 * HEAD] = krot
        v_ref[...] = kv_ref[:, wq:].astype(BF16)

    return _pcall(body, name="mla_rope", grid=(s // ts,),
                  in_specs=[_rows(ts, wq), _rows(ts, wq + nh * HEAD), _rows(ts, LANE, misc_col),
                            _rows(ts, 1), _full((1, LANE)), _full((1, LANE))],
                  out_specs=[_rows(ts, wq), _rows(ts, wq), _rows(ts, nh * HEAD)],
                  out_shape=[SDS((s, wq), BF16), SDS((s, wq), BF16), SDS((s, nh * HEAD), BF16)])(
                      qraw, kvraw, proj, pos, invf, sgn)


def _mla_rope_bwd(dqc, dkc, dv, pos, nh):
    s = dqc.shape[0]
    ts = _tile(s, 256, 8)
    wq = nh * 2 * HEAD
    invf, sgn = _rope_consts()

    def body(dq_ref, dk_ref, dv_ref, pos_ref, if_ref, sg_ref, oq_ref, okv_ref, okr_ref):
        c, sn = _rope_tables(pos_ref[...], if_ref[...], sg_ref[...])
        lane = _lane_iota(c.shape)
        unrot = lambda d: d * c + _swap_halves_lanes(d * sn)
        dkr = jnp.zeros(c.shape, F32)
        for h in range(nh):
            b0 = 2 * HEAD * h
            oq_ref[:, b0:b0 + HEAD] = (dq_ref[:, b0:b0 + HEAD] * SM_SCALE).astype(BF16)
            oq_ref[:, b0 + HEAD:b0 + 2 * HEAD] = (
                unrot(dq_ref[:, b0 + HEAD:b0 + 2 * HEAD]) * SM_SCALE).astype(BF16)
            okv_ref[:, b0:b0 + HEAD] = (dk_ref[:, b0:b0 + HEAD] * LN2).astype(BF16)
            okv_ref[:, b0 + HEAD:b0 + 2 * HEAD] = jnp.zeros((ts, HEAD), BF16)
            dkr = dkr + dk_ref[:, b0 + HEAD:b0 + 2 * HEAD]
        okv_ref[:, wq:] = dv_ref[...].astype(BF16)
        okr_ref[...] = jnp.where(lane < ROPE, unrot(jnp.where(lane < ROPE, dkr * LN2, 0.0)), 0.0)

    return _pcall(body, name="mla_rope_bwd", grid=(s // ts,),
                  in_specs=[_rows(ts, wq), _rows(ts, wq), _rows(ts, nh * HEAD), _rows(ts, 1),
                            _full((1, LANE)), _full((1, LANE))],
                  out_specs=[_rows(ts, wq), _rows(ts, wq + nh * HEAD), _rows(ts, LANE)],
                  out_shape=[SDS((s, wq), BF16), SDS((s, wq + nh * HEAD), BF16), SDS((s, LANE), F32)])(
                      dqc, dkc, dv, pos, invf, sgn)


def _causal_mask(blk):
    i = lax.broadcasted_iota(jnp.int32, (blk, blk), 0)
    j = lax.broadcasted_iota(jnp.int32, (blk, blk), 1)
    return j <= i


MLA_HP = 2


def _pair_pack(a, b):
    return jnp.where(_lane_iota(a.shape) < LANE // 2, a, b)


def _pair_unpack(x, e):
    lane = _lane_iota(x.shape)
    keep = (lane < LANE // 2) if e == 0 else (lane >= LANE // 2)
    return jnp.where(keep, x, pltpu.roll(x, LANE // 2, 1))


def _mla_fwd(qc, kc, v, nh, rider):
    s = qc.shape[0]
    blk = _tile(s, MLA_BLOCK)
    nb = s // blk
    rep = blk // LANE
    hp = MLA_HP
    assert nh % hp == 0
    once = pl.Buffered(1)
    r_in, r_out = len(rider.arrays), len(rider.out_shapes)

    def body(*refs):
        q_ref, k_ref, v_ref = refs[:3]
        o_ref, lse_ref = refs[3 + r_in:5 + r_in]
        m_sc, l_sc, acc = refs[5 + r_in + r_out:8 + r_in + r_out]
        r_refs = (refs[3:3 + r_in], refs[5 + r_in:5 + r_in + r_out], refs[8 + r_in + r_out:])
        i = pl.program_id(1)
        grid_step = pl.program_id(0) * nb + i
        _ride_begin(rider, r_refs, grid_step)
        m_sc[...] = jnp.full_like(m_sc, -1e30)
        l_sc[...] = jnp.zeros_like(l_sc)
        acc[...] = jnp.zeros_like(acc)

        def step(j, masked):
            rows = pl.ds(pl.multiple_of(j * blk, blk), blk)
            es = range(hp)
            sc = [_dot_nt(q_ref[:, 2 * HEAD * e:2 * HEAD * (e + 1)], k_ref[rows, 2 * HEAD * e:2 * HEAD * (e + 1)])
                  for e in es]
            if masked:
                sc = [jnp.where(_causal_mask(blk), x, -1e30) for x in sc]
            m_prev = [m_sc[e] for e in es]
            m_new = [jnp.maximum(m_prev[e], jnp.max(sc[e], axis=-1, keepdims=True)) for e in es]
            p = [jnp.exp2(sc[e] - jnp.tile(m_new[e], (1, rep))) for e in es]
            alpha = [jnp.exp2(m_prev[e] - m_new[e]) for e in es]
            pv = [_dot(p[e], v_ref[rows, HEAD * e:HEAD * (e + 1)]) for e in es]
            for e in es:
                l_sc[e] = alpha[e] * l_sc[e] + jnp.sum(p[e], axis=-1, keepdims=True)
                acc[e] = alpha[e] * acc[e] + pv[e]
                m_sc[e] = m_new[e]

        def loop_body(j, carry):
            step(j, False)
            return carry

        lax.fori_loop(0, i, loop_body, 0)
        step(i, True)
        for e in range(hp):
            o_ref[:, HEAD * e:HEAD * (e + 1)] = acc[e] / l_sc[e]
        lse = [m_sc[e] + jnp.log(l_sc[e]) * LOG2E for e in range(hp)]
        lse_ref[...] = _pair_pack(lse[0], lse[1])
        _ride_end(rider, r_refs, grid_step, (nh // hp) * nb)

    outs = _pcall(
        body, name="mla_fwd", grid=(nh // hp, nb),
        in_specs=[pl.BlockSpec((blk, hp * 2 * HEAD), lambda g, i: (i, g)),
                  pl.BlockSpec((s, hp * 2 * HEAD), lambda g, i: (0, g), pipeline_mode=once),
                  pl.BlockSpec((s, hp * HEAD), lambda g, i: (0, g), pipeline_mode=once)] + [_ANY] * r_in,
        out_specs=[pl.BlockSpec((blk, hp * HEAD), lambda g, i: (i, g)),
                   pl.BlockSpec((None, blk, LANE), lambda g, i: (g, i, 0))] + [_ANY] * r_out,
        out_shape=[SDS((s, nh * HEAD), F32), SDS((nh // hp, s, LANE), F32)] + rider.out_shapes,
        scratch=[pltpu.VMEM((hp, blk, LANE), F32), pltpu.VMEM((hp, blk, LANE), F32),
                 pltpu.VMEM((hp, blk, HEAD), F32)] + rider.scratch)(qc, kc, v, *rider.arrays)
    return outs[0], outs[1], outs[2:]


def _mla_bwd(qc, kc, v, do, lse, delta, nh, rider):
    s = qc.shape[0]
    blk = _tile(s, MLA_BLOCK)
    nb = s // blk
    rep = blk // LANE
    hp = MLA_HP
    once = pl.Buffered(1)
    r_in, r_out = len(rider.arrays), len(rider.out_shapes)
    qs = [slice(2 * HEAD * e, 2 * HEAD * (e + 1)) for e in range(hp)]
    vs = [slice(HEAD * e, HEAD * (e + 1)) for e in range(hp)]

    def body(*refs):
        q_ref, do_ref, lse_ref, dl_ref, k_ref, v_ref = refs[:6]
        dq_ref, dk_ref, dv_ref = refs[6 + r_in:9 + r_in]
        dk_acc, dv_acc = refs[9 + r_in + r_out:11 + r_in + r_out]
        r_refs = (refs[6:6 + r_in], refs[9 + r_in:9 + r_in + r_out], refs[11 + r_in + r_out:])
        j = pl.program_id(1)
        grid_step = pl.program_id(0) * nb + j
        _ride_begin(rider, r_refs, grid_step)

        @pl.when(j == 0)
        def _():
            dq_ref[...] = jnp.zeros_like(dq_ref)

        dk_acc[...] = jnp.zeros_like(dk_acc)
        dv_acc[...] = jnp.zeros_like(dv_acc)
        es = range(hp)
        kj = [k_ref[:, qs[e]] for e in es]
        vj = [v_ref[:, vs[e]] for e in es]

        def step(i, masked):
            rows = pl.ds(pl.multiple_of(i * blk, blk), blk)
            qi = [q_ref[rows, qs[e]] for e in es]
            doi = [do_ref[rows, vs[e]] for e in es]
            lse, dl = lse_ref[rows, :], dl_ref[rows, :]
            sc = [_dot_nt(qi[e], kj[e]) for e in es]
            dp = [_dot_nt(doi[e], vj[e]) for e in es]
            if masked:
                sc = [jnp.where(_causal_mask(blk), x, -1e30) for x in sc]
            p = [jnp.exp2(sc[e] - jnp.tile(_pair_unpack(lse, e), (1, rep))) for e in es]
            ds = [p[e] * (dp[e] - jnp.tile(_pair_unpack(dl, e), (1, rep))) for e in es]
            dv = [_dot_tn(p[e], doi[e]) for e in es]
            dk = [_dot_tn(ds[e], qi[e]) for e in es]
            dq = [_dot(ds[e], kj[e]) for e in es]
            for e in es:
                dv_acc[:, vs[e]] += dv[e]
                dk_acc[:, qs[e]] += dk[e]
                dq_ref[rows, qs[e]] += dq[e]

        def loop_body(i, carry):
            step(i, False)
            return carry

        step(j, True)
        lax.fori_loop(j + 1, nb, loop_body, 0)
        dk_ref[...] = dk_acc[...]
        dv_ref[...] = dv_acc[...]
        _ride_end(rider, r_refs, grid_step, (nh // hp) * nb)

    outs = _pcall(
        body, name="mla_bwd", grid=(nh // hp, nb),
        in_specs=[pl.BlockSpec((s, hp * 2 * HEAD), lambda g, j: (0, g), pipeline_mode=once),
                  pl.BlockSpec((s, hp * HEAD), lambda g, j: (0, g), pipeline_mode=once),
                  pl.BlockSpec((None, s, LANE), lambda g, j: (g, 0, 0), pipeline_mode=once),
                  pl.BlockSpec((None, s, LANE), lambda g, j: (g, 0, 0), pipeline_mode=once),
                  pl.BlockSpec((blk, hp * 2 * HEAD), lambda g, j: (j, g)),
                  pl.BlockSpec((blk, hp * HEAD), lambda g, j: (j, g))] + [_ANY] * r_in,
        out_specs=[pl.BlockSpec((s, hp * 2 * HEAD), lambda g, j: (0, g), pipeline_mode=once),
                   pl.BlockSpec((blk, hp * 2 * HEAD), lambda g, j: (j, g)),
                   pl.BlockSpec((blk, hp * HEAD), lambda g, j: (j, g))] + [_ANY] * r_out,
        out_shape=[SDS((s, nh * 2 * HEAD), F32), SDS((s, nh * 2 * HEAD), F32),
                   SDS((s, nh * HEAD), F32)] + rider.out_shapes,
        scratch=[pltpu.VMEM((blk, hp * 2 * HEAD), F32), pltpu.VMEM((blk, hp * HEAD), F32)] + rider.scratch,
        vmem=VMEM_LIMIT_WIDE)(qc, do, lse, delta, kc, v, *rider.arrays)
    return outs[0], outs[1], outs[2], outs[3:]


def _mix_fwd(og, proj, om, gw, mw, nh, z_col):
    s = og.shape[0]
    w = nh * HEAD
    ts = _tile(s, 256, 8)

    def body(og_ref, z_ref, om_ref, gw_ref, mw_ref, o_ref):
        for h in range(nh):
            hs = slice(HEAD * h, HEAD * (h + 1))
            a = og_ref[:, hs]
            r = lax.rsqrt(jnp.mean(a * a, axis=-1, keepdims=True) + EPS)
            o_ref[:, hs] = (a * r * gw_ref[...] * _silu(z_ref[:, hs])).astype(BF16)
            b = om_ref[:, hs]
            r = lax.rsqrt(jnp.mean(b * b, axis=-1, keepdims=True) + EPS)
            o_ref[:, w + HEAD * h:w + HEAD * (h + 1)] = (b * r * mw_ref[...]).astype(BF16)

    return _pcall(body, name="mix_fwd", grid=(s // ts,),
                  in_specs=[_rows(ts, w), _rows(ts, w, z_col), _rows(ts, w), _full((1, HEAD)),
                            _full((1, HEAD))],
                  out_specs=_rows(ts, 2 * w), out_shape=SDS((s, 2 * w), BF16))(og, proj, om, gw, mw)


def _mix_bwd(dmix, og, proj, om, gw, mw, nh, z_col):
    s = og.shape[0]
    w = nh * HEAD
    ts = _tile(s, 256, 8)

    def body(d_ref, og_ref, z_ref, om_ref, gw_ref, mw_ref, dog_ref, dz_ref, dom_ref, dgw_ref, dmw_ref,
             dl_ref):
        @pl.when(pl.program_id(0) == 0)
        def _():
            dgw_ref[...] = jnp.zeros_like(dgw_ref)
            dmw_ref[...] = jnp.zeros_like(dmw_ref)

        dgw = jnp.zeros((1, HEAD), F32)
        dmw = jnp.zeros((1, HEAD), F32)
        deltas = []
        for h in range(nh):
            hs = slice(HEAD * h, HEAD * (h + 1))
            a, z, dy = og_ref[:, hs], z_ref[:, hs], d_ref[:, hs]
            r = lax.rsqrt(jnp.mean(a * a, axis=-1, keepdims=True) + EPS)
            ah = a * r
            sz = _silu(z)
            dz_ref[:, hs] = (dy * (ah * gw_ref[...]) * _dsilu(z)).astype(BF16)
            dn = dy * sz
            dgw = dgw + jnp.sum(dn * ah, axis=0, keepdims=True)
            dah = dn * gw_ref[...]
            dog_ref[:, hs] = r * (dah - ah * jnp.mean(dah * ah, axis=-1, keepdims=True))
            b, dyb = om_ref[:, hs], d_ref[:, w + HEAD * h:w + HEAD * (h + 1)]
            r = lax.rsqrt(jnp.mean(b * b, axis=-1, keepdims=True) + EPS)
            bh = b * r
            dmw = dmw + jnp.sum(dyb * bh, axis=0, keepdims=True)
            dbh = dyb * mw_ref[...]
            dom = r * (dbh - bh * jnp.mean(dbh * bh, axis=-1, keepdims=True))
            dom_ref[:, hs] = dom.astype(BF16)
            deltas.append(jnp.broadcast_to(jnp.sum(dom * b, axis=-1, keepdims=True), (ts, LANE)))
        for g in range(nh // MLA_HP):
            dl_ref[g] = _pair_pack(deltas[2 * g], deltas[2 * g + 1])
        dgw_ref[...] += dgw
        dmw_ref[...] += dmw

    return _pcall(body, name="mix_bwd", grid=(s // ts,),
                  in_specs=[_rows(ts, 2 * w), _rows(ts, w), _rows(ts, w, z_col), _rows(ts, w),
                            _full((1, HEAD)), _full((1, HEAD))],
                  out_specs=[_rows(ts, w), _rows(ts, w), _rows(ts, w), _full((1, HEAD)), _full((1, HEAD)),
                             pl.BlockSpec((nh // MLA_HP, ts, LANE), lambda i: (0, i, 0))],
                  out_shape=[SDS((s, w), F32), SDS((s, w), BF16), SDS((s, w), BF16), SDS((1, HEAD), F32),
                             SDS((1, HEAD), F32), SDS((nh // MLA_HP, s, LANE), F32)])(dmix, og, proj, om, gw, mw)


def _swiglu_fwd(h2, wg, wu):
    m, kdim = h2.shape
    tn = wg.shape[2]
    n = 4 * tn
    tm, tk = _tile(m, 512), _tile(kdim, 2048)
    nk = kdim // tk

    def body(a_ref, g_ref, u_ref, act_ref, go_ref, uo_ref, gacc, uacc):
        k = pl.program_id(2)

        @pl.when(k == 0)
        def _():
            gacc[...] = jnp.zeros_like(gacc)
            uacc[...] = jnp.zeros_like(uacc)

        a = a_ref[...]
        gacc[...] += _dot(a, g_ref[...])
        uacc[...] += _dot(a, u_ref[...])

        @pl.when(k == nk - 1)
        def _():
            g, u = gacc[...], uacc[...]
            act_ref[...] = (_silu(g) * u).astype(BF16)
            go_ref[...] = g.astype(BF16)
            uo_ref[...] = u.astype(BF16)

    a_spec = pl.BlockSpec((tm, tk), lambda i, j, k: (i, k))
    b_spec = pl.BlockSpec((None, tk, tn), lambda i, j, k: (j, k, 0))
    o_spec = pl.BlockSpec((tm, tn), lambda i, j, k: (i, j))
    return _pcall(body, name="swiglu_fwd", grid=(m // tm, n // tn, nk),
                  in_specs=[a_spec, b_spec, b_spec], out_specs=[o_spec] * 3,
                  out_shape=[SDS((m, n), BF16)] * 3,
                  scratch=[pltpu.VMEM((tm, tn), F32), pltpu.VMEM((tm, tn), F32)])(h2, wg, wu)


def _swiglu_bwd(dx3, wd, g, u):
    m, kdim = dx3.shape
    n = wd.shape[0]
    tm, tn = _tile(m, 1024), _tile(n, 512)
    parts = 2 if tm % 16 == 0 else 1
    th = tm // parts

    def body(a_ref, b_ref, g_ref, u_ref, dg_ref, du_ref):
        b = b_ref[...]
        rows = [slice(th * c, th * (c + 1)) for c in range(parts)]
        da = [_dot_nt(a_ref[rs, :], b) for rs in rows]
        for rs, d in zip(rows, da):
            gv, uv = g_ref[rs, :].astype(F32), u_ref[rs, :].astype(F32)
            dg_ref[rs, :] = (d * uv * _dsilu(gv)).astype(BF16)
            du_ref[rs, :] = (d * _silu(gv)).astype(BF16)

    a_spec = pl.BlockSpec((tm, kdim), lambda i, j: (i, 0))
    b_spec = pl.BlockSpec((tn, kdim), lambda i, j: (j, 0))
    o_spec = pl.BlockSpec((tm, tn), lambda i, j: (i, j))
    return _pcall(body, name="swiglu_bwd", grid=(m // tm, n // tn),
                  in_specs=[a_spec, b_spec, o_spec, o_spec], out_specs=[o_spec] * 2,
                  out_shape=[SDS((m, n), BF16)] * 2)(dx3, wd, g, u)


def _sum_pair(g, recv, place, name):
    _, _, rh, c = g.shape
    tr = _tile(rh, 256, 16)

    def body(pl_ref, g_ref, r_ref, o16_ref, own_ref):
        sm = g_ref[...].astype(F32) + r_ref[...].astype(F32)
        o16_ref[...] = sm.astype(BF16)

        @pl.when(pl.program_id(1) == pl_ref[1])
        def _():
            own_ref[...] = sm

    grid_spec = pltpu.PrefetchScalarGridSpec(
        num_scalar_prefetch=1, grid=(rh // tr, 4),
        in_specs=[pl.BlockSpec((None, None, tr, c), lambda i, t, p: (t, p[0], i, 0)),
                  pl.BlockSpec((None, tr, c), lambda i, t, p: (t, i, 0))],
        out_specs=[pl.BlockSpec((None, tr, c), lambda i, t, p: (t, i, 0)),
                   pl.BlockSpec((tr, c), lambda i, t, p: (i, 0))])
    return pl.pallas_call(
        body, name=name, grid_spec=grid_spec,
        out_shape=[SDS((4, rh, c), BF16), SDS((rh, c), F32)],
        compiler_params=pltpu.CompilerParams(dimension_semantics=("arbitrary",) * 2,
                                             vmem_limit_bytes=VMEM_LIMIT))(place, g, recv)


def _sum_chips(own, recv, name):
    rh, c = own.shape
    tr = _tile(rh, 256, 16)

    def body(o_ref, r_ref, out_ref):
        acc = o_ref[...]
        for j in range(3):
            acc = acc + r_ref[j].astype(F32)
        out_ref[...] = acc

    return _pcall(body, name=name, grid=(rh // tr,),
                  in_specs=[_rows(tr, c), pl.BlockSpec((3, tr, c), lambda i: (0, i, 0))],
                  out_specs=_rows(tr, c), out_shape=SDS(own.shape, F32))(own, recv)


def _adamw_update(wv, gv, mv, vv):
    mn = ADAM_B1 * mv + (1.0 - ADAM_B1) * gv
    vn = ADAM_B2 * vv + (1.0 - ADAM_B2) * (gv * gv)
    m_hat = mn / (1.0 - ADAM_B1 ** ADAM_STEP)
    v_hat = vn / (1.0 - ADAM_B2 ** ADAM_STEP)
    return -ADAM_LR * (m_hat / (jnp.sqrt(v_hat) + ADAM_EPS) + ADAM_WD * wv), mn, vn


def _adamw(w, g, m, v, name):
    r, c = w.shape
    tr = _tile(r, 256, 8)

    def body(w_ref, g_ref, m_ref, v_ref, d_ref, mo_ref, vo_ref):
        d_ref[...], mo_ref[...], vo_ref[...] = _adamw_update(w_ref[...], g_ref[...], m_ref[...], v_ref[...])

    spec = _rows(tr, c)
    return _pcall(body, name=name, grid=(r // tr,), in_specs=[spec] * 4, out_specs=[spec] * 3,
                  out_shape=[SDS(w.shape, F32)] * 3)(w, g, m, v)


def _adamw_halves(w, mine, theirs, m, v, place, name):
    r, c = w.shape
    rh = r // 2
    tr = _tile(rh, 256, 8)
    nt = rh // tr

    def body(p_ref, w_ref, a_ref, b_ref, m_ref, v_ref, g_ref, d_ref, mo_ref, vo_ref):
        gv = jnp.where(pl.program_id(0) // nt == p_ref[0], a_ref[...], b_ref[...])
        g_ref[...] = gv
        d_ref[...], mo_ref[...], vo_ref[...] = _adamw_update(w_ref[...], gv, m_ref[...], v_ref[...])

    full = pl.BlockSpec((tr, c), lambda i, p: (i, 0))
    half = pl.BlockSpec((tr, c), lambda i, p: (i % nt, 0))
    grid_spec = pltpu.PrefetchScalarGridSpec(num_scalar_prefetch=1, grid=(2 * nt,),
                                             in_specs=[full, half, half, full, full], out_specs=[full] * 4)
    return pl.pallas_call(
        body, name=name, grid_spec=grid_spec, out_shape=[SDS(w.shape, F32)] * 4,
        compiler_params=pltpu.CompilerParams(dimension_semantics=("arbitrary",),
                                             vmem_limit_bytes=VMEM_LIMIT))(place, w, mine, theirs, m, v)


def _place():
    x, y, c = lax.axis_index("x"), lax.axis_index("y"), lax.axis_index("c")
    chips = [(1 - x, y), (x, 1 - y), (1 - x, 1 - y)]
    return x, y, c, chips


_ANY = pl.BlockSpec(memory_space=pl.ANY)


def _remote(src, dst, sems, k, to):
    return pltpu.make_async_remote_copy(src_ref=src, dst_ref=dst, send_sem=sems[0].at[k], recv_sem=sems[1].at[k],
                                        device_id=to, device_id_type=MESH)


class _Gather:
    def __init__(self, shards):
        n = len(shards)
        self.arrays = list(shards)
        self.out_shapes = [SDS((4,) + a.shape, a.dtype) for a in shards]
        self.scratch = [pltpu.SemaphoreType.DMA((7 * n,)), pltpu.SemaphoreType.DMA((7 * n,))]

    def _plan(self, ins, outs, sems):
        x, y, c, chips = _place()
        own, sib = 2 * x + y, (x, y, 1 - c)
        plan = []
        for wi, (w, o) in enumerate(zip(ins, outs)):
            rh = w.shape[0] // 2
            mine, theirs = pl.ds(c * rh, rh), pl.ds((1 - c) * rh, rh)
            whole = _remote(w, o.at[own], sems, 7 * wi + 6, sib)
            ici, d2d, d2d_in = [], [], []
            for j, (tx, ty) in enumerate(chips):
                t = 2 * tx + ty
                ici.append(_remote(w.at[mine], o.at[own, mine], sems, 7 * wi + j, (tx, ty, c)))
                d2d.append(_remote(o.at[t, mine], o.at[t, mine], sems, 7 * wi + 3 + j, sib))
                d2d_in.append(_remote(o.at[t, theirs], o.at[t, theirs], sems, 7 * wi + 3 + j, sib))
            plan.append((whole, ici, d2d, d2d_in))
        return plan

    def begin(self, ins, outs, sems):
        for whole, ici, _, _ in self._plan(ins, outs, sems):
            whole.start()
            for cp in ici:
                cp.start()

    def middle(self, ins, outs, sems):
        for _, ici, d2d, _ in self._plan(ins, outs, sems):
            for cp_in, cp_on in zip(ici, d2d):
                cp_in.wait_recv()
                cp_on.start()

    def finish(self, ins, outs, sems):
        for whole, ici, d2d, d2d_in in self._plan(ins, outs, sems):
            for cp in d2d_in:
                cp.wait_recv()
            for cp in ici + d2d:
                cp.wait_send()
            whole.wait()


class _Swap:
    def __init__(self, grads):
        n = len(grads)
        self.arrays = list(grads)
        self.out_shapes = [SDS((4,) + g.shape[2:], g.dtype) for g in grads]
        self.scratch = [pltpu.SemaphoreType.DMA((4 * n,)), pltpu.SemaphoreType.DMA((4 * n,))]

    def _plan(self, ins, outs, sems):
        x, y, c, _ = _place()
        return [_remote(g.at[t, 1 - c], o.at[t], sems, 4 * wi + t, (x, y, 1 - c))
                for wi, (g, o) in enumerate(zip(ins, outs)) for t in range(4)]

    def begin(self, ins, outs, sems):
        for cp in self._plan(ins, outs, sems):
            cp.start()

    def middle(self, ins, outs, sems):
        pass

    def finish(self, ins, outs, sems):
        for cp in self._plan(ins, outs, sems):
            cp.wait()


class _Exchange:
    def __init__(self, pieces):
        n = len(pieces)
        self.arrays = list(pieces)
        self.out_shapes = [SDS((3,) + p.shape[1:], p.dtype) for p in pieces]
        self.scratch = [pltpu.SemaphoreType.DMA((3 * n,)), pltpu.SemaphoreType.DMA((3 * n,))]

    def _plan(self, ins, outs, sems):
        x, y, c, chips = _place()
        return [_remote(g.at[2 * tx + ty], o.at[j], sems, 3 * wi + j, (tx, ty, c))
                for wi, (g, o) in enumerate(zip(ins, outs)) for j, (tx, ty) in enumerate(chips)]

    def begin(self, ins, outs, sems):
        for cp in self._plan(ins, outs, sems):
            cp.start()

    def middle(self, ins, outs, sems):
        pass

    def finish(self, ins, outs, sems):
        for cp in self._plan(ins, outs, sems):
            cp.wait()


class _Share:
    def __init__(self, totals):
        n = len(totals)
        self.arrays = list(totals)
        self.out_shapes = [SDS(t.shape, t.dtype) for t in totals]
        self.scratch = [pltpu.SemaphoreType.DMA((n,)), pltpu.SemaphoreType.DMA((n,))]

    def _plan(self, ins, outs, sems):
        x, y, c, _ = _place()
        return [_remote(t, o, sems, wi, (x, y, 1 - c)) for wi, (t, o) in enumerate(zip(ins, outs))]

    def begin(self, ins, outs, sems):
        for cp in self._plan(ins, outs, sems):
            cp.start()

    def middle(self, ins, outs, sems):
        pass

    def finish(self, ins, outs, sems):
        for cp in self._plan(ins, outs, sems):
            cp.wait()


def _ride_begin(rider, r_refs, step):
    @pl.when(step == 0)
    def _():
        rider.begin(*r_refs)


def _ride_end(rider, r_refs, step, nsteps):
    @pl.when(step == min(3 * nsteps // 4, nsteps - 1))
    def _():
        rider.middle(*r_refs)

    @pl.when(step == nsteps - 1)
    def _():
        rider.finish(*r_refs)


def _comm(rider, name):
    n_in, n_out = len(rider.arrays), len(rider.out_shapes)

    def body(*refs):
        r_refs = (refs[:n_in], refs[n_in:n_in + n_out], refs[n_in + n_out:])
        rider.begin(*r_refs)
        rider.middle(*r_refs)
        rider.finish(*r_refs)

    return pl.pallas_call(body, name=name, out_shape=rider.out_shapes, in_specs=[_ANY] * n_in,
                          out_specs=[_ANY] * n_out, scratch_shapes=rider.scratch)(*rider.arrays)


def _small_allreduce(pk, name):
    r = pk.shape[0]
    rels = [(dx, dy, dc) for dx in (0, 1) for dy in (0, 1) for dc in (0, 1) if dx or dy or dc]

    def body(p_ref, o_ref, buf, send_sems, recv_sems):
        x, y, c, _ = _place()
        me = 4 * x + 2 * y + c
        buf[me] = p_ref[...]
        cps = []
        for k, (dx, dy, dc) in enumerate(rels):
            to = (1 - x if dx else x, 1 - y if dy else y, 1 - c if dc else c)
            cps.append(pltpu.make_async_remote_copy(src_ref=p_ref, dst_ref=buf.at[me], send_sem=send_sems.at[k],
                                                    recv_sem=recv_sems.at[k], device_id=to,
                                                    device_id_type=MESH))
        for cpy in cps:
            cpy.start()
        for cpy in cps:
            cpy.wait()
        acc = buf[0]
        for d in range(1, 8):
            acc = acc + buf[d]
        o_ref[...] = acc

    vm = pl.BlockSpec(memory_space=pltpu.VMEM)
    return pl.pallas_call(body, name=name, out_shape=SDS(pk.shape, F32), in_specs=[vm], out_specs=vm,
                          scratch_shapes=[pltpu.VMEM((8, r, LANE), F32), pltpu.SemaphoreType.DMA((7,)),
                                          pltpu.SemaphoreType.DMA((7,))])(pk)


ATTN_W = ("w_in", "w_uq", "w_ukv", "w_out")
FFN_W = ("w_gate", "w_up", "w_down")
BIG = ATTN_W + FFN_W


def _cols_from_chips(g):
    return jnp.concatenate([g[t] for t in range(4)], axis=1)


def _cols_to_chips(full):
    r, n = full.shape
    return full.reshape(r, 4, n // 4).transpose(1, 0, 2).reshape(4, 2, r // 2, n // 4)


def _rows_to_chips(full):
    n, c = full.shape
    return full.reshape(4, 2, n // 8, c)


def _permute_w_in(w, nh):
    d = w.shape[0]
    g = 4 * nh * HEAD
    lr = (w.shape[1] - g - 2 * nh - ROPE) // 2
    o = g + 2 * nh
    pad = jnp.zeros((d, LANE - ROPE - 8 - nh), w.dtype)
    pad8 = jnp.zeros((d, 8 - nh), w.dtype)
    return jnp.concatenate([w[:, :g], w[:, o:o + 2 * lr], w[:, o + 2 * lr:], w[:, g:g + nh], pad8,
                            w[:, g + nh:g + 2 * nh], pad, jnp.zeros((d, LANE), w.dtype)], axis=1)


def _unpermute_w_in(wp, nh, lr):
    g = 4 * nh * HEAD
    mc = g + 2 * lr
    return jnp.concatenate([wp[:, :g], wp[:, mc + B_LANE:mc + B_LANE + nh], wp[:, mc + A_LANE:mc + A_LANE + nh],
                            wp[:, g:g + 2 * lr], wp[:, mc:mc + ROPE]], axis=1)


def _permute_w_uq(w, nh):
    lr = w.shape[0]
    w3 = w.reshape(lr, nh, HEAD + ROPE)
    return jnp.concatenate([w3, jnp.zeros((lr, nh, HEAD - ROPE), w.dtype)], axis=2).reshape(lr, nh * 2 * HEAD)


def _unpermute_w_uq(wp, nh):
    lr = wp.shape[0]
    return wp.reshape(lr, nh, 2 * HEAD)[:, :, :HEAD + ROPE].reshape(lr, nh * (HEAD + ROPE))


def _permute_w_ukv(w, nh):
    lr = w.shape[0]
    w3 = w.reshape(lr, nh, 2 * HEAD)
    kp = jnp.concatenate([w3[:, :, :HEAD], jnp.zeros((lr, nh, HEAD), w.dtype)], axis=2)
    return jnp.concatenate([kp.reshape(lr, nh * 2 * HEAD), w3[:, :, HEAD:].reshape(lr, nh * HEAD)], axis=1)


def _unpermute_w_ukv(wp, nh):
    lr = wp.shape[0]
    kp = wp[:, :nh * 2 * HEAD].reshape(lr, nh, 2 * HEAD)[:, :, :HEAD]
    vp = wp[:, nh * 2 * HEAD:].reshape(lr, nh, HEAD)
    return jnp.concatenate([kp, vp], axis=2).reshape(lr, nh * 2 * HEAD)


def _sum_pairs(grads, recv, place, tag):
    sums = [_sum_pair(g, r, place, "sum_pair_%s%d" % (tag, k)) for k, (g, r) in enumerate(zip(grads, recv))]
    return [s[0] for s in sums], [s[1] for s in sums]


def _reduce_end(own, recv, tag):
    return [_sum_chips(o, r, "sum_chips_%s%d" % (tag, k)) for k, (o, r) in enumerate(zip(own, recv))]


def _step(x, pos, tgt, w_in, attn_shards, ffn_shards, small, place):
    nh = small["a_log"].shape[1]
    lr = small["q_norm_w"].shape[1]
    w = nh * HEAD
    z_col, col_q, col_kv = 3, 4 * w // lr, 4 * w // lr + 1
    misc_c = 4 * w + 2 * lr
    misc_col = misc_c // LANE
    assert (4 * w) % lr == 0 and small["kv_norm_w"].shape[1] == lr

    win_p = _permute_w_in(w_in, nh)
    zl = jnp.zeros((1, LANE), F32)
    alog_l = zl.at[:, A_LANE:A_LANE + nh].set(small["a_log"])
    dtb_l = zl.at[:, A_LANE:A_LANE + nh].set(small["dt_bias"])
    conv_w = small["conv_w"]

    h1 = _norm_fwd(x, small["attn_norm_w"], "norm1")
    proj, (uq4, ukv4, out4) = _mm([(h1, win_p)], name="proj_in", rider=_Gather(attn_shards))
    wuq_p = _permute_w_uq(_cols_from_chips(uq4), nh)
    wukv_p = _permute_w_ukv(_cols_from_chips(ukv4), nh)
    w_out = out4.reshape(-1, out4.shape[2])
    gq, gk, gv, gb, gbt = _gdn_prep(proj, conv_w, alog_l, dtb_l, nh, misc_col)
    o_gdn, states = _gdn_fwd(gq, gk, gv, gb, gbt, nh)
    cqn, ckvn = _mla_norm(proj, small["q_norm_w"], small["kv_norm_w"], col_q, col_kv)
    qraw = _mm([(cqn, wuq_p)], name="proj_uq")
    kvraw = _mm([(ckvn, wukv_p)], name="proj_ukv")
    qc, kc, vv = _mla_rope(qraw, kvraw, proj, pos, nh, misc_col)
    o_mla, lse, (wg4, wu4, wd4) = _mla_fwd(qc, kc, vv, nh, _Gather(ffn_shards))
    w_down = wd4.reshape(-1, wd4.shape[2])
    mixed = _mix_fwd(o_gdn, proj, o_mla, small["gdn_norm_w"], small["mla_out_norm_w"], nh, z_col)
    x2 = _mm([(mixed, w_out)], name="proj_out", res=x)
    h2 = _norm_fwd(x2, small["ffn_norm_w"], "norm2")
    act, gpre, upre = _swiglu_fwd(h2, wg4, wu4)
    x3 = _mm([(act, w_down)], name="proj_down", res=x2, tk=2816)
    dx3, d_final, loss, dx3h = _final_loss(x3, tgt, small["final_norm_w"])

    gs = {"final_norm_w": d_final}
    dgate, dup = _swiglu_bwd(dx3h, w_down, gpre, upre)
    g_down = _rows_to_chips(_mm([(act, dx3h)], name="dw_down", ta=True, out_dtype=BF16))
    g_gate = _mm([(h2, dgate)], name="dw_gate", ta=True, out_dtype=BF16, out_chips=True)
    g_up = _mm([(h2, dup)], name="dw_up", ta=True, out_dtype=BF16, out_chips=True)
    halves = lambda g: g.reshape(4, 2, g.shape[1] // 2, g.shape[2])
    ffn_g = [halves(g_gate), halves(g_up), g_down]
    dh2, ffn_sib = _mm([(dgate, wg4), (dup, wu4)], name="dh2", tb=True, b_chips=True, rider=_Swap(ffn_g))
    ffn16, ffn_own = _sum_pairs(ffn_g, ffn_sib, place, "ffn")
    dx2, gs["ffn_norm_w"], dx2h = _norm_bwd(dh2, x2, small["ffn_norm_w"], dx3, "norm2_bwd", True)
    dmix = _mm([(dx2h, w_out)], name="dmix", tb=True)
    g_out = _rows_to_chips(_mm([(mixed, dx2h)], name="dw_out", ta=True, out_dtype=BF16))
    d_ogdn, dz, d_omla, gs["gdn_norm_w"], gs["mla_out_norm_w"], delta = _mix_bwd(
        dmix, o_gdn, proj, o_mla, small["gdn_norm_w"], small["mla_out_norm_w"], nh, z_col)
    dqc, dkc, dvv, ffn_recv = _mla_bwd(qc, kc, vv, d_omla, lse, delta, nh, _Exchange(ffn16))
    ffn_tot = _reduce_end(ffn_own, ffn_recv, "ffn")
    dqraw, dkvraw, dkr = _mla_rope_bwd(dqc, dkc, dvv, pos, nh)
    dcqn = _mm([(dqraw, wuq_p)], name="dcqn", tb=True)
    dckvn = _mm([(dkvraw, wukv_p)], name="dckvn", tb=True)
    g_uq = _cols_to_chips(_unpermute_w_uq(_mm([(cqn, dqraw)], name="dw_uq", ta=True, out_dtype=BF16), nh))
    g_ukv = _cols_to_chips(_unpermute_w_ukv(_mm([(ckvn, dkvraw)], name="dw_ukv", ta=True, out_dtype=BF16), nh))
    dcq, dckv, gs["q_norm_w"], gs["kv_norm_w"] = _mla_norm_bwd(
        proj, small["q_norm_w"], small["kv_norm_w"], dcqn, dckvn, col_q, col_kv)
    (dgq, dgk, dgv, dgb), ffn_shared = _gdn_bwd(gq, gk, gv, gb, gbt, states, d_ogdn, nh, _Share(ffn_tot))
    dconv, dmisc, gs["conv_w"], dal, ddb = _gdn_prep_bwd(
        proj, conv_w, alog_l, dtb_l, dgq, dgk, dgv, dgb, dkr, nh, misc_col)
    gs["a_log"] = dal[:, A_LANE:A_LANE + nh]
    gs["dt_bias"] = ddb[:, A_LANE:A_LANE + nh]
    dqkv = _conv_bwd_input(dconv, conv_w)
    dproj = jnp.concatenate([dqkv, dz, dcq, dckv, dmisc, jnp.zeros((x.shape[0], LANE), BF16)], axis=1)
    g_in = _cols_to_chips(_unpermute_w_in(_mm([(h1, dproj)], name="dw_in", ta=True, out_dtype=BF16), nh, lr))
    att_g = [g_in, g_uq, g_ukv, g_out]
    att16, att_own = _sum_pairs(att_g, _comm(_Swap(att_g), "swap_att"), place, "att")
    dh1, att_recv = _mm([(dproj, win_p)], name="dh1", tb=True, rider=_Exchange(att16))
    att_tot = _reduce_end(att_own, att_recv, "att")
    grad_x, gs["attn_norm_w"], att_shared = _norm_bwd(dh1, x, small["attn_norm_w"], dx2, "norm1_bwd", False,
                                                      rider=_Share(att_tot))
    return loss, grad_x, att_tot + ffn_tot, list(att_shared) + list(ffn_shared), gs


SMALL = ("attn_norm_w", "ffn_norm_w", "final_norm_w", "q_norm_w", "kv_norm_w", "gdn_norm_w",
         "mla_out_norm_w", "a_log", "dt_bias")
WEIGHTS = ("attn_norm_w", "w_in", "conv_w", "a_log", "dt_bias", "gdn_norm_w", "q_norm_w", "w_uq",
           "kv_norm_w", "w_ukv", "mla_out_norm_w", "w_out", "ffn_norm_w", "w_gate", "w_up", "w_down",
           "final_norm_w")


def _pack_small(vecs):
    flat = jnp.concatenate([v.astype(F32).reshape(-1) for v in vecs])
    pad = (-flat.shape[0]) % (8 * LANE)
    return jnp.concatenate([flat, jnp.zeros((pad,), F32)]).reshape(-1, LANE)


def kernel(x, positions, attn_norm_w, w_in, conv_w, a_log, dt_bias, gdn_norm_w, q_norm_w, w_uq, kv_norm_w, w_ukv, mla_out_norm_w, w_out, ffn_norm_w, w_gate, w_up, w_down, final_norm_w, loss_target, m_attn_norm_w, m_w_in, m_conv_w, m_a_log, m_dt_bias, m_gdn_norm_w, m_q_norm_w, m_w_uq, m_kv_norm_w, m_w_ukv, m_mla_out_norm_w, m_w_out, m_ffn_norm_w, m_w_gate, m_w_up, m_w_down, m_final_norm_w, v_attn_norm_w, v_w_in, v_conv_w, v_a_log, v_dt_bias, v_gdn_norm_w, v_q_norm_w, v_w_uq, v_kv_norm_w, v_w_ukv, v_mla_out_norm_w, v_w_out, v_ffn_norm_w, v_w_gate, v_w_up, v_w_down, v_final_norm_w):
    args = dict(locals())
    xi, yi, ci = lax.axis_index("x"), lax.axis_index("y"), lax.axis_index("c")
    chip = 2 * xi + yi

    def two_d(a):
        return a.reshape(a.shape[-2:]) if a.ndim >= 2 else a.reshape(1, -1)

    wloc = {n: two_d(args[n]) for n in WEIGHTS}
    mloc = {n: two_d(args["m_" + n]) for n in WEIGHTS}
    vloc = {n: two_d(args["v_" + n]) for n in WEIGHTS}

    w_in_full = _cols_from_chips(_comm(_Gather([wloc["w_in"].astype(BF16)]), "gather_w_in")[0])
    cw = wloc["conv_w"]
    cshard = cw.shape[1]
    cfull = jnp.zeros((CONV, 4 * cshard), F32)
    cfull = lax.dynamic_update_slice(cfull, jnp.where(ci == 0, cw, 0.0), (0, chip * cshard))
    conv_full = _small_allreduce(_pack_small([cfull]), "gather_conv_w").reshape(-1)[:CONV * 4 * cshard]
    conv_full = conv_full.reshape(CONV, 4 * cshard)

    small = {n: wloc[n] for n in SMALL}
    small["conv_w"] = conv_full

    pos = positions.reshape(-1, 1).astype(F32)
    place = jnp.stack([ci, chip]).astype(jnp.int32)
    loss, grad_x, totals, from_sib, gs = _step(
        two_d(x), pos, two_d(loss_target), w_in_full, [wloc[n].astype(BF16) for n in ATTN_W[1:]],
        [wloc[n].astype(BF16) for n in FFN_W], small, place)

    small_names = SMALL + ("conv_w",)
    pk = _pack_small([gs[n] for n in small_names] + [loss])
    red = _small_allreduce(pk, "reduce_small").reshape(-1)
    gsm, off = {}, 0
    for n in small_names:
        shp = gs[n].shape
        gsm[n] = red[off:off + shp[0] * shp[1]].reshape(shp)
        off += shp[0] * shp[1]
    loss_out = red[off]
    gsm["conv_w"] = lax.dynamic_slice(gsm["conv_w"], (0, chip * cshard), (CONV, cshard))

    grads, deltas, new_m, new_v = {}, {}, {}, {}
    for n, mine, theirs in zip(BIG, totals, from_sib):
        grads[n], deltas[n], new_m[n], new_v[n] = _adamw_halves(wloc[n], mine, theirs, mloc[n], vloc[n], place,
                                                                "adamw_" + n)
    grads["conv_w"] = gsm["conv_w"]
    deltas["conv_w"], new_m["conv_w"], new_v["conv_w"] = _adamw(wloc["conv_w"], gsm["conv_w"], mloc["conv_w"],
                                                                vloc["conv_w"], "adamw_conv_w")
    sm_shapes = [wloc[n].shape for n in SMALL]
    pd, pm, pv = _adamw(_pack_small([wloc[n] for n in SMALL]), _pack_small([gsm[n] for n in SMALL]),
                        _pack_small([mloc[n] for n in SMALL]), _pack_small([vloc[n] for n in SMALL]),
                        "adamw_small")
    for dst, packed in ((deltas, pd), (new_m, pm), (new_v, pv)):
        flat, off = packed.reshape(-1), 0
        for n, shp in zip(SMALL, sm_shapes):
            dst[n] = flat[off:off + shp[0] * shp[1]].reshape(shp)
            off += shp[0] * shp[1]
    for n in SMALL:
        grads[n] = gsm[n]

    def like(n, a):
        return a.reshape(args[n].shape)

    outs = [loss_out.reshape(()), grad_x.reshape(x.shape)]
    for group in (grads, deltas, new_m, new_v):
        outs += [like(n, group[n]) for n in WEIGHTS]
    return tuple(outs)
```

```python
import functools

import jax
import jax.numpy as jnp
from jax import lax
from jax.experimental import pallas as pl
from jax.experimental.pallas import tpu as pltpu

F32, BF16 = jnp.float32, jnp.bfloat16
SDS = jax.ShapeDtypeStruct
MESH = pl.DeviceIdType.MESH

HEAD = 128
ROPE = 64
CHUNK = 64
PAIR = 2 * CHUNK
CONV = 4
EPS = 1e-6
ROPE_THETA = 10000.0
LANE = 128
B_LANE = 64
A_LANE = 72
VMEM_LIMIT = 48 * 1024 * 1024
VMEM_LIMIT_WIDE = 56 * 1024 * 1024
MLA_BLOCK = 512
LOG2E = 1.4426950408889634
LN2 = 0.6931471805599453
SM_SCALE = (HEAD + ROPE) ** -0.5

ADAM_LR = 0.001
ADAM_B1 = 0.9
ADAM_B2 = 0.999
ADAM_EPS = 1e-08
ADAM_WD = 0.01
ADAM_STEP = 10


def _tile(n, pref, mult=LANE):
    if n <= pref:
        return n
    t = (pref // mult) * mult
    while t >= mult:
        if n % t == 0:
            return t
        t -= mult
    return n


def _pcall(body, *, name, grid, in_specs, out_specs, out_shape, scratch=(), vmem=VMEM_LIMIT):
    return pl.pallas_call(
        body, name=name, grid=grid, in_specs=in_specs, out_specs=out_specs,
        out_shape=out_shape, scratch_shapes=list(scratch),
        compiler_params=pltpu.CompilerParams(
            dimension_semantics=("arbitrary",) * len(grid), vmem_limit_bytes=vmem))


def _pcall_riding(core, rider, *, name, grid, in_specs, out_specs, out_shape, args, scratch=()):
    n_in, n_out, n_scr = len(in_specs), len(out_specs), len(scratch)
    r_in, r_out = len(rider.arrays), len(rider.out_shapes)

    def body(*refs):
        ins, refs = refs[:n_in], refs[n_in:]
        r_ins, refs = refs[:r_in], refs[r_in:]
        outs, refs = refs[:n_out], refs[n_out:]
        r_outs, refs = refs[:r_out], refs[r_out:]
        scr, sems = refs[:n_scr], refs[n_scr:]
        r_refs = (r_ins, r_outs, sems)
        _ride_begin(rider, r_refs, pl.program_id(0))
        core(*ins, *outs, *scr)
        _ride_end(rider, r_refs, pl.program_id(0), grid[0])

    res = _pcall(body, name=name, grid=grid, in_specs=list(in_specs) + [_ANY] * r_in,
                 out_specs=list(out_specs) + [_ANY] * r_out, out_shape=list(out_shape) + rider.out_shapes,
                 scratch=list(scratch) + rider.scratch)(*args, *rider.arrays)
    return res[:n_out], res[n_out:]


def _rows(ts, width, col=0):
    return pl.BlockSpec((ts, width), lambda i: (i, col))


def _full(shape):
    nd = len(shape)
    return pl.BlockSpec(shape, lambda i: (0,) * nd)


def _dot(a, b):
    return jnp.dot(a.astype(BF16), b.astype(BF16), preferred_element_type=F32)


def _dot_nt(a, b):
    return lax.dot_general(a.astype(BF16), b.astype(BF16), (((1,), (1,)), ((), ())),
                           preferred_element_type=F32)


def _dot_tn(a, b):
    return lax.dot_general(a.astype(BF16), b.astype(BF16), (((0,), (0,)), ((), ())),
                           preferred_element_type=F32)


def _sigmoid(x):
    return 1.0 / (1.0 + jnp.exp(-x))


def _silu(x):
    return x * _sigmoid(x)


def _dsilu(x):
    s = _sigmoid(x)
    return s * (1.0 + x * (1.0 - s))


def _lane_iota(shape):
    return lax.broadcasted_iota(jnp.int32, shape, len(shape) - 1)


def _col(block, idx):
    return jnp.sum(jnp.where(_lane_iota(block.shape) == idx, block, 0.0), axis=-1, keepdims=True)


def _mm(pairs, *, name, ta=False, tb=False, out_dtype=F32, res=None, tm=1024, tn=1024, tk=2048,
        b_chips=False, out_chips=False, rider=None):
    a0, b0 = pairs[0]
    if ta:
        kdim, m = a0.shape
    else:
        m, kdim = a0.shape
    if b_chips and tb:
        n, tk = b0.shape[1], b0.shape[2]
        assert kdim == 4 * tk
    elif b_chips:
        n, tn = 4 * b0.shape[2], b0.shape[2]
        assert kdim == b0.shape[1]
    else:
        n = b0.shape[0] if tb else b0.shape[1]
    if out_chips:
        tn = n // 4
    tm = _tile(m, tm)
    tn = tn if (out_chips or (b_chips and not tb)) else _tile(n, tn)
    tk = tk if (b_chips and tb) else _tile(kdim, tk)
    assert m % tm == 0 and n % tn == 0 and kdim % tk == 0
    nk, npair = kdim // tk, len(pairs)
    grid = (m // tm, n // tn, nk)
    dims = (((0 if ta else 1,), (1 if tb else 0,)), ((), ()))
    n_in = 2 * npair + (res is not None)
    r_in, r_out = (len(rider.arrays), len(rider.out_shapes)) if rider else (0, 0)

    def body(*refs):
        o_ref = refs[n_in + r_in]
        acc = refs[n_in + r_in + 1 + r_out]
        k = pl.program_id(2)
        if rider:
            r_refs = (refs[n_in:n_in + r_in], refs[n_in + r_in + 1:n_in + r_in + 1 + r_out],
                      refs[n_in + r_in + 2 + r_out:])
            step = (pl.program_id(0) * grid[1] + pl.program_id(1)) * nk + k
            _ride_begin(rider, r_refs, step)

        @pl.when(k == 0)
        def _():
            acc[...] = jnp.zeros_like(acc)

        tot = None
        for p in range(npair):
            d = lax.dot_general(refs[2 * p][...].astype(BF16), refs[2 * p + 1][...].astype(BF16),
                                dims, preferred_element_type=F32)
            tot = d if tot is None else tot + d
        acc[...] += tot

        @pl.when(k == nk - 1)
        def _():
            r = acc[...]
            if res is not None:
                r = r + refs[2 * npair][...]
            o_ref[...] = r.astype(out_dtype)

        if rider:
            _ride_end(rider, r_refs, step, grid[0] * grid[1] * nk)

    if ta:
        a_spec = pl.BlockSpec((tk, tm), lambda i, j, k: (k, i))
    else:
        a_spec = pl.BlockSpec((tm, tk), lambda i, j, k: (i, k))
    if b_chips and tb:
        b_spec = pl.BlockSpec((None, tn, tk), lambda i, j, k: (k, j, 0))
    elif b_chips:
        b_spec = pl.BlockSpec((None, tk, tn), lambda i, j, k: (j, k, 0))
    elif tb:
        b_spec = pl.BlockSpec((tn, tk), lambda i, j, k: (j, k))
    else:
        b_spec = pl.BlockSpec((tk, tn), lambda i, j, k: (k, j))
    if out_chips:
        o_spec = pl.BlockSpec((None, tm, tn), lambda i, j, k: (j, i, 0))
        o_shape = SDS((4, m, tn), out_dtype)
    else:
        o_spec = pl.BlockSpec((tm, tn), lambda i, j, k: (i, j))
        o_shape = SDS((m, n), out_dtype)
    in_specs, args = [], []
    for a, b in pairs:
        in_specs += [a_spec, b_spec]
        args += [a, b]
    if res is not None:
        in_specs.append(o_spec)
        args.append(res)
    out_specs, out_shapes, scratch = [o_spec], [o_shape], [pltpu.VMEM((tm, tn), F32)]
    if rider:
        in_specs += [_ANY] * r_in
        args += rider.arrays
        out_specs += [_ANY] * r_out
        out_shapes += rider.out_shapes
        scratch += rider.scratch
    outs = _pcall(body, name=name, grid=grid, in_specs=in_specs, out_specs=out_specs, out_shape=out_shapes,
                  scratch=scratch)(*args)
    return (outs[0], outs[1:]) if rider else outs[0]


def _norm_fwd(x, w, name, rider=None):
    s, d = x.shape
    ts = _tile(s, 512, 8)

    def body(x_ref, w_ref, h_ref):
        xv = x_ref[...]
        r = lax.rsqrt(jnp.mean(xv * xv, axis=-1, keepdims=True) + EPS)
        h_ref[...] = (xv * r * w_ref[...]).astype(BF16)

    spec = dict(name=name, grid=(s // ts,), in_specs=[_rows(ts, d), _full((1, d))])
    if rider is None:
        return _pcall(body, out_specs=_rows(ts, d), out_shape=SDS((s, d), BF16), **spec)(x, w)
    outs, r_outs = _pcall_riding(body, rider, out_specs=[_rows(ts, d)], out_shape=[SDS((s, d), BF16)],
                                 args=(x, w), **spec)
    return outs[0], r_outs


def _norm_bwd(dh, x, w, dres, name, with_bf16):
    s, d = x.shape
    ts = _tile(s, 256, 8)

    def body(dh_ref, x_ref, w_ref, dres_ref, dx_ref, dw_ref, *dx16_ref):
        @pl.when(pl.program_id(0) == 0)
        def _():
            dw_ref[...] = jnp.zeros_like(dw_ref)

        xv, dhv = x_ref[...], dh_ref[...]
        r = lax.rsqrt(jnp.mean(xv * xv, axis=-1, keepdims=True) + EPS)
        xh = xv * r
        dw_ref[...] += jnp.sum(dhv * xh, axis=0, keepdims=True)
        dxh = dhv * w_ref[...]
        dx = dres_ref[...] + r * (dxh - xh * jnp.mean(dxh * xh, axis=-1, keepdims=True))
        dx_ref[...] = dx
        for ref in dx16_ref:
            ref[...] = dx.astype(BF16)

    extra = 1 if with_bf16 else 0
    return _pcall(body, name=name, grid=(s // ts,),
                  in_specs=[_rows(ts, d), _rows(ts, d), _full((1, d)), _rows(ts, d)],
                  out_specs=[_rows(ts, d), _full((1, d))] + [_rows(ts, d)] * extra,
                  out_shape=[SDS((s, d), F32), SDS((1, d), F32)] + [SDS((s, d), BF16)] * extra)(
                      dh, x, w, dres)


def _final_loss(x3, tgt, w):
    s, d = x3.shape
    ts = _tile(s, 256, 8)

    def body(x_ref, t_ref, w_ref, dx_ref, dw_ref, loss_ref, dx16_ref):
        @pl.when(pl.program_id(0) == 0)
        def _():
            dw_ref[...] = jnp.zeros_like(dw_ref)
            loss_ref[...] = jnp.zeros_like(loss_ref)

        xv, wv = x_ref[...], w_ref[...]
        r = lax.rsqrt(jnp.mean(xv * xv, axis=-1, keepdims=True) + EPS)
        xh = xv * r
        err = xh * wv - t_ref[...]
        row = jnp.mean(err * err, axis=-1, keepdims=True)
        loss_ref[...] += 0.5 * jnp.sum(row, axis=0, keepdims=True)
        dy = err * (1.0 / d)
        dw_ref[...] += jnp.sum(dy * xh, axis=0, keepdims=True)
        dxh = dy * wv
        dx = r * (dxh - xh * jnp.mean(dxh * xh, axis=-1, keepdims=True))
        dx_ref[...] = dx
        dx16_ref[...] = dx.astype(BF16)

    return _pcall(body, name="final_loss", grid=(s // ts,),
                  in_specs=[_rows(ts, d), _rows(ts, d), _full((1, d))],
                  out_specs=[_rows(ts, d), _full((1, d)), _full((1, 1)), _rows(ts, d)],
                  out_shape=[SDS((s, d), F32), SDS((1, d), F32), SDS((1, 1), F32), SDS((s, d), BF16)])(
                      x3, tgt, w)


def _shift_down(cur, halo, s):
    if s == 0:
        return cur
    row8 = lax.broadcasted_iota(jnp.int32, halo.shape, 0)
    r = pltpu.roll(cur, s, 0)
    top = jnp.where(row8 < s, pltpu.roll(halo, s, 0), r[0:8])
    return jnp.concatenate([top, r[8:]], axis=0)


def _shift_up(cur, halo, s):
    if s == 0:
        return cur
    ts = cur.shape[0]
    row8 = lax.broadcasted_iota(jnp.int32, halo.shape, 0)
    r = pltpu.roll(cur, ts - s, 0)
    bot = jnp.where(row8 >= 8 - s, pltpu.roll(halo, 8 - s, 0), r[ts - 8:ts])
    return jnp.concatenate([r[:ts - 8], bot], axis=0)


def _chunk_tri(ts, upper):
    i = lax.broadcasted_iota(jnp.int32, (ts, ts), 0)
    j = lax.broadcasted_iota(jnp.int32, (ts, ts), 1)
    same = jnp.right_shift(i, 6) == jnp.right_shift(j, 6)
    return jnp.where(same & ((j >= i) if upper else (j <= i)), 1.0, 0.0).astype(F32)


def _gate_values(m, alog, dtb):
    lane = _lane_iota(m.shape)
    beta = _sigmoid(m)
    xg = m + dtb
    sp = jnp.maximum(xg, 0.0) + jnp.log(1.0 + jnp.exp(-jnp.abs(xg)))
    ga = (lane >= A_LANE) & (lane < A_LANE + 8)
    g = jnp.where(ga, -jnp.exp(alog) * sp, 0.0)
    return beta, g, xg, ga


def _l2_heads(a, nh, scale):
    outs, rs = [], []
    for h in range(nh):
        ah = a[:, HEAD * h:HEAD * (h + 1)]
        r = lax.rsqrt(jnp.sum(ah * ah, axis=-1, keepdims=True) + EPS)
        outs.append(ah * (r * scale))
        rs.append(r)
    return jnp.concatenate(outs, axis=-1), rs


def _gdn_prep(proj, conv_w, alog_l, dtb_l, nh, misc_col):
    s = proj.shape[0]
    w = nh * HEAD
    ts = _tile(s, 256, PAIR)
    hb = ts // 8

    def body(cur_ref, halo_ref, misc_ref, cw_ref, al_ref, db_ref, q_ref, k_ref, v_ref, gb_ref, gbt_ref):
        first = pl.program_id(0) == 0
        outs = (q_ref, k_ref, v_ref)
        for sec in range(3):
            cs = slice(sec * w, (sec + 1) * w)
            cur = cur_ref[:, cs]
            halo = jnp.where(first, 0.0, halo_ref[:, cs])
            pre = None
            for j in range(CONV):
                term = cw_ref[j:j + 1, cs] * _shift_down(cur, halo, CONV - 1 - j)
                pre = term if pre is None else pre + term
            act = _silu(pre)
            if sec == 0:
                act, _ = _l2_heads(act, nh, HEAD ** -0.5)
            elif sec == 1:
                act, _ = _l2_heads(act, nh, 1.0)
            outs[sec][...] = act
        m = misc_ref[...]
        lane = _lane_iota(m.shape)
        beta, g, _, ga = _gate_values(m, al_ref[...], db_ref[...])
        gcc = jnp.dot(_chunk_tri(ts, False), g, precision=lax.Precision.HIGHEST,
                      preferred_element_type=F32)
        gb = jnp.where((lane >= B_LANE) & (lane < B_LANE + 8), beta, jnp.where(ga, gcc, 0.0))
        gb_ref[...] = gb
        gbt_ref[...] = gb.T

    return _pcall(
        body, name="gdn_prep", grid=(s // ts,),
        in_specs=[_rows(ts, 3 * w),
                  pl.BlockSpec((8, 3 * w), lambda i: (jnp.maximum(i * hb - 1, 0), 0)),
                  _rows(ts, LANE, misc_col), _full((CONV, 3 * w)), _full((1, LANE)), _full((1, LANE))],
        out_specs=[_rows(ts, w), _rows(ts, w), _rows(ts, w), _rows(ts, LANE),
                   pl.BlockSpec((LANE, ts), lambda i: (0, i))],
        out_shape=[SDS((s, w), F32), SDS((s, w), F32), SDS((s, w), F32), SDS((s, LANE), F32),
                   SDS((LANE, s), F32)])(proj, proj, proj, conv_w, alog_l, dtb_l)


def _gdn_prep_bwd(proj, conv_w, alog_l, dtb_l, dq, dk, dv, dgb, dkr, nh, misc_col):
    s = proj.shape[0]
    w = nh * HEAD
    ts = _tile(s, 256, PAIR)
    hb = ts // 8

    def body(cur_ref, halo_ref, misc_ref, cw_ref, al_ref, db_ref, dq_ref, dk_ref, dv_ref, dgb_ref,
             dkr_ref, dc_ref, dm_ref, dcw_ref, dal_ref, ddb_ref):
        first = pl.program_id(0) == 0

        @pl.when(first)
        def _():
            dcw_ref[...] = jnp.zeros_like(dcw_ref)
            dal_ref[...] = jnp.zeros_like(dal_ref)
            ddb_ref[...] = jnp.zeros_like(ddb_ref)

        dins = (dq_ref, dk_ref, dv_ref)
        for sec in range(3):
            cs = slice(sec * w, (sec + 1) * w)
            cur = cur_ref[:, cs]
            halo = jnp.where(first, 0.0, halo_ref[:, cs])
            us = [_shift_down(cur, halo, CONV - 1 - j) for j in range(CONV)]
            pre = None
            for j in range(CONV):
                term = cw_ref[j:j + 1, cs] * us[j]
                pre = term if pre is None else pre + term
            act = _silu(pre)
            dout = dins[sec][...]
            if sec < 2:
                scale = HEAD ** -0.5 if sec == 0 else 1.0
                parts = []
                for h in range(nh):
                    hs = slice(HEAD * h, HEAD * (h + 1))
                    ah = act[:, hs]
                    r = lax.rsqrt(jnp.sum(ah * ah, axis=-1, keepdims=True) + EPS)
                    ahat = ah * r
                    dy = dout[:, hs]
                    parts.append((scale * r) * (dy - ahat * jnp.sum(dy * ahat, axis=-1, keepdims=True)))
                dact = jnp.concatenate(parts, axis=-1)
            else:
                dact = dout
            dconv = dact * _dsilu(pre)
            dc_ref[:, cs] = dconv
            for j in range(CONV):
                dcw_ref[j:j + 1, cs] += jnp.sum(dconv * us[j], axis=0, keepdims=True)
        m = misc_ref[...]
        lane = _lane_iota(m.shape)
        al = al_ref[...]
        beta, g, xg, ga = _gate_values(m, al, db_ref[...])
        dgbv = dgb_ref[...]
        dg = jnp.dot(_chunk_tri(ts, True), jnp.where(ga, dgbv, 0.0), precision=lax.Precision.HIGHEST,
                     preferred_element_type=F32)
        da_raw = jnp.where(ga, dg * (-jnp.exp(al)) * _sigmoid(xg), 0.0)
        db_raw = jnp.where((lane >= B_LANE) & (lane < B_LANE + 8), dgbv * beta * (1.0 - beta), 0.0)
        dal_ref[...] += jnp.sum(dg * g, axis=0, keepdims=True)
        ddb_ref[...] += jnp.sum(da_raw, axis=0, keepdims=True)
        dm_ref[...] = (dkr_ref[...] + da_raw + db_raw).astype(BF16)

    return _pcall(
        body, name="gdn_prep_bwd", grid=(s // ts,),
        in_specs=[_rows(ts, 3 * w),
                  pl.BlockSpec((8, 3 * w), lambda i: (jnp.maximum(i * hb - 1, 0), 0)),
                  _rows(ts, LANE, misc_col), _full((CONV, 3 * w)), _full((1, LANE)), _full((1, LANE)),
                  _rows(ts, w), _rows(ts, w), _rows(ts, w), _rows(ts, LANE), _rows(ts, LANE)],
        out_specs=[_rows(ts, 3 * w), _rows(ts, LANE), _full((CONV, 3 * w)), _full((1, LANE)),
                   _full((1, LANE))],
        out_shape=[SDS((s, 3 * w), F32), SDS((s, LANE), BF16), SDS((CONV, 3 * w), F32),
                   SDS((1, LANE), F32), SDS((1, LANE), F32)])(
                       proj, proj, proj, conv_w, alog_l, dtb_l, dq, dk, dv, dgb, dkr)


def _conv_bwd_input(dconv, conv_w):
    s, c = dconv.shape
    ts = _tile(s, 256, 8)
    hb = ts // 8
    nblk8 = s // 8
    nt = s // ts

    def body(cur_ref, nxt_ref, cw_ref, o_ref):
        last = pl.program_id(0) == nt - 1
        cur = cur_ref[...]
        halo = jnp.where(last, 0.0, nxt_ref[...])
        acc = None
        for j in range(CONV):
            term = cw_ref[j:j + 1, :] * _shift_up(cur, halo, CONV - 1 - j)
            acc = term if acc is None else acc + term
        o_ref[...] = acc.astype(BF16)

    return _pcall(
        body, name="conv_bwd_input", grid=(nt,),
        in_specs=[_rows(ts, c),
                  pl.BlockSpec((8, c), lambda i: (jnp.minimum((i + 1) * hb, nblk8 - 1), 0)),
                  _full((CONV, c))],
        out_specs=_rows(ts, c), out_shape=SDS((s, c), BF16))(dconv, dconv, conv_w)


def _inv_unit_lower(a):
    n = a[0].shape[0]
    i = lax.broadcasted_iota(jnp.int32, (n, n), 0)
    j = lax.broadcasted_iota(jnp.int32, (n, n), 1)
    eye = jnp.where(i == j, 1.0, 0.0)
    t = [eye - ah for ah in a]
    x = a
    for _ in range(5):
        x = [_dot(xh, xh) for xh in x]
        t = [th + _dot(th, xh) for th, xh in zip(t, x)]
    return t


def _pair_common(q, k, gcol, grow, bcol):
    i = lax.broadcasted_iota(jnp.int32, (PAIR, PAIR), 0)
    j = lax.broadcasted_iota(jnp.int32, (PAIR, PAIR), 1)
    same = jnp.right_shift(i, 6) == jnp.right_shift(j, 6)
    tril = same & (i >= j)
    strict = same & (i > j)
    dec = [jnp.where(tril, jnp.exp(jnp.minimum(gc - gr, 0.0)), 0.0) for gc, gr in zip(gcol, grow)]
    kk = [_dot_nt(kh, kh) for kh in k]
    qk = [_dot_nt(qh, kh) for qh, kh in zip(q, k)]
    a = [jnp.where(strict, b * kkh * d, 0.0) for b, kkh, d in zip(bcol, kk, dec)]
    t = _inv_unit_lower(a)
    p = [qkh * d for qkh, d in zip(qk, dec)]
    return dec, kk, a, t, p, tril, strict


def _ext(v, a):
    z = jnp.zeros_like(v)
    return jnp.concatenate([v, z] if a == 0 else [z, v], axis=0)


def _gdn_fwd(q, k, v, gb, gbt, nh):
    s = q.shape[0]
    w = nh * HEAD
    npair = s // PAIR

    def body(q_ref, k_ref, v_ref, gb_ref, gbt_ref, o_ref, st_ref, s_ref):
        @pl.when(pl.program_id(0) == 0)
        def _():
            s_ref[...] = jnp.zeros_like(s_ref)

        heads = range(nh)
        hs = [slice(HEAD * h, HEAD * (h + 1)) for h in heads]
        gbv = gb_ref[...]
        q, k, v = [q_ref[:, s_] for s_ in hs], [k_ref[:, s_] for s_ in hs], [v_ref[:, s_] for s_ in hs]
        gcol = [_col(gbv, A_LANE + h) for h in heads]
        bcol = [_col(gbv, B_LANE + h) for h in heads]
        grow = [gbt_ref[A_LANE + h:A_LANE + h + 1, :] for h in heads]
        _, _, _, t, p, _, _ = _pair_common(q, k, gcol, grow, bcol)
        eg = [jnp.exp(gc) for gc in gcol]
        qg = [x * e for x, e in zip(q, eg)]
        kg = [x * e for x, e in zip(k, eg)]
        outs = []
        for a in range(2):
            sl = slice(CHUNK * a, CHUNK * (a + 1))
            st = [s_ref[h] for h in heads]
            for h in heads:
                st_ref[a, h] = st[h]
            r = [v[h][sl] - _dot(kg[h][sl], st[h]) for h in heads]
            vn = [_dot(t[h][sl], _ext(bcol[h][sl] * r[h], a)) for h in heads]
            outs.append([_dot(qg[h][sl], st[h]) + _dot(p[h][sl], _ext(vn[h], a)) for h in heads])
            gl = [_col(gr, CHUNK * (a + 1) - 1) for gr in grow]
            kd = [k[h][sl] * jnp.exp(gl[h] - gcol[h][sl]) for h in heads]
            upd = [_dot_tn(kd[h], vn[h]) for h in heads]
            for h in heads:
                s_ref[h] = jnp.exp(gl[h]) * st[h] + upd[h]
        for h in heads:
            o_ref[:, hs[h]] = jnp.concatenate([outs[0][h], outs[1][h]], axis=0)

    return _pcall(
        body, name="gdn_fwd", grid=(npair,),
        in_specs=[_rows(PAIR, w), _rows(PAIR, w), _rows(PAIR, w), _rows(PAIR, LANE),
                  pl.BlockSpec((LANE, PAIR), lambda i: (0, i))],
        out_specs=[_rows(PAIR, w), pl.BlockSpec((2, nh, HEAD, HEAD), lambda i: (i, 0, 0, 0))],
        out_shape=[SDS((s, w), F32), SDS((2 * npair, nh, HEAD, HEAD), F32)],
        scratch=[pltpu.VMEM((nh, HEAD, HEAD), F32)])(q, k, v, gb, gbt)


def _gdn_bwd(q, k, v, gb, gbt, states, do, nh, rider):
    s = q.shape[0]
    w = nh * HEAD
    npair = s // PAIR
    rev = lambda i: (npair - 1 - i, 0)

    def body(q_ref, k_ref, v_ref, gb_ref, gbt_ref, st_ref, do_ref, dq_ref, dk_ref, dv_ref, dgb_ref,
             ds_ref):
        @pl.when(pl.program_id(0) == 0)
        def _():
            ds_ref[...] = jnp.zeros_like(ds_ref)

        lane = _lane_iota((PAIR, LANE))
        row = lax.broadcasted_iota(jnp.int32, (CHUNK, 1), 0)
        heads = range(nh)
        hs = [slice(HEAD * h, HEAD * (h + 1)) for h in heads]
        gbv = gb_ref[...]
        q, k, v = [q_ref[:, s_] for s_ in hs], [k_ref[:, s_] for s_ in hs], [v_ref[:, s_] for s_ in hs]
        do = [do_ref[:, s_] for s_ in hs]
        gcol = [_col(gbv, A_LANE + h) for h in heads]
        bcol = [_col(gbv, B_LANE + h) for h in heads]
        grow = [gbt_ref[A_LANE + h:A_LANE + h + 1, :] for h in heads]
        dec, kk, amat, t, p, tril, strict = _pair_common(q, k, gcol, grow, bcol)
        tt, pt = [x.T for x in t], [x.T for x in p]
        eg = [jnp.exp(gc) for gc in gcol]
        qg = [x * e for x, e in zip(q, eg)]
        kg = [x * e for x, e in zip(k, eg)]
        sums = lambda x: jnp.sum(x, axis=-1, keepdims=True)
        rs, vns = [None, None], [None, None]
        for a in range(2):
            sl = slice(CHUNK * a, CHUNK * (a + 1))
            rs[a] = [v[h][sl] - _dot(kg[h][sl], st_ref[a, h]) for h in heads]
            vns[a] = [_dot(t[h][sl], _ext(bcol[h][sl] * rs[a][h], a)) for h in heads]
        dsn = [ds_ref[h] for h in heads]
        dqs, dks, dvs, dgcs, dbs, drbs = ([None, None] for _ in range(6))
        for a in (1, 0):
            sl = slice(CHUNK * a, CHUNK * (a + 1))
            st = [st_ref[a, h] for h in heads]
            gl = [_col(gr, CHUNK * (a + 1) - 1) for gr in grow]
            egl = [jnp.exp(x) for x in gl]
            dk_dec = [jnp.exp(gl[h] - gcol[h][sl]) for h in heads]
            kd = [k[h][sl] * dk_dec[h] for h in heads]
            d_vn = [_dot(pt[h][sl], _ext(do[h][sl], a)) + _dot(kd[h], dsn[h]) for h in heads]
            d_qg = [_dot_nt(do[h][sl], st[h]) for h in heads]
            d_rb = [_dot(tt[h][sl], _ext(d_vn[h], a)) for h in heads]
            d_r = [bcol[h][sl] * d_rb[h] for h in heads]
            d_kg = [-_dot_nt(d_r[h], st[h]) for h in heads]
            d_kd = [_dot_nt(vns[a][h], dsn[h]) for h in heads]
            dsn_new = [_dot_tn(qg[h][sl], do[h][sl]) - _dot_tn(kg[h][sl], d_r[h]) for h in heads]
            dbs[a] = [sums(d_rb[h] * rs[a][h]) for h in heads]
            dgl = [egl[h] * jnp.sum(dsn[h] * st[h], keepdims=True) + jnp.sum(d_kd[h] * kd[h], keepdims=True)
                   for h in heads]
            dgcs[a] = [sums(d_qg[h] * qg[h][sl]) + sums(d_kg[h] * kg[h][sl]) - sums(d_kd[h] * kd[h])
                       + jnp.where(row == CHUNK - 1, dgl[h], 0.0) for h in heads]
            dqs[a] = [d_qg[h] * eg[h][sl] for h in heads]
            dks[a] = [d_kg[h] * eg[h][sl] + d_kd[h] * dk_dec[h] for h in heads]
            dvs[a] = d_r
            drbs[a] = d_rb
            dsn = [dsn_new[h] + egl[h] * dsn[h] for h in heads]
        for h in heads:
            ds_ref[h] = dsn[h]
        cat = lambda xs, h: jnp.concatenate([xs[0][h], xs[1][h]], axis=0)
        vn = [cat(vns, h) for h in heads]
        d_rb = [cat(drbs, h) for h in heads]
        dp = [jnp.where(tril, _dot_nt(do[h], vn[h]), 0.0) for h in heads]
        dam = [jnp.where(strict, -_dot_nt(d_rb[h], vn[h]), 0.0) for h in heads]
        g_p = [dp[h] * dec[h] for h in heads]
        g_a = [dam[h] * dec[h] for h in heads]
        gbk = [bcol[h] * g_a[h] for h in heads]
        dq2 = [_dot(g_p[h], k[h]) for h in heads]
        dk2 = [_dot_tn(g_p[h], q[h]) + _dot(gbk[h], k[h]) + _dot_tn(gbk[h], k[h]) for h in heads]
        dgb = jnp.zeros((PAIR, LANE), F32)
        for h in heads:
            dq_ref[:, hs[h]] = cat(dqs, h) + dq2[h]
            dk_ref[:, hs[h]] = cat(dks, h) + dk2[h]
            dv_ref[:, hs[h]] = cat(dvs, h)
            dbeta = cat(dbs, h) + sums(g_a[h] * kk[h])
            mm = dp[h] * p[h] + dam[h] * amat[h]
            dgc = cat(dgcs, h) + sums(mm) - sums(mm.T)
            dgb = dgb + jnp.where(lane == A_LANE + h, dgc, 0.0) + jnp.where(lane == B_LANE + h, dbeta, 0.0)
        dgb_ref[...] = dgb

    return _pcall_riding(
        body, rider, name="gdn_bwd", grid=(npair,),
        in_specs=[pl.BlockSpec((PAIR, w), rev), pl.BlockSpec((PAIR, w), rev), pl.BlockSpec((PAIR, w), rev),
                  pl.BlockSpec((PAIR, LANE), rev),
                  pl.BlockSpec((LANE, PAIR), lambda i: (0, npair - 1 - i)),
                  pl.BlockSpec((2, nh, HEAD, HEAD), lambda i: (npair - 1 - i, 0, 0, 0)),
                  pl.BlockSpec((PAIR, w), rev)],
        out_specs=[pl.BlockSpec((PAIR, w), rev), pl.BlockSpec((PAIR, w), rev), pl.BlockSpec((PAIR, w), rev),
                   pl.BlockSpec((PAIR, LANE), rev)],
        out_shape=[SDS((s, w), F32), SDS((s, w), F32), SDS((s, w), F32), SDS((s, LANE), F32)],
        scratch=[pltpu.VMEM((nh, HEAD, HEAD), F32)], args=(q, k, v, gb, gbt, states, do))


def _mla_norm(proj, qw, kvw, col_q, col_kv):
    s = proj.shape[0]
    lr = qw.shape[1]
    ts = _tile(s, 512, 8)

    def body(cq_ref, ckv_ref, qw_ref, kvw_ref, oq_ref, okv_ref):
        for x_ref, w_ref, o_ref in ((cq_ref, qw_ref, oq_ref), (ckv_ref, kvw_ref, okv_ref)):
            xv = x_ref[...]
            r = lax.rsqrt(jnp.mean(xv * xv, axis=-1, keepdims=True) + EPS)
            o_ref[...] = (xv * r * w_ref[...]).astype(BF16)

    return _pcall(body, name="mla_norm", grid=(s // ts,),
                  in_specs=[_rows(ts, lr, col_q), _rows(ts, lr, col_kv), _full((1, lr)), _full((1, lr))],
                  out_specs=[_rows(ts, lr), _rows(ts, lr)],
                  out_shape=[SDS((s, lr), BF16), SDS((s, lr), BF16)])(proj, proj, qw, kvw)


def _mla_norm_bwd(proj, qw, kvw, dq, dkv, col_q, col_kv):
    s = proj.shape[0]
    lr = qw.shape[1]
    ts = _tile(s, 512, 8)

    def body(cq_ref, ckv_ref, qw_ref, kvw_ref, dq_ref, dkv_ref, oq_ref, okv_ref, dqw_ref, dkvw_ref):
        @pl.when(pl.program_id(0) == 0)
        def _():
            dqw_ref[...] = jnp.zeros_like(dqw_ref)
            dkvw_ref[...] = jnp.zeros_like(dkvw_ref)

        for x_ref, w_ref, d_ref, o_ref, dw_ref in ((cq_ref, qw_ref, dq_ref, oq_ref, dqw_ref),
                                                    (ckv_ref, kvw_ref, dkv_ref, okv_ref, dkvw_ref)):
            xv, dh = x_ref[...], d_ref[...]
            r = lax.rsqrt(jnp.mean(xv * xv, axis=-1, keepdims=True) + EPS)
            xh = xv * r
            dw_ref[...] += jnp.sum(dh * xh, axis=0, keepdims=True)
            dxh = dh * w_ref[...]
            o_ref[...] = (r * (dxh - xh * jnp.mean(dxh * xh, axis=-1, keepdims=True))).astype(BF16)

    return _pcall(body, name="mla_norm_bwd", grid=(s // ts,),
                  in_specs=[_rows(ts, lr, col_q), _rows(ts, lr, col_kv), _full((1, lr)), _full((1, lr)),
                            _rows(ts, lr), _rows(ts, lr)],
                  out_specs=[_rows(ts, lr), _rows(ts, lr), _full((1, lr)), _full((1, lr))],
                  out_shape=[SDS((s, lr), BF16), SDS((s, lr), BF16), SDS((1, lr), F32),
                             SDS((1, lr), F32)])(proj, proj, qw, kvw, dq, dkv)


def _rope_tables(pos, invf, sgn):
    ang = pos * invf
    return jnp.cos(ang), jnp.sin(ang) * sgn


def _swap_halves_lanes(y):
    lane = _lane_iota(y.shape)
    return jnp.where(lane < ROPE // 2, pltpu.roll(y, LANE - ROPE // 2, 1), pltpu.roll(y, ROPE // 2, 1))


def _rope_consts():
    half = ROPE // 2
    inv = ROPE_THETA ** (-jnp.arange(half, dtype=F32) / half)
    invf = jnp.concatenate([inv, inv, jnp.zeros((LANE - ROPE,), F32)])[None, :]
    sgn = jnp.concatenate([-jnp.ones((half,), F32), jnp.ones((half,), F32),
                           jnp.zeros((LANE - ROPE,), F32)])[None, :]
    return invf, sgn


def _mla_rope(qraw, kvraw, proj, pos, nh, misc_col):
    s = qraw.shape[0]
    ts = _tile(s, 256, 8)
    wq = nh * 2 * HEAD
    invf, sgn = _rope_consts()

    def body(q_ref, kv_ref, misc_ref, pos_ref, if_ref, sg_ref, qc_ref, kc_ref, v_ref):
        c, sn = _rope_tables(pos_ref[...], if_ref[...], sg_ref[...])
        lane = _lane_iota(c.shape)
        rot = lambda xb: xb * c + _swap_halves_lanes(xb) * sn
        qs = SM_SCALE * LOG2E
        krot = jnp.where(lane < ROPE, rot(misc_ref[...]), 0.0).astype(BF16)
        for h in range(nh):
            b0 = 2 * HEAD * h
            qc_ref[:, b0:b0 + HEAD] = (q_ref[:, b0:b0 + HEAD].astype(F32) * qs).astype(BF16)
            qc_ref[:, b0 + HEAD:b0 + 2 * HEAD] = (
                rot(q_ref[:, b0 + HEAD:b0 + 2 * HEAD].astype(F32)) * qs).astype(BF16)
            kc_ref[:, b0:b0 + HEAD] = kv_ref[:, b0:b0 + HEAD].astype(BF16)
            kc_ref[:, b0 + HEAD:b0 + 2 * HEAD] = krot
        v_ref[...] = kv_ref[:, wq:].astype(BF16)

    return _pcall(body, name="mla_rope", grid=(s // ts,),
                  in_specs=[_rows(ts, wq), _rows(ts, wq + nh * HEAD), _rows(ts, LANE, misc_col),
                            _rows(ts, 1), _full((1, LANE)), _full((1, LANE))],
                  out_specs=[_rows(ts, wq), _rows(ts, wq), _rows(ts, nh * HEAD)],
                  out_shape=[SDS((s, wq), BF16), SDS((s, wq), BF16), SDS((s, nh * HEAD), BF16)])(
                      qraw, kvraw, proj, pos, invf, sgn)


def _mla_rope_bwd(dqc, dkc, dv, pos, nh):
    s = dqc.shape[0]
    ts = _tile(s, 256, 8)
    wq = nh * 2 * HEAD
    invf, sgn = _rope_consts()

    def body(dq_ref, dk_ref, dv_ref, pos_ref, if_ref, sg_ref, oq_ref, okv_ref, okr_ref):
        c, sn = _rope_tables(pos_ref[...], if_ref[...], sg_ref[...])
        lane = _lane_iota(c.shape)
        unrot = lambda d: d * c + _swap_halves_lanes(d * sn)
        dkr = jnp.zeros(c.shape, F32)
        for h in range(nh):
            b0 = 2 * HEAD * h
            oq_ref[:, b0:b0 + HEAD] = (dq_ref[:, b0:b0 + HEAD] * SM_SCALE).astype(BF16)
            oq_ref[:, b0 + HEAD:b0 + 2 * HEAD] = (
                unrot(dq_ref[:, b0 + HEAD:b0 + 2 * HEAD]) * SM_SCALE).astype(BF16)
            okv_ref[:, b0:b0 + HEAD] = (dk_ref[:, b0:b0 + HEAD] * LN2).astype(BF16)
            okv_ref[:, b0 + HEAD:b0 + 2 * HEAD] = jnp.zeros((ts, HEAD), BF16)
            dkr = dkr + dk_ref[:, b0 + HEAD:b0 + 2 * HEAD]
        okv_ref[:, wq:] = dv_ref[...].astype(BF16)
        okr_ref[...] = jnp.where(lane < ROPE, unrot(jnp.where(lane < ROPE, dkr * LN2, 0.0)), 0.0)

    return _pcall(body, name="mla_rope_bwd", grid=(s // ts,),
                  in_specs=[_rows(ts, wq), _rows(ts, wq), _rows(ts, nh * HEAD), _rows(ts, 1),
                            _full((1, LANE)), _full((1, LANE))],
                  out_specs=[_rows(ts, wq), _rows(ts, wq + nh * HEAD), _rows(ts, LANE)],
                  out_shape=[SDS((s, wq), BF16), SDS((s, wq + nh * HEAD), BF16), SDS((s, LANE), F32)])(
                      dqc, dkc, dv, pos, invf, sgn)


def _causal_mask(blk):
    i = lax.broadcasted_iota(jnp.int32, (blk, blk), 0)
    j = lax.broadcasted_iota(jnp.int32, (blk, blk), 1)
    return j <= i


MLA_HP = 2


def _pair_pack(a, b):
    return jnp.where(_lane_iota(a.shape) < LANE // 2, a, b)


def _pair_unpack(x, e):
    lane = _lane_iota(x.shape)
    keep = (lane < LANE // 2) if e == 0 else (lane >= LANE // 2)
    return jnp.where(keep, x, pltpu.roll(x, LANE // 2, 1))


def _mla_fwd(qc, kc, v, nh, rider):
    s = qc.shape[0]
    blk = _tile(s, MLA_BLOCK)
    nb = s // blk
    rep = blk // LANE
    hp = MLA_HP
    assert nh % hp == 0
    once = pl.Buffered(1)
    r_in, r_out = len(rider.arrays), len(rider.out_shapes)

    def body(*refs):
        q_ref, k_ref, v_ref = refs[:3]
        o_ref, lse_ref = refs[3 + r_in:5 + r_in]
        m_sc, l_sc, acc = refs[5 + r_in + r_out:8 + r_in + r_out]
        r_refs = (refs[3:3 + r_in], refs[5 + r_in:5 + r_in + r_out], refs[8 + r_in + r_out:])
        i = pl.program_id(1)
        grid_step = pl.program_id(0) * nb + i
        _ride_begin(rider, r_refs, grid_step)
        m_sc[...] = jnp.full_like(m_sc, -1e30)
        l_sc[...] = jnp.zeros_like(l_sc)
        acc[...] = jnp.zeros_like(acc)

        def step(j, masked):
            rows = pl.ds(pl.multiple_of(j * blk, blk), blk)
            es = range(hp)
            sc = [_dot_nt(q_ref[:, 2 * HEAD * e:2 * HEAD * (e + 1)], k_ref[rows, 2 * HEAD * e:2 * HEAD * (e + 1)])
                  for e in es]
            if masked:
                sc = [jnp.where(_causal_mask(blk), x, -1e30) for x in sc]
            m_prev = [m_sc[e] for e in es]
            m_new = [jnp.maximum(m_prev[e], jnp.max(sc[e], axis=-1, keepdims=True)) for e in es]
            p = [jnp.exp2(sc[e] - jnp.tile(m_new[e], (1, rep))) for e in es]
            alpha = [jnp.exp2(m_prev[e] - m_new[e]) for e in es]
            pv = [_dot(p[e], v_ref[rows, HEAD * e:HEAD * (e + 1)]) for e in es]
            for e in es:
                l_sc[e] = alpha[e] * l_sc[e] + jnp.sum(p[e], axis=-1, keepdims=True)
                acc[e] = alpha[e] * acc[e] + pv[e]
                m_sc[e] = m_new[e]

        def loop_body(j, carry):
            step(j, False)
            return carry

        lax.fori_loop(0, i, loop_body, 0)
        step(i, True)
        for e in range(hp):
            o_ref[:, HEAD * e:HEAD * (e + 1)] = acc[e] / l_sc[e]
        lse = [m_sc[e] + jnp.log(l_sc[e]) * LOG2E for e in range(hp)]
        lse_ref[...] = _pair_pack(lse[0], lse[1])
        _ride_end(rider, r_refs, grid_step, (nh // hp) * nb)

    outs = _pcall(
        body, name="mla_fwd", grid=(nh // hp, nb),
        in_specs=[pl.BlockSpec((blk, hp * 2 * HEAD), lambda g, i: (i, g)),
                  pl.BlockSpec((s, hp * 2 * HEAD), lambda g, i: (0, g), pipeline_mode=once),
                  pl.BlockSpec((s, hp * HEAD), lambda g, i: (0, g), pipeline_mode=once)] + [_ANY] * r_in,
        out_specs=[pl.BlockSpec((blk, hp * HEAD), lambda g, i: (i, g)),
                   pl.BlockSpec((None, blk, LANE), lambda g, i: (g, i, 0))] + [_ANY] * r_out,
        out_shape=[SDS((s, nh * HEAD), F32), SDS((nh // hp, s, LANE), F32)] + rider.out_shapes,
        scratch=[pltpu.VMEM((hp, blk, LANE), F32), pltpu.VMEM((hp, blk, LANE), F32),
                 pltpu.VMEM((hp, blk, HEAD), F32)] + rider.scratch)(qc, kc, v, *rider.arrays)
    return outs[0], outs[1], outs[2:]


def _mla_bwd(qc, kc, v, do, lse, delta, nh, rider):
    s = qc.shape[0]
    blk = _tile(s, MLA_BLOCK)
    nb = s // blk
    rep = blk // LANE
    hp = MLA_HP
    once = pl.Buffered(1)
    r_in, r_out = len(rider.arrays), len(rider.out_shapes)
    qs = [slice(2 * HEAD * e, 2 * HEAD * (e + 1)) for e in range(hp)]
    vs = [slice(HEAD * e, HEAD * (e + 1)) for e in range(hp)]

    def body(*refs):
        q_ref, do_ref, lse_ref, dl_ref, k_ref, v_ref = refs[:6]
        dq_ref, dk_ref, dv_ref = refs[6 + r_in:9 + r_in]
        dk_acc, dv_acc = refs[9 + r_in + r_out:11 + r_in + r_out]
        r_refs = (refs[6:6 + r_in], refs[9 + r_in:9 + r_in + r_out], refs[11 + r_in + r_out:])
        j = pl.program_id(1)
        grid_step = pl.program_id(0) * nb + j
        _ride_begin(rider, r_refs, grid_step)

        @pl.when(j == 0)
        def _():
            dq_ref[...] = jnp.zeros_like(dq_ref)

        dk_acc[...] = jnp.zeros_like(dk_acc)
        dv_acc[...] = jnp.zeros_like(dv_acc)
        es = range(hp)
        kj = [k_ref[:, qs[e]] for e in es]
        vj = [v_ref[:, vs[e]] for e in es]

        def step(i, masked):
            rows = pl.ds(pl.multiple_of(i * blk, blk), blk)
            qi = [q_ref[rows, qs[e]] for e in es]
            doi = [do_ref[rows, vs[e]] for e in es]
            lse, dl = lse_ref[rows, :], dl_ref[rows, :]
            sc = [_dot_nt(qi[e], kj[e]) for e in es]
            dp = [_dot_nt(doi[e], vj[e]) for e in es]
            if masked:
                sc = [jnp.where(_causal_mask(blk), x, -1e30) for x in sc]
            p = [jnp.exp2(sc[e] - jnp.tile(_pair_unpack(lse, e), (1, rep))) for e in es]
            ds = [p[e] * (dp[e] - jnp.tile(_pair_unpack(dl, e), (1, rep))) for e in es]
            dv = [_dot_tn(p[e], doi[e]) for e in es]
            dk = [_dot_tn(ds[e], qi[e]) for e in es]
            dq = [_dot(ds[e], kj[e]) for e in es]
            for e in es:
                dv_acc[:, vs[e]] += dv[e]
                dk_acc[:, qs[e]] += dk[e]
                dq_ref[rows, qs[e]] += dq[e]

        def loop_body(i, carry):
            step(i, False)
            return carry

        step(j, True)
        lax.fori_loop(j + 1, nb, loop_body, 0)
        dk_ref[...] = dk_acc[...]
        dv_ref[...] = dv_acc[...]
        _ride_end(rider, r_refs, grid_step, (nh // hp) * nb)

    outs = _pcall(
        body, name="mla_bwd", grid=(nh // hp, nb),
        in_specs=[pl.BlockSpec((s, hp * 2 * HEAD), lambda g, j: (0, g), pipeline_mode=once),
                  pl.BlockSpec((s, hp * HEAD), lambda g, j: (0, g), pipeline_mode=once),
                  pl.BlockSpec((None, s, LANE), lambda g, j: (g, 0, 0), pipeline_mode=once),
                  pl.BlockSpec((None, s, LANE), lambda g, j: (g, 0, 0), pipeline_mode=once),
                  pl.BlockSpec((blk, hp * 2 * HEAD), lambda g, j: (j, g)),
                  pl.BlockSpec((blk, hp * HEAD), lambda g, j: (j, g))] + [_ANY] * r_in,
        out_specs=[pl.BlockSpec((s, hp * 2 * HEAD), lambda g, j: (0, g), pipeline_mode=once),
                   pl.BlockSpec((blk, hp * 2 * HEAD), lambda g, j: (j, g)),
                   pl.BlockSpec((blk, hp * HEAD), lambda g, j: (j, g))] + [_ANY] * r_out,
        out_shape=[SDS((s, nh * 2 * HEAD), F32), SDS((s, nh * 2 * HEAD), F32),
                   SDS((s, nh * HEAD), F32)] + rider.out_shapes,
        scratch=[pltpu.VMEM((blk, hp * 2 * HEAD), F32), pltpu.VMEM((blk, hp * HEAD), F32)] + rider.scratch,
        vmem=VMEM_LIMIT_WIDE)(qc, do, lse, delta, kc, v, *rider.arrays)
    return outs[0], outs[1], outs[2], outs[3:]


def _mix_fwd(og, proj, om, gw, mw, nh, z_col):
    s = og.shape[0]
    w = nh * HEAD
    ts = _tile(s, 256, 8)

    def body(og_ref, z_ref, om_ref, gw_ref, mw_ref, o_ref):
        for h in range(nh):
            hs = slice(HEAD * h, HEAD * (h + 1))
            a = og_ref[:, hs]
            r = lax.rsqrt(jnp.mean(a * a, axis=-1, keepdims=True) + EPS)
            o_ref[:, hs] = (a * r * gw_ref[...] * _silu(z_ref[:, hs])).astype(BF16)
            b = om_ref[:, hs]
            r = lax.rsqrt(jnp.mean(b * b, axis=-1, keepdims=True) + EPS)
            o_ref[:, w + HEAD * h:w + HEAD * (h + 1)] = (b * r * mw_ref[...]).astype(BF16)

    return _pcall(body, name="mix_fwd", grid=(s // ts,),
                  in_specs=[_rows(ts, w), _rows(ts, w, z_col), _rows(ts, w), _full((1, HEAD)),
                            _full((1, HEAD))],
                  out_specs=_rows(ts, 2 * w), out_shape=SDS((s, 2 * w), BF16))(og, proj, om, gw, mw)


def _mix_bwd(dmix, og, proj, om, gw, mw, nh, z_col):
    s = og.shape[0]
    w = nh * HEAD
    ts = _tile(s, 256, 8)

    def body(d_ref, og_ref, z_ref, om_ref, gw_ref, mw_ref, dog_ref, dz_ref, dom_ref, dgw_ref, dmw_ref,
             dl_ref):
        @pl.when(pl.program_id(0) == 0)
        def _():
            dgw_ref[...] = jnp.zeros_like(dgw_ref)
            dmw_ref[...] = jnp.zeros_like(dmw_ref)

        dgw = jnp.zeros((1, HEAD), F32)
        dmw = jnp.zeros((1, HEAD), F32)
        deltas = []
        for h in range(nh):
            hs = slice(HEAD * h, HEAD * (h + 1))
            a, z, dy = og_ref[:, hs], z_ref[:, hs], d_ref[:, hs]
            r = lax.rsqrt(jnp.mean(a * a, axis=-1, keepdims=True) + EPS)
            ah = a * r
            sz = _silu(z)
            dz_ref[:, hs] = (dy * (ah * gw_ref[...]) * _dsilu(z)).astype(BF16)
            dn = dy * sz
            dgw = dgw + jnp.sum(dn * ah, axis=0, keepdims=True)
            dah = dn * gw_ref[...]
            dog_ref[:, hs] = r * (dah - ah * jnp.mean(dah * ah, axis=-1, keepdims=True))
            b, dyb = om_ref[:, hs], d_ref[:, w + HEAD * h:w + HEAD * (h + 1)]
            r = lax.rsqrt(jnp.mean(b * b, axis=-1, keepdims=True) + EPS)
            bh = b * r
            dmw = dmw + jnp.sum(dyb * bh, axis=0, keepdims=True)
            dbh = dyb * mw_ref[...]
            dom = r * (dbh - bh * jnp.mean(dbh * bh, axis=-1, keepdims=True))
            dom_ref[:, hs] = dom.astype(BF16)
            deltas.append(jnp.broadcast_to(jnp.sum(dom * b, axis=-1, keepdims=True), (ts, LANE)))
        for g in range(nh // MLA_HP):
            dl_ref[g] = _pair_pack(deltas[2 * g], deltas[2 * g + 1])
        dgw_ref[...] += dgw
        dmw_ref[...] += dmw

    return _pcall(body, name="mix_bwd", grid=(s // ts,),
                  in_specs=[_rows(ts, 2 * w), _rows(ts, w), _rows(ts, w, z_col), _rows(ts, w),
                            _full((1, HEAD)), _full((1, HEAD))],
                  out_specs=[_rows(ts, w), _rows(ts, w), _rows(ts, w), _full((1, HEAD)), _full((1, HEAD)),
                             pl.BlockSpec((nh // MLA_HP, ts, LANE), lambda i: (0, i, 0))],
                  out_shape=[SDS((s, w), F32), SDS((s, w), BF16), SDS((s, w), BF16), SDS((1, HEAD), F32),
                             SDS((1, HEAD), F32), SDS((nh // MLA_HP, s, LANE), F32)])(dmix, og, proj, om, gw, mw)


def _swiglu_fwd(h2, wg, wu):
    m, kdim = h2.shape
    tn = wg.shape[2]
    n = 4 * tn
    tm, tk = _tile(m, 512), _tile(kdim, 2048)
    nk = kdim // tk

    def body(a_ref, g_ref, u_ref, act_ref, go_ref, uo_ref, gacc, uacc):
        k = pl.program_id(2)

        @pl.when(k == 0)
        def _():
            gacc[...] = jnp.zeros_like(gacc)
            uacc[...] = jnp.zeros_like(uacc)

        a = a_ref[...]
        gacc[...] += _dot(a, g_ref[...])
        uacc[...] += _dot(a, u_ref[...])

        @pl.when(k == nk - 1)
        def _():
            g, u = gacc[...], uacc[...]
            act_ref[...] = (_silu(g) * u).astype(BF16)
            go_ref[...] = g.astype(BF16)
            uo_ref[...] = u.astype(BF16)

    a_spec = pl.BlockSpec((tm, tk), lambda i, j, k: (i, k))
    b_spec = pl.BlockSpec((None, tk, tn), lambda i, j, k: (j, k, 0))
    o_spec = pl.BlockSpec((tm, tn), lambda i, j, k: (i, j))
    return _pcall(body, name="swiglu_fwd", grid=(m // tm, n // tn, nk),
                  in_specs=[a_spec, b_spec, b_spec], out_specs=[o_spec] * 3,
                  out_shape=[SDS((m, n), BF16)] * 3,
                  scratch=[pltpu.VMEM((tm, tn), F32), pltpu.VMEM((tm, tn), F32)])(h2, wg, wu)


def _swiglu_bwd(dx3, wd, g, u):
    m, kdim = dx3.shape
    n = wd.shape[0]
    tm, tn = _tile(m, 1024), _tile(n, 512)
    parts = 2 if tm % 16 == 0 else 1
    th = tm // parts

    def body(a_ref, b_ref, g_ref, u_ref, dg_ref, du_ref):
        b = b_ref[...]
        rows = [slice(th * c, th * (c + 1)) for c in range(parts)]
        da = [_dot_nt(a_ref[rs, :], b) for rs in rows]
        for rs, d in zip(rows, da):
            gv, uv = g_ref[rs, :].astype(F32), u_ref[rs, :].astype(F32)
            dg_ref[rs, :] = (d * uv * _dsilu(gv)).astype(BF16)
            du_ref[rs, :] = (d * _silu(gv)).astype(BF16)

    a_spec = pl.BlockSpec((tm, kdim), lambda i, j: (i, 0))
    b_spec = pl.BlockSpec((tn, kdim), lambda i, j: (j, 0))
    o_spec = pl.BlockSpec((tm, tn), lambda i, j: (i, j))
    return _pcall(body, name="swiglu_bwd", grid=(m // tm, n // tn),
                  in_specs=[a_spec, b_spec, o_spec, o_spec], out_specs=[o_spec] * 2,
                  out_shape=[SDS((m, n), BF16)] * 2)(dx3, wd, g, u)


def _sum_pair(g, recv, place, name):
    _, _, rh, c = g.shape
    tr = _tile(rh, 256, 16)

    def body(pl_ref, g_ref, r_ref, o16_ref, own_ref):
        sm = g_ref[...].astype(F32) + r_ref[...].astype(F32)
        o16_ref[...] = sm.astype(BF16)

        @pl.when(pl.program_id(1) == pl_ref[1])
        def _():
            own_ref[...] = sm

    grid_spec = pltpu.PrefetchScalarGridSpec(
        num_scalar_prefetch=1, grid=(rh // tr, 4),
        in_specs=[pl.BlockSpec((None, None, tr, c), lambda i, t, p: (t, p[0], i, 0)),
                  pl.BlockSpec((None, tr, c), lambda i, t, p: (t, i, 0))],
        out_specs=[pl.BlockSpec((None, tr, c), lambda i, t, p: (t, i, 0)),
                   pl.BlockSpec((tr, c), lambda i, t, p: (i, 0))])
    return pl.pallas_call(
        body, name=name, grid_spec=grid_spec,
        out_shape=[SDS((4, rh, c), BF16), SDS((rh, c), F32)],
        compiler_params=pltpu.CompilerParams(dimension_semantics=("arbitrary",) * 2,
                                             vmem_limit_bytes=VMEM_LIMIT))(place, g, recv)


def _sum_chips(own, recv, name):
    rh, c = own.shape
    tr = _tile(rh, 256, 16)

    def body(o_ref, r_ref, out_ref):
        acc = o_ref[...]
        for j in range(3):
            acc = acc + r_ref[j].astype(F32)
        out_ref[...] = acc

    return _pcall(body, name=name, grid=(rh // tr,),
                  in_specs=[_rows(tr, c), pl.BlockSpec((3, tr, c), lambda i: (0, i, 0))],
                  out_specs=_rows(tr, c), out_shape=SDS(own.shape, F32))(own, recv)


def _adamw_update(wv, gv, mv, vv):
    mn = ADAM_B1 * mv + (1.0 - ADAM_B1) * gv
    vn = ADAM_B2 * vv + (1.0 - ADAM_B2) * (gv * gv)
    m_hat = mn / (1.0 - ADAM_B1 ** ADAM_STEP)
    v_hat = vn / (1.0 - ADAM_B2 ** ADAM_STEP)
    return -ADAM_LR * (m_hat / (jnp.sqrt(v_hat) + ADAM_EPS) + ADAM_WD * wv), mn, vn


def _adamw(w, g, m, v, name):
    r, c = w.shape
    tr = _tile(r, 256, 8)

    def body(w_ref, g_ref, m_ref, v_ref, d_ref, mo_ref, vo_ref):
        d_ref[...], mo_ref[...], vo_ref[...] = _adamw_update(w_ref[...], g_ref[...], m_ref[...], v_ref[...])

    spec = _rows(tr, c)
    return _pcall(body, name=name, grid=(r // tr,), in_specs=[spec] * 4, out_specs=[spec] * 3,
                  out_shape=[SDS(w.shape, F32)] * 3)(w, g, m, v)


def _adamw_halves(w, mine, theirs, m, v, place, name):
    r, c = w.shape
    rh = r // 2
    tr = _tile(rh, 256, 8)
    nt = rh // tr

    def body(p_ref, w_ref, a_ref, b_ref, m_ref, v_ref, g_ref, d_ref, mo_ref, vo_ref):
        gv = jnp.where(pl.program_id(0) // nt == p_ref[0], a_ref[...], b_ref[...])
        g_ref[...] = gv
        d_ref[...], mo_ref[...], vo_ref[...] = _adamw_update(w_ref[...], gv, m_ref[...], v_ref[...])

    full = pl.BlockSpec((tr, c), lambda i, p: (i, 0))
    half = pl.BlockSpec((tr, c), lambda i, p: (i % nt, 0))
    grid_spec = pltpu.PrefetchScalarGridSpec(num_scalar_prefetch=1, grid=(2 * nt,),
                                             in_specs=[full, half, half, full, full], out_specs=[full] * 4)
    return pl.pallas_call(
        body, name=name, grid_spec=grid_spec, out_shape=[SDS(w.shape, F32)] * 4,
        compiler_params=pltpu.CompilerParams(dimension_semantics=("arbitrary",),
                                             vmem_limit_bytes=VMEM_LIMIT))(place, w, mine, theirs, m, v)


def _place():
    x, y, c = lax.axis_index("x"), lax.axis_index("y"), lax.axis_index("c")
    chips = [(1 - x, y), (x, 1 - y), (1 - x, 1 - y)]
    return x, y, c, chips


_ANY = pl.BlockSpec(memory_space=pl.ANY)


def _remote(src, dst, sems, k, to):
    return pltpu.make_async_remote_copy(src_ref=src, dst_ref=dst, send_sem=sems[0].at[k], recv_sem=sems[1].at[k],
                                        device_id=to, device_id_type=MESH)


class _Gather:
    def __init__(self, shards):
        n = len(shards)
        self.arrays = list(shards)
        self.out_shapes = [SDS((4,) + a.shape, a.dtype) for a in shards]
        self.scratch = [pltpu.SemaphoreType.DMA((7 * n,)), pltpu.SemaphoreType.DMA((7 * n,))]

    def _plan(self, ins, outs, sems):
        x, y, c, chips = _place()
        own, sib = 2 * x + y, (x, y, 1 - c)
        plan = []
        for wi, (w, o) in enumerate(zip(ins, outs)):
            rh = w.shape[0] // 2
            mine, theirs = pl.ds(c * rh, rh), pl.ds((1 - c) * rh, rh)
            whole = _remote(w, o.at[own], sems, 7 * wi + 6, sib)
            ici, d2d, d2d_in = [], [], []
            for j, (tx, ty) in enumerate(chips):
                t = 2 * tx + ty
                ici.append(_remote(w.at[mine], o.at[own, mine], sems, 7 * wi + j, (tx, ty, c)))
                d2d.append(_remote(o.at[t, mine], o.at[t, mine], sems, 7 * wi + 3 + j, sib))
                d2d_in.append(_remote(o.at[t, theirs], o.at[t, theirs], sems, 7 * wi + 3 + j, sib))
            plan.append((whole, ici, d2d, d2d_in))
        return plan

    def begin(self, ins, outs, sems):
        for whole, ici, _, _ in self._plan(ins, outs, sems):
            whole.start()
            for cp in ici:
                cp.start()

    def middle(self, ins, outs, sems):
        for _, ici, d2d, _ in self._plan(ins, outs, sems):
            for cp_in, cp_on in zip(ici, d2d):
                cp_in.wait_recv()
                cp_on.start()

    def finish(self, ins, outs, sems):
        for whole, ici, d2d, d2d_in in self._plan(ins, outs, sems):
            for cp in d2d_in:
                cp.wait_recv()
            for cp in ici + d2d:
                cp.wait_send()
            whole.wait()


class _Swap:
    def __init__(self, grads):
        n = len(grads)
        self.arrays = list(grads)
        self.out_shapes = [SDS((4,) + g.shape[2:], g.dtype) for g in grads]
        self.scratch = [pltpu.SemaphoreType.DMA((4 * n,)), pltpu.SemaphoreType.DMA((4 * n,))]

    def _plan(self, ins, outs, sems):
        x, y, c, _ = _place()
        return [_remote(g.at[t, 1 - c], o.at[t], sems, 4 * wi + t, (x, y, 1 - c))
                for wi, (g, o) in enumerate(zip(ins, outs)) for t in range(4)]

    def begin(self, ins, outs, sems):
        for cp in self._plan(ins, outs, sems):
            cp.start()

    def middle(self, ins, outs, sems):
        pass

    def finish(self, ins, outs, sems):
        for cp in self._plan(ins, outs, sems):
            cp.wait()


class _Exchange:
    def __init__(self, pieces):
        n = len(pieces)
        self.arrays = list(pieces)
        self.out_shapes = [SDS((3,) + p.shape[1:], p.dtype) for p in pieces]
        self.scratch = [pltpu.SemaphoreType.DMA((3 * n,)), pltpu.SemaphoreType.DMA((3 * n,))]

    def _plan(self, ins, outs, sems):
        x, y, c, chips = _place()
        return [_remote(g.at[2 * tx + ty], o.at[j], sems, 3 * wi + j, (tx, ty, c))
                for wi, (g, o) in enumerate(zip(ins, outs)) for j, (tx, ty) in enumerate(chips)]

    def begin(self, ins, outs, sems):
        for cp in self._plan(ins, outs, sems):
            cp.start()

    def middle(self, ins, outs, sems):
        pass

    def finish(self, ins, outs, sems):
        for cp in self._plan(ins, outs, sems):
            cp.wait()


class _Share:
    def __init__(self, totals):
        n = len(totals)
        self.arrays = list(totals)
        self.out_shapes = [SDS(t.shape, t.dtype) for t in totals]
        self.scratch = [pltpu.SemaphoreType.DMA((n,)), pltpu.SemaphoreType.DMA((n,))]

    def _plan(self, ins, outs, sems):
        x, y, c, _ = _place()
        return [_remote(t, o, sems, wi, (x, y, 1 - c)) for wi, (t, o) in enumerate(zip(ins, outs))]

    def begin(self, ins, outs, sems):
        for cp in self._plan(ins, outs, sems):
            cp.start()

    def middle(self, ins, outs, sems):
        pass

    def finish(self, ins, outs, sems):
        for cp in self._plan(ins, outs, sems):
            cp.wait()


def _ride_begin(rider, r_refs, step):
    @pl.when(step == 0)
    def _():
        rider.begin(*r_refs)


def _ride_end(rider, r_refs, step, nsteps):
    @pl.when(step == min(3 * nsteps // 4, nsteps - 1))
    def _():
        rider.middle(*r_refs)

    @pl.when(step == nsteps - 1)
    def _():
        rider.finish(*r_refs)


def _comm(rider, name):
    n_in, n_out = len(rider.arrays), len(rider.out_shapes)

    def body(*refs):
        r_refs = (refs[:n_in], refs[n_in:n_in + n_out], refs[n_in + n_out:])
        rider.begin(*r_refs)
        rider.middle(*r_refs)
        rider.finish(*r_refs)

    return pl.pallas_call(body, name=name, out_shape=rider.out_shapes, in_specs=[_ANY] * n_in,
                          out_specs=[_ANY] * n_out, scratch_shapes=rider.scratch)(*rider.arrays)


def _small_allreduce(pk, name):
    r = pk.shape[0]
    rels = [(dx, dy, dc) for dx in (0, 1) for dy in (0, 1) for dc in (0, 1) if dx or dy or dc]

    def body(p_ref, o_ref, buf, send_sems, recv_sems):
        x, y, c, _ = _place()
        me = 4 * x + 2 * y + c
        buf[me] = p_ref[...]
        cps = []
        for k, (dx, dy, dc) in enumerate(rels):
            to = (1 - x if dx else x, 1 - y if dy else y, 1 - c if dc else c)
            cps.append(pltpu.make_async_remote_copy(src_ref=p_ref, dst_ref=buf.at[me], send_sem=send_sems.at[k],
                                                    recv_sem=recv_sems.at[k], device_id=to,
                                                    device_id_type=MESH))
        for cpy in cps:
            cpy.start()
        for cpy in cps:
            cpy.wait()
        acc = buf[0]
        for d in range(1, 8):
            acc = acc + buf[d]
        o_ref[...] = acc

    vm = pl.BlockSpec(memory_space=pltpu.VMEM)
    return pl.pallas_call(body, name=name, out_shape=SDS(pk.shape, F32), in_specs=[vm], out_specs=vm,
                          scratch_shapes=[pltpu.VMEM((8, r, LANE), F32), pltpu.SemaphoreType.DMA((7,)),
                                          pltpu.SemaphoreType.DMA((7,))])(pk)


ATTN_W = ("w_in", "w_uq", "w_ukv", "w_out")
FFN_W = ("w_gate", "w_up", "w_down")
BIG = ATTN_W + FFN_W


def _cols_from_chips(g):
    return jnp.concatenate([g[t] for t in range(4)], axis=1)


def _cols_to_chips(full):
    r, n = full.shape
    return full.reshape(r, 4, n // 4).transpose(1, 0, 2).reshape(4, 2, r // 2, n // 4)


def _rows_to_chips(full):
    n, c = full.shape
    return full.reshape(4, 2, n // 8, c)


def _permute_w_in(w, nh):
    d = w.shape[0]
    g = 4 * nh * HEAD
    lr = (w.shape[1] - g - 2 * nh - ROPE) // 2
    o = g + 2 * nh
    pad = jnp.zeros((d, LANE - ROPE - 8 - nh), w.dtype)
    pad8 = jnp.zeros((d, 8 - nh), w.dtype)
    return jnp.concatenate([w[:, :g], w[:, o:o + 2 * lr], w[:, o + 2 * lr:], w[:, g:g + nh], pad8,
                            w[:, g + nh:g + 2 * nh], pad, jnp.zeros((d, LANE), w.dtype)], axis=1)


def _unpermute_w_in(wp, nh, lr):
    g = 4 * nh * HEAD
    mc = g + 2 * lr
    return jnp.concatenate([wp[:, :g], wp[:, mc + B_LANE:mc + B_LANE + nh], wp[:, mc + A_LANE:mc + A_LANE + nh],
                            wp[:, g:g + 2 * lr], wp[:, mc:mc + ROPE]], axis=1)


def _permute_w_uq(w, nh):
    lr = w.shape[0]
    w3 = w.reshape(lr, nh, HEAD + ROPE)
    return jnp.concatenate([w3, jnp.zeros((lr, nh, HEAD - ROPE), w.dtype)], axis=2).reshape(lr, nh * 2 * HEAD)


def _unpermute_w_uq(wp, nh):
    lr = wp.shape[0]
    return wp.reshape(lr, nh, 2 * HEAD)[:, :, :HEAD + ROPE].reshape(lr, nh * (HEAD + ROPE))


def _permute_w_ukv(w, nh):
    lr = w.shape[0]
    w3 = w.reshape(lr, nh, 2 * HEAD)
    kp = jnp.concatenate([w3[:, :, :HEAD], jnp.zeros((lr, nh, HEAD), w.dtype)], axis=2)
    return jnp.concatenate([kp.reshape(lr, nh * 2 * HEAD), w3[:, :, HEAD:].reshape(lr, nh * HEAD)], axis=1)


def _unpermute_w_ukv(wp, nh):
    lr = wp.shape[0]
    kp = wp[:, :nh * 2 * HEAD].reshape(lr, nh, 2 * HEAD)[:, :, :HEAD]
    vp = wp[:, nh * 2 * HEAD:].reshape(lr, nh, HEAD)
    return jnp.concatenate([kp, vp], axis=2).reshape(lr, nh * 2 * HEAD)


def _sum_pairs(grads, recv, place, tag):
    sums = [_sum_pair(g, r, place, "sum_pair_%s%d" % (tag, k)) for k, (g, r) in enumerate(zip(grads, recv))]
    return [s[0] for s in sums], [s[1] for s in sums]


def _reduce_end(own, recv, tag):
    return [_sum_chips(o, r, "sum_chips_%s%d" % (tag, k)) for k, (o, r) in enumerate(zip(own, recv))]


def _step(x, pos, tgt, w_in, attn_shards, ffn_shards, small, place):
    nh = small["a_log"].shape[1]
    lr = small["q_norm_w"].shape[1]
    w = nh * HEAD
    z_col, col_q, col_kv = 3, 4 * w // lr, 4 * w // lr + 1
    misc_c = 4 * w + 2 * lr
    misc_col = misc_c // LANE
    assert (4 * w) % lr == 0 and small["kv_norm_w"].shape[1] == lr

    zl = jnp.zeros((1, LANE), F32)
    alog_l = zl.at[:, A_LANE:A_LANE + nh].set(small["a_log"])
    dtb_l = zl.at[:, A_LANE:A_LANE + nh].set(small["dt_bias"])
    conv_w = small["conv_w"]

    h1, (in4,) = _norm_fwd(x, small["attn_norm_w"], "norm1", rider=_Gather([w_in]))
    win_p = _permute_w_in(_cols_from_chips(in4), nh)
    proj, (uq4, ukv4, out4) = _mm([(h1, win_p)], name="proj_in", rider=_Gather(attn_shards))
    wuq_p = _permute_w_uq(_cols_from_chips(uq4), nh)
    wukv_p = _permute_w_ukv(_cols_from_chips(ukv4), nh)
    w_out = out4.reshape(-1, out4.shape[2])
    gq, gk, gv, gb, gbt = _gdn_prep(proj, conv_w, alog_l, dtb_l, nh, misc_col)
    o_gdn, states = _gdn_fwd(gq, gk, gv, gb, gbt, nh)
    cqn, ckvn = _mla_norm(proj, small["q_norm_w"], small["kv_norm_w"], col_q, col_kv)
    qraw = _mm([(cqn, wuq_p)], name="proj_uq", out_dtype=BF16)
    kvraw = _mm([(ckvn, wukv_p)], name="proj_ukv", out_dtype=BF16)
    qc, kc, vv = _mla_rope(qraw, kvraw, proj, pos, nh, misc_col)
    o_mla, lse, (wg4, wu4, wd4) = _mla_fwd(qc, kc, vv, nh, _Gather(ffn_shards))
    w_down = wd4.reshape(-1, wd4.shape[2])
    mixed = _mix_fwd(o_gdn, proj, o_mla, small["gdn_norm_w"], small["mla_out_norm_w"], nh, z_col)
    x2 = _mm([(mixed, w_out)], name="proj_out", res=x)
    h2 = _norm_fwd(x2, small["ffn_norm_w"], "norm2")
    act, gpre, upre = _swiglu_fwd(h2, wg4, wu4)
    x3 = _mm([(act, w_down)], name="proj_down", res=x2, tk=2816)
    dx3, d_final, loss, dx3h = _final_loss(x3, tgt, small["final_norm_w"])

    gs = {"final_norm_w": d_final}
    dgate, dup = _swiglu_bwd(dx3h, w_down, gpre, upre)
    g_down = _rows_to_chips(_mm([(act, dx3h)], name="dw_down", ta=True, out_dtype=BF16))
    g_gate = _mm([(h2, dgate)], name="dw_gate", ta=True, out_dtype=BF16, out_chips=True)
    g_up = _mm([(h2, dup)], name="dw_up", ta=True, out_dtype=BF16, out_chips=True)
    halves = lambda g: g.reshape(4, 2, g.shape[1] // 2, g.shape[2])
    ffn_g = [halves(g_gate), halves(g_up), g_down]
    dh2, ffn_sib = _mm([(dgate, wg4), (dup, wu4)], name="dh2", tb=True, b_chips=True, rider=_Swap(ffn_g))
    ffn16, ffn_own = _sum_pairs(ffn_g, ffn_sib, place, "ffn")
    dx2, gs["ffn_norm_w"], dx2h = _norm_bwd(dh2, x2, small["ffn_norm_w"], dx3, "norm2_bwd", True)
    dmix = _mm([(dx2h, w_out)], name="dmix", tb=True)
    g_out = _rows_to_chips(_mm([(mixed, dx2h)], name="dw_out", ta=True, out_dtype=BF16))
    d_ogdn, dz, d_omla, gs["gdn_norm_w"], gs["mla_out_norm_w"], delta = _mix_bwd(
        dmix, o_gdn, proj, o_mla, small["gdn_norm_w"], small["mla_out_norm_w"], nh, z_col)
    dqc, dkc, dvv, ffn_recv = _mla_bwd(qc, kc, vv, d_omla, lse, delta, nh, _Exchange(ffn16))
    ffn_tot = _reduce_end(ffn_own, ffn_recv, "ffn")
    dqraw, dkvraw, dkr = _mla_rope_bwd(dqc, dkc, dvv, pos, nh)
    dcqn = _mm([(dqraw, wuq_p)], name="dcqn", tb=True)
    dckvn = _mm([(dkvraw, wukv_p)], name="dckvn", tb=True)
    g_uq = _cols_to_chips(_unpermute_w_uq(_mm([(cqn, dqraw)], name="dw_uq", ta=True, out_dtype=BF16), nh))
    g_ukv = _cols_to_chips(_unpermute_w_ukv(_mm([(ckvn, dkvraw)], name="dw_ukv", ta=True, out_dtype=BF16), nh))
    dcq, dckv, gs["q_norm_w"], gs["kv_norm_w"] = _mla_norm_bwd(
        proj, small["q_norm_w"], small["kv_norm_w"], dcqn, dckvn, col_q, col_kv)
    (dgq, dgk, dgv, dgb), ffn_shared = _gdn_bwd(gq, gk, gv, gb, gbt, states, d_ogdn, nh, _Share(ffn_tot))
    dconv, dmisc, gs["conv_w"], dal, ddb = _gdn_prep_bwd(
        proj, conv_w, alog_l, dtb_l, dgq, dgk, dgv, dgb, dkr, nh, misc_col)
    gs["a_log"] = dal[:, A_LANE:A_LANE + nh]
    gs["dt_bias"] = ddb[:, A_LANE:A_LANE + nh]
    dqkv = _conv_bwd_input(dconv, conv_w)
    dproj = jnp.concatenate([dqkv, dz, dcq, dckv, dmisc, jnp.zeros((x.shape[0], LANE), BF16)], axis=1)
    g_in = _cols_to_chips(_unpermute_w_in(_mm([(h1, dproj)], name="dw_in", ta=True, out_dtype=BF16), nh, lr))
    att_g = [g_in, g_uq, g_ukv, g_out]
    att16, att_own = _sum_pairs(att_g, _comm(_Swap(att_g), "swap_att"), place, "att")
    dh1, att_recv = _mm([(dproj, win_p)], name="dh1", tb=True, rider=_Exchange(att16))
    att_tot = _reduce_end(att_own, att_recv, "att")
    att_shared = _comm(_Share(att_tot), "share_att")
    grad_x, gs["attn_norm_w"] = _norm_bwd(dh1, x, small["attn_norm_w"], dx2, "norm1_bwd", False)
    return loss, grad_x, att_tot + ffn_tot, list(att_shared) + list(ffn_shared), gs


SMALL = ("attn_norm_w", "ffn_norm_w", "final_norm_w", "q_norm_w", "kv_norm_w", "gdn_norm_w",
         "mla_out_norm_w", "a_log", "dt_bias")
WEIGHTS = ("attn_norm_w", "w_in", "conv_w", "a_log", "dt_bias", "gdn_norm_w", "q_norm_w", "w_uq",
           "kv_norm_w", "w_ukv", "mla_out_norm_w", "w_out", "ffn_norm_w", "w_gate", "w_up", "w_down",
           "final_norm_w")


def _pack_small(vecs):
    flat = jnp.concatenate([v.astype(F32).reshape(-1) for v in vecs])
    pad = (-flat.shape[0]) % (8 * LANE)
    return jnp.concatenate([flat, jnp.zeros((pad,), F32)]).reshape(-1, LANE)


def kernel(x, positions, attn_norm_w, w_in, conv_w, a_log, dt_bias, gdn_norm_w, q_norm_w, w_uq, kv_norm_w, w_ukv, mla_out_norm_w, w_out, ffn_norm_w, w_gate, w_up, w_down, final_norm_w, loss_target, m_attn_norm_w, m_w_in, m_conv_w, m_a_log, m_dt_bias, m_gdn_norm_w, m_q_norm_w, m_w_uq, m_kv_norm_w, m_w_ukv, m_mla_out_norm_w, m_w_out, m_ffn_norm_w, m_w_gate, m_w_up, m_w_down, m_final_norm_w, v_attn_norm_w, v_w_in, v_conv_w, v_a_log, v_dt_bias, v_gdn_norm_w, v_q_norm_w, v_w_uq, v_kv_norm_w, v_w_ukv, v_mla_out_norm_w, v_w_out, v_ffn_norm_w, v_w_gate, v_w_up, v_w_down, v_final_norm_w):
    args = dict(locals())
    xi, yi, ci = lax.axis_index("x"), lax.axis_index("y"), lax.axis_index("c")
    chip = 2 * xi + yi

    def two_d(a):
        return a.reshape(a.shape[-2:]) if a.ndim >= 2 else a.reshape(1, -1)

    wloc = {n: two_d(args[n]) for n in WEIGHTS}
    mloc = {n: two_d(args["m_" + n]) for n in WEIGHTS}
    vloc = {n: two_d(args["v_" + n]) for n in WEIGHTS}

    cw = wloc["conv_w"]
    cshard = cw.shape[1]
    cfull = jnp.zeros((CONV, 4 * cshard), F32)
    cfull = lax.dynamic_update_slice(cfull, jnp.where(ci == 0, cw, 0.0), (0, chip * cshard))
    conv_full = _small_allreduce(_pack_small([cfull]), "gather_conv_w").reshape(-1)[:CONV * 4 * cshard]
    conv_full = conv_full.reshape(CONV, 4 * cshard)

    small = {n: wloc[n] for n in SMALL}
    small["conv_w"] = conv_full

    pos = positions.reshape(-1, 1).astype(F32)
    place = jnp.stack([ci, chip]).astype(jnp.int32)
    loss, grad_x, totals, from_sib, gs = _step(
        two_d(x), pos, two_d(loss_target), wloc["w_in"].astype(BF16), [wloc[n].astype(BF16) for n in ATTN_W[1:]],
        [wloc[n].astype(BF16) for n in FFN_W], small, place)

    small_names = SMALL + ("conv_w",)
    pk = _pack_small([gs[n] for n in small_names] + [loss])
    red = _small_allreduce(pk, "reduce_small").reshape(-1)
    gsm, off = {}, 0
    for n in small_names:
        shp = gs[n].shape
        gsm[n] = red[off:off + shp[0] * shp[1]].reshape(shp)
        off += shp[0] * shp[1]
    loss_out = red[off]
    gsm["conv_w"] = lax.dynamic_slice(gsm["conv_w"], (0, chip * cshard), (CONV, cshard))

    grads, deltas, new_m, new_v = {}, {}, {}, {}
    for n, mine, theirs in zip(BIG, totals, from_sib):
        grads[n], deltas[n], new_m[n], new_v[n] = _adamw_halves(wloc[n], mine, theirs, mloc[n], vloc[n], place,
                                                                "adamw_" + n)
    grads["conv_w"] = gsm["conv_w"]
    deltas["conv_w"], new_m["conv_w"], new_v["conv_w"] = _adamw(wloc["conv_w"], gsm["conv_w"], mloc["conv_w"],
                                                                vloc["conv_w"], "adamw_conv_w")
    sm_shapes = [wloc[n].shape for n in SMALL]
    pd, pm, pv = _adamw(_pack_small([wloc[n] for n in SMALL]), _pack_small([gsm[n] for n in SMALL]),
                        _pack_small([mloc[n] for n in SMALL]), _pack_small([vloc[n] for n in SMALL]),
                        "adamw_small")
    for dst, packed in ((deltas, pd), (new_m, pm), (new_v, pv)):
        flat, off = packed.reshape(-1), 0
        for n, shp in zip(SMALL, sm_shapes):
            dst[n] = flat[off:off + shp[0] * shp[1]].reshape(shp)
            off += shp[0] * shp[1]
    for n in SMALL:
        grads[n] = gsm[n]

    def like(n, a):
        return a.reshape(args[n].shape)

    outs = [loss_out.reshape(()), grad_x.reshape(x.shape)]
    for group in (grads, deltas, new_m, new_v):
        outs += [like(n, group[n]) for n in WEIGHTS]
    return tuple(outs)
```

```python
import functools

import jax
import jax.numpy as jnp
from jax import lax
from jax.experimental import pallas as pl
from jax.experimental.pallas import tpu as pltpu

F32, BF16 = jnp.float32, jnp.bfloat16
SDS = jax.ShapeDtypeStruct
MESH = pl.DeviceIdType.MESH

HEAD = 128
ROPE = 64
CHUNK = 64
PAIR = 2 * CHUNK
CONV = 4
EPS = 1e-6
ROPE_THETA = 10000.0
LANE = 128
B_LANE = 64
A_LANE = 72
VMEM_LIMIT = 48 * 1024 * 1024
VMEM_LIMIT_WIDE = 56 * 1024 * 1024
MLA_BLOCK = 512
LOG2E = 1.4426950408889634
LN2 = 0.6931471805599453
SM_SCALE = (HEAD + ROPE) ** -0.5

ADAM_LR = 0.001
ADAM_B1 = 0.9
ADAM_B2 = 0.999
ADAM_EPS = 1e-08
ADAM_WD = 0.01
ADAM_STEP = 10


def _tile(n, pref, mult=LANE):
    if n <= pref:
        return n
    t = (pref // mult) * mult
    while t >= mult:
        if n % t == 0:
            return t
        t -= mult
    return n


def _pcall(body, *, name, grid, in_specs, out_specs, out_shape, scratch=(), vmem=VMEM_LIMIT, aliases=None):
    return pl.pallas_call(
        body, name=name, grid=grid, in_specs=in_specs, out_specs=out_specs,
        out_shape=out_shape, scratch_shapes=list(scratch), input_output_aliases=aliases or {},
        compiler_params=pltpu.CompilerParams(
            dimension_semantics=("arbitrary",) * len(grid), vmem_limit_bytes=vmem))


def _pcall_riding(core, rider, *, name, grid, in_specs, out_specs, out_shape, args, scratch=()):
    n_in, n_out, n_scr = len(in_specs), len(out_specs), len(scratch)
    r_in, r_out = len(rider.arrays), len(rider.out_shapes)

    def body(*refs):
        ins, refs = refs[:n_in], refs[n_in:]
        r_ins, refs = refs[:r_in], refs[r_in:]
        outs, refs = refs[:n_out], refs[n_out:]
        r_outs, refs = refs[:r_out], refs[r_out:]
        scr, sems = refs[:n_scr], refs[n_scr:]
        r_refs = (r_ins, r_outs, sems)
        _ride_begin(rider, r_refs, pl.program_id(0))
        core(*ins, *outs, *scr)
        _ride_end(rider, r_refs, pl.program_id(0), grid[0])

    res = _pcall(body, name=name, grid=grid, in_specs=list(in_specs) + [_ANY] * r_in,
                 out_specs=list(out_specs) + [_ANY] * r_out, out_shape=list(out_shape) + rider.out_shapes,
                 scratch=list(scratch) + rider.scratch)(*args, *rider.arrays)
    return res[:n_out], res[n_out:]


def _rows(ts, width, col=0):
    return pl.BlockSpec((ts, width), lambda i: (i, col))


def _full(shape):
    nd = len(shape)
    return pl.BlockSpec(shape, lambda i: (0,) * nd)


def _dot(a, b):
    return jnp.dot(a.astype(BF16), b.astype(BF16), preferred_element_type=F32)


def _dot_nt(a, b):
    return lax.dot_general(a.astype(BF16), b.astype(BF16), (((1,), (1,)), ((), ())),
                           preferred_element_type=F32)


def _dot_tn(a, b):
    return lax.dot_general(a.astype(BF16), b.astype(BF16), (((0,), (0,)), ((), ())),
                           preferred_element_type=F32)


def _sigmoid(x):
    return 1.0 / (1.0 + jnp.exp(-x))


def _silu(x):
    return x * _sigmoid(x)


def _dsilu(x):
    s = _sigmoid(x)
    return s * (1.0 + x * (1.0 - s))


def _lane_iota(shape):
    return lax.broadcasted_iota(jnp.int32, shape, len(shape) - 1)


def _col(block, idx):
    return jnp.sum(jnp.where(_lane_iota(block.shape) == idx, block, 0.0), axis=-1, keepdims=True)


def _mm(pairs, *, name, ta=False, tb=False, out_dtype=F32, res=None, tm=1024, tn=1024, tk=2048,
        b_chips=False, out_chips=False, rider=None):
    a0, b0 = pairs[0]
    if ta:
        kdim, m = a0.shape
    else:
        m, kdim = a0.shape
    if b_chips and tb:
        n, tk = b0.shape[1], b0.shape[2]
        assert kdim == 4 * tk
    elif b_chips:
        n, tn = 4 * b0.shape[2], b0.shape[2]
        assert kdim == b0.shape[1]
    else:
        n = b0.shape[0] if tb else b0.shape[1]
    if out_chips:
        tn = n // 4
    tm = _tile(m, tm)
    tn = tn if (out_chips or (b_chips and not tb)) else _tile(n, tn)
    tk = tk if (b_chips and tb) else _tile(kdim, tk)
    assert m % tm == 0 and n % tn == 0 and kdim % tk == 0
    nk, npair = kdim // tk, len(pairs)
    grid = (m // tm, n // tn, nk)
    dims = (((0 if ta else 1,), (1 if tb else 0,)), ((), ()))
    n_in = 2 * npair + (res is not None)
    r_in, r_out = (len(rider.arrays), len(rider.out_shapes)) if rider else (0, 0)

    def body(*refs):
        o_ref = refs[n_in + r_in]
        acc = refs[n_in + r_in + 1 + r_out]
        k = pl.program_id(2)
        if rider:
            r_refs = (refs[n_in:n_in + r_in], refs[n_in + r_in + 1:n_in + r_in + 1 + r_out],
                      refs[n_in + r_in + 2 + r_out:])
            step = (pl.program_id(0) * grid[1] + pl.program_id(1)) * nk + k
            _ride_begin(rider, r_refs, step)

        @pl.when(k == 0)
        def _():
            acc[...] = jnp.zeros_like(acc)

        tot = None
        for p in range(npair):
            d = lax.dot_general(refs[2 * p][...].astype(BF16), refs[2 * p + 1][...].astype(BF16),
                                dims, preferred_element_type=F32)
            tot = d if tot is None else tot + d
        acc[...] += tot

        @pl.when(k == nk - 1)
        def _():
            r = acc[...]
            if res is not None:
                r = r + refs[2 * npair][...]
            o_ref[...] = r.astype(out_dtype)

        if rider:
            _ride_end(rider, r_refs, step, grid[0] * grid[1] * nk)

    if ta:
        a_spec = pl.BlockSpec((tk, tm), lambda i, j, k: (k, i))
    else:
        a_spec = pl.BlockSpec((tm, tk), lambda i, j, k: (i, k))
    if b_chips and tb:
        b_spec = pl.BlockSpec((None, tn, tk), lambda i, j, k: (k, j, 0))
    elif b_chips:
        b_spec = pl.BlockSpec((None, tk, tn), lambda i, j, k: (j, k, 0))
    elif tb:
        b_spec = pl.BlockSpec((tn, tk), lambda i, j, k: (j, k))
    else:
        b_spec = pl.BlockSpec((tk, tn), lambda i, j, k: (k, j))
    if out_chips:
        o_spec = pl.BlockSpec((None, tm, tn), lambda i, j, k: (j, i, 0))
        o_shape = SDS((4, m, tn), out_dtype)
    else:
        o_spec = pl.BlockSpec((tm, tn), lambda i, j, k: (i, j))
        o_shape = SDS((m, n), out_dtype)
    in_specs, args = [], []
    for a, b in pairs:
        in_specs += [a_spec, b_spec]
        args += [a, b]
    if res is not None:
        in_specs.append(o_spec)
        args.append(res)
    out_specs, out_shapes, scratch = [o_spec], [o_shape], [pltpu.VMEM((tm, tn), F32)]
    if rider:
        in_specs += [_ANY] * r_in
        args += rider.arrays
        out_specs += [_ANY] * r_out
        out_shapes += rider.out_shapes
        scratch += rider.scratch
    outs = _pcall(body, name=name, grid=grid, in_specs=in_specs, out_specs=out_specs, out_shape=out_shapes,
                  scratch=scratch)(*args)
    return (outs[0], outs[1:]) if rider else outs[0]


def _norm_fwd(x, w, name, rider=None):
    s, d = x.shape
    ts = _tile(s, 512, 8)

    def body(x_ref, w_ref, h_ref):
        xv = x_ref[...]
        r = lax.rsqrt(jnp.mean(xv * xv, axis=-1, keepdims=True) + EPS)
        h_ref[...] = (xv * r * w_ref[...]).astype(BF16)

    spec = dict(name=name, grid=(s // ts,), in_specs=[_rows(ts, d), _full((1, d))])
    if rider is None:
        return _pcall(body, out_specs=_rows(ts, d), out_shape=SDS((s, d), BF16), **spec)(x, w)
    outs, r_outs = _pcall_riding(body, rider, out_specs=[_rows(ts, d)], out_shape=[SDS((s, d), BF16)],
                                 args=(x, w), **spec)
    return outs[0], r_outs


def _norm_bwd(dh, x, w, dres, name, with_bf16):
    s, d = x.shape
    ts = _tile(s, 256, 8)

    def body(dh_ref, x_ref, w_ref, dres_ref, dx_ref, dw_ref, *dx16_ref):
        @pl.when(pl.program_id(0) == 0)
        def _():
            dw_ref[...] = jnp.zeros_like(dw_ref)

        xv, dhv = x_ref[...], dh_ref[...]
        r = lax.rsqrt(jnp.mean(xv * xv, axis=-1, keepdims=True) + EPS)
        xh = xv * r
        dw_ref[...] += jnp.sum(dhv * xh, axis=0, keepdims=True)
        dxh = dhv * w_ref[...]
        dx = dres_ref[...] + r * (dxh - xh * jnp.mean(dxh * xh, axis=-1, keepdims=True))
        dx_ref[...] = dx
        for ref in dx16_ref:
            ref[...] = dx.astype(BF16)

    extra = 1 if with_bf16 else 0
    return _pcall(body, name=name, grid=(s // ts,),
                  in_specs=[_rows(ts, d), _rows(ts, d), _full((1, d)), _rows(ts, d)],
                  out_specs=[_rows(ts, d), _full((1, d))] + [_rows(ts, d)] * extra,
                  out_shape=[SDS((s, d), F32), SDS((1, d), F32)] + [SDS((s, d), BF16)] * extra)(
                      dh, x, w, dres)


def _final_loss(x3, tgt, w):
    s, d = x3.shape
    ts = _tile(s, 256, 8)

    def body(x_ref, t_ref, w_ref, dx_ref, dw_ref, loss_ref, dx16_ref):
        @pl.when(pl.program_id(0) == 0)
        def _():
            dw_ref[...] = jnp.zeros_like(dw_ref)
            loss_ref[...] = jnp.zeros_like(loss_ref)

        xv, wv = x_ref[...], w_ref[...]
        r = lax.rsqrt(jnp.mean(xv * xv, axis=-1, keepdims=True) + EPS)
        xh = xv * r
        err = xh * wv - t_ref[...]
        row = jnp.mean(err * err, axis=-1, keepdims=True)
        loss_ref[...] += 0.5 * jnp.sum(row, axis=0, keepdims=True)
        dy = err * (1.0 / d)
        dw_ref[...] += jnp.sum(dy * xh, axis=0, keepdims=True)
        dxh = dy * wv
        dx = r * (dxh - xh * jnp.mean(dxh * xh, axis=-1, keepdims=True))
        dx_ref[...] = dx
        dx16_ref[...] = dx.astype(BF16)

    return _pcall(body, name="final_loss", grid=(s // ts,),
                  in_specs=[_rows(ts, d), _rows(ts, d), _full((1, d))],
                  out_specs=[_rows(ts, d), _full((1, d)), _full((1, 1)), _rows(ts, d)],
                  out_shape=[SDS((s, d), F32), SDS((1, d), F32), SDS((1, 1), F32), SDS((s, d), BF16)])(
                      x3, tgt, w)


def _shift_down(cur, halo, s):
    if s == 0:
        return cur
    row8 = lax.broadcasted_iota(jnp.int32, halo.shape, 0)
    r = pltpu.roll(cur, s, 0)
    top = jnp.where(row8 < s, pltpu.roll(halo, s, 0), r[0:8])
    return jnp.concatenate([top, r[8:]], axis=0)


def _shift_up(cur, halo, s):
    if s == 0:
        return cur
    ts = cur.shape[0]
    row8 = lax.broadcasted_iota(jnp.int32, halo.shape, 0)
    r = pltpu.roll(cur, ts - s, 0)
    bot = jnp.where(row8 >= 8 - s, pltpu.roll(halo, 8 - s, 0), r[ts - 8:ts])
    return jnp.concatenate([r[:ts - 8], bot], axis=0)


def _chunk_tri(ts, upper):
    i = lax.broadcasted_iota(jnp.int32, (ts, ts), 0)
    j = lax.broadcasted_iota(jnp.int32, (ts, ts), 1)
    same = jnp.right_shift(i, 6) == jnp.right_shift(j, 6)
    return jnp.where(same & ((j >= i) if upper else (j <= i)), 1.0, 0.0).astype(F32)


def _gate_values(m, alog, dtb):
    lane = _lane_iota(m.shape)
    beta = _sigmoid(m)
    xg = m + dtb
    sp = jnp.maximum(xg, 0.0) + jnp.log(1.0 + jnp.exp(-jnp.abs(xg)))
    ga = (lane >= A_LANE) & (lane < A_LANE + 8)
    g = jnp.where(ga, -jnp.exp(alog) * sp, 0.0)
    return beta, g, xg, ga


def _l2_heads(a, nh, scale):
    outs, rs = [], []
    for h in range(nh):
        ah = a[:, HEAD * h:HEAD * (h + 1)]
        r = lax.rsqrt(jnp.sum(ah * ah, axis=-1, keepdims=True) + EPS)
        outs.append(ah * (r * scale))
        rs.append(r)
    return jnp.concatenate(outs, axis=-1), rs


def _gdn_prep(proj, conv_w, alog_l, dtb_l, nh, misc_col):
    s = proj.shape[0]
    w = nh * HEAD
    ts = _tile(s, 256, PAIR)
    hb = ts // 8

    def body(cur_ref, halo_ref, misc_ref, cw_ref, al_ref, db_ref, q_ref, k_ref, v_ref, gb_ref, gbt_ref):
        first = pl.program_id(0) == 0
        outs = (q_ref, k_ref, v_ref)
        for sec in range(3):
            cs = slice(sec * w, (sec + 1) * w)
            cur = cur_ref[:, cs]
            halo = jnp.where(first, 0.0, halo_ref[:, cs])
            pre = None
            for j in range(CONV):
                term = cw_ref[j:j + 1, cs] * _shift_down(cur, halo, CONV - 1 - j)
                pre = term if pre is None else pre + term
            act = _silu(pre)
            if sec == 0:
                act, _ = _l2_heads(act, nh, HEAD ** -0.5)
            elif sec == 1:
                act, _ = _l2_heads(act, nh, 1.0)
            outs[sec][...] = act
        m = misc_ref[...]
        lane = _lane_iota(m.shape)
        beta, g, _, ga = _gate_values(m, al_ref[...], db_ref[...])
        gcc = jnp.dot(_chunk_tri(ts, False), g, precision=lax.Precision.HIGHEST,
                      preferred_element_type=F32)
        gb = jnp.where((lane >= B_LANE) & (lane < B_LANE + 8), beta, jnp.where(ga, gcc, 0.0))
        gb_ref[...] = gb
        gbt_ref[...] = gb.T

    return _pcall(
        body, name="gdn_prep", grid=(s // ts,),
        in_specs=[_rows(ts, 3 * w),
                  pl.BlockSpec((8, 3 * w), lambda i: (jnp.maximum(i * hb - 1, 0), 0)),
                  _rows(ts, LANE, misc_col), _full((CONV, 3 * w)), _full((1, LANE)), _full((1, LANE))],
        out_specs=[_rows(ts, w), _rows(ts, w), _rows(ts, w), _rows(ts, LANE),
                   pl.BlockSpec((LANE, ts), lambda i: (0, i))],
        out_shape=[SDS((s, w), F32), SDS((s, w), F32), SDS((s, w), F32), SDS((s, LANE), F32),
                   SDS((LANE, s), F32)])(proj, proj, proj, conv_w, alog_l, dtb_l)


def _gdn_prep_bwd(proj, conv_w, alog_l, dtb_l, dq, dk, dv, dgb, dkr, dproj, nh, misc_col):
    s = proj.shape[0]
    w = nh * HEAD
    ts = _tile(s, 256, PAIR)
    hb = ts // 8
    assert misc_col % 2 == 0

    def body(cur_ref, halo_ref, misc_ref, cw_ref, al_ref, db_ref, dq_ref, dk_ref, dv_ref, dgb_ref,
             dkr_ref, _, dc_ref, dm_ref, dcw_ref, dal_ref, ddb_ref):
        first = pl.program_id(0) == 0

        @pl.when(first)
        def _():
            dcw_ref[...] = jnp.zeros_like(dcw_ref)
            dal_ref[...] = jnp.zeros_like(dal_ref)
            ddb_ref[...] = jnp.zeros_like(ddb_ref)

        dins = (dq_ref, dk_ref, dv_ref)
        for sec in range(3):
            cs = slice(sec * w, (sec + 1) * w)
            cur = cur_ref[:, cs]
            halo = jnp.where(first, 0.0, halo_ref[:, cs])
            us = [_shift_down(cur, halo, CONV - 1 - j) for j in range(CONV)]
            pre = None
            for j in range(CONV):
                term = cw_ref[j:j + 1, cs] * us[j]
                pre = term if pre is None else pre + term
            act = _silu(pre)
            dout = dins[sec][...]
            if sec < 2:
                scale = HEAD ** -0.5 if sec == 0 else 1.0
                parts = []
                for h in range(nh):
                    hs = slice(HEAD * h, HEAD * (h + 1))
                    ah = act[:, hs]
                    r = lax.rsqrt(jnp.sum(ah * ah, axis=-1, keepdims=True) + EPS)
                    ahat = ah * r
                    dy = dout[:, hs]
                    parts.append((scale * r) * (dy - ahat * jnp.sum(dy * ahat, axis=-1, keepdims=True)))
                dact = jnp.concatenate(parts, axis=-1)
            else:
                dact = dout
            dconv = dact * _dsilu(pre)
            dc_ref[:, cs] = dconv
            for j in range(CONV):
                dcw_ref[j:j + 1, cs] += jnp.sum(dconv * us[j], axis=0, keepdims=True)
        m = misc_ref[...]
        lane = _lane_iota(m.shape)
        al = al_ref[...]
        beta, g, xg, ga = _gate_values(m, al, db_ref[...])
        dgbv = dgb_ref[...]
        dg = jnp.dot(_chunk_tri(ts, True), jnp.where(ga, dgbv, 0.0), precision=lax.Precision.HIGHEST,
                     preferred_element_type=F32)
        da_raw = jnp.where(ga, dg * (-jnp.exp(al)) * _sigmoid(xg), 0.0)
        db_raw = jnp.where((lane >= B_LANE) & (lane < B_LANE + 8), dgbv * beta * (1.0 - beta), 0.0)
        dal_ref[...] += jnp.sum(dg * g, axis=0, keepdims=True)
        ddb_ref[...] += jnp.sum(da_raw, axis=0, keepdims=True)
        dm_ref[:, :LANE] = (dkr_ref[...] + da_raw + db_raw).astype(BF16)
        dm_ref[:, LANE:] = jnp.zeros((ts, LANE), BF16)

    return _pcall(
        body, name="gdn_prep_bwd", grid=(s // ts,),
        in_specs=[_rows(ts, 3 * w),
                  pl.BlockSpec((8, 3 * w), lambda i: (jnp.maximum(i * hb - 1, 0), 0)),
                  _rows(ts, LANE, misc_col), _full((CONV, 3 * w)), _full((1, LANE)), _full((1, LANE)),
                  _rows(ts, w), _rows(ts, w), _rows(ts, w), _rows(ts, LANE), _rows(ts, LANE), _ANY],
        out_specs=[_rows(ts, 3 * w), _rows(ts, 2 * LANE, misc_col // 2), _full((CONV, 3 * w)), _full((1, LANE)),
                   _full((1, LANE))],
        out_shape=[SDS((s, 3 * w), F32), SDS(dproj.shape, BF16), SDS((CONV, 3 * w), F32),
                   SDS((1, LANE), F32), SDS((1, LANE), F32)], aliases={11: 1})(
                       proj, proj, proj, conv_w, alog_l, dtb_l, dq, dk, dv, dgb, dkr, dproj)


def _conv_bwd_input(dconv, conv_w, dproj):
    s, c = dconv.shape
    ts = _tile(s, 256, 8)
    hb = ts // 8
    nblk8 = s // 8
    nt = s // ts

    def body(cur_ref, nxt_ref, cw_ref, _, o_ref):
        last = pl.program_id(0) == nt - 1
        cur = cur_ref[...]
        halo = jnp.where(last, 0.0, nxt_ref[...])
        acc = None
        for j in range(CONV):
            term = cw_ref[j:j + 1, :] * _shift_up(cur, halo, CONV - 1 - j)
            acc = term if acc is None else acc + term
        o_ref[...] = acc.astype(BF16)

    return _pcall(
        body, name="conv_bwd_input", grid=(nt,),
        in_specs=[_rows(ts, c),
                  pl.BlockSpec((8, c), lambda i: (jnp.minimum((i + 1) * hb, nblk8 - 1), 0)),
                  _full((CONV, c)), _ANY],
        out_specs=_rows(ts, c), out_shape=SDS(dproj.shape, BF16), aliases={3: 0})(
            dconv, dconv, conv_w, dproj)


def _inv_unit_lower(a):
    n = a[0].shape[0]
    i = lax.broadcasted_iota(jnp.int32, (n, n), 0)
    j = lax.broadcasted_iota(jnp.int32, (n, n), 1)
    eye = jnp.where(i == j, 1.0, 0.0)
    t = [eye - ah for ah in a]
    x = a
    for _ in range(5):
        x = [_dot(xh, xh) for xh in x]
        t = [th + _dot(th, xh) for th, xh in zip(t, x)]
    return t


def _pair_common(q, k, gcol, grow, bcol):
    i = lax.broadcasted_iota(jnp.int32, (PAIR, PAIR), 0)
    j = lax.broadcasted_iota(jnp.int32, (PAIR, PAIR), 1)
    same = jnp.right_shift(i, 6) == jnp.right_shift(j, 6)
    tril = same & (i >= j)
    strict = same & (i > j)
    dec = [jnp.where(tril, jnp.exp(jnp.minimum(gc - gr, 0.0)), 0.0) for gc, gr in zip(gcol, grow)]
    kk = [_dot_nt(kh, kh) for kh in k]
    qk = [_dot_nt(qh, kh) for qh, kh in zip(q, k)]
    a = [jnp.where(strict, b * kkh * d, 0.0) for b, kkh, d in zip(bcol, kk, dec)]
    t = _inv_unit_lower(a)
    p = [qkh * d for qkh, d in zip(qk, dec)]
    return dec, kk, a, t, p, tril, strict


def _ext(v, a):
    z = jnp.zeros_like(v)
    return jnp.concatenate([v, z] if a == 0 else [z, v], axis=0)


def _gdn_fwd(q, k, v, gb, gbt, nh):
    s = q.shape[0]
    w = nh * HEAD
    npair = s // PAIR

    def body(q_ref, k_ref, v_ref, gb_ref, gbt_ref, o_ref, st_ref, s_ref):
        @pl.when(pl.program_id(0) == 0)
        def _():
            s_ref[...] = jnp.zeros_like(s_ref)

        heads = range(nh)
        hs = [slice(HEAD * h, HEAD * (h + 1)) for h in heads]
        gbv = gb_ref[...]
        q, k, v = [q_ref[:, s_] for s_ in hs], [k_ref[:, s_] for s_ in hs], [v_ref[:, s_] for s_ in hs]
        gcol = [_col(gbv, A_LANE + h) for h in heads]
        bcol = [_col(gbv, B_LANE + h) for h in heads]
        grow = [gbt_ref[A_LANE + h:A_LANE + h + 1, :] for h in heads]
        _, _, _, t, p, _, _ = _pair_common(q, k, gcol, grow, bcol)
        eg = [jnp.exp(gc) for gc in gcol]
        qg = [x * e for x, e in zip(q, eg)]
        kg = [x * e for x, e in zip(k, eg)]
        outs = []
        for a in range(2):
            sl = slice(CHUNK * a, CHUNK * (a + 1))
            st = [s_ref[h] for h in heads]
            for h in heads:
                st_ref[a, h] = st[h]
            r = [v[h][sl] - _dot(kg[h][sl], st[h]) for h in heads]
            vn = [_dot(t[h][sl], _ext(bcol[h][sl] * r[h], a)) for h in heads]
            outs.append([_dot(qg[h][sl], st[h]) + _dot(p[h][sl], _ext(vn[h], a)) for h in heads])
            gl = [_col(gr, CHUNK * (a + 1) - 1) for gr in grow]
            kd = [k[h][sl] * jnp.exp(gl[h] - gcol[h][sl]) for h in heads]
            upd = [_dot_tn(kd[h], vn[h]) for h in heads]
            for h in heads:
                s_ref[h] = jnp.exp(gl[h]) * st[h] + upd[h]
        for h in heads:
            o_ref[:, hs[h]] = jnp.concatenate([outs[0][h], outs[1][h]], axis=0)

    return _pcall(
        body, name="gdn_fwd", grid=(npair,),
        in_specs=[_rows(PAIR, w), _rows(PAIR, w), _rows(PAIR, w), _rows(PAIR, LANE),
                  pl.BlockSpec((LANE, PAIR), lambda i: (0, i))],
        out_specs=[_rows(PAIR, w), pl.BlockSpec((2, nh, HEAD, HEAD), lambda i: (i, 0, 0, 0))],
        out_shape=[SDS((s, w), F32), SDS((2 * npair, nh, HEAD, HEAD), F32)],
        scratch=[pltpu.VMEM((nh, HEAD, HEAD), F32)])(q, k, v, gb, gbt)


def _gdn_bwd(q, k, v, gb, gbt, states, do, nh, rider):
    s = q.shape[0]
    w = nh * HEAD
    npair = s // PAIR
    rev = lambda i: (npair - 1 - i, 0)

    def body(q_ref, k_ref, v_ref, gb_ref, gbt_ref, st_ref, do_ref, dq_ref, dk_ref, dv_ref, dgb_ref,
             ds_ref):
        @pl.when(pl.program_id(0) == 0)
        def _():
            ds_ref[...] = jnp.zeros_like(ds_ref)

        lane = _lane_iota((PAIR, LANE))
        row = lax.broadcasted_iota(jnp.int32, (CHUNK, 1), 0)
        heads = range(nh)
        hs = [slice(HEAD * h, HEAD * (h + 1)) for h in heads]
        gbv = gb_ref[...]
        q, k, v = [q_ref[:, s_] for s_ in hs], [k_ref[:, s_] for s_ in hs], [v_ref[:, s_] for s_ in hs]
        do = [do_ref[:, s_] for s_ in hs]
        gcol = [_col(gbv, A_LANE + h) for h in heads]
        bcol = [_col(gbv, B_LANE + h) for h in heads]
        grow = [gbt_ref[A_LANE + h:A_LANE + h + 1, :] for h in heads]
        dec, kk, amat, t, p, tril, strict = _pair_common(q, k, gcol, grow, bcol)
        tt, pt = [x.T for x in t], [x.T for x in p]
        eg = [jnp.exp(gc) for gc in gcol]
        qg = [x * e for x, e in zip(q, eg)]
        kg = [x * e for x, e in zip(k, eg)]
        sums = lambda x: jnp.sum(x, axis=-1, keepdims=True)
        rs, vns = [None, None], [None, None]
        for a in range(2):
            sl = slice(CHUNK * a, CHUNK * (a + 1))
            rs[a] = [v[h][sl] - _dot(kg[h][sl], st_ref[a, h]) for h in heads]
            vns[a] = [_dot(t[h][sl], _ext(bcol[h][sl] * rs[a][h], a)) for h in heads]
        dsn = [ds_ref[h] for h in heads]
        dqs, dks, dvs, dgcs, dbs, drbs = ([None, None] for _ in range(6))
        for a in (1, 0):
            sl = slice(CHUNK * a, CHUNK * (a + 1))
            st = [st_ref[a, h] for h in heads]
            gl = [_col(gr, CHUNK * (a + 1) - 1) for gr in grow]
            egl = [jnp.exp(x) for x in gl]
            dk_dec = [jnp.exp(gl[h] - gcol[h][sl]) for h in heads]
            kd = [k[h][sl] * dk_dec[h] for h in heads]
            d_vn = [_dot(pt[h][sl], _ext(do[h][sl], a)) + _dot(kd[h], dsn[h]) for h in heads]
            d_qg = [_dot_nt(do[h][sl], st[h]) for h in heads]
            d_rb = [_dot(tt[h][sl], _ext(d_vn[h], a)) for h in heads]
            d_r = [bcol[h][sl] * d_rb[h] for h in heads]
            d_kg = [-_dot_nt(d_r[h], st[h]) for h in heads]
            d_kd = [_dot_nt(vns[a][h], dsn[h]) for h in heads]
            dsn_new = [_dot_tn(qg[h][sl], do[h][sl]) - _dot_tn(kg[h][sl], d_r[h]) for h in heads]
            dbs[a] = [sums(d_rb[h] * rs[a][h]) for h in heads]
            dgl = [egl[h] * jnp.sum(dsn[h] * st[h], keepdims=True) + jnp.sum(d_kd[h] * kd[h], keepdims=True)
                   for h in heads]
            dgcs[a] = [sums(d_qg[h] * qg[h][sl]) + sums(d_kg[h] * kg[h][sl]) - sums(d_kd[h] * kd[h])
                       + jnp.where(row == CHUNK - 1, dgl[h], 0.0) for h in heads]
            dqs[a] = [d_qg[h] * eg[h][sl] for h in heads]
            dks[a] = [d_kg[h] * eg[h][sl] + d_kd[h] * dk_dec[h] for h in heads]
            dvs[a] = d_r
            drbs[a] = d_rb
            dsn = [dsn_new[h] + egl[h] * dsn[h] for h in heads]
        for h in heads:
            ds_ref[h] = dsn[h]
        cat = lambda xs, h: jnp.concatenate([xs[0][h], xs[1][h]], axis=0)
        vn = [cat(vns, h) for h in heads]
        d_rb = [cat(drbs, h) for h in heads]
        dp = [jnp.where(tril, _dot_nt(do[h], vn[h]), 0.0) for h in heads]
        dam = [jnp.where(strict, -_dot_nt(d_rb[h], vn[h]), 0.0) for h in heads]
        g_p = [dp[h] * dec[h] for h in heads]
        g_a = [dam[h] * dec[h] for h in heads]
        gbk = [bcol[h] * g_a[h] for h in heads]
        dq2 = [_dot(g_p[h], k[h]) for h in heads]
        dk2 = [_dot_tn(g_p[h], q[h]) + _dot(gbk[h], k[h]) + _dot_tn(gbk[h], k[h]) for h in heads]
        dgb = jnp.zeros((PAIR, LANE), F32)
        for h in heads:
            dq_ref[:, hs[h]] = cat(dqs, h) + dq2[h]
            dk_ref[:, hs[h]] = cat(dks, h) + dk2[h]
            dv_ref[:, hs[h]] = cat(dvs, h)
            dbeta = cat(dbs, h) + sums(g_a[h] * kk[h])
            mm = dp[h] * p[h] + dam[h] * amat[h]
            dgc = cat(dgcs, h) + sums(mm) - sums(mm.T)
            dgb = dgb + jnp.where(lane == A_LANE + h, dgc, 0.0) + jnp.where(lane == B_LANE + h, dbeta, 0.0)
        dgb_ref[...] = dgb

    return _pcall_riding(
        body, rider, name="gdn_bwd", grid=(npair,),
        in_specs=[pl.BlockSpec((PAIR, w), rev), pl.BlockSpec((PAIR, w), rev), pl.BlockSpec((PAIR, w), rev),
                  pl.BlockSpec((PAIR, LANE), rev),
                  pl.BlockSpec((LANE, PAIR), lambda i: (0, npair - 1 - i)),
                  pl.BlockSpec((2, nh, HEAD, HEAD), lambda i: (npair - 1 - i, 0, 0, 0)),
                  pl.BlockSpec((PAIR, w), rev)],
        out_specs=[pl.BlockSpec((PAIR, w), rev), pl.BlockSpec((PAIR, w), rev), pl.BlockSpec((PAIR, w), rev),
                   pl.BlockSpec((PAIR, LANE), rev)],
        out_shape=[SDS((s, w), F32), SDS((s, w), F32), SDS((s, w), F32), SDS((s, LANE), F32)],
        scratch=[pltpu.VMEM((nh, HEAD, HEAD), F32)], args=(q, k, v, gb, gbt, states, do))


def _mla_norm(proj, qw, kvw, col_q, col_kv):
    s = proj.shape[0]
    lr = qw.shape[1]
    ts = _tile(s, 512, 8)

    def body(cq_ref, ckv_ref, qw_ref, kvw_ref, oq_ref, okv_ref):
        for x_ref, w_ref, o_ref in ((cq_ref, qw_ref, oq_ref), (ckv_ref, kvw_ref, okv_ref)):
            xv = x_ref[...]
            r = lax.rsqrt(jnp.mean(xv * xv, axis=-1, keepdims=True) + EPS)
            o_ref[...] = (xv * r * w_ref[...]).astype(BF16)

    return _pcall(body, name="mla_norm", grid=(s // ts,),
                  in_specs=[_rows(ts, lr, col_q), _rows(ts, lr, col_kv), _full((1, lr)), _full((1, lr))],
                  out_specs=[_rows(ts, lr), _rows(ts, lr)],
                  out_shape=[SDS((s, lr), BF16), SDS((s, lr), BF16)])(proj, proj, qw, kvw)


def _mla_norm_bwd(proj, qw, kvw, dq, dkv, dproj, col_q, col_kv):
    s = proj.shape[0]
    lr = qw.shape[1]
    ts = _tile(s, 512, 8)

    assert col_kv == col_q + 1 and col_q % 2 == 0

    def body(cq_ref, ckv_ref, qw_ref, kvw_ref, dq_ref, dkv_ref, _, o_ref, dqw_ref, dkvw_ref):
        @pl.when(pl.program_id(0) == 0)
        def _():
            dqw_ref[...] = jnp.zeros_like(dqw_ref)
            dkvw_ref[...] = jnp.zeros_like(dkvw_ref)

        for k, (x_ref, w_ref, d_ref, dw_ref) in enumerate(((cq_ref, qw_ref, dq_ref, dqw_ref),
                                                           (ckv_ref, kvw_ref, dkv_ref, dkvw_ref))):
            xv, dh = x_ref[...], d_ref[...]
            r = lax.rsqrt(jnp.mean(xv * xv, axis=-1, keepdims=True) + EPS)
            xh = xv * r
            dw_ref[...] += jnp.sum(dh * xh, axis=0, keepdims=True)
            dxh = dh * w_ref[...]
            o_ref[:, lr * k:lr * (k + 1)] = (
                r * (dxh - xh * jnp.mean(dxh * xh, axis=-1, keepdims=True))).astype(BF16)

    return _pcall(body, name="mla_norm_bwd", grid=(s // ts,),
                  in_specs=[_rows(ts, lr, col_q), _rows(ts, lr, col_kv), _full((1, lr)), _full((1, lr)),
                            _rows(ts, lr), _rows(ts, lr), _ANY],
                  out_specs=[_rows(ts, 2 * lr, col_q // 2), _full((1, lr)), _full((1, lr))],
                  out_shape=[SDS(dproj.shape, BF16), SDS((1, lr), F32), SDS((1, lr), F32)],
                  aliases={6: 0})(proj, proj, qw, kvw, dq, dkv, dproj)


def _rope_tables(pos, invf, sgn):
    ang = pos * invf
    return jnp.cos(ang), jnp.sin(ang) * sgn


def _swap_halves_lanes(y):
    lane = _lane_iota(y.shape)
    return jnp.where(lane < ROPE // 2, pltpu.roll(y, LANE - ROPE // 2, 1), pltpu.roll(y, ROPE // 2, 1))


def _rope_consts():
    half = ROPE // 2
    inv = ROPE_THETA ** (-jnp.arange(half, dtype=F32) / half)
    invf = jnp.concatenate([inv, inv, jnp.zeros((LANE - ROPE,), F32)])[None, :]
    sgn = jnp.concatenate([-jnp.ones((half,), F32), jnp.ones((half,), F32),
                           jnp.zeros((LANE - ROPE,), F32)])[None, :]
    return invf, sgn


def _mla_rope(qraw, kvraw, proj, pos, nh, misc_col):
    s = qraw.shape[0]
    ts = _tile(s, 256, 8)
    wq = nh * 2 * HEAD
    invf, sgn = _rope_consts()

    def body(q_ref, kv_ref, misc_ref, pos_ref, if_ref, sg_ref, qc_ref, kc_ref, v_ref):
        c, sn = _rope_tables(pos_ref[...], if_ref[...], sg_ref[...])
        lane = _lane_iota(c.shape)
        rot = lambda xb: xb * c + _swap_halves_lanes(xb) * sn
        qs = SM_SCALE * LOG2E
        krot = jnp.where(lane < ROPE, rot(misc_ref[...]), 0.0).astype(BF16)
        for h in range(nh):
            b0 = 2 * HEAD * h
            qc_ref[:, b0:b0 + HEAD] = (q_ref[:, b0:b0 + HEAD].astype(F32) * qs).astype(BF16)
            qc_ref[:, b0 + HEAD:b0 + 2 * HEAD] = (
                rot(q_ref[:, b0 + HEAD:b0 + 2 * HEAD].astype(F32)) * qs).astype(BF16)
            kc_ref[:, b0:b0 + HEAD] = kv_ref[:, b0:b0 + HEAD].astype(BF16)
            kc_ref[:, b0 + HEAD:b0 + 2 * HEAD] = krot
        v_ref[...] = kv_ref[:, wq:].astype(BF16)

    return _pcall(body, name="mla_rope", grid=(s // ts,),
                  in_specs=[_rows(ts, wq), _rows(ts, wq + nh * HEAD), _rows(ts, LANE, misc_col),
                            _rows(ts, 1), _full((1, LANE)), _full((1, LANE))],
                  out_specs=[_rows(ts, wq), _rows(ts, wq), _rows(ts, nh * HEAD)],
                  out_shape=[SDS((s, wq), BF16), SDS((s, wq), BF16), SDS((s, nh * HEAD), BF16)])(
                      qraw, kvraw, proj, pos, invf, sgn)


def _mla_rope_bwd(dqc, dkc, dv, pos, nh):
    s = dqc.shape[0]
    ts = _tile(s, 256, 8)
    wq = nh * 2 * HEAD
    invf, sgn = _rope_consts()

    def body(dq_ref, dk_ref, dv_ref, pos_ref, if_ref, sg_ref, oq_ref, okv_ref, okr_ref):
        c, sn = _rope_tables(pos_ref[...], if_ref[...], sg_ref[...])
        lane = _lane_iota(c.shape)
        unrot = lambda d: d * c + _swap_halves_lanes(d * sn)
        dkr = jnp.zeros(c.shape, F32)
        for h in range(nh):
            b0 = 2 * HEAD * h
            oq_ref[:, b0:b0 + HEAD] = (dq_ref[:, b0:b0 + HEAD] * SM_SCALE).astype(BF16)
            oq_ref[:, b0 + HEAD:b0 + 2 * HEAD] = (
                unrot(dq_ref[:, b0 + HEAD:b0 + 2 * HEAD]) * SM_SCALE).astype(BF16)
            okv_ref[:, b0:b0 + HEAD] = (dk_ref[:, b0:b0 + HEAD] * LN2).astype(BF16)
            okv_ref[:, b0 + HEAD:b0 + 2 * HEAD] = jnp.zeros((ts, HEAD), BF16)
            dkr = dkr + dk_ref[:, b0 + HEAD:b0 + 2 * HEAD]
        okv_ref[:, wq:] = dv_ref[...].astype(BF16)
        okr_ref[...] = jnp.where(lane < ROPE, unrot(jnp.where(lane < ROPE, dkr * LN2, 0.0)), 0.0)

    return _pcall(body, name="mla_rope_bwd", grid=(s // ts,),
                  in_specs=[_rows(ts, wq), _rows(ts, wq), _rows(ts, nh * HEAD), _rows(ts, 1),
                            _full((1, LANE)), _full((1, LANE))],
                  out_specs=[_rows(ts, wq), _rows(ts, wq + nh * HEAD), _rows(ts, LANE)],
                  out_shape=[SDS((s, wq), BF16), SDS((s, wq + nh * HEAD), BF16), SDS((s, LANE), F32)])(
                      dqc, dkc, dv, pos, invf, sgn)


def _causal_mask(blk):
    i = lax.broadcasted_iota(jnp.int32, (blk, blk), 0)
    j = lax.broadcasted_iota(jnp.int32, (blk, blk), 1)
    return j <= i


MLA_HP = 2


def _pair_pack(a, b):
    return jnp.where(_lane_iota(a.shape) < LANE // 2, a, b)


def _pair_unpack(x, e):
    lane = _lane_iota(x.shape)
    keep = (lane < LANE // 2) if e == 0 else (lane >= LANE // 2)
    return jnp.where(keep, x, pltpu.roll(x, LANE // 2, 1))


def _mla_fwd(qc, kc, v, nh, rider):
    s = qc.shape[0]
    blk = _tile(s, MLA_BLOCK)
    nb = s // blk
    rep = blk // LANE
    hp = MLA_HP
    assert nh % hp == 0
    once = pl.Buffered(1)
    r_in, r_out = len(rider.arrays), len(rider.out_shapes)

    def body(*refs):
        q_ref, k_ref, v_ref = refs[:3]
        o_ref, lse_ref = refs[3 + r_in:5 + r_in]
        m_sc, l_sc, acc = refs[5 + r_in + r_out:8 + r_in + r_out]
        r_refs = (refs[3:3 + r_in], refs[5 + r_in:5 + r_in + r_out], refs[8 + r_in + r_out:])
        i = pl.program_id(1)
        grid_step = pl.program_id(0) * nb + i
        _ride_begin(rider, r_refs, grid_step)
        m_sc[...] = jnp.full_like(m_sc, -1e30)
        l_sc[...] = jnp.zeros_like(l_sc)
        acc[...] = jnp.zeros_like(acc)

        def step(j, masked):
            rows = pl.ds(pl.multiple_of(j * blk, blk), blk)
            es = range(hp)
            sc = [_dot_nt(q_ref[:, 2 * HEAD * e:2 * HEAD * (e + 1)], k_ref[rows, 2 * HEAD * e:2 * HEAD * (e + 1)])
                  for e in es]
            if masked:
                sc = [jnp.where(_causal_mask(blk), x, -1e30) for x in sc]
            m_prev = [m_sc[e] for e in es]
            m_new = [jnp.maximum(m_prev[e], jnp.max(sc[e], axis=-1, keepdims=True)) for e in es]
            p = [jnp.exp2(sc[e] - jnp.tile(m_new[e], (1, rep))) for e in es]
            alpha = [jnp.exp2(m_prev[e] - m_new[e]) for e in es]
            pv = [_dot(p[e], v_ref[rows, HEAD * e:HEAD * (e + 1)]) for e in es]
            for e in es:
                l_sc[e] = alpha[e] * l_sc[e] + jnp.sum(p[e], axis=-1, keepdims=True)
                acc[e] = alpha[e] * acc[e] + pv[e]
                m_sc[e] = m_new[e]

        def loop_body(j, carry):
            step(j, False)
            return carry

        lax.fori_loop(0, i, loop_body, 0)
        step(i, True)
        for e in range(hp):
            o_ref[:, HEAD * e:HEAD * (e + 1)] = acc[e] / l_sc[e]
        lse = [m_sc[e] + jnp.log(l_sc[e]) * LOG2E for e in range(hp)]
        lse_ref[...] = _pair_pack(lse[0], lse[1])
        _ride_end(rider, r_refs, grid_step, (nh // hp) * nb)

    outs = _pcall(
        body, name="mla_fwd", grid=(nh // hp, nb),
        in_specs=[pl.BlockSpec((blk, hp * 2 * HEAD), lambda g, i: (i, g)),
                  pl.BlockSpec((s, hp * 2 * HEAD), lambda g, i: (0, g), pipeline_mode=once),
                  pl.BlockSpec((s, hp * HEAD), lambda g, i: (0, g), pipeline_mode=once)] + [_ANY] * r_in,
        out_specs=[pl.BlockSpec((blk, hp * HEAD), lambda g, i: (i, g)),
                   pl.BlockSpec((None, blk, LANE), lambda g, i: (g, i, 0))] + [_ANY] * r_out,
        out_shape=[SDS((s, nh * HEAD), F32), SDS((nh // hp, s, LANE), F32)] + rider.out_shapes,
        scratch=[pltpu.VMEM((hp, blk, LANE), F32), pltpu.VMEM((hp, blk, LANE), F32),
                 pltpu.VMEM((hp, blk, HEAD), F32)] + rider.scratch)(qc, kc, v, *rider.arrays)
    return outs[0], outs[1], outs[2:]


def _mla_bwd(qc, kc, v, do, lse, delta, nh, rider):
    s = qc.shape[0]
    blk = _tile(s, MLA_BLOCK)
    nb = s // blk
    rep = blk // LANE
    hp = MLA_HP
    once = pl.Buffered(1)
    r_in, r_out = len(rider.arrays), len(rider.out_shapes)
    qs = [slice(2 * HEAD * e, 2 * HEAD * (e + 1)) for e in range(hp)]
    vs = [slice(HEAD * e, HEAD * (e + 1)) for e in range(hp)]

    def body(*refs):
        q_ref, do_ref, lse_ref, dl_ref, k_ref, v_ref = refs[:6]
        dq_ref, dk_ref, dv_ref = refs[6 + r_in:9 + r_in]
        dk_acc, dv_acc = refs[9 + r_in + r_out:11 + r_in + r_out]
        r_refs = (refs[6:6 + r_in], refs[9 + r_in:9 + r_in + r_out], refs[11 + r_in + r_out:])
        j = pl.program_id(1)
        grid_step = pl.program_id(0) * nb + j
        _ride_begin(rider, r_refs, grid_step)

        @pl.when(j == 0)
        def _():
            dq_ref[...] = jnp.zeros_like(dq_ref)

        dk_acc[...] = jnp.zeros_like(dk_acc)
        dv_acc[...] = jnp.zeros_like(dv_acc)
        es = range(hp)
        kj = [k_ref[:, qs[e]] for e in es]
        vj = [v_ref[:, vs[e]] for e in es]

        def step(i, masked):
            rows = pl.ds(pl.multiple_of(i * blk, blk), blk)
            qi = [q_ref[rows, qs[e]] for e in es]
            doi = [do_ref[rows, vs[e]] for e in es]
            lse, dl = lse_ref[rows, :], dl_ref[rows, :]
            sc = [_dot_nt(qi[e], kj[e]) for e in es]
            dp = [_dot_nt(doi[e], vj[e]) for e in es]
            if masked:
                sc = [jnp.where(_causal_mask(blk), x, -1e30) for x in sc]
            p = [jnp.exp2(sc[e] - jnp.tile(_pair_unpack(lse, e), (1, rep))) for e in es]
            ds = [p[e] * (dp[e] - jnp.tile(_pair_unpack(dl, e), (1, rep))) for e in es]
            dv = [_dot_tn(p[e], doi[e]) for e in es]
            dk = [_dot_tn(ds[e], qi[e]) for e in es]
            dq = [_dot(ds[e], kj[e]) for e in es]
            for e in es:
                dv_acc[:, vs[e]] += dv[e]
                dk_acc[:, qs[e]] += dk[e]
                dq_ref[rows, qs[e]] += dq[e]

        def loop_body(i, carry):
            step(i, False)
            return carry

        step(j, True)
        lax.fori_loop(j + 1, nb, loop_body, 0)
        dk_ref[...] = dk_acc[...]
        dv_ref[...] = dv_acc[...]
        _ride_end(rider, r_refs, grid_step, (nh // hp) * nb)

    outs = _pcall(
        body, name="mla_bwd", grid=(nh // hp, nb),
        in_specs=[pl.BlockSpec((s, hp * 2 * HEAD), lambda g, j: (0, g), pipeline_mode=once),
                  pl.BlockSpec((s, hp * HEAD), lambda g, j: (0, g), pipeline_mode=once),
                  pl.BlockSpec((None, s, LANE), lambda g, j: (g, 0, 0), pipeline_mode=once),
                  pl.BlockSpec((None, s, LANE), lambda g, j: (g, 0, 0), pipeline_mode=once),
                  pl.BlockSpec((blk, hp * 2 * HEAD), lambda g, j: (j, g)),
                  pl.BlockSpec((blk, hp * HEAD), lambda g, j: (j, g))] + [_ANY] * r_in,
        out_specs=[pl.BlockSpec((s, hp * 2 * HEAD), lambda g, j: (0, g), pipeline_mode=once),
                   pl.BlockSpec((blk, hp * 2 * HEAD), lambda g, j: (j, g)),
                   pl.BlockSpec((blk, hp * HEAD), lambda g, j: (j, g))] + [_ANY] * r_out,
        out_shape=[SDS((s, nh * 2 * HEAD), F32), SDS((s, nh * 2 * HEAD), F32),
                   SDS((s, nh * HEAD), F32)] + rider.out_shapes,
        scratch=[pltpu.VMEM((blk, hp * 2 * HEAD), F32), pltpu.VMEM((blk, hp * HEAD), F32)] + rider.scratch,
        vmem=VMEM_LIMIT_WIDE)(qc, do, lse, delta, kc, v, *rider.arrays)
    return outs[0], outs[1], outs[2], outs[3:]


def _mix_fwd(og, proj, om, gw, mw, nh, z_col):
    s = og.shape[0]
    w = nh * HEAD
    ts = _tile(s, 256, 8)

    def body(og_ref, z_ref, om_ref, gw_ref, mw_ref, o_ref):
        for h in range(nh):
            hs = slice(HEAD * h, HEAD * (h + 1))
            a = og_ref[:, hs]
            r = lax.rsqrt(jnp.mean(a * a, axis=-1, keepdims=True) + EPS)
            o_ref[:, hs] = (a * r * gw_ref[...] * _silu(z_ref[:, hs])).astype(BF16)
            b = om_ref[:, hs]
            r = lax.rsqrt(jnp.mean(b * b, axis=-1, keepdims=True) + EPS)
            o_ref[:, w + HEAD * h:w + HEAD * (h + 1)] = (b * r * mw_ref[...]).astype(BF16)

    return _pcall(body, name="mix_fwd", grid=(s // ts,),
                  in_specs=[_rows(ts, w), _rows(ts, w, z_col), _rows(ts, w), _full((1, HEAD)),
                            _full((1, HEAD))],
                  out_specs=_rows(ts, 2 * w), out_shape=SDS((s, 2 * w), BF16))(og, proj, om, gw, mw)


def _mix_bwd(dmix, og, proj, om, gw, mw, nh, z_col):
    s = og.shape[0]
    w = nh * HEAD
    ts = _tile(s, 256, 8)

    def body(d_ref, og_ref, z_ref, om_ref, gw_ref, mw_ref, dog_ref, dz_ref, dom_ref, dgw_ref, dmw_ref,
             dl_ref):
        @pl.when(pl.program_id(0) == 0)
        def _():
            dgw_ref[...] = jnp.zeros_like(dgw_ref)
            dmw_ref[...] = jnp.zeros_like(dmw_ref)

        dgw = jnp.zeros((1, HEAD), F32)
        dmw = jnp.zeros((1, HEAD), F32)
        deltas = []
        for h in range(nh):
            hs = slice(HEAD * h, HEAD * (h + 1))
            a, z, dy = og_ref[:, hs], z_ref[:, hs], d_ref[:, hs]
            r = lax.rsqrt(jnp.mean(a * a, axis=-1, keepdims=True) + EPS)
            ah = a * r
            sz = _silu(z)
            dz_ref[:, hs] = (dy * (ah * gw_ref[...]) * _dsilu(z)).astype(BF16)
            dn = dy * sz
            dgw = dgw + jnp.sum(dn * ah, axis=0, keepdims=True)
            dah = dn * gw_ref[...]
            dog_ref[:, hs] = r * (dah - ah * jnp.mean(dah * ah, axis=-1, keepdims=True))
            b, dyb = om_ref[:, hs], d_ref[:, w + HEAD * h:w + HEAD * (h + 1)]
            r = lax.rsqrt(jnp.mean(b * b, axis=-1, keepdims=True) + EPS)
            bh = b * r
            dmw = dmw + jnp.sum(dyb * bh, axis=0, keepdims=True)
            dbh = dyb * mw_ref[...]
            dom = r * (dbh - bh * jnp.mean(dbh * bh, axis=-1, keepdims=True))
            dom_ref[:, hs] = dom.astype(BF16)
            deltas.append(jnp.broadcast_to(jnp.sum(dom * b, axis=-1, keepdims=True), (ts, LANE)))
        for g in range(nh // MLA_HP):
            dl_ref[g] = _pair_pack(deltas[2 * g], deltas[2 * g + 1])
        dgw_ref[...] += dgw
        dmw_ref[...] += dmw

    return _pcall(body, name="mix_bwd", grid=(s // ts,),
                  in_specs=[_rows(ts, 2 * w), _rows(ts, w), _rows(ts, w, z_col), _rows(ts, w),
                            _full((1, HEAD)), _full((1, HEAD))],
                  out_specs=[_rows(ts, w), _rows(ts, w, z_col), _rows(ts, w), _full((1, HEAD)), _full((1, HEAD)),
                             pl.BlockSpec((nh // MLA_HP, ts, LANE), lambda i: (0, i, 0))],
                  out_shape=[SDS((s, w), F32), SDS((s, proj.shape[1]), BF16), SDS((s, w), BF16),
                             SDS((1, HEAD), F32), SDS((1, HEAD), F32),
                             SDS((nh // MLA_HP, s, LANE), F32)])(dmix, og, proj, om, gw, mw)


def _swiglu_fwd(h2, wg, wu):
    m, kdim = h2.shape
    tn = wg.shape[2]
    n = 4 * tn
    tm, tk = _tile(m, 512), _tile(kdim, 2048)
    nk = kdim // tk

    def body(a_ref, g_ref, u_ref, act_ref, go_ref, uo_ref, gacc, uacc):
        k = pl.program_id(2)

        @pl.when(k == 0)
        def _():
            gacc[...] = jnp.zeros_like(gacc)
            uacc[...] = jnp.zeros_like(uacc)

        a = a_ref[...]
        gacc[...] += _dot(a, g_ref[...])
        uacc[...] += _dot(a, u_ref[...])

        @pl.when(k == nk - 1)
        def _():
            g, u = gacc[...], uacc[...]
            act_ref[...] = (_silu(g) * u).astype(BF16)
            go_ref[...] = g.astype(BF16)
            uo_ref[...] = u.astype(BF16)

    a_spec = pl.BlockSpec((tm, tk), lambda i, j, k: (i, k))
    b_spec = pl.BlockSpec((None, tk, tn), lambda i, j, k: (j, k, 0))
    o_spec = pl.BlockSpec((tm, tn), lambda i, j, k: (i, j))
    return _pcall(body, name="swiglu_fwd", grid=(m // tm, n // tn, nk),
                  in_specs=[a_spec, b_spec, b_spec], out_specs=[o_spec] * 3,
                  out_shape=[SDS((m, n), BF16)] * 3,
                  scratch=[pltpu.VMEM((tm, tn), F32), pltpu.VMEM((tm, tn), F32)])(h2, wg, wu)


def _swiglu_bwd(dx3, wd, g, u):
    m, kdim = dx3.shape
    n = wd.shape[0]
    tm, tn = _tile(m, 1024), _tile(n, 512)
    parts = 2 if tm % 16 == 0 else 1
    th = tm // parts

    def body(a_ref, b_ref, g_ref, u_ref, dg_ref, du_ref):
        b = b_ref[...]
        rows = [slice(th * c, th * (c + 1)) for c in range(parts)]
        da = [_dot_nt(a_ref[rs, :], b) for rs in rows]
        for rs, d in zip(rows, da):
            gv, uv = g_ref[rs, :].astype(F32), u_ref[rs, :].astype(F32)
            dg_ref[rs, :] = (d * uv * _dsilu(gv)).astype(BF16)
            du_ref[rs, :] = (d * _silu(gv)).astype(BF16)

    a_spec = pl.BlockSpec((tm, kdim), lambda i, j: (i, 0))
    b_spec = pl.BlockSpec((tn, kdim), lambda i, j: (j, 0))
    o_spec = pl.BlockSpec((tm, tn), lambda i, j: (i, j))
    return _pcall(body, name="swiglu_bwd", grid=(m // tm, n // tn),
                  in_specs=[a_spec, b_spec, o_spec, o_spec], out_specs=[o_spec] * 2,
                  out_shape=[SDS((m, n), BF16)] * 2)(dx3, wd, g, u)


def _sum_pair(g, recv, place, name):
    _, _, rh, c = g.shape
    tr = _tile(rh, 256, 16)

    def body(pl_ref, g_ref, r_ref, o16_ref, own_ref):
        sm = g_ref[...].astype(F32) + r_ref[...].astype(F32)
        o16_ref[...] = sm.astype(BF16)

        @pl.when(pl.program_id(1) == pl_ref[1])
        def _():
            own_ref[...] = sm

    grid_spec = pltpu.PrefetchScalarGridSpec(
        num_scalar_prefetch=1, grid=(rh // tr, 4),
        in_specs=[pl.BlockSpec((None, None, tr, c), lambda i, t, p: (t, p[0], i, 0)),
                  pl.BlockSpec((None, tr, c), lambda i, t, p: (t, i, 0))],
        out_specs=[pl.BlockSpec((None, tr, c), lambda i, t, p: (t, i, 0)),
                   pl.BlockSpec((tr, c), lambda i, t, p: (i, 0))])
    return pl.pallas_call(
        body, name=name, grid_spec=grid_spec,
        out_shape=[SDS((4, rh, c), BF16), SDS((rh, c), F32)],
        compiler_params=pltpu.CompilerParams(dimension_semantics=("arbitrary",) * 2,
                                             vmem_limit_bytes=VMEM_LIMIT))(place, g, recv)


def _sum_chips(own, recv, name):
    rh, c = own.shape
    tr = _tile(rh, 256, 16)

    def body(o_ref, r_ref, out_ref):
        acc = o_ref[...]
        for j in range(3):
            acc = acc + r_ref[j].astype(F32)
        out_ref[...] = acc

    return _pcall(body, name=name, grid=(rh // tr,),
                  in_specs=[_rows(tr, c), pl.BlockSpec((3, tr, c), lambda i: (0, i, 0))],
                  out_specs=_rows(tr, c), out_shape=SDS(own.shape, F32))(own, recv)


def _adamw_update(wv, gv, mv, vv):
    mn = ADAM_B1 * mv + (1.0 - ADAM_B1) * gv
    vn = ADAM_B2 * vv + (1.0 - ADAM_B2) * (gv * gv)
    m_hat = mn / (1.0 - ADAM_B1 ** ADAM_STEP)
    v_hat = vn / (1.0 - ADAM_B2 ** ADAM_STEP)
    return -ADAM_LR * (m_hat / (jnp.sqrt(v_hat) + ADAM_EPS) + ADAM_WD * wv), mn, vn


def _adamw(w, g, m, v, name):
    r, c = w.shape
    tr = _tile(r, 256, 8)

    def body(w_ref, g_ref, m_ref, v_ref, d_ref, mo_ref, vo_ref):
        d_ref[...], mo_ref[...], vo_ref[...] = _adamw_update(w_ref[...], g_ref[...], m_ref[...], v_ref[...])

    spec = _rows(tr, c)
    return _pcall(body, name=name, grid=(r // tr,), in_specs=[spec] * 4, out_specs=[spec] * 3,
                  out_shape=[SDS(w.shape, F32)] * 3)(w, g, m, v)


def _adamw_halves(w, mine, theirs, m, v, place, name):
    r, c = w.shape
    rh = r // 2
    tr = _tile(rh, 256, 8)
    nt = rh // tr

    def body(p_ref, w_ref, a_ref, b_ref, m_ref, v_ref, g_ref, d_ref, mo_ref, vo_ref):
        gv = jnp.where(pl.program_id(0) // nt == p_ref[0], a_ref[...], b_ref[...])
        g_ref[...] = gv
        d_ref[...], mo_ref[...], vo_ref[...] = _adamw_update(w_ref[...], gv, m_ref[...], v_ref[...])

    full = pl.BlockSpec((tr, c), lambda i, p: (i, 0))
    half = pl.BlockSpec((tr, c), lambda i, p: (i % nt, 0))
    grid_spec = pltpu.PrefetchScalarGridSpec(num_scalar_prefetch=1, grid=(2 * nt,),
                                             in_specs=[full, half, half, full, full], out_specs=[full] * 4)
    return pl.pallas_call(
        body, name=name, grid_spec=grid_spec, out_shape=[SDS(w.shape, F32)] * 4,
        compiler_params=pltpu.CompilerParams(dimension_semantics=("arbitrary",),
                                             vmem_limit_bytes=VMEM_LIMIT))(place, w, mine, theirs, m, v)


def _place():
    x, y, c = lax.axis_index("x"), lax.axis_index("y"), lax.axis_index("c")
    chips = [(1 - x, y), (x, 1 - y), (1 - x, 1 - y)]
    return x, y, c, chips


_ANY = pl.BlockSpec(memory_space=pl.ANY)


def _remote(src, dst, sems, k, to):
    return pltpu.make_async_remote_copy(src_ref=src, dst_ref=dst, send_sem=sems[0].at[k], recv_sem=sems[1].at[k],
                                        device_id=to, device_id_type=MESH)


class _Gather:
    def __init__(self, shards):
        n = len(shards)
        self.arrays = list(shards)
        self.out_shapes = [SDS((4,) + a.shape, a.dtype) for a in shards]
        self.scratch = [pltpu.SemaphoreType.DMA((7 * n,)), pltpu.SemaphoreType.DMA((7 * n,))]

    def _plan(self, ins, outs, sems):
        x, y, c, chips = _place()
        own, sib = 2 * x + y, (x, y, 1 - c)
        plan = []
        for wi, (w, o) in enumerate(zip(ins, outs)):
            rh = w.shape[0] // 2
            mine, theirs = pl.ds(c * rh, rh), pl.ds((1 - c) * rh, rh)
            whole = _remote(w, o.at[own], sems, 7 * wi + 6, sib)
            ici, d2d, d2d_in = [], [], []
            for j, (tx, ty) in enumerate(chips):
                t = 2 * tx + ty
                ici.append(_remote(w.at[mine], o.at[own, mine], sems, 7 * wi + j, (tx, ty, c)))
                d2d.append(_remote(o.at[t, mine], o.at[t, mine], sems, 7 * wi + 3 + j, sib))
                d2d_in.append(_remote(o.at[t, theirs], o.at[t, theirs], sems, 7 * wi + 3 + j, sib))
            plan.append((whole, ici, d2d, d2d_in))
        return plan

    def begin(self, ins, outs, sems):
        for whole, ici, _, _ in self._plan(ins, outs, sems):
            whole.start()
            for cp in ici:
                cp.start()

    def middle(self, ins, outs, sems):
        for _, ici, d2d, _ in self._plan(ins, outs, sems):
            for cp_in, cp_on in zip(ici, d2d):
                cp_in.wait_recv()
                cp_on.start()

    def finish(self, ins, outs, sems):
        for whole, ici, d2d, d2d_in in self._plan(ins, outs, sems):
            for cp in d2d_in:
                cp.wait_recv()
            for cp in ici + d2d:
                cp.wait_send()
            whole.wait()


class _Swap:
    def __init__(self, grads):
        n = len(grads)
        self.arrays = list(grads)
        self.out_shapes = [SDS((4,) + g.shape[2:], g.dtype) for g in grads]
        self.scratch = [pltpu.SemaphoreType.DMA((4 * n,)), pltpu.SemaphoreType.DMA((4 * n,))]

    def _plan(self, ins, outs, sems):
        x, y, c, _ = _place()
        return [_remote(g.at[t, 1 - c], o.at[t], sems, 4 * wi + t, (x, y, 1 - c))
                for wi, (g, o) in enumerate(zip(ins, outs)) for t in range(4)]

    def begin(self, ins, outs, sems):
        for cp in self._plan(ins, outs, sems):
            cp.start()

    def middle(self, ins, outs, sems):
        pass

    def finish(self, ins, outs, sems):
        for cp in self._plan(ins, outs, sems):
            cp.wait()


class _Exchange:
    def __init__(self, pieces):
        n = len(pieces)
        self.arrays = list(pieces)
        self.out_shapes = [SDS((3,) + p.shape[1:], p.dtype) for p in pieces]
        self.scratch = [pltpu.SemaphoreType.DMA((3 * n,)), pltpu.SemaphoreType.DMA((3 * n,))]

    def _plan(self, ins, outs, sems):
        x, y, c, chips = _place()
        return [_remote(g.at[2 * tx + ty], o.at[j], sems, 3 * wi + j, (tx, ty, c))
                for wi, (g, o) in enumerate(zip(ins, outs)) for j, (tx, ty) in enumerate(chips)]

    def begin(self, ins, outs, sems):
        for cp in self._plan(ins, outs, sems):
            cp.start()

    def middle(self, ins, outs, sems):
        pass

    def finish(self, ins, outs, sems):
        for cp in self._plan(ins, outs, sems):
            cp.wait()


class _Share:
    def __init__(self, totals):
        n = len(totals)
        self.arrays = list(totals)
        self.out_shapes = [SDS(t.shape, t.dtype) for t in totals]
        self.scratch = [pltpu.SemaphoreType.DMA((n,)), pltpu.SemaphoreType.DMA((n,))]

    def _plan(self, ins, outs, sems):
        x, y, c, _ = _place()
        return [_remote(t, o, sems, wi, (x, y, 1 - c)) for wi, (t, o) in enumerate(zip(ins, outs))]

    def begin(self, ins, outs, sems):
        for cp in self._plan(ins, outs, sems):
            cp.start()

    def middle(self, ins, outs, sems):
        pass

    def finish(self, ins, outs, sems):
        for cp in self._plan(ins, outs, sems):
            cp.wait()


def _ride_begin(rider, r_refs, step):
    @pl.when(step == 0)
    def _():
        rider.begin(*r_refs)


def _ride_end(rider, r_refs, step, nsteps):
    @pl.when(step == min(3 * nsteps // 4, nsteps - 1))
    def _():
        rider.middle(*r_refs)

    @pl.when(step == nsteps - 1)
    def _():
        rider.finish(*r_refs)


def _comm(rider, name):
    n_in, n_out = len(rider.arrays), len(rider.out_shapes)

    def body(*refs):
        r_refs = (refs[:n_in], refs[n_in:n_in + n_out], refs[n_in + n_out:])
        rider.begin(*r_refs)
        rider.middle(*r_refs)
        rider.finish(*r_refs)

    return pl.pallas_call(body, name=name, out_shape=rider.out_shapes, in_specs=[_ANY] * n_in,
                          out_specs=[_ANY] * n_out, scratch_shapes=rider.scratch)(*rider.arrays)


def _small_allreduce(pk, name):
    r = pk.shape[0]
    rels = [(dx, dy, dc) for dx in (0, 1) for dy in (0, 1) for dc in (0, 1) if dx or dy or dc]

    def body(p_ref, o_ref, buf, send_sems, recv_sems):
        x, y, c, _ = _place()
        me = 4 * x + 2 * y + c
        buf[me] = p_ref[...]
        cps = []
        for k, (dx, dy, dc) in enumerate(rels):
            to = (1 - x if dx else x, 1 - y if dy else y, 1 - c if dc else c)
            cps.append(pltpu.make_async_remote_copy(src_ref=p_ref, dst_ref=buf.at[me], send_sem=send_sems.at[k],
                                                    recv_sem=recv_sems.at[k], device_id=to,
                                                    device_id_type=MESH))
        for cpy in cps:
            cpy.start()
        for cpy in cps:
            cpy.wait()
        acc = buf[0]
        for d in range(1, 8):
            acc = acc + buf[d]
        o_ref[...] = acc

    vm = pl.BlockSpec(memory_space=pltpu.VMEM)
    return pl.pallas_call(body, name=name, out_shape=SDS(pk.shape, F32), in_specs=[vm], out_specs=vm,
                          scratch_shapes=[pltpu.VMEM((8, r, LANE), F32), pltpu.SemaphoreType.DMA((7,)),
                                          pltpu.SemaphoreType.DMA((7,))])(pk)


ATTN_W = ("w_in", "w_uq", "w_ukv", "w_out")
FFN_W = ("w_gate", "w_up", "w_down")
BIG = ATTN_W + FFN_W


def _cols_from_chips(g):
    return jnp.concatenate([g[t] for t in range(4)], axis=1)


def _cols_to_chips(full):
    r, n = full.shape
    return full.reshape(r, 4, n // 4).transpose(1, 0, 2).reshape(4, 2, r // 2, n // 4)


def _rows_to_chips(full):
    n, c = full.shape
    return full.reshape(4, 2, n // 8, c)


def _permute_w_in(w, nh):
    d = w.shape[0]
    g = 4 * nh * HEAD
    lr = (w.shape[1] - g - 2 * nh - ROPE) // 2
    o = g + 2 * nh
    pad = jnp.zeros((d, LANE - ROPE - 8 - nh), w.dtype)
    pad8 = jnp.zeros((d, 8 - nh), w.dtype)
    return jnp.concatenate([w[:, :g], w[:, o:o + 2 * lr], w[:, o + 2 * lr:], w[:, g:g + nh], pad8,
                            w[:, g + nh:g + 2 * nh], pad, jnp.zeros((d, LANE), w.dtype)], axis=1)


def _unpermute_w_in(wp, nh, lr):
    g = 4 * nh * HEAD
    mc = g + 2 * lr
    return jnp.concatenate([wp[:, :g], wp[:, mc + B_LANE:mc + B_LANE + nh], wp[:, mc + A_LANE:mc + A_LANE + nh],
                            wp[:, g:g + 2 * lr], wp[:, mc:mc + ROPE]], axis=1)


def _permute_w_uq(w, nh):
    lr = w.shape[0]
    w3 = w.reshape(lr, nh, HEAD + ROPE)
    return jnp.concatenate([w3, jnp.zeros((lr, nh, HEAD - ROPE), w.dtype)], axis=2).reshape(lr, nh * 2 * HEAD)


def _unpermute_w_uq(wp, nh):
    lr = wp.shape[0]
    return wp.reshape(lr, nh, 2 * HEAD)[:, :, :HEAD + ROPE].reshape(lr, nh * (HEAD + ROPE))


def _permute_w_ukv(w, nh):
    lr = w.shape[0]
    w3 = w.reshape(lr, nh, 2 * HEAD)
    kp = jnp.concatenate([w3[:, :, :HEAD], jnp.zeros((lr, nh, HEAD), w.dtype)], axis=2)
    return jnp.concatenate([kp.reshape(lr, nh * 2 * HEAD), w3[:, :, HEAD:].reshape(lr, nh * HEAD)], axis=1)


def _unpermute_w_ukv(wp, nh):
    lr = wp.shape[0]
    kp = wp[:, :nh * 2 * HEAD].reshape(lr, nh, 2 * HEAD)[:, :, :HEAD]
    vp = wp[:, nh * 2 * HEAD:].reshape(lr, nh, HEAD)
    return jnp.concatenate([kp, vp], axis=2).reshape(lr, nh * 2 * HEAD)


def _sum_pairs(grads, recv, place, tag):
    sums = [_sum_pair(g, r, place, "sum_pair_%s%d" % (tag, k)) for k, (g, r) in enumerate(zip(grads, recv))]
    return [s[0] for s in sums], [s[1] for s in sums]


def _reduce_end(own, recv, tag):
    return [_sum_chips(o, r, "sum_chips_%s%d" % (tag, k)) for k, (o, r) in enumerate(zip(own, recv))]


def _step(x, pos, tgt, w_in, attn_shards, ffn_shards, small, place):
    nh = small["a_log"].shape[1]
    lr = small["q_norm_w"].shape[1]
    w = nh * HEAD
    z_col, col_q, col_kv = 3, 4 * w // lr, 4 * w // lr + 1
    misc_c = 4 * w + 2 * lr
    misc_col = misc_c // LANE
    assert (4 * w) % lr == 0 and small["kv_norm_w"].shape[1] == lr

    zl = jnp.zeros((1, LANE), F32)
    alog_l = zl.at[:, A_LANE:A_LANE + nh].set(small["a_log"])
    dtb_l = zl.at[:, A_LANE:A_LANE + nh].set(small["dt_bias"])
    conv_w = small["conv_w"]

    h1, (in4,) = _norm_fwd(x, small["attn_norm_w"], "norm1", rider=_Gather([w_in]))
    win_p = _permute_w_in(_cols_from_chips(in4), nh)
    proj, (uq4, ukv4, out4) = _mm([(h1, win_p)], name="proj_in", rider=_Gather(attn_shards))
    wuq_p = _permute_w_uq(_cols_from_chips(uq4), nh)
    wukv_p = _permute_w_ukv(_cols_from_chips(ukv4), nh)
    w_out = out4.reshape(-1, out4.shape[2])
    gq, gk, gv, gb, gbt = _gdn_prep(proj, conv_w, alog_l, dtb_l, nh, misc_col)
    o_gdn, states = _gdn_fwd(gq, gk, gv, gb, gbt, nh)
    cqn, ckvn = _mla_norm(proj, small["q_norm_w"], small["kv_norm_w"], col_q, col_kv)
    qraw = _mm([(cqn, wuq_p)], name="proj_uq", out_dtype=BF16)
    kvraw = _mm([(ckvn, wukv_p)], name="proj_ukv", out_dtype=BF16)
    qc, kc, vv = _mla_rope(qraw, kvraw, proj, pos, nh, misc_col)
    o_mla, lse, (wg4, wu4, wd4) = _mla_fwd(qc, kc, vv, nh, _Gather(ffn_shards))
    w_down = wd4.reshape(-1, wd4.shape[2])
    mixed = _mix_fwd(o_gdn, proj, o_mla, small["gdn_norm_w"], small["mla_out_norm_w"], nh, z_col)
    x2 = _mm([(mixed, w_out)], name="proj_out", res=x)
    h2 = _norm_fwd(x2, small["ffn_norm_w"], "norm2")
    act, gpre, upre = _swiglu_fwd(h2, wg4, wu4)
    x3 = _mm([(act, w_down)], name="proj_down", res=x2, tk=2816)
    dx3, d_final, loss, dx3h = _final_loss(x3, tgt, small["final_norm_w"])

    gs = {"final_norm_w": d_final}
    dgate, dup = _swiglu_bwd(dx3h, w_down, gpre, upre)
    g_down = _rows_to_chips(_mm([(act, dx3h)], name="dw_down", ta=True, out_dtype=BF16))
    g_gate = _mm([(h2, dgate)], name="dw_gate", ta=True, out_dtype=BF16, out_chips=True)
    g_up = _mm([(h2, dup)], name="dw_up", ta=True, out_dtype=BF16, out_chips=True)
    halves = lambda g: g.reshape(4, 2, g.shape[1] // 2, g.shape[2])
    ffn_g = [halves(g_gate), halves(g_up), g_down]
    dh2, ffn_sib = _mm([(dgate, wg4), (dup, wu4)], name="dh2", tb=True, b_chips=True, rider=_Swap(ffn_g))
    ffn16, ffn_own = _sum_pairs(ffn_g, ffn_sib, place, "ffn")
    dx2, gs["ffn_norm_w"], dx2h = _norm_bwd(dh2, x2, small["ffn_norm_w"], dx3, "norm2_bwd", True)
    dmix = _mm([(dx2h, w_out)], name="dmix", tb=True)
    g_out = _rows_to_chips(_mm([(mixed, dx2h)], name="dw_out", ta=True, out_dtype=BF16))
    d_ogdn, dproj, d_omla, gs["gdn_norm_w"], gs["mla_out_norm_w"], delta = _mix_bwd(
        dmix, o_gdn, proj, o_mla, small["gdn_norm_w"], small["mla_out_norm_w"], nh, z_col)
    dqc, dkc, dvv, ffn_recv = _mla_bwd(qc, kc, vv, d_omla, lse, delta, nh, _Exchange(ffn16))
    ffn_tot = _reduce_end(ffn_own, ffn_recv, "ffn")
    dqraw, dkvraw, dkr = _mla_rope_bwd(dqc, dkc, dvv, pos, nh)
    dcqn = _mm([(dqraw, wuq_p)], name="dcqn", tb=True)
    dckvn = _mm([(dkvraw, wukv_p)], name="dckvn", tb=True)
    g_uq = _cols_to_chips(_unpermute_w_uq(_mm([(cqn, dqraw)], name="dw_uq", ta=True, out_dtype=BF16), nh))
    g_ukv = _cols_to_chips(_unpermute_w_ukv(_mm([(ckvn, dkvraw)], name="dw_ukv", ta=True, out_dtype=BF16), nh))
    dproj, gs["q_norm_w"], gs["kv_norm_w"] = _mla_norm_bwd(
        proj, small["q_norm_w"], small["kv_norm_w"], dcqn, dckvn, dproj, col_q, col_kv)
    (dgq, dgk, dgv, dgb), ffn_shared = _gdn_bwd(gq, gk, gv, gb, gbt, states, d_ogdn, nh, _Share(ffn_tot))
    dconv, dproj, gs["conv_w"], dal, ddb = _gdn_prep_bwd(
        proj, conv_w, alog_l, dtb_l, dgq, dgk, dgv, dgb, dkr, dproj, nh, misc_col)
    gs["a_log"] = dal[:, A_LANE:A_LANE + nh]
    gs["dt_bias"] = ddb[:, A_LANE:A_LANE + nh]
    dproj = _conv_bwd_input(dconv, conv_w, dproj)
    g_in = _cols_to_chips(_unpermute_w_in(_mm([(h1, dproj)], name="dw_in", ta=True, out_dtype=BF16), nh, lr))
    att_g = [g_in, g_uq, g_ukv, g_out]
    att16, att_own = _sum_pairs(att_g, _comm(_Swap(att_g), "swap_att"), place, "att")
    dh1, att_recv = _mm([(dproj, win_p)], name="dh1", tb=True, rider=_Exchange(att16))
    att_tot = _reduce_end(att_own, att_recv, "att")
    att_shared = _comm(_Share(att_tot), "share_att")
    grad_x, gs["attn_norm_w"] = _norm_bwd(dh1, x, small["attn_norm_w"], dx2, "norm1_bwd", False)
    return loss, grad_x, att_tot + ffn_tot, list(att_shared) + list(ffn_shared), gs


SMALL = ("attn_norm_w", "ffn_norm_w", "final_norm_w", "q_norm_w", "kv_norm_w", "gdn_norm_w",
         "mla_out_norm_w", "a_log", "dt_bias")
WEIGHTS = ("attn_norm_w", "w_in", "conv_w", "a_log", "dt_bias", "gdn_norm_w", "q_norm_w", "w_uq",
           "kv_norm_w", "w_ukv", "mla_out_norm_w", "w_out", "ffn_norm_w", "w_gate", "w_up", "w_down",
           "final_norm_w")


def _pack_small(vecs):
    flat = jnp.concatenate([v.astype(F32).reshape(-1) for v in vecs])
    pad = (-flat.shape[0]) % (8 * LANE)
    return jnp.concatenate([flat, jnp.zeros((pad,), F32)]).reshape(-1, LANE)


def kernel(x, positions, attn_norm_w, w_in, conv_w, a_log, dt_bias, gdn_norm_w, q_norm_w, w_uq, kv_norm_w, w_ukv, mla_out_norm_w, w_out, ffn_norm_w, w_gate, w_up, w_down, final_norm_w, loss_target, m_attn_norm_w, m_w_in, m_conv_w, m_a_log, m_dt_bias, m_gdn_norm_w, m_q_norm_w, m_w_uq, m_kv_norm_w, m_w_ukv, m_mla_out_norm_w, m_w_out, m_ffn_norm_w, m_w_gate, m_w_up, m_w_down, m_final_norm_w, v_attn_norm_w, v_w_in, v_conv_w, v_a_log, v_dt_bias, v_gdn_norm_w, v_q_norm_w, v_w_uq, v_kv_norm_w, v_w_ukv, v_mla_out_norm_w, v_w_out, v_ffn_norm_w, v_w_gate, v_w_up, v_w_down, v_final_norm_w):
    args = dict(locals())
    xi, yi, ci = lax.axis_index("x"), lax.axis_index("y"), lax.axis_index("c")
    chip = 2 * xi + yi

    def two_d(a):
        return a.reshape(a.shape[-2:]) if a.ndim >= 2 else a.reshape(1, -1)

    wloc = {n: two_d(args[n]) for n in WEIGHTS}
    mloc = {n: two_d(args["m_" + n]) for n in WEIGHTS}
    vloc = {n: two_d(args["v_" + n]) for n in WEIGHTS}

    cw = wloc["conv_w"]
    cshard = cw.shape[1]
    cfull = jnp.zeros((CONV, 4 * cshard), F32)
    cfull = lax.dynamic_update_slice(cfull, jnp.where(ci == 0, cw, 0.0), (0, chip * cshard))
    conv_full = _small_allreduce(_pack_small([cfull]), "gather_conv_w").reshape(-1)[:CONV * 4 * cshard]
    conv_full = conv_full.reshape(CONV, 4 * cshard)

    small = {n: wloc[n] for n in SMALL}
    small["conv_w"] = conv_full

    pos = positions.reshape(-1, 1).astype(F32)
    place = jnp.stack([ci, chip]).astype(jnp.int32)
    loss, grad_x, totals, from_sib, gs = _step(
        two_d(x), pos, two_d(loss_target), wloc["w_in"].astype(BF16), [wloc[n].astype(BF16) for n in ATTN_W[1:]],
        [wloc[n].astype(BF16) for n in FFN_W], small, place)

    small_names = SMALL + ("conv_w",)
    pk = _pack_small([gs[n] for n in small_names] + [loss])
    red = _small_allreduce(pk, "reduce_small").reshape(-1)
    gsm, off = {}, 0
    for n in small_names:
        shp = gs[n].shape
        gsm[n] = red[off:off + shp[0] * shp[1]].reshape(shp)
        off += shp[0] * shp[1]
    loss_out = red[off]
    gsm["conv_w"] = lax.dynamic_slice(gsm["conv_w"], (0, chip * cshard), (CONV, cshard))

    grads, deltas, new_m, new_v = {}, {}, {}, {}
    for n, mine, theirs in zip(BIG, totals, from_sib):
        grads[n], deltas[n], new_m[n], new_v[n] = _adamw_halves(wloc[n], mine, theirs, mloc[n], vloc[n], place,
                                                                "adamw_" + n)
    grads["conv_w"] = gsm["conv_w"]
    deltas["conv_w"], new_m["conv_w"], new_v["conv_w"] = _adamw(wloc["conv_w"], gsm["conv_w"], mloc["conv_w"],
                                                                vloc["conv_w"], "adamw_conv_w")
    sm_shapes = [wloc[n].shape for n in SMALL]
    pd, pm, pv = _adamw(_pack_small([wloc[n] for n in SMALL]), _pack_small([gsm[n] for n in SMALL]),
                        _pack_small([mloc[n] for n in SMALL]), _pack_small([vloc[n] for n in SMALL]),
                        "adamw_small")
    for dst, packed in ((deltas, pd), (new_m, pm), (new_v, pv)):
        flat, off = packed.reshape(-1), 0
        for n, shp in zip(SMALL, sm_shapes):
            dst[n] = flat[off:off + shp[0] * shp[1]].reshape(shp)
            off += shp[0] * shp[1]
    for n in SMALL:
        grads[n] = gsm[n]

    def like(n, a):
        return a.reshape(args[n].shape)

    outs = [loss_out.reshape(()), grad_x.reshape(x.shape)]
    for group in (grads, deltas, new_m, new_v):
        outs += [like(n, group[n]) for n in WEIGHTS]
    return tuple(outs)
```

```python
import functools

import jax
import jax.numpy as jnp
from jax import lax
from jax.experimental import pallas as pl
from jax.experimental.pallas import tpu as pltpu

F32, BF16 = jnp.float32, jnp.bfloat16
SDS = jax.ShapeDtypeStruct
MESH = pl.DeviceIdType.MESH

HEAD = 128
ROPE = 64
CHUNK = 64
PAIR = 2 * CHUNK
CONV = 4
EPS = 1e-6
ROPE_THETA = 10000.0
LANE = 128
B_LANE = 64
A_LANE = 72
VMEM_LIMIT = 48 * 1024 * 1024
VMEM_LIMIT_WIDE = 56 * 1024 * 1024
MLA_BLOCK = 512
LOG2E = 1.4426950408889634
LN2 = 0.6931471805599453
SM_SCALE = (HEAD + ROPE) ** -0.5

ADAM_LR = 0.001
ADAM_B1 = 0.9
ADAM_B2 = 0.999
ADAM_EPS = 1e-08
ADAM_WD = 0.01
ADAM_STEP = 10


def _tile(n, pref, mult=LANE):
    if n <= pref:
        return n
    t = (pref // mult) * mult
    while t >= mult:
        if n % t == 0:
            return t
        t -= mult
    return n


def _pcall(body, *, name, grid, in_specs, out_specs, out_shape, scratch=(), vmem=VMEM_LIMIT, aliases=None):
    return pl.pallas_call(
        body, name=name, grid=grid, in_specs=in_specs, out_specs=out_specs,
        out_shape=out_shape, scratch_shapes=list(scratch), input_output_aliases=aliases or {},
        compiler_params=pltpu.CompilerParams(
            dimension_semantics=("arbitrary",) * len(grid), vmem_limit_bytes=vmem))


def _pcall_riding(core, rider, *, name, grid, in_specs, out_specs, out_shape, args, scratch=()):
    n_in, n_out, n_scr = len(in_specs), len(out_specs), len(scratch)
    r_in, r_out = len(rider.arrays), len(rider.out_shapes)

    def body(*refs):
        ins, refs = refs[:n_in], refs[n_in:]
        r_ins, refs = refs[:r_in], refs[r_in:]
        outs, refs = refs[:n_out], refs[n_out:]
        r_outs, refs = refs[:r_out], refs[r_out:]
        scr, sems = refs[:n_scr], refs[n_scr:]
        r_refs = (r_ins, r_outs, sems)
        _ride_begin(rider, r_refs, pl.program_id(0))
        core(*ins, *outs, *scr)
        _ride_end(rider, r_refs, pl.program_id(0), grid[0])

    res = _pcall(body, name=name, grid=grid, in_specs=list(in_specs) + [_ANY] * r_in,
                 out_specs=list(out_specs) + [_ANY] * r_out, out_shape=list(out_shape) + rider.out_shapes,
                 scratch=list(scratch) + rider.scratch)(*args, *rider.arrays)
    return res[:n_out], res[n_out:]


def _rows(ts, width, col=0):
    return pl.BlockSpec((ts, width), lambda i: (i, col))


def _full(shape):
    nd = len(shape)
    return pl.BlockSpec(shape, lambda i: (0,) * nd)


def _dot(a, b):
    return jnp.dot(a.astype(BF16), b.astype(BF16), preferred_element_type=F32)


def _dot_nt(a, b):
    return lax.dot_general(a.astype(BF16), b.astype(BF16), (((1,), (1,)), ((), ())),
                           preferred_element_type=F32)


def _dot_tn(a, b):
    return lax.dot_general(a.astype(BF16), b.astype(BF16), (((0,), (0,)), ((), ())),
                           preferred_element_type=F32)


def _sigmoid(x):
    return 1.0 / (1.0 + jnp.exp(-x))


def _silu(x):
    return x * _sigmoid(x)


def _dsilu(x):
    s = _sigmoid(x)
    return s * (1.0 + x * (1.0 - s))


def _lane_iota(shape):
    return lax.broadcasted_iota(jnp.int32, shape, len(shape) - 1)


def _col(block, idx):
    return jnp.sum(jnp.where(_lane_iota(block.shape) == idx, block, 0.0), axis=-1, keepdims=True)


def _mm(pairs, *, name, ta=False, tb=False, out_dtype=F32, res=None, tm=1024, tn=1024, tk=2048,
        b_chips=False, out_chips=False, rider=None):
    a0, b0 = pairs[0]
    if ta:
        kdim, m = a0.shape
    else:
        m, kdim = a0.shape
    if b_chips and tb:
        n, tk = b0.shape[1], b0.shape[2]
        assert kdim == 4 * tk
    elif b_chips:
        n, tn = 4 * b0.shape[2], b0.shape[2]
        assert kdim == b0.shape[1]
    else:
        n = b0.shape[0] if tb else b0.shape[1]
    if out_chips:
        tn = n // 4
    tm = _tile(m, tm)
    tn = tn if (out_chips or (b_chips and not tb)) else _tile(n, tn)
    tk = tk if (b_chips and tb) else _tile(kdim, tk)
    assert m % tm == 0 and n % tn == 0 and kdim % tk == 0
    nk, npair = kdim // tk, len(pairs)
    grid = (m // tm, n // tn, nk)
    dims = (((0 if ta else 1,), (1 if tb else 0,)), ((), ()))
    n_in = 2 * npair + (res is not None)
    r_in, r_out = (len(rider.arrays), len(rider.out_shapes)) if rider else (0, 0)

    def body(*refs):
        o_ref = refs[n_in + r_in]
        acc = refs[n_in + r_in + 1 + r_out]
        k = pl.program_id(2)
        if rider:
            r_refs = (refs[n_in:n_in + r_in], refs[n_in + r_in + 1:n_in + r_in + 1 + r_out],
                      refs[n_in + r_in + 2 + r_out:])
            step = (pl.program_id(0) * grid[1] + pl.program_id(1)) * nk + k
            _ride_begin(rider, r_refs, step)

        @pl.when(k == 0)
        def _():
            acc[...] = jnp.zeros_like(acc)

        tot = None
        for p in range(npair):
            d = lax.dot_general(refs[2 * p][...].astype(BF16), refs[2 * p + 1][...].astype(BF16),
                                dims, preferred_element_type=F32)
            tot = d if tot is None else tot + d
        acc[...] += tot

        @pl.when(k == nk - 1)
        def _():
            r = acc[...]
            if res is not None:
                r = r + refs[2 * npair][...]
            o_ref[...] = r.astype(out_dtype)

        if rider:
            _ride_end(rider, r_refs, step, grid[0] * grid[1] * nk)

    if ta:
        a_spec = pl.BlockSpec((tk, tm), lambda i, j, k: (k, i))
    else:
        a_spec = pl.BlockSpec((tm, tk), lambda i, j, k: (i, k))
    if b_chips and tb:
        b_spec = pl.BlockSpec((None, tn, tk), lambda i, j, k: (k, j, 0))
    elif b_chips:
        b_spec = pl.BlockSpec((None, tk, tn), lambda i, j, k: (j, k, 0))
    elif tb:
        b_spec = pl.BlockSpec((tn, tk), lambda i, j, k: (j, k))
    else:
        b_spec = pl.BlockSpec((tk, tn), lambda i, j, k: (k, j))
    if out_chips:
        o_spec = pl.BlockSpec((None, tm, tn), lambda i, j, k: (j, i, 0))
        o_shape = SDS((4, m, tn), out_dtype)
    else:
        o_spec = pl.BlockSpec((tm, tn), lambda i, j, k: (i, j))
        o_shape = SDS((m, n), out_dtype)
    in_specs, args = [], []
    for a, b in pairs:
        in_specs += [a_spec, b_spec]
        args += [a, b]
    if res is not None:
        in_specs.append(o_spec)
        args.append(res)
    out_specs, out_shapes, scratch = [o_spec], [o_shape], [pltpu.VMEM((tm, tn), F32)]
    if rider:
        in_specs += [_ANY] * r_in
        args += rider.arrays
        out_specs += [_ANY] * r_out
        out_shapes += rider.out_shapes
        scratch += rider.scratch
    outs = _pcall(body, name=name, grid=grid, in_specs=in_specs, out_specs=out_specs, out_shape=out_shapes,
                  scratch=scratch)(*args)
    return (outs[0], outs[1:]) if rider else outs[0]


def _norm_fwd(x, w, name, rider=None):
    s, d = x.shape
    ts = _tile(s, 512, 8)

    def body(x_ref, w_ref, h_ref):
        xv = x_ref[...]
        r = lax.rsqrt(jnp.mean(xv * xv, axis=-1, keepdims=True) + EPS)
        h_ref[...] = (xv * r * w_ref[...]).astype(BF16)

    spec = dict(name=name, grid=(s // ts,), in_specs=[_rows(ts, d), _full((1, d))])
    if rider is None:
        return _pcall(body, out_specs=_rows(ts, d), out_shape=SDS((s, d), BF16), **spec)(x, w)
    outs, r_outs = _pcall_riding(body, rider, out_specs=[_rows(ts, d)], out_shape=[SDS((s, d), BF16)],
                                 args=(x, w), **spec)
    return outs[0], r_outs


def _norm_bwd(dh, x, w, dres, name, with_bf16):
    s, d = x.shape
    ts = _tile(s, 256, 8)

    def body(dh_ref, x_ref, w_ref, dres_ref, dx_ref, dw_ref, *dx16_ref):
        @pl.when(pl.program_id(0) == 0)
        def _():
            dw_ref[...] = jnp.zeros_like(dw_ref)

        xv, dhv = x_ref[...], dh_ref[...]
        r = lax.rsqrt(jnp.mean(xv * xv, axis=-1, keepdims=True) + EPS)
        xh = xv * r
        dw_ref[...] += jnp.sum(dhv * xh, axis=0, keepdims=True)
        dxh = dhv * w_ref[...]
        dx = dres_ref[...] + r * (dxh - xh * jnp.mean(dxh * xh, axis=-1, keepdims=True))
        dx_ref[...] = dx
        for ref in dx16_ref:
            ref[...] = dx.astype(BF16)

    extra = 1 if with_bf16 else 0
    return _pcall(body, name=name, grid=(s // ts,),
                  in_specs=[_rows(ts, d), _rows(ts, d), _full((1, d)), _rows(ts, d)],
                  out_specs=[_rows(ts, d), _full((1, d))] + [_rows(ts, d)] * extra,
                  out_shape=[SDS((s, d), F32), SDS((1, d), F32)] + [SDS((s, d), BF16)] * extra)(
                      dh, x, w, dres)


def _final_loss(x3, tgt, w):
    s, d = x3.shape
    ts = _tile(s, 256, 8)

    def body(x_ref, t_ref, w_ref, dx_ref, dw_ref, loss_ref, dx16_ref):
        @pl.when(pl.program_id(0) == 0)
        def _():
            dw_ref[...] = jnp.zeros_like(dw_ref)
            loss_ref[...] = jnp.zeros_like(loss_ref)

        xv, wv = x_ref[...], w_ref[...]
        r = lax.rsqrt(jnp.mean(xv * xv, axis=-1, keepdims=True) + EPS)
        xh = xv * r
        err = xh * wv - t_ref[...]
        row = jnp.mean(err * err, axis=-1, keepdims=True)
        loss_ref[...] += 0.5 * jnp.sum(row, axis=0, keepdims=True)
        dy = err * (1.0 / d)
        dw_ref[...] += jnp.sum(dy * xh, axis=0, keepdims=True)
        dxh = dy * wv
        dx = r * (dxh - xh * jnp.mean(dxh * xh, axis=-1, keepdims=True))
        dx_ref[...] = dx
        dx16_ref[...] = dx.astype(BF16)

    return _pcall(body, name="final_loss", grid=(s // ts,),
                  in_specs=[_rows(ts, d), _rows(ts, d), _full((1, d))],
                  out_specs=[_rows(ts, d), _full((1, d)), _full((1, 1)), _rows(ts, d)],
                  out_shape=[SDS((s, d), F32), SDS((1, d), F32), SDS((1, 1), F32), SDS((s, d), BF16)])(
                      x3, tgt, w)


def _shift_down(cur, halo, s):
    if s == 0:
        return cur
    row8 = lax.broadcasted_iota(jnp.int32, halo.shape, 0)
    r = pltpu.roll(cur, s, 0)
    top = jnp.where(row8 < s, pltpu.roll(halo, s, 0), r[0:8])
    return jnp.concatenate([top, r[8:]], axis=0)


def _shift_up(cur, halo, s):
    if s == 0:
        return cur
    ts = cur.shape[0]
    row8 = lax.broadcasted_iota(jnp.int32, halo.shape, 0)
    r = pltpu.roll(cur, ts - s, 0)
    bot = jnp.where(row8 >= 8 - s, pltpu.roll(halo, 8 - s, 0), r[ts - 8:ts])
    return jnp.concatenate([r[:ts - 8], bot], axis=0)


def _chunk_tri(ts, upper):
    i = lax.broadcasted_iota(jnp.int32, (ts, ts), 0)
    j = lax.broadcasted_iota(jnp.int32, (ts, ts), 1)
    same = jnp.right_shift(i, 6) == jnp.right_shift(j, 6)
    return jnp.where(same & ((j >= i) if upper else (j <= i)), 1.0, 0.0).astype(F32)


def _gate_values(m, alog, dtb):
    lane = _lane_iota(m.shape)
    beta = _sigmoid(m)
    xg = m + dtb
    sp = jnp.maximum(xg, 0.0) + jnp.log(1.0 + jnp.exp(-jnp.abs(xg)))
    ga = (lane >= A_LANE) & (lane < A_LANE + 8)
    g = jnp.where(ga, -jnp.exp(alog) * sp, 0.0)
    return beta, g, xg, ga


def _l2_heads(a, nh, scale):
    outs, rs = [], []
    for h in range(nh):
        ah = a[:, HEAD * h:HEAD * (h + 1)]
        r = lax.rsqrt(jnp.sum(ah * ah, axis=-1, keepdims=True) + EPS)
        outs.append(ah * (r * scale))
        rs.append(r)
    return jnp.concatenate(outs, axis=-1), rs


def _gdn_prep(proj, conv_w, alog_l, dtb_l, nh, misc_col):
    s = proj.shape[0]
    w = nh * HEAD
    ts = _tile(s, 256, PAIR)
    hb = ts // 8

    def body(cur_ref, halo_ref, misc_ref, cw_ref, al_ref, db_ref, q_ref, k_ref, v_ref, gb_ref, gbt_ref):
        first = pl.program_id(0) == 0
        outs = (q_ref, k_ref, v_ref)
        for sec in range(3):
            cs = slice(sec * w, (sec + 1) * w)
            cur = cur_ref[:, cs]
            halo = jnp.where(first, 0.0, halo_ref[:, cs])
            pre = None
            for j in range(CONV):
                term = cw_ref[j:j + 1, cs] * _shift_down(cur, halo, CONV - 1 - j)
                pre = term if pre is None else pre + term
            act = _silu(pre)
            if sec == 0:
                act, _ = _l2_heads(act, nh, HEAD ** -0.5)
            elif sec == 1:
                act, _ = _l2_heads(act, nh, 1.0)
            outs[sec][...] = act
        m = misc_ref[...]
        lane = _lane_iota(m.shape)
        beta, g, _, ga = _gate_values(m, al_ref[...], db_ref[...])
        gcc = jnp.dot(_chunk_tri(ts, False), g, precision=lax.Precision.HIGHEST,
                      preferred_element_type=F32)
        gb = jnp.where((lane >= B_LANE) & (lane < B_LANE + 8), beta, jnp.where(ga, gcc, 0.0))
        gb_ref[...] = gb
        gbt_ref[...] = gb.T

    return _pcall(
        body, name="gdn_prep", grid=(s // ts,),
        in_specs=[_rows(ts, 3 * w),
                  pl.BlockSpec((8, 3 * w), lambda i: (jnp.maximum(i * hb - 1, 0), 0)),
                  _rows(ts, LANE, misc_col), _full((CONV, 3 * w)), _full((1, LANE)), _full((1, LANE))],
        out_specs=[_rows(ts, w), _rows(ts, w), _rows(ts, w), _rows(ts, LANE),
                   pl.BlockSpec((LANE, ts), lambda i: (0, i))],
        out_shape=[SDS((s, w), F32), SDS((s, w), F32), SDS((s, w), F32), SDS((s, LANE), F32),
                   SDS((LANE, s), F32)])(proj, proj, proj, conv_w, alog_l, dtb_l)


def _gdn_prep_bwd(proj, conv_w, alog_l, dtb_l, dq, dk, dv, dgb, dkr, dproj, nh, misc_col):
    s = proj.shape[0]
    w = nh * HEAD
    ts = _tile(s, 256, PAIR)
    hb = ts // 8
    assert misc_col % 2 == 0

    def body(cur_ref, halo_ref, misc_ref, cw_ref, al_ref, db_ref, dq_ref, dk_ref, dv_ref, dgb_ref,
             dkr_ref, _, dc_ref, dm_ref, dcw_ref, dal_ref, ddb_ref):
        first = pl.program_id(0) == 0

        @pl.when(first)
        def _():
            dcw_ref[...] = jnp.zeros_like(dcw_ref)
            dal_ref[...] = jnp.zeros_like(dal_ref)
            ddb_ref[...] = jnp.zeros_like(ddb_ref)

        dins = (dq_ref, dk_ref, dv_ref)
        for sec in range(3):
            cs = slice(sec * w, (sec + 1) * w)
            cur = cur_ref[:, cs]
            halo = jnp.where(first, 0.0, halo_ref[:, cs])
            us = [_shift_down(cur, halo, CONV - 1 - j) for j in range(CONV)]
            pre = None
            for j in range(CONV):
                term = cw_ref[j:j + 1, cs] * us[j]
                pre = term if pre is None else pre + term
            act = _silu(pre)
            dout = dins[sec][...]
            if sec < 2:
                scale = HEAD ** -0.5 if sec == 0 else 1.0
                parts = []
                for h in range(nh):
                    hs = slice(HEAD * h, HEAD * (h + 1))
                    ah = act[:, hs]
                    r = lax.rsqrt(jnp.sum(ah * ah, axis=-1, keepdims=True) + EPS)
                    ahat = ah * r
                    dy = dout[:, hs]
                    parts.append((scale * r) * (dy - ahat * jnp.sum(dy * ahat, axis=-1, keepdims=True)))
                dact = jnp.concatenate(parts, axis=-1)
            else:
                dact = dout
            dconv = dact * _dsilu(pre)
            dc_ref[:, cs] = dconv
            for j in range(CONV):
                dcw_ref[j:j + 1, cs] += jnp.sum(dconv * us[j], axis=0, keepdims=True)
        m = misc_ref[...]
        lane = _lane_iota(m.shape)
        al = al_ref[...]
        beta, g, xg, ga = _gate_values(m, al, db_ref[...])
        dgbv = dgb_ref[...]
        dg = jnp.dot(_chunk_tri(ts, True), jnp.where(ga, dgbv, 0.0), precision=lax.Precision.HIGHEST,
                     preferred_element_type=F32)
        da_raw = jnp.where(ga, dg * (-jnp.exp(al)) * _sigmoid(xg), 0.0)
        db_raw = jnp.where((lane >= B_LANE) & (lane < B_LANE + 8), dgbv * beta * (1.0 - beta), 0.0)
        dal_ref[...] += jnp.sum(dg * g, axis=0, keepdims=True)
        ddb_ref[...] += jnp.sum(da_raw, axis=0, keepdims=True)
        dm_ref[:, :LANE] = (dkr_ref[...] + da_raw + db_raw).astype(BF16)
        dm_ref[:, LANE:] = jnp.zeros((ts, LANE), BF16)

    return _pcall(
        body, name="gdn_prep_bwd", grid=(s // ts,),
        in_specs=[_rows(ts, 3 * w),
                  pl.BlockSpec((8, 3 * w), lambda i: (jnp.maximum(i * hb - 1, 0), 0)),
                  _rows(ts, LANE, misc_col), _full((CONV, 3 * w)), _full((1, LANE)), _full((1, LANE)),
                  _rows(ts, w), _rows(ts, w), _rows(ts, w), _rows(ts, LANE), _rows(ts, LANE), _ANY],
        out_specs=[_rows(ts, 3 * w), _rows(ts, 2 * LANE, misc_col // 2), _full((CONV, 3 * w)), _full((1, LANE)),
                   _full((1, LANE))],
        out_shape=[SDS((s, 3 * w), F32), SDS(dproj.shape, BF16), SDS((CONV, 3 * w), F32),
                   SDS((1, LANE), F32), SDS((1, LANE), F32)], aliases={11: 1})(
                       proj, proj, proj, conv_w, alog_l, dtb_l, dq, dk, dv, dgb, dkr, dproj)


def _conv_bwd_input(dconv, conv_w, dproj):
    s, c = dconv.shape
    ts = _tile(s, 256, 8)
    hb = ts // 8
    nblk8 = s // 8
    nt = s // ts

    def body(cur_ref, nxt_ref, cw_ref, _, o_ref):
        last = pl.program_id(0) == nt - 1
        cur = cur_ref[...]
        halo = jnp.where(last, 0.0, nxt_ref[...])
        acc = None
        for j in range(CONV):
            term = cw_ref[j:j + 1, :] * _shift_up(cur, halo, CONV - 1 - j)
            acc = term if acc is None else acc + term
        o_ref[...] = acc.astype(BF16)

    return _pcall(
        body, name="conv_bwd_input", grid=(nt,),
        in_specs=[_rows(ts, c),
                  pl.BlockSpec((8, c), lambda i: (jnp.minimum((i + 1) * hb, nblk8 - 1), 0)),
                  _full((CONV, c)), _ANY],
        out_specs=_rows(ts, c), out_shape=SDS(dproj.shape, BF16), aliases={3: 0})(
            dconv, dconv, conv_w, dproj)


def _inv_unit_lower(a):
    n = a[0].shape[0]
    i = lax.broadcasted_iota(jnp.int32, (n, n), 0)
    j = lax.broadcasted_iota(jnp.int32, (n, n), 1)
    eye = jnp.where(i == j, 1.0, 0.0)
    t = [eye - ah for ah in a]
    x = a
    for _ in range(5):
        x = [_dot(xh, xh) for xh in x]
        t = [th + _dot(th, xh) for th, xh in zip(t, x)]
    return t


def _pair_common(q, k, gcol, grow, bcol):
    i = lax.broadcasted_iota(jnp.int32, (PAIR, PAIR), 0)
    j = lax.broadcasted_iota(jnp.int32, (PAIR, PAIR), 1)
    same = jnp.right_shift(i, 6) == jnp.right_shift(j, 6)
    tril = same & (i >= j)
    strict = same & (i > j)
    dec = [jnp.where(tril, jnp.exp(jnp.minimum(gc - gr, 0.0)), 0.0) for gc, gr in zip(gcol, grow)]
    kk = [_dot_nt(kh, kh) for kh in k]
    qk = [_dot_nt(qh, kh) for qh, kh in zip(q, k)]
    a = [jnp.where(strict, b * kkh * d, 0.0) for b, kkh, d in zip(bcol, kk, dec)]
    t = _inv_unit_lower(a)
    p = [qkh * d for qkh, d in zip(qk, dec)]
    return dec, kk, a, t, p, tril, strict


def _ext(v, a):
    z = jnp.zeros_like(v)
    return jnp.concatenate([v, z] if a == 0 else [z, v], axis=0)


def _gdn_fwd(q, k, v, gb, gbt, nh):
    s = q.shape[0]
    w = nh * HEAD
    npair = s // PAIR

    def body(q_ref, k_ref, v_ref, gb_ref, gbt_ref, o_ref, st_ref, s_ref):
        @pl.when(pl.program_id(0) == 0)
        def _():
            s_ref[...] = jnp.zeros_like(s_ref)

        heads = range(nh)
        hs = [slice(HEAD * h, HEAD * (h + 1)) for h in heads]
        gbv = gb_ref[...]
        q, k, v = [q_ref[:, s_] for s_ in hs], [k_ref[:, s_] for s_ in hs], [v_ref[:, s_] for s_ in hs]
        gcol = [_col(gbv, A_LANE + h) for h in heads]
        bcol = [_col(gbv, B_LANE + h) for h in heads]
        grow = [gbt_ref[A_LANE + h:A_LANE + h + 1, :] for h in heads]
        _, _, _, t, p, _, _ = _pair_common(q, k, gcol, grow, bcol)
        eg = [jnp.exp(gc) for gc in gcol]
        qg = [x * e for x, e in zip(q, eg)]
        kg = [x * e for x, e in zip(k, eg)]
        outs = []
        for a in range(2):
            sl = slice(CHUNK * a, CHUNK * (a + 1))
            st = [s_ref[h] for h in heads]
            for h in heads:
                st_ref[a, h] = st[h]
            r = [v[h][sl] - _dot(kg[h][sl], st[h]) for h in heads]
            vn = [_dot(t[h][sl], _ext(bcol[h][sl] * r[h], a)) for h in heads]
            outs.append([_dot(qg[h][sl], st[h]) + _dot(p[h][sl], _ext(vn[h], a)) for h in heads])
            gl = [_col(gr, CHUNK * (a + 1) - 1) for gr in grow]
            kd = [k[h][sl] * jnp.exp(gl[h] - gcol[h][sl]) for h in heads]
            upd = [_dot_tn(kd[h], vn[h]) for h in heads]
            for h in heads:
                s_ref[h] = jnp.exp(gl[h]) * st[h] + upd[h]
        for h in heads:
            o_ref[:, hs[h]] = jnp.concatenate([outs[0][h], outs[1][h]], axis=0)

    return _pcall(
        body, name="gdn_fwd", grid=(npair,),
        in_specs=[_rows(PAIR, w), _rows(PAIR, w), _rows(PAIR, w), _rows(PAIR, LANE),
                  pl.BlockSpec((LANE, PAIR), lambda i: (0, i))],
        out_specs=[_rows(PAIR, w), pl.BlockSpec((2, nh, HEAD, HEAD), lambda i: (i, 0, 0, 0))],
        out_shape=[SDS((s, w), F32), SDS((2 * npair, nh, HEAD, HEAD), F32)],
        scratch=[pltpu.VMEM((nh, HEAD, HEAD), F32)])(q, k, v, gb, gbt)


def _gdn_bwd(q, k, v, gb, gbt, states, do, nh, rider):
    s = q.shape[0]
    w = nh * HEAD
    npair = s // PAIR
    rev = lambda i: (npair - 1 - i, 0)

    def body(q_ref, k_ref, v_ref, gb_ref, gbt_ref, st_ref, do_ref, dq_ref, dk_ref, dv_ref, dgb_ref,
             ds_ref):
        @pl.when(pl.program_id(0) == 0)
        def _():
            ds_ref[...] = jnp.zeros_like(ds_ref)

        lane = _lane_iota((PAIR, LANE))
        row = lax.broadcasted_iota(jnp.int32, (CHUNK, 1), 0)
        heads = range(nh)
        hs = [slice(HEAD * h, HEAD * (h + 1)) for h in heads]
        gbv = gb_ref[...]
        q, k, v = [q_ref[:, s_] for s_ in hs], [k_ref[:, s_] for s_ in hs], [v_ref[:, s_] for s_ in hs]
        do = [do_ref[:, s_] for s_ in hs]
        gcol = [_col(gbv, A_LANE + h) for h in heads]
        bcol = [_col(gbv, B_LANE + h) for h in heads]
        grow = [gbt_ref[A_LANE + h:A_LANE + h + 1, :] for h in heads]
        dec, kk, amat, t, p, tril, strict = _pair_common(q, k, gcol, grow, bcol)
        tt, pt = [x.T for x in t], [x.T for x in p]
        eg = [jnp.exp(gc) for gc in gcol]
        qg = [x * e for x, e in zip(q, eg)]
        kg = [x * e for x, e in zip(k, eg)]
        sums = lambda x: jnp.sum(x, axis=-1, keepdims=True)
        rs, vns = [None, None], [None, None]
        for a in range(2):
            sl = slice(CHUNK * a, CHUNK * (a + 1))
            rs[a] = [v[h][sl] - _dot(kg[h][sl], st_ref[a, h]) for h in heads]
            vns[a] = [_dot(t[h][sl], _ext(bcol[h][sl] * rs[a][h], a)) for h in heads]
        dsn = [ds_ref[h] for h in heads]
        dqs, dks, dvs, dgcs, dbs, drbs = ([None, None] for _ in range(6))
        for a in (1, 0):
            sl = slice(CHUNK * a, CHUNK * (a + 1))
            st = [st_ref[a, h] for h in heads]
            gl = [_col(gr, CHUNK * (a + 1) - 1) for gr in grow]
            egl = [jnp.exp(x) for x in gl]
            dk_dec = [jnp.exp(gl[h] - gcol[h][sl]) for h in heads]
            kd = [k[h][sl] * dk_dec[h] for h in heads]
            d_vn = [_dot(pt[h][sl], _ext(do[h][sl], a)) + _dot(kd[h], dsn[h]) for h in heads]
            d_qg = [_dot_nt(do[h][sl], st[h]) for h in heads]
            d_rb = [_dot(tt[h][sl], _ext(d_vn[h], a)) for h in heads]
            d_r = [bcol[h][sl] * d_rb[h] for h in heads]
            d_kg = [-_dot_nt(d_r[h], st[h]) for h in heads]
            d_kd = [_dot_nt(vns[a][h], dsn[h]) for h in heads]
            dsn_new = [_dot_tn(qg[h][sl], do[h][sl]) - _dot_tn(kg[h][sl], d_r[h]) for h in heads]
            dbs[a] = [sums(d_rb[h] * rs[a][h]) for h in heads]
            dgl = [egl[h] * jnp.sum(dsn[h] * st[h], keepdims=True) + jnp.sum(d_kd[h] * kd[h], keepdims=True)
                   for h in heads]
            dgcs[a] = [sums(d_qg[h] * qg[h][sl]) + sums(d_kg[h] * kg[h][sl]) - sums(d_kd[h] * kd[h])
                       + jnp.where(row == CHUNK - 1, dgl[h], 0.0) for h in heads]
            dqs[a] = [d_qg[h] * eg[h][sl] for h in heads]
            dks[a] = [d_kg[h] * eg[h][sl] + d_kd[h] * dk_dec[h] for h in heads]
            dvs[a] = d_r
            drbs[a] = d_rb
            dsn = [dsn_new[h] + egl[h] * dsn[h] for h in heads]
        for h in heads:
            ds_ref[h] = dsn[h]
        cat = lambda xs, h: jnp.concatenate([xs[0][h], xs[1][h]], axis=0)
        vn = [cat(vns, h) for h in heads]
        d_rb = [cat(drbs, h) for h in heads]
        dp = [jnp.where(tril, _dot_nt(do[h], vn[h]), 0.0) for h in heads]
        dam = [jnp.where(strict, -_dot_nt(d_rb[h], vn[h]), 0.0) for h in heads]
        g_p = [dp[h] * dec[h] for h in heads]
        g_a = [dam[h] * dec[h] for h in heads]
        gbk = [bcol[h] * g_a[h] for h in heads]
        dq2 = [_dot(g_p[h], k[h]) for h in heads]
        dk2 = [_dot_tn(g_p[h], q[h]) + _dot(gbk[h], k[h]) + _dot_tn(gbk[h], k[h]) for h in heads]
        dgb = jnp.zeros((PAIR, LANE), F32)
        for h in heads:
            dq_ref[:, hs[h]] = cat(dqs, h) + dq2[h]
            dk_ref[:, hs[h]] = cat(dks, h) + dk2[h]
            dv_ref[:, hs[h]] = cat(dvs, h)
            dbeta = cat(dbs, h) + sums(g_a[h] * kk[h])
            mm = dp[h] * p[h] + dam[h] * amat[h]
            dgc = cat(dgcs, h) + sums(mm) - sums(mm.T)
            dgb = dgb + jnp.where(lane == A_LANE + h, dgc, 0.0) + jnp.where(lane == B_LANE + h, dbeta, 0.0)
        dgb_ref[...] = dgb

    return _pcall_riding(
        body, rider, name="gdn_bwd", grid=(npair,),
        in_specs=[pl.BlockSpec((PAIR, w), rev), pl.BlockSpec((PAIR, w), rev), pl.BlockSpec((PAIR, w), rev),
                  pl.BlockSpec((PAIR, LANE), rev),
                  pl.BlockSpec((LANE, PAIR), lambda i: (0, npair - 1 - i)),
                  pl.BlockSpec((2, nh, HEAD, HEAD), lambda i: (npair - 1 - i, 0, 0, 0)),
                  pl.BlockSpec((PAIR, w), rev)],
        out_specs=[pl.BlockSpec((PAIR, w), rev), pl.BlockSpec((PAIR, w), rev), pl.BlockSpec((PAIR, w), rev),
                   pl.BlockSpec((PAIR, LANE), rev)],
        out_shape=[SDS((s, w), F32), SDS((s, w), F32), SDS((s, w), F32), SDS((s, LANE), F32)],
        scratch=[pltpu.VMEM((nh, HEAD, HEAD), F32)], args=(q, k, v, gb, gbt, states, do))


def _mla_norm(proj, qw, kvw, col_q, col_kv):
    s = proj.shape[0]
    lr = qw.shape[1]
    ts = _tile(s, 512, 8)

    def body(cq_ref, ckv_ref, qw_ref, kvw_ref, oq_ref, okv_ref):
        for x_ref, w_ref, o_ref in ((cq_ref, qw_ref, oq_ref), (ckv_ref, kvw_ref, okv_ref)):
            xv = x_ref[...]
            r = lax.rsqrt(jnp.mean(xv * xv, axis=-1, keepdims=True) + EPS)
            o_ref[...] = (xv * r * w_ref[...]).astype(BF16)

    return _pcall(body, name="mla_norm", grid=(s // ts,),
                  in_specs=[_rows(ts, lr, col_q), _rows(ts, lr, col_kv), _full((1, lr)), _full((1, lr))],
                  out_specs=[_rows(ts, lr), _rows(ts, lr)],
                  out_shape=[SDS((s, lr), BF16), SDS((s, lr), BF16)])(proj, proj, qw, kvw)


def _mla_norm_bwd(proj, qw, kvw, dq, dkv, dproj, col_q, col_kv):
    s = proj.shape[0]
    lr = qw.shape[1]
    ts = _tile(s, 512, 8)

    assert col_kv == col_q + 1 and col_q % 2 == 0

    def body(cq_ref, ckv_ref, qw_ref, kvw_ref, dq_ref, dkv_ref, _, o_ref, dqw_ref, dkvw_ref):
        @pl.when(pl.program_id(0) == 0)
        def _():
            dqw_ref[...] = jnp.zeros_like(dqw_ref)
            dkvw_ref[...] = jnp.zeros_like(dkvw_ref)

        for k, (x_ref, w_ref, d_ref, dw_ref) in enumerate(((cq_ref, qw_ref, dq_ref, dqw_ref),
                                                           (ckv_ref, kvw_ref, dkv_ref, dkvw_ref))):
            xv, dh = x_ref[...], d_ref[...]
            r = lax.rsqrt(jnp.mean(xv * xv, axis=-1, keepdims=True) + EPS)
            xh = xv * r
            dw_ref[...] += jnp.sum(dh * xh, axis=0, keepdims=True)
            dxh = dh * w_ref[...]
            o_ref[:, lr * k:lr * (k + 1)] = (
                r * (dxh - xh * jnp.mean(dxh * xh, axis=-1, keepdims=True))).astype(BF16)

    return _pcall(body, name="mla_norm_bwd", grid=(s // ts,),
                  in_specs=[_rows(ts, lr, col_q), _rows(ts, lr, col_kv), _full((1, lr)), _full((1, lr)),
                            _rows(ts, lr), _rows(ts, lr), _ANY],
                  out_specs=[_rows(ts, 2 * lr, col_q // 2), _full((1, lr)), _full((1, lr))],
                  out_shape=[SDS(dproj.shape, BF16), SDS((1, lr), F32), SDS((1, lr), F32)],
                  aliases={6: 0})(proj, proj, qw, kvw, dq, dkv, dproj)


def _rope_tables(pos, invf, sgn):
    ang = pos * invf
    return jnp.cos(ang), jnp.sin(ang) * sgn


def _swap_halves_lanes(y):
    lane = _lane_iota(y.shape)
    return jnp.where(lane < ROPE // 2, pltpu.roll(y, LANE - ROPE // 2, 1), pltpu.roll(y, ROPE // 2, 1))


def _rope_consts():
    half = ROPE // 2
    inv = ROPE_THETA ** (-jnp.arange(half, dtype=F32) / half)
    invf = jnp.concatenate([inv, inv, jnp.zeros((LANE - ROPE,), F32)])[None, :]
    sgn = jnp.concatenate([-jnp.ones((half,), F32), jnp.ones((half,), F32),
                           jnp.zeros((LANE - ROPE,), F32)])[None, :]
    return invf, sgn


def _mla_rope(qraw, kvraw, proj, pos, nh, misc_col):
    s = qraw.shape[0]
    ts = _tile(s, 256, 8)
    wq = nh * 2 * HEAD
    invf, sgn = _rope_consts()

    def body(q_ref, kv_ref, misc_ref, pos_ref, if_ref, sg_ref, qc_ref, kc_ref, v_ref):
        c, sn = _rope_tables(pos_ref[...], if_ref[...], sg_ref[...])
        lane = _lane_iota(c.shape)
        rot = lambda xb: xb * c + _swap_halves_lanes(xb) * sn
        qs = SM_SCALE * LOG2E
        krot = jnp.where(lane < ROPE, rot(misc_ref[...]), 0.0).astype(BF16)
        for h in range(nh):
            b0 = 2 * HEAD * h
            qc_ref[:, b0:b0 + HEAD] = (q_ref[:, b0:b0 + HEAD].astype(F32) * qs).astype(BF16)
            qc_ref[:, b0 + HEAD:b0 + 2 * HEAD] = (
                rot(q_ref[:, b0 + HEAD:b0 + 2 * HEAD].astype(F32)) * qs).astype(BF16)
            kc_ref[:, b0:b0 + HEAD] = kv_ref[:, b0:b0 + HEAD].astype(BF16)
            kc_ref[:, b0 + HEAD:b0 + 2 * HEAD] = krot
        v_ref[...] = kv_ref[:, wq:].astype(BF16)

    return _pcall(body, name="mla_rope", grid=(s // ts,),
                  in_specs=[_rows(ts, wq), _rows(ts, wq + nh * HEAD), _rows(ts, LANE, misc_col),
                            _rows(ts, 1), _full((1, LANE)), _full((1, LANE))],
                  out_specs=[_rows(ts, wq), _rows(ts, wq), _rows(ts, nh * HEAD)],
                  out_shape=[SDS((s, wq), BF16), SDS((s, wq), BF16), SDS((s, nh * HEAD), BF16)])(
                      qraw, kvraw, proj, pos, invf, sgn)


def _mla_rope_bwd(dqc, dkc, dv, pos, nh):
    s = dqc.shape[0]
    ts = _tile(s, 256, 8)
    wq = nh * 2 * HEAD
    invf, sgn = _rope_consts()

    def body(dq_ref, dk_ref, dv_ref, pos_ref, if_ref, sg_ref, oq_ref, okv_ref, okr_ref):
        c, sn = _rope_tables(pos_ref[...], if_ref[...], sg_ref[...])
        lane = _lane_iota(c.shape)
        unrot = lambda d: d * c + _swap_halves_lanes(d * sn)
        dkr = jnp.zeros(c.shape, F32)
        for h in range(nh):
            b0 = 2 * HEAD * h
            oq_ref[:, b0:b0 + HEAD] = (dq_ref[:, b0:b0 + HEAD] * SM_SCALE).astype(BF16)
            oq_ref[:, b0 + HEAD:b0 + 2 * HEAD] = (
                unrot(dq_ref[:, b0 + HEAD:b0 + 2 * HEAD]) * SM_SCALE).astype(BF16)
            okv_ref[:, b0:b0 + HEAD] = (dk_ref[:, b0:b0 + HEAD] * LN2).astype(BF16)
            okv_ref[:, b0 + HEAD:b0 + 2 * HEAD] = jnp.zeros((ts, HEAD), BF16)
            dkr = dkr + dk_ref[:, b0 + HEAD:b0 + 2 * HEAD]
        okv_ref[:, wq:] = dv_ref[...].astype(BF16)
        okr_ref[...] = jnp.where(lane < ROPE, unrot(jnp.where(lane < ROPE, dkr * LN2, 0.0)), 0.0)

    return _pcall(body, name="mla_rope_bwd", grid=(s // ts,),
                  in_specs=[_rows(ts, wq), _rows(ts, wq), _rows(ts, nh * HEAD), _rows(ts, 1),
                            _full((1, LANE)), _full((1, LANE))],
                  out_specs=[_rows(ts, wq), _rows(ts, wq + nh * HEAD), _rows(ts, LANE)],
                  out_shape=[SDS((s, wq), BF16), SDS((s, wq + nh * HEAD), BF16), SDS((s, LANE), F32)])(
                      dqc, dkc, dv, pos, invf, sgn)


def _causal_mask(blk):
    i = lax.broadcasted_iota(jnp.int32, (blk, blk), 0)
    j = lax.broadcasted_iota(jnp.int32, (blk, blk), 1)
    return j <= i


MLA_HP = 2
MLA_FWD_HP = 4


def _pair_pack(a, b):
    return jnp.where(_lane_iota(a.shape) < LANE // 2, a, b)


def _pair_unpack(x, e):
    lane = _lane_iota(x.shape)
    keep = (lane < LANE // 2) if e == 0 else (lane >= LANE // 2)
    return jnp.where(keep, x, pltpu.roll(x, LANE // 2, 1))


def _mla_fwd(qc, kc, v, nh, rider):
    s = qc.shape[0]
    blk = _tile(s, MLA_BLOCK)
    nb = s // blk
    rep = blk // LANE
    hp = MLA_FWD_HP if nh % MLA_FWD_HP == 0 else MLA_HP
    assert nh % hp == 0 and hp % MLA_HP == 0
    once = pl.Buffered(1)
    r_in, r_out = len(rider.arrays), len(rider.out_shapes)

    def body(*refs):
        q_ref, k_ref, v_ref = refs[:3]
        o_ref, lse_ref = refs[3 + r_in:5 + r_in]
        m_sc, l_sc, acc = refs[5 + r_in + r_out:8 + r_in + r_out]
        r_refs = (refs[3:3 + r_in], refs[5 + r_in:5 + r_in + r_out], refs[8 + r_in + r_out:])
        i = pl.program_id(1)
        grid_step = pl.program_id(0) * nb + i
        _ride_begin(rider, r_refs, grid_step)
        m_sc[...] = jnp.full_like(m_sc, -1e30)
        l_sc[...] = jnp.zeros_like(l_sc)
        acc[...] = jnp.zeros_like(acc)

        def step(j, masked):
            rows = pl.ds(pl.multiple_of(j * blk, blk), blk)
            es = range(hp)
            sc = [_dot_nt(q_ref[:, 2 * HEAD * e:2 * HEAD * (e + 1)], k_ref[rows, 2 * HEAD * e:2 * HEAD * (e + 1)])
                  for e in es]
            if masked:
                sc = [jnp.where(_causal_mask(blk), x, -1e30) for x in sc]
            m_prev = [m_sc[e] for e in es]
            m_new = [jnp.maximum(m_prev[e], jnp.max(sc[e], axis=-1, keepdims=True)) for e in es]
            p = [jnp.exp2(sc[e] - jnp.tile(m_new[e], (1, rep))) for e in es]
            alpha = [jnp.exp2(m_prev[e] - m_new[e]) for e in es]
            pv = [_dot(p[e], v_ref[rows, HEAD * e:HEAD * (e + 1)]) for e in es]
            for e in es:
                l_sc[e] = alpha[e] * l_sc[e] + jnp.sum(p[e], axis=-1, keepdims=True)
                acc[e] = alpha[e] * acc[e] + pv[e]
                m_sc[e] = m_new[e]

        def loop_body(j, carry):
            step(j, False)
            return carry

        lax.fori_loop(0, i, loop_body, 0)
        step(i, True)
        for e in range(hp):
            o_ref[:, HEAD * e:HEAD * (e + 1)] = acc[e] / l_sc[e]
        lse = [m_sc[e] + jnp.log(l_sc[e]) * LOG2E for e in range(hp)]
        for g in range(hp // MLA_HP):
            lse_ref[g] = _pair_pack(lse[MLA_HP * g], lse[MLA_HP * g + 1])
        _ride_end(rider, r_refs, grid_step, (nh // hp) * nb)

    outs = _pcall(
        body, name="mla_fwd", grid=(nh // hp, nb),
        in_specs=[pl.BlockSpec((blk, hp * 2 * HEAD), lambda g, i: (i, g)),
                  pl.BlockSpec((s, hp * 2 * HEAD), lambda g, i: (0, g), pipeline_mode=once),
                  pl.BlockSpec((s, hp * HEAD), lambda g, i: (0, g), pipeline_mode=once)] + [_ANY] * r_in,
        out_specs=[pl.BlockSpec((blk, hp * HEAD), lambda g, i: (i, g)),
                   pl.BlockSpec((hp // MLA_HP, blk, LANE), lambda g, i: (g, i, 0))] + [_ANY] * r_out,
        out_shape=[SDS((s, nh * HEAD), F32), SDS((nh // MLA_HP, s, LANE), F32)] + rider.out_shapes,
        scratch=[pltpu.VMEM((hp, blk, LANE), F32), pltpu.VMEM((hp, blk, LANE), F32),
                 pltpu.VMEM((hp, blk, HEAD), F32)] + rider.scratch)(qc, kc, v, *rider.arrays)
    return outs[0], outs[1], outs[2:]


def _mla_bwd(qc, kc, v, do, lse, delta, nh, rider):
    s = qc.shape[0]
    blk = _tile(s, MLA_BLOCK)
    nb = s // blk
    rep = blk // LANE
    hp = MLA_HP
    once = pl.Buffered(1)
    r_in, r_out = len(rider.arrays), len(rider.out_shapes)
    qs = [slice(2 * HEAD * e, 2 * HEAD * (e + 1)) for e in range(hp)]
    vs = [slice(HEAD * e, HEAD * (e + 1)) for e in range(hp)]

    def body(*refs):
        q_ref, do_ref, lse_ref, dl_ref, k_ref, v_ref = refs[:6]
        dq_ref, dk_ref, dv_ref = refs[6 + r_in:9 + r_in]
        dk_acc, dv_acc = refs[9 + r_in + r_out:11 + r_in + r_out]
        r_refs = (refs[6:6 + r_in], refs[9 + r_in:9 + r_in + r_out], refs[11 + r_in + r_out:])
        j = pl.program_id(1)
        grid_step = pl.program_id(0) * nb + j
        _ride_begin(rider, r_refs, grid_step)

        @pl.when(j == 0)
        def _():
            dq_ref[...] = jnp.zeros_like(dq_ref)

        dk_acc[...] = jnp.zeros_like(dk_acc)
        dv_acc[...] = jnp.zeros_like(dv_acc)
        es = range(hp)
        kj = [k_ref[:, qs[e]] for e in es]
        vj = [v_ref[:, vs[e]] for e in es]

        def step(i, masked):
            rows = pl.ds(pl.multiple_of(i * blk, blk), blk)
            qi = [q_ref[rows, qs[e]] for e in es]
            doi = [do_ref[rows, vs[e]] for e in es]
            lse, dl = lse_ref[rows, :], dl_ref[rows, :]
            sc = [_dot_nt(qi[e], kj[e]) for e in es]
            dp = [_dot_nt(doi[e], vj[e]) for e in es]
            if masked:
                sc = [jnp.where(_causal_mask(blk), x, -1e30) for x in sc]
            p = [jnp.exp2(sc[e] - jnp.tile(_pair_unpack(lse, e), (1, rep))) for e in es]
            ds = [p[e] * (dp[e] - jnp.tile(_pair_unpack(dl, e), (1, rep))) for e in es]
            dv = [_dot_tn(p[e], doi[e]) for e in es]
            dk = [_dot_tn(ds[e], qi[e]) for e in es]
            dq = [_dot(ds[e], kj[e]) for e in es]
            for e in es:
                dv_acc[:, vs[e]] += dv[e]
                dk_acc[:, qs[e]] += dk[e]
                dq_ref[rows, qs[e]] += dq[e]

        def loop_body(i, carry):
            step(i, False)
            return carry

        step(j, True)
        lax.fori_loop(j + 1, nb, loop_body, 0)
        dk_ref[...] = dk_acc[...]
        dv_ref[...] = dv_acc[...]
        _ride_end(rider, r_refs, grid_step, (nh // hp) * nb)

    outs = _pcall(
        body, name="mla_bwd", grid=(nh // hp, nb),
        in_specs=[pl.BlockSpec((s, hp * 2 * HEAD), lambda g, j: (0, g), pipeline_mode=once),
                  pl.BlockSpec((s, hp * HEAD), lambda g, j: (0, g), pipeline_mode=once),
                  pl.BlockSpec((None, s, LANE), lambda g, j: (g, 0, 0), pipeline_mode=once),
                  pl.BlockSpec((None, s, LANE), lambda g, j: (g, 0, 0), pipeline_mode=once),
                  pl.BlockSpec((blk, hp * 2 * HEAD), lambda g, j: (j, g)),
                  pl.BlockSpec((blk, hp * HEAD), lambda g, j: (j, g))] + [_ANY] * r_in,
        out_specs=[pl.BlockSpec((s, hp * 2 * HEAD), lambda g, j: (0, g), pipeline_mode=once),
                   pl.BlockSpec((blk, hp * 2 * HEAD), lambda g, j: (j, g)),
                   pl.BlockSpec((blk, hp * HEAD), lambda g, j: (j, g))] + [_ANY] * r_out,
        out_shape=[SDS((s, nh * 2 * HEAD), F32), SDS((s, nh * 2 * HEAD), F32),
                   SDS((s, nh * HEAD), F32)] + rider.out_shapes,
        scratch=[pltpu.VMEM((blk, hp * 2 * HEAD), F32), pltpu.VMEM((blk, hp * HEAD), F32)] + rider.scratch,
        vmem=VMEM_LIMIT_WIDE)(qc, do, lse, delta, kc, v, *rider.arrays)
    return outs[0], outs[1], outs[2], outs[3:]


def _mix_fwd(og, proj, om, gw, mw, nh, z_col):
    s = og.shape[0]
    w = nh * HEAD
    ts = _tile(s, 256, 8)

    def body(og_ref, z_ref, om_ref, gw_ref, mw_ref, o_ref):
        for h in range(nh):
            hs = slice(HEAD * h, HEAD * (h + 1))
            a = og_ref[:, hs]
            r = lax.rsqrt(jnp.mean(a * a, axis=-1, keepdims=True) + EPS)
            o_ref[:, hs] = (a * r * gw_ref[...] * _silu(z_ref[:, hs])).astype(BF16)
            b = om_ref[:, hs]
            r = lax.rsqrt(jnp.mean(b * b, axis=-1, keepdims=True) + EPS)
            o_ref[:, w + HEAD * h:w + HEAD * (h + 1)] = (b * r * mw_ref[...]).astype(BF16)

    return _pcall(body, name="mix_fwd", grid=(s // ts,),
                  in_specs=[_rows(ts, w), _rows(ts, w, z_col), _rows(ts, w), _full((1, HEAD)),
                            _full((1, HEAD))],
                  out_specs=_rows(ts, 2 * w), out_shape=SDS((s, 2 * w), BF16))(og, proj, om, gw, mw)


def _mix_bwd(dmix, og, proj, om, gw, mw, nh, z_col):
    s = og.shape[0]
    w = nh * HEAD
    ts = _tile(s, 256, 8)

    def body(d_ref, og_ref, z_ref, om_ref, gw_ref, mw_ref, dog_ref, dz_ref, dom_ref, dgw_ref, dmw_ref,
             dl_ref):
        @pl.when(pl.program_id(0) == 0)
        def _():
            dgw_ref[...] = jnp.zeros_like(dgw_ref)
            dmw_ref[...] = jnp.zeros_like(dmw_ref)

        dgw = jnp.zeros((1, HEAD), F32)
        dmw = jnp.zeros((1, HEAD), F32)
        deltas = []
        for h in range(nh):
            hs = slice(HEAD * h, HEAD * (h + 1))
            a, z, dy = og_ref[:, hs], z_ref[:, hs], d_ref[:, hs]
            r = lax.rsqrt(jnp.mean(a * a, axis=-1, keepdims=True) + EPS)
            ah = a * r
            sz = _silu(z)
            dz_ref[:, hs] = (dy * (ah * gw_ref[...]) * _dsilu(z)).astype(BF16)
            dn = dy * sz
            dgw = dgw + jnp.sum(dn * ah, axis=0, keepdims=True)
            dah = dn * gw_ref[...]
            dog_ref[:, hs] = r * (dah - ah * jnp.mean(dah * ah, axis=-1, keepdims=True))
            b, dyb = om_ref[:, hs], d_ref[:, w + HEAD * h:w + HEAD * (h + 1)]
            r = lax.rsqrt(jnp.mean(b * b, axis=-1, keepdims=True) + EPS)
            bh = b * r
            dmw = dmw + jnp.sum(dyb * bh, axis=0, keepdims=True)
            dbh = dyb * mw_ref[...]
            dom = r * (dbh - bh * jnp.mean(dbh * bh, axis=-1, keepdims=True))
            dom_ref[:, hs] = dom.astype(BF16)
            deltas.append(jnp.broadcast_to(jnp.sum(dom * b, axis=-1, keepdims=True), (ts, LANE)))
        for g in range(nh // MLA_HP):
            dl_ref[g] = _pair_pack(deltas[2 * g], deltas[2 * g + 1])
        dgw_ref[...] += dgw
        dmw_ref[...] += dmw

    return _pcall(body, name="mix_bwd", grid=(s // ts,),
                  in_specs=[_rows(ts, 2 * w), _rows(ts, w), _rows(ts, w, z_col), _rows(ts, w),
                            _full((1, HEAD)), _full((1, HEAD))],
                  out_specs=[_rows(ts, w), _rows(ts, w, z_col), _rows(ts, w), _full((1, HEAD)), _full((1, HEAD)),
                             pl.BlockSpec((nh // MLA_HP, ts, LANE), lambda i: (0, i, 0))],
                  out_shape=[SDS((s, w), F32), SDS((s, proj.shape[1]), BF16), SDS((s, w), BF16),
                             SDS((1, HEAD), F32), SDS((1, HEAD), F32),
                             SDS((nh // MLA_HP, s, LANE), F32)])(dmix, og, proj, om, gw, mw)


def _swiglu_fwd(h2, wg, wu):
    m, kdim = h2.shape
    tn = wg.shape[2]
    n = 4 * tn
    tm, tk = _tile(m, 512), _tile(kdim, 2048)
    nk = kdim // tk

    def body(a_ref, g_ref, u_ref, act_ref, go_ref, uo_ref, gacc, uacc):
        k = pl.program_id(2)

        @pl.when(k == 0)
        def _():
            gacc[...] = jnp.zeros_like(gacc)
            uacc[...] = jnp.zeros_like(uacc)

        a = a_ref[...]
        gacc[...] += _dot(a, g_ref[...])
        uacc[...] += _dot(a, u_ref[...])

        @pl.when(k == nk - 1)
        def _():
            g, u = gacc[...], uacc[...]
            act_ref[...] = (_silu(g) * u).astype(BF16)
            go_ref[...] = g.astype(BF16)
            uo_ref[...] = u.astype(BF16)

    a_spec = pl.BlockSpec((tm, tk), lambda i, j, k: (i, k))
    b_spec = pl.BlockSpec((None, tk, tn), lambda i, j, k: (j, k, 0))
    o_spec = pl.BlockSpec((tm, tn), lambda i, j, k: (i, j))
    return _pcall(body, name="swiglu_fwd", grid=(m // tm, n // tn, nk),
                  in_specs=[a_spec, b_spec, b_spec], out_specs=[o_spec] * 3,
                  out_shape=[SDS((m, n), BF16)] * 3,
                  scratch=[pltpu.VMEM((tm, tn), F32), pltpu.VMEM((tm, tn), F32)])(h2, wg, wu)


def _swiglu_bwd(dx3, wd, g, u):
    m, kdim = dx3.shape
    n = wd.shape[0]
    tm, tn = _tile(m, 1024), _tile(n, 512)
    parts = 2 if tm % 16 == 0 else 1
    th = tm // parts

    def body(a_ref, b_ref, g_ref, u_ref, dg_ref, du_ref):
        b = b_ref[...]
        rows = [slice(th * c, th * (c + 1)) for c in range(parts)]
        da = [_dot_nt(a_ref[rs, :], b) for rs in rows]
        for rs, d in zip(rows, da):
            gv, uv = g_ref[rs, :].astype(F32), u_ref[rs, :].astype(F32)
            dg_ref[rs, :] = (d * uv * _dsilu(gv)).astype(BF16)
            du_ref[rs, :] = (d * _silu(gv)).astype(BF16)

    a_spec = pl.BlockSpec((tm, kdim), lambda i, j: (i, 0))
    b_spec = pl.BlockSpec((tn, kdim), lambda i, j: (j, 0))
    o_spec = pl.BlockSpec((tm, tn), lambda i, j: (i, j))
    return _pcall(body, name="swiglu_bwd", grid=(m // tm, n // tn),
                  in_specs=[a_spec, b_spec, o_spec, o_spec], out_specs=[o_spec] * 2,
                  out_shape=[SDS((m, n), BF16)] * 2)(dx3, wd, g, u)


def _sum_pair(g, recv, place, name):
    _, _, rh, c = g.shape
    tr = _tile(rh, 256, 16)

    def body(pl_ref, g_ref, r_ref, o16_ref, own_ref):
        sm = g_ref[...].astype(F32) + r_ref[...].astype(F32)
        o16_ref[...] = sm.astype(BF16)

        @pl.when(pl.program_id(1) == pl_ref[1])
        def _():
            own_ref[...] = sm

    grid_spec = pltpu.PrefetchScalarGridSpec(
        num_scalar_prefetch=1, grid=(rh // tr, 4),
        in_specs=[pl.BlockSpec((None, None, tr, c), lambda i, t, p: (t, p[0], i, 0)),
                  pl.BlockSpec((None, tr, c), lambda i, t, p: (t, i, 0))],
        out_specs=[pl.BlockSpec((None, tr, c), lambda i, t, p: (t, i, 0)),
                   pl.BlockSpec((tr, c), lambda i, t, p: (i, 0))])
    return pl.pallas_call(
        body, name=name, grid_spec=grid_spec,
        out_shape=[SDS((4, rh, c), BF16), SDS((rh, c), F32)],
        compiler_params=pltpu.CompilerParams(dimension_semantics=("arbitrary",) * 2,
                                             vmem_limit_bytes=VMEM_LIMIT))(place, g, recv)


def _sum_chips(own, recv, name):
    rh, c = own.shape
    tr = _tile(rh, 256, 16)

    def body(o_ref, r_ref, out_ref):
        acc = o_ref[...]
        for j in range(3):
            acc = acc + r_ref[j].astype(F32)
        out_ref[...] = acc

    return _pcall(body, name=name, grid=(rh // tr,),
                  in_specs=[_rows(tr, c), pl.BlockSpec((3, tr, c), lambda i: (0, i, 0))],
                  out_specs=_rows(tr, c), out_shape=SDS(own.shape, F32))(own, recv)


def _adamw_update(wv, gv, mv, vv):
    mn = ADAM_B1 * mv + (1.0 - ADAM_B1) * gv
    vn = ADAM_B2 * vv + (1.0 - ADAM_B2) * (gv * gv)
    m_hat = mn / (1.0 - ADAM_B1 ** ADAM_STEP)
    v_hat = vn / (1.0 - ADAM_B2 ** ADAM_STEP)
    return -ADAM_LR * (m_hat / (jnp.sqrt(v_hat) + ADAM_EPS) + ADAM_WD * wv), mn, vn


def _adamw(w, g, m, v, name):
    r, c = w.shape
    tr = _tile(r, 256, 8)

    def body(w_ref, g_ref, m_ref, v_ref, d_ref, mo_ref, vo_ref):
        d_ref[...], mo_ref[...], vo_ref[...] = _adamw_update(w_ref[...], g_ref[...], m_ref[...], v_ref[...])

    spec = _rows(tr, c)
    return _pcall(body, name=name, grid=(r // tr,), in_specs=[spec] * 4, out_specs=[spec] * 3,
                  out_shape=[SDS(w.shape, F32)] * 3)(w, g, m, v)


def _adamw_halves(w, mine, theirs, m, v, place, name):
    r, c = w.shape
    rh = r // 2
    tr = _tile(rh, 256, 8)
    nt = rh // tr

    def body(p_ref, w_ref, a_ref, b_ref, m_ref, v_ref, g_ref, d_ref, mo_ref, vo_ref):
        gv = jnp.where(pl.program_id(0) // nt == p_ref[0], a_ref[...], b_ref[...])
        g_ref[...] = gv
        d_ref[...], mo_ref[...], vo_ref[...] = _adamw_update(w_ref[...], gv, m_ref[...], v_ref[...])

    full = pl.BlockSpec((tr, c), lambda i, p: (i, 0))
    half = pl.BlockSpec((tr, c), lambda i, p: (i % nt, 0))
    grid_spec = pltpu.PrefetchScalarGridSpec(num_scalar_prefetch=1, grid=(2 * nt,),
                                             in_specs=[full, half, half, full, full], out_specs=[full] * 4)
    return pl.pallas_call(
        body, name=name, grid_spec=grid_spec, out_shape=[SDS(w.shape, F32)] * 4,
        compiler_params=pltpu.CompilerParams(dimension_semantics=("arbitrary",),
                                             vmem_limit_bytes=VMEM_LIMIT))(place, w, mine, theirs, m, v)


def _place():
    x, y, c = lax.axis_index("x"), lax.axis_index("y"), lax.axis_index("c")
    chips = [(1 - x, y), (x, 1 - y), (1 - x, 1 - y)]
    return x, y, c, chips


_ANY = pl.BlockSpec(memory_space=pl.ANY)


def _remote(src, dst, sems, k, to):
    return pltpu.make_async_remote_copy(src_ref=src, dst_ref=dst, send_sem=sems[0].at[k], recv_sem=sems[1].at[k],
                                        device_id=to, device_id_type=MESH)


class _Gather:
    def __init__(self, shards):
        n = len(shards)
        self.arrays = list(shards)
        self.out_shapes = [SDS((4,) + a.shape, a.dtype) for a in shards]
        self.scratch = [pltpu.SemaphoreType.DMA((7 * n,)), pltpu.SemaphoreType.DMA((7 * n,))]

    def _plan(self, ins, outs, sems):
        x, y, c, chips = _place()
        own, sib = 2 * x + y, (x, y, 1 - c)
        plan = []
        for wi, (w, o) in enumerate(zip(ins, outs)):
            rh = w.shape[0] // 2
            mine, theirs = pl.ds(c * rh, rh), pl.ds((1 - c) * rh, rh)
            whole = _remote(w, o.at[own], sems, 7 * wi + 6, sib)
            ici, d2d, d2d_in = [], [], []
            for j, (tx, ty) in enumerate(chips):
                t = 2 * tx + ty
                ici.append(_remote(w.at[mine], o.at[own, mine], sems, 7 * wi + j, (tx, ty, c)))
                d2d.append(_remote(o.at[t, mine], o.at[t, mine], sems, 7 * wi + 3 + j, sib))
                d2d_in.append(_remote(o.at[t, theirs], o.at[t, theirs], sems, 7 * wi + 3 + j, sib))
            plan.append((whole, ici, d2d, d2d_in))
        return plan

    def begin(self, ins, outs, sems):
        for whole, ici, _, _ in self._plan(ins, outs, sems):
            whole.start()
            for cp in ici:
                cp.start()

    def middle(self, ins, outs, sems):
        for _, ici, d2d, _ in self._plan(ins, outs, sems):
            for cp_in, cp_on in zip(ici, d2d):
                cp_in.wait_recv()
                cp_on.start()

    def finish(self, ins, outs, sems):
        for whole, ici, d2d, d2d_in in self._plan(ins, outs, sems):
            for cp in d2d_in:
                cp.wait_recv()
            for cp in ici + d2d:
                cp.wait_send()
            whole.wait()


class _Swap:
    def __init__(self, grads):
        n = len(grads)
        self.arrays = list(grads)
        self.out_shapes = [SDS((4,) + g.shape[2:], g.dtype) for g in grads]
        self.scratch = [pltpu.SemaphoreType.DMA((4 * n,)), pltpu.SemaphoreType.DMA((4 * n,))]

    def _plan(self, ins, outs, sems):
        x, y, c, _ = _place()
        return [_remote(g.at[t, 1 - c], o.at[t], sems, 4 * wi + t, (x, y, 1 - c))
                for wi, (g, o) in enumerate(zip(ins, outs)) for t in range(4)]

    def begin(self, ins, outs, sems):
        for cp in self._plan(ins, outs, sems):
            cp.start()

    def middle(self, ins, outs, sems):
        pass

    def finish(self, ins, outs, sems):
        for cp in self._plan(ins, outs, sems):
            cp.wait()


class _Exchange:
    def __init__(self, pieces):
        n = len(pieces)
        self.arrays = list(pieces)
        self.out_shapes = [SDS((3,) + p.shape[1:], p.dtype) for p in pieces]
        self.scratch = [pltpu.SemaphoreType.DMA((3 * n,)), pltpu.SemaphoreType.DMA((3 * n,))]

    def _plan(self, ins, outs, sems):
        x, y, c, chips = _place()
        return [_remote(g.at[2 * tx + ty], o.at[j], sems, 3 * wi + j, (tx, ty, c))
                for wi, (g, o) in enumerate(zip(ins, outs)) for j, (tx, ty) in enumerate(chips)]

    def begin(self, ins, outs, sems):
        for cp in self._plan(ins, outs, sems):
            cp.start()

    def middle(self, ins, outs, sems):
        pass

    def finish(self, ins, outs, sems):
        for cp in self._plan(ins, outs, sems):
            cp.wait()


class _Share:
    def __init__(self, totals):
        n = len(totals)
        self.arrays = list(totals)
        self.out_shapes = [SDS(t.shape, t.dtype) for t in totals]
        self.scratch = [pltpu.SemaphoreType.DMA((n,)), pltpu.SemaphoreType.DMA((n,))]

    def _plan(self, ins, outs, sems):
        x, y, c, _ = _place()
        return [_remote(t, o, sems, wi, (x, y, 1 - c)) for wi, (t, o) in enumerate(zip(ins, outs))]

    def begin(self, ins, outs, sems):
        for cp in self._plan(ins, outs, sems):
            cp.start()

    def middle(self, ins, outs, sems):
        pass

    def finish(self, ins, outs, sems):
        for cp in self._plan(ins, outs, sems):
            cp.wait()


def _ride_begin(rider, r_refs, step):
    @pl.when(step == 0)
    def _():
        rider.begin(*r_refs)


def _ride_end(rider, r_refs, step, nsteps):
    @pl.when(step == min(3 * nsteps // 4, nsteps - 1))
    def _():
        rider.middle(*r_refs)

    @pl.when(step == nsteps - 1)
    def _():
        rider.finish(*r_refs)


def _comm(rider, name):
    n_in, n_out = len(rider.arrays), len(rider.out_shapes)

    def body(*refs):
        r_refs = (refs[:n_in], refs[n_in:n_in + n_out], refs[n_in + n_out:])
        rider.begin(*r_refs)
        rider.middle(*r_refs)
        rider.finish(*r_refs)

    return pl.pallas_call(body, name=name, out_shape=rider.out_shapes, in_specs=[_ANY] * n_in,
                          out_specs=[_ANY] * n_out, scratch_shapes=rider.scratch)(*rider.arrays)


def _small_allreduce(pk, name):
    r = pk.shape[0]
    rels = [(dx, dy, dc) for dx in (0, 1) for dy in (0, 1) for dc in (0, 1) if dx or dy or dc]

    def body(p_ref, o_ref, buf, send_sems, recv_sems):
        x, y, c, _ = _place()
        me = 4 * x + 2 * y + c
        buf[me] = p_ref[...]
        cps = []
        for k, (dx, dy, dc) in enumerate(rels):
            to = (1 - x if dx else x, 1 - y if dy else y, 1 - c if dc else c)
            cps.append(pltpu.make_async_remote_copy(src_ref=p_ref, dst_ref=buf.at[me], send_sem=send_sems.at[k],
                                                    recv_sem=recv_sems.at[k], device_id=to,
                                                    device_id_type=MESH))
        for cpy in cps:
            cpy.start()
        for cpy in cps:
            cpy.wait()
        acc = buf[0]
        for d in range(1, 8):
            acc = acc + buf[d]
        o_ref[...] = acc

    vm = pl.BlockSpec(memory_space=pltpu.VMEM)
    return pl.pallas_call(body, name=name, out_shape=SDS(pk.shape, F32), in_specs=[vm], out_specs=vm,
                          scratch_shapes=[pltpu.VMEM((8, r, LANE), F32), pltpu.SemaphoreType.DMA((7,)),
                                          pltpu.SemaphoreType.DMA((7,))])(pk)


ATTN_W = ("w_in", "w_uq", "w_ukv", "w_out")
FFN_W = ("w_gate", "w_up", "w_down")
BIG = ATTN_W + FFN_W


def _cols_from_chips(g):
    return jnp.concatenate([g[t] for t in range(4)], axis=1)


def _cols_to_chips(full):
    r, n = full.shape
    return full.reshape(r, 4, n // 4).transpose(1, 0, 2).reshape(4, 2, r // 2, n // 4)


def _rows_to_chips(full):
    n, c = full.shape
    return full.reshape(4, 2, n // 8, c)


def _permute_w_in(w, nh):
    d = w.shape[0]
    g = 4 * nh * HEAD
    lr = (w.shape[1] - g - 2 * nh - ROPE) // 2
    o = g + 2 * nh
    pad = jnp.zeros((d, LANE - ROPE - 8 - nh), w.dtype)
    pad8 = jnp.zeros((d, 8 - nh), w.dtype)
    return jnp.concatenate([w[:, :g], w[:, o:o + 2 * lr], w[:, o + 2 * lr:], w[:, g:g + nh], pad8,
                            w[:, g + nh:g + 2 * nh], pad, jnp.zeros((d, LANE), w.dtype)], axis=1)


def _unpermute_w_in(wp, nh, lr):
    g = 4 * nh * HEAD
    mc = g + 2 * lr
    return jnp.concatenate([wp[:, :g], wp[:, mc + B_LANE:mc + B_LANE + nh], wp[:, mc + A_LANE:mc + A_LANE + nh],
                            wp[:, g:g + 2 * lr], wp[:, mc:mc + ROPE]], axis=1)


def _permute_w_uq(w, nh):
    lr = w.shape[0]
    w3 = w.reshape(lr, nh, HEAD + ROPE)
    return jnp.concatenate([w3, jnp.zeros((lr, nh, HEAD - ROPE), w.dtype)], axis=2).reshape(lr, nh * 2 * HEAD)


def _unpermute_w_uq(wp, nh):
    lr = wp.shape[0]
    return wp.reshape(lr, nh, 2 * HEAD)[:, :, :HEAD + ROPE].reshape(lr, nh * (HEAD + ROPE))


def _permute_w_ukv(w, nh):
    lr = w.shape[0]
    w3 = w.reshape(lr, nh, 2 * HEAD)
    kp = jnp.concatenate([w3[:, :, :HEAD], jnp.zeros((lr, nh, HEAD), w.dtype)], axis=2)
    return jnp.concatenate([kp.reshape(lr, nh * 2 * HEAD), w3[:, :, HEAD:].reshape(lr, nh * HEAD)], axis=1)


def _unpermute_w_ukv(wp, nh):
    lr = wp.shape[0]
    kp = wp[:, :nh * 2 * HEAD].reshape(lr, nh, 2 * HEAD)[:, :, :HEAD]
    vp = wp[:, nh * 2 * HEAD:].reshape(lr, nh, HEAD)
    return jnp.concatenate([kp, vp], axis=2).reshape(lr, nh * 2 * HEAD)


def _sum_pairs(grads, recv, place, tag):
    sums = [_sum_pair(g, r, place, "sum_pair_%s%d" % (tag, k)) for k, (g, r) in enumerate(zip(grads, recv))]
    return [s[0] for s in sums], [s[1] for s in sums]


def _reduce_end(own, recv, tag):
    return [_sum_chips(o, r, "sum_chips_%s%d" % (tag, k)) for k, (o, r) in enumerate(zip(own, recv))]


def _step(x, pos, tgt, w_in, attn_shards, ffn_shards, small, place):
    nh = small["a_log"].shape[1]
    lr = small["q_norm_w"].shape[1]
    w = nh * HEAD
    z_col, col_q, col_kv = 3, 4 * w // lr, 4 * w // lr + 1
    misc_c = 4 * w + 2 * lr
    misc_col = misc_c // LANE
    assert (4 * w) % lr == 0 and small["kv_norm_w"].shape[1] == lr

    zl = jnp.zeros((1, LANE), F32)
    alog_l = zl.at[:, A_LANE:A_LANE + nh].set(small["a_log"])
    dtb_l = zl.at[:, A_LANE:A_LANE + nh].set(small["dt_bias"])
    conv_w = small["conv_w"]

    h1, (in4,) = _norm_fwd(x, small["attn_norm_w"], "norm1", rider=_Gather([w_in]))
    win_p = _permute_w_in(_cols_from_chips(in4), nh)
    proj, (uq4, ukv4, out4) = _mm([(h1, win_p)], name="proj_in", rider=_Gather(attn_shards))
    wuq_p = _permute_w_uq(_cols_from_chips(uq4), nh)
    wukv_p = _permute_w_ukv(_cols_from_chips(ukv4), nh)
    w_out = out4.reshape(-1, out4.shape[2])
    gq, gk, gv, gb, gbt = _gdn_prep(proj, conv_w, alog_l, dtb_l, nh, misc_col)
    o_gdn, states = _gdn_fwd(gq, gk, gv, gb, gbt, nh)
    cqn, ckvn = _mla_norm(proj, small["q_norm_w"], small["kv_norm_w"], col_q, col_kv)
    qraw = _mm([(cqn, wuq_p)], name="proj_uq", out_dtype=BF16)
    kvraw = _mm([(ckvn, wukv_p)], name="proj_ukv", out_dtype=BF16)
    qc, kc, vv = _mla_rope(qraw, kvraw, proj, pos, nh, misc_col)
    o_mla, lse, (wg4, wu4, wd4) = _mla_fwd(qc, kc, vv, nh, _Gather(ffn_shards))
    w_down = wd4.reshape(-1, wd4.shape[2])
    mixed = _mix_fwd(o_gdn, proj, o_mla, small["gdn_norm_w"], small["mla_out_norm_w"], nh, z_col)
    x2 = _mm([(mixed, w_out)], name="proj_out", res=x)
    h2 = _norm_fwd(x2, small["ffn_norm_w"], "norm2")
    act, gpre, upre = _swiglu_fwd(h2, wg4, wu4)
    x3 = _mm([(act, w_down)], name="proj_down", res=x2, tk=2816)
    dx3, d_final, loss, dx3h = _final_loss(x3, tgt, small["final_norm_w"])

    gs = {"final_norm_w": d_final}
    dgate, dup = _swiglu_bwd(dx3h, w_down, gpre, upre)
    g_down = _rows_to_chips(_mm([(act, dx3h)], name="dw_down", ta=True, out_dtype=BF16))
    g_gate = _mm([(h2, dgate)], name="dw_gate", ta=True, out_dtype=BF16, out_chips=True)
    g_up = _mm([(h2, dup)], name="dw_up", ta=True, out_dtype=BF16, out_chips=True)
    halves = lambda g: g.reshape(4, 2, g.shape[1] // 2, g.shape[2])
    ffn_g = [halves(g_gate), halves(g_up), g_down]
    dh2, ffn_sib = _mm([(dgate, wg4), (dup, wu4)], name="dh2", tb=True, b_chips=True, out_dtype=BF16,
                       rider=_Swap(ffn_g))
    ffn16, ffn_own = _sum_pairs(ffn_g, ffn_sib, place, "ffn")
    dx2, gs["ffn_norm_w"], dx2h = _norm_bwd(dh2, x2, small["ffn_norm_w"], dx3, "norm2_bwd", True)
    dmix = _mm([(dx2h, w_out)], name="dmix", tb=True, out_dtype=BF16)
    g_out = _rows_to_chips(_mm([(mixed, dx2h)], name="dw_out", ta=True, out_dtype=BF16))
    d_ogdn, dproj, d_omla, gs["gdn_norm_w"], gs["mla_out_norm_w"], delta = _mix_bwd(
        dmix, o_gdn, proj, o_mla, small["gdn_norm_w"], small["mla_out_norm_w"], nh, z_col)
    dqc, dkc, dvv, ffn_recv = _mla_bwd(qc, kc, vv, d_omla, lse, delta, nh, _Exchange(ffn16))
    ffn_tot = _reduce_end(ffn_own, ffn_recv, "ffn")
    dqraw, dkvraw, dkr = _mla_rope_bwd(dqc, dkc, dvv, pos, nh)
    dcqn = _mm([(dqraw, wuq_p)], name="dcqn", tb=True)
    dckvn = _mm([(dkvraw, wukv_p)], name="dckvn", tb=True)
    g_uq = _cols_to_chips(_unpermute_w_uq(_mm([(cqn, dqraw)], name="dw_uq", ta=True, out_dtype=BF16), nh))
    g_ukv = _cols_to_chips(_unpermute_w_ukv(_mm([(ckvn, dkvraw)], name="dw_ukv", ta=True, out_dtype=BF16), nh))
    dproj, gs["q_norm_w"], gs["kv_norm_w"] = _mla_norm_bwd(
        proj, small["q_norm_w"], small["kv_norm_w"], dcqn, dckvn, dproj, col_q, col_kv)
    (dgq, dgk, dgv, dgb), ffn_shared = _gdn_bwd(gq, gk, gv, gb, gbt, states, d_ogdn, nh, _Share(ffn_tot))
    dconv, dproj, gs["conv_w"], dal, ddb = _gdn_prep_bwd(
        proj, conv_w, alog_l, dtb_l, dgq, dgk, dgv, dgb, dkr, dproj, nh, misc_col)
    gs["a_log"] = dal[:, A_LANE:A_LANE + nh]
    gs["dt_bias"] = ddb[:, A_LANE:A_LANE + nh]
    dproj = _conv_bwd_input(dconv, conv_w, dproj)
    g_in = _cols_to_chips(_unpermute_w_in(_mm([(h1, dproj)], name="dw_in", ta=True, out_dtype=BF16), nh, lr))
    att_g = [g_in, g_uq, g_ukv, g_out]
    att16, att_own = _sum_pairs(att_g, _comm(_Swap(att_g), "swap_att"), place, "att")
    dh1, att_recv = _mm([(dproj, win_p)], name="dh1", tb=True, out_dtype=BF16, rider=_Exchange(att16))
    att_tot = _reduce_end(att_own, att_recv, "att")
    att_shared = _comm(_Share(att_tot), "share_att")
    grad_x, gs["attn_norm_w"] = _norm_bwd(dh1, x, small["attn_norm_w"], dx2, "norm1_bwd", False)
    return loss, grad_x, att_tot + ffn_tot, list(att_shared) + list(ffn_shared), gs


SMALL = ("attn_norm_w", "ffn_norm_w", "final_norm_w", "q_norm_w", "kv_norm_w", "gdn_norm_w",
         "mla_out_norm_w", "a_log", "dt_bias")
WEIGHTS = ("attn_norm_w", "w_in", "conv_w", "a_log", "dt_bias", "gdn_norm_w", "q_norm_w", "w_uq",
           "kv_norm_w", "w_ukv", "mla_out_norm_w", "w_out", "ffn_norm_w", "w_gate", "w_up", "w_down",
           "final_norm_w")


def _pack_small(vecs):
    flat = jnp.concatenate([v.astype(F32).reshape(-1) for v in vecs])
    pad = (-flat.shape[0]) % (8 * LANE)
    return jnp.concatenate([flat, jnp.zeros((pad,), F32)]).reshape(-1, LANE)


def kernel(x, positions, attn_norm_w, w_in, conv_w, a_log, dt_bias, gdn_norm_w, q_norm_w, w_uq, kv_norm_w, w_ukv, mla_out_norm_w, w_out, ffn_norm_w, w_gate, w_up, w_down, final_norm_w, loss_target, m_attn_norm_w, m_w_in, m_conv_w, m_a_log, m_dt_bias, m_gdn_norm_w, m_q_norm_w, m_w_uq, m_kv_norm_w, m_w_ukv, m_mla_out_norm_w, m_w_out, m_ffn_norm_w, m_w_gate, m_w_up, m_w_down, m_final_norm_w, v_attn_norm_w, v_w_in, v_conv_w, v_a_log, v_dt_bias, v_gdn_norm_w, v_q_norm_w, v_w_uq, v_kv_norm_w, v_w_ukv, v_mla_out_norm_w, v_w_out, v_ffn_norm_w, v_w_gate, v_w_up, v_w_down, v_final_norm_w):
    args = dict(locals())
    xi, yi, ci = lax.axis_index("x"), lax.axis_index("y"), lax.axis_index("c")
    chip = 2 * xi + yi

    def two_d(a):
        return a.reshape(a.shape[-2:]) if a.ndim >= 2 else a.reshape(1, -1)

    wloc = {n: two_d(args[n]) for n in WEIGHTS}
    mloc = {n: two_d(args["m_" + n]) for n in WEIGHTS}
    vloc = {n: two_d(args["v_" + n]) for n in WEIGHTS}

    cw = wloc["conv_w"]
    cshard = cw.shape[1]
    cfull = jnp.zeros((CONV, 4 * cshard), F32)
    cfull = lax.dynamic_update_slice(cfull, jnp.where(ci == 0, cw, 0.0), (0, chip * cshard))
    conv_full = _small_allreduce(_pack_small([cfull]), "gather_conv_w").reshape(-1)[:CONV * 4 * cshard]
    conv_full = conv_full.reshape(CONV, 4 * cshard)

    small = {n: wloc[n] for n in SMALL}
    small["conv_w"] = conv_full

    pos = positions.reshape(-1, 1).astype(F32)
    place = jnp.stack([ci, chip]).astype(jnp.int32)
    loss, grad_x, totals, from_sib, gs = _step(
        two_d(x), pos, two_d(loss_target), wloc["w_in"].astype(BF16), [wloc[n].astype(BF16) for n in ATTN_W[1:]],
        [wloc[n].astype(BF16) for n in FFN_W], small, place)

    small_names = SMALL + ("conv_w",)
    pk = _pack_small([gs[n] for n in small_names] + [loss])
    red = _small_allreduce(pk, "reduce_small").reshape(-1)
    gsm, off = {}, 0
    for n in small_names:
        shp = gs[n].shape
        gsm[n] = red[off:off + shp[0] * shp[1]].reshape(shp)
        off += shp[0] * shp[1]
    loss_out = red[off]
    gsm["conv_w"] = lax.dynamic_slice(gsm["conv_w"], (0, chip * cshard), (CONV, cshard))

    grads, deltas, new_m, new_v = {}, {}, {}, {}
    for n, mine, theirs in zip(BIG, totals, from_sib):
        grads[n], deltas[n], new_m[n], new_v[n] = _adamw_halves(wloc[n], mine, theirs, mloc[n], vloc[n], place,
                                                                "adamw_" + n)
    grads["conv_w"] = gsm["conv_w"]
    deltas["conv_w"], new_m["conv_w"], new_v["conv_w"] = _adamw(wloc["conv_w"], gsm["conv_w"], mloc["conv_w"],
                                                                vloc["conv_w"], "adamw_conv_w")
    sm_shapes = [wloc[n].shape for n in SMALL]
    pd, pm, pv = _adamw(_pack_small([wloc[n] for n in SMALL]), _pack_small([gsm[n] for n in SMALL]),
                        _pack_small([mloc[n] for n in SMALL]), _pack_small([vloc[n] for n in SMALL]),
                        "adamw_small")
    for dst, packed in ((deltas, pd), (new_m, pm), (new_v, pv)):
        flat, off = packed.reshape(-1), 0
        for n, shp in zip(SMALL, sm_shapes):
            dst[n] = flat[off:off + shp[0] * shp[1]].reshape(shp)
            off += shp[0] * shp[1]
    for n in SMALL:
        grads[n] = gsm[n]

    def like(n, a):
        return a.reshape(args[n].shape)

    outs = [loss_out.reshape(()), grad_x.reshape(x.shape)]
    for group in (grads, deltas, new_m, new_v):
        outs += [like(n, group[n]) for n in WEIGHTS]
    return tuple(outs)
```

```python
import functools

import jax
import jax.numpy as jnp
from jax import lax
from jax.experimental import pallas as pl
from jax.experimental.pallas import tpu as pltpu

F32, BF16 = jnp.float32, jnp.bfloat16
SDS = jax.ShapeDtypeStruct
MESH = pl.DeviceIdType.MESH

HEAD = 128
ROPE = 64
CHUNK = 64
PAIR = 2 * CHUNK
CONV = 4
EPS = 1e-6
ROPE_THETA = 10000.0
LANE = 128
B_LANE = 64
A_LANE = 72
VMEM_LIMIT = 48 * 1024 * 1024
VMEM_LIMIT_WIDE = 56 * 1024 * 1024
MLA_BLOCK = 512
LOG2E = 1.4426950408889634
LN2 = 0.6931471805599453
SM_SCALE = (HEAD + ROPE) ** -0.5

ADAM_LR = 0.001
ADAM_B1 = 0.9
ADAM_B2 = 0.999
ADAM_EPS = 1e-08
ADAM_WD = 0.01
ADAM_STEP = 10


def _tile(n, pref, mult=LANE):
    if n <= pref:
        return n
    t = (pref // mult) * mult
    while t >= mult:
        if n % t == 0:
            return t
        t -= mult
    return n


def _pcall(body, *, name, grid, in_specs, out_specs, out_shape, scratch=(), vmem=VMEM_LIMIT, aliases=None):
    return pl.pallas_call(
        body, name=name, grid=grid, in_specs=in_specs, out_specs=out_specs,
        out_shape=out_shape, scratch_shapes=list(scratch), input_output_aliases=aliases or {},
        compiler_params=pltpu.CompilerParams(
            dimension_semantics=("arbitrary",) * len(grid), vmem_limit_bytes=vmem))


def _pcall_riding(core, rider, *, name, grid, in_specs, out_specs, out_shape, args, scratch=()):
    n_in, n_out, n_scr = len(in_specs), len(out_specs), len(scratch)
    r_in, r_out = len(rider.arrays), len(rider.out_shapes)

    def body(*refs):
        ins, refs = refs[:n_in], refs[n_in:]
        r_ins, refs = refs[:r_in], refs[r_in:]
        outs, refs = refs[:n_out], refs[n_out:]
        r_outs, refs = refs[:r_out], refs[r_out:]
        scr, sems = refs[:n_scr], refs[n_scr:]
        r_refs = (r_ins, r_outs, sems)
        _ride_begin(rider, r_refs, pl.program_id(0))
        core(*ins, *outs, *scr)
        _ride_end(rider, r_refs, pl.program_id(0), grid[0])

    res = _pcall(body, name=name, grid=grid, in_specs=list(in_specs) + [_ANY] * r_in,
                 out_specs=list(out_specs) + [_ANY] * r_out, out_shape=list(out_shape) + rider.out_shapes,
                 scratch=list(scratch) + rider.scratch)(*args, *rider.arrays)
    return res[:n_out], res[n_out:]


def _rows(ts, width, col=0):
    return pl.BlockSpec((ts, width), lambda i: (i, col))


def _full(shape):
    nd = len(shape)
    return pl.BlockSpec(shape, lambda i: (0,) * nd)


def _dot(a, b):
    return jnp.dot(a.astype(BF16), b.astype(BF16), preferred_element_type=F32)


def _dot_nt(a, b):
    return lax.dot_general(a.astype(BF16), b.astype(BF16), (((1,), (1,)), ((), ())),
                           preferred_element_type=F32)


def _dot_tn(a, b):
    return lax.dot_general(a.astype(BF16), b.astype(BF16), (((0,), (0,)), ((), ())),
                           preferred_element_type=F32)


def _sigmoid(x):
    return 1.0 / (1.0 + jnp.exp(-x))


def _silu(x):
    return x * _sigmoid(x)


def _dsilu(x):
    s = _sigmoid(x)
    return s * (1.0 + x * (1.0 - s))


def _lane_iota(shape):
    return lax.broadcasted_iota(jnp.int32, shape, len(shape) - 1)


def _col(block, idx):
    return jnp.sum(jnp.where(_lane_iota(block.shape) == idx, block, 0.0), axis=-1, keepdims=True)


def _mm(pairs, *, name, ta=False, tb=False, out_dtype=F32, res=None, tm=1024, tn=1024, tk=2048,
        b_chips=False, out_chips=False, rider=None):
    a0, b0 = pairs[0]
    if ta:
        kdim, m = a0.shape
    else:
        m, kdim = a0.shape
    if b_chips and tb:
        n, tk = b0.shape[1], b0.shape[2]
        assert kdim == 4 * tk
    elif b_chips:
        n, tn = 4 * b0.shape[2], b0.shape[2]
        assert kdim == b0.shape[1]
    else:
        n = b0.shape[0] if tb else b0.shape[1]
    if out_chips:
        tn = n // 4
    tm = _tile(m, tm)
    tn = tn if (out_chips or (b_chips and not tb)) else _tile(n, tn)
    tk = tk if (b_chips and tb) else _tile(kdim, tk)
    assert m % tm == 0 and n % tn == 0 and kdim % tk == 0
    nk, npair = kdim // tk, len(pairs)
    grid = (m // tm, n // tn, nk)
    dims = (((0 if ta else 1,), (1 if tb else 0,)), ((), ()))
    n_in = 2 * npair + (res is not None)
    r_in, r_out = (len(rider.arrays), len(rider.out_shapes)) if rider else (0, 0)

    def body(*refs):
        o_ref = refs[n_in + r_in]
        acc = refs[n_in + r_in + 1 + r_out]
        k = pl.program_id(2)
        if rider:
            r_refs = (refs[n_in:n_in + r_in], refs[n_in + r_in + 1:n_in + r_in + 1 + r_out],
                      refs[n_in + r_in + 2 + r_out:])
            step = (pl.program_id(0) * grid[1] + pl.program_id(1)) * nk + k
            _ride_begin(rider, r_refs, step)

        @pl.when(k == 0)
        def _():
            acc[...] = jnp.zeros_like(acc)

        tot = None
        for p in range(npair):
            d = lax.dot_general(refs[2 * p][...].astype(BF16), refs[2 * p + 1][...].astype(BF16),
                                dims, preferred_element_type=F32)
            tot = d if tot is None else tot + d
        acc[...] += tot

        @pl.when(k == nk - 1)
        def _():
            r = acc[...]
            if res is not None:
                r = r + refs[2 * npair][...]
            o_ref[...] = r.astype(out_dtype)

        if rider:
            _ride_end(rider, r_refs, step, grid[0] * grid[1] * nk)

    if ta:
        a_spec = pl.BlockSpec((tk, tm), lambda i, j, k: (k, i))
    else:
        a_spec = pl.BlockSpec((tm, tk), lambda i, j, k: (i, k))
    if b_chips and tb:
        b_spec = pl.BlockSpec((None, tn, tk), lambda i, j, k: (k, j, 0))
    elif b_chips:
        b_spec = pl.BlockSpec((None, tk, tn), lambda i, j, k: (j, k, 0))
    elif tb:
        b_spec = pl.BlockSpec((tn, tk), lambda i, j, k: (j, k))
    else:
        b_spec = pl.BlockSpec((tk, tn), lambda i, j, k: (k, j))
    if out_chips:
        o_spec = pl.BlockSpec((None, tm, tn), lambda i, j, k: (j, i, 0))
        o_shape = SDS((4, m, tn), out_dtype)
    else:
        o_spec = pl.BlockSpec((tm, tn), lambda i, j, k: (i, j))
        o_shape = SDS((m, n), out_dtype)
    in_specs, args = [], []
    for a, b in pairs:
        in_specs += [a_spec, b_spec]
        args += [a, b]
    if res is not None:
        in_specs.append(o_spec)
        args.append(res)
    out_specs, out_shapes, scratch = [o_spec], [o_shape], [pltpu.VMEM((tm, tn), F32)]
    if rider:
        in_specs += [_ANY] * r_in
        args += rider.arrays
        out_specs += [_ANY] * r_out
        out_shapes += rider.out_shapes
        scratch += rider.scratch
    outs = _pcall(body, name=name, grid=grid, in_specs=in_specs, out_specs=out_specs, out_shape=out_shapes,
                  scratch=scratch)(*args)
    return (outs[0], outs[1:]) if rider else outs[0]


def _norm_fwd(x, w, name, rider=None):
    s, d = x.shape
    ts = _tile(s, 512, 8)

    def body(x_ref, w_ref, h_ref):
        xv = x_ref[...]
        r = lax.rsqrt(jnp.mean(xv * xv, axis=-1, keepdims=True) + EPS)
        h_ref[...] = (xv * r * w_ref[...]).astype(BF16)

    spec = dict(name=name, grid=(s // ts,), in_specs=[_rows(ts, d), _full((1, d))])
    if rider is None:
        return _pcall(body, out_specs=_rows(ts, d), out_shape=SDS((s, d), BF16), **spec)(x, w)
    outs, r_outs = _pcall_riding(body, rider, out_specs=[_rows(ts, d)], out_shape=[SDS((s, d), BF16)],
                                 args=(x, w), **spec)
    return outs[0], r_outs


def _norm_bwd(dh, x, w, dres, name, with_bf16):
    s, d = x.shape
    ts = _tile(s, 256, 8)

    def body(dh_ref, x_ref, w_ref, dres_ref, dx_ref, dw_ref, *dx16_ref):
        @pl.when(pl.program_id(0) == 0)
        def _():
            dw_ref[...] = jnp.zeros_like(dw_ref)

        xv, dhv = x_ref[...], dh_ref[...]
        r = lax.rsqrt(jnp.mean(xv * xv, axis=-1, keepdims=True) + EPS)
        xh = xv * r
        dw_ref[...] += jnp.sum(dhv * xh, axis=0, keepdims=True)
        dxh = dhv * w_ref[...]
        dx = dres_ref[...] + r * (dxh - xh * jnp.mean(dxh * xh, axis=-1, keepdims=True))
        dx_ref[...] = dx
        for ref in dx16_ref:
            ref[...] = dx.astype(BF16)

    extra = 1 if with_bf16 else 0
    return _pcall(body, name=name, grid=(s // ts,),
                  in_specs=[_rows(ts, d), _rows(ts, d), _full((1, d)), _rows(ts, d)],
                  out_specs=[_rows(ts, d), _full((1, d))] + [_rows(ts, d)] * extra,
                  out_shape=[SDS((s, d), F32), SDS((1, d), F32)] + [SDS((s, d), BF16)] * extra)(
                      dh, x, w, dres)


def _final_loss(x3, tgt, w):
    s, d = x3.shape
    ts = _tile(s, 256, 8)

    def body(x_ref, t_ref, w_ref, dx_ref, dw_ref, loss_ref, dx16_ref):
        @pl.when(pl.program_id(0) == 0)
        def _():
            dw_ref[...] = jnp.zeros_like(dw_ref)
            loss_ref[...] = jnp.zeros_like(loss_ref)

        xv, wv = x_ref[...], w_ref[...]
        r = lax.rsqrt(jnp.mean(xv * xv, axis=-1, keepdims=True) + EPS)
        xh = xv * r
        err = xh * wv - t_ref[...]
        row = jnp.mean(err * err, axis=-1, keepdims=True)
        loss_ref[...] += 0.5 * jnp.sum(row, axis=0, keepdims=True)
        dy = err * (1.0 / d)
        dw_ref[...] += jnp.sum(dy * xh, axis=0, keepdims=True)
        dxh = dy * wv
        dx = r * (dxh - xh * jnp.mean(dxh * xh, axis=-1, keepdims=True))
        dx_ref[...] = dx
        dx16_ref[...] = dx.astype(BF16)

    return _pcall(body, name="final_loss", grid=(s // ts,),
                  in_specs=[_rows(ts, d), _rows(ts, d), _full((1, d))],
                  out_specs=[_rows(ts, d), _full((1, d)), _full((1, 1)), _rows(ts, d)],
                  out_shape=[SDS((s, d), F32), SDS((1, d), F32), SDS((1, 1), F32), SDS((s, d), BF16)])(
                      x3, tgt, w)


def _shift_down(cur, halo, s):
    if s == 0:
        return cur
    row8 = lax.broadcasted_iota(jnp.int32, halo.shape, 0)
    r = pltpu.roll(cur, s, 0)
    top = jnp.where(row8 < s, pltpu.roll(halo, s, 0), r[0:8])
    return jnp.concatenate([top, r[8:]], axis=0)


def _shift_up(cur, halo, s):
    if s == 0:
        return cur
    ts = cur.shape[0]
    row8 = lax.broadcasted_iota(jnp.int32, halo.shape, 0)
    r = pltpu.roll(cur, ts - s, 0)
    bot = jnp.where(row8 >= 8 - s, pltpu.roll(halo, 8 - s, 0), r[ts - 8:ts])
    return jnp.concatenate([r[:ts - 8], bot], axis=0)


def _chunk_tri(ts, upper):
    i = lax.broadcasted_iota(jnp.int32, (ts, ts), 0)
    j = lax.broadcasted_iota(jnp.int32, (ts, ts), 1)
    same = jnp.right_shift(i, 6) == jnp.right_shift(j, 6)
    return jnp.where(same & ((j >= i) if upper else (j <= i)), 1.0, 0.0).astype(F32)


def _gate_values(m, alog, dtb):
    lane = _lane_iota(m.shape)
    beta = _sigmoid(m)
    xg = m + dtb
    sp = jnp.maximum(xg, 0.0) + jnp.log(1.0 + jnp.exp(-jnp.abs(xg)))
    ga = (lane >= A_LANE) & (lane < A_LANE + 8)
    g = jnp.where(ga, -jnp.exp(alog) * sp, 0.0)
    return beta, g, xg, ga


def _l2_heads(a, nh, scale):
    outs, rs = [], []
    for h in range(nh):
        ah = a[:, HEAD * h:HEAD * (h + 1)]
        r = lax.rsqrt(jnp.sum(ah * ah, axis=-1, keepdims=True) + EPS)
        outs.append(ah * (r * scale))
        rs.append(r)
    return jnp.concatenate(outs, axis=-1), rs


def _gdn_prep(proj, conv_w, alog_l, dtb_l, nh, misc_col):
    s = proj.shape[0]
    w = nh * HEAD
    ts = _tile(s, 256, PAIR)
    hb = ts // 8

    def body(cur_ref, halo_ref, misc_ref, cw_ref, al_ref, db_ref, q_ref, k_ref, v_ref, gb_ref, gbt_ref):
        first = pl.program_id(0) == 0
        outs = (q_ref, k_ref, v_ref)
        for sec in range(3):
            cs = slice(sec * w, (sec + 1) * w)
            cur = cur_ref[:, cs]
            halo = jnp.where(first, 0.0, halo_ref[:, cs])
            pre = None
            for j in range(CONV):
                term = cw_ref[j:j + 1, cs] * _shift_down(cur, halo, CONV - 1 - j)
                pre = term if pre is None else pre + term
            act = _silu(pre)
            if sec == 0:
                act, _ = _l2_heads(act, nh, HEAD ** -0.5)
            elif sec == 1:
                act, _ = _l2_heads(act, nh, 1.0)
            outs[sec][...] = act
        m = misc_ref[...]
        lane = _lane_iota(m.shape)
        beta, g, _, ga = _gate_values(m, al_ref[...], db_ref[...])
        gcc = jnp.dot(_chunk_tri(ts, False), g, precision=lax.Precision.HIGHEST,
                      preferred_element_type=F32)
        gb = jnp.where((lane >= B_LANE) & (lane < B_LANE + 8), beta, jnp.where(ga, gcc, 0.0))
        gb_ref[...] = gb
        gbt_ref[...] = gb.T

    return _pcall(
        body, name="gdn_prep", grid=(s // ts,),
        in_specs=[_rows(ts, 3 * w),
                  pl.BlockSpec((8, 3 * w), lambda i: (jnp.maximum(i * hb - 1, 0), 0)),
                  _rows(ts, LANE, misc_col), _full((CONV, 3 * w)), _full((1, LANE)), _full((1, LANE))],
        out_specs=[_rows(ts, w), _rows(ts, w), _rows(ts, w), _rows(ts, LANE),
                   pl.BlockSpec((LANE, ts), lambda i: (0, i))],
        out_shape=[SDS((s, w), F32), SDS((s, w), F32), SDS((s, w), F32), SDS((s, LANE), F32),
                   SDS((LANE, s), F32)])(proj, proj, proj, conv_w, alog_l, dtb_l)


def _gdn_prep_bwd(proj, conv_w, alog_l, dtb_l, dq, dk, dv, dgb, dkr, dproj, nh, misc_col):
    s = proj.shape[0]
    w = nh * HEAD
    ts = _tile(s, 256, PAIR)
    hb = ts // 8
    assert misc_col % 2 == 0

    def body(cur_ref, halo_ref, misc_ref, cw_ref, al_ref, db_ref, dq_ref, dk_ref, dv_ref, dgb_ref,
             dkr_ref, _, dc_ref, dm_ref, dcw_ref, dal_ref, ddb_ref):
        first = pl.program_id(0) == 0

        @pl.when(first)
        def _():
            dcw_ref[...] = jnp.zeros_like(dcw_ref)
            dal_ref[...] = jnp.zeros_like(dal_ref)
            ddb_ref[...] = jnp.zeros_like(ddb_ref)

        dins = (dq_ref, dk_ref, dv_ref)
        for sec in range(3):
            cs = slice(sec * w, (sec + 1) * w)
            cur = cur_ref[:, cs]
            halo = jnp.where(first, 0.0, halo_ref[:, cs])
            us = [_shift_down(cur, halo, CONV - 1 - j) for j in range(CONV)]
            pre = None
            for j in range(CONV):
                term = cw_ref[j:j + 1, cs] * us[j]
                pre = term if pre is None else pre + term
            act = _silu(pre)
            dout = dins[sec][...]
            if sec < 2:
                scale = HEAD ** -0.5 if sec == 0 else 1.0
                parts = []
                for h in range(nh):
                    hs = slice(HEAD * h, HEAD * (h + 1))
                    ah = act[:, hs]
                    r = lax.rsqrt(jnp.sum(ah * ah, axis=-1, keepdims=True) + EPS)
                    ahat = ah * r
                    dy = dout[:, hs]
                    parts.append((scale * r) * (dy - ahat * jnp.sum(dy * ahat, axis=-1, keepdims=True)))
                dact = jnp.concatenate(parts, axis=-1)
            else:
                dact = dout
            dconv = dact * _dsilu(pre)
            dc_ref[:, cs] = dconv
            for j in range(CONV):
                dcw_ref[j:j + 1, cs] += jnp.sum(dconv * us[j], axis=0, keepdims=True)
        m = misc_ref[...]
        lane = _lane_iota(m.shape)
        al = al_ref[...]
        beta, g, xg, ga = _gate_values(m, al, db_ref[...])
        dgbv = dgb_ref[...]
        dg = jnp.dot(_chunk_tri(ts, True), jnp.where(ga, dgbv, 0.0), precision=lax.Precision.HIGHEST,
                     preferred_element_type=F32)
        da_raw = jnp.where(ga, dg * (-jnp.exp(al)) * _sigmoid(xg), 0.0)
        db_raw = jnp.where((lane >= B_LANE) & (lane < B_LANE + 8), dgbv * beta * (1.0 - beta), 0.0)
        dal_ref[...] += jnp.sum(dg * g, axis=0, keepdims=True)
        ddb_ref[...] += jnp.sum(da_raw, axis=0, keepdims=True)
        dm_ref[:, :LANE] = (dkr_ref[...] + da_raw + db_raw).astype(BF16)
        dm_ref[:, LANE:] = jnp.zeros((ts, LANE), BF16)

    return _pcall(
        body, name="gdn_prep_bwd", grid=(s // ts,),
        in_specs=[_rows(ts, 3 * w),
                  pl.BlockSpec((8, 3 * w), lambda i: (jnp.maximum(i * hb - 1, 0), 0)),
                  _rows(ts, LANE, misc_col), _full((CONV, 3 * w)), _full((1, LANE)), _full((1, LANE)),
                  _rows(ts, w), _rows(ts, w), _rows(ts, w), _rows(ts, LANE), _rows(ts, LANE), _ANY],
        out_specs=[_rows(ts, 3 * w), _rows(ts, 2 * LANE, misc_col // 2), _full((CONV, 3 * w)), _full((1, LANE)),
                   _full((1, LANE))],
        out_shape=[SDS((s, 3 * w), F32), SDS(dproj.shape, BF16), SDS((CONV, 3 * w), F32),
                   SDS((1, LANE), F32), SDS((1, LANE), F32)], aliases={11: 1})(
                       proj, proj, proj, conv_w, alog_l, dtb_l, dq, dk, dv, dgb, dkr, dproj)


def _conv_bwd_input(dconv, conv_w, dproj):
    s, c = dconv.shape
    ts = _tile(s, 256, 8)
    hb = ts // 8
    nblk8 = s // 8
    nt = s // ts

    def body(cur_ref, nxt_ref, cw_ref, _, o_ref):
        last = pl.program_id(0) == nt - 1
        cur = cur_ref[...]
        halo = jnp.where(last, 0.0, nxt_ref[...])
        acc = None
        for j in range(CONV):
            term = cw_ref[j:j + 1, :] * _shift_up(cur, halo, CONV - 1 - j)
            acc = term if acc is None else acc + term
        o_ref[...] = acc.astype(BF16)

    return _pcall(
        body, name="conv_bwd_input", grid=(nt,),
        in_specs=[_rows(ts, c),
                  pl.BlockSpec((8, c), lambda i: (jnp.minimum((i + 1) * hb, nblk8 - 1), 0)),
                  _full((CONV, c)), _ANY],
        out_specs=_rows(ts, c), out_shape=SDS(dproj.shape, BF16), aliases={3: 0})(
            dconv, dconv, conv_w, dproj)


def _inv_unit_lower(a):
    n = a[0].shape[0]
    i = lax.broadcasted_iota(jnp.int32, (n, n), 0)
    j = lax.broadcasted_iota(jnp.int32, (n, n), 1)
    eye = jnp.where(i == j, 1.0, 0.0)
    t = [eye - ah for ah in a]
    x = a
    for _ in range(5):
        x = [_dot(xh, xh) for xh in x]
        t = [th + _dot(th, xh) for th, xh in zip(t, x)]
    return t


def _pair_common(q, k, gcol, grow, bcol):
    i = lax.broadcasted_iota(jnp.int32, (PAIR, PAIR), 0)
    j = lax.broadcasted_iota(jnp.int32, (PAIR, PAIR), 1)
    same = jnp.right_shift(i, 6) == jnp.right_shift(j, 6)
    tril = same & (i >= j)
    strict = same & (i > j)
    dec = [jnp.where(tril, jnp.exp(jnp.minimum(gc - gr, 0.0)), 0.0) for gc, gr in zip(gcol, grow)]
    kk = [_dot_nt(kh, kh) for kh in k]
    qk = [_dot_nt(qh, kh) for qh, kh in zip(q, k)]
    a = [jnp.where(strict, b * kkh * d, 0.0) for b, kkh, d in zip(bcol, kk, dec)]
    t = _inv_unit_lower(a)
    p = [qkh * d for qkh, d in zip(qk, dec)]
    return dec, kk, a, t, p, tril, strict


def _ext(v, a):
    z = jnp.zeros_like(v)
    return jnp.concatenate([v, z] if a == 0 else [z, v], axis=0)


def _gdn_fwd(q, k, v, gb, gbt, nh):
    s = q.shape[0]
    w = nh * HEAD
    npair = s // PAIR

    def body(q_ref, k_ref, v_ref, gb_ref, gbt_ref, o_ref, st_ref, s_ref):
        @pl.when(pl.program_id(0) == 0)
        def _():
            s_ref[...] = jnp.zeros_like(s_ref)

        heads = range(nh)
        hs = [slice(HEAD * h, HEAD * (h + 1)) for h in heads]
        gbv = gb_ref[...]
        q, k, v = [q_ref[:, s_] for s_ in hs], [k_ref[:, s_] for s_ in hs], [v_ref[:, s_] for s_ in hs]
        gcol = [_col(gbv, A_LANE + h) for h in heads]
        bcol = [_col(gbv, B_LANE + h) for h in heads]
        grow = [gbt_ref[A_LANE + h:A_LANE + h + 1, :] for h in heads]
        _, _, _, t, p, _, _ = _pair_common(q, k, gcol, grow, bcol)
        eg = [jnp.exp(gc) for gc in gcol]
        qg = [x * e for x, e in zip(q, eg)]
        kg = [x * e for x, e in zip(k, eg)]
        outs = []
        for a in range(2):
            sl = slice(CHUNK * a, CHUNK * (a + 1))
            st = [s_ref[h] for h in heads]
            for h in heads:
                st_ref[a, h] = st[h]
            r = [v[h][sl] - _dot(kg[h][sl], st[h]) for h in heads]
            vn = [_dot(t[h][sl], _ext(bcol[h][sl] * r[h], a)) for h in heads]
            outs.append([_dot(qg[h][sl], st[h]) + _dot(p[h][sl], _ext(vn[h], a)) for h in heads])
            gl = [_col(gr, CHUNK * (a + 1) - 1) for gr in grow]
            kd = [k[h][sl] * jnp.exp(gl[h] - gcol[h][sl]) for h in heads]
            upd = [_dot_tn(kd[h], vn[h]) for h in heads]
            for h in heads:
                s_ref[h] = jnp.exp(gl[h]) * st[h] + upd[h]
        for h in heads:
            o_ref[:, hs[h]] = jnp.concatenate([outs[0][h], outs[1][h]], axis=0)

    return _pcall(
        body, name="gdn_fwd", grid=(npair,),
        in_specs=[_rows(PAIR, w), _rows(PAIR, w), _rows(PAIR, w), _rows(PAIR, LANE),
                  pl.BlockSpec((LANE, PAIR), lambda i: (0, i))],
        out_specs=[_rows(PAIR, w), pl.BlockSpec((2, nh, HEAD, HEAD), lambda i: (i, 0, 0, 0))],
        out_shape=[SDS((s, w), F32), SDS((2 * npair, nh, HEAD, HEAD), F32)],
        scratch=[pltpu.VMEM((nh, HEAD, HEAD), F32)])(q, k, v, gb, gbt)


def _gdn_bwd(q, k, v, gb, gbt, states, do, nh, rider):
    s = q.shape[0]
    w = nh * HEAD
    npair = s // PAIR
    rev = lambda i: (npair - 1 - i, 0)

    def body(q_ref, k_ref, v_ref, gb_ref, gbt_ref, st_ref, do_ref, dq_ref, dk_ref, dv_ref, dgb_ref,
             ds_ref):
        @pl.when(pl.program_id(0) == 0)
        def _():
            ds_ref[...] = jnp.zeros_like(ds_ref)

        lane = _lane_iota((PAIR, LANE))
        row = lax.broadcasted_iota(jnp.int32, (CHUNK, 1), 0)
        heads = range(nh)
        hs = [slice(HEAD * h, HEAD * (h + 1)) for h in heads]
        gbv = gb_ref[...]
        q, k, v = [q_ref[:, s_] for s_ in hs], [k_ref[:, s_] for s_ in hs], [v_ref[:, s_] for s_ in hs]
        do = [do_ref[:, s_] for s_ in hs]
        gcol = [_col(gbv, A_LANE + h) for h in heads]
        bcol = [_col(gbv, B_LANE + h) for h in heads]
        grow = [gbt_ref[A_LANE + h:A_LANE + h + 1, :] for h in heads]
        dec, kk, amat, t, p, tril, strict = _pair_common(q, k, gcol, grow, bcol)
        tt, pt = [x.T for x in t], [x.T for x in p]
        eg = [jnp.exp(gc) for gc in gcol]
        qg = [x * e for x, e in zip(q, eg)]
        kg = [x * e for x, e in zip(k, eg)]
        sums = lambda x: jnp.sum(x, axis=-1, keepdims=True)
        rs, vns = [None, None], [None, None]
        for a in range(2):
            sl = slice(CHUNK * a, CHUNK * (a + 1))
            rs[a] = [v[h][sl] - _dot(kg[h][sl], st_ref[a, h]) for h in heads]
            vns[a] = [_dot(t[h][sl], _ext(bcol[h][sl] * rs[a][h], a)) for h in heads]
        dsn = [ds_ref[h] for h in heads]
        dqs, dks, dvs, dgcs, dbs, drbs = ([None, None] for _ in range(6))
        for a in (1, 0):
            sl = slice(CHUNK * a, CHUNK * (a + 1))
            st = [st_ref[a, h] for h in heads]
            gl = [_col(gr, CHUNK * (a + 1) - 1) for gr in grow]
            egl = [jnp.exp(x) for x in gl]
            dk_dec = [jnp.exp(gl[h] - gcol[h][sl]) for h in heads]
            kd = [k[h][sl] * dk_dec[h] for h in heads]
            d_vn = [_dot(pt[h][sl], _ext(do[h][sl], a)) + _dot(kd[h], dsn[h]) for h in heads]
            d_qg = [_dot_nt(do[h][sl], st[h]) for h in heads]
            d_rb = [_dot(tt[h][sl], _ext(d_vn[h], a)) for h in heads]
            d_r = [bcol[h][sl] * d_rb[h] for h in heads]
            d_kg = [-_dot_nt(d_r[h], st[h]) for h in heads]
            d_kd = [_dot_nt(vns[a][h], dsn[h]) for h in heads]
            dsn_new = [_dot_tn(qg[h][sl], do[h][sl]) - _dot_tn(kg[h][sl], d_r[h]) for h in heads]
            dbs[a] = [sums(d_rb[h] * rs[a][h]) for h in heads]
            dgl = [egl[h] * jnp.sum(dsn[h] * st[h], keepdims=True) + jnp.sum(d_kd[h] * kd[h], keepdims=True)
                   for h in heads]
            dgcs[a] = [sums(d_qg[h] * qg[h][sl]) + sums(d_kg[h] * kg[h][sl]) - sums(d_kd[h] * kd[h])
                       + jnp.where(row == CHUNK - 1, dgl[h], 0.0) for h in heads]
            dqs[a] = [d_qg[h] * eg[h][sl] for h in heads]
            dks[a] = [d_kg[h] * eg[h][sl] + d_kd[h] * dk_dec[h] for h in heads]
            dvs[a] = d_r
            drbs[a] = d_rb
            dsn = [dsn_new[h] + egl[h] * dsn[h] for h in heads]
        for h in heads:
            ds_ref[h] = dsn[h]
        cat = lambda xs, h: jnp.concatenate([xs[0][h], xs[1][h]], axis=0)
        vn = [cat(vns, h) for h in heads]
        d_rb = [cat(drbs, h) for h in heads]
        dp = [jnp.where(tril, _dot_nt(do[h], vn[h]), 0.0) for h in heads]
        dam = [jnp.where(strict, -_dot_nt(d_rb[h], vn[h]), 0.0) for h in heads]
        g_p = [dp[h] * dec[h] for h in heads]
        g_a = [dam[h] * dec[h] for h in heads]
        gbk = [bcol[h] * g_a[h] for h in heads]
        dq2 = [_dot(g_p[h], k[h]) for h in heads]
        dk2 = [_dot_tn(g_p[h], q[h]) + _dot(gbk[h], k[h]) + _dot_tn(gbk[h], k[h]) for h in heads]
        dgb = jnp.zeros((PAIR, LANE), F32)
        for h in heads:
            dq_ref[:, hs[h]] = cat(dqs, h) + dq2[h]
            dk_ref[:, hs[h]] = cat(dks, h) + dk2[h]
            dv_ref[:, hs[h]] = cat(dvs, h)
            dbeta = cat(dbs, h) + sums(g_a[h] * kk[h])
            mm = dp[h] * p[h] + dam[h] * amat[h]
            dgc = cat(dgcs, h) + sums(mm) - sums(mm.T)
            dgb = dgb + jnp.where(lane == A_LANE + h, dgc, 0.0) + jnp.where(lane == B_LANE + h, dbeta, 0.0)
        dgb_ref[...] = dgb

    return _pcall_riding(
        body, rider, name="gdn_bwd", grid=(npair,),
        in_specs=[pl.BlockSpec((PAIR, w), rev), pl.BlockSpec((PAIR, w), rev), pl.BlockSpec((PAIR, w), rev),
                  pl.BlockSpec((PAIR, LANE), rev),
                  pl.BlockSpec((LANE, PAIR), lambda i: (0, npair - 1 - i)),
                  pl.BlockSpec((2, nh, HEAD, HEAD), lambda i: (npair - 1 - i, 0, 0, 0)),
                  pl.BlockSpec((PAIR, w), rev)],
        out_specs=[pl.BlockSpec((PAIR, w), rev), pl.BlockSpec((PAIR, w), rev), pl.BlockSpec((PAIR, w), rev),
                   pl.BlockSpec((PAIR, LANE), rev)],
        out_shape=[SDS((s, w), F32), SDS((s, w), F32), SDS((s, w), F32), SDS((s, LANE), F32)],
        scratch=[pltpu.VMEM((nh, HEAD, HEAD), F32)], args=(q, k, v, gb, gbt, states, do))


def _mla_norm(proj, qw, kvw, col_q, col_kv):
    s = proj.shape[0]
    lr = qw.shape[1]
    ts = _tile(s, 512, 8)

    def body(cq_ref, ckv_ref, qw_ref, kvw_ref, oq_ref, okv_ref):
        for x_ref, w_ref, o_ref in ((cq_ref, qw_ref, oq_ref), (ckv_ref, kvw_ref, okv_ref)):
            xv = x_ref[...]
            r = lax.rsqrt(jnp.mean(xv * xv, axis=-1, keepdims=True) + EPS)
            o_ref[...] = (xv * r * w_ref[...]).astype(BF16)

    return _pcall(body, name="mla_norm", grid=(s // ts,),
                  in_specs=[_rows(ts, lr, col_q), _rows(ts, lr, col_kv), _full((1, lr)), _full((1, lr))],
                  out_specs=[_rows(ts, lr), _rows(ts, lr)],
                  out_shape=[SDS((s, lr), BF16), SDS((s, lr), BF16)])(proj, proj, qw, kvw)


def _mla_norm_bwd(proj, qw, kvw, dq, dkv, dproj, col_q, col_kv):
    s = proj.shape[0]
    lr = qw.shape[1]
    ts = _tile(s, 512, 8)

    assert col_kv == col_q + 1 and col_q % 2 == 0

    def body(cq_ref, ckv_ref, qw_ref, kvw_ref, dq_ref, dkv_ref, _, o_ref, dqw_ref, dkvw_ref):
        @pl.when(pl.program_id(0) == 0)
        def _():
            dqw_ref[...] = jnp.zeros_like(dqw_ref)
            dkvw_ref[...] = jnp.zeros_like(dkvw_ref)

        for k, (x_ref, w_ref, d_ref, dw_ref) in enumerate(((cq_ref, qw_ref, dq_ref, dqw_ref),
                                                           (ckv_ref, kvw_ref, dkv_ref, dkvw_ref))):
            xv, dh = x_ref[...], d_ref[...]
            r = lax.rsqrt(jnp.mean(xv * xv, axis=-1, keepdims=True) + EPS)
            xh = xv * r
            dw_ref[...] += jnp.sum(dh * xh, axis=0, keepdims=True)
            dxh = dh * w_ref[...]
            o_ref[:, lr * k:lr * (k + 1)] = (
                r * (dxh - xh * jnp.mean(dxh * xh, axis=-1, keepdims=True))).astype(BF16)

    return _pcall(body, name="mla_norm_bwd", grid=(s // ts,),
                  in_specs=[_rows(ts, lr, col_q), _rows(ts, lr, col_kv), _full((1, lr)), _full((1, lr)),
                            _rows(ts, lr), _rows(ts, lr), _ANY],
                  out_specs=[_rows(ts, 2 * lr, col_q // 2), _full((1, lr)), _full((1, lr))],
                  out_shape=[SDS(dproj.shape, BF16), SDS((1, lr), F32), SDS((1, lr), F32)],
                  aliases={6: 0})(proj, proj, qw, kvw, dq, dkv, dproj)


def _rope_tables(pos, invf, sgn):
    ang = pos * invf
    return jnp.cos(ang), jnp.sin(ang) * sgn


def _swap_halves_lanes(y):
    lane = _lane_iota(y.shape)
    return jnp.where(lane < ROPE // 2, pltpu.roll(y, LANE - ROPE // 2, 1), pltpu.roll(y, ROPE // 2, 1))


def _rope_consts():
    half = ROPE // 2
    inv = ROPE_THETA ** (-jnp.arange(half, dtype=F32) / half)
    invf = jnp.concatenate([inv, inv, jnp.zeros((LANE - ROPE,), F32)])[None, :]
    sgn = jnp.concatenate([-jnp.ones((half,), F32), jnp.ones((half,), F32),
                           jnp.zeros((LANE - ROPE,), F32)])[None, :]
    return invf, sgn


def _mla_rope(qraw, kvraw, proj, pos, nh, misc_col):
    s = qraw.shape[0]
    ts = _tile(s, MLA_BLOCK)
    wq = nh * 2 * HEAD
    invf, sgn = _rope_consts()

    def body(q_ref, kv_ref, misc_ref, pos_ref, if_ref, sg_ref, qc_ref, kc_ref, v_ref, qt_ref, vt_ref):
        c, sn = _rope_tables(pos_ref[...], if_ref[...], sg_ref[...])
        lane = _lane_iota(c.shape)
        rot = lambda xb: xb * c + _swap_halves_lanes(xb) * sn
        qs = SM_SCALE * LOG2E
        krot = jnp.where(lane < ROPE, rot(misc_ref[...]), 0.0).astype(BF16)
        for h in range(nh):
            b0 = 2 * HEAD * h
            qn = q_ref[:, b0:b0 + HEAD].astype(F32) * qs
            qr = rot(q_ref[:, b0 + HEAD:b0 + 2 * HEAD].astype(F32)) * qs
            qc_ref[:, b0:b0 + HEAD] = qn.astype(BF16)
            qc_ref[:, b0 + HEAD:b0 + 2 * HEAD] = qr.astype(BF16)
            qt_ref[b0:b0 + HEAD, :] = qn.T.astype(BF16)
            qt_ref[b0 + HEAD:b0 + 2 * HEAD, :] = qr.T.astype(BF16)
            kc_ref[:, b0:b0 + HEAD] = kv_ref[:, b0:b0 + HEAD].astype(BF16)
            kc_ref[:, b0 + HEAD:b0 + 2 * HEAD] = krot
            vh = kv_ref[:, wq + HEAD * h:wq + HEAD * (h + 1)]
            v_ref[:, HEAD * h:HEAD * (h + 1)] = vh.astype(BF16)
            vt_ref[HEAD * h:HEAD * (h + 1), :] = vh.astype(F32).T.astype(BF16)

    return _pcall(body, name="mla_rope", grid=(s // ts,),
                  in_specs=[_rows(ts, wq), _rows(ts, wq + nh * HEAD), _rows(ts, LANE, misc_col),
                            _rows(ts, 1), _full((1, LANE)), _full((1, LANE))],
                  out_specs=[_rows(ts, wq), _rows(ts, wq), _rows(ts, nh * HEAD),
                             pl.BlockSpec((None, wq, ts), lambda i: (i, 0, 0)),
                             pl.BlockSpec((None, nh * HEAD, ts), lambda i: (i, 0, 0))],
                  out_shape=[SDS((s, wq), BF16), SDS((s, wq), BF16), SDS((s, nh * HEAD), BF16),
                             SDS((s // ts, wq, ts), BF16), SDS((s // ts, nh * HEAD, ts), BF16)])(
                      qraw, kvraw, proj, pos, invf, sgn)


def _mla_rope_bwd(dqc, dkc, dv, pos, nh):
    s = dqc.shape[0]
    ts = _tile(s, 256, 8)
    wq = nh * 2 * HEAD
    invf, sgn = _rope_consts()

    def body(dq_ref, dk_ref, dv_ref, pos_ref, if_ref, sg_ref, oq_ref, okv_ref, okr_ref):
        c, sn = _rope_tables(pos_ref[...], if_ref[...], sg_ref[...])
        lane = _lane_iota(c.shape)
        unrot = lambda d: d * c + _swap_halves_lanes(d * sn)
        dkr = jnp.zeros(c.shape, F32)
        for h in range(nh):
            b0 = 2 * HEAD * h
            oq_ref[:, b0:b0 + HEAD] = (dq_ref[:, b0:b0 + HEAD] * SM_SCALE).astype(BF16)
            oq_ref[:, b0 + HEAD:b0 + 2 * HEAD] = (
                unrot(dq_ref[:, b0 + HEAD:b0 + 2 * HEAD]) * SM_SCALE).astype(BF16)
            okv_ref[:, b0:b0 + HEAD] = (dk_ref[:, b0:b0 + HEAD] * LN2).astype(BF16)
            okv_ref[:, b0 + HEAD:b0 + 2 * HEAD] = jnp.zeros((ts, HEAD), BF16)
            dkr = dkr + dk_ref[:, b0 + HEAD:b0 + 2 * HEAD]
        okv_ref[:, wq:] = dv_ref[...].astype(BF16)
        okr_ref[...] = jnp.where(lane < ROPE, unrot(jnp.where(lane < ROPE, dkr * LN2, 0.0)), 0.0)

    return _pcall(body, name="mla_rope_bwd", grid=(s // ts,),
                  in_specs=[_rows(ts, wq), _rows(ts, wq), _rows(ts, nh * HEAD), _rows(ts, 1),
                            _full((1, LANE)), _full((1, LANE))],
                  out_specs=[_rows(ts, wq), _rows(ts, wq + nh * HEAD), _rows(ts, LANE)],
                  out_shape=[SDS((s, wq), BF16), SDS((s, wq + nh * HEAD), BF16), SDS((s, LANE), F32)])(
                      dqc, dkc, dv, pos, invf, sgn)


def _causal_mask(blk):
    i = lax.broadcasted_iota(jnp.int32, (blk, blk), 0)
    j = lax.broadcasted_iota(jnp.int32, (blk, blk), 1)
    return j <= i


MLA_HP = 2
MLA_FWD_HP = 4


def _pair_pack(a, b):
    return jnp.where(_lane_iota(a.shape) < LANE // 2, a, b)


def _pair_unpack(x, e):
    lane = _lane_iota(x.shape)
    keep = (lane < LANE // 2) if e == 0 else (lane >= LANE // 2)
    return jnp.where(keep, x, pltpu.roll(x, LANE // 2, 1))


def _mla_fwd(qt, kc, vt, nh, rider):
    nb, _, blk = qt.shape
    s = nb * blk
    hp = MLA_FWD_HP if nh % MLA_FWD_HP == 0 else MLA_HP
    assert nh % hp == 0 and hp % MLA_HP == 0
    once = pl.Buffered(1)
    r_in, r_out = len(rider.arrays), len(rider.out_shapes)

    def body(*refs):
        qt_ref, k_ref, vt_ref = refs[:3]
        o_ref, lse_ref = refs[3 + r_in:5 + r_in]
        m_sc, l_sc, acc = refs[5 + r_in + r_out:8 + r_in + r_out]
        r_refs = (refs[3:3 + r_in], refs[5 + r_in:5 + r_in + r_out], refs[8 + r_in + r_out:])
        i = pl.program_id(1)
        grid_step = pl.program_id(0) * nb + i
        _ride_begin(rider, r_refs, grid_step)
        m_sc[...] = jnp.full_like(m_sc, -1e30)
        l_sc[...] = jnp.zeros_like(l_sc)
        acc[...] = jnp.zeros_like(acc)
        es = range(hp)

        def step(j, masked):
            rows = pl.ds(pl.multiple_of(j * blk, blk), blk)
            sc = [_dot(k_ref[rows, 2 * HEAD * e:2 * HEAD * (e + 1)], qt_ref[2 * HEAD * e:2 * HEAD * (e + 1), :])
                  for e in es]
            if masked:
                key = lax.broadcasted_iota(jnp.int32, (blk, blk), 0)
                qry = lax.broadcasted_iota(jnp.int32, (blk, blk), 1)
                sc = [jnp.where(key <= qry, x, -1e30) for x in sc]
            m_prev = [m_sc[e] for e in es]
            m_new = [jnp.maximum(m_prev[e], jnp.max(sc[e], axis=0, keepdims=True)) for e in es]
            p = [jnp.exp2(sc[e] - m_new[e]) for e in es]
            alpha = [jnp.exp2(m_prev[e] - m_new[e]) for e in es]
            pv = [_dot(vt_ref[j, HEAD * e:HEAD * (e + 1), :], p[e]) for e in es]
            for e in es:
                l_sc[e] = alpha[e] * l_sc[e] + jnp.sum(p[e], axis=0, keepdims=True)
                acc[e] = alpha[e] * acc[e] + pv[e]
                m_sc[e] = m_new[e]

        def loop_body(j, carry):
            step(j, False)
            return carry

        lax.fori_loop(0, i, loop_body, 0)
        step(i, True)
        lse = []
        for e in es:
            o_ref[:, HEAD * e:HEAD * (e + 1)] = (acc[e] / l_sc[e]).T
            row = m_sc[e] + jnp.log(l_sc[e]) * LOG2E
            lse.append(jnp.broadcast_to(row, (LANE, blk)).T)
        for g in range(hp // MLA_HP):
            lse_ref[g] = _pair_pack(lse[MLA_HP * g], lse[MLA_HP * g + 1])
        _ride_end(rider, r_refs, grid_step, (nh // hp) * nb)

    outs = _pcall(
        body, name="mla_fwd", grid=(nh // hp, nb),
        in_specs=[pl.BlockSpec((None, hp * 2 * HEAD, blk), lambda g, i: (i, g, 0)),
                  pl.BlockSpec((s, hp * 2 * HEAD), lambda g, i: (0, g), pipeline_mode=once),
                  pl.BlockSpec((nb, hp * HEAD, blk), lambda g, i: (0, g, 0), pipeline_mode=once)]
        + [_ANY] * r_in,
        out_specs=[pl.BlockSpec((blk, hp * HEAD), lambda g, i: (i, g)),
                   pl.BlockSpec((hp // MLA_HP, blk, LANE), lambda g, i: (g, i, 0))] + [_ANY] * r_out,
        out_shape=[SDS((s, nh * HEAD), F32), SDS((nh // MLA_HP, s, LANE), F32)] + rider.out_shapes,
        scratch=[pltpu.VMEM((hp, 1, blk), F32), pltpu.VMEM((hp, 1, blk), F32),
                 pltpu.VMEM((hp, HEAD, blk), F32)] + rider.scratch)(qt, kc, vt, *rider.arrays)
    return outs[0], outs[1], outs[2:]


def _mla_bwd(qc, kc, v, do, lse, delta, nh, rider):
    s = qc.shape[0]
    blk = _tile(s, MLA_BLOCK)
    nb = s // blk
    rep = blk // LANE
    hp = MLA_HP
    once = pl.Buffered(1)
    r_in, r_out = len(rider.arrays), len(rider.out_shapes)
    qs = [slice(2 * HEAD * e, 2 * HEAD * (e + 1)) for e in range(hp)]
    vs = [slice(HEAD * e, HEAD * (e + 1)) for e in range(hp)]

    def body(*refs):
        q_ref, do_ref, lse_ref, dl_ref, k_ref, v_ref = refs[:6]
        dq_ref, dk_ref, dv_ref = refs[6 + r_in:9 + r_in]
        dk_acc, dv_acc = refs[9 + r_in + r_out:11 + r_in + r_out]
        r_refs = (refs[6:6 + r_in], refs[9 + r_in:9 + r_in + r_out], refs[11 + r_in + r_out:])
        j = pl.program_id(1)
        grid_step = pl.program_id(0) * nb + j
        _ride_begin(rider, r_refs, grid_step)

        @pl.when(j == 0)
        def _():
            dq_ref[...] = jnp.zeros_like(dq_ref)

        dk_acc[...] = jnp.zeros_like(dk_acc)
        dv_acc[...] = jnp.zeros_like(dv_acc)
        es = range(hp)
        kj = [k_ref[:, qs[e]] for e in es]
        vj = [v_ref[:, vs[e]] for e in es]

        def step(i, masked):
            rows = pl.ds(pl.multiple_of(i * blk, blk), blk)
            qi = [q_ref[rows, qs[e]] for e in es]
            doi = [do_ref[rows, vs[e]] for e in es]
            lse, dl = lse_ref[rows, :], dl_ref[rows, :]
            sc = [_dot_nt(qi[e], kj[e]) for e in es]
            dp = [_dot_nt(doi[e], vj[e]) for e in es]
            if masked:
                sc = [jnp.where(_causal_mask(blk), x, -1e30) for x in sc]
            p = [jnp.exp2(sc[e] - jnp.tile(_pair_unpack(lse, e), (1, rep))) for e in es]
            ds = [p[e] * (dp[e] - jnp.tile(_pair_unpack(dl, e), (1, rep))) for e in es]
            dv = [_dot_tn(p[e], doi[e]) for e in es]
            dk = [_dot_tn(ds[e], qi[e]) for e in es]
            dq = [_dot(ds[e], kj[e]) for e in es]
            for e in es:
                dv_acc[:, vs[e]] += dv[e]
                dk_acc[:, qs[e]] += dk[e]
                dq_ref[rows, qs[e]] += dq[e]

        def loop_body(i, carry):
            step(i, False)
            return carry

        step(j, True)
        lax.fori_loop(j + 1, nb, loop_body, 0)
        dk_ref[...] = dk_acc[...]
        dv_ref[...] = dv_acc[...]
        _ride_end(rider, r_refs, grid_step, (nh // hp) * nb)

    outs = _pcall(
        body, name="mla_bwd", grid=(nh // hp, nb),
        in_specs=[pl.BlockSpec((s, hp * 2 * HEAD), lambda g, j: (0, g), pipeline_mode=once),
                  pl.BlockSpec((s, hp * HEAD), lambda g, j: (0, g), pipeline_mode=once),
                  pl.BlockSpec((None, s, LANE), lambda g, j: (g, 0, 0), pipeline_mode=once),
                  pl.BlockSpec((None, s, LANE), lambda g, j: (g, 0, 0), pipeline_mode=once),
                  pl.BlockSpec((blk, hp * 2 * HEAD), lambda g, j: (j, g)),
                  pl.BlockSpec((blk, hp * HEAD), lambda g, j: (j, g))] + [_ANY] * r_in,
        out_specs=[pl.BlockSpec((s, hp * 2 * HEAD), lambda g, j: (0, g), pipeline_mode=once),
                   pl.BlockSpec((blk, hp * 2 * HEAD), lambda g, j: (j, g)),
                   pl.BlockSpec((blk, hp * HEAD), lambda g, j: (j, g))] + [_ANY] * r_out,
        out_shape=[SDS((s, nh * 2 * HEAD), F32), SDS((s, nh * 2 * HEAD), F32),
                   SDS((s, nh * HEAD), F32)] + rider.out_shapes,
        scratch=[pltpu.VMEM((blk, hp * 2 * HEAD), F32), pltpu.VMEM((blk, hp * HEAD), F32)] + rider.scratch,
        vmem=VMEM_LIMIT_WIDE)(qc, do, lse, delta, kc, v, *rider.arrays)
    return outs[0], outs[1], outs[2], outs[3:]


def _mix_fwd(og, proj, om, gw, mw, nh, z_col):
    s = og.shape[0]
    w = nh * HEAD
    ts = _tile(s, 256, 8)

    def body(og_ref, z_ref, om_ref, gw_ref, mw_ref, o_ref):
        for h in range(nh):
            hs = slice(HEAD * h, HEAD * (h + 1))
            a = og_ref[:, hs]
            r = lax.rsqrt(jnp.mean(a * a, axis=-1, keepdims=True) + EPS)
            o_ref[:, hs] = (a * r * gw_ref[...] * _silu(z_ref[:, hs])).astype(BF16)
            b = om_ref[:, hs]
            r = lax.rsqrt(jnp.mean(b * b, axis=-1, keepdims=True) + EPS)
            o_ref[:, w + HEAD * h:w + HEAD * (h + 1)] = (b * r * mw_ref[...]).astype(BF16)

    return _pcall(body, name="mix_fwd", grid=(s // ts,),
                  in_specs=[_rows(ts, w), _rows(ts, w, z_col), _rows(ts, w), _full((1, HEAD)),
                            _full((1, HEAD))],
                  out_specs=_rows(ts, 2 * w), out_shape=SDS((s, 2 * w), BF16))(og, proj, om, gw, mw)


def _mix_bwd(dmix, og, proj, om, gw, mw, nh, z_col):
    s = og.shape[0]
    w = nh * HEAD
    ts = _tile(s, 256, 8)

    def body(d_ref, og_ref, z_ref, om_ref, gw_ref, mw_ref, dog_ref, dz_ref, dom_ref, dgw_ref, dmw_ref,
             dl_ref):
        @pl.when(pl.program_id(0) == 0)
        def _():
            dgw_ref[...] = jnp.zeros_like(dgw_ref)
            dmw_ref[...] = jnp.zeros_like(dmw_ref)

        dgw = jnp.zeros((1, HEAD), F32)
        dmw = jnp.zeros((1, HEAD), F32)
        deltas = []
        for h in range(nh):
            hs = slice(HEAD * h, HEAD * (h + 1))
            a, z, dy = og_ref[:, hs], z_ref[:, hs], d_ref[:, hs]
            r = lax.rsqrt(jnp.mean(a * a, axis=-1, keepdims=True) + EPS)
            ah = a * r
            sz = _silu(z)
            dz_ref[:, hs] = (dy * (ah * gw_ref[...]) * _dsilu(z)).astype(BF16)
            dn = dy * sz
            dgw = dgw + jnp.sum(dn * ah, axis=0, keepdims=True)
            dah = dn * gw_ref[...]
            dog_ref[:, hs] = r * (dah - ah * jnp.mean(dah * ah, axis=-1, keepdims=True))
            b, dyb = om_ref[:, hs], d_ref[:, w + HEAD * h:w + HEAD * (h + 1)]
            r = lax.rsqrt(jnp.mean(b * b, axis=-1, keepdims=True) + EPS)
            bh = b * r
            dmw = dmw + jnp.sum(dyb * bh, axis=0, keepdims=True)
            dbh = dyb * mw_ref[...]
            dom = r * (dbh - bh * jnp.mean(dbh * bh, axis=-1, keepdims=True))
            dom_ref[:, hs] = dom.astype(BF16)
            deltas.append(jnp.broadcast_to(jnp.sum(dom * b, axis=-1, keepdims=True), (ts, LANE)))
        for g in range(nh // MLA_HP):
            dl_ref[g] = _pair_pack(deltas[2 * g], deltas[2 * g + 1])
        dgw_ref[...] += dgw
        dmw_ref[...] += dmw

    return _pcall(body, name="mix_bwd", grid=(s // ts,),
                  in_specs=[_rows(ts, 2 * w), _rows(ts, w), _rows(ts, w, z_col), _rows(ts, w),
                            _full((1, HEAD)), _full((1, HEAD))],
                  out_specs=[_rows(ts, w), _rows(ts, w, z_col), _rows(ts, w), _full((1, HEAD)), _full((1, HEAD)),
                             pl.BlockSpec((nh // MLA_HP, ts, LANE), lambda i: (0, i, 0))],
                  out_shape=[SDS((s, w), F32), SDS((s, proj.shape[1]), BF16), SDS((s, w), BF16),
                             SDS((1, HEAD), F32), SDS((1, HEAD), F32),
                             SDS((nh // MLA_HP, s, LANE), F32)])(dmix, og, proj, om, gw, mw)


def _swiglu_fwd(h2, wg, wu):
    m, kdim = h2.shape
    tn = wg.shape[2]
    n = 4 * tn
    tm, tk = _tile(m, 512), _tile(kdim, 2048)
    nk = kdim // tk

    def body(a_ref, g_ref, u_ref, act_ref, go_ref, uo_ref, gacc, uacc):
        k = pl.program_id(2)

        @pl.when(k == 0)
        def _():
            gacc[...] = jnp.zeros_like(gacc)
            uacc[...] = jnp.zeros_like(uacc)

        a = a_ref[...]
        gacc[...] += _dot(a, g_ref[...])
        uacc[...] += _dot(a, u_ref[...])

        @pl.when(k == nk - 1)
        def _():
            g, u = gacc[...], uacc[...]
            act_ref[...] = (_silu(g) * u).astype(BF16)
            go_ref[...] = g.astype(BF16)
            uo_ref[...] = u.astype(BF16)

    a_spec = pl.BlockSpec((tm, tk), lambda i, j, k: (i, k))
    b_spec = pl.BlockSpec((None, tk, tn), lambda i, j, k: (j, k, 0))
    o_spec = pl.BlockSpec((tm, tn), lambda i, j, k: (i, j))
    return _pcall(body, name="swiglu_fwd", grid=(m // tm, n // tn, nk),
                  in_specs=[a_spec, b_spec, b_spec], out_specs=[o_spec] * 3,
                  out_shape=[SDS((m, n), BF16)] * 3,
                  scratch=[pltpu.VMEM((tm, tn), F32), pltpu.VMEM((tm, tn), F32)])(h2, wg, wu)


def _swiglu_bwd(dx3, wd, g, u):
    m, kdim = dx3.shape
    n = wd.shape[0]
    tm, tn = _tile(m, 1024), _tile(n, 512)
    parts = 2 if tm % 16 == 0 else 1
    th = tm // parts

    def body(a_ref, b_ref, g_ref, u_ref, dg_ref, du_ref):
        b = b_ref[...]
        rows = [slice(th * c, th * (c + 1)) for c in range(parts)]
        da = [_dot_nt(a_ref[rs, :], b) for rs in rows]
        for rs, d in zip(rows, da):
            gv, uv = g_ref[rs, :].astype(F32), u_ref[rs, :].astype(F32)
            dg_ref[rs, :] = (d * uv * _dsilu(gv)).astype(BF16)
            du_ref[rs, :] = (d * _silu(gv)).astype(BF16)

    a_spec = pl.BlockSpec((tm, kdim), lambda i, j: (i, 0))
    b_spec = pl.BlockSpec((tn, kdim), lambda i, j: (j, 0))
    o_spec = pl.BlockSpec((tm, tn), lambda i, j: (i, j))
    return _pcall(body, name="swiglu_bwd", grid=(m // tm, n // tn),
                  in_specs=[a_spec, b_spec, o_spec, o_spec], out_specs=[o_spec] * 2,
                  out_shape=[SDS((m, n), BF16)] * 2)(dx3, wd, g, u)


def _sum_pair(g, recv, place, name):
    _, _, rh, c = g.shape
    tr = _tile(rh, 256, 16)

    def body(pl_ref, g_ref, r_ref, o16_ref, own_ref):
        sm = g_ref[...].astype(F32) + r_ref[...].astype(F32)
        o16_ref[...] = sm.astype(BF16)

        @pl.when(pl.program_id(1) == pl_ref[1])
        def _():
            own_ref[...] = sm

    grid_spec = pltpu.PrefetchScalarGridSpec(
        num_scalar_prefetch=1, grid=(rh // tr, 4),
        in_specs=[pl.BlockSpec((None, None, tr, c), lambda i, t, p: (t, p[0], i, 0)),
                  pl.BlockSpec((None, tr, c), lambda i, t, p: (t, i, 0))],
        out_specs=[pl.BlockSpec((None, tr, c), lambda i, t, p: (t, i, 0)),
                   pl.BlockSpec((tr, c), lambda i, t, p: (i, 0))])
    return pl.pallas_call(
        body, name=name, grid_spec=grid_spec,
        out_shape=[SDS((4, rh, c), BF16), SDS((rh, c), F32)],
        compiler_params=pltpu.CompilerParams(dimension_semantics=("arbitrary",) * 2,
                                             vmem_limit_bytes=VMEM_LIMIT))(place, g, recv)


def _sum_chips(own, recv, name):
    rh, c = own.shape
    tr = _tile(rh, 256, 16)

    def body(o_ref, r_ref, out_ref):
        acc = o_ref[...]
        for j in range(3):
            acc = acc + r_ref[j].astype(F32)
        out_ref[...] = acc

    return _pcall(body, name=name, grid=(rh // tr,),
                  in_specs=[_rows(tr, c), pl.BlockSpec((3, tr, c), lambda i: (0, i, 0))],
                  out_specs=_rows(tr, c), out_shape=SDS(own.shape, F32))(own, recv)


def _adamw_update(wv, gv, mv, vv):
    mn = ADAM_B1 * mv + (1.0 - ADAM_B1) * gv
    vn = ADAM_B2 * vv + (1.0 - ADAM_B2) * (gv * gv)
    m_hat = mn / (1.0 - ADAM_B1 ** ADAM_STEP)
    v_hat = vn / (1.0 - ADAM_B2 ** ADAM_STEP)
    return -ADAM_LR * (m_hat / (jnp.sqrt(v_hat) + ADAM_EPS) + ADAM_WD * wv), mn, vn


def _adamw(w, g, m, v, name):
    r, c = w.shape
    tr = _tile(r, 256, 8)

    def body(w_ref, g_ref, m_ref, v_ref, d_ref, mo_ref, vo_ref):
        d_ref[...], mo_ref[...], vo_ref[...] = _adamw_update(w_ref[...], g_ref[...], m_ref[...], v_ref[...])

    spec = _rows(tr, c)
    return _pcall(body, name=name, grid=(r // tr,), in_specs=[spec] * 4, out_specs=[spec] * 3,
                  out_shape=[SDS(w.shape, F32)] * 3)(w, g, m, v)


def _adamw_halves(w, mine, theirs, m, v, place, name):
    r, c = w.shape
    rh = r // 2
    tr = _tile(rh, 256, 8)
    nt = rh // tr

    def body(p_ref, w_ref, a_ref, b_ref, m_ref, v_ref, g_ref, d_ref, mo_ref, vo_ref):
        gv = jnp.where(pl.program_id(0) // nt == p_ref[0], a_ref[...], b_ref[...])
        g_ref[...] = gv
        d_ref[...], mo_ref[...], vo_ref[...] = _adamw_update(w_ref[...], gv, m_ref[...], v_ref[...])

    full = pl.BlockSpec((tr, c), lambda i, p: (i, 0))
    half = pl.BlockSpec((tr, c), lambda i, p: (i % nt, 0))
    grid_spec = pltpu.PrefetchScalarGridSpec(num_scalar_prefetch=1, grid=(2 * nt,),
                                             in_specs=[full, half, half, full, full], out_specs=[full] * 4)
    return pl.pallas_call(
        body, name=name, grid_spec=grid_spec, out_shape=[SDS(w.shape, F32)] * 4,
        compiler_params=pltpu.CompilerParams(dimension_semantics=("arbitrary",),
                                             vmem_limit_bytes=VMEM_LIMIT))(place, w, mine, theirs, m, v)


def _place():
    x, y, c = lax.axis_index("x"), lax.axis_index("y"), lax.axis_index("c")
    chips = [(1 - x, y), (x, 1 - y), (1 - x, 1 - y)]
    return x, y, c, chips


_ANY = pl.BlockSpec(memory_space=pl.ANY)


def _remote(src, dst, sems, k, to):
    return pltpu.make_async_remote_copy(src_ref=src, dst_ref=dst, send_sem=sems[0].at[k], recv_sem=sems[1].at[k],
                                        device_id=to, device_id_type=MESH)


class _Gather:
    def __init__(self, shards):
        n = len(shards)
        self.arrays = list(shards)
        self.out_shapes = [SDS((4,) + a.shape, a.dtype) for a in shards]
        self.scratch = [pltpu.SemaphoreType.DMA((7 * n,)), pltpu.SemaphoreType.DMA((7 * n,))]

    def _plan(self, ins, outs, sems):
        x, y, c, chips = _place()
        own, sib = 2 * x + y, (x, y, 1 - c)
        plan = []
        for wi, (w, o) in enumerate(zip(ins, outs)):
            rh = w.shape[0] // 2
            mine, theirs = pl.ds(c * rh, rh), pl.ds((1 - c) * rh, rh)
            whole = _remote(w, o.at[own], sems, 7 * wi + 6, sib)
            ici, d2d, d2d_in = [], [], []
            for j, (tx, ty) in enumerate(chips):
                t = 2 * tx + ty
                ici.append(_remote(w.at[mine], o.at[own, mine], sems, 7 * wi + j, (tx, ty, c)))
                d2d.append(_remote(o.at[t, mine], o.at[t, mine], sems, 7 * wi + 3 + j, sib))
                d2d_in.append(_remote(o.at[t, theirs], o.at[t, theirs], sems, 7 * wi + 3 + j, sib))
            plan.append((whole, ici, d2d, d2d_in))
        return plan

    def begin(self, ins, outs, sems):
        for whole, ici, _, _ in self._plan(ins, outs, sems):
            whole.start()
            for cp in ici:
                cp.start()

    def middle(self, ins, outs, sems):
        for _, ici, d2d, _ in self._plan(ins, outs, sems):
            for cp_in, cp_on in zip(ici, d2d):
                cp_in.wait_recv()
                cp_on.start()

    def finish(self, ins, outs, sems):
        for whole, ici, d2d, d2d_in in self._plan(ins, outs, sems):
            for cp in d2d_in:
                cp.wait_recv()
            for cp in ici + d2d:
                cp.wait_send()
            whole.wait()


class _Swap:
    def __init__(self, grads):
        n = len(grads)
        self.arrays = list(grads)
        self.out_shapes = [SDS((4,) + g.shape[2:], g.dtype) for g in grads]
        self.scratch = [pltpu.SemaphoreType.DMA((4 * n,)), pltpu.SemaphoreType.DMA((4 * n,))]

    def _plan(self, ins, outs, sems):
        x, y, c, _ = _place()
        return [_remote(g.at[t, 1 - c], o.at[t], sems, 4 * wi + t, (x, y, 1 - c))
                for wi, (g, o) in enumerate(zip(ins, outs)) for t in range(4)]

    def begin(self, ins, outs, sems):
        for cp in self._plan(ins, outs, sems):
            cp.start()

    def middle(self, ins, outs, sems):
        pass

    def finish(self, ins, outs, sems):
        for cp in self._plan(ins, outs, sems):
            cp.wait()


class _Exchange:
    def __init__(self, pieces):
        n = len(pieces)
        self.arrays = list(pieces)
        self.out_shapes = [SDS((3,) + p.shape[1:], p.dtype) for p in pieces]
        self.scratch = [pltpu.SemaphoreType.DMA((3 * n,)), pltpu.SemaphoreType.DMA((3 * n,))]

    def _plan(self, ins, outs, sems):
        x, y, c, chips = _place()
        return [_remote(g.at[2 * tx + ty], o.at[j], sems, 3 * wi + j, (tx, ty, c))
                for wi, (g, o) in enumerate(zip(ins, outs)) for j, (tx, ty) in enumerate(chips)]

    def begin(self, ins, outs, sems):
        for cp in self._plan(ins, outs, sems):
            cp.start()

    def middle(self, ins, outs, sems):
        pass

    def finish(self, ins, outs, sems):
        for cp in self._plan(ins, outs, sems):
            cp.wait()


class _Share:
    def __init__(self, totals):
        n = len(totals)
        self.arrays = list(totals)
        self.out_shapes = [SDS(t.shape, t.dtype) for t in totals]
        self.scratch = [pltpu.SemaphoreType.DMA((n,)), pltpu.SemaphoreType.DMA((n,))]

    def _plan(self, ins, outs, sems):
        x, y, c, _ = _place()
        return [_remote(t, o, sems, wi, (x, y, 1 - c)) for wi, (t, o) in enumerate(zip(ins, outs))]

    def begin(self, ins, outs, sems):
        for cp in self._plan(ins, outs, sems):
            cp.start()

    def middle(self, ins, outs, sems):
        pass

    def finish(self, ins, outs, sems):
        for cp in self._plan(ins, outs, sems):
            cp.wait()


def _ride_begin(rider, r_refs, step):
    @pl.when(step == 0)
    def _():
        rider.begin(*r_refs)


def _ride_end(rider, r_refs, step, nsteps):
    @pl.when(step == min(3 * nsteps // 4, nsteps - 1))
    def _():
        rider.middle(*r_refs)

    @pl.when(step == nsteps - 1)
    def _():
        rider.finish(*r_refs)


def _comm(rider, name):
    n_in, n_out = len(rider.arrays), len(rider.out_shapes)

    def body(*refs):
        r_refs = (refs[:n_in], refs[n_in:n_in + n_out], refs[n_in + n_out:])
        rider.begin(*r_refs)
        rider.middle(*r_refs)
        rider.finish(*r_refs)

    return pl.pallas_call(body, name=name, out_shape=rider.out_shapes, in_specs=[_ANY] * n_in,
                          out_specs=[_ANY] * n_out, scratch_shapes=rider.scratch)(*rider.arrays)


def _small_allreduce(pk, name):
    r = pk.shape[0]
    rels = [(dx, dy, dc) for dx in (0, 1) for dy in (0, 1) for dc in (0, 1) if dx or dy or dc]

    def body(p_ref, o_ref, buf, send_sems, recv_sems):
        x, y, c, _ = _place()
        me = 4 * x + 2 * y + c
        buf[me] = p_ref[...]
        cps = []
        for k, (dx, dy, dc) in enumerate(rels):
            to = (1 - x if dx else x, 1 - y if dy else y, 1 - c if dc else c)
            cps.append(pltpu.make_async_remote_copy(src_ref=p_ref, dst_ref=buf.at[me], send_sem=send_sems.at[k],
                                                    recv_sem=recv_sems.at[k], device_id=to,
                                                    device_id_type=MESH))
        for cpy in cps:
            cpy.start()
        for cpy in cps:
            cpy.wait()
        acc = buf[0]
        for d in range(1, 8):
            acc = acc + buf[d]
        o_ref[...] = acc

    vm = pl.BlockSpec(memory_space=pltpu.VMEM)
    return pl.pallas_call(body, name=name, out_shape=SDS(pk.shape, F32), in_specs=[vm], out_specs=vm,
                          scratch_shapes=[pltpu.VMEM((8, r, LANE), F32), pltpu.SemaphoreType.DMA((7,)),
                                          pltpu.SemaphoreType.DMA((7,))])(pk)


ATTN_W = ("w_in", "w_uq", "w_ukv", "w_out")
FFN_W = ("w_gate", "w_up", "w_down")
BIG = ATTN_W + FFN_W


def _cols_from_chips(g):
    return jnp.concatenate([g[t] for t in range(4)], axis=1)


def _cols_to_chips(full):
    r, n = full.shape
    return full.reshape(r, 4, n // 4).transpose(1, 0, 2).reshape(4, 2, r // 2, n // 4)


def _rows_to_chips(full):
    n, c = full.shape
    return full.reshape(4, 2, n // 8, c)


def _permute_w_in(w, nh):
    d = w.shape[0]
    g = 4 * nh * HEAD
    lr = (w.shape[1] - g - 2 * nh - ROPE) // 2
    o = g + 2 * nh
    pad = jnp.zeros((d, LANE - ROPE - 8 - nh), w.dtype)
    pad8 = jnp.zeros((d, 8 - nh), w.dtype)
    return jnp.concatenate([w[:, :g], w[:, o:o + 2 * lr], w[:, o + 2 * lr:], w[:, g:g + nh], pad8,
                            w[:, g + nh:g + 2 * nh], pad, jnp.zeros((d, LANE), w.dtype)], axis=1)


def _unpermute_w_in(wp, nh, lr):
    g = 4 * nh * HEAD
    mc = g + 2 * lr
    return jnp.concatenate([wp[:, :g], wp[:, mc + B_LANE:mc + B_LANE + nh], wp[:, mc + A_LANE:mc + A_LANE + nh],
                            wp[:, g:g + 2 * lr], wp[:, mc:mc + ROPE]], axis=1)


def _permute_w_uq(w, nh):
    lr = w.shape[0]
    w3 = w.reshape(lr, nh, HEAD + ROPE)
    return jnp.concatenate([w3, jnp.zeros((lr, nh, HEAD - ROPE), w.dtype)], axis=2).reshape(lr, nh * 2 * HEAD)


def _unpermute_w_uq(wp, nh):
    lr = wp.shape[0]
    return wp.reshape(lr, nh, 2 * HEAD)[:, :, :HEAD + ROPE].reshape(lr, nh * (HEAD + ROPE))


def _permute_w_ukv(w, nh):
    lr = w.shape[0]
    w3 = w.reshape(lr, nh, 2 * HEAD)
    kp = jnp.concatenate([w3[:, :, :HEAD], jnp.zeros((lr, nh, HEAD), w.dtype)], axis=2)
    return jnp.concatenate([kp.reshape(lr, nh * 2 * HEAD), w3[:, :, HEAD:].reshape(lr, nh * HEAD)], axis=1)


def _unpermute_w_ukv(wp, nh):
    lr = wp.shape[0]
    kp = wp[:, :nh * 2 * HEAD].reshape(lr, nh, 2 * HEAD)[:, :, :HEAD]
    vp = wp[:, nh * 2 * HEAD:].reshape(lr, nh, HEAD)
    return jnp.concatenate([kp, vp], axis=2).reshape(lr, nh * 2 * HEAD)


def _sum_pairs(grads, recv, place, tag):
    sums = [_sum_pair(g, r, place, "sum_pair_%s%d" % (tag, k)) for k, (g, r) in enumerate(zip(grads, recv))]
    return [s[0] for s in sums], [s[1] for s in sums]


def _reduce_end(own, recv, tag):
    return [_sum_chips(o, r, "sum_chips_%s%d" % (tag, k)) for k, (o, r) in enumerate(zip(own, recv))]


def _step(x, pos, tgt, w_in, attn_shards, ffn_shards, small, place):
    nh = small["a_log"].shape[1]
    lr = small["q_norm_w"].shape[1]
    w = nh * HEAD
    z_col, col_q, col_kv = 3, 4 * w // lr, 4 * w // lr + 1
    misc_c = 4 * w + 2 * lr
    misc_col = misc_c // LANE
    assert (4 * w) % lr == 0 and small["kv_norm_w"].shape[1] == lr

    zl = jnp.zeros((1, LANE), F32)
    alog_l = zl.at[:, A_LANE:A_LANE + nh].set(small["a_log"])
    dtb_l = zl.at[:, A_LANE:A_LANE + nh].set(small["dt_bias"])
    conv_w = small["conv_w"]

    h1, (in4,) = _norm_fwd(x, small["attn_norm_w"], "norm1", rider=_Gather([w_in]))
    win_p = _permute_w_in(_cols_from_chips(in4), nh)
    proj, (uq4, ukv4, out4) = _mm([(h1, win_p)], name="proj_in", rider=_Gather(attn_shards))
    wuq_p = _permute_w_uq(_cols_from_chips(uq4), nh)
    wukv_p = _permute_w_ukv(_cols_from_chips(ukv4), nh)
    w_out = out4.reshape(-1, out4.shape[2])
    gq, gk, gv, gb, gbt = _gdn_prep(proj, conv_w, alog_l, dtb_l, nh, misc_col)
    o_gdn, states = _gdn_fwd(gq, gk, gv, gb, gbt, nh)
    cqn, ckvn = _mla_norm(proj, small["q_norm_w"], small["kv_norm_w"], col_q, col_kv)
    qraw = _mm([(cqn, wuq_p)], name="proj_uq", out_dtype=BF16)
    kvraw = _mm([(ckvn, wukv_p)], name="proj_ukv", out_dtype=BF16)
    qc, kc, vv, qt, vt = _mla_rope(qraw, kvraw, proj, pos, nh, misc_col)
    o_mla, lse, (wg4, wu4, wd4) = _mla_fwd(qt, kc, vt, nh, _Gather(ffn_shards))
    w_down = wd4.reshape(-1, wd4.shape[2])
    mixed = _mix_fwd(o_gdn, proj, o_mla, small["gdn_norm_w"], small["mla_out_norm_w"], nh, z_col)
    x2 = _mm([(mixed, w_out)], name="proj_out", res=x)
    h2 = _norm_fwd(x2, small["ffn_norm_w"], "norm2")
    act, gpre, upre = _swiglu_fwd(h2, wg4, wu4)
    x3 = _mm([(act, w_down)], name="proj_down", res=x2, tk=2816)
    dx3, d_final, loss, dx3h = _final_loss(x3, tgt, small["final_norm_w"])

    gs = {"final_norm_w": d_final}
    dgate, dup = _swiglu_bwd(dx3h, w_down, gpre, upre)
    g_down = _rows_to_chips(_mm([(act, dx3h)], name="dw_down", ta=True, out_dtype=BF16))
    g_gate = _mm([(h2, dgate)], name="dw_gate", ta=True, out_dtype=BF16, out_chips=True)
    g_up = _mm([(h2, dup)], name="dw_up", ta=True, out_dtype=BF16, out_chips=True)
    halves = lambda g: g.reshape(4, 2, g.shape[1] // 2, g.shape[2])
    ffn_g = [halves(g_gate), halves(g_up), g_down]
    dh2, ffn_sib = _mm([(dgate, wg4), (dup, wu4)], name="dh2", tb=True, b_chips=True, out_dtype=BF16,
                       rider=_Swap(ffn_g))
    ffn16, ffn_own = _sum_pairs(ffn_g, ffn_sib, place, "ffn")
    dx2, gs["ffn_norm_w"], dx2h = _norm_bwd(dh2, x2, small["ffn_norm_w"], dx3, "norm2_bwd", True)
    dmix = _mm([(dx2h, w_out)], name="dmix", tb=True, out_dtype=BF16)
    g_out = _rows_to_chips(_mm([(mixed, dx2h)], name="dw_out", ta=True, out_dtype=BF16))
    d_ogdn, dproj, d_omla, gs["gdn_norm_w"], gs["mla_out_norm_w"], delta = _mix_bwd(
        dmix, o_gdn, proj, o_mla, small["gdn_norm_w"], small["mla_out_norm_w"], nh, z_col)
    dqc, dkc, dvv, ffn_recv = _mla_bwd(qc, kc, vv, d_omla, lse, delta, nh, _Exchange(ffn16))
    ffn_tot = _reduce_end(ffn_own, ffn_recv, "ffn")
    dqraw, dkvraw, dkr = _mla_rope_bwd(dqc, dkc, dvv, pos, nh)
    dcqn = _mm([(dqraw, wuq_p)], name="dcqn", tb=True)
    dckvn = _mm([(dkvraw, wukv_p)], name="dckvn", tb=True)
    g_uq = _cols_to_chips(_unpermute_w_uq(_mm([(cqn, dqraw)], name="dw_uq", ta=True, out_dtype=BF16), nh))
    g_ukv = _cols_to_chips(_unpermute_w_ukv(_mm([(ckvn, dkvraw)], name="dw_ukv", ta=True, out_dtype=BF16), nh))
    dproj, gs["q_norm_w"], gs["kv_norm_w"] = _mla_norm_bwd(
        proj, small["q_norm_w"], small["kv_norm_w"], dcqn, dckvn, dproj, col_q, col_kv)
    (dgq, dgk, dgv, dgb), ffn_shared = _gdn_bwd(gq, gk, gv, gb, gbt, states, d_ogdn, nh, _Share(ffn_tot))
    dconv, dproj, gs["conv_w"], dal, ddb = _gdn_prep_bwd(
        proj, conv_w, alog_l, dtb_l, dgq, dgk, dgv, dgb, dkr, dproj, nh, misc_col)
    gs["a_log"] = dal[:, A_LANE:A_LANE + nh]
    gs["dt_bias"] = ddb[:, A_LANE:A_LANE + nh]
    dproj = _conv_bwd_input(dconv, conv_w, dproj)
    g_in = _cols_to_chips(_unpermute_w_in(_mm([(h1, dproj)], name="dw_in", ta=True, out_dtype=BF16), nh, lr))
    att_g = [g_in, g_uq, g_ukv, g_out]
    att16, att_own = _sum_pairs(att_g, _comm(_Swap(att_g), "swap_att"), place, "att")
    dh1, att_recv = _mm([(dproj, win_p)], name="dh1", tb=True, out_dtype=BF16, rider=_Exchange(att16))
    att_tot = _reduce_end(att_own, att_recv, "att")
    att_shared = _comm(_Share(att_tot), "share_att")
    grad_x, gs["attn_norm_w"] = _norm_bwd(dh1, x, small["attn_norm_w"], dx2, "norm1_bwd", False)
    return loss, grad_x, att_tot + ffn_tot, list(att_shared) + list(ffn_shared), gs


SMALL = ("attn_norm_w", "ffn_norm_w", "final_norm_w", "q_norm_w", "kv_norm_w", "gdn_norm_w",
         "mla_out_norm_w", "a_log", "dt_bias")
WEIGHTS = ("attn_norm_w", "w_in", "conv_w", "a_log", "dt_bias", "gdn_norm_w", "q_norm_w", "w_uq",
           "kv_norm_w", "w_ukv", "mla_out_norm_w", "w_out", "ffn_norm_w", "w_gate", "w_up", "w_down",
           "final_norm_w")


def _pack_small(vecs):
    flat = jnp.concatenate([v.astype(F32).reshape(-1) for v in vecs])
    pad = (-flat.shape[0]) % (8 * LANE)
    return jnp.concatenate([flat, jnp.zeros((pad,), F32)]).reshape(-1, LANE)


def kernel(x, positions, attn_norm_w, w_in, conv_w, a_log, dt_bias, gdn_norm_w, q_norm_w, w_uq, kv_norm_w, w_ukv, mla_out_norm_w, w_out, ffn_norm_w, w_gate, w_up, w_down, final_norm_w, loss_target, m_attn_norm_w, m_w_in, m_conv_w, m_a_log, m_dt_bias, m_gdn_norm_w, m_q_norm_w, m_w_uq, m_kv_norm_w, m_w_ukv, m_mla_out_norm_w, m_w_out, m_ffn_norm_w, m_w_gate, m_w_up, m_w_down, m_final_norm_w, v_attn_norm_w, v_w_in, v_conv_w, v_a_log, v_dt_bias, v_gdn_norm_w, v_q_norm_w, v_w_uq, v_kv_norm_w, v_w_ukv, v_mla_out_norm_w, v_w_out, v_ffn_norm_w, v_w_gate, v_w_up, v_w_down, v_final_norm_w):
    args = dict(locals())
    xi, yi, ci = lax.axis_index("x"), lax.axis_index("y"), lax.axis_index("c")
    chip = 2 * xi + yi

    def two_d(a):
        return a.reshape(a.shape[-2:]) if a.ndim >= 2 else a.reshape(1, -1)

    wloc = {n: two_d(args[n]) for n in WEIGHTS}
    mloc = {n: two_d(args["m_" + n]) for n in WEIGHTS}
    vloc = {n: two_d(args["v_" + n]) for n in WEIGHTS}

    cw = wloc["conv_w"]
    cshard = cw.shape[1]
    cfull = jnp.zeros((CONV, 4 * cshard), F32)
    cfull = lax.dynamic_update_slice(cfull, jnp.where(ci == 0, cw, 0.0), (0, chip * cshard))
    conv_full = _small_allreduce(_pack_small([cfull]), "gather_conv_w").reshape(-1)[:CONV * 4 * cshard]
    conv_full = conv_full.reshape(CONV, 4 * cshard)

    small = {n: wloc[n] for n in SMALL}
    small["conv_w"] = conv_full

    pos = positions.reshape(-1, 1).astype(F32)
    place = jnp.stack([ci, chip]).astype(jnp.int32)
    loss, grad_x, totals, from_sib, gs = _step(
        two_d(x), pos, two_d(loss_target), wloc["w_in"].astype(BF16), [wloc[n].astype(BF16) for n in ATTN_W[1:]],
        [wloc[n].astype(BF16) for n in FFN_W], small, place)

    small_names = SMALL + ("conv_w",)
    pk = _pack_small([gs[n] for n in small_names] + [loss])
    red = _small_allreduce(pk, "reduce_small").reshape(-1)
    gsm, off = {}, 0
    for n in small_names:
        shp = gs[n].shape
        gsm[n] = red[off:off + shp[0] * shp[1]].reshape(shp)
        off += shp[0] * shp[1]
    loss_out = red[off]
    gsm["conv_w"] = lax.dynamic_slice(gsm["conv_w"], (0, chip * cshard), (CONV, cshard))

    grads, deltas, new_m, new_v = {}, {}, {}, {}
    for n, mine, theirs in zip(BIG, totals, from_sib):
        grads[n], deltas[n], new_m[n], new_v[n] = _adamw_halves(wloc[n], mine, theirs, mloc[n], vloc[n], place,
                                                                "adamw_" + n)
    grads["conv_w"] = gsm["conv_w"]
    deltas["conv_w"], new_m["conv_w"], new_v["conv_w"] = _adamw(wloc["conv_w"], gsm["conv_w"], mloc["conv_w"],
                                                                vloc["conv_w"], "adamw_conv_w")
    sm_shapes = [wloc[n].shape for n in SMALL]
    pd, pm, pv = _adamw(_pack_small([wloc[n] for n in SMALL]), _pack_small([gsm[n] for n in SMALL]),
                        _pack_small([mloc[n] for n in SMALL]), _pack_small([vloc[n] for n in SMALL]),
                        "adamw_small")
    for dst, packed in ((deltas, pd), (new_m, pm), (new_v, pv)):
        flat, off = packed.reshape(-1), 0
        for n, shp in zip(SMALL, sm_shapes):
            dst[n] = flat[off:off + shp[0] * shp[1]].reshape(shp)
            off += shp[0] * shp[1]
    for n in SMALL:
        grads[n] = gsm[n]

    def like(n, a):
        return a.reshape(args[n].shape)

    outs = [loss_out.reshape(()), grad_x.reshape(x.shape)]
    for group in (grads, deltas, new_m, new_v):
        outs += [like(n, group[n]) for n in WEIGHTS]
    return tuple(outs)
```

```python
import functools

import jax
import jax.numpy as jnp
from jax import lax
from jax.experimental import pallas as pl
from jax.experimental.pallas import tpu as pltpu

F32, BF16 = jnp.float32, jnp.bfloat16
SDS = jax.ShapeDtypeStruct
MESH = pl.DeviceIdType.MESH

HEAD = 128
ROPE = 64
CHUNK = 64
PAIR = 2 * CHUNK
CONV = 4
EPS = 1e-6
ROPE_THETA = 10000.0
LANE = 128
B_LANE = 64
A_LANE = 72
VMEM_LIMIT = 48 * 1024 * 1024
VMEM_LIMIT_WIDE = 56 * 1024 * 1024
MLA_BLOCK = 512
LOG2E = 1.4426950408889634
LN2 = 0.6931471805599453
SM_SCALE = (HEAD + ROPE) ** -0.5

ADAM_LR = 0.001
ADAM_B1 = 0.9
ADAM_B2 = 0.999
ADAM_EPS = 1e-08
ADAM_WD = 0.01
ADAM_STEP = 10


def _tile(n, pref, mult=LANE):
    if n <= pref:
        return n
    t = (pref // mult) * mult
    while t >= mult:
        if n % t == 0:
            return t
        t -= mult
    return n


def _pcall(body, *, name, grid, in_specs, out_specs, out_shape, scratch=(), vmem=VMEM_LIMIT, aliases=None):
    return pl.pallas_call(
        body, name=name, grid=grid, in_specs=in_specs, out_specs=out_specs,
        out_shape=out_shape, scratch_shapes=list(scratch), input_output_aliases=aliases or {},
        compiler_params=pltpu.CompilerParams(
            dimension_semantics=("arbitrary",) * len(grid), vmem_limit_bytes=vmem))


def _pcall_riding(core, rider, *, name, grid, in_specs, out_specs, out_shape, args, scratch=()):
    n_in, n_out, n_scr = len(in_specs), len(out_specs), len(scratch)
    r_in, r_out = len(rider.arrays), len(rider.out_shapes)

    def body(*refs):
        ins, refs = refs[:n_in], refs[n_in:]
        r_ins, refs = refs[:r_in], refs[r_in:]
        outs, refs = refs[:n_out], refs[n_out:]
        r_outs, refs = refs[:r_out], refs[r_out:]
        scr, sems = refs[:n_scr], refs[n_scr:]
        r_refs = (r_ins, r_outs, sems)
        _ride_begin(rider, r_refs, pl.program_id(0))
        core(*ins, *outs, *scr)
        _ride_end(rider, r_refs, pl.program_id(0), grid[0])

    res = _pcall(body, name=name, grid=grid, in_specs=list(in_specs) + [_ANY] * r_in,
                 out_specs=list(out_specs) + [_ANY] * r_out, out_shape=list(out_shape) + rider.out_shapes,
                 scratch=list(scratch) + rider.scratch)(*args, *rider.arrays)
    return res[:n_out], res[n_out:]


def _rows(ts, width, col=0):
    return pl.BlockSpec((ts, width), lambda i: (i, col))


def _full(shape):
    nd = len(shape)
    return pl.BlockSpec(shape, lambda i: (0,) * nd)


def _dot(a, b):
    return jnp.dot(a.astype(BF16), b.astype(BF16), preferred_element_type=F32)


def _dot_nt(a, b):
    return lax.dot_general(a.astype(BF16), b.astype(BF16), (((1,), (1,)), ((), ())),
                           preferred_element_type=F32)


def _dot_tn(a, b):
    return lax.dot_general(a.astype(BF16), b.astype(BF16), (((0,), (0,)), ((), ())),
                           preferred_element_type=F32)


def _sigmoid(x):
    return 1.0 / (1.0 + jnp.exp(-x))


def _silu(x):
    return x * _sigmoid(x)


def _dsilu(x):
    s = _sigmoid(x)
    return s * (1.0 + x * (1.0 - s))


def _lane_iota(shape):
    return lax.broadcasted_iota(jnp.int32, shape, len(shape) - 1)


def _col(block, idx):
    return jnp.sum(jnp.where(_lane_iota(block.shape) == idx, block, 0.0), axis=-1, keepdims=True)


def _mm(pairs, *, name, ta=False, tb=False, out_dtype=F32, res=None, tm=1024, tn=1024, tk=2048,
        b_chips=False, out_chips=False, rider=None):
    a0, b0 = pairs[0]
    if ta:
        kdim, m = a0.shape
    else:
        m, kdim = a0.shape
    if b_chips and tb:
        n, tk = b0.shape[1], b0.shape[2]
        assert kdim == 4 * tk
    elif b_chips:
        n, tn = 4 * b0.shape[2], b0.shape[2]
        assert kdim == b0.shape[1]
    else:
        n = b0.shape[0] if tb else b0.shape[1]
    if out_chips:
        tn = n // 4
    tm = _tile(m, tm)
    tn = tn if (out_chips or (b_chips and not tb)) else _tile(n, tn)
    tk = tk if (b_chips and tb) else _tile(kdim, tk)
    assert m % tm == 0 and n % tn == 0 and kdim % tk == 0
    nk, npair = kdim // tk, len(pairs)
    grid = (m // tm, n // tn, nk)
    dims = (((0 if ta else 1,), (1 if tb else 0,)), ((), ()))
    n_in = 2 * npair + (res is not None)
    r_in, r_out = (len(rider.arrays), len(rider.out_shapes)) if rider else (0, 0)

    def body(*refs):
        o_ref = refs[n_in + r_in]
        acc = refs[n_in + r_in + 1 + r_out]
        k = pl.program_id(2)
        if rider:
            r_refs = (refs[n_in:n_in + r_in], refs[n_in + r_in + 1:n_in + r_in + 1 + r_out],
                      refs[n_in + r_in + 2 + r_out:])
            step = (pl.program_id(0) * grid[1] + pl.program_id(1)) * nk + k
            _ride_begin(rider, r_refs, step)

        @pl.when(k == 0)
        def _():
            acc[...] = jnp.zeros_like(acc)

        tot = None
        for p in range(npair):
            d = lax.dot_general(refs[2 * p][...].astype(BF16), refs[2 * p + 1][...].astype(BF16),
                                dims, preferred_element_type=F32)
            tot = d if tot is None else tot + d
        acc[...] += tot

        @pl.when(k == nk - 1)
        def _():
            r = acc[...]
            if res is not None:
                r = r + refs[2 * npair][...]
            o_ref[...] = r.astype(out_dtype)

        if rider:
            _ride_end(rider, r_refs, step, grid[0] * grid[1] * nk)

    if ta:
        a_spec = pl.BlockSpec((tk, tm), lambda i, j, k: (k, i))
    else:
        a_spec = pl.BlockSpec((tm, tk), lambda i, j, k: (i, k))
    if b_chips and tb:
        b_spec = pl.BlockSpec((None, tn, tk), lambda i, j, k: (k, j, 0))
    elif b_chips:
        b_spec = pl.BlockSpec((None, tk, tn), lambda i, j, k: (j, k, 0))
    elif tb:
        b_spec = pl.BlockSpec((tn, tk), lambda i, j, k: (j, k))
    else:
        b_spec = pl.BlockSpec((tk, tn), lambda i, j, k: (k, j))
    if out_chips:
        o_spec = pl.BlockSpec((None, tm, tn), lambda i, j, k: (j, i, 0))
        o_shape = SDS((4, m, tn), out_dtype)
    else:
        o_spec = pl.BlockSpec((tm, tn), lambda i, j, k: (i, j))
        o_shape = SDS((m, n), out_dtype)
    in_specs, args = [], []
    for a, b in pairs:
        in_specs += [a_spec, b_spec]
        args += [a, b]
    if res is not None:
        in_specs.append(o_spec)
        args.append(res)
    out_specs, out_shapes, scratch = [o_spec], [o_shape], [pltpu.VMEM((tm, tn), F32)]
    if rider:
        in_specs += [_ANY] * r_in
        args += rider.arrays
        out_specs += [_ANY] * r_out
        out_shapes += rider.out_shapes
        scratch += rider.scratch
    outs = _pcall(body, name=name, grid=grid, in_specs=in_specs, out_specs=out_specs, out_shape=out_shapes,
                  scratch=scratch)(*args)
    return (outs[0], outs[1:]) if rider else outs[0]


def _norm_fwd(x, w, name, rider=None):
    s, d = x.shape
    ts = _tile(s, 512, 8)

    def body(x_ref, w_ref, h_ref):
        xv = x_ref[...]
        r = lax.rsqrt(jnp.mean(xv * xv, axis=-1, keepdims=True) + EPS)
        h_ref[...] = (xv * r * w_ref[...]).astype(BF16)

    spec = dict(name=name, grid=(s // ts,), in_specs=[_rows(ts, d), _full((1, d))])
    if rider is None:
        return _pcall(body, out_specs=_rows(ts, d), out_shape=SDS((s, d), BF16), **spec)(x, w)
    outs, r_outs = _pcall_riding(body, rider, out_specs=[_rows(ts, d)], out_shape=[SDS((s, d), BF16)],
                                 args=(x, w), **spec)
    return outs[0], r_outs


def _norm_bwd(dh, x, w, dres, name, with_bf16):
    s, d = x.shape
    ts = _tile(s, 256, 8)

    def body(dh_ref, x_ref, w_ref, dres_ref, dx_ref, dw_ref, *dx16_ref):
        @pl.when(pl.program_id(0) == 0)
        def _():
            dw_ref[...] = jnp.zeros_like(dw_ref)

        xv, dhv = x_ref[...], dh_ref[...]
        r = lax.rsqrt(jnp.mean(xv * xv, axis=-1, keepdims=True) + EPS)
        xh = xv * r
        dw_ref[...] += jnp.sum(dhv * xh, axis=0, keepdims=True)
        dxh = dhv * w_ref[...]
        dx = dres_ref[...] + r * (dxh - xh * jnp.mean(dxh * xh, axis=-1, keepdims=True))
        dx_ref[...] = dx
        for ref in dx16_ref:
            ref[...] = dx.astype(BF16)

    extra = 1 if with_bf16 else 0
    return _pcall(body, name=name, grid=(s // ts,),
                  in_specs=[_rows(ts, d), _rows(ts, d), _full((1, d)), _rows(ts, d)],
                  out_specs=[_rows(ts, d), _full((1, d))] + [_rows(ts, d)] * extra,
                  out_shape=[SDS((s, d), F32), SDS((1, d), F32)] + [SDS((s, d), BF16)] * extra)(
                      dh, x, w, dres)


def _final_loss(x3, tgt, w):
    s, d = x3.shape
    ts = _tile(s, 256, 8)

    def body(x_ref, t_ref, w_ref, dx_ref, dw_ref, loss_ref, dx16_ref):
        @pl.when(pl.program_id(0) == 0)
        def _():
            dw_ref[...] = jnp.zeros_like(dw_ref)
            loss_ref[...] = jnp.zeros_like(loss_ref)

        xv, wv = x_ref[...], w_ref[...]
        r = lax.rsqrt(jnp.mean(xv * xv, axis=-1, keepdims=True) + EPS)
        xh = xv * r
        err = xh * wv - t_ref[...]
        row = jnp.mean(err * err, axis=-1, keepdims=True)
        loss_ref[...] += 0.5 * jnp.sum(row, axis=0, keepdims=True)
        dy = err * (1.0 / d)
        dw_ref[...] += jnp.sum(dy * xh, axis=0, keepdims=True)
        dxh = dy * wv
        dx = r * (dxh - xh * jnp.mean(dxh * xh, axis=-1, keepdims=True))
        dx_ref[...] = dx
        dx16_ref[...] = dx.astype(BF16)

    return _pcall(body, name="final_loss", grid=(s // ts,),
                  in_specs=[_rows(ts, d), _rows(ts, d), _full((1, d))],
                  out_specs=[_rows(ts, d), _full((1, d)), _full((1, 1)), _rows(ts, d)],
                  out_shape=[SDS((s, d), F32), SDS((1, d), F32), SDS((1, 1), F32), SDS((s, d), BF16)])(
                      x3, tgt, w)


def _shift_down(cur, halo, s):
    if s == 0:
        return cur
    row8 = lax.broadcasted_iota(jnp.int32, halo.shape, 0)
    r = pltpu.roll(cur, s, 0)
    top = jnp.where(row8 < s, pltpu.roll(halo, s, 0), r[0:8])
    return jnp.concatenate([top, r[8:]], axis=0)


def _shift_up(cur, halo, s):
    if s == 0:
        return cur
    ts = cur.shape[0]
    row8 = lax.broadcasted_iota(jnp.int32, halo.shape, 0)
    r = pltpu.roll(cur, ts - s, 0)
    bot = jnp.where(row8 >= 8 - s, pltpu.roll(halo, 8 - s, 0), r[ts - 8:ts])
    return jnp.concatenate([r[:ts - 8], bot], axis=0)


def _chunk_tri(ts, upper):
    i = lax.broadcasted_iota(jnp.int32, (ts, ts), 0)
    j = lax.broadcasted_iota(jnp.int32, (ts, ts), 1)
    same = jnp.right_shift(i, 6) == jnp.right_shift(j, 6)
    return jnp.where(same & ((j >= i) if upper else (j <= i)), 1.0, 0.0).astype(F32)


def _gate_values(m, alog, dtb):
    lane = _lane_iota(m.shape)
    beta = _sigmoid(m)
    xg = m + dtb
    sp = jnp.maximum(xg, 0.0) + jnp.log(1.0 + jnp.exp(-jnp.abs(xg)))
    ga = (lane >= A_LANE) & (lane < A_LANE + 8)
    g = jnp.where(ga, -jnp.exp(alog) * sp, 0.0)
    return beta, g, xg, ga


def _l2_heads(a, nh, scale):
    outs, rs = [], []
    for h in range(nh):
        ah = a[:, HEAD * h:HEAD * (h + 1)]
        r = lax.rsqrt(jnp.sum(ah * ah, axis=-1, keepdims=True) + EPS)
        outs.append(ah * (r * scale))
        rs.append(r)
    return jnp.concatenate(outs, axis=-1), rs


def _gdn_prep(proj, conv_w, alog_l, dtb_l, nh, misc_col):
    s = proj.shape[0]
    w = nh * HEAD
    ts = _tile(s, 256, PAIR)
    hb = ts // 8

    def body(cur_ref, halo_ref, misc_ref, cw_ref, al_ref, db_ref, q_ref, k_ref, v_ref, gb_ref, gbt_ref):
        first = pl.program_id(0) == 0
        outs = (q_ref, k_ref, v_ref)
        for sec in range(3):
            cs = slice(sec * w, (sec + 1) * w)
            cur = cur_ref[:, cs]
            halo = jnp.where(first, 0.0, halo_ref[:, cs])
            pre = None
            for j in range(CONV):
                term = cw_ref[j:j + 1, cs] * _shift_down(cur, halo, CONV - 1 - j)
                pre = term if pre is None else pre + term
            act = _silu(pre)
            if sec == 0:
                act, _ = _l2_heads(act, nh, HEAD ** -0.5)
            elif sec == 1:
                act, _ = _l2_heads(act, nh, 1.0)
            outs[sec][...] = act
        m = misc_ref[...]
        lane = _lane_iota(m.shape)
        beta, g, _, ga = _gate_values(m, al_ref[...], db_ref[...])
        gcc = jnp.dot(_chunk_tri(ts, False), g, precision=lax.Precision.HIGHEST,
                      preferred_element_type=F32)
        gb = jnp.where((lane >= B_LANE) & (lane < B_LANE + 8), beta, jnp.where(ga, gcc, 0.0))
        gb_ref[...] = gb
        gbt_ref[...] = gb.T

    return _pcall(
        body, name="gdn_prep", grid=(s // ts,),
        in_specs=[_rows(ts, 3 * w),
                  pl.BlockSpec((8, 3 * w), lambda i: (jnp.maximum(i * hb - 1, 0), 0)),
                  _rows(ts, LANE, misc_col), _full((CONV, 3 * w)), _full((1, LANE)), _full((1, LANE))],
        out_specs=[_rows(ts, w), _rows(ts, w), _rows(ts, w), _rows(ts, LANE),
                   pl.BlockSpec((LANE, ts), lambda i: (0, i))],
        out_shape=[SDS((s, w), F32), SDS((s, w), F32), SDS((s, w), F32), SDS((s, LANE), F32),
                   SDS((LANE, s), F32)])(proj, proj, proj, conv_w, alog_l, dtb_l)


def _gdn_prep_bwd(proj, conv_w, alog_l, dtb_l, dq, dk, dv, dgb, dkr, dproj, nh, misc_col):
    s = proj.shape[0]
    w = nh * HEAD
    ts = _tile(s, 256, PAIR)
    hb = ts // 8
    assert misc_col % 2 == 0

    def body(cur_ref, halo_ref, misc_ref, cw_ref, al_ref, db_ref, dq_ref, dk_ref, dv_ref, dgb_ref,
             dkr_ref, _, dc_ref, dm_ref, dcw_ref, dal_ref, ddb_ref):
        first = pl.program_id(0) == 0

        @pl.when(first)
        def _():
            dcw_ref[...] = jnp.zeros_like(dcw_ref)
            dal_ref[...] = jnp.zeros_like(dal_ref)
            ddb_ref[...] = jnp.zeros_like(ddb_ref)

        dins = (dq_ref, dk_ref, dv_ref)
        for sec in range(3):
            cs = slice(sec * w, (sec + 1) * w)
            cur = cur_ref[:, cs]
            halo = jnp.where(first, 0.0, halo_ref[:, cs])
            us = [_shift_down(cur, halo, CONV - 1 - j) for j in range(CONV)]
            pre = None
            for j in range(CONV):
                term = cw_ref[j:j + 1, cs] * us[j]
                pre = term if pre is None else pre + term
            act = _silu(pre)
            dout = dins[sec][...]
            if sec < 2:
                scale = HEAD ** -0.5 if sec == 0 else 1.0
                parts = []
                for h in range(nh):
                    hs = slice(HEAD * h, HEAD * (h + 1))
                    ah = act[:, hs]
                    r = lax.rsqrt(jnp.sum(ah * ah, axis=-1, keepdims=True) + EPS)
                    ahat = ah * r
                    dy = dout[:, hs]
                    parts.append((scale * r) * (dy - ahat * jnp.sum(dy * ahat, axis=-1, keepdims=True)))
                dact = jnp.concatenate(parts, axis=-1)
            else:
                dact = dout
            dconv = dact * _dsilu(pre)
            dc_ref[:, cs] = dconv
            for j in range(CONV):
                dcw_ref[j:j + 1, cs] += jnp.sum(dconv * us[j], axis=0, keepdims=True)
        m = misc_ref[...]
        lane = _lane_iota(m.shape)
        al = al_ref[...]
        beta, g, xg, ga = _gate_values(m, al, db_ref[...])
        dgbv = dgb_ref[...]
        dg = jnp.dot(_chunk_tri(ts, True), jnp.where(ga, dgbv, 0.0), precision=lax.Precision.HIGHEST,
                     preferred_element_type=F32)
        da_raw = jnp.where(ga, dg * (-jnp.exp(al)) * _sigmoid(xg), 0.0)
        db_raw = jnp.where((lane >= B_LANE) & (lane < B_LANE + 8), dgbv * beta * (1.0 - beta), 0.0)
        dal_ref[...] += jnp.sum(dg * g, axis=0, keepdims=True)
        ddb_ref[...] += jnp.sum(da_raw, axis=0, keepdims=True)
        dm_ref[:, :LANE] = (dkr_ref[...] + da_raw + db_raw).astype(BF16)
        dm_ref[:, LANE:] = jnp.zeros((ts, LANE), BF16)

    return _pcall(
        body, name="gdn_prep_bwd", grid=(s // ts,),
        in_specs=[_rows(ts, 3 * w),
                  pl.BlockSpec((8, 3 * w), lambda i: (jnp.maximum(i * hb - 1, 0), 0)),
                  _rows(ts, LANE, misc_col), _full((CONV, 3 * w)), _full((1, LANE)), _full((1, LANE)),
                  _rows(ts, w), _rows(ts, w), _rows(ts, w), _rows(ts, LANE), _rows(ts, LANE), _ANY],
        out_specs=[_rows(ts, 3 * w), _rows(ts, 2 * LANE, misc_col // 2), _full((CONV, 3 * w)), _full((1, LANE)),
                   _full((1, LANE))],
        out_shape=[SDS((s, 3 * w), F32), SDS(dproj.shape, BF16), SDS((CONV, 3 * w), F32),
                   SDS((1, LANE), F32), SDS((1, LANE), F32)], aliases={11: 1})(
                       proj, proj, proj, conv_w, alog_l, dtb_l, dq, dk, dv, dgb, dkr, dproj)


def _conv_bwd_input(dconv, conv_w, dproj):
    s, c = dconv.shape
    ts = _tile(s, 256, 8)
    hb = ts // 8
    nblk8 = s // 8
    nt = s // ts

    def body(cur_ref, nxt_ref, cw_ref, _, o_ref):
        last = pl.program_id(0) == nt - 1
        cur = cur_ref[...]
        halo = jnp.where(last, 0.0, nxt_ref[...])
        acc = None
        for j in range(CONV):
            term = cw_ref[j:j + 1, :] * _shift_up(cur, halo, CONV - 1 - j)
            acc = term if acc is None else acc + term
        o_ref[...] = acc.astype(BF16)

    return _pcall(
        body, name="conv_bwd_input", grid=(nt,),
        in_specs=[_rows(ts, c),
                  pl.BlockSpec((8, c), lambda i: (jnp.minimum((i + 1) * hb, nblk8 - 1), 0)),
                  _full((CONV, c)), _ANY],
        out_specs=_rows(ts, c), out_shape=SDS(dproj.shape, BF16), aliases={3: 0})(
            dconv, dconv, conv_w, dproj)


def _inv_unit_lower(a):
    n = a[0].shape[0]
    i = lax.broadcasted_iota(jnp.int32, (n, n), 0)
    j = lax.broadcasted_iota(jnp.int32, (n, n), 1)
    eye = jnp.where(i == j, 1.0, 0.0)
    t = [eye - ah for ah in a]
    x = a
    for _ in range(5):
        x = [_dot(xh, xh) for xh in x]
        t = [th + _dot(th, xh) for th, xh in zip(t, x)]
    return t


def _pair_common(q, k, gcol, grow, bcol):
    i = lax.broadcasted_iota(jnp.int32, (PAIR, PAIR), 0)
    j = lax.broadcasted_iota(jnp.int32, (PAIR, PAIR), 1)
    same = jnp.right_shift(i, 6) == jnp.right_shift(j, 6)
    tril = same & (i >= j)
    strict = same & (i > j)
    dec = [jnp.where(tril, jnp.exp(jnp.minimum(gc - gr, 0.0)), 0.0) for gc, gr in zip(gcol, grow)]
    kk = [_dot_nt(kh, kh) for kh in k]
    qk = [_dot_nt(qh, kh) for qh, kh in zip(q, k)]
    a = [jnp.where(strict, b * kkh * d, 0.0) for b, kkh, d in zip(bcol, kk, dec)]
    t = _inv_unit_lower(a)
    p = [qkh * d for qkh, d in zip(qk, dec)]
    return dec, kk, a, t, p, tril, strict


def _ext(v, a):
    z = jnp.zeros_like(v)
    return jnp.concatenate([v, z] if a == 0 else [z, v], axis=0)


def _gdn_fwd(q, k, v, gb, gbt, nh, rider):
    s = q.shape[0]
    w = nh * HEAD
    npair = s // PAIR

    def body(q_ref, k_ref, v_ref, gb_ref, gbt_ref, o_ref, st_ref, s_ref):
        @pl.when(pl.program_id(0) == 0)
        def _():
            s_ref[...] = jnp.zeros_like(s_ref)

        heads = range(nh)
        hs = [slice(HEAD * h, HEAD * (h + 1)) for h in heads]
        gbv = gb_ref[...]
        q, k, v = [q_ref[:, s_] for s_ in hs], [k_ref[:, s_] for s_ in hs], [v_ref[:, s_] for s_ in hs]
        gcol = [_col(gbv, A_LANE + h) for h in heads]
        bcol = [_col(gbv, B_LANE + h) for h in heads]
        grow = [gbt_ref[A_LANE + h:A_LANE + h + 1, :] for h in heads]
        _, _, _, t, p, _, _ = _pair_common(q, k, gcol, grow, bcol)
        eg = [jnp.exp(gc) for gc in gcol]
        qg = [x * e for x, e in zip(q, eg)]
        kg = [x * e for x, e in zip(k, eg)]
        outs = []
        for a in range(2):
            sl = slice(CHUNK * a, CHUNK * (a + 1))
            st = [s_ref[h] for h in heads]
            for h in heads:
                st_ref[a, h] = st[h]
            r = [v[h][sl] - _dot(kg[h][sl], st[h]) for h in heads]
            vn = [_dot(t[h][sl], _ext(bcol[h][sl] * r[h], a)) for h in heads]
            outs.append([_dot(qg[h][sl], st[h]) + _dot(p[h][sl], _ext(vn[h], a)) for h in heads])
            gl = [_col(gr, CHUNK * (a + 1) - 1) for gr in grow]
            kd = [k[h][sl] * jnp.exp(gl[h] - gcol[h][sl]) for h in heads]
            upd = [_dot_tn(kd[h], vn[h]) for h in heads]
            for h in heads:
                s_ref[h] = jnp.exp(gl[h]) * st[h] + upd[h]
        for h in heads:
            o_ref[:, hs[h]] = jnp.concatenate([outs[0][h], outs[1][h]], axis=0)

    return _pcall_riding(
        body, rider, name="gdn_fwd", grid=(npair,),
        in_specs=[_rows(PAIR, w), _rows(PAIR, w), _rows(PAIR, w), _rows(PAIR, LANE),
                  pl.BlockSpec((LANE, PAIR), lambda i: (0, i))],
        out_specs=[_rows(PAIR, w), pl.BlockSpec((2, nh, HEAD, HEAD), lambda i: (i, 0, 0, 0))],
        out_shape=[SDS((s, w), F32), SDS((2 * npair, nh, HEAD, HEAD), F32)],
        scratch=[pltpu.VMEM((nh, HEAD, HEAD), F32)], args=(q, k, v, gb, gbt))


def _gdn_bwd(q, k, v, gb, gbt, states, do, nh, rider):
    s = q.shape[0]
    w = nh * HEAD
    npair = s // PAIR
    rev = lambda i: (npair - 1 - i, 0)

    def body(q_ref, k_ref, v_ref, gb_ref, gbt_ref, st_ref, do_ref, dq_ref, dk_ref, dv_ref, dgb_ref,
             ds_ref):
        @pl.when(pl.program_id(0) == 0)
        def _():
            ds_ref[...] = jnp.zeros_like(ds_ref)

        lane = _lane_iota((PAIR, LANE))
        row = lax.broadcasted_iota(jnp.int32, (CHUNK, 1), 0)
        heads = range(nh)
        hs = [slice(HEAD * h, HEAD * (h + 1)) for h in heads]
        gbv = gb_ref[...]
        q, k, v = [q_ref[:, s_] for s_ in hs], [k_ref[:, s_] for s_ in hs], [v_ref[:, s_] for s_ in hs]
        do = [do_ref[:, s_] for s_ in hs]
        gcol = [_col(gbv, A_LANE + h) for h in heads]
        bcol = [_col(gbv, B_LANE + h) for h in heads]
        grow = [gbt_ref[A_LANE + h:A_LANE + h + 1, :] for h in heads]
        dec, kk, amat, t, p, tril, strict = _pair_common(q, k, gcol, grow, bcol)
        tt, pt = [x.T for x in t], [x.T for x in p]
        eg = [jnp.exp(gc) for gc in gcol]
        qg = [x * e for x, e in zip(q, eg)]
        kg = [x * e for x, e in zip(k, eg)]
        sums = lambda x: jnp.sum(x, axis=-1, keepdims=True)
        rs, vns = [None, None], [None, None]
        for a in range(2):
            sl = slice(CHUNK * a, CHUNK * (a + 1))
            rs[a] = [v[h][sl] - _dot(kg[h][sl], st_ref[a, h]) for h in heads]
            vns[a] = [_dot(t[h][sl], _ext(bcol[h][sl] * rs[a][h], a)) for h in heads]
        dsn = [ds_ref[h] for h in heads]
        dqs, dks, dvs, dgcs, dbs, drbs = ([None, None] for _ in range(6))
        for a in (1, 0):
            sl = slice(CHUNK * a, CHUNK * (a + 1))
            st = [st_ref[a, h] for h in heads]
            gl = [_col(gr, CHUNK * (a + 1) - 1) for gr in grow]
            egl = [jnp.exp(x) for x in gl]
            dk_dec = [jnp.exp(gl[h] - gcol[h][sl]) for h in heads]
            kd = [k[h][sl] * dk_dec[h] for h in heads]
            d_vn = [_dot(pt[h][sl], _ext(do[h][sl], a)) + _dot(kd[h], dsn[h]) for h in heads]
            d_qg = [_dot_nt(do[h][sl], st[h]) for h in heads]
            d_rb = [_dot(tt[h][sl], _ext(d_vn[h], a)) for h in heads]
            d_r = [bcol[h][sl] * d_rb[h] for h in heads]
            d_kg = [-_dot_nt(d_r[h], st[h]) for h in heads]
            d_kd = [_dot_nt(vns[a][h], dsn[h]) for h in heads]
            dsn_new = [_dot_tn(qg[h][sl], do[h][sl]) - _dot_tn(kg[h][sl], d_r[h]) for h in heads]
            dbs[a] = [sums(d_rb[h] * rs[a][h]) for h in heads]
            dgl = [egl[h] * jnp.sum(dsn[h] * st[h], keepdims=True) + jnp.sum(d_kd[h] * kd[h], keepdims=True)
                   for h in heads]
            dgcs[a] = [sums(d_qg[h] * qg[h][sl]) + sums(d_kg[h] * kg[h][sl]) - sums(d_kd[h] * kd[h])
                       + jnp.where(row == CHUNK - 1, dgl[h], 0.0) for h in heads]
            dqs[a] = [d_qg[h] * eg[h][sl] for h in heads]
            dks[a] = [d_kg[h] * eg[h][sl] + d_kd[h] * dk_dec[h] for h in heads]
            dvs[a] = d_r
            drbs[a] = d_rb
            dsn = [dsn_new[h] + egl[h] * dsn[h] for h in heads]
        for h in heads:
            ds_ref[h] = dsn[h]
        cat = lambda xs, h: jnp.concatenate([xs[0][h], xs[1][h]], axis=0)
        vn = [cat(vns, h) for h in heads]
        d_rb = [cat(drbs, h) for h in heads]
        dp = [jnp.where(tril, _dot_nt(do[h], vn[h]), 0.0) for h in heads]
        dam = [jnp.where(strict, -_dot_nt(d_rb[h], vn[h]), 0.0) for h in heads]
        g_p = [dp[h] * dec[h] for h in heads]
        g_a = [dam[h] * dec[h] for h in heads]
        gbk = [bcol[h] * g_a[h] for h in heads]
        dq2 = [_dot(g_p[h], k[h]) for h in heads]
        dk2 = [_dot_tn(g_p[h], q[h]) + _dot(gbk[h], k[h]) + _dot_tn(gbk[h], k[h]) for h in heads]
        dgb = jnp.zeros((PAIR, LANE), F32)
        for h in heads:
            dq_ref[:, hs[h]] = cat(dqs, h) + dq2[h]
            dk_ref[:, hs[h]] = cat(dks, h) + dk2[h]
            dv_ref[:, hs[h]] = cat(dvs, h)
            dbeta = cat(dbs, h) + sums(g_a[h] * kk[h])
            mm = dp[h] * p[h] + dam[h] * amat[h]
            dgc = cat(dgcs, h) + sums(mm) - sums(mm.T)
            dgb = dgb + jnp.where(lane == A_LANE + h, dgc, 0.0) + jnp.where(lane == B_LANE + h, dbeta, 0.0)
        dgb_ref[...] = dgb

    return _pcall_riding(
        body, rider, name="gdn_bwd", grid=(npair,),
        in_specs=[pl.BlockSpec((PAIR, w), rev), pl.BlockSpec((PAIR, w), rev), pl.BlockSpec((PAIR, w), rev),
                  pl.BlockSpec((PAIR, LANE), rev),
                  pl.BlockSpec((LANE, PAIR), lambda i: (0, npair - 1 - i)),
                  pl.BlockSpec((2, nh, HEAD, HEAD), lambda i: (npair - 1 - i, 0, 0, 0)),
                  pl.BlockSpec((PAIR, w), rev)],
        out_specs=[pl.BlockSpec((PAIR, w), rev), pl.BlockSpec((PAIR, w), rev), pl.BlockSpec((PAIR, w), rev),
                   pl.BlockSpec((PAIR, LANE), rev)],
        out_shape=[SDS((s, w), F32), SDS((s, w), F32), SDS((s, w), F32), SDS((s, LANE), F32)],
        scratch=[pltpu.VMEM((nh, HEAD, HEAD), F32)], args=(q, k, v, gb, gbt, states, do))


def _mla_norm(proj, qw, kvw, col_q, col_kv):
    s = proj.shape[0]
    lr = qw.shape[1]
    ts = _tile(s, 512, 8)

    def body(cq_ref, ckv_ref, qw_ref, kvw_ref, oq_ref, okv_ref):
        for x_ref, w_ref, o_ref in ((cq_ref, qw_ref, oq_ref), (ckv_ref, kvw_ref, okv_ref)):
            xv = x_ref[...]
            r = lax.rsqrt(jnp.mean(xv * xv, axis=-1, keepdims=True) + EPS)
            o_ref[...] = (xv * r * w_ref[...]).astype(BF16)

    return _pcall(body, name="mla_norm", grid=(s // ts,),
                  in_specs=[_rows(ts, lr, col_q), _rows(ts, lr, col_kv), _full((1, lr)), _full((1, lr))],
                  out_specs=[_rows(ts, lr), _rows(ts, lr)],
                  out_shape=[SDS((s, lr), BF16), SDS((s, lr), BF16)])(proj, proj, qw, kvw)


def _mla_norm_bwd(proj, qw, kvw, dq, dkv, dproj, col_q, col_kv):
    s = proj.shape[0]
    lr = qw.shape[1]
    ts = _tile(s, 512, 8)

    assert col_kv == col_q + 1 and col_q % 2 == 0

    def body(cq_ref, ckv_ref, qw_ref, kvw_ref, dq_ref, dkv_ref, _, o_ref, dqw_ref, dkvw_ref):
        @pl.when(pl.program_id(0) == 0)
        def _():
            dqw_ref[...] = jnp.zeros_like(dqw_ref)
            dkvw_ref[...] = jnp.zeros_like(dkvw_ref)

        for k, (x_ref, w_ref, d_ref, dw_ref) in enumerate(((cq_ref, qw_ref, dq_ref, dqw_ref),
                                                           (ckv_ref, kvw_ref, dkv_ref, dkvw_ref))):
            xv, dh = x_ref[...], d_ref[...]
            r = lax.rsqrt(jnp.mean(xv * xv, axis=-1, keepdims=True) + EPS)
            xh = xv * r
            dw_ref[...] += jnp.sum(dh * xh, axis=0, keepdims=True)
            dxh = dh * w_ref[...]
            o_ref[:, lr * k:lr * (k + 1)] = (
                r * (dxh - xh * jnp.mean(dxh * xh, axis=-1, keepdims=True))).astype(BF16)

    return _pcall(body, name="mla_norm_bwd", grid=(s // ts,),
                  in_specs=[_rows(ts, lr, col_q), _rows(ts, lr, col_kv), _full((1, lr)), _full((1, lr)),
                            _rows(ts, lr), _rows(ts, lr), _ANY],
                  out_specs=[_rows(ts, 2 * lr, col_q // 2), _full((1, lr)), _full((1, lr))],
                  out_shape=[SDS(dproj.shape, BF16), SDS((1, lr), F32), SDS((1, lr), F32)],
                  aliases={6: 0})(proj, proj, qw, kvw, dq, dkv, dproj)


def _rope_tables(pos, invf, sgn):
    ang = pos * invf
    return jnp.cos(ang), jnp.sin(ang) * sgn


def _swap_halves_lanes(y):
    lane = _lane_iota(y.shape)
    return jnp.where(lane < ROPE // 2, pltpu.roll(y, LANE - ROPE // 2, 1), pltpu.roll(y, ROPE // 2, 1))


def _rope_consts():
    half = ROPE // 2
    inv = ROPE_THETA ** (-jnp.arange(half, dtype=F32) / half)
    invf = jnp.concatenate([inv, inv, jnp.zeros((LANE - ROPE,), F32)])[None, :]
    sgn = jnp.concatenate([-jnp.ones((half,), F32), jnp.ones((half,), F32),
                           jnp.zeros((LANE - ROPE,), F32)])[None, :]
    return invf, sgn


def _mla_rope(qraw, kvraw, proj, pos, nh, misc_col):
    s = qraw.shape[0]
    ts = _tile(s, MLA_BLOCK)
    wq = nh * 2 * HEAD
    invf, sgn = _rope_consts()

    def body(q_ref, kv_ref, misc_ref, pos_ref, if_ref, sg_ref, qc_ref, kc_ref, v_ref, qt_ref, vt_ref):
        c, sn = _rope_tables(pos_ref[...], if_ref[...], sg_ref[...])
        lane = _lane_iota(c.shape)
        rot = lambda xb: xb * c + _swap_halves_lanes(xb) * sn
        qs = SM_SCALE * LOG2E
        krot = jnp.where(lane < ROPE, rot(misc_ref[...]), 0.0).astype(BF16)
        for h in range(nh):
            b0 = 2 * HEAD * h
            qn = q_ref[:, b0:b0 + HEAD].astype(F32) * qs
            qr = rot(q_ref[:, b0 + HEAD:b0 + 2 * HEAD].astype(F32)) * qs
            qc_ref[:, b0:b0 + HEAD] = qn.astype(BF16)
            qc_ref[:, b0 + HEAD:b0 + 2 * HEAD] = qr.astype(BF16)
            qt_ref[b0:b0 + HEAD, :] = qn.T.astype(BF16)
            qt_ref[b0 + HEAD:b0 + 2 * HEAD, :] = qr.T.astype(BF16)
            kc_ref[:, b0:b0 + HEAD] = kv_ref[:, b0:b0 + HEAD].astype(BF16)
            kc_ref[:, b0 + HEAD:b0 + 2 * HEAD] = krot
            vh = kv_ref[:, wq + HEAD * h:wq + HEAD * (h + 1)]
            v_ref[:, HEAD * h:HEAD * (h + 1)] = vh.astype(BF16)
            vt_ref[HEAD * h:HEAD * (h + 1), :] = vh.astype(F32).T.astype(BF16)

    return _pcall(body, name="mla_rope", grid=(s // ts,),
                  in_specs=[_rows(ts, wq), _rows(ts, wq + nh * HEAD), _rows(ts, LANE, misc_col),
                            _rows(ts, 1), _full((1, LANE)), _full((1, LANE))],
                  out_specs=[_rows(ts, wq), _rows(ts, wq), _rows(ts, nh * HEAD),
                             pl.BlockSpec((None, wq, ts), lambda i: (i, 0, 0)),
                             pl.BlockSpec((None, nh * HEAD, ts), lambda i: (i, 0, 0))],
                  out_shape=[SDS((s, wq), BF16), SDS((s, wq), BF16), SDS((s, nh * HEAD), BF16),
                             SDS((s // ts, wq, ts), BF16), SDS((s // ts, nh * HEAD, ts), BF16)])(
                      qraw, kvraw, proj, pos, invf, sgn)


def _mla_rope_bwd(dqc, dkc, dv, pos, nh):
    s = dqc.shape[0]
    ts = _tile(s, 256, 8)
    wq = nh * 2 * HEAD
    invf, sgn = _rope_consts()

    def body(dq_ref, dk_ref, dv_ref, pos_ref, if_ref, sg_ref, oq_ref, okv_ref, okr_ref):
        c, sn = _rope_tables(pos_ref[...], if_ref[...], sg_ref[...])
        lane = _lane_iota(c.shape)
        unrot = lambda d: d * c + _swap_halves_lanes(d * sn)
        dkr = jnp.zeros(c.shape, F32)
        for h in range(nh):
            b0 = 2 * HEAD * h
            oq_ref[:, b0:b0 + HEAD] = (dq_ref[:, b0:b0 + HEAD] * SM_SCALE).astype(BF16)
            oq_ref[:, b0 + HEAD:b0 + 2 * HEAD] = (
                unrot(dq_ref[:, b0 + HEAD:b0 + 2 * HEAD]) * SM_SCALE).astype(BF16)
            okv_ref[:, b0:b0 + HEAD] = (dk_ref[:, b0:b0 + HEAD] * LN2).astype(BF16)
            okv_ref[:, b0 + HEAD:b0 + 2 * HEAD] = jnp.zeros((ts, HEAD), BF16)
            dkr = dkr + dk_ref[:, b0 + HEAD:b0 + 2 * HEAD]
        okv_ref[:, wq:] = dv_ref[...].astype(BF16)
        okr_ref[...] = jnp.where(lane < ROPE, unrot(jnp.where(lane < ROPE, dkr * LN2, 0.0)), 0.0)

    return _pcall(body, name="mla_rope_bwd", grid=(s // ts,),
                  in_specs=[_rows(ts, wq), _rows(ts, wq), _rows(ts, nh * HEAD), _rows(ts, 1),
                            _full((1, LANE)), _full((1, LANE))],
                  out_specs=[_rows(ts, wq), _rows(ts, wq + nh * HEAD), _rows(ts, LANE)],
                  out_shape=[SDS((s, wq), BF16), SDS((s, wq + nh * HEAD), BF16), SDS((s, LANE), F32)])(
                      dqc, dkc, dv, pos, invf, sgn)


def _causal_mask(blk):
    i = lax.broadcasted_iota(jnp.int32, (blk, blk), 0)
    j = lax.broadcasted_iota(jnp.int32, (blk, blk), 1)
    return j <= i


MLA_HP = 2
MLA_FWD_HP = 4


def _pair_pack(a, b):
    return jnp.where(_lane_iota(a.shape) < LANE // 2, a, b)


def _pair_unpack(x, e):
    lane = _lane_iota(x.shape)
    keep = (lane < LANE // 2) if e == 0 else (lane >= LANE // 2)
    return jnp.where(keep, x, pltpu.roll(x, LANE // 2, 1))


def _mla_fwd(qt, kc, vt, nh, rider):
    nb, _, blk = qt.shape
    s = nb * blk
    hp = MLA_FWD_HP if nh % MLA_FWD_HP == 0 else MLA_HP
    assert nh % hp == 0 and hp % MLA_HP == 0
    once = pl.Buffered(1)
    r_in, r_out = len(rider.arrays), len(rider.out_shapes)

    def body(*refs):
        qt_ref, k_ref, vt_ref = refs[:3]
        o_ref, lse_ref = refs[3 + r_in:5 + r_in]
        m_sc, l_sc, acc = refs[5 + r_in + r_out:8 + r_in + r_out]
        r_refs = (refs[3:3 + r_in], refs[5 + r_in:5 + r_in + r_out], refs[8 + r_in + r_out:])
        i = pl.program_id(1)
        grid_step = pl.program_id(0) * nb + i
        _ride_begin(rider, r_refs, grid_step)
        m_sc[...] = jnp.full_like(m_sc, -1e30)
        l_sc[...] = jnp.zeros_like(l_sc)
        acc[...] = jnp.zeros_like(acc)
        es = range(hp)

        def step(j, masked):
            rows = pl.ds(pl.multiple_of(j * blk, blk), blk)
            sc = [_dot(k_ref[rows, 2 * HEAD * e:2 * HEAD * (e + 1)], qt_ref[2 * HEAD * e:2 * HEAD * (e + 1), :])
                  for e in es]
            if masked:
                key = lax.broadcasted_iota(jnp.int32, (blk, blk), 0)
                qry = lax.broadcasted_iota(jnp.int32, (blk, blk), 1)
                sc = [jnp.where(key <= qry, x, -1e30) for x in sc]
            m_prev = [m_sc[e] for e in es]
            m_new = [jnp.maximum(m_prev[e], jnp.max(sc[e], axis=0, keepdims=True)) for e in es]
            p = [jnp.exp2(sc[e] - m_new[e]) for e in es]
            alpha = [jnp.exp2(m_prev[e] - m_new[e]) for e in es]
            pv = [_dot(vt_ref[j, HEAD * e:HEAD * (e + 1), :], p[e]) for e in es]
            for e in es:
                l_sc[e] = alpha[e] * l_sc[e] + jnp.sum(p[e], axis=0, keepdims=True)
                acc[e] = alpha[e] * acc[e] + pv[e]
                m_sc[e] = m_new[e]

        def loop_body(j, carry):
            step(j, False)
            return carry

        lax.fori_loop(0, i, loop_body, 0)
        step(i, True)
        lse = []
        for e in es:
            o_ref[:, HEAD * e:HEAD * (e + 1)] = (acc[e] / l_sc[e]).T
            row = m_sc[e] + jnp.log(l_sc[e]) * LOG2E
            lse.append(jnp.broadcast_to(row, (LANE, blk)).T)
        for g in range(hp // MLA_HP):
            lse_ref[g] = _pair_pack(lse[MLA_HP * g], lse[MLA_HP * g + 1])
        _ride_end(rider, r_refs, grid_step, (nh // hp) * nb)

    outs = _pcall(
        body, name="mla_fwd", grid=(nh // hp, nb),
        in_specs=[pl.BlockSpec((None, hp * 2 * HEAD, blk), lambda g, i: (i, g, 0)),
                  pl.BlockSpec((s, hp * 2 * HEAD), lambda g, i: (0, g), pipeline_mode=once),
                  pl.BlockSpec((nb, hp * HEAD, blk), lambda g, i: (0, g, 0), pipeline_mode=once)]
        + [_ANY] * r_in,
        out_specs=[pl.BlockSpec((blk, hp * HEAD), lambda g, i: (i, g)),
                   pl.BlockSpec((hp // MLA_HP, blk, LANE), lambda g, i: (g, i, 0))] + [_ANY] * r_out,
        out_shape=[SDS((s, nh * HEAD), F32), SDS((nh // MLA_HP, s, LANE), F32)] + rider.out_shapes,
        scratch=[pltpu.VMEM((hp, 1, blk), F32), pltpu.VMEM((hp, 1, blk), F32),
                 pltpu.VMEM((hp, HEAD, blk), F32)] + rider.scratch)(qt, kc, vt, *rider.arrays)
    return outs[0], outs[1], outs[2:]


def _mla_bwd(qc, kc, v, do, lse, delta, nh, rider):
    s = qc.shape[0]
    blk = _tile(s, MLA_BLOCK)
    nb = s // blk
    rep = blk // LANE
    hp = MLA_HP
    once = pl.Buffered(1)
    r_in, r_out = len(rider.arrays), len(rider.out_shapes)
    qs = [slice(2 * HEAD * e, 2 * HEAD * (e + 1)) for e in range(hp)]
    vs = [slice(HEAD * e, HEAD * (e + 1)) for e in range(hp)]

    def body(*refs):
        q_ref, do_ref, lse_ref, dl_ref, k_ref, v_ref = refs[:6]
        dq_ref, dk_ref, dv_ref = refs[6 + r_in:9 + r_in]
        dk_acc, dv_acc = refs[9 + r_in + r_out:11 + r_in + r_out]
        r_refs = (refs[6:6 + r_in], refs[9 + r_in:9 + r_in + r_out], refs[11 + r_in + r_out:])
        j = pl.program_id(1)
        grid_step = pl.program_id(0) * nb + j
        _ride_begin(rider, r_refs, grid_step)

        @pl.when(j == 0)
        def _():
            dq_ref[...] = jnp.zeros_like(dq_ref)

        dk_acc[...] = jnp.zeros_like(dk_acc)
        dv_acc[...] = jnp.zeros_like(dv_acc)
        es = range(hp)
        kj = [k_ref[:, qs[e]] for e in es]
        vj = [v_ref[:, vs[e]] for e in es]

        def step(i, masked):
            rows = pl.ds(pl.multiple_of(i * blk, blk), blk)
            qi = [q_ref[rows, qs[e]] for e in es]
            doi = [do_ref[rows, vs[e]] for e in es]
            lse, dl = lse_ref[rows, :], dl_ref[rows, :]
            sc = [_dot_nt(qi[e], kj[e]) for e in es]
            dp = [_dot_nt(doi[e], vj[e]) for e in es]
            if masked:
                sc = [jnp.where(_causal_mask(blk), x, -1e30) for x in sc]
            p = [jnp.exp2(sc[e] - jnp.tile(_pair_unpack(lse, e), (1, rep))) for e in es]
            ds = [p[e] * (dp[e] - jnp.tile(_pair_unpack(dl, e), (1, rep))) for e in es]
            dv = [_dot_tn(p[e], doi[e]) for e in es]
            dk = [_dot_tn(ds[e], qi[e]) for e in es]
            dq = [_dot(ds[e], kj[e]) for e in es]
            for e in es:
                dv_acc[:, vs[e]] += dv[e]
                dk_acc[:, qs[e]] += dk[e]
                dq_ref[rows, qs[e]] += dq[e]

        def loop_body(i, carry):
            step(i, False)
            return carry

        step(j, True)
        lax.fori_loop(j + 1, nb, loop_body, 0)
        dk_ref[...] = dk_acc[...]
        dv_ref[...] = dv_acc[...]
        _ride_end(rider, r_refs, grid_step, (nh // hp) * nb)

    outs = _pcall(
        body, name="mla_bwd", grid=(nh // hp, nb),
        in_specs=[pl.BlockSpec((s, hp * 2 * HEAD), lambda g, j: (0, g), pipeline_mode=once),
                  pl.BlockSpec((s, hp * HEAD), lambda g, j: (0, g), pipeline_mode=once),
                  pl.BlockSpec((None, s, LANE), lambda g, j: (g, 0, 0), pipeline_mode=once),
                  pl.BlockSpec((None, s, LANE), lambda g, j: (g, 0, 0), pipeline_mode=once),
                  pl.BlockSpec((blk, hp * 2 * HEAD), lambda g, j: (j, g)),
                  pl.BlockSpec((blk, hp * HEAD), lambda g, j: (j, g))] + [_ANY] * r_in,
        out_specs=[pl.BlockSpec((s, hp * 2 * HEAD), lambda g, j: (0, g), pipeline_mode=once),
                   pl.BlockSpec((blk, hp * 2 * HEAD), lambda g, j: (j, g)),
                   pl.BlockSpec((blk, hp * HEAD), lambda g, j: (j, g))] + [_ANY] * r_out,
        out_shape=[SDS((s, nh * 2 * HEAD), F32), SDS((s, nh * 2 * HEAD), F32),
                   SDS((s, nh * HEAD), F32)] + rider.out_shapes,
        scratch=[pltpu.VMEM((blk, hp * 2 * HEAD), F32), pltpu.VMEM((blk, hp * HEAD), F32)] + rider.scratch,
        vmem=VMEM_LIMIT_WIDE)(qc, do, lse, delta, kc, v, *rider.arrays)
    return outs[0], outs[1], outs[2], outs[3:]


def _mix_fwd(og, proj, om, gw, mw, nh, z_col):
    s = og.shape[0]
    w = nh * HEAD
    ts = _tile(s, 256, 8)

    def body(og_ref, z_ref, om_ref, gw_ref, mw_ref, o_ref):
        for h in range(nh):
            hs = slice(HEAD * h, HEAD * (h + 1))
            a = og_ref[:, hs]
            r = lax.rsqrt(jnp.mean(a * a, axis=-1, keepdims=True) + EPS)
            o_ref[:, hs] = (a * r * gw_ref[...] * _silu(z_ref[:, hs])).astype(BF16)
            b = om_ref[:, hs]
            r = lax.rsqrt(jnp.mean(b * b, axis=-1, keepdims=True) + EPS)
            o_ref[:, w + HEAD * h:w + HEAD * (h + 1)] = (b * r * mw_ref[...]).astype(BF16)

    return _pcall(body, name="mix_fwd", grid=(s // ts,),
                  in_specs=[_rows(ts, w), _rows(ts, w, z_col), _rows(ts, w), _full((1, HEAD)),
                            _full((1, HEAD))],
                  out_specs=_rows(ts, 2 * w), out_shape=SDS((s, 2 * w), BF16))(og, proj, om, gw, mw)


def _mix_bwd(dmix, og, proj, om, gw, mw, nh, z_col):
    s = og.shape[0]
    w = nh * HEAD
    ts = _tile(s, 256, 8)

    def body(d_ref, og_ref, z_ref, om_ref, gw_ref, mw_ref, dog_ref, dz_ref, dom_ref, dgw_ref, dmw_ref,
             dl_ref):
        @pl.when(pl.program_id(0) == 0)
        def _():
            dgw_ref[...] = jnp.zeros_like(dgw_ref)
            dmw_ref[...] = jnp.zeros_like(dmw_ref)

        dgw = jnp.zeros((1, HEAD), F32)
        dmw = jnp.zeros((1, HEAD), F32)
        deltas = []
        for h in range(nh):
            hs = slice(HEAD * h, HEAD * (h + 1))
            a, z, dy = og_ref[:, hs], z_ref[:, hs], d_ref[:, hs]
            r = lax.rsqrt(jnp.mean(a * a, axis=-1, keepdims=True) + EPS)
            ah = a * r
            sz = _silu(z)
            dz_ref[:, hs] = (dy * (ah * gw_ref[...]) * _dsilu(z)).astype(BF16)
            dn = dy * sz
            dgw = dgw + jnp.sum(dn * ah, axis=0, keepdims=True)
            dah = dn * gw_ref[...]
            dog_ref[:, hs] = r * (dah - ah * jnp.mean(dah * ah, axis=-1, keepdims=True))
            b, dyb = om_ref[:, hs], d_ref[:, w + HEAD * h:w + HEAD * (h + 1)]
            r = lax.rsqrt(jnp.mean(b * b, axis=-1, keepdims=True) + EPS)
            bh = b * r
            dmw = dmw + jnp.sum(dyb * bh, axis=0, keepdims=True)
            dbh = dyb * mw_ref[...]
            dom = r * (dbh - bh * jnp.mean(dbh * bh, axis=-1, keepdims=True))
            dom_ref[:, hs] = dom.astype(BF16)
            deltas.append(jnp.broadcast_to(jnp.sum(dom * b, axis=-1, keepdims=True), (ts, LANE)))
        for g in range(nh // MLA_HP):
            dl_ref[g] = _pair_pack(deltas[2 * g], deltas[2 * g + 1])
        dgw_ref[...] += dgw
        dmw_ref[...] += dmw

    return _pcall(body, name="mix_bwd", grid=(s // ts,),
                  in_specs=[_rows(ts, 2 * w), _rows(ts, w), _rows(ts, w, z_col), _rows(ts, w),
                            _full((1, HEAD)), _full((1, HEAD))],
                  out_specs=[_rows(ts, w), _rows(ts, w, z_col), _rows(ts, w), _full((1, HEAD)), _full((1, HEAD)),
                             pl.BlockSpec((nh // MLA_HP, ts, LANE), lambda i: (0, i, 0))],
                  out_shape=[SDS((s, w), F32), SDS((s, proj.shape[1]), BF16), SDS((s, w), BF16),
                             SDS((1, HEAD), F32), SDS((1, HEAD), F32),
                             SDS((nh // MLA_HP, s, LANE), F32)])(dmix, og, proj, om, gw, mw)


def _swiglu_fwd(h2, wg, wu):
    m, kdim = h2.shape
    tn = wg.shape[2]
    n = 4 * tn
    tm, tk = _tile(m, 512), _tile(kdim, 2048)
    nk = kdim // tk

    def body(a_ref, g_ref, u_ref, act_ref, go_ref, uo_ref, gacc, uacc):
        k = pl.program_id(2)

        @pl.when(k == 0)
        def _():
            gacc[...] = jnp.zeros_like(gacc)
            uacc[...] = jnp.zeros_like(uacc)

        a = a_ref[...]
        gacc[...] += _dot(a, g_ref[...])
        uacc[...] += _dot(a, u_ref[...])

        @pl.when(k == nk - 1)
        def _():
            g, u = gacc[...], uacc[...]
            act_ref[...] = (_silu(g) * u).astype(BF16)
            go_ref[...] = g.astype(BF16)
            uo_ref[...] = u.astype(BF16)

    a_spec = pl.BlockSpec((tm, tk), lambda i, j, k: (i, k))
    b_spec = pl.BlockSpec((None, tk, tn), lambda i, j, k: (j, k, 0))
    o_spec = pl.BlockSpec((tm, tn), lambda i, j, k: (i, j))
    return _pcall(body, name="swiglu_fwd", grid=(m // tm, n // tn, nk),
                  in_specs=[a_spec, b_spec, b_spec], out_specs=[o_spec] * 3,
                  out_shape=[SDS((m, n), BF16)] * 3,
                  scratch=[pltpu.VMEM((tm, tn), F32), pltpu.VMEM((tm, tn), F32)])(h2, wg, wu)


def _swiglu_bwd(dx3, wd, g, u):
    m, kdim = dx3.shape
    n = wd.shape[0]
    tm, tn = _tile(m, 1024), _tile(n, 512)
    parts = 2 if tm % 16 == 0 else 1
    th = tm // parts

    def body(a_ref, b_ref, g_ref, u_ref, dg_ref, du_ref):
        b = b_ref[...]
        rows = [slice(th * c, th * (c + 1)) for c in range(parts)]
        da = [_dot_nt(a_ref[rs, :], b) for rs in rows]
        for rs, d in zip(rows, da):
            gv, uv = g_ref[rs, :].astype(F32), u_ref[rs, :].astype(F32)
            dg_ref[rs, :] = (d * uv * _dsilu(gv)).astype(BF16)
            du_ref[rs, :] = (d * _silu(gv)).astype(BF16)

    a_spec = pl.BlockSpec((tm, kdim), lambda i, j: (i, 0))
    b_spec = pl.BlockSpec((tn, kdim), lambda i, j: (j, 0))
    o_spec = pl.BlockSpec((tm, tn), lambda i, j: (i, j))
    return _pcall(body, name="swiglu_bwd", grid=(m // tm, n // tn),
                  in_specs=[a_spec, b_spec, o_spec, o_spec], out_specs=[o_spec] * 2,
                  out_shape=[SDS((m, n), BF16)] * 2)(dx3, wd, g, u)


def _sum_pair(g, recv, place, name):
    _, _, rh, c = g.shape
    tr = _tile(rh, 256, 16)

    def body(pl_ref, g_ref, r_ref, o16_ref, own_ref):
        sm = g_ref[...].astype(F32) + r_ref[...].astype(F32)
        o16_ref[...] = sm.astype(BF16)

        @pl.when(pl.program_id(1) == pl_ref[1])
        def _():
            own_ref[...] = sm

    grid_spec = pltpu.PrefetchScalarGridSpec(
        num_scalar_prefetch=1, grid=(rh // tr, 4),
        in_specs=[pl.BlockSpec((None, None, tr, c), lambda i, t, p: (t, p[0], i, 0)),
                  pl.BlockSpec((None, tr, c), lambda i, t, p: (t, i, 0))],
        out_specs=[pl.BlockSpec((None, tr, c), lambda i, t, p: (t, i, 0)),
                   pl.BlockSpec((tr, c), lambda i, t, p: (i, 0))])
    return pl.pallas_call(
        body, name=name, grid_spec=grid_spec,
        out_shape=[SDS((4, rh, c), BF16), SDS((rh, c), F32)],
        compiler_params=pltpu.CompilerParams(dimension_semantics=("arbitrary",) * 2,
                                             vmem_limit_bytes=VMEM_LIMIT))(place, g, recv)


def _sum_chips(own, recv, name):
    rh, c = own.shape
    tr = _tile(rh, 256, 16)

    def body(o_ref, r_ref, out_ref):
        acc = o_ref[...]
        for j in range(3):
            acc = acc + r_ref[j].astype(F32)
        out_ref[...] = acc

    return _pcall(body, name=name, grid=(rh // tr,),
                  in_specs=[_rows(tr, c), pl.BlockSpec((3, tr, c), lambda i: (0, i, 0))],
                  out_specs=_rows(tr, c), out_shape=SDS(own.shape, F32))(own, recv)


def _adamw_update(wv, gv, mv, vv):
    mn = ADAM_B1 * mv + (1.0 - ADAM_B1) * gv
    vn = ADAM_B2 * vv + (1.0 - ADAM_B2) * (gv * gv)
    m_hat = mn / (1.0 - ADAM_B1 ** ADAM_STEP)
    v_hat = vn / (1.0 - ADAM_B2 ** ADAM_STEP)
    return -ADAM_LR * (m_hat / (jnp.sqrt(v_hat) + ADAM_EPS) + ADAM_WD * wv), mn, vn


def _adamw(w, g, m, v, name):
    r, c = w.shape
    tr = _tile(r, 256, 8)

    def body(w_ref, g_ref, m_ref, v_ref, d_ref, mo_ref, vo_ref):
        d_ref[...], mo_ref[...], vo_ref[...] = _adamw_update(w_ref[...], g_ref[...], m_ref[...], v_ref[...])

    spec = _rows(tr, c)
    return _pcall(body, name=name, grid=(r // tr,), in_specs=[spec] * 4, out_specs=[spec] * 3,
                  out_shape=[SDS(w.shape, F32)] * 3)(w, g, m, v)


def _adamw_halves(w, mine, theirs, m, v, place, name):
    r, c = w.shape
    rh = r // 2
    tr = _tile(rh, 256, 8)
    nt = rh // tr

    def body(p_ref, w_ref, a_ref, b_ref, m_ref, v_ref, g_ref, d_ref, mo_ref, vo_ref):
        gv = jnp.where(pl.program_id(0) // nt == p_ref[0], a_ref[...], b_ref[...])
        g_ref[...] = gv
        d_ref[...], mo_ref[...], vo_ref[...] = _adamw_update(w_ref[...], gv, m_ref[...], v_ref[...])

    full = pl.BlockSpec((tr, c), lambda i, p: (i, 0))
    half = pl.BlockSpec((tr, c), lambda i, p: (i % nt, 0))
    grid_spec = pltpu.PrefetchScalarGridSpec(num_scalar_prefetch=1, grid=(2 * nt,),
                                             in_specs=[full, half, half, full, full], out_specs=[full] * 4)
    return pl.pallas_call(
        body, name=name, grid_spec=grid_spec, out_shape=[SDS(w.shape, F32)] * 4,
        compiler_params=pltpu.CompilerParams(dimension_semantics=("arbitrary",),
                                             vmem_limit_bytes=VMEM_LIMIT))(place, w, mine, theirs, m, v)


def _place():
    x, y, c = lax.axis_index("x"), lax.axis_index("y"), lax.axis_index("c")
    chips = [(1 - x, y), (x, 1 - y), (1 - x, 1 - y)]
    return x, y, c, chips


_ANY = pl.BlockSpec(memory_space=pl.ANY)


def _remote(src, dst, sems, k, to):
    return pltpu.make_async_remote_copy(src_ref=src, dst_ref=dst, send_sem=sems[0].at[k], recv_sem=sems[1].at[k],
                                        device_id=to, device_id_type=MESH)


class _Gather:
    def __init__(self, shards):
        n = len(shards)
        self.arrays = list(shards)
        self.out_shapes = [SDS((4,) + a.shape, a.dtype) for a in shards]
        self.scratch = [pltpu.SemaphoreType.DMA((7 * n,)), pltpu.SemaphoreType.DMA((7 * n,))]

    def _plan(self, ins, outs, sems):
        x, y, c, chips = _place()
        own, sib = 2 * x + y, (x, y, 1 - c)
        plan = []
        for wi, (w, o) in enumerate(zip(ins, outs)):
            rh = w.shape[0] // 2
            mine, theirs = pl.ds(c * rh, rh), pl.ds((1 - c) * rh, rh)
            whole = _remote(w, o.at[own], sems, 7 * wi + 6, sib)
            ici, d2d, d2d_in = [], [], []
            for j, (tx, ty) in enumerate(chips):
                t = 2 * tx + ty
                ici.append(_remote(w.at[mine], o.at[own, mine], sems, 7 * wi + j, (tx, ty, c)))
                d2d.append(_remote(o.at[t, mine], o.at[t, mine], sems, 7 * wi + 3 + j, sib))
                d2d_in.append(_remote(o.at[t, theirs], o.at[t, theirs], sems, 7 * wi + 3 + j, sib))
            plan.append((whole, ici, d2d, d2d_in))
        return plan

    def begin(self, ins, outs, sems):
        for whole, ici, _, _ in self._plan(ins, outs, sems):
            whole.start()
            for cp in ici:
                cp.start()

    def middle(self, ins, outs, sems):
        for _, ici, d2d, _ in self._plan(ins, outs, sems):
            for cp_in, cp_on in zip(ici, d2d):
                cp_in.wait_recv()
                cp_on.start()

    def finish(self, ins, outs, sems):
        for whole, ici, d2d, d2d_in in self._plan(ins, outs, sems):
            for cp in d2d_in:
                cp.wait_recv()
            for cp in ici + d2d:
                cp.wait_send()
            whole.wait()


class _Swap:
    def __init__(self, grads):
        n = len(grads)
        self.arrays = list(grads)
        self.out_shapes = [SDS((4,) + g.shape[2:], g.dtype) for g in grads]
        self.scratch = [pltpu.SemaphoreType.DMA((4 * n,)), pltpu.SemaphoreType.DMA((4 * n,))]

    def _plan(self, ins, outs, sems):
        x, y, c, _ = _place()
        return [_remote(g.at[t, 1 - c], o.at[t], sems, 4 * wi + t, (x, y, 1 - c))
                for wi, (g, o) in enumerate(zip(ins, outs)) for t in range(4)]

    def begin(self, ins, outs, sems):
        for cp in self._plan(ins, outs, sems):
            cp.start()

    def middle(self, ins, outs, sems):
        pass

    def finish(self, ins, outs, sems):
        for cp in self._plan(ins, outs, sems):
            cp.wait()


class _Exchange:
    def __init__(self, pieces):
        n = len(pieces)
        self.arrays = list(pieces)
        self.out_shapes = [SDS((3,) + p.shape[1:], p.dtype) for p in pieces]
        self.scratch = [pltpu.SemaphoreType.DMA((3 * n,)), pltpu.SemaphoreType.DMA((3 * n,))]

    def _plan(self, ins, outs, sems):
        x, y, c, chips = _place()
        return [_remote(g.at[2 * tx + ty], o.at[j], sems, 3 * wi + j, (tx, ty, c))
                for wi, (g, o) in enumerate(zip(ins, outs)) for j, (tx, ty) in enumerate(chips)]

    def begin(self, ins, outs, sems):
        for cp in self._plan(ins, outs, sems):
            cp.start()

    def middle(self, ins, outs, sems):
        pass

    def finish(self, ins, outs, sems):
        for cp in self._plan(ins, outs, sems):
            cp.wait()


class _Share:
    def __init__(self, totals):
        n = len(totals)
        self.arrays = list(totals)
        self.out_shapes = [SDS(t.shape, t.dtype) for t in totals]
        self.scratch = [pltpu.SemaphoreType.DMA((n,)), pltpu.SemaphoreType.DMA((n,))]

    def _plan(self, ins, outs, sems):
        x, y, c, _ = _place()
        return [_remote(t, o, sems, wi, (x, y, 1 - c)) for wi, (t, o) in enumerate(zip(ins, outs))]

    def begin(self, ins, outs, sems):
        for cp in self._plan(ins, outs, sems):
            cp.start()

    def middle(self, ins, outs, sems):
        pass

    def finish(self, ins, outs, sems):
        for cp in self._plan(ins, outs, sems):
            cp.wait()


def _ride_begin(rider, r_refs, step):
    @pl.when(step == 0)
    def _():
        rider.begin(*r_refs)


def _ride_end(rider, r_refs, step, nsteps):
    @pl.when(step == min(3 * nsteps // 4, nsteps - 1))
    def _():
        rider.middle(*r_refs)

    @pl.when(step == nsteps - 1)
    def _():
        rider.finish(*r_refs)


def _comm(rider, name):
    n_in, n_out = len(rider.arrays), len(rider.out_shapes)

    def body(*refs):
        r_refs = (refs[:n_in], refs[n_in:n_in + n_out], refs[n_in + n_out:])
        rider.begin(*r_refs)
        rider.middle(*r_refs)
        rider.finish(*r_refs)

    return pl.pallas_call(body, name=name, out_shape=rider.out_shapes, in_specs=[_ANY] * n_in,
                          out_specs=[_ANY] * n_out, scratch_shapes=rider.scratch)(*rider.arrays)


def _small_allreduce(pk, name):
    r = pk.shape[0]
    rels = [(dx, dy, dc) for dx in (0, 1) for dy in (0, 1) for dc in (0, 1) if dx or dy or dc]

    def body(p_ref, o_ref, buf, send_sems, recv_sems):
        x, y, c, _ = _place()
        me = 4 * x + 2 * y + c
        buf[me] = p_ref[...]
        cps = []
        for k, (dx, dy, dc) in enumerate(rels):
            to = (1 - x if dx else x, 1 - y if dy else y, 1 - c if dc else c)
            cps.append(pltpu.make_async_remote_copy(src_ref=p_ref, dst_ref=buf.at[me], send_sem=send_sems.at[k],
                                                    recv_sem=recv_sems.at[k], device_id=to,
                                                    device_id_type=MESH))
        for cpy in cps:
            cpy.start()
        for cpy in cps:
            cpy.wait()
        acc = buf[0]
        for d in range(1, 8):
            acc = acc + buf[d]
        o_ref[...] = acc

    vm = pl.BlockSpec(memory_space=pltpu.VMEM)
    return pl.pallas_call(body, name=name, out_shape=SDS(pk.shape, F32), in_specs=[vm], out_specs=vm,
                          scratch_shapes=[pltpu.VMEM((8, r, LANE), F32), pltpu.SemaphoreType.DMA((7,)),
                                          pltpu.SemaphoreType.DMA((7,))])(pk)


ATTN_W = ("w_in", "w_uq", "w_ukv", "w_out")
FFN_W = ("w_gate", "w_up", "w_down")
BIG = ATTN_W + FFN_W


def _cols_from_chips(g):
    return jnp.concatenate([g[t] for t in range(4)], axis=1)


def _cols_to_chips(full):
    r, n = full.shape
    return full.reshape(r, 4, n // 4).transpose(1, 0, 2).reshape(4, 2, r // 2, n // 4)


def _rows_to_chips(full):
    n, c = full.shape
    return full.reshape(4, 2, n // 8, c)


def _permute_w_in(w, nh):
    d = w.shape[0]
    g = 4 * nh * HEAD
    lr = (w.shape[1] - g - 2 * nh - ROPE) // 2
    o = g + 2 * nh
    pad = jnp.zeros((d, LANE - ROPE - 8 - nh), w.dtype)
    pad8 = jnp.zeros((d, 8 - nh), w.dtype)
    return jnp.concatenate([w[:, :g], w[:, o:o + 2 * lr], w[:, o + 2 * lr:], w[:, g:g + nh], pad8,
                            w[:, g + nh:g + 2 * nh], pad, jnp.zeros((d, LANE), w.dtype)], axis=1)


def _unpermute_w_in(wp, nh, lr):
    g = 4 * nh * HEAD
    mc = g + 2 * lr
    return jnp.concatenate([wp[:, :g], wp[:, mc + B_LANE:mc + B_LANE + nh], wp[:, mc + A_LANE:mc + A_LANE + nh],
                            wp[:, g:g + 2 * lr], wp[:, mc:mc + ROPE]], axis=1)


def _permute_w_uq(w, nh):
    lr = w.shape[0]
    w3 = w.reshape(lr, nh, HEAD + ROPE)
    return jnp.concatenate([w3, jnp.zeros((lr, nh, HEAD - ROPE), w.dtype)], axis=2).reshape(lr, nh * 2 * HEAD)


def _unpermute_w_uq(wp, nh):
    lr = wp.shape[0]
    return wp.reshape(lr, nh, 2 * HEAD)[:, :, :HEAD + ROPE].reshape(lr, nh * (HEAD + ROPE))


def _permute_w_ukv(w, nh):
    lr = w.shape[0]
    w3 = w.reshape(lr, nh, 2 * HEAD)
    kp = jnp.concatenate([w3[:, :, :HEAD], jnp.zeros((lr, nh, HEAD), w.dtype)], axis=2)
    return jnp.concatenate([kp.reshape(lr, nh * 2 * HEAD), w3[:, :, HEAD:].reshape(lr, nh * HEAD)], axis=1)


def _unpermute_w_ukv(wp, nh):
    lr = wp.shape[0]
    kp = wp[:, :nh * 2 * HEAD].reshape(lr, nh, 2 * HEAD)[:, :, :HEAD]
    vp = wp[:, nh * 2 * HEAD:].reshape(lr, nh, HEAD)
    return jnp.concatenate([kp, vp], axis=2).reshape(lr, nh * 2 * HEAD)


def _sum_pairs(grads, recv, place, tag):
    sums = [_sum_pair(g, r, place, "sum_pair_%s%d" % (tag, k)) for k, (g, r) in enumerate(zip(grads, recv))]
    return [s[0] for s in sums], [s[1] for s in sums]


def _reduce_end(own, recv, tag):
    return [_sum_chips(o, r, "sum_chips_%s%d" % (tag, k)) for k, (o, r) in enumerate(zip(own, recv))]


def _step(x, pos, tgt, w_in, attn_shards, ffn_shards, small, place):
    nh = small["a_log"].shape[1]
    lr = small["q_norm_w"].shape[1]
    w = nh * HEAD
    z_col, col_q, col_kv = 3, 4 * w // lr, 4 * w // lr + 1
    misc_c = 4 * w + 2 * lr
    misc_col = misc_c // LANE
    assert (4 * w) % lr == 0 and small["kv_norm_w"].shape[1] == lr

    zl = jnp.zeros((1, LANE), F32)
    alog_l = zl.at[:, A_LANE:A_LANE + nh].set(small["a_log"])
    dtb_l = zl.at[:, A_LANE:A_LANE + nh].set(small["dt_bias"])
    conv_w = small["conv_w"]

    h1, (in4,) = _norm_fwd(x, small["attn_norm_w"], "norm1", rider=_Gather([w_in]))
    win_p = _permute_w_in(_cols_from_chips(in4), nh)
    proj, (uq4, ukv4, out4) = _mm([(h1, win_p)], name="proj_in", rider=_Gather(attn_shards))
    wuq_p = _permute_w_uq(_cols_from_chips(uq4), nh)
    wukv_p = _permute_w_ukv(_cols_from_chips(ukv4), nh)
    w_out = out4.reshape(-1, out4.shape[2])
    gq, gk, gv, gb, gbt = _gdn_prep(proj, conv_w, alog_l, dtb_l, nh, misc_col)
    (o_gdn, states), (wg4,) = _gdn_fwd(gq, gk, gv, gb, gbt, nh, _Gather(ffn_shards[:1]))
    cqn, ckvn = _mla_norm(proj, small["q_norm_w"], small["kv_norm_w"], col_q, col_kv)
    qraw = _mm([(cqn, wuq_p)], name="proj_uq", out_dtype=BF16)
    kvraw = _mm([(ckvn, wukv_p)], name="proj_ukv", out_dtype=BF16)
    qc, kc, vv, qt, vt = _mla_rope(qraw, kvraw, proj, pos, nh, misc_col)
    o_mla, lse, (wu4, wd4) = _mla_fwd(qt, kc, vt, nh, _Gather(ffn_shards[1:]))
    w_down = wd4.reshape(-1, wd4.shape[2])
    mixed = _mix_fwd(o_gdn, proj, o_mla, small["gdn_norm_w"], small["mla_out_norm_w"], nh, z_col)
    x2 = _mm([(mixed, w_out)], name="proj_out", res=x)
    h2 = _norm_fwd(x2, small["ffn_norm_w"], "norm2")
    act, gpre, upre = _swiglu_fwd(h2, wg4, wu4)
    x3 = _mm([(act, w_down)], name="proj_down", res=x2, tk=2816)
    dx3, d_final, loss, dx3h = _final_loss(x3, tgt, small["final_norm_w"])

    gs = {"final_norm_w": d_final}
    dgate, dup = _swiglu_bwd(dx3h, w_down, gpre, upre)
    g_down = _rows_to_chips(_mm([(act, dx3h)], name="dw_down", ta=True, out_dtype=BF16))
    g_gate = _mm([(h2, dgate)], name="dw_gate", ta=True, out_dtype=BF16, out_chips=True)
    g_up = _mm([(h2, dup)], name="dw_up", ta=True, out_dtype=BF16, out_chips=True)
    halves = lambda g: g.reshape(4, 2, g.shape[1] // 2, g.shape[2])
    ffn_g = [halves(g_gate), halves(g_up), g_down]
    dh2, ffn_sib = _mm([(dgate, wg4), (dup, wu4)], name="dh2", tb=True, b_chips=True, out_dtype=BF16,
                       rider=_Swap(ffn_g))
    ffn16, ffn_own = _sum_pairs(ffn_g, ffn_sib, place, "ffn")
    dx2, gs["ffn_norm_w"], dx2h = _norm_bwd(dh2, x2, small["ffn_norm_w"], dx3, "norm2_bwd", True)
    dmix = _mm([(dx2h, w_out)], name="dmix", tb=True, out_dtype=BF16)
    g_out = _rows_to_chips(_mm([(mixed, dx2h)], name="dw_out", ta=True, out_dtype=BF16))
    d_ogdn, dproj, d_omla, gs["gdn_norm_w"], gs["mla_out_norm_w"], delta = _mix_bwd(
        dmix, o_gdn, proj, o_mla, small["gdn_norm_w"], small["mla_out_norm_w"], nh, z_col)
    dqc, dkc, dvv, ffn_recv = _mla_bwd(qc, kc, vv, d_omla, lse, delta, nh, _Exchange(ffn16))
    ffn_tot = _reduce_end(ffn_own, ffn_recv, "ffn")
    dqraw, dkvraw, dkr = _mla_rope_bwd(dqc, dkc, dvv, pos, nh)
    dcqn = _mm([(dqraw, wuq_p)], name="dcqn", tb=True)
    dckvn = _mm([(dkvraw, wukv_p)], name="dckvn", tb=True)
    g_uq = _cols_to_chips(_unpermute_w_uq(_mm([(cqn, dqraw)], name="dw_uq", ta=True, out_dtype=BF16), nh))
    g_ukv = _cols_to_chips(_unpermute_w_ukv(_mm([(ckvn, dkvraw)], name="dw_ukv", ta=True, out_dtype=BF16), nh))
    dproj, gs["q_norm_w"], gs["kv_norm_w"] = _mla_norm_bwd(
        proj, small["q_norm_w"], small["kv_norm_w"], dcqn, dckvn, dproj, col_q, col_kv)
    (dgq, dgk, dgv, dgb), ffn_shared = _gdn_bwd(gq, gk, gv, gb, gbt, states, d_ogdn, nh, _Share(ffn_tot))
    dconv, dproj, gs["conv_w"], dal, ddb = _gdn_prep_bwd(
        proj, conv_w, alog_l, dtb_l, dgq, dgk, dgv, dgb, dkr, dproj, nh, misc_col)
    gs["a_log"] = dal[:, A_LANE:A_LANE + nh]
    gs["dt_bias"] = ddb[:, A_LANE:A_LANE + nh]
    dproj = _conv_bwd_input(dconv, conv_w, dproj)
    g_in = _cols_to_chips(_unpermute_w_in(_mm([(h1, dproj)], name="dw_in", ta=True, out_dtype=BF16), nh, lr))
    att_g = [g_in, g_uq, g_ukv, g_out]
    att16, att_own = _sum_pairs(att_g, _comm(_Swap(att_g), "swap_att"), place, "att")
    dh1, att_recv = _mm([(dproj, win_p)], name="dh1", tb=True, out_dtype=BF16, rider=_Exchange(att16))
    att_tot = _reduce_end(att_own, att_recv, "att")
    att_shared = _comm(_Share(att_tot), "share_att")
    grad_x, gs["attn_norm_w"] = _norm_bwd(dh1, x, small["attn_norm_w"], dx2, "norm1_bwd", False)
    return loss, grad_x, att_tot + ffn_tot, list(att_shared) + list(ffn_shared), gs


SMALL = ("attn_norm_w", "ffn_norm_w", "final_norm_w", "q_norm_w", "kv_norm_w", "gdn_norm_w",
         "mla_out_norm_w", "a_log", "dt_bias")
WEIGHTS = ("attn_norm_w", "w_in", "conv_w", "a_log", "dt_bias", "gdn_norm_w", "q_norm_w", "w_uq",
           "kv_norm_w", "w_ukv", "mla_out_norm_w", "w_out", "ffn_norm_w", "w_gate", "w_up", "w_down",
           "final_norm_w")


def _pack_small(vecs):
    flat = jnp.concatenate([v.astype(F32).reshape(-1) for v in vecs])
    pad = (-flat.shape[0]) % (8 * LANE)
    return jnp.concatenate([flat, jnp.zeros((pad,), F32)]).reshape(-1, LANE)


def kernel(x, positions, attn_norm_w, w_in, conv_w, a_log, dt_bias, gdn_norm_w, q_norm_w, w_uq, kv_norm_w, w_ukv, mla_out_norm_w, w_out, ffn_norm_w, w_gate, w_up, w_down, final_norm_w, loss_target, m_attn_norm_w, m_w_in, m_conv_w, m_a_log, m_dt_bias, m_gdn_norm_w, m_q_norm_w, m_w_uq, m_kv_norm_w, m_w_ukv, m_mla_out_norm_w, m_w_out, m_ffn_norm_w, m_w_gate, m_w_up, m_w_down, m_final_norm_w, v_attn_norm_w, v_w_in, v_conv_w, v_a_log, v_dt_bias, v_gdn_norm_w, v_q_norm_w, v_w_uq, v_kv_norm_w, v_w_ukv, v_mla_out_norm_w, v_w_out, v_ffn_norm_w, v_w_gate, v_w_up, v_w_down, v_final_norm_w):
    args = dict(locals())
    xi, yi, ci = lax.axis_index("x"), lax.axis_index("y"), lax.axis_index("c")
    chip = 2 * xi + yi

    def two_d(a):
        return a.reshape(a.shape[-2:]) if a.ndim >= 2 else a.reshape(1, -1)

    wloc = {n: two_d(args[n]) for n in WEIGHTS}
    mloc = {n: two_d(args["m_" + n]) for n in WEIGHTS}
    vloc = {n: two_d(args["v_" + n]) for n in WEIGHTS}

    cw = wloc["conv_w"]
    cshard = cw.shape[1]
    cfull = jnp.zeros((CONV, 4 * cshard), F32)
    cfull = lax.dynamic_update_slice(cfull, jnp.where(ci == 0, cw, 0.0), (0, chip * cshard))
    conv_full = _small_allreduce(_pack_small([cfull]), "gather_conv_w").reshape(-1)[:CONV * 4 * cshard]
    conv_full = conv_full.reshape(CONV, 4 * cshard)

    small = {n: wloc[n] for n in SMALL}
    small["conv_w"] = conv_full

    pos = positions.reshape(-1, 1).astype(F32)
    place = jnp.stack([ci, chip]).astype(jnp.int32)
    loss, grad_x, totals, from_sib, gs = _step(
        two_d(x), pos, two_d(loss_target), wloc["w_in"].astype(BF16), [wloc[n].astype(BF16) for n in ATTN_W[1:]],
        [wloc[n].astype(BF16) for n in FFN_W], small, place)

    small_names = SMALL + ("conv_w",)
    pk = _pack_small([gs[n] for n in small_names] + [loss])
    red = _small_allreduce(pk, "reduce_small").reshape(-1)
    gsm, off = {}, 0
    for n in small_names:
        shp = gs[n].shape
        gsm[n] = red[off:off + shp[0] * shp[1]].reshape(shp)
        off += shp[0] * shp[1]
    loss_out = red[off]
    gsm["conv_w"] = lax.dynamic_slice(gsm["conv_w"], (0, chip * cshard), (CONV, cshard))

    grads, deltas, new_m, new_v = {}, {}, {}, {}
    for n, mine, theirs in zip(BIG, totals, from_sib):
        grads[n], deltas[n], new_m[n], new_v[n] = _adamw_halves(wloc[n], mine, theirs, mloc[n], vloc[n], place,
                                                                "adamw_" + n)
    grads["conv_w"] = gsm["conv_w"]
    deltas["conv_w"], new_m["conv_w"], new_v["conv_w"] = _adamw(wloc["conv_w"], gsm["conv_w"], mloc["conv_w"],
                                                                vloc["conv_w"], "adamw_conv_w")
    sm_shapes = [wloc[n].shape for n in SMALL]
    pd, pm, pv = _adamw(_pack_small([wloc[n] for n in SMALL]), _pack_small([gsm[n] for n in SMALL]),
                        _pack_small([mloc[n] for n in SMALL]), _pack_small([vloc[n] for n in SMALL]),
                        "adamw_small")
    for dst, packed in ((deltas, pd), (new_m, pm), (new_v, pv)):
        flat, off = packed.reshape(-1), 0
        for n, shp in zip(SMALL, sm_shapes):
            dst[n] = flat[off:off + shp[0] * shp[1]].reshape(shp)
            off += shp[0] * shp[1]
    for n in SMALL:
        grads[n] = gsm[n]

    def like(n, a):
        return a.reshape(args[n].shape)

    outs = [loss_out.reshape(()), grad_x.reshape(x.shape)]
    for group in (grads, deltas, new_m, new_v):
        outs += [like(n, group[n]) for n in WEIGHTS]
    return tuple(outs)
```

```python
import jax
import jax.numpy as jnp
from jax import lax
from jax.experimental import pallas as pl
from jax.experimental.pallas import tpu as pltpu

F32, BF16 = jnp.float32, jnp.bfloat16
SDS = jax.ShapeDtypeStruct
MESH = pl.DeviceIdType.MESH

HEAD = 128
ROPE = 64
CHUNK = 64
PAIR = 2 * CHUNK
CONV = 4
EPS = 1e-6
ROPE_THETA = 10000.0
LANE = 128
B_LANE = 64
A_LANE = 72
VMEM_LIMIT = 48 * 1024 * 1024
VMEM_LIMIT_WIDE = 56 * 1024 * 1024
MLA_BLOCK = 512
LOG2E = 1.4426950408889634
LN2 = 0.6931471805599453
SM_SCALE = (HEAD + ROPE) ** -0.5

ADAM_LR = 0.001
ADAM_B1 = 0.9
ADAM_B2 = 0.999
ADAM_EPS = 1e-08
ADAM_WD = 0.01
ADAM_STEP = 10


def _tile(n, pref, mult=LANE):
    if n <= pref:
        return n
    t = (pref // mult) * mult
    while t >= mult:
        if n % t == 0:
            return t
        t -= mult
    return n


def _pcall(body, *, name, grid, in_specs, out_specs, out_shape, scratch=(), vmem=VMEM_LIMIT, aliases=None):
    return pl.pallas_call(
        body, name=name, grid=grid, in_specs=in_specs, out_specs=out_specs,
        out_shape=out_shape, scratch_shapes=list(scratch), input_output_aliases=aliases or {},
        compiler_params=pltpu.CompilerParams(
            dimension_semantics=("arbitrary",) * len(grid), vmem_limit_bytes=vmem))


def _pcall_riding(core, rider, *, name, grid, in_specs, out_specs, out_shape, args, scratch=()):
    n_in, n_out, n_scr = len(in_specs), len(out_specs), len(scratch)
    r_in, r_out = len(rider.arrays), len(rider.out_shapes)

    def body(*refs):
        ins, refs = refs[:n_in], refs[n_in:]
        r_ins, refs = refs[:r_in], refs[r_in:]
        outs, refs = refs[:n_out], refs[n_out:]
        r_outs, refs = refs[:r_out], refs[r_out:]
        scr, sems = refs[:n_scr], refs[n_scr:]
        r_refs = (r_ins, r_outs, sems)
        _ride_begin(rider, r_refs, pl.program_id(0))
        core(*ins, *outs, *scr)
        _ride_end(rider, r_refs, pl.program_id(0), grid[0])

    res = _pcall(body, name=name, grid=grid, in_specs=list(in_specs) + [_ANY] * r_in,
                 out_specs=list(out_specs) + [_ANY] * r_out, out_shape=list(out_shape) + rider.out_shapes,
                 scratch=list(scratch) + rider.scratch)(*args, *rider.arrays)
    return res[:n_out], res[n_out:]


def _rows(ts, width, col=0):
    return pl.BlockSpec((ts, width), lambda i: (i, col))


def _full(shape):
    nd = len(shape)
    return pl.BlockSpec(shape, lambda i: (0,) * nd)


def _dot(a, b):
    return jnp.dot(a.astype(BF16), b.astype(BF16), preferred_element_type=F32)


def _dot_nt(a, b):
    return lax.dot_general(a.astype(BF16), b.astype(BF16), (((1,), (1,)), ((), ())),
                           preferred_element_type=F32)


def _dot_tn(a, b):
    return lax.dot_general(a.astype(BF16), b.astype(BF16), (((0,), (0,)), ((), ())),
                           preferred_element_type=F32)


def _sigmoid(x):
    return 1.0 / (1.0 + jnp.exp(-x))


def _silu(x):
    return x * _sigmoid(x)


def _dsilu(x):
    s = _sigmoid(x)
    return s * (1.0 + x * (1.0 - s))


def _lane_iota(shape):
    return lax.broadcasted_iota(jnp.int32, shape, len(shape) - 1)


def _col(block, idx):
    return jnp.sum(jnp.where(_lane_iota(block.shape) == idx, block, 0.0), axis=-1, keepdims=True)


def _mm(pairs, *, name, ta=False, tb=False, out_dtype=F32, res=None, tm=1024, tn=1024, tk=2048,
        b_chips=False, out_chips=False, rider=None):
    a0, b0 = pairs[0]
    if ta:
        kdim, m = a0.shape
    else:
        m, kdim = a0.shape
    if b_chips and tb:
        n, tk = b0.shape[1], b0.shape[2]
        assert kdim == 4 * tk
    elif b_chips:
        n, tn = 4 * b0.shape[2], b0.shape[2]
        assert kdim == b0.shape[1]
    else:
        n = b0.shape[0] if tb else b0.shape[1]
    if out_chips:
        tn = n // 4
    tm = _tile(m, tm)
    tn = tn if (out_chips or (b_chips and not tb)) else _tile(n, tn)
    tk = tk if (b_chips and tb) else _tile(kdim, tk)
    assert m % tm == 0 and n % tn == 0 and kdim % tk == 0
    nk, npair = kdim // tk, len(pairs)
    grid = (m // tm, n // tn, nk)
    dims = (((0 if ta else 1,), (1 if tb else 0,)), ((), ()))
    n_in = 2 * npair + (res is not None)
    r_in, r_out = (len(rider.arrays), len(rider.out_shapes)) if rider else (0, 0)

    def body(*refs):
        o_ref = refs[n_in + r_in]
        acc = refs[n_in + r_in + 1 + r_out]
        k = pl.program_id(2)
        if rider:
            r_refs = (refs[n_in:n_in + r_in], refs[n_in + r_in + 1:n_in + r_in + 1 + r_out],
                      refs[n_in + r_in + 2 + r_out:])
            step = (pl.program_id(0) * grid[1] + pl.program_id(1)) * nk + k
            _ride_begin(rider, r_refs, step)

        @pl.when(k == 0)
        def _():
            acc[...] = jnp.zeros_like(acc)

        tot = None
        for p in range(npair):
            d = lax.dot_general(refs[2 * p][...].astype(BF16), refs[2 * p + 1][...].astype(BF16),
                                dims, preferred_element_type=F32)
            tot = d if tot is None else tot + d
        acc[...] += tot

        @pl.when(k == nk - 1)
        def _():
            r = acc[...]
            if res is not None:
                r = r + refs[2 * npair][...]
            o_ref[...] = r.astype(out_dtype)

        if rider:
            _ride_end(rider, r_refs, step, grid[0] * grid[1] * nk)

    if ta:
        a_spec = pl.BlockSpec((tk, tm), lambda i, j, k: (k, i))
    else:
        a_spec = pl.BlockSpec((tm, tk), lambda i, j, k: (i, k))
    if b_chips and tb:
        b_spec = pl.BlockSpec((None, tn, tk), lambda i, j, k: (k, j, 0))
    elif b_chips:
        b_spec = pl.BlockSpec((None, tk, tn), lambda i, j, k: (j, k, 0))
    elif tb:
        b_spec = pl.BlockSpec((tn, tk), lambda i, j, k: (j, k))
    else:
        b_spec = pl.BlockSpec((tk, tn), lambda i, j, k: (k, j))
    if out_chips:
        o_spec = pl.BlockSpec((None, tm, tn), lambda i, j, k: (j, i, 0))
        o_shape = SDS((4, m, tn), out_dtype)
    else:
        o_spec = pl.BlockSpec((tm, tn), lambda i, j, k: (i, j))
        o_shape = SDS((m, n), out_dtype)
    in_specs, args = [], []
    for a, b in pairs:
        in_specs += [a_spec, b_spec]
        args += [a, b]
    if res is not None:
        in_specs.append(o_spec)
        args.append(res)
    out_specs, out_shapes, scratch = [o_spec], [o_shape], [pltpu.VMEM((tm, tn), F32)]
    if rider:
        in_specs += [_ANY] * r_in
        args += rider.arrays
        out_specs += [_ANY] * r_out
        out_shapes += rider.out_shapes
        scratch += rider.scratch
    outs = _pcall(body, name=name, grid=grid, in_specs=in_specs, out_specs=out_specs, out_shape=out_shapes,
                  scratch=scratch)(*args)
    return (outs[0], outs[1:]) if rider else outs[0]


def _norm_fwd(x, w, name, rider=None):
    s, d = x.shape
    ts = _tile(s, 512, 8)

    def body(x_ref, w_ref, h_ref):
        xv = x_ref[...]
        r = lax.rsqrt(jnp.mean(xv * xv, axis=-1, keepdims=True) + EPS)
        h_ref[...] = (xv * r * w_ref[...]).astype(BF16)

    spec = dict(name=name, grid=(s // ts,), in_specs=[_rows(ts, d), _full((1, d))])
    if rider is None:
        return _pcall(body, out_specs=_rows(ts, d), out_shape=SDS((s, d), BF16), **spec)(x, w)
    outs, r_outs = _pcall_riding(body, rider, out_specs=[_rows(ts, d)], out_shape=[SDS((s, d), BF16)],
                                 args=(x, w), **spec)
    return outs[0], r_outs


def _norm_bwd(dh, x, w, dres, name, with_bf16):
    s, d = x.shape
    ts = _tile(s, 256, 8)

    def body(dh_ref, x_ref, w_ref, dres_ref, dx_ref, dw_ref, *dx16_ref):
        @pl.when(pl.program_id(0) == 0)
        def _():
            dw_ref[...] = jnp.zeros_like(dw_ref)

        xv, dhv = x_ref[...], dh_ref[...]
        r = lax.rsqrt(jnp.mean(xv * xv, axis=-1, keepdims=True) + EPS)
        xh = xv * r
        dw_ref[...] += jnp.sum(dhv * xh, axis=0, keepdims=True)
        dxh = dhv * w_ref[...]
        dx = dres_ref[...] + r * (dxh - xh * jnp.mean(dxh * xh, axis=-1, keepdims=True))
        dx_ref[...] = dx
        for ref in dx16_ref:
            ref[...] = dx.astype(BF16)

    extra = 1 if with_bf16 else 0
    return _pcall(body, name=name, grid=(s // ts,),
                  in_specs=[_rows(ts, d), _rows(ts, d), _full((1, d)), _rows(ts, d)],
                  out_specs=[_rows(ts, d), _full((1, d))] + [_rows(ts, d)] * extra,
                  out_shape=[SDS((s, d), F32), SDS((1, d), F32)] + [SDS((s, d), BF16)] * extra)(
                      dh, x, w, dres)


def _final_loss(x3, tgt, w):
    s, d = x3.shape
    ts = _tile(s, 256, 8)

    def body(x_ref, t_ref, w_ref, dx_ref, dw_ref, loss_ref, dx16_ref):
        @pl.when(pl.program_id(0) == 0)
        def _():
            dw_ref[...] = jnp.zeros_like(dw_ref)
            loss_ref[...] = jnp.zeros_like(loss_ref)

        xv, wv = x_ref[...], w_ref[...]
        r = lax.rsqrt(jnp.mean(xv * xv, axis=-1, keepdims=True) + EPS)
        xh = xv * r
        err = xh * wv - t_ref[...]
        row = jnp.mean(err * err, axis=-1, keepdims=True)
        loss_ref[...] += 0.5 * jnp.sum(row, axis=0, keepdims=True)
        dy = err * (1.0 / d)
        dw_ref[...] += jnp.sum(dy * xh, axis=0, keepdims=True)
        dxh = dy * wv
        dx = r * (dxh - xh * jnp.mean(dxh * xh, axis=-1, keepdims=True))
        dx_ref[...] = dx
        dx16_ref[...] = dx.astype(BF16)

    return _pcall(body, name="final_loss", grid=(s // ts,),
                  in_specs=[_rows(ts, d), _rows(ts, d), _full((1, d))],
                  out_specs=[_rows(ts, d), _full((1, d)), _full((1, 1)), _rows(ts, d)],
                  out_shape=[SDS((s, d), F32), SDS((1, d), F32), SDS((1, 1), F32), SDS((s, d), BF16)])(
                      x3, tgt, w)


def _shift_down(cur, halo, s):
    if s == 0:
        return cur
    row8 = lax.broadcasted_iota(jnp.int32, halo.shape, 0)
    r = pltpu.roll(cur, s, 0)
    top = jnp.where(row8 < s, pltpu.roll(halo, s, 0), r[0:8])
    return jnp.concatenate([top, r[8:]], axis=0)


def _shift_up(cur, halo, s):
    if s == 0:
        return cur
    ts = cur.shape[0]
    row8 = lax.broadcasted_iota(jnp.int32, halo.shape, 0)
    r = pltpu.roll(cur, ts - s, 0)
    bot = jnp.where(row8 >= 8 - s, pltpu.roll(halo, 8 - s, 0), r[ts - 8:ts])
    return jnp.concatenate([r[:ts - 8], bot], axis=0)


def _chunk_tri(ts, upper):
    i = lax.broadcasted_iota(jnp.int32, (ts, ts), 0)
    j = lax.broadcasted_iota(jnp.int32, (ts, ts), 1)
    same = jnp.right_shift(i, 6) == jnp.right_shift(j, 6)
    return jnp.where(same & ((j >= i) if upper else (j <= i)), 1.0, 0.0).astype(F32)


def _gate_values(m, alog, dtb):
    lane = _lane_iota(m.shape)
    beta = _sigmoid(m)
    xg = m + dtb
    sp = jnp.maximum(xg, 0.0) + jnp.log(1.0 + jnp.exp(-jnp.abs(xg)))
    ga = (lane >= A_LANE) & (lane < A_LANE + 8)
    g = jnp.where(ga, -jnp.exp(alog) * sp, 0.0)
    return beta, g, xg, ga


def _l2_heads(a, nh, scale):
    outs, rs = [], []
    for h in range(nh):
        ah = a[:, HEAD * h:HEAD * (h + 1)]
        r = lax.rsqrt(jnp.sum(ah * ah, axis=-1, keepdims=True) + EPS)
        outs.append(ah * (r * scale))
        rs.append(r)
    return jnp.concatenate(outs, axis=-1), rs


def _gdn_prep(proj, conv_w, alog_l, dtb_l, nh, misc_col):
    s = proj.shape[0]
    w = nh * HEAD
    ts = _tile(s, 256, PAIR)
    hb = ts // 8

    def body(cur_ref, halo_ref, misc_ref, cw_ref, al_ref, db_ref, q_ref, k_ref, v_ref, gb_ref, gbt_ref):
        first = pl.program_id(0) == 0
        outs = (q_ref, k_ref, v_ref)
        for sec in range(3):
            cs = slice(sec * w, (sec + 1) * w)
            cur = cur_ref[:, cs]
            halo = jnp.where(first, 0.0, halo_ref[:, cs])
            pre = None
            for j in range(CONV):
                term = cw_ref[j:j + 1, cs] * _shift_down(cur, halo, CONV - 1 - j)
                pre = term if pre is None else pre + term
            act = _silu(pre)
            if sec == 0:
                act, _ = _l2_heads(act, nh, HEAD ** -0.5)
            elif sec == 1:
                act, _ = _l2_heads(act, nh, 1.0)
            outs[sec][...] = act
        m = misc_ref[...]
        lane = _lane_iota(m.shape)
        beta, g, _, ga = _gate_values(m, al_ref[...], db_ref[...])
        gcc = jnp.dot(_chunk_tri(ts, False), g, precision=lax.Precision.HIGHEST,
                      preferred_element_type=F32)
        gb = jnp.where((lane >= B_LANE) & (lane < B_LANE + 8), beta, jnp.where(ga, gcc, 0.0))
        gb_ref[...] = gb
        gbt_ref[...] = gb.T

    return _pcall(
        body, name="gdn_prep", grid=(s // ts,),
        in_specs=[_rows(ts, 3 * w),
                  pl.BlockSpec((8, 3 * w), lambda i: (jnp.maximum(i * hb - 1, 0), 0)),
                  _rows(ts, LANE, misc_col), _full((CONV, 3 * w)), _full((1, LANE)), _full((1, LANE))],
        out_specs=[_rows(ts, w), _rows(ts, w), _rows(ts, w), _rows(ts, LANE),
                   pl.BlockSpec((LANE, ts), lambda i: (0, i))],
        out_shape=[SDS((s, w), F32), SDS((s, w), F32), SDS((s, w), F32), SDS((s, LANE), F32),
                   SDS((LANE, s), F32)])(proj, proj, proj, conv_w, alog_l, dtb_l)


def _gdn_prep_bwd(proj, conv_w, alog_l, dtb_l, dq, dk, dv, dgb, dkr, dproj, nh, misc_col):
    s = proj.shape[0]
    w = nh * HEAD
    ts = _tile(s, 256, PAIR)
    hb = ts // 8
    assert misc_col % 2 == 0

    def body(cur_ref, halo_ref, misc_ref, cw_ref, al_ref, db_ref, dq_ref, dk_ref, dv_ref, dgb_ref,
             dkr_ref, _, dc_ref, dm_ref, dcw_ref, dal_ref, ddb_ref):
        first = pl.program_id(0) == 0

        @pl.when(first)
        def _():
            dcw_ref[...] = jnp.zeros_like(dcw_ref)
            dal_ref[...] = jnp.zeros_like(dal_ref)
            ddb_ref[...] = jnp.zeros_like(ddb_ref)

        dins = (dq_ref, dk_ref, dv_ref)
        for sec in range(3):
            cs = slice(sec * w, (sec + 1) * w)
            cur = cur_ref[:, cs]
            halo = jnp.where(first, 0.0, halo_ref[:, cs])
            us = [_shift_down(cur, halo, CONV - 1 - j) for j in range(CONV)]
            pre = None
            for j in range(CONV):
                term = cw_ref[j:j + 1, cs] * us[j]
                pre = term if pre is None else pre + term
            act = _silu(pre)
            dout = dins[sec][...]
            if sec < 2:
                scale = HEAD ** -0.5 if sec == 0 else 1.0
                parts = []
                for h in range(nh):
                    hs = slice(HEAD * h, HEAD * (h + 1))
                    ah = act[:, hs]
                    r = lax.rsqrt(jnp.sum(ah * ah, axis=-1, keepdims=True) + EPS)
                    ahat = ah * r
                    dy = dout[:, hs]
                    parts.append((scale * r) * (dy - ahat * jnp.sum(dy * ahat, axis=-1, keepdims=True)))
                dact = jnp.concatenate(parts, axis=-1)
            else:
                dact = dout
            dconv = dact * _dsilu(pre)
            dc_ref[:, cs] = dconv
            for j in range(CONV):
                dcw_ref[j:j + 1, cs] += jnp.sum(dconv * us[j], axis=0, keepdims=True)
        m = misc_ref[...]
        lane = _lane_iota(m.shape)
        al = al_ref[...]
        beta, g, xg, ga = _gate_values(m, al, db_ref[...])
        dgbv = dgb_ref[...]
        dg = jnp.dot(_chunk_tri(ts, True), jnp.where(ga, dgbv, 0.0), precision=lax.Precision.HIGHEST,
                     preferred_element_type=F32)
        da_raw = jnp.where(ga, dg * (-jnp.exp(al)) * _sigmoid(xg), 0.0)
        db_raw = jnp.where((lane >= B_LANE) & (lane < B_LANE + 8), dgbv * beta * (1.0 - beta), 0.0)
        dal_ref[...] += jnp.sum(dg * g, axis=0, keepdims=True)
        ddb_ref[...] += jnp.sum(da_raw, axis=0, keepdims=True)
        dm_ref[:, :LANE] = (dkr_ref[...] + da_raw + db_raw).astype(BF16)
        dm_ref[:, LANE:] = jnp.zeros((ts, LANE), BF16)

    return _pcall(
        body, name="gdn_prep_bwd", grid=(s // ts,),
        in_specs=[_rows(ts, 3 * w),
                  pl.BlockSpec((8, 3 * w), lambda i: (jnp.maximum(i * hb - 1, 0), 0)),
                  _rows(ts, LANE, misc_col), _full((CONV, 3 * w)), _full((1, LANE)), _full((1, LANE)),
                  _rows(ts, w), _rows(ts, w), _rows(ts, w), _rows(ts, LANE), _rows(ts, LANE), _ANY],
        out_specs=[_rows(ts, 3 * w), _rows(ts, 2 * LANE, misc_col // 2), _full((CONV, 3 * w)), _full((1, LANE)),
                   _full((1, LANE))],
        out_shape=[SDS((s, 3 * w), F32), SDS(dproj.shape, BF16), SDS((CONV, 3 * w), F32),
                   SDS((1, LANE), F32), SDS((1, LANE), F32)], aliases={11: 1})(
                       proj, proj, proj, conv_w, alog_l, dtb_l, dq, dk, dv, dgb, dkr, dproj)


def _conv_bwd_input(dconv, conv_w, dproj):
    s, c = dconv.shape
    ts = _tile(s, 256, 8)
    hb = ts // 8
    nblk8 = s // 8
    nt = s // ts

    def body(cur_ref, nxt_ref, cw_ref, _, o_ref):
        last = pl.program_id(0) == nt - 1
        cur = cur_ref[...]
        halo = jnp.where(last, 0.0, nxt_ref[...])
        acc = None
        for j in range(CONV):
            term = cw_ref[j:j + 1, :] * _shift_up(cur, halo, CONV - 1 - j)
            acc = term if acc is None else acc + term
        o_ref[...] = acc.astype(BF16)

    return _pcall(
        body, name="conv_bwd_input", grid=(nt,),
        in_specs=[_rows(ts, c),
                  pl.BlockSpec((8, c), lambda i: (jnp.minimum((i + 1) * hb, nblk8 - 1), 0)),
                  _full((CONV, c)), _ANY],
        out_specs=_rows(ts, c), out_shape=SDS(dproj.shape, BF16), aliases={3: 0})(
            dconv, dconv, conv_w, dproj)


def _inv_unit_lower(a):
    n = a[0].shape[0]
    i = lax.broadcasted_iota(jnp.int32, (n, n), 0)
    j = lax.broadcasted_iota(jnp.int32, (n, n), 1)
    eye = jnp.where(i == j, 1.0, 0.0)
    t = [eye - ah for ah in a]
    x = a
    for _ in range(5):
        x = [_dot(xh, xh) for xh in x]
        t = [th + _dot(th, xh) for th, xh in zip(t, x)]
    return t


def _pair_common(q, k, gcol, grow, bcol):
    i = lax.broadcasted_iota(jnp.int32, (PAIR, PAIR), 0)
    j = lax.broadcasted_iota(jnp.int32, (PAIR, PAIR), 1)
    same = jnp.right_shift(i, 6) == jnp.right_shift(j, 6)
    tril = same & (i >= j)
    strict = same & (i > j)
    dec = [jnp.where(tril, jnp.exp(jnp.minimum(gc - gr, 0.0)), 0.0) for gc, gr in zip(gcol, grow)]
    kk = [_dot_nt(kh, kh) for kh in k]
    qk = [_dot_nt(qh, kh) for qh, kh in zip(q, k)]
    a = [jnp.where(strict, b * kkh * d, 0.0) for b, kkh, d in zip(bcol, kk, dec)]
    t = _inv_unit_lower(a)
    p = [qkh * d for qkh, d in zip(qk, dec)]
    return dec, kk, a, t, p, tril, strict


def _ext(v, a):
    z = jnp.zeros_like(v)
    return jnp.concatenate([v, z] if a == 0 else [z, v], axis=0)


def _gdn_fwd(q, k, v, gb, gbt, nh, rider):
    s = q.shape[0]
    w = nh * HEAD
    npair = s // PAIR

    def body(q_ref, k_ref, v_ref, gb_ref, gbt_ref, o_ref, st_ref, s_ref):
        @pl.when(pl.program_id(0) == 0)
        def _():
            s_ref[...] = jnp.zeros_like(s_ref)

        heads = range(nh)
        hs = [slice(HEAD * h, HEAD * (h + 1)) for h in heads]
        gbv = gb_ref[...]
        q, k, v = [q_ref[:, s_] for s_ in hs], [k_ref[:, s_] for s_ in hs], [v_ref[:, s_] for s_ in hs]
        gcol = [_col(gbv, A_LANE + h) for h in heads]
        bcol = [_col(gbv, B_LANE + h) for h in heads]
        grow = [gbt_ref[A_LANE + h:A_LANE + h + 1, :] for h in heads]
        _, _, _, t, p, _, _ = _pair_common(q, k, gcol, grow, bcol)
        eg = [jnp.exp(gc) for gc in gcol]
        qg = [x * e for x, e in zip(q, eg)]
        kg = [x * e for x, e in zip(k, eg)]
        outs = []
        for a in range(2):
            sl = slice(CHUNK * a, CHUNK * (a + 1))
            st = [s_ref[h] for h in heads]
            for h in heads:
                st_ref[a, h] = st[h]
            r = [v[h][sl] - _dot(kg[h][sl], st[h]) for h in heads]
            vn = [_dot(t[h][sl], _ext(bcol[h][sl] * r[h], a)) for h in heads]
            outs.append([_dot(qg[h][sl], st[h]) + _dot(p[h][sl], _ext(vn[h], a)) for h in heads])
            gl = [_col(gr, CHUNK * (a + 1) - 1) for gr in grow]
            kd = [k[h][sl] * jnp.exp(gl[h] - gcol[h][sl]) for h in heads]
            upd = [_dot_tn(kd[h], vn[h]) for h in heads]
            for h in heads:
                s_ref[h] = jnp.exp(gl[h]) * st[h] + upd[h]
        for h in heads:
            o_ref[:, hs[h]] = jnp.concatenate([outs[0][h], outs[1][h]], axis=0)

    return _pcall_riding(
        body, rider, name="gdn_fwd", grid=(npair,),
        in_specs=[_rows(PAIR, w), _rows(PAIR, w), _rows(PAIR, w), _rows(PAIR, LANE),
                  pl.BlockSpec((LANE, PAIR), lambda i: (0, i))],
        out_specs=[_rows(PAIR, w), pl.BlockSpec((2, nh, HEAD, HEAD), lambda i: (i, 0, 0, 0))],
        out_shape=[SDS((s, w), F32), SDS((2 * npair, nh, HEAD, HEAD), F32)],
        scratch=[pltpu.VMEM((nh, HEAD, HEAD), F32)], args=(q, k, v, gb, gbt))


def _gdn_bwd(q, k, v, gb, gbt, states, do, nh, rider):
    s = q.shape[0]
    w = nh * HEAD
    npair = s // PAIR
    rev = lambda i: (npair - 1 - i, 0)

    def body(q_ref, k_ref, v_ref, gb_ref, gbt_ref, st_ref, do_ref, dq_ref, dk_ref, dv_ref, dgb_ref,
             ds_ref):
        @pl.when(pl.program_id(0) == 0)
        def _():
            ds_ref[...] = jnp.zeros_like(ds_ref)

        lane = _lane_iota((PAIR, LANE))
        row = lax.broadcasted_iota(jnp.int32, (CHUNK, 1), 0)
        heads = range(nh)
        hs = [slice(HEAD * h, HEAD * (h + 1)) for h in heads]
        gbv = gb_ref[...]
        q, k, v = [q_ref[:, s_] for s_ in hs], [k_ref[:, s_] for s_ in hs], [v_ref[:, s_] for s_ in hs]
        do = [do_ref[:, s_] for s_ in hs]
        gcol = [_col(gbv, A_LANE + h) for h in heads]
        bcol = [_col(gbv, B_LANE + h) for h in heads]
        grow = [gbt_ref[A_LANE + h:A_LANE + h + 1, :] for h in heads]
        dec, kk, amat, t, p, tril, strict = _pair_common(q, k, gcol, grow, bcol)
        tt, pt = [x.T for x in t], [x.T for x in p]
        eg = [jnp.exp(gc) for gc in gcol]
        qg = [x * e for x, e in zip(q, eg)]
        kg = [x * e for x, e in zip(k, eg)]
        sums = lambda x: jnp.sum(x, axis=-1, keepdims=True)
        rs, vns = [None, None], [None, None]
        for a in range(2):
            sl = slice(CHUNK * a, CHUNK * (a + 1))
            rs[a] = [v[h][sl] - _dot(kg[h][sl], st_ref[a, h]) for h in heads]
            vns[a] = [_dot(t[h][sl], _ext(bcol[h][sl] * rs[a][h], a)) for h in heads]
        dsn = [ds_ref[h] for h in heads]
        dqs, dks, dvs, dgcs, dbs, drbs = ([None, None] for _ in range(6))
        for a in (1, 0):
            sl = slice(CHUNK * a, CHUNK * (a + 1))
            st = [st_ref[a, h] for h in heads]
            gl = [_col(gr, CHUNK * (a + 1) - 1) for gr in grow]
            egl = [jnp.exp(x) for x in gl]
            dk_dec = [jnp.exp(gl[h] - gcol[h][sl]) for h in heads]
            kd = [k[h][sl] * dk_dec[h] for h in heads]
            d_vn = [_dot(pt[h][sl], _ext(do[h][sl], a)) + _dot(kd[h], dsn[h]) for h in heads]
            d_qg = [_dot_nt(do[h][sl], st[h]) for h in heads]
            d_rb = [_dot(tt[h][sl], _ext(d_vn[h], a)) for h in heads]
            d_r = [bcol[h][sl] * d_rb[h] for h in heads]
            d_kg = [-_dot_nt(d_r[h], st[h]) for h in heads]
            d_kd = [_dot_nt(vns[a][h], dsn[h]) for h in heads]
            dsn_new = [_dot_tn(qg[h][sl], do[h][sl]) - _dot_tn(kg[h][sl], d_r[h]) for h in heads]
            dbs[a] = [sums(d_rb[h] * rs[a][h]) for h in heads]
            dgl = [egl[h] * jnp.sum(dsn[h] * st[h], keepdims=True) + jnp.sum(d_kd[h] * kd[h], keepdims=True)
                   for h in heads]
            dgcs[a] = [sums(d_qg[h] * qg[h][sl]) + sums(d_kg[h] * kg[h][sl]) - sums(d_kd[h] * kd[h])
                       + jnp.where(row == CHUNK - 1, dgl[h], 0.0) for h in heads]
            dqs[a] = [d_qg[h] * eg[h][sl] for h in heads]
            dks[a] = [d_kg[h] * eg[h][sl] + d_kd[h] * dk_dec[h] for h in heads]
            dvs[a] = d_r
            drbs[a] = d_rb
            dsn = [dsn_new[h] + egl[h] * dsn[h] for h in heads]
        for h in heads:
            ds_ref[h] = dsn[h]
        cat = lambda xs, h: jnp.concatenate([xs[0][h], xs[1][h]], axis=0)
        vn = [cat(vns, h) for h in heads]
        d_rb = [cat(drbs, h) for h in heads]
        dp = [jnp.where(tril, _dot_nt(do[h], vn[h]), 0.0) for h in heads]
        dam = [jnp.where(strict, -_dot_nt(d_rb[h], vn[h]), 0.0) for h in heads]
        g_p = [dp[h] * dec[h] for h in heads]
        g_a = [dam[h] * dec[h] for h in heads]
        gbk = [bcol[h] * g_a[h] for h in heads]
        dq2 = [_dot(g_p[h], k[h]) for h in heads]
        dk2 = [_dot_tn(g_p[h], q[h]) + _dot(gbk[h], k[h]) + _dot_tn(gbk[h], k[h]) for h in heads]
        dgb = jnp.zeros((PAIR, LANE), F32)
        for h in heads:
            dq_ref[:, hs[h]] = cat(dqs, h) + dq2[h]
            dk_ref[:, hs[h]] = cat(dks, h) + dk2[h]
            dv_ref[:, hs[h]] = cat(dvs, h)
            dbeta = cat(dbs, h) + sums(g_a[h] * kk[h])
            mm = dp[h] * p[h] + dam[h] * amat[h]
            dgc = cat(dgcs, h) + sums(mm) - sums(mm.T)
            dgb = dgb + jnp.where(lane == A_LANE + h, dgc, 0.0) + jnp.where(lane == B_LANE + h, dbeta, 0.0)
        dgb_ref[...] = dgb

    return _pcall_riding(
        body, rider, name="gdn_bwd", grid=(npair,),
        in_specs=[pl.BlockSpec((PAIR, w), rev), pl.BlockSpec((PAIR, w), rev), pl.BlockSpec((PAIR, w), rev),
                  pl.BlockSpec((PAIR, LANE), rev),
                  pl.BlockSpec((LANE, PAIR), lambda i: (0, npair - 1 - i)),
                  pl.BlockSpec((2, nh, HEAD, HEAD), lambda i: (npair - 1 - i, 0, 0, 0)),
                  pl.BlockSpec((PAIR, w), rev)],
        out_specs=[pl.BlockSpec((PAIR, w), rev), pl.BlockSpec((PAIR, w), rev), pl.BlockSpec((PAIR, w), rev),
                   pl.BlockSpec((PAIR, LANE), rev)],
        out_shape=[SDS((s, w), F32), SDS((s, w), F32), SDS((s, w), F32), SDS((s, LANE), F32)],
        scratch=[pltpu.VMEM((nh, HEAD, HEAD), F32)], args=(q, k, v, gb, gbt, states, do))


def _mla_norm(proj, qw, kvw, col_q, col_kv):
    s = proj.shape[0]
    lr = qw.shape[1]
    ts = _tile(s, 512, 8)

    def body(cq_ref, ckv_ref, qw_ref, kvw_ref, oq_ref, okv_ref):
        for x_ref, w_ref, o_ref in ((cq_ref, qw_ref, oq_ref), (ckv_ref, kvw_ref, okv_ref)):
            xv = x_ref[...]
            r = lax.rsqrt(jnp.mean(xv * xv, axis=-1, keepdims=True) + EPS)
            o_ref[...] = (xv * r * w_ref[...]).astype(BF16)

    return _pcall(body, name="mla_norm", grid=(s // ts,),
                  in_specs=[_rows(ts, lr, col_q), _rows(ts, lr, col_kv), _full((1, lr)), _full((1, lr))],
                  out_specs=[_rows(ts, lr), _rows(ts, lr)],
                  out_shape=[SDS((s, lr), BF16), SDS((s, lr), BF16)])(proj, proj, qw, kvw)


def _mla_norm_bwd(proj, qw, kvw, dq, dkv, dproj, col_q, col_kv):
    s = proj.shape[0]
    lr = qw.shape[1]
    ts = _tile(s, 512, 8)

    assert col_kv == col_q + 1 and col_q % 2 == 0

    def body(cq_ref, ckv_ref, qw_ref, kvw_ref, dq_ref, dkv_ref, _, o_ref, dqw_ref, dkvw_ref):
        @pl.when(pl.program_id(0) == 0)
        def _():
            dqw_ref[...] = jnp.zeros_like(dqw_ref)
            dkvw_ref[...] = jnp.zeros_like(dkvw_ref)

        for k, (x_ref, w_ref, d_ref, dw_ref) in enumerate(((cq_ref, qw_ref, dq_ref, dqw_ref),
                                                           (ckv_ref, kvw_ref, dkv_ref, dkvw_ref))):
            xv, dh = x_ref[...], d_ref[...]
            r = lax.rsqrt(jnp.mean(xv * xv, axis=-1, keepdims=True) + EPS)
            xh = xv * r
            dw_ref[...] += jnp.sum(dh * xh, axis=0, keepdims=True)
            dxh = dh * w_ref[...]
            o_ref[:, lr * k:lr * (k + 1)] = (
                r * (dxh - xh * jnp.mean(dxh * xh, axis=-1, keepdims=True))).astype(BF16)

    return _pcall(body, name="mla_norm_bwd", grid=(s // ts,),
                  in_specs=[_rows(ts, lr, col_q), _rows(ts, lr, col_kv), _full((1, lr)), _full((1, lr)),
                            _rows(ts, lr), _rows(ts, lr), _ANY],
                  out_specs=[_rows(ts, 2 * lr, col_q // 2), _full((1, lr)), _full((1, lr))],
                  out_shape=[SDS(dproj.shape, BF16), SDS((1, lr), F32), SDS((1, lr), F32)],
                  aliases={6: 0})(proj, proj, qw, kvw, dq, dkv, dproj)


def _rope_tables(pos, invf, sgn):
    ang = pos * invf
    return jnp.cos(ang), jnp.sin(ang) * sgn


def _swap_halves_lanes(y):
    lane = _lane_iota(y.shape)
    return jnp.where(lane < ROPE // 2, pltpu.roll(y, LANE - ROPE // 2, 1), pltpu.roll(y, ROPE // 2, 1))


def _rope_consts():
    half = ROPE // 2
    inv = ROPE_THETA ** (-jnp.arange(half, dtype=F32) / half)
    invf = jnp.concatenate([inv, inv, jnp.zeros((LANE - ROPE,), F32)])[None, :]
    sgn = jnp.concatenate([-jnp.ones((half,), F32), jnp.ones((half,), F32),
                           jnp.zeros((LANE - ROPE,), F32)])[None, :]
    return invf, sgn


def _mla_rope(qraw, kvraw, proj, pos, nh, misc_col):
    s = qraw.shape[0]
    ts = _tile(s, MLA_BLOCK)
    wq = nh * 2 * HEAD
    invf, sgn = _rope_consts()

    def body(q_ref, kv_ref, misc_ref, pos_ref, if_ref, sg_ref, qc_ref, kc_ref, v_ref, qt_ref, vt_ref):
        c, sn = _rope_tables(pos_ref[...], if_ref[...], sg_ref[...])
        lane = _lane_iota(c.shape)
        rot = lambda xb: xb * c + _swap_halves_lanes(xb) * sn
        qs = SM_SCALE * LOG2E
        krot = jnp.where(lane < ROPE, rot(misc_ref[...]), 0.0).astype(BF16)
        for h in range(nh):
            b0 = 2 * HEAD * h
            qn = q_ref[:, b0:b0 + HEAD].astype(F32) * qs
            qr = rot(q_ref[:, b0 + HEAD:b0 + 2 * HEAD].astype(F32)) * qs
            qc_ref[:, b0:b0 + HEAD] = qn.astype(BF16)
            qc_ref[:, b0 + HEAD:b0 + 2 * HEAD] = qr.astype(BF16)
            qt_ref[b0:b0 + HEAD, :] = qn.T.astype(BF16)
            qt_ref[b0 + HEAD:b0 + 2 * HEAD, :] = qr.T.astype(BF16)
            kc_ref[:, b0:b0 + HEAD] = kv_ref[:, b0:b0 + HEAD].astype(BF16)
            kc_ref[:, b0 + HEAD:b0 + 2 * HEAD] = krot
            vh = kv_ref[:, wq + HEAD * h:wq + HEAD * (h + 1)]
            v_ref[:, HEAD * h:HEAD * (h + 1)] = vh.astype(BF16)
            vt_ref[HEAD * h:HEAD * (h + 1), :] = vh.astype(F32).T.astype(BF16)

    return _pcall(body, name="mla_rope", grid=(s // ts,),
                  in_specs=[_rows(ts, wq), _rows(ts, wq + nh * HEAD), _rows(ts, LANE, misc_col),
                            _rows(ts, 1), _full((1, LANE)), _full((1, LANE))],
                  out_specs=[_rows(ts, wq), _rows(ts, wq), _rows(ts, nh * HEAD),
                             pl.BlockSpec((None, wq, ts), lambda i: (i, 0, 0)),
                             pl.BlockSpec((None, nh * HEAD, ts), lambda i: (i, 0, 0))],
                  out_shape=[SDS((s, wq), BF16), SDS((s, wq), BF16), SDS((s, nh * HEAD), BF16),
                             SDS((s // ts, wq, ts), BF16), SDS((s // ts, nh * HEAD, ts), BF16)])(
                      qraw, kvraw, proj, pos, invf, sgn)


def _mla_rope_bwd(dqc, dkc, dv, pos, nh):
    s = dqc.shape[0]
    ts = _tile(s, 256, 8)
    wq = nh * 2 * HEAD
    invf, sgn = _rope_consts()

    def body(dq_ref, dk_ref, dv_ref, pos_ref, if_ref, sg_ref, oq_ref, okv_ref, okr_ref):
        c, sn = _rope_tables(pos_ref[...], if_ref[...], sg_ref[...])
        lane = _lane_iota(c.shape)
        unrot = lambda d: d * c + _swap_halves_lanes(d * sn)
        dkr = jnp.zeros(c.shape, F32)
        for h in range(nh):
            b0 = 2 * HEAD * h
            oq_ref[:, b0:b0 + HEAD] = (dq_ref[:, b0:b0 + HEAD] * SM_SCALE).astype(BF16)
            oq_ref[:, b0 + HEAD:b0 + 2 * HEAD] = (
                unrot(dq_ref[:, b0 + HEAD:b0 + 2 * HEAD]) * SM_SCALE).astype(BF16)
            okv_ref[:, b0:b0 + HEAD] = (dk_ref[:, b0:b0 + HEAD] * LN2).astype(BF16)
            okv_ref[:, b0 + HEAD:b0 + 2 * HEAD] = jnp.zeros((ts, HEAD), BF16)
            dkr = dkr + dk_ref[:, b0 + HEAD:b0 + 2 * HEAD]
        okv_ref[:, wq:] = dv_ref[...].astype(BF16)
        okr_ref[...] = jnp.where(lane < ROPE, unrot(jnp.where(lane < ROPE, dkr * LN2, 0.0)), 0.0)

    return _pcall(body, name="mla_rope_bwd", grid=(s // ts,),
                  in_specs=[_rows(ts, wq), _rows(ts, wq), _rows(ts, nh * HEAD), _rows(ts, 1),
                            _full((1, LANE)), _full((1, LANE))],
                  out_specs=[_rows(ts, wq), _rows(ts, wq + nh * HEAD), _rows(ts, LANE)],
                  out_shape=[SDS((s, wq), BF16), SDS((s, wq + nh * HEAD), BF16), SDS((s, LANE), F32)])(
                      dqc, dkc, dv, pos, invf, sgn)


def _causal_mask(blk):
    i = lax.broadcasted_iota(jnp.int32, (blk, blk), 0)
    j = lax.broadcasted_iota(jnp.int32, (blk, blk), 1)
    return j <= i


MLA_HP = 2
MLA_FWD_HP = 4


def _pair_pack(a, b):
    return jnp.where(_lane_iota(a.shape) < LANE // 2, a, b)


def _pair_unpack(x, e):
    lane = _lane_iota(x.shape)
    keep = (lane < LANE // 2) if e == 0 else (lane >= LANE // 2)
    return jnp.where(keep, x, pltpu.roll(x, LANE // 2, 1))


def _mla_fwd(qt, kc, vt, nh, rider):
    nb, _, blk = qt.shape
    s = nb * blk
    hp = MLA_FWD_HP if nh % MLA_FWD_HP == 0 else MLA_HP
    assert nh % hp == 0 and hp % MLA_HP == 0
    once = pl.Buffered(1)
    r_in, r_out = len(rider.arrays), len(rider.out_shapes)

    def body(*refs):
        qt_ref, k_ref, vt_ref = refs[:3]
        o_ref, lse_ref = refs[3 + r_in:5 + r_in]
        m_sc, l_sc, acc = refs[5 + r_in + r_out:8 + r_in + r_out]
        r_refs = (refs[3:3 + r_in], refs[5 + r_in:5 + r_in + r_out], refs[8 + r_in + r_out:])
        i = pl.program_id(1)
        grid_step = pl.program_id(0) * nb + i
        _ride_begin(rider, r_refs, grid_step)
        m_sc[...] = jnp.full_like(m_sc, -1e30)
        l_sc[...] = jnp.zeros_like(l_sc)
        acc[...] = jnp.zeros_like(acc)
        es = range(hp)

        def step(j, masked):
            rows = pl.ds(pl.multiple_of(j * blk, blk), blk)
            sc = [_dot(k_ref[rows, 2 * HEAD * e:2 * HEAD * (e + 1)], qt_ref[2 * HEAD * e:2 * HEAD * (e + 1), :])
                  for e in es]
            if masked:
                key = lax.broadcasted_iota(jnp.int32, (blk, blk), 0)
                qry = lax.broadcasted_iota(jnp.int32, (blk, blk), 1)
                sc = [jnp.where(key <= qry, x, -1e30) for x in sc]
            m_prev = [m_sc[e] for e in es]
            m_new = [jnp.maximum(m_prev[e], jnp.max(sc[e], axis=0, keepdims=True)) for e in es]
            p = [jnp.exp2(sc[e] - m_new[e]) for e in es]
            alpha = [jnp.exp2(m_prev[e] - m_new[e]) for e in es]
            pv = [_dot(vt_ref[j, HEAD * e:HEAD * (e + 1), :], p[e]) for e in es]
            for e in es:
                l_sc[e] = alpha[e] * l_sc[e] + jnp.sum(p[e], axis=0, keepdims=True)
                acc[e] = alpha[e] * acc[e] + pv[e]
                m_sc[e] = m_new[e]

        def loop_body(j, carry):
            step(j, False)
            return carry

        lax.fori_loop(0, i, loop_body, 0)
        step(i, True)
        lse = []
        for e in es:
            o_ref[:, HEAD * e:HEAD * (e + 1)] = (acc[e] / l_sc[e]).T
            row = m_sc[e] + jnp.log(l_sc[e]) * LOG2E
            lse.append(jnp.broadcast_to(row, (LANE, blk)).T)
        for g in range(hp // MLA_HP):
            lse_ref[g] = _pair_pack(lse[MLA_HP * g], lse[MLA_HP * g + 1])
        _ride_end(rider, r_refs, grid_step, (nh // hp) * nb)

    outs = _pcall(
        body, name="mla_fwd", grid=(nh // hp, nb),
        in_specs=[pl.BlockSpec((None, hp * 2 * HEAD, blk), lambda g, i: (i, g, 0)),
                  pl.BlockSpec((s, hp * 2 * HEAD), lambda g, i: (0, g), pipeline_mode=once),
                  pl.BlockSpec((nb, hp * HEAD, blk), lambda g, i: (0, g, 0), pipeline_mode=once)]
        + [_ANY] * r_in,
        out_specs=[pl.BlockSpec((blk, hp * HEAD), lambda g, i: (i, g)),
                   pl.BlockSpec((hp // MLA_HP, blk, LANE), lambda g, i: (g, i, 0))] + [_ANY] * r_out,
        out_shape=[SDS((s, nh * HEAD), F32), SDS((nh // MLA_HP, s, LANE), F32)] + rider.out_shapes,
        scratch=[pltpu.VMEM((hp, 1, blk), F32), pltpu.VMEM((hp, 1, blk), F32),
                 pltpu.VMEM((hp, HEAD, blk), F32)] + rider.scratch)(qt, kc, vt, *rider.arrays)
    return outs[0], outs[1], outs[2:]


def _mla_bwd(qc, kc, v, do, lse, delta, nh, rider):
    s = qc.shape[0]
    blk = _tile(s, MLA_BLOCK)
    nb = s // blk
    rep = blk // LANE
    hp = MLA_HP
    once = pl.Buffered(1)
    r_in, r_out = len(rider.arrays), len(rider.out_shapes)
    qs = [slice(2 * HEAD * e, 2 * HEAD * (e + 1)) for e in range(hp)]
    vs = [slice(HEAD * e, HEAD * (e + 1)) for e in range(hp)]

    def body(*refs):
        q_ref, do_ref, lse_ref, dl_ref, k_ref, v_ref = refs[:6]
        dq_ref, dk_ref, dv_ref = refs[6 + r_in:9 + r_in]
        dk_acc, dv_acc = refs[9 + r_in + r_out:11 + r_in + r_out]
        r_refs = (refs[6:6 + r_in], refs[9 + r_in:9 + r_in + r_out], refs[11 + r_in + r_out:])
        j = pl.program_id(1)
        grid_step = pl.program_id(0) * nb + j
        _ride_begin(rider, r_refs, grid_step)

        @pl.when(j == 0)
        def _():
            dq_ref[...] = jnp.zeros_like(dq_ref)

        dk_acc[...] = jnp.zeros_like(dk_acc)
        dv_acc[...] = jnp.zeros_like(dv_acc)
        es = range(hp)
        kj = [k_ref[:, qs[e]] for e in es]
        vj = [v_ref[:, vs[e]] for e in es]

        def step(i, masked):
            rows = pl.ds(pl.multiple_of(i * blk, blk), blk)
            qi = [q_ref[rows, qs[e]] for e in es]
            doi = [do_ref[rows, vs[e]] for e in es]
            lse, dl = lse_ref[rows, :], dl_ref[rows, :]
            sc = [_dot_nt(qi[e], kj[e]) for e in es]
            dp = [_dot_nt(doi[e], vj[e]) for e in es]
            if masked:
                sc = [jnp.where(_causal_mask(blk), x, -1e30) for x in sc]
            p = [jnp.exp2(sc[e] - jnp.tile(_pair_unpack(lse, e), (1, rep))) for e in es]
            ds = [p[e] * (dp[e] - jnp.tile(_pair_unpack(dl, e), (1, rep))) for e in es]
            dv = [_dot_tn(p[e], doi[e]) for e in es]
            dk = [_dot_tn(ds[e], qi[e]) for e in es]
            dq = [_dot(ds[e], kj[e]) for e in es]
            for e in es:
                dv_acc[:, vs[e]] += dv[e]
                dk_acc[:, qs[e]] += dk[e]
                dq_ref[rows, qs[e]] += dq[e]

        def loop_body(i, carry):
            step(i, False)
            return carry

        step(j, True)
        lax.fori_loop(j + 1, nb, loop_body, 0)
        dk_ref[...] = dk_acc[...]
        dv_ref[...] = dv_acc[...]
        _ride_end(rider, r_refs, grid_step, (nh // hp) * nb)

    outs = _pcall(
        body, name="mla_bwd", grid=(nh // hp, nb),
        in_specs=[pl.BlockSpec((s, hp * 2 * HEAD), lambda g, j: (0, g), pipeline_mode=once),
                  pl.BlockSpec((s, hp * HEAD), lambda g, j: (0, g), pipeline_mode=once),
                  pl.BlockSpec((None, s, LANE), lambda g, j: (g, 0, 0), pipeline_mode=once),
                  pl.BlockSpec((None, s, LANE), lambda g, j: (g, 0, 0), pipeline_mode=once),
                  pl.BlockSpec((blk, hp * 2 * HEAD), lambda g, j: (j, g)),
                  pl.BlockSpec((blk, hp * HEAD), lambda g, j: (j, g))] + [_ANY] * r_in,
        out_specs=[pl.BlockSpec((s, hp * 2 * HEAD), lambda g, j: (0, g), pipeline_mode=once),
                   pl.BlockSpec((blk, hp * 2 * HEAD), lambda g, j: (j, g)),
                   pl.BlockSpec((blk, hp * HEAD), lambda g, j: (j, g))] + [_ANY] * r_out,
        out_shape=[SDS((s, nh * 2 * HEAD), F32), SDS((s, nh * 2 * HEAD), F32),
                   SDS((s, nh * HEAD), F32)] + rider.out_shapes,
        scratch=[pltpu.VMEM((blk, hp * 2 * HEAD), F32), pltpu.VMEM((blk, hp * HEAD), F32)] + rider.scratch,
        vmem=VMEM_LIMIT_WIDE)(qc, do, lse, delta, kc, v, *rider.arrays)
    return outs[0], outs[1], outs[2], outs[3:]


def _mix_fwd(og, proj, om, gw, mw, nh, z_col):
    s = og.shape[0]
    w = nh * HEAD
    ts = _tile(s, 256, 8)

    def body(og_ref, z_ref, om_ref, gw_ref, mw_ref, o_ref):
        for h in range(nh):
            hs = slice(HEAD * h, HEAD * (h + 1))
            a = og_ref[:, hs]
            r = lax.rsqrt(jnp.mean(a * a, axis=-1, keepdims=True) + EPS)
            o_ref[:, hs] = (a * r * gw_ref[...] * _silu(z_ref[:, hs])).astype(BF16)
            b = om_ref[:, hs]
            r = lax.rsqrt(jnp.mean(b * b, axis=-1, keepdims=True) + EPS)
            o_ref[:, w + HEAD * h:w + HEAD * (h + 1)] = (b * r * mw_ref[...]).astype(BF16)

    return _pcall(body, name="mix_fwd", grid=(s // ts,),
                  in_specs=[_rows(ts, w), _rows(ts, w, z_col), _rows(ts, w), _full((1, HEAD)),
                            _full((1, HEAD))],
                  out_specs=_rows(ts, 2 * w), out_shape=SDS((s, 2 * w), BF16))(og, proj, om, gw, mw)


def _mix_bwd(dmix, og, proj, om, gw, mw, nh, z_col):
    s = og.shape[0]
    w = nh * HEAD
    ts = _tile(s, 256, 8)

    def body(d_ref, og_ref, z_ref, om_ref, gw_ref, mw_ref, dog_ref, dz_ref, dom_ref, dgw_ref, dmw_ref,
             dl_ref):
        @pl.when(pl.program_id(0) == 0)
        def _():
            dgw_ref[...] = jnp.zeros_like(dgw_ref)
            dmw_ref[...] = jnp.zeros_like(dmw_ref)

        dgw = jnp.zeros((1, HEAD), F32)
        dmw = jnp.zeros((1, HEAD), F32)
        deltas = []
        for h in range(nh):
            hs = slice(HEAD * h, HEAD * (h + 1))
            a, z, dy = og_ref[:, hs], z_ref[:, hs], d_ref[:, hs]
            r = lax.rsqrt(jnp.mean(a * a, axis=-1, keepdims=True) + EPS)
            ah = a * r
            sz = _silu(z)
            dz_ref[:, hs] = (dy * (ah * gw_ref[...]) * _dsilu(z)).astype(BF16)
            dn = dy * sz
            dgw = dgw + jnp.sum(dn * ah, axis=0, keepdims=True)
            dah = dn * gw_ref[...]
            dog_ref[:, hs] = r * (dah - ah * jnp.mean(dah * ah, axis=-1, keepdims=True))
            b, dyb = om_ref[:, hs], d_ref[:, w + HEAD * h:w + HEAD * (h + 1)]
            r = lax.rsqrt(jnp.mean(b * b, axis=-1, keepdims=True) + EPS)
            bh = b * r
            dmw = dmw + jnp.sum(dyb * bh, axis=0, keepdims=True)
            dbh = dyb * mw_ref[...]
            dom = r * (dbh - bh * jnp.mean(dbh * bh, axis=-1, keepdims=True))
            dom_ref[:, hs] = dom.astype(BF16)
            deltas.append(jnp.broadcast_to(jnp.sum(dom * b, axis=-1, keepdims=True), (ts, LANE)))
        for g in range(nh // MLA_HP):
            dl_ref[g] = _pair_pack(deltas[2 * g], deltas[2 * g + 1])
        dgw_ref[...] += dgw
        dmw_ref[...] += dmw

    return _pcall(body, name="mix_bwd", grid=(s // ts,),
                  in_specs=[_rows(ts, 2 * w), _rows(ts, w), _rows(ts, w, z_col), _rows(ts, w),
                            _full((1, HEAD)), _full((1, HEAD))],
                  out_specs=[_rows(ts, w), _rows(ts, w, z_col), _rows(ts, w), _full((1, HEAD)), _full((1, HEAD)),
                             pl.BlockSpec((nh // MLA_HP, ts, LANE), lambda i: (0, i, 0))],
                  out_shape=[SDS((s, w), F32), SDS((s, proj.shape[1]), BF16), SDS((s, w), BF16),
                             SDS((1, HEAD), F32), SDS((1, HEAD), F32),
                             SDS((nh // MLA_HP, s, LANE), F32)])(dmix, og, proj, om, gw, mw)


def _swiglu_fwd(h2, wg, wu):
    m, kdim = h2.shape
    tn = wg.shape[2]
    n = 4 * tn
    tm, tk = _tile(m, 512), _tile(kdim, 2048)
    nk = kdim // tk

    def body(a_ref, g_ref, u_ref, act_ref, go_ref, uo_ref, gacc, uacc):
        k = pl.program_id(2)

        @pl.when(k == 0)
        def _():
            gacc[...] = jnp.zeros_like(gacc)
            uacc[...] = jnp.zeros_like(uacc)

        a = a_ref[...]
        gacc[...] += _dot(a, g_ref[...])
        uacc[...] += _dot(a, u_ref[...])

        @pl.when(k == nk - 1)
        def _():
            g, u = gacc[...], uacc[...]
            act_ref[...] = (_silu(g) * u).astype(BF16)
            go_ref[...] = g.astype(BF16)
            uo_ref[...] = u.astype(BF16)

    a_spec = pl.BlockSpec((tm, tk), lambda i, j, k: (i, k))
    b_spec = pl.BlockSpec((None, tk, tn), lambda i, j, k: (j, k, 0))
    o_spec = pl.BlockSpec((tm, tn), lambda i, j, k: (i, j))
    return _pcall(body, name="swiglu_fwd", grid=(m // tm, n // tn, nk),
                  in_specs=[a_spec, b_spec, b_spec], out_specs=[o_spec] * 3,
                  out_shape=[SDS((m, n), BF16)] * 3,
                  scratch=[pltpu.VMEM((tm, tn), F32), pltpu.VMEM((tm, tn), F32)])(h2, wg, wu)


def _swiglu_bwd(dx3, wd, g, u):
    m, kdim = dx3.shape
    n = wd.shape[0]
    tm, tn = _tile(m, 1024), _tile(n, 512)

    def body(a_ref, b_ref, g_ref, u_ref, dg_ref, du_ref):
        da = _dot_nt(a_ref[...], b_ref[...])
        gv, uv = g_ref[...].astype(F32), u_ref[...].astype(F32)
        dg_ref[...] = (da * uv * _dsilu(gv)).astype(BF16)
        du_ref[...] = (da * _silu(gv)).astype(BF16)

    a_spec = pl.BlockSpec((tm, kdim), lambda i, j: (i, 0))
    b_spec = pl.BlockSpec((tn, kdim), lambda i, j: (j, 0))
    o_spec = pl.BlockSpec((tm, tn), lambda i, j: (i, j))
    return _pcall(body, name="swiglu_bwd", grid=(m // tm, n // tn),
                  in_specs=[a_spec, b_spec, o_spec, o_spec], out_specs=[o_spec] * 2,
                  out_shape=[SDS((m, n), BF16)] * 2)(dx3, wd, g, u)


def _sum_pair(g, recv, place, name):
    _, _, rh, c = g.shape
    tr = _tile(rh, 256, 16)

    def body(pl_ref, g_ref, r_ref, o16_ref, own_ref):
        sm = g_ref[...].astype(F32) + r_ref[...].astype(F32)
        o16_ref[...] = sm.astype(BF16)

        @pl.when(pl.program_id(1) == pl_ref[1])
        def _():
            own_ref[...] = sm

    grid_spec = pltpu.PrefetchScalarGridSpec(
        num_scalar_prefetch=1, grid=(rh // tr, 4),
        in_specs=[pl.BlockSpec((None, None, tr, c), lambda i, t, p: (t, p[0], i, 0)),
                  pl.BlockSpec((None, tr, c), lambda i, t, p: (t, i, 0))],
        out_specs=[pl.BlockSpec((None, tr, c), lambda i, t, p: (t, i, 0)),
                   pl.BlockSpec((tr, c), lambda i, t, p: (i, 0))])
    return pl.pallas_call(
        body, name=name, grid_spec=grid_spec,
        out_shape=[SDS((4, rh, c), BF16), SDS((rh, c), F32)],
        compiler_params=pltpu.CompilerParams(dimension_semantics=("arbitrary",) * 2,
                                             vmem_limit_bytes=VMEM_LIMIT))(place, g, recv)


def _sum_chips(own, recv, name):
    rh, c = own.shape
    tr = _tile(rh, 256, 16)

    def body(o_ref, r_ref, out_ref):
        acc = o_ref[...]
        for j in range(3):
            acc = acc + r_ref[j].astype(F32)
        out_ref[...] = acc

    return _pcall(body, name=name, grid=(rh // tr,),
                  in_specs=[_rows(tr, c), pl.BlockSpec((3, tr, c), lambda i: (0, i, 0))],
                  out_specs=_rows(tr, c), out_shape=SDS(own.shape, F32))(own, recv)


def _adamw_update(wv, gv, mv, vv):
    mn = ADAM_B1 * mv + (1.0 - ADAM_B1) * gv
    vn = ADAM_B2 * vv + (1.0 - ADAM_B2) * (gv * gv)
    m_hat = mn / (1.0 - ADAM_B1 ** ADAM_STEP)
    v_hat = vn / (1.0 - ADAM_B2 ** ADAM_STEP)
    return -ADAM_LR * (m_hat / (jnp.sqrt(v_hat) + ADAM_EPS) + ADAM_WD * wv), mn, vn


def _adamw(w, g, m, v, name):
    r, c = w.shape
    tr = _tile(r, 256, 8)

    def body(w_ref, g_ref, m_ref, v_ref, d_ref, mo_ref, vo_ref):
        d_ref[...], mo_ref[...], vo_ref[...] = _adamw_update(w_ref[...], g_ref[...], m_ref[...], v_ref[...])

    spec = _rows(tr, c)
    return _pcall(body, name=name, grid=(r // tr,), in_specs=[spec] * 4, out_specs=[spec] * 3,
                  out_shape=[SDS(w.shape, F32)] * 3)(w, g, m, v)


def _adamw_halves(w, mine, theirs, m, v, place, name):
    r, c = w.shape
    rh = r // 2
    tr = _tile(rh, 256, 8)
    nt = rh // tr

    def body(p_ref, w_ref, a_ref, b_ref, m_ref, v_ref, g_ref, d_ref, mo_ref, vo_ref):
        gv = jnp.where(pl.program_id(0) // nt == p_ref[0], a_ref[...], b_ref[...])
        g_ref[...] = gv
        d_ref[...], mo_ref[...], vo_ref[...] = _adamw_update(w_ref[...], gv, m_ref[...], v_ref[...])

    full = pl.BlockSpec((tr, c), lambda i, p: (i, 0))
    half = pl.BlockSpec((tr, c), lambda i, p: (i % nt, 0))
    grid_spec = pltpu.PrefetchScalarGridSpec(num_scalar_prefetch=1, grid=(2 * nt,),
                                             in_specs=[full, half, half, full, full], out_specs=[full] * 4)
    return pl.pallas_call(
        body, name=name, grid_spec=grid_spec, out_shape=[SDS(w.shape, F32)] * 4,
        compiler_params=pltpu.CompilerParams(dimension_semantics=("arbitrary",),
                                             vmem_limit_bytes=VMEM_LIMIT))(place, w, mine, theirs, m, v)


def _place():
    x, y, c = lax.axis_index("x"), lax.axis_index("y"), lax.axis_index("c")
    chips = [(1 - x, y), (x, 1 - y), (1 - x, 1 - y)]
    return x, y, c, chips


_ANY = pl.BlockSpec(memory_space=pl.ANY)


def _remote(src, dst, sems, k, to):
    return pltpu.make_async_remote_copy(src_ref=src, dst_ref=dst, send_sem=sems[0].at[k], recv_sem=sems[1].at[k],
                                        device_id=to, device_id_type=MESH)


class _Gather:
    def __init__(self, shards):
        n = len(shards)
        self.arrays = list(shards)
        self.out_shapes = [SDS((4,) + a.shape, a.dtype) for a in shards]
        self.scratch = [pltpu.SemaphoreType.DMA((7 * n,)), pltpu.SemaphoreType.DMA((7 * n,))]

    def _plan(self, ins, outs, sems):
        x, y, c, chips = _place()
        own, sib = 2 * x + y, (x, y, 1 - c)
        plan = []
        for wi, (w, o) in enumerate(zip(ins, outs)):
            rh = w.shape[0] // 2
            mine, theirs = pl.ds(c * rh, rh), pl.ds((1 - c) * rh, rh)
            whole = _remote(w, o.at[own], sems, 7 * wi + 6, sib)
            ici, d2d, d2d_in = [], [], []
            for j, (tx, ty) in enumerate(chips):
                t = 2 * tx + ty
                ici.append(_remote(w.at[mine], o.at[own, mine], sems, 7 * wi + j, (tx, ty, c)))
                d2d.append(_remote(o.at[t, mine], o.at[t, mine], sems, 7 * wi + 3 + j, sib))
                d2d_in.append(_remote(o.at[t, theirs], o.at[t, theirs], sems, 7 * wi + 3 + j, sib))
            plan.append((whole, ici, d2d, d2d_in))
        return plan

    def begin(self, ins, outs, sems):
        for whole, ici, _, _ in self._plan(ins, outs, sems):
            whole.start()
            for cp in ici:
                cp.start()

    def middle(self, ins, outs, sems):
        for _, ici, d2d, _ in self._plan(ins, outs, sems):
            for cp_in, cp_on in zip(ici, d2d):
                cp_in.wait_recv()
                cp_on.start()

    def finish(self, ins, outs, sems):
        for whole, ici, d2d, d2d_in in self._plan(ins, outs, sems):
            for cp in d2d_in:
                cp.wait_recv()
            for cp in ici + d2d:
                cp.wait_send()
            whole.wait()


class _Swap:
    def __init__(self, grads):
        n = len(grads)
        self.arrays = list(grads)
        self.out_shapes = [SDS((4,) + g.shape[2:], g.dtype) for g in grads]
        self.scratch = [pltpu.SemaphoreType.DMA((4 * n,)), pltpu.SemaphoreType.DMA((4 * n,))]

    def _plan(self, ins, outs, sems):
        x, y, c, _ = _place()
        return [_remote(g.at[t, 1 - c], o.at[t], sems, 4 * wi + t, (x, y, 1 - c))
                for wi, (g, o) in enumerate(zip(ins, outs)) for t in range(4)]

    def begin(self, ins, outs, sems):
        for cp in self._plan(ins, outs, sems):
            cp.start()

    def middle(self, ins, outs, sems):
        pass

    def finish(self, ins, outs, sems):
        for cp in self._plan(ins, outs, sems):
            cp.wait()


class _Exchange:
    def __init__(self, pieces):
        n = len(pieces)
        self.arrays = list(pieces)
        self.out_shapes = [SDS((3,) + p.shape[1:], p.dtype) for p in pieces]
        self.scratch = [pltpu.SemaphoreType.DMA((3 * n,)), pltpu.SemaphoreType.DMA((3 * n,))]

    def _plan(self, ins, outs, sems):
        x, y, c, chips = _place()
        return [_remote(g.at[2 * tx + ty], o.at[j], sems, 3 * wi + j, (tx, ty, c))
                for wi, (g, o) in enumerate(zip(ins, outs)) for j, (tx, ty) in enumerate(chips)]

    def begin(self, ins, outs, sems):
        for cp in self._plan(ins, outs, sems):
            cp.start()

    def middle(self, ins, outs, sems):
        pass

    def finish(self, ins, outs, sems):
        for cp in self._plan(ins, outs, sems):
            cp.wait()


class _Share:
    def __init__(self, totals):
        n = len(totals)
        self.arrays = list(totals)
        self.out_shapes = [SDS(t.shape, t.dtype) for t in totals]
        self.scratch = [pltpu.SemaphoreType.DMA((n,)), pltpu.SemaphoreType.DMA((n,))]

    def _plan(self, ins, outs, sems):
        x, y, c, _ = _place()
        return [_remote(t, o, sems, wi, (x, y, 1 - c)) for wi, (t, o) in enumerate(zip(ins, outs))]

    def begin(self, ins, outs, sems):
        for cp in self._plan(ins, outs, sems):
            cp.start()

    def middle(self, ins, outs, sems):
        pass

    def finish(self, ins, outs, sems):
        for cp in self._plan(ins, outs, sems):
            cp.wait()


def _ride_begin(rider, r_refs, step):
    @pl.when(step == 0)
    def _():
        rider.begin(*r_refs)


def _ride_end(rider, r_refs, step, nsteps):
    @pl.when(step == min(3 * nsteps // 4, nsteps - 1))
    def _():
        rider.middle(*r_refs)

    @pl.when(step == nsteps - 1)
    def _():
        rider.finish(*r_refs)


def _comm(rider, name):
    n_in, n_out = len(rider.arrays), len(rider.out_shapes)

    def body(*refs):
        r_refs = (refs[:n_in], refs[n_in:n_in + n_out], refs[n_in + n_out:])
        rider.begin(*r_refs)
        rider.middle(*r_refs)
        rider.finish(*r_refs)

    return pl.pallas_call(body, name=name, out_shape=rider.out_shapes, in_specs=[_ANY] * n_in,
                          out_specs=[_ANY] * n_out, scratch_shapes=rider.scratch)(*rider.arrays)


def _small_allreduce(pk, name):
    r = pk.shape[0]
    rels = [(dx, dy, dc) for dx in (0, 1) for dy in (0, 1) for dc in (0, 1) if dx or dy or dc]

    def body(p_ref, o_ref, buf, send_sems, recv_sems):
        x, y, c, _ = _place()
        me = 4 * x + 2 * y + c
        buf[me] = p_ref[...]
        cps = []
        for k, (dx, dy, dc) in enumerate(rels):
            to = (1 - x if dx else x, 1 - y if dy else y, 1 - c if dc else c)
            cps.append(pltpu.make_async_remote_copy(src_ref=p_ref, dst_ref=buf.at[me], send_sem=send_sems.at[k],
                                                    recv_sem=recv_sems.at[k], device_id=to,
                                                    device_id_type=MESH))
        for cpy in cps:
            cpy.start()
        for cpy in cps:
            cpy.wait()
        acc = buf[0]
        for d in range(1, 8):
            acc = acc + buf[d]
        o_ref[...] = acc

    vm = pl.BlockSpec(memory_space=pltpu.VMEM)
    return pl.pallas_call(body, name=name, out_shape=SDS(pk.shape, F32), in_specs=[vm], out_specs=vm,
                          scratch_shapes=[pltpu.VMEM((8, r, LANE), F32), pltpu.SemaphoreType.DMA((7,)),
                                          pltpu.SemaphoreType.DMA((7,))])(pk)


ATTN_W = ("w_in", "w_uq", "w_ukv", "w_out")
FFN_W = ("w_gate", "w_up", "w_down")
BIG = ATTN_W + FFN_W


def _cols_from_chips(g):
    return jnp.concatenate([g[t] for t in range(4)], axis=1)


def _cols_to_chips(full):
    r, n = full.shape
    return full.reshape(r, 4, n // 4).transpose(1, 0, 2).reshape(4, 2, r // 2, n // 4)


def _rows_to_chips(full):
    n, c = full.shape
    return full.reshape(4, 2, n // 8, c)


def _permute_w_in(w, nh):
    d = w.shape[0]
    g = 4 * nh * HEAD
    lr = (w.shape[1] - g - 2 * nh - ROPE) // 2
    o = g + 2 * nh
    pad = jnp.zeros((d, LANE - ROPE - 8 - nh), w.dtype)
    pad8 = jnp.zeros((d, 8 - nh), w.dtype)
    return jnp.concatenate([w[:, :g], w[:, o:o + 2 * lr], w[:, o + 2 * lr:], w[:, g:g + nh], pad8,
                            w[:, g + nh:g + 2 * nh], pad, jnp.zeros((d, LANE), w.dtype)], axis=1)


def _unpermute_w_in(wp, nh, lr):
    g = 4 * nh * HEAD
    mc = g + 2 * lr
    return jnp.concatenate([wp[:, :g], wp[:, mc + B_LANE:mc + B_LANE + nh], wp[:, mc + A_LANE:mc + A_LANE + nh],
                            wp[:, g:g + 2 * lr], wp[:, mc:mc + ROPE]], axis=1)


def _permute_w_uq(w, nh):
    lr = w.shape[0]
    w3 = w.reshape(lr, nh, HEAD + ROPE)
    return jnp.concatenate([w3, jnp.zeros((lr, nh, HEAD - ROPE), w.dtype)], axis=2).reshape(lr, nh * 2 * HEAD)


def _unpermute_w_uq(wp, nh):
    lr = wp.shape[0]
    return wp.reshape(lr, nh, 2 * HEAD)[:, :, :HEAD + ROPE].reshape(lr, nh * (HEAD + ROPE))


def _permute_w_ukv(w, nh):
    lr = w.shape[0]
    w3 = w.reshape(lr, nh, 2 * HEAD)
    kp = jnp.concatenate([w3[:, :, :HEAD], jnp.zeros((lr, nh, HEAD), w.dtype)], axis=2)
    return jnp.concatenate([kp.reshape(lr, nh * 2 * HEAD), w3[:, :, HEAD:].reshape(lr, nh * HEAD)], axis=1)


def _unpermute_w_ukv(wp, nh):
    lr = wp.shape[0]
    kp = wp[:, :nh * 2 * HEAD].reshape(lr, nh, 2 * HEAD)[:, :, :HEAD]
    vp = wp[:, nh * 2 * HEAD:].reshape(lr, nh, HEAD)
    return jnp.concatenate([kp, vp], axis=2).reshape(lr, nh * 2 * HEAD)


def _sum_pairs(grads, recv, place, tag):
    sums = [_sum_pair(g, r, place, "sum_pair_%s%d" % (tag, k)) for k, (g, r) in enumerate(zip(grads, recv))]
    return [s[0] for s in sums], [s[1] for s in sums]


def _reduce_end(own, recv, tag):
    return [_sum_chips(o, r, "sum_chips_%s%d" % (tag, k)) for k, (o, r) in enumerate(zip(own, recv))]


def _step(x, pos, tgt, w_in, attn_shards, ffn_shards, small, place):
    nh = small["a_log"].shape[1]
    lr = small["q_norm_w"].shape[1]
    w = nh * HEAD
    z_col, col_q, col_kv = 3, 4 * w // lr, 4 * w // lr + 1
    misc_c = 4 * w + 2 * lr
    misc_col = misc_c // LANE
    assert (4 * w) % lr == 0 and small["kv_norm_w"].shape[1] == lr

    zl = jnp.zeros((1, LANE), F32)
    alog_l = zl.at[:, A_LANE:A_LANE + nh].set(small["a_log"])
    dtb_l = zl.at[:, A_LANE:A_LANE + nh].set(small["dt_bias"])
    conv_w = small["conv_w"]

    h1, (in4,) = _norm_fwd(x, small["attn_norm_w"], "norm1", rider=_Gather([w_in]))
    win_p = _permute_w_in(_cols_from_chips(in4), nh)
    proj, (uq4, ukv4, out4) = _mm([(h1, win_p)], name="proj_in", rider=_Gather(attn_shards))
    wuq_p = _permute_w_uq(_cols_from_chips(uq4), nh)
    wukv_p = _permute_w_ukv(_cols_from_chips(ukv4), nh)
    w_out = out4.reshape(-1, out4.shape[2])
    gq, gk, gv, gb, gbt = _gdn_prep(proj, conv_w, alog_l, dtb_l, nh, misc_col)
    (o_gdn, states), (wg4,) = _gdn_fwd(gq, gk, gv, gb, gbt, nh, _Gather(ffn_shards[:1]))
    cqn, ckvn = _mla_norm(proj, small["q_norm_w"], small["kv_norm_w"], col_q, col_kv)
    qraw = _mm([(cqn, wuq_p)], name="proj_uq", out_dtype=BF16)
    kvraw = _mm([(ckvn, wukv_p)], name="proj_ukv", out_dtype=BF16)
    qc, kc, vv, qt, vt = _mla_rope(qraw, kvraw, proj, pos, nh, misc_col)
    o_mla, lse, (wu4, wd4) = _mla_fwd(qt, kc, vt, nh, _Gather(ffn_shards[1:]))
    w_down = wd4.reshape(-1, wd4.shape[2])
    mixed = _mix_fwd(o_gdn, proj, o_mla, small["gdn_norm_w"], small["mla_out_norm_w"], nh, z_col)
    x2 = _mm([(mixed, w_out)], name="proj_out", res=x)
    h2 = _norm_fwd(x2, small["ffn_norm_w"], "norm2")
    act, gpre, upre = _swiglu_fwd(h2, wg4, wu4)
    x3 = _mm([(act, w_down)], name="proj_down", res=x2, tk=2816)
    dx3, d_final, loss, dx3h = _final_loss(x3, tgt, small["final_norm_w"])

    gs = {"final_norm_w": d_final}
    dgate, dup = _swiglu_bwd(dx3h, w_down, gpre, upre)
    g_down = _rows_to_chips(_mm([(act, dx3h)], name="dw_down", ta=True, out_dtype=BF16))
    g_gate = _mm([(h2, dgate)], name="dw_gate", ta=True, out_dtype=BF16, out_chips=True)
    g_up = _mm([(h2, dup)], name="dw_up", ta=True, out_dtype=BF16, out_chips=True)
    halves = lambda g: g.reshape(4, 2, g.shape[1] // 2, g.shape[2])
    ffn_g = [halves(g_gate), halves(g_up), g_down]
    dh2, ffn_sib = _mm([(dgate, wg4), (dup, wu4)], name="dh2", tb=True, b_chips=True, out_dtype=BF16,
                       rider=_Swap(ffn_g))
    ffn16, ffn_own = _sum_pairs(ffn_g, ffn_sib, place, "ffn")
    dx2, gs["ffn_norm_w"], dx2h = _norm_bwd(dh2, x2, small["ffn_norm_w"], dx3, "norm2_bwd", True)
    dmix = _mm([(dx2h, w_out)], name="dmix", tb=True, out_dtype=BF16)
    g_out = _rows_to_chips(_mm([(mixed, dx2h)], name="dw_out", ta=True, out_dtype=BF16))
    d_ogdn, dproj, d_omla, gs["gdn_norm_w"], gs["mla_out_norm_w"], delta = _mix_bwd(
        dmix, o_gdn, proj, o_mla, small["gdn_norm_w"], small["mla_out_norm_w"], nh, z_col)
    dqc, dkc, dvv, ffn_recv = _mla_bwd(qc, kc, vv, d_omla, lse, delta, nh, _Exchange(ffn16))
    ffn_tot = _reduce_end(ffn_own, ffn_recv, "ffn")
    dqraw, dkvraw, dkr = _mla_rope_bwd(dqc, dkc, dvv, pos, nh)
    dcqn = _mm([(dqraw, wuq_p)], name="dcqn", tb=True)
    dckvn = _mm([(dkvraw, wukv_p)], name="dckvn", tb=True)
    g_uq = _cols_to_chips(_unpermute_w_uq(_mm([(cqn, dqraw)], name="dw_uq", ta=True, out_dtype=BF16), nh))
    g_ukv = _cols_to_chips(_unpermute_w_ukv(_mm([(ckvn, dkvraw)], name="dw_ukv", ta=True, out_dtype=BF16), nh))
    dproj, gs["q_norm_w"], gs["kv_norm_w"] = _mla_norm_bwd(
        proj, small["q_norm_w"], small["kv_norm_w"], dcqn, dckvn, dproj, col_q, col_kv)
    (dgq, dgk, dgv, dgb), ffn_shared = _gdn_bwd(gq, gk, gv, gb, gbt, states, d_ogdn, nh, _Share(ffn_tot))
    dconv, dproj, gs["conv_w"], dal, ddb = _gdn_prep_bwd(
        proj, conv_w, alog_l, dtb_l, dgq, dgk, dgv, dgb, dkr, dproj, nh, misc_col)
    gs["a_log"] = dal[:, A_LANE:A_LANE + nh]
    gs["dt_bias"] = ddb[:, A_LANE:A_LANE + nh]
    dproj = _conv_bwd_input(dconv, conv_w, dproj)
    g_in = _cols_to_chips(_unpermute_w_in(_mm([(h1, dproj)], name="dw_in", ta=True, out_dtype=BF16), nh, lr))
    att_g = [g_in, g_uq, g_ukv, g_out]
    att16, att_own = _sum_pairs(att_g, _comm(_Swap(att_g), "swap_att"), place, "att")
    dh1, att_recv = _mm([(dproj, win_p)], name="dh1", tb=True, out_dtype=BF16, rider=_Exchange(att16))
    att_tot = _reduce_end(att_own, att_recv, "att")
    att_shared = _comm(_Share(att_tot), "share_att")
    grad_x, gs["attn_norm_w"] = _norm_bwd(dh1, x, small["attn_norm_w"], dx2, "norm1_bwd", False)
    return loss, grad_x, att_tot + ffn_tot, list(att_shared) + list(ffn_shared), gs


SMALL = ("attn_norm_w", "ffn_norm_w", "final_norm_w", "q_norm_w", "kv_norm_w", "gdn_norm_w",
         "mla_out_norm_w", "a_log", "dt_bias")
WEIGHTS = ("attn_norm_w", "w_in", "conv_w", "a_log", "dt_bias", "gdn_norm_w", "q_norm_w", "w_uq",
           "kv_norm_w", "w_ukv", "mla_out_norm_w", "w_out", "ffn_norm_w", "w_gate", "w_up", "w_down",
           "final_norm_w")


def _pack_small(vecs):
    flat = jnp.concatenate([v.astype(F32).reshape(-1) for v in vecs])
    pad = (-flat.shape[0]) % (8 * LANE)
    return jnp.concatenate([flat, jnp.zeros((pad,), F32)]).reshape(-1, LANE)


def kernel(x, positions, attn_norm_w, w_in, conv_w, a_log, dt_bias, gdn_norm_w, q_norm_w, w_uq, kv_norm_w, w_ukv, mla_out_norm_w, w_out, ffn_norm_w, w_gate, w_up, w_down, final_norm_w, loss_target, m_attn_norm_w, m_w_in, m_conv_w, m_a_log, m_dt_bias, m_gdn_norm_w, m_q_norm_w, m_w_uq, m_kv_norm_w, m_w_ukv, m_mla_out_norm_w, m_w_out, m_ffn_norm_w, m_w_gate, m_w_up, m_w_down, m_final_norm_w, v_attn_norm_w, v_w_in, v_conv_w, v_a_log, v_dt_bias, v_gdn_norm_w, v_q_norm_w, v_w_uq, v_kv_norm_w, v_w_ukv, v_mla_out_norm_w, v_w_out, v_ffn_norm_w, v_w_gate, v_w_up, v_w_down, v_final_norm_w):
    args = dict(locals())
    xi, yi, ci = lax.axis_index("x"), lax.axis_index("y"), lax.axis_index("c")
    chip = 2 * xi + yi

    def two_d(a):
        return a.reshape(a.shape[-2:]) if a.ndim >= 2 else a.reshape(1, -1)

    wloc = {n: two_d(args[n]) for n in WEIGHTS}
    mloc = {n: two_d(args["m_" + n]) for n in WEIGHTS}
    vloc = {n: two_d(args["v_" + n]) for n in WEIGHTS}

    cw = wloc["conv_w"]
    cshard = cw.shape[1]
    cfull = jnp.zeros((CONV, 4 * cshard), F32)
    cfull = lax.dynamic_update_slice(cfull, jnp.where(ci == 0, cw, 0.0), (0, chip * cshard))
    conv_full = _small_allreduce(_pack_small([cfull]), "gather_conv_w").reshape(-1)[:CONV * 4 * cshard]
    conv_full = conv_full.reshape(CONV, 4 * cshard)

    small = {n: wloc[n] for n in SMALL}
    small["conv_w"] = conv_full

    pos = positions.reshape(-1, 1).astype(F32)
    place = jnp.stack([ci, chip]).astype(jnp.int32)
    loss, grad_x, totals, from_sib, gs = _step(
        two_d(x), pos, two_d(loss_target), wloc["w_in"].astype(BF16), [wloc[n].astype(BF16) for n in ATTN_W[1:]],
        [wloc[n].astype(BF16) for n in FFN_W], small, place)

    small_names = SMALL + ("conv_w",)
    pk = _pack_small([gs[n] for n in small_names] + [loss])
    red = _small_allreduce(pk, "reduce_small").reshape(-1)
    gsm, off = {}, 0
    for n in small_names:
        shp = gs[n].shape
        gsm[n] = red[off:off + shp[0] * shp[1]].reshape(shp)
        off += shp[0] * shp[1]
    loss_out = red[off]
    gsm["conv_w"] = lax.dynamic_slice(gsm["conv_w"], (0, chip * cshard), (CONV, cshard))

    grads, deltas, new_m, new_v = {}, {}, {}, {}
    for n, mine, theirs in zip(BIG, totals, from_sib):
        grads[n], deltas[n], new_m[n], new_v[n] = _adamw_halves(wloc[n], mine, theirs, mloc[n], vloc[n], place,
                                                                "adamw_" + n)
    grads["conv_w"] = gsm["conv_w"]
    deltas["conv_w"], new_m["conv_w"], new_v["conv_w"] = _adamw(wloc["conv_w"], gsm["conv_w"], mloc["conv_w"],
                                                                vloc["conv_w"], "adamw_conv_w")
    sm_shapes = [wloc[n].shape for n in SMALL]
    pd, pm, pv = _adamw(_pack_small([wloc[n] for n in SMALL]), _pack_small([gsm[n] for n in SMALL]),
                        _pack_small([mloc[n] for n in SMALL]), _pack_small([vloc[n] for n in SMALL]),
                        "adamw_small")
    for dst, packed in ((deltas, pd), (new_m, pm), (new_v, pv)):
        flat, off = packed.reshape(-1), 0
        for n, shp in zip(SMALL, sm_shapes):
            dst[n] = flat[off:off + shp[0] * shp[1]].reshape(shp)
            off += shp[0] * shp[1]
    for n in SMALL:
        grads[n] = gsm[n]

    def like(n, a):
        return a.reshape(args[n].shape)

    outs = [loss_out.reshape(()), grad_x.reshape(x.shape)]
    for group in (grads, deltas, new_m, new_v):
        outs += [like(n, group[n]) for n in WEIGHTS]
    return tuple(outs)
```

```python
import jax
import jax.numpy as jnp
from jax import lax
from jax.experimental import pallas as pl
from jax.experimental.pallas import tpu as pltpu

F32, BF16 = jnp.float32, jnp.bfloat16
SDS = jax.ShapeDtypeStruct
MESH = pl.DeviceIdType.MESH

HEAD = 128
ROPE = 64
CHUNK = 64
PAIR = 2 * CHUNK
CONV = 4
EPS = 1e-6
ROPE_THETA = 10000.0
LANE = 128
B_LANE = 64
A_LANE = 72
VMEM_LIMIT = 48 * 1024 * 1024
VMEM_LIMIT_WIDE = 56 * 1024 * 1024
MLA_BLOCK = 512
LOG2E = 1.4426950408889634
LN2 = 0.6931471805599453
SM_SCALE = (HEAD + ROPE) ** -0.5

ADAM_LR = 0.001
ADAM_B1 = 0.9
ADAM_B2 = 0.999
ADAM_EPS = 1e-08
ADAM_WD = 0.01
ADAM_STEP = 10


def _tile(n, pref, mult=LANE):
    if n <= pref:
        return n
    t = (pref // mult) * mult
    while t >= mult:
        if n % t == 0:
            return t
        t -= mult
    return n


def _pcall(body, *, name, grid, in_specs, out_specs, out_shape, scratch=(), vmem=VMEM_LIMIT, aliases=None):
    return pl.pallas_call(
        body, name=name, grid=grid, in_specs=in_specs, out_specs=out_specs,
        out_shape=out_shape, scratch_shapes=list(scratch), input_output_aliases=aliases or {},
        compiler_params=pltpu.CompilerParams(
            dimension_semantics=("arbitrary",) * len(grid), vmem_limit_bytes=vmem))


def _pcall_riding(core, rider, *, name, grid, in_specs, out_specs, out_shape, args, scratch=()):
    n_in, n_out, n_scr = len(in_specs), len(out_specs), len(scratch)
    r_in, r_out = len(rider.arrays), len(rider.out_shapes)

    def body(*refs):
        ins, refs = refs[:n_in], refs[n_in:]
        r_ins, refs = refs[:r_in], refs[r_in:]
        outs, refs = refs[:n_out], refs[n_out:]
        r_outs, refs = refs[:r_out], refs[r_out:]
        scr, sems = refs[:n_scr], refs[n_scr:]
        r_refs = (r_ins, r_outs, sems)
        _ride_begin(rider, r_refs, pl.program_id(0))
        core(*ins, *outs, *scr)
        _ride_end(rider, r_refs, pl.program_id(0), grid[0])

    res = _pcall(body, name=name, grid=grid, in_specs=list(in_specs) + [_ANY] * r_in,
                 out_specs=list(out_specs) + [_ANY] * r_out, out_shape=list(out_shape) + rider.out_shapes,
                 scratch=list(scratch) + rider.scratch)(*args, *rider.arrays)
    return res[:n_out], res[n_out:]


def _rows(ts, width, col=0):
    return pl.BlockSpec((ts, width), lambda i: (i, col))


def _full(shape):
    nd = len(shape)
    return pl.BlockSpec(shape, lambda i: (0,) * nd)


def _dot(a, b):
    return jnp.dot(a.astype(BF16), b.astype(BF16), preferred_element_type=F32)


def _dot_nt(a, b):
    return lax.dot_general(a.astype(BF16), b.astype(BF16), (((1,), (1,)), ((), ())),
                           preferred_element_type=F32)


def _dot_tn(a, b):
    return lax.dot_general(a.astype(BF16), b.astype(BF16), (((0,), (0,)), ((), ())),
                           preferred_element_type=F32)


def _sigmoid(x):
    return 1.0 / (1.0 + jnp.exp(-x))


def _silu(x):
    return x * _sigmoid(x)


def _dsilu(x):
    s = _sigmoid(x)
    return s * (1.0 + x * (1.0 - s))


def _lane_iota(shape):
    return lax.broadcasted_iota(jnp.int32, shape, len(shape) - 1)


def _col(block, idx):
    return jnp.sum(jnp.where(_lane_iota(block.shape) == idx, block, 0.0), axis=-1, keepdims=True)


def _mm(pairs, *, name, ta=False, tb=False, out_dtype=F32, res=None, tm=1024, tn=1024, tk=2048,
        b_chips=False, out_chips=False, rider=None):
    a0, b0 = pairs[0]
    if ta:
        kdim, m = a0.shape
    else:
        m, kdim = a0.shape
    if b_chips and tb:
        n, tk = b0.shape[1], b0.shape[2]
        assert kdim == 4 * tk
    elif b_chips:
        n, tn = 4 * b0.shape[2], b0.shape[2]
        assert kdim == b0.shape[1]
    else:
        n = b0.shape[0] if tb else b0.shape[1]
    if out_chips:
        tn = n // 4
    tm = _tile(m, tm)
    tn = tn if (out_chips or (b_chips and not tb)) else _tile(n, tn)
    tk = tk if (b_chips and tb) else _tile(kdim, tk)
    assert m % tm == 0 and n % tn == 0 and kdim % tk == 0
    nk, npair = kdim // tk, len(pairs)
    grid = (m // tm, n // tn, nk)
    dims = (((0 if ta else 1,), (1 if tb else 0,)), ((), ()))
    n_in = 2 * npair + (res is not None)
    r_in, r_out = (len(rider.arrays), len(rider.out_shapes)) if rider else (0, 0)

    def body(*refs):
        o_ref = refs[n_in + r_in]
        acc = refs[n_in + r_in + 1 + r_out]
        k = pl.program_id(2)
        if rider:
            r_refs = (refs[n_in:n_in + r_in], refs[n_in + r_in + 1:n_in + r_in + 1 + r_out],
                      refs[n_in + r_in + 2 + r_out:])
            step = (pl.program_id(0) * grid[1] + pl.program_id(1)) * nk + k
            _ride_begin(rider, r_refs, step)

        @pl.when(k == 0)
        def _():
            acc[...] = jnp.zeros_like(acc)

        tot = None
        for p in range(npair):
            d = lax.dot_general(refs[2 * p][...].astype(BF16), refs[2 * p + 1][...].astype(BF16),
                                dims, preferred_element_type=F32)
            tot = d if tot is None else tot + d
        acc[...] += tot

        @pl.when(k == nk - 1)
        def _():
            r = acc[...]
            if res is not None:
                r = r + refs[2 * npair][...]
            o_ref[...] = r.astype(out_dtype)

        if rider:
            _ride_end(rider, r_refs, step, grid[0] * grid[1] * nk)

    if ta:
        a_spec = pl.BlockSpec((tk, tm), lambda i, j, k: (k, i))
    else:
        a_spec = pl.BlockSpec((tm, tk), lambda i, j, k: (i, k))
    if b_chips and tb:
        b_spec = pl.BlockSpec((None, tn, tk), lambda i, j, k: (k, j, 0))
    elif b_chips:
        b_spec = pl.BlockSpec((None, tk, tn), lambda i, j, k: (j, k, 0))
    elif tb:
        b_spec = pl.BlockSpec((tn, tk), lambda i, j, k: (j, k))
    else:
        b_spec = pl.BlockSpec((tk, tn), lambda i, j, k: (k, j))
    if out_chips:
        o_spec = pl.BlockSpec((None, tm, tn), lambda i, j, k: (j, i, 0))
        o_shape = SDS((4, m, tn), out_dtype)
    else:
        o_spec = pl.BlockSpec((tm, tn), lambda i, j, k: (i, j))
        o_shape = SDS((m, n), out_dtype)
    in_specs, args = [], []
    for a, b in pairs:
        in_specs += [a_spec, b_spec]
        args += [a, b]
    if res is not None:
        in_specs.append(o_spec)
        args.append(res)
    out_specs, out_shapes, scratch = [o_spec], [o_shape], [pltpu.VMEM((tm, tn), F32)]
    if rider:
        in_specs += [_ANY] * r_in
        args += rider.arrays
        out_specs += [_ANY] * r_out
        out_shapes += rider.out_shapes
        scratch += rider.scratch
    outs = _pcall(body, name=name, grid=grid, in_specs=in_specs, out_specs=out_specs, out_shape=out_shapes,
                  scratch=scratch)(*args)
    return (outs[0], outs[1:]) if rider else outs[0]


def _norm_fwd(x, w, name, rider=None):
    s, d = x.shape
    ts = _tile(s, 512, 8)

    def body(x_ref, w_ref, h_ref):
        xv = x_ref[...]
        r = lax.rsqrt(jnp.mean(xv * xv, axis=-1, keepdims=True) + EPS)
        h_ref[...] = (xv * r * w_ref[...]).astype(BF16)

    spec = dict(name=name, grid=(s // ts,), in_specs=[_rows(ts, d), _full((1, d))])
    if rider is None:
        return _pcall(body, out_specs=_rows(ts, d), out_shape=SDS((s, d), BF16), **spec)(x, w)
    outs, r_outs = _pcall_riding(body, rider, out_specs=[_rows(ts, d)], out_shape=[SDS((s, d), BF16)],
                                 args=(x, w), **spec)
    return outs[0], r_outs


def _norm_bwd(dh, x, w, dres, name, with_bf16):
    s, d = x.shape
    ts = _tile(s, 256, 8)

    def body(dh_ref, x_ref, w_ref, dres_ref, dx_ref, dw_ref, *dx16_ref):
        @pl.when(pl.program_id(0) == 0)
        def _():
            dw_ref[...] = jnp.zeros_like(dw_ref)

        xv, dhv = x_ref[...], dh_ref[...]
        r = lax.rsqrt(jnp.mean(xv * xv, axis=-1, keepdims=True) + EPS)
        xh = xv * r
        dw_ref[...] += jnp.sum(dhv * xh, axis=0, keepdims=True)
        dxh = dhv * w_ref[...]
        dx = dres_ref[...] + r * (dxh - xh * jnp.mean(dxh * xh, axis=-1, keepdims=True))
        dx_ref[...] = dx
        for ref in dx16_ref:
            ref[...] = dx.astype(BF16)

    extra = 1 if with_bf16 else 0
    return _pcall(body, name=name, grid=(s // ts,),
                  in_specs=[_rows(ts, d), _rows(ts, d), _full((1, d)), _rows(ts, d)],
                  out_specs=[_rows(ts, d), _full((1, d))] + [_rows(ts, d)] * extra,
                  out_shape=[SDS((s, d), F32), SDS((1, d), F32)] + [SDS((s, d), BF16)] * extra)(
                      dh, x, w, dres)


def _final_loss(x3, tgt, w):
    s, d = x3.shape
    ts = _tile(s, 256, 8)

    def body(x_ref, t_ref, w_ref, dx_ref, dw_ref, loss_ref, dx16_ref):
        @pl.when(pl.program_id(0) == 0)
        def _():
            dw_ref[...] = jnp.zeros_like(dw_ref)
            loss_ref[...] = jnp.zeros_like(loss_ref)

        xv, wv = x_ref[...], w_ref[...]
        r = lax.rsqrt(jnp.mean(xv * xv, axis=-1, keepdims=True) + EPS)
        xh = xv * r
        err = xh * wv - t_ref[...]
        row = jnp.mean(err * err, axis=-1, keepdims=True)
        loss_ref[...] += 0.5 * jnp.sum(row, axis=0, keepdims=True)
        dy = err * (1.0 / d)
        dw_ref[...] += jnp.sum(dy * xh, axis=0, keepdims=True)
        dxh = dy * wv
        dx = r * (dxh - xh * jnp.mean(dxh * xh, axis=-1, keepdims=True))
        dx_ref[...] = dx
        dx16_ref[...] = dx.astype(BF16)

    return _pcall(body, name="final_loss", grid=(s // ts,),
                  in_specs=[_rows(ts, d), _rows(ts, d), _full((1, d))],
                  out_specs=[_rows(ts, d), _full((1, d)), _full((1, 1)), _rows(ts, d)],
                  out_shape=[SDS((s, d), F32), SDS((1, d), F32), SDS((1, 1), F32), SDS((s, d), BF16)])(
                      x3, tgt, w)


def _shift_down(cur, halo, s):
    if s == 0:
        return cur
    row8 = lax.broadcasted_iota(jnp.int32, halo.shape, 0)
    r = pltpu.roll(cur, s, 0)
    top = jnp.where(row8 < s, pltpu.roll(halo, s, 0), r[0:8])
    return jnp.concatenate([top, r[8:]], axis=0)


def _shift_up(cur, halo, s):
    if s == 0:
        return cur
    ts = cur.shape[0]
    row8 = lax.broadcasted_iota(jnp.int32, halo.shape, 0)
    r = pltpu.roll(cur, ts - s, 0)
    bot = jnp.where(row8 >= 8 - s, pltpu.roll(halo, 8 - s, 0), r[ts - 8:ts])
    return jnp.concatenate([r[:ts - 8], bot], axis=0)


def _chunk_tri(ts, upper):
    i = lax.broadcasted_iota(jnp.int32, (ts, ts), 0)
    j = lax.broadcasted_iota(jnp.int32, (ts, ts), 1)
    same = jnp.right_shift(i, 6) == jnp.right_shift(j, 6)
    return jnp.where(same & ((j >= i) if upper else (j <= i)), 1.0, 0.0).astype(F32)


def _gate_values(m, alog, dtb):
    lane = _lane_iota(m.shape)
    beta = _sigmoid(m)
    xg = m + dtb
    sp = jnp.maximum(xg, 0.0) + jnp.log(1.0 + jnp.exp(-jnp.abs(xg)))
    ga = (lane >= A_LANE) & (lane < A_LANE + 8)
    g = jnp.where(ga, -jnp.exp(alog) * sp, 0.0)
    return beta, g, xg, ga


def _l2_heads(a, nh, scale):
    outs, rs = [], []
    for h in range(nh):
        ah = a[:, HEAD * h:HEAD * (h + 1)]
        r = lax.rsqrt(jnp.sum(ah * ah, axis=-1, keepdims=True) + EPS)
        outs.append(ah * (r * scale))
        rs.append(r)
    return jnp.concatenate(outs, axis=-1), rs


def _gdn_prep(proj, conv_w, alog_l, dtb_l, nh, misc_col):
    s = proj.shape[0]
    w = nh * HEAD
    ts = _tile(s, 256, PAIR)
    hb = ts // 8

    def body(cur_ref, halo_ref, misc_ref, cw_ref, al_ref, db_ref, q_ref, k_ref, v_ref, gb_ref, gbt_ref):
        first = pl.program_id(0) == 0
        outs = (q_ref, k_ref, v_ref)
        for sec in range(3):
            cs = slice(sec * w, (sec + 1) * w)
            cur = cur_ref[:, cs]
            halo = jnp.where(first, 0.0, halo_ref[:, cs])
            pre = None
            for j in range(CONV):
                term = cw_ref[j:j + 1, cs] * _shift_down(cur, halo, CONV - 1 - j)
                pre = term if pre is None else pre + term
            act = _silu(pre)
            if sec == 0:
                act, _ = _l2_heads(act, nh, HEAD ** -0.5)
            elif sec == 1:
                act, _ = _l2_heads(act, nh, 1.0)
            outs[sec][...] = act
        m = misc_ref[...]
        lane = _lane_iota(m.shape)
        beta, g, _, ga = _gate_values(m, al_ref[...], db_ref[...])
        gcc = jnp.dot(_chunk_tri(ts, False), g, precision=lax.Precision.HIGHEST,
                      preferred_element_type=F32)
        gb = jnp.where((lane >= B_LANE) & (lane < B_LANE + 8), beta, jnp.where(ga, gcc, 0.0))
        gb_ref[...] = gb
        gbt_ref[...] = gb.T

    return _pcall(
        body, name="gdn_prep", grid=(s // ts,),
        in_specs=[_rows(ts, 3 * w),
                  pl.BlockSpec((8, 3 * w), lambda i: (jnp.maximum(i * hb - 1, 0), 0)),
                  _rows(ts, LANE, misc_col), _full((CONV, 3 * w)), _full((1, LANE)), _full((1, LANE))],
        out_specs=[_rows(ts, w), _rows(ts, w), _rows(ts, w), _rows(ts, LANE),
                   pl.BlockSpec((LANE, ts), lambda i: (0, i))],
        out_shape=[SDS((s, w), F32), SDS((s, w), F32), SDS((s, w), F32), SDS((s, LANE), F32),
                   SDS((LANE, s), F32)])(proj, proj, proj, conv_w, alog_l, dtb_l)


def _gdn_prep_bwd(proj, conv_w, alog_l, dtb_l, dq, dk, dv, dgb, dkr, dproj, nh, misc_col):
    s = proj.shape[0]
    w = nh * HEAD
    ts = _tile(s, 256, PAIR)
    hb = ts // 8
    assert misc_col % 2 == 0

    def body(cur_ref, halo_ref, misc_ref, cw_ref, al_ref, db_ref, dq_ref, dk_ref, dv_ref, dgb_ref,
             dkr_ref, _, dc_ref, dm_ref, dcw_ref, dal_ref, ddb_ref):
        first = pl.program_id(0) == 0

        @pl.when(first)
        def _():
            dcw_ref[...] = jnp.zeros_like(dcw_ref)
            dal_ref[...] = jnp.zeros_like(dal_ref)
            ddb_ref[...] = jnp.zeros_like(ddb_ref)

        dins = (dq_ref, dk_ref, dv_ref)
        for sec in range(3):
            cs = slice(sec * w, (sec + 1) * w)
            cur = cur_ref[:, cs]
            halo = jnp.where(first, 0.0, halo_ref[:, cs])
            us = [_shift_down(cur, halo, CONV - 1 - j) for j in range(CONV)]
            pre = None
            for j in range(CONV):
                term = cw_ref[j:j + 1, cs] * us[j]
                pre = term if pre is None else pre + term
            act = _silu(pre)
            dout = dins[sec][...]
            if sec < 2:
                scale = HEAD ** -0.5 if sec == 0 else 1.0
                parts = []
                for h in range(nh):
                    hs = slice(HEAD * h, HEAD * (h + 1))
                    ah = act[:, hs]
                    r = lax.rsqrt(jnp.sum(ah * ah, axis=-1, keepdims=True) + EPS)
                    ahat = ah * r
                    dy = dout[:, hs]
                    parts.append((scale * r) * (dy - ahat * jnp.sum(dy * ahat, axis=-1, keepdims=True)))
                dact = jnp.concatenate(parts, axis=-1)
            else:
                dact = dout
            dconv = dact * _dsilu(pre)
            dc_ref[:, cs] = dconv
            for j in range(CONV):
                dcw_ref[j:j + 1, cs] += jnp.sum(dconv * us[j], axis=0, keepdims=True)
        m = misc_ref[...]
        lane = _lane_iota(m.shape)
        al = al_ref[...]
        beta, g, xg, ga = _gate_values(m, al, db_ref[...])
        dgbv = dgb_ref[...]
        dg = jnp.dot(_chunk_tri(ts, True), jnp.where(ga, dgbv, 0.0), precision=lax.Precision.HIGHEST,
                     preferred_element_type=F32)
        da_raw = jnp.where(ga, dg * (-jnp.exp(al)) * _sigmoid(xg), 0.0)
        db_raw = jnp.where((lane >= B_LANE) & (lane < B_LANE + 8), dgbv * beta * (1.0 - beta), 0.0)
        dal_ref[...] += jnp.sum(dg * g, axis=0, keepdims=True)
        ddb_ref[...] += jnp.sum(da_raw, axis=0, keepdims=True)
        dm_ref[:, :LANE] = (dkr_ref[...] + da_raw + db_raw).astype(BF16)
        dm_ref[:, LANE:] = jnp.zeros((ts, LANE), BF16)

    return _pcall(
        body, name="gdn_prep_bwd", grid=(s // ts,),
        in_specs=[_rows(ts, 3 * w),
                  pl.BlockSpec((8, 3 * w), lambda i: (jnp.maximum(i * hb - 1, 0), 0)),
                  _rows(ts, LANE, misc_col), _full((CONV, 3 * w)), _full((1, LANE)), _full((1, LANE)),
                  _rows(ts, w), _rows(ts, w), _rows(ts, w), _rows(ts, LANE), _rows(ts, LANE), _ANY],
        out_specs=[_rows(ts, 3 * w), _rows(ts, 2 * LANE, misc_col // 2), _full((CONV, 3 * w)), _full((1, LANE)),
                   _full((1, LANE))],
        out_shape=[SDS((s, 3 * w), F32), SDS(dproj.shape, BF16), SDS((CONV, 3 * w), F32),
                   SDS((1, LANE), F32), SDS((1, LANE), F32)], aliases={11: 1})(
                       proj, proj, proj, conv_w, alog_l, dtb_l, dq, dk, dv, dgb, dkr, dproj)


def _conv_bwd_input(dconv, conv_w, dproj):
    s, c = dconv.shape
    ts = _tile(s, 256, 8)
    hb = ts // 8
    nblk8 = s // 8
    nt = s // ts

    def body(cur_ref, nxt_ref, cw_ref, _, o_ref):
        last = pl.program_id(0) == nt - 1
        cur = cur_ref[...]
        halo = jnp.where(last, 0.0, nxt_ref[...])
        acc = None
        for j in range(CONV):
            term = cw_ref[j:j + 1, :] * _shift_up(cur, halo, CONV - 1 - j)
            acc = term if acc is None else acc + term
        o_ref[...] = acc.astype(BF16)

    return _pcall(
        body, name="conv_bwd_input", grid=(nt,),
        in_specs=[_rows(ts, c),
                  pl.BlockSpec((8, c), lambda i: (jnp.minimum((i + 1) * hb, nblk8 - 1), 0)),
                  _full((CONV, c)), _ANY],
        out_specs=_rows(ts, c), out_shape=SDS(dproj.shape, BF16), aliases={3: 0})(
            dconv, dconv, conv_w, dproj)


def _inv_unit_lower(a):
    n = a[0].shape[0]
    i = lax.broadcasted_iota(jnp.int32, (n, n), 0)
    j = lax.broadcasted_iota(jnp.int32, (n, n), 1)
    eye = jnp.where(i == j, 1.0, 0.0)
    t = [eye - ah for ah in a]
    x = a
    for _ in range(5):
        x = [_dot(xh, xh) for xh in x]
        t = [th + _dot(th, xh) for th, xh in zip(t, x)]
    return t


def _pair_common(q, k, gcol, grow, bcol):
    i = lax.broadcasted_iota(jnp.int32, (PAIR, PAIR), 0)
    j = lax.broadcasted_iota(jnp.int32, (PAIR, PAIR), 1)
    same = jnp.right_shift(i, 6) == jnp.right_shift(j, 6)
    tril = same & (i >= j)
    strict = same & (i > j)
    dec = [jnp.where(tril, jnp.exp(jnp.minimum(gc - gr, 0.0)), 0.0) for gc, gr in zip(gcol, grow)]
    kk = [_dot_nt(kh, kh) for kh in k]
    qk = [_dot_nt(qh, kh) for qh, kh in zip(q, k)]
    a = [jnp.where(strict, b * kkh * d, 0.0) for b, kkh, d in zip(bcol, kk, dec)]
    t = _inv_unit_lower(a)
    p = [qkh * d for qkh, d in zip(qk, dec)]
    return dec, kk, a, t, p, tril, strict


def _ext(v, a):
    z = jnp.zeros_like(v)
    return jnp.concatenate([v, z] if a == 0 else [z, v], axis=0)


def _gdn_fwd(q, k, v, gb, gbt, nh, rider):
    s = q.shape[0]
    w = nh * HEAD
    npair = s // PAIR

    def body(q_ref, k_ref, v_ref, gb_ref, gbt_ref, o_ref, st_ref, s_ref):
        @pl.when(pl.program_id(0) == 0)
        def _():
            s_ref[...] = jnp.zeros_like(s_ref)

        heads = range(nh)
        hs = [slice(HEAD * h, HEAD * (h + 1)) for h in heads]
        gbv = gb_ref[...]
        q, k, v = [q_ref[:, s_] for s_ in hs], [k_ref[:, s_] for s_ in hs], [v_ref[:, s_] for s_ in hs]
        gcol = [_col(gbv, A_LANE + h) for h in heads]
        bcol = [_col(gbv, B_LANE + h) for h in heads]
        grow = [gbt_ref[A_LANE + h:A_LANE + h + 1, :] for h in heads]
        _, _, _, t, p, _, _ = _pair_common(q, k, gcol, grow, bcol)
        eg = [jnp.exp(gc) for gc in gcol]
        qg = [x * e for x, e in zip(q, eg)]
        kg = [x * e for x, e in zip(k, eg)]
        outs = []
        for a in range(2):
            sl = slice(CHUNK * a, CHUNK * (a + 1))
            st = [s_ref[h] for h in heads]
            for h in heads:
                st_ref[a, h] = st[h]
            r = [v[h][sl] - _dot(kg[h][sl], st[h]) for h in heads]
            vn = [_dot(t[h][sl], _ext(bcol[h][sl] * r[h], a)) for h in heads]
            outs.append([_dot(qg[h][sl], st[h]) + _dot(p[h][sl], _ext(vn[h], a)) for h in heads])
            gl = [_col(gr, CHUNK * (a + 1) - 1) for gr in grow]
            kd = [k[h][sl] * jnp.exp(gl[h] - gcol[h][sl]) for h in heads]
            upd = [_dot_tn(kd[h], vn[h]) for h in heads]
            for h in heads:
                s_ref[h] = jnp.exp(gl[h]) * st[h] + upd[h]
        for h in heads:
            o_ref[:, hs[h]] = jnp.concatenate([outs[0][h], outs[1][h]], axis=0)

    return _pcall_riding(
        body, rider, name="gdn_fwd", grid=(npair,),
        in_specs=[_rows(PAIR, w), _rows(PAIR, w), _rows(PAIR, w), _rows(PAIR, LANE),
                  pl.BlockSpec((LANE, PAIR), lambda i: (0, i))],
        out_specs=[_rows(PAIR, w), pl.BlockSpec((2, nh, HEAD, HEAD), lambda i: (i, 0, 0, 0))],
        out_shape=[SDS((s, w), F32), SDS((2 * npair, nh, HEAD, HEAD), F32)],
        scratch=[pltpu.VMEM((nh, HEAD, HEAD), F32)], args=(q, k, v, gb, gbt))


def _gdn_bwd(q, k, v, gb, gbt, states, do, nh, rider):
    s = q.shape[0]
    w = nh * HEAD
    npair = s // PAIR
    rev = lambda i: (npair - 1 - i, 0)

    def body(q_ref, k_ref, v_ref, gb_ref, gbt_ref, st_ref, do_ref, dq_ref, dk_ref, dv_ref, dgb_ref,
             ds_ref):
        @pl.when(pl.program_id(0) == 0)
        def _():
            ds_ref[...] = jnp.zeros_like(ds_ref)

        lane = _lane_iota((PAIR, LANE))
        row = lax.broadcasted_iota(jnp.int32, (CHUNK, 1), 0)
        heads = range(nh)
        hs = [slice(HEAD * h, HEAD * (h + 1)) for h in heads]
        gbv = gb_ref[...]
        q, k, v = [q_ref[:, s_] for s_ in hs], [k_ref[:, s_] for s_ in hs], [v_ref[:, s_] for s_ in hs]
        do = [do_ref[:, s_] for s_ in hs]
        gcol = [_col(gbv, A_LANE + h) for h in heads]
        bcol = [_col(gbv, B_LANE + h) for h in heads]
        grow = [gbt_ref[A_LANE + h:A_LANE + h + 1, :] for h in heads]
        dec, kk, amat, t, p, tril, strict = _pair_common(q, k, gcol, grow, bcol)
        tt, pt = [x.T for x in t], [x.T for x in p]
        eg = [jnp.exp(gc) for gc in gcol]
        qg = [x * e for x, e in zip(q, eg)]
        kg = [x * e for x, e in zip(k, eg)]
        sums = lambda x: jnp.sum(x, axis=-1, keepdims=True)
        rs, vns = [None, None], [None, None]
        for a in range(2):
            sl = slice(CHUNK * a, CHUNK * (a + 1))
            rs[a] = [v[h][sl] - _dot(kg[h][sl], st_ref[a, h]) for h in heads]
            vns[a] = [_dot(t[h][sl], _ext(bcol[h][sl] * rs[a][h], a)) for h in heads]
        dsn = [ds_ref[h] for h in heads]
        dqs, dks, dvs, dgcs, dbs, drbs = ([None, None] for _ in range(6))
        for a in (1, 0):
            sl = slice(CHUNK * a, CHUNK * (a + 1))
            st = [st_ref[a, h] for h in heads]
            gl = [_col(gr, CHUNK * (a + 1) - 1) for gr in grow]
            egl = [jnp.exp(x) for x in gl]
            dk_dec = [jnp.exp(gl[h] - gcol[h][sl]) for h in heads]
            kd = [k[h][sl] * dk_dec[h] for h in heads]
            d_vn = [_dot(pt[h][sl], _ext(do[h][sl], a)) + _dot(kd[h], dsn[h]) for h in heads]
            d_qg = [_dot_nt(do[h][sl], st[h]) for h in heads]
            d_rb = [_dot(tt[h][sl], _ext(d_vn[h], a)) for h in heads]
            d_r = [bcol[h][sl] * d_rb[h] for h in heads]
            d_kg = [-_dot_nt(d_r[h], st[h]) for h in heads]
            d_kd = [_dot_nt(vns[a][h], dsn[h]) for h in heads]
            dsn_new = [_dot_tn(qg[h][sl], do[h][sl]) - _dot_tn(kg[h][sl], d_r[h]) for h in heads]
            dbs[a] = [sums(d_rb[h] * rs[a][h]) for h in heads]
            dgl = [egl[h] * jnp.sum(dsn[h] * st[h], keepdims=True) + jnp.sum(d_kd[h] * kd[h], keepdims=True)
                   for h in heads]
            dgcs[a] = [sums(d_qg[h] * qg[h][sl]) + sums(d_kg[h] * kg[h][sl]) - sums(d_kd[h] * kd[h])
                       + jnp.where(row == CHUNK - 1, dgl[h], 0.0) for h in heads]
            dqs[a] = [d_qg[h] * eg[h][sl] for h in heads]
            dks[a] = [d_kg[h] * eg[h][sl] + d_kd[h] * dk_dec[h] for h in heads]
            dvs[a] = d_r
            drbs[a] = d_rb
            dsn = [dsn_new[h] + egl[h] * dsn[h] for h in heads]
        for h in heads:
            ds_ref[h] = dsn[h]
        cat = lambda xs, h: jnp.concatenate([xs[0][h], xs[1][h]], axis=0)
        vn = [cat(vns, h) for h in heads]
        d_rb = [cat(drbs, h) for h in heads]
        dp = [jnp.where(tril, _dot_nt(do[h], vn[h]), 0.0) for h in heads]
        dam = [jnp.where(strict, -_dot_nt(d_rb[h], vn[h]), 0.0) for h in heads]
        g_p = [dp[h] * dec[h] for h in heads]
        g_a = [dam[h] * dec[h] for h in heads]
        gbk = [bcol[h] * g_a[h] for h in heads]
        dq2 = [_dot(g_p[h], k[h]) for h in heads]
        dk2 = [_dot_tn(g_p[h], q[h]) + _dot(gbk[h], k[h]) + _dot_tn(gbk[h], k[h]) for h in heads]
        dgb = jnp.zeros((PAIR, LANE), F32)
        for h in heads:
            dq_ref[:, hs[h]] = cat(dqs, h) + dq2[h]
            dk_ref[:, hs[h]] = cat(dks, h) + dk2[h]
            dv_ref[:, hs[h]] = cat(dvs, h)
            dbeta = cat(dbs, h) + sums(g_a[h] * kk[h])
            mm = dp[h] * p[h] + dam[h] * amat[h]
            dgc = cat(dgcs, h) + sums(mm) - sums(mm.T)
            dgb = dgb + jnp.where(lane == A_LANE + h, dgc, 0.0) + jnp.where(lane == B_LANE + h, dbeta, 0.0)
        dgb_ref[...] = dgb

    return _pcall_riding(
        body, rider, name="gdn_bwd", grid=(npair,),
        in_specs=[pl.BlockSpec((PAIR, w), rev), pl.BlockSpec((PAIR, w), rev), pl.BlockSpec((PAIR, w), rev),
                  pl.BlockSpec((PAIR, LANE), rev),
                  pl.BlockSpec((LANE, PAIR), lambda i: (0, npair - 1 - i)),
                  pl.BlockSpec((2, nh, HEAD, HEAD), lambda i: (npair - 1 - i, 0, 0, 0)),
                  pl.BlockSpec((PAIR, w), rev)],
        out_specs=[pl.BlockSpec((PAIR, w), rev), pl.BlockSpec((PAIR, w), rev), pl.BlockSpec((PAIR, w), rev),
                   pl.BlockSpec((PAIR, LANE), rev)],
        out_shape=[SDS((s, w), F32), SDS((s, w), F32), SDS((s, w), F32), SDS((s, LANE), F32)],
        scratch=[pltpu.VMEM((nh, HEAD, HEAD), F32)], args=(q, k, v, gb, gbt, states, do))


def _mla_norm(proj, qw, kvw, col_q, col_kv):
    s = proj.shape[0]
    lr = qw.shape[1]
    ts = _tile(s, 512, 8)

    def body(cq_ref, ckv_ref, qw_ref, kvw_ref, oq_ref, okv_ref):
        for x_ref, w_ref, o_ref in ((cq_ref, qw_ref, oq_ref), (ckv_ref, kvw_ref, okv_ref)):
            xv = x_ref[...]
            r = lax.rsqrt(jnp.mean(xv * xv, axis=-1, keepdims=True) + EPS)
            o_ref[...] = (xv * r * w_ref[...]).astype(BF16)

    return _pcall(body, name="mla_norm", grid=(s // ts,),
                  in_specs=[_rows(ts, lr, col_q), _rows(ts, lr, col_kv), _full((1, lr)), _full((1, lr))],
                  out_specs=[_rows(ts, lr), _rows(ts, lr)],
                  out_shape=[SDS((s, lr), BF16), SDS((s, lr), BF16)])(proj, proj, qw, kvw)


def _mla_norm_bwd(proj, qw, kvw, dq, dkv, dproj, col_q, col_kv):
    s = proj.shape[0]
    lr = qw.shape[1]
    ts = _tile(s, 512, 8)

    assert col_kv == col_q + 1 and col_q % 2 == 0

    def body(cq_ref, ckv_ref, qw_ref, kvw_ref, dq_ref, dkv_ref, _, o_ref, dqw_ref, dkvw_ref):
        @pl.when(pl.program_id(0) == 0)
        def _():
            dqw_ref[...] = jnp.zeros_like(dqw_ref)
            dkvw_ref[...] = jnp.zeros_like(dkvw_ref)

        for k, (x_ref, w_ref, d_ref, dw_ref) in enumerate(((cq_ref, qw_ref, dq_ref, dqw_ref),
                                                           (ckv_ref, kvw_ref, dkv_ref, dkvw_ref))):
            xv, dh = x_ref[...], d_ref[...]
            r = lax.rsqrt(jnp.mean(xv * xv, axis=-1, keepdims=True) + EPS)
            xh = xv * r
            dw_ref[...] += jnp.sum(dh * xh, axis=0, keepdims=True)
            dxh = dh * w_ref[...]
            o_ref[:, lr * k:lr * (k + 1)] = (
                r * (dxh - xh * jnp.mean(dxh * xh, axis=-1, keepdims=True))).astype(BF16)

    return _pcall(body, name="mla_norm_bwd", grid=(s // ts,),
                  in_specs=[_rows(ts, lr, col_q), _rows(ts, lr, col_kv), _full((1, lr)), _full((1, lr)),
                            _rows(ts, lr), _rows(ts, lr), _ANY],
                  out_specs=[_rows(ts, 2 * lr, col_q // 2), _full((1, lr)), _full((1, lr))],
                  out_shape=[SDS(dproj.shape, BF16), SDS((1, lr), F32), SDS((1, lr), F32)],
                  aliases={6: 0})(proj, proj, qw, kvw, dq, dkv, dproj)


def _rope_tables(pos, invf, sgn):
    ang = pos * invf
    return jnp.cos(ang), jnp.sin(ang) * sgn


def _swap_halves_lanes(y):
    lane = _lane_iota(y.shape)
    return jnp.where(lane < ROPE // 2, pltpu.roll(y, LANE - ROPE // 2, 1), pltpu.roll(y, ROPE // 2, 1))


def _rope_consts():
    half = ROPE // 2
    inv = ROPE_THETA ** (-jnp.arange(half, dtype=F32) / half)
    invf = jnp.concatenate([inv, inv, jnp.zeros((LANE - ROPE,), F32)])[None, :]
    sgn = jnp.concatenate([-jnp.ones((half,), F32), jnp.ones((half,), F32),
                           jnp.zeros((LANE - ROPE,), F32)])[None, :]
    return invf, sgn


def _mla_rope(qraw, kvraw, proj, pos, nh, misc_col):
    s = qraw.shape[0]
    ts = _tile(s, MLA_BLOCK)
    wq = nh * 2 * HEAD
    invf, sgn = _rope_consts()

    def body(q_ref, kv_ref, misc_ref, pos_ref, if_ref, sg_ref, qc_ref, kc_ref, v_ref, qt_ref, vt_ref, kt_ref):
        c, sn = _rope_tables(pos_ref[...], if_ref[...], sg_ref[...])
        lane = _lane_iota(c.shape)
        rot = lambda xb: xb * c + _swap_halves_lanes(xb) * sn
        qs = SM_SCALE * LOG2E
        krot32 = jnp.where(lane < ROPE, rot(misc_ref[...]), 0.0)
        krot, krot_t = krot32.astype(BF16), krot32.T.astype(BF16)
        for h in range(nh):
            b0 = 2 * HEAD * h
            qn = q_ref[:, b0:b0 + HEAD].astype(F32) * qs
            qr = rot(q_ref[:, b0 + HEAD:b0 + 2 * HEAD].astype(F32)) * qs
            qc_ref[:, b0:b0 + HEAD] = qn.astype(BF16)
            qc_ref[:, b0 + HEAD:b0 + 2 * HEAD] = qr.astype(BF16)
            qt_ref[b0:b0 + HEAD, :] = qn.T.astype(BF16)
            qt_ref[b0 + HEAD:b0 + 2 * HEAD, :] = qr.T.astype(BF16)
            kn = kv_ref[:, b0:b0 + HEAD]
            kc_ref[:, b0:b0 + HEAD] = kn.astype(BF16)
            kc_ref[:, b0 + HEAD:b0 + 2 * HEAD] = krot
            kt_ref[b0:b0 + HEAD, :] = kn.astype(F32).T.astype(BF16)
            kt_ref[b0 + HEAD:b0 + 2 * HEAD, :] = krot_t
            vh = kv_ref[:, wq + HEAD * h:wq + HEAD * (h + 1)]
            v_ref[:, HEAD * h:HEAD * (h + 1)] = vh.astype(BF16)
            vt_ref[HEAD * h:HEAD * (h + 1), :] = vh.astype(F32).T.astype(BF16)

    return _pcall(body, name="mla_rope", grid=(s // ts,),
                  in_specs=[_rows(ts, wq), _rows(ts, wq + nh * HEAD), _rows(ts, LANE, misc_col),
                            _rows(ts, 1), _full((1, LANE)), _full((1, LANE))],
                  out_specs=[_rows(ts, wq), _rows(ts, wq), _rows(ts, nh * HEAD),
                             pl.BlockSpec((None, wq, ts), lambda i: (i, 0, 0)),
                             pl.BlockSpec((None, nh * HEAD, ts), lambda i: (i, 0, 0)),
                             pl.BlockSpec((None, wq, ts), lambda i: (i, 0, 0))],
                  out_shape=[SDS((s, wq), BF16), SDS((s, wq), BF16), SDS((s, nh * HEAD), BF16),
                             SDS((s // ts, wq, ts), BF16), SDS((s // ts, nh * HEAD, ts), BF16),
                             SDS((s // ts, wq, ts), BF16)])(
                      qraw, kvraw, proj, pos, invf, sgn)


def _mla_rope_bwd(dqt, dkc, dv, pos, nh):
    nb, wq, ts = dqt.shape
    s = nb * ts
    invf, sgn = _rope_consts()

    def body(dq_ref, dk_ref, dv_ref, pos_ref, if_ref, sg_ref, oq_ref, okv_ref, okr_ref):
        c, sn = _rope_tables(pos_ref[...], if_ref[...], sg_ref[...])
        lane = _lane_iota(c.shape)
        unrot = lambda d: d * c + _swap_halves_lanes(d * sn)
        dkr = jnp.zeros(c.shape, F32)
        for h in range(nh):
            b0 = 2 * HEAD * h
            oq_ref[:, b0:b0 + HEAD] = (dq_ref[b0:b0 + HEAD, :].T * SM_SCALE).astype(BF16)
            oq_ref[:, b0 + HEAD:b0 + 2 * HEAD] = (
                unrot(dq_ref[b0 + HEAD:b0 + 2 * HEAD, :].T) * SM_SCALE).astype(BF16)
            okv_ref[:, b0:b0 + HEAD] = (dk_ref[:, b0:b0 + HEAD] * LN2).astype(BF16)
            okv_ref[:, b0 + HEAD:b0 + 2 * HEAD] = jnp.zeros((ts, HEAD), BF16)
            dkr = dkr + dk_ref[:, b0 + HEAD:b0 + 2 * HEAD]
        okv_ref[:, wq:] = dv_ref[...].astype(BF16)
        okr_ref[...] = jnp.where(lane < ROPE, unrot(jnp.where(lane < ROPE, dkr * LN2, 0.0)), 0.0)

    return _pcall(body, name="mla_rope_bwd", grid=(s // ts,),
                  in_specs=[pl.BlockSpec((None, wq, ts), lambda i: (i, 0, 0)), _rows(ts, wq), _rows(ts, nh * HEAD),
                            _rows(ts, 1), _full((1, LANE)), _full((1, LANE))],
                  out_specs=[_rows(ts, wq), _rows(ts, wq + nh * HEAD), _rows(ts, LANE)],
                  out_shape=[SDS((s, wq), BF16), SDS((s, wq + nh * HEAD), BF16), SDS((s, LANE), F32)])(
                      dqt, dkc, dv, pos, invf, sgn)


MLA_HP = 2
MLA_FWD_HP = 4


def _mla_fwd(qt, kc, vt, nh, rider):
    nb, _, blk = qt.shape
    s = nb * blk
    hp = MLA_FWD_HP if nh % MLA_FWD_HP == 0 else MLA_HP
    assert nh % hp == 0 and hp % MLA_HP == 0
    once = pl.Buffered(1)
    r_in, r_out = len(rider.arrays), len(rider.out_shapes)

    def body(*refs):
        qt_ref, k_ref, vt_ref = refs[:3]
        o_ref, lse_ref = refs[3 + r_in:5 + r_in]
        m_sc, l_sc, acc = refs[5 + r_in + r_out:8 + r_in + r_out]
        r_refs = (refs[3:3 + r_in], refs[5 + r_in:5 + r_in + r_out], refs[8 + r_in + r_out:])
        i = pl.program_id(1)
        grid_step = pl.program_id(0) * nb + i
        _ride_begin(rider, r_refs, grid_step)
        m_sc[...] = jnp.full_like(m_sc, -1e30)
        l_sc[...] = jnp.zeros_like(l_sc)
        acc[...] = jnp.zeros_like(acc)
        es = range(hp)

        def step(j, masked):
            rows = pl.ds(pl.multiple_of(j * blk, blk), blk)
            sc = [_dot(k_ref[rows, 2 * HEAD * e:2 * HEAD * (e + 1)], qt_ref[2 * HEAD * e:2 * HEAD * (e + 1), :])
                  for e in es]
            if masked:
                key = lax.broadcasted_iota(jnp.int32, (blk, blk), 0)
                qry = lax.broadcasted_iota(jnp.int32, (blk, blk), 1)
                sc = [jnp.where(key <= qry, x, -1e30) for x in sc]
            m_prev = [m_sc[e] for e in es]
            m_new = [jnp.maximum(m_prev[e], jnp.max(sc[e], axis=0, keepdims=True)) for e in es]
            p = [jnp.exp2(sc[e] - m_new[e]) for e in es]
            alpha = [jnp.exp2(m_prev[e] - m_new[e]) for e in es]
            pv = [_dot(vt_ref[j, HEAD * e:HEAD * (e + 1), :], p[e]) for e in es]
            for e in es:
                l_sc[e] = alpha[e] * l_sc[e] + jnp.sum(p[e], axis=0, keepdims=True)
                acc[e] = alpha[e] * acc[e] + pv[e]
                m_sc[e] = m_new[e]

        def loop_body(j, carry):
            step(j, False)
            return carry

        lax.fori_loop(0, i, loop_body, 0)
        step(i, True)
        for e in es:
            o_ref[:, HEAD * e:HEAD * (e + 1)] = (acc[e] / l_sc[e]).T
            lse_ref[e] = jnp.broadcast_to(m_sc[e] + jnp.log(l_sc[e]) * LOG2E, (8, blk))
        _ride_end(rider, r_refs, grid_step, (nh // hp) * nb)

    outs = _pcall(
        body, name="mla_fwd", grid=(nh // hp, nb),
        in_specs=[pl.BlockSpec((None, hp * 2 * HEAD, blk), lambda g, i: (i, g, 0)),
                  pl.BlockSpec((s, hp * 2 * HEAD), lambda g, i: (0, g), pipeline_mode=once),
                  pl.BlockSpec((nb, hp * HEAD, blk), lambda g, i: (0, g, 0), pipeline_mode=once)]
        + [_ANY] * r_in,
        out_specs=[pl.BlockSpec((blk, hp * HEAD), lambda g, i: (i, g)),
                   pl.BlockSpec((hp, None, 8, blk), lambda g, i: (g, i, 0, 0))] + [_ANY] * r_out,
        out_shape=[SDS((s, nh * HEAD), F32), SDS((nh, nb, 8, blk), F32)] + rider.out_shapes,
        scratch=[pltpu.VMEM((hp, 1, blk), F32), pltpu.VMEM((hp, 1, blk), F32),
                 pltpu.VMEM((hp, HEAD, blk), F32)] + rider.scratch)(qt, kc, vt, *rider.arrays)
    return outs[0], outs[1], outs[2:]


def _mla_bwd(qc, qt, kc, kt, v, do, dot, lse, delta, nh, rider):
    nb, _, blk = qt.shape
    s = nb * blk
    hp = MLA_HP
    once = pl.Buffered(1)
    r_in, r_out = len(rider.arrays), len(rider.out_shapes)
    qs = [slice(2 * HEAD * e, 2 * HEAD * (e + 1)) for e in range(hp)]
    vs = [slice(HEAD * e, HEAD * (e + 1)) for e in range(hp)]

    def body(*refs):
        q_ref, qt_ref, do_ref, dot_ref, lse_ref, dl_ref, k_ref, kt_ref, v_ref = refs[:9]
        dqt_ref, dk_ref, dv_ref = refs[9 + r_in:12 + r_in]
        dk_acc, dv_acc = refs[12 + r_in + r_out:14 + r_in + r_out]
        r_refs = (refs[9:9 + r_in], refs[12 + r_in:12 + r_in + r_out], refs[14 + r_in + r_out:])
        j = pl.program_id(1)
        grid_step = pl.program_id(0) * nb + j
        _ride_begin(rider, r_refs, grid_step)

        @pl.when(j == 0)
        def _():
            dqt_ref[...] = jnp.zeros_like(dqt_ref)

        dk_acc[...] = jnp.zeros_like(dk_acc)
        dv_acc[...] = jnp.zeros_like(dv_acc)
        es = range(hp)
        kj = [k_ref[:, qs[e]] for e in es]
        ktj = [kt_ref[qs[e], :] for e in es]
        vj = [v_ref[:, vs[e]] for e in es]

        def step(i, masked):
            rows = pl.ds(pl.multiple_of(i * blk, blk), blk)
            sc = [_dot(kj[e], qt_ref[i, qs[e], :]) for e in es]
            dp = [_dot(vj[e], dot_ref[i, vs[e], :]) for e in es]
            if masked:
                key = lax.broadcasted_iota(jnp.int32, (blk, blk), 0)
                qry = lax.broadcasted_iota(jnp.int32, (blk, blk), 1)
                sc = [jnp.where(key <= qry, x, -1e30) for x in sc]
            p = [jnp.exp2(sc[e] - lse_ref[e, i, 0:1, :]) for e in es]
            ds = [p[e] * (dp[e] - dl_ref[e, i, 0:1, :]) for e in es]
            dv = [_dot(p[e], do_ref[rows, vs[e]]) for e in es]
            dk = [_dot(ds[e], q_ref[rows, qs[e]]) for e in es]
            dq = [_dot(ktj[e], ds[e]) for e in es]
            for e in es:
                dv_acc[:, vs[e]] += dv[e]
                dk_acc[:, qs[e]] += dk[e]
                dqt_ref[i, qs[e], :] += dq[e]

        def loop_body(i, carry):
            step(i, False)
            return carry

        step(j, True)
        lax.fori_loop(j + 1, nb, loop_body, 0)
        dk_ref[...] = dk_acc[...]
        dv_ref[...] = dv_acc[...]
        _ride_end(rider, r_refs, grid_step, (nh // hp) * nb)

    rows_spec = pl.BlockSpec((hp, nb, 8, blk), lambda g, j: (g, 0, 0, 0), pipeline_mode=once)
    outs = _pcall(
        body, name="mla_bwd", grid=(nh // hp, nb),
        in_specs=[pl.BlockSpec((s, hp * 2 * HEAD), lambda g, j: (0, g), pipeline_mode=once),
                  pl.BlockSpec((nb, hp * 2 * HEAD, blk), lambda g, j: (0, g, 0), pipeline_mode=once),
                  pl.BlockSpec((s, hp * HEAD), lambda g, j: (0, g), pipeline_mode=once),
                  pl.BlockSpec((nb, hp * HEAD, blk), lambda g, j: (0, g, 0), pipeline_mode=once),
                  rows_spec, rows_spec,
                  pl.BlockSpec((blk, hp * 2 * HEAD), lambda g, j: (j, g)),
                  pl.BlockSpec((None, hp * 2 * HEAD, blk), lambda g, j: (j, g, 0)),
                  pl.BlockSpec((blk, hp * HEAD), lambda g, j: (j, g))] + [_ANY] * r_in,
        out_specs=[pl.BlockSpec((nb, hp * 2 * HEAD, blk), lambda g, j: (0, g, 0), pipeline_mode=once),
                   pl.BlockSpec((blk, hp * 2 * HEAD), lambda g, j: (j, g)),
                   pl.BlockSpec((blk, hp * HEAD), lambda g, j: (j, g))] + [_ANY] * r_out,
        out_shape=[SDS((nb, nh * 2 * HEAD, blk), F32), SDS((s, nh * 2 * HEAD), F32),
                   SDS((s, nh * HEAD), F32)] + rider.out_shapes,
        scratch=[pltpu.VMEM((blk, hp * 2 * HEAD), F32), pltpu.VMEM((blk, hp * HEAD), F32)] + rider.scratch,
        vmem=VMEM_LIMIT_WIDE)(qc, qt, do, dot, lse, delta, kc, kt, v, *rider.arrays)
    return outs[0], outs[1], outs[2], outs[3:]


def _mix_fwd(og, proj, om, gw, mw, nh, z_col):
    s = og.shape[0]
    w = nh * HEAD
    ts = _tile(s, 256, 8)

    def body(og_ref, z_ref, om_ref, gw_ref, mw_ref, o_ref):
        for h in range(nh):
            hs = slice(HEAD * h, HEAD * (h + 1))
            a = og_ref[:, hs]
            r = lax.rsqrt(jnp.mean(a * a, axis=-1, keepdims=True) + EPS)
            o_ref[:, hs] = (a * r * gw_ref[...] * _silu(z_ref[:, hs])).astype(BF16)
            b = om_ref[:, hs]
            r = lax.rsqrt(jnp.mean(b * b, axis=-1, keepdims=True) + EPS)
            o_ref[:, w + HEAD * h:w + HEAD * (h + 1)] = (b * r * mw_ref[...]).astype(BF16)

    return _pcall(body, name="mix_fwd", grid=(s // ts,),
                  in_specs=[_rows(ts, w), _rows(ts, w, z_col), _rows(ts, w), _full((1, HEAD)),
                            _full((1, HEAD))],
                  out_specs=_rows(ts, 2 * w), out_shape=SDS((s, 2 * w), BF16))(og, proj, om, gw, mw)


def _mix_bwd(dmix, og, proj, om, gw, mw, nh, z_col):
    s = og.shape[0]
    w = nh * HEAD
    ts = _tile(s, MLA_BLOCK)

    def body(d_ref, og_ref, z_ref, om_ref, gw_ref, mw_ref, dog_ref, dz_ref, dom_ref, dgw_ref, dmw_ref,
             dl_ref, domt_ref):
        @pl.when(pl.program_id(0) == 0)
        def _():
            dgw_ref[...] = jnp.zeros_like(dgw_ref)
            dmw_ref[...] = jnp.zeros_like(dmw_ref)

        dgw = jnp.zeros((1, HEAD), F32)
        dmw = jnp.zeros((1, HEAD), F32)
        for h in range(nh):
            hs = slice(HEAD * h, HEAD * (h + 1))
            a, z, dy = og_ref[:, hs], z_ref[:, hs], d_ref[:, hs]
            r = lax.rsqrt(jnp.mean(a * a, axis=-1, keepdims=True) + EPS)
            ah = a * r
            sz = _silu(z)
            dz_ref[:, hs] = (dy * (ah * gw_ref[...]) * _dsilu(z)).astype(BF16)
            dn = dy * sz
            dgw = dgw + jnp.sum(dn * ah, axis=0, keepdims=True)
            dah = dn * gw_ref[...]
            dog_ref[:, hs] = r * (dah - ah * jnp.mean(dah * ah, axis=-1, keepdims=True))
            b, dyb = om_ref[:, hs], d_ref[:, w + HEAD * h:w + HEAD * (h + 1)]
            r = lax.rsqrt(jnp.mean(b * b, axis=-1, keepdims=True) + EPS)
            bh = b * r
            dmw = dmw + jnp.sum(dyb * bh, axis=0, keepdims=True)
            dbh = dyb * mw_ref[...]
            dom = r * (dbh - bh * jnp.mean(dbh * bh, axis=-1, keepdims=True))
            dom_ref[:, hs] = dom.astype(BF16)
            domt_ref[hs, :] = dom.T.astype(BF16)
            delta = jnp.broadcast_to(jnp.sum(dom * b, axis=-1, keepdims=True), (ts, LANE))
            dl_ref[h] = delta.T[0:8, :]
        dgw_ref[...] += dgw
        dmw_ref[...] += dmw

    return _pcall(body, name="mix_bwd", grid=(s // ts,),
                  in_specs=[_rows(ts, 2 * w), _rows(ts, w), _rows(ts, w, z_col), _rows(ts, w),
                            _full((1, HEAD)), _full((1, HEAD))],
                  out_specs=[_rows(ts, w), _rows(ts, w, z_col), _rows(ts, w), _full((1, HEAD)), _full((1, HEAD)),
                             pl.BlockSpec((nh, None, 8, ts), lambda i: (0, i, 0, 0)),
                             pl.BlockSpec((None, w, ts), lambda i: (i, 0, 0))],
                  out_shape=[SDS((s, w), F32), SDS((s, proj.shape[1]), BF16), SDS((s, w), BF16),
                             SDS((1, HEAD), F32), SDS((1, HEAD), F32),
                             SDS((nh, s // ts, 8, ts), F32), SDS((s // ts, w, ts), BF16)])(
                                 dmix, og, proj, om, gw, mw)


def _swiglu_fwd(h2, wg, wu):
    m, kdim = h2.shape
    tn = wg.shape[2]
    n = 4 * tn
    tm, tk = _tile(m, 512), _tile(kdim, 2048)
    nk = kdim // tk

    def body(a_ref, g_ref, u_ref, act_ref, go_ref, uo_ref, gacc, uacc):
        k = pl.program_id(2)

        @pl.when(k == 0)
        def _():
            gacc[...] = jnp.zeros_like(gacc)
            uacc[...] = jnp.zeros_like(uacc)

        a = a_ref[...]
        gacc[...] += _dot(a, g_ref[...])
        uacc[...] += _dot(a, u_ref[...])

        @pl.when(k == nk - 1)
        def _():
            g, u = gacc[...], uacc[...]
            act_ref[...] = (_silu(g) * u).astype(BF16)
            go_ref[...] = g.astype(BF16)
            uo_ref[...] = u.astype(BF16)

    a_spec = pl.BlockSpec((tm, tk), lambda i, j, k: (i, k))
    b_spec = pl.BlockSpec((None, tk, tn), lambda i, j, k: (j, k, 0))
    o_spec = pl.BlockSpec((tm, tn), lambda i, j, k: (i, j))
    return _pcall(body, name="swiglu_fwd", grid=(m // tm, n // tn, nk),
                  in_specs=[a_spec, b_spec, b_spec], out_specs=[o_spec] * 3,
                  out_shape=[SDS((m, n), BF16)] * 3,
                  scratch=[pltpu.VMEM((tm, tn), F32), pltpu.VMEM((tm, tn), F32)])(h2, wg, wu)


def _swiglu_bwd(dx3, wd, g, u):
    m, kdim = dx3.shape
    n = wd.shape[0]
    tm, tn = _tile(m, 1024), _tile(n, 512)

    def body(a_ref, b_ref, g_ref, u_ref, dg_ref, du_ref):
        da = _dot_nt(a_ref[...], b_ref[...])
        gv, uv = g_ref[...].astype(F32), u_ref[...].astype(F32)
        dg_ref[...] = (da * uv * _dsilu(gv)).astype(BF16)
        du_ref[...] = (da * _silu(gv)).astype(BF16)

    a_spec = pl.BlockSpec((tm, kdim), lambda i, j: (i, 0))
    b_spec = pl.BlockSpec((tn, kdim), lambda i, j: (j, 0))
    o_spec = pl.BlockSpec((tm, tn), lambda i, j: (i, j))
    return _pcall(body, name="swiglu_bwd", grid=(m // tm, n // tn),
                  in_specs=[a_spec, b_spec, o_spec, o_spec], out_specs=[o_spec] * 2,
                  out_shape=[SDS((m, n), BF16)] * 2)(dx3, wd, g, u)


def _sum_pair(g, recv, place, name):
    _, _, rh, c = g.shape
    tr = _tile(rh, 256, 16)

    def body(pl_ref, g_ref, r_ref, o16_ref, own_ref):
        sm = g_ref[...].astype(F32) + r_ref[...].astype(F32)
        o16_ref[...] = sm.astype(BF16)

        @pl.when(pl.program_id(1) == pl_ref[1])
        def _():
            own_ref[...] = sm

    grid_spec = pltpu.PrefetchScalarGridSpec(
        num_scalar_prefetch=1, grid=(rh // tr, 4),
        in_specs=[pl.BlockSpec((None, None, tr, c), lambda i, t, p: (t, p[0], i, 0)),
                  pl.BlockSpec((None, tr, c), lambda i, t, p: (t, i, 0))],
        out_specs=[pl.BlockSpec((None, tr, c), lambda i, t, p: (t, i, 0)),
                   pl.BlockSpec((tr, c), lambda i, t, p: (i, 0))])
    return pl.pallas_call(
        body, name=name, grid_spec=grid_spec,
        out_shape=[SDS((4, rh, c), BF16), SDS((rh, c), F32)],
        compiler_params=pltpu.CompilerParams(dimension_semantics=("arbitrary",) * 2,
                                             vmem_limit_bytes=VMEM_LIMIT))(place, g, recv)


def _sum_chips(own, recv, name):
    rh, c = own.shape
    tr = _tile(rh, 256, 16)

    def body(o_ref, r_ref, out_ref):
        acc = o_ref[...]
        for j in range(3):
            acc = acc + r_ref[j].astype(F32)
        out_ref[...] = acc

    return _pcall(body, name=name, grid=(rh // tr,),
                  in_specs=[_rows(tr, c), pl.BlockSpec((3, tr, c), lambda i: (0, i, 0))],
                  out_specs=_rows(tr, c), out_shape=SDS(own.shape, F32))(own, recv)


def _adamw_update(wv, gv, mv, vv):
    mn = ADAM_B1 * mv + (1.0 - ADAM_B1) * gv
    vn = ADAM_B2 * vv + (1.0 - ADAM_B2) * (gv * gv)
    m_hat = mn / (1.0 - ADAM_B1 ** ADAM_STEP)
    v_hat = vn / (1.0 - ADAM_B2 ** ADAM_STEP)
    return -ADAM_LR * (m_hat / (jnp.sqrt(v_hat) + ADAM_EPS) + ADAM_WD * wv), mn, vn


def _adamw(w, g, m, v, name):
    r, c = w.shape
    tr = _tile(r, 256, 8)

    def body(w_ref, g_ref, m_ref, v_ref, d_ref, mo_ref, vo_ref):
        d_ref[...], mo_ref[...], vo_ref[...] = _adamw_update(w_ref[...], g_ref[...], m_ref[...], v_ref[...])

    spec = _rows(tr, c)
    return _pcall(body, name=name, grid=(r // tr,), in_specs=[spec] * 4, out_specs=[spec] * 3,
                  out_shape=[SDS(w.shape, F32)] * 3)(w, g, m, v)


def _adamw_halves(w, mine, theirs, m, v, place, name):
    r, c = w.shape
    rh = r // 2
    tr = _tile(rh, 256, 8)
    nt = rh // tr

    def body(p_ref, w_ref, a_ref, b_ref, m_ref, v_ref, g_ref, d_ref, mo_ref, vo_ref):
        gv = jnp.where(pl.program_id(0) // nt == p_ref[0], a_ref[...], b_ref[...])
        g_ref[...] = gv
        d_ref[...], mo_ref[...], vo_ref[...] = _adamw_update(w_ref[...], gv, m_ref[...], v_ref[...])

    full = pl.BlockSpec((tr, c), lambda i, p: (i, 0))
    half = pl.BlockSpec((tr, c), lambda i, p: (i % nt, 0))
    grid_spec = pltpu.PrefetchScalarGridSpec(num_scalar_prefetch=1, grid=(2 * nt,),
                                             in_specs=[full, half, half, full, full], out_specs=[full] * 4)
    return pl.pallas_call(
        body, name=name, grid_spec=grid_spec, out_shape=[SDS(w.shape, F32)] * 4,
        compiler_params=pltpu.CompilerParams(dimension_semantics=("arbitrary",),
                                             vmem_limit_bytes=VMEM_LIMIT))(place, w, mine, theirs, m, v)


def _place():
    x, y, c = lax.axis_index("x"), lax.axis_index("y"), lax.axis_index("c")
    chips = [(1 - x, y), (x, 1 - y), (1 - x, 1 - y)]
    return x, y, c, chips


_ANY = pl.BlockSpec(memory_space=pl.ANY)


def _remote(src, dst, sems, k, to):
    return pltpu.make_async_remote_copy(src_ref=src, dst_ref=dst, send_sem=sems[0].at[k], recv_sem=sems[1].at[k],
                                        device_id=to, device_id_type=MESH)


class _Gather:
    def __init__(self, shards):
        n = len(shards)
        self.arrays = list(shards)
        self.out_shapes = [SDS((4,) + a.shape, a.dtype) for a in shards]
        self.scratch = [pltpu.SemaphoreType.DMA((7 * n,)), pltpu.SemaphoreType.DMA((7 * n,))]

    def _plan(self, ins, outs, sems):
        x, y, c, chips = _place()
        own, sib = 2 * x + y, (x, y, 1 - c)
        plan = []
        for wi, (w, o) in enumerate(zip(ins, outs)):
            rh = w.shape[0] // 2
            mine, theirs = pl.ds(c * rh, rh), pl.ds((1 - c) * rh, rh)
            whole = _remote(w, o.at[own], sems, 7 * wi + 6, sib)
            ici, d2d, d2d_in = [], [], []
            for j, (tx, ty) in enumerate(chips):
                t = 2 * tx + ty
                ici.append(_remote(w.at[mine], o.at[own, mine], sems, 7 * wi + j, (tx, ty, c)))
                d2d.append(_remote(o.at[t, mine], o.at[t, mine], sems, 7 * wi + 3 + j, sib))
                d2d_in.append(_remote(o.at[t, theirs], o.at[t, theirs], sems, 7 * wi + 3 + j, sib))
            plan.append((whole, ici, d2d, d2d_in))
        return plan

    def begin(self, ins, outs, sems):
        for whole, ici, _, _ in self._plan(ins, outs, sems):
            whole.start()
            for cp in ici:
                cp.start()

    def middle(self, ins, outs, sems):
        for _, ici, d2d, _ in self._plan(ins, outs, sems):
            for cp_in, cp_on in zip(ici, d2d):
                cp_in.wait_recv()
                cp_on.start()

    def finish(self, ins, outs, sems):
        for whole, ici, d2d, d2d_in in self._plan(ins, outs, sems):
            for cp in d2d_in:
                cp.wait_recv()
            for cp in ici + d2d:
                cp.wait_send()
            whole.wait()


class _Swap:
    def __init__(self, grads):
        n = len(grads)
        self.arrays = list(grads)
        self.out_shapes = [SDS((4,) + g.shape[2:], g.dtype) for g in grads]
        self.scratch = [pltpu.SemaphoreType.DMA((4 * n,)), pltpu.SemaphoreType.DMA((4 * n,))]

    def _plan(self, ins, outs, sems):
        x, y, c, _ = _place()
        return [_remote(g.at[t, 1 - c], o.at[t], sems, 4 * wi + t, (x, y, 1 - c))
                for wi, (g, o) in enumerate(zip(ins, outs)) for t in range(4)]

    def begin(self, ins, outs, sems):
        for cp in self._plan(ins, outs, sems):
            cp.start()

    def middle(self, ins, outs, sems):
        pass

    def finish(self, ins, outs, sems):
        for cp in self._plan(ins, outs, sems):
            cp.wait()


class _Exchange:
    def __init__(self, pieces):
        n = len(pieces)
        self.arrays = list(pieces)
        self.out_shapes = [SDS((3,) + p.shape[1:], p.dtype) for p in pieces]
        self.scratch = [pltpu.SemaphoreType.DMA((3 * n,)), pltpu.SemaphoreType.DMA((3 * n,))]

    def _plan(self, ins, outs, sems):
        x, y, c, chips = _place()
        return [_remote(g.at[2 * tx + ty], o.at[j], sems, 3 * wi + j, (tx, ty, c))
                for wi, (g, o) in enumerate(zip(ins, outs)) for j, (tx, ty) in enumerate(chips)]

    def begin(self, ins, outs, sems):
        for cp in self._plan(ins, outs, sems):
            cp.start()

    def middle(self, ins, outs, sems):
        pass

    def finish(self, ins, outs, sems):
        for cp in self._plan(ins, outs, sems):
            cp.wait()


class _Share:
    def __init__(self, totals):
        n = len(totals)
        self.arrays = list(totals)
        self.out_shapes = [SDS(t.shape, t.dtype) for t in totals]
        self.scratch = [pltpu.SemaphoreType.DMA((n,)), pltpu.SemaphoreType.DMA((n,))]

    def _plan(self, ins, outs, sems):
        x, y, c, _ = _place()
        return [_remote(t, o, sems, wi, (x, y, 1 - c)) for wi, (t, o) in enumerate(zip(ins, outs))]

    def begin(self, ins, outs, sems):
        for cp in self._plan(ins, outs, sems):
            cp.start()

    def middle(self, ins, outs, sems):
        pass

    def finish(self, ins, outs, sems):
        for cp in self._plan(ins, outs, sems):
            cp.wait()


def _ride_begin(rider, r_refs, step):
    @pl.when(step == 0)
    def _():
        rider.begin(*r_refs)


def _ride_end(rider, r_refs, step, nsteps):
    @pl.when(step == min(3 * nsteps // 4, nsteps - 1))
    def _():
        rider.middle(*r_refs)

    @pl.when(step == nsteps - 1)
    def _():
        rider.finish(*r_refs)


def _comm(rider, name):
    n_in, n_out = len(rider.arrays), len(rider.out_shapes)

    def body(*refs):
        r_refs = (refs[:n_in], refs[n_in:n_in + n_out], refs[n_in + n_out:])
        rider.begin(*r_refs)
        rider.middle(*r_refs)
        rider.finish(*r_refs)

    return pl.pallas_call(body, name=name, out_shape=rider.out_shapes, in_specs=[_ANY] * n_in,
                          out_specs=[_ANY] * n_out, scratch_shapes=rider.scratch)(*rider.arrays)


def _small_allreduce(pk, name):
    r = pk.shape[0]
    rels = [(dx, dy, dc) for dx in (0, 1) for dy in (0, 1) for dc in (0, 1) if dx or dy or dc]

    def body(p_ref, o_ref, buf, send_sems, recv_sems):
        x, y, c, _ = _place()
        me = 4 * x + 2 * y + c
        buf[me] = p_ref[...]
        cps = []
        for k, (dx, dy, dc) in enumerate(rels):
            to = (1 - x if dx else x, 1 - y if dy else y, 1 - c if dc else c)
            cps.append(pltpu.make_async_remote_copy(src_ref=p_ref, dst_ref=buf.at[me], send_sem=send_sems.at[k],
                                                    recv_sem=recv_sems.at[k], device_id=to,
                                                    device_id_type=MESH))
        for cpy in cps:
            cpy.start()
        for cpy in cps:
            cpy.wait()
        acc = buf[0]
        for d in range(1, 8):
            acc = acc + buf[d]
        o_ref[...] = acc

    vm = pl.BlockSpec(memory_space=pltpu.VMEM)
    return pl.pallas_call(body, name=name, out_shape=SDS(pk.shape, F32), in_specs=[vm], out_specs=vm,
                          scratch_shapes=[pltpu.VMEM((8, r, LANE), F32), pltpu.SemaphoreType.DMA((7,)),
                                          pltpu.SemaphoreType.DMA((7,))])(pk)


ATTN_W = ("w_in", "w_uq", "w_ukv", "w_out")
FFN_W = ("w_gate", "w_up", "w_down")
BIG = ATTN_W + FFN_W


def _cols_from_chips(g):
    return jnp.concatenate([g[t] for t in range(4)], axis=1)


def _cols_to_chips(full):
    r, n = full.shape
    return full.reshape(r, 4, n // 4).transpose(1, 0, 2).reshape(4, 2, r // 2, n // 4)


def _rows_to_chips(full):
    n, c = full.shape
    return full.reshape(4, 2, n // 8, c)


def _permute_w_in(w, nh):
    d = w.shape[0]
    g = 4 * nh * HEAD
    lr = (w.shape[1] - g - 2 * nh - ROPE) // 2
    o = g + 2 * nh
    pad = jnp.zeros((d, LANE - ROPE - 8 - nh), w.dtype)
    pad8 = jnp.zeros((d, 8 - nh), w.dtype)
    return jnp.concatenate([w[:, :g], w[:, o:o + 2 * lr], w[:, o + 2 * lr:], w[:, g:g + nh], pad8,
                            w[:, g + nh:g + 2 * nh], pad, jnp.zeros((d, LANE), w.dtype)], axis=1)


def _unpermute_w_in(wp, nh, lr):
    g = 4 * nh * HEAD
    mc = g + 2 * lr
    return jnp.concatenate([wp[:, :g], wp[:, mc + B_LANE:mc + B_LANE + nh], wp[:, mc + A_LANE:mc + A_LANE + nh],
                            wp[:, g:g + 2 * lr], wp[:, mc:mc + ROPE]], axis=1)


def _permute_w_uq(w, nh):
    lr = w.shape[0]
    w3 = w.reshape(lr, nh, HEAD + ROPE)
    return jnp.concatenate([w3, jnp.zeros((lr, nh, HEAD - ROPE), w.dtype)], axis=2).reshape(lr, nh * 2 * HEAD)


def _unpermute_w_uq(wp, nh):
    lr = wp.shape[0]
    return wp.reshape(lr, nh, 2 * HEAD)[:, :, :HEAD + ROPE].reshape(lr, nh * (HEAD + ROPE))


def _permute_w_ukv(w, nh):
    lr = w.shape[0]
    w3 = w.reshape(lr, nh, 2 * HEAD)
    kp = jnp.concatenate([w3[:, :, :HEAD], jnp.zeros((lr, nh, HEAD), w.dtype)], axis=2)
    return jnp.concatenate([kp.reshape(lr, nh * 2 * HEAD), w3[:, :, HEAD:].reshape(lr, nh * HEAD)], axis=1)


def _unpermute_w_ukv(wp, nh):
    lr = wp.shape[0]
    kp = wp[:, :nh * 2 * HEAD].reshape(lr, nh, 2 * HEAD)[:, :, :HEAD]
    vp = wp[:, nh * 2 * HEAD:].reshape(lr, nh, HEAD)
    return jnp.concatenate([kp, vp], axis=2).reshape(lr, nh * 2 * HEAD)


def _sum_pairs(grads, recv, place, tag):
    sums = [_sum_pair(g, r, place, "sum_pair_%s%d" % (tag, k)) for k, (g, r) in enumerate(zip(grads, recv))]
    return [s[0] for s in sums], [s[1] for s in sums]


def _reduce_end(own, recv, tag):
    return [_sum_chips(o, r, "sum_chips_%s%d" % (tag, k)) for k, (o, r) in enumerate(zip(own, recv))]


def _step(x, pos, tgt, w_in, attn_shards, ffn_shards, small, place):
    nh = small["a_log"].shape[1]
    lr = small["q_norm_w"].shape[1]
    w = nh * HEAD
    z_col, col_q, col_kv = 3, 4 * w // lr, 4 * w // lr + 1
    misc_c = 4 * w + 2 * lr
    misc_col = misc_c // LANE
    assert (4 * w) % lr == 0 and small["kv_norm_w"].shape[1] == lr

    zl = jnp.zeros((1, LANE), F32)
    alog_l = zl.at[:, A_LANE:A_LANE + nh].set(small["a_log"])
    dtb_l = zl.at[:, A_LANE:A_LANE + nh].set(small["dt_bias"])
    conv_w = small["conv_w"]

    h1, (in4,) = _norm_fwd(x, small["attn_norm_w"], "norm1", rider=_Gather([w_in]))
    win_p = _permute_w_in(_cols_from_chips(in4), nh)
    proj, (uq4, ukv4, out4) = _mm([(h1, win_p)], name="proj_in", rider=_Gather(attn_shards))
    wuq_p = _permute_w_uq(_cols_from_chips(uq4), nh)
    wukv_p = _permute_w_ukv(_cols_from_chips(ukv4), nh)
    w_out = out4.reshape(-1, out4.shape[2])
    gq, gk, gv, gb, gbt = _gdn_prep(proj, conv_w, alog_l, dtb_l, nh, misc_col)
    (o_gdn, states), (wg4,) = _gdn_fwd(gq, gk, gv, gb, gbt, nh, _Gather(ffn_shards[:1]))
    cqn, ckvn = _mla_norm(proj, small["q_norm_w"], small["kv_norm_w"], col_q, col_kv)
    qraw = _mm([(cqn, wuq_p)], name="proj_uq", out_dtype=BF16)
    kvraw = _mm([(ckvn, wukv_p)], name="proj_ukv", out_dtype=BF16)
    qc, kc, vv, qt, vt, kt = _mla_rope(qraw, kvraw, proj, pos, nh, misc_col)
    o_mla, lse, (wu4, wd4) = _mla_fwd(qt, kc, vt, nh, _Gather(ffn_shards[1:]))
    w_down = wd4.reshape(-1, wd4.shape[2])
    mixed = _mix_fwd(o_gdn, proj, o_mla, small["gdn_norm_w"], small["mla_out_norm_w"], nh, z_col)
    x2 = _mm([(mixed, w_out)], name="proj_out", res=x)
    h2 = _norm_fwd(x2, small["ffn_norm_w"], "norm2")
    act, gpre, upre = _swiglu_fwd(h2, wg4, wu4)
    x3 = _mm([(act, w_down)], name="proj_down", res=x2, tk=2816)
    dx3, d_final, loss, dx3h = _final_loss(x3, tgt, small["final_norm_w"])

    gs = {"final_norm_w": d_final}
    dgate, dup = _swiglu_bwd(dx3h, w_down, gpre, upre)
    g_down = _rows_to_chips(_mm([(act, dx3h)], name="dw_down", ta=True, out_dtype=BF16))
    g_gate = _mm([(h2, dgate)], name="dw_gate", ta=True, out_dtype=BF16, out_chips=True)
    g_up = _mm([(h2, dup)], name="dw_up", ta=True, out_dtype=BF16, out_chips=True)
    halves = lambda g: g.reshape(4, 2, g.shape[1] // 2, g.shape[2])
    ffn_g = [halves(g_gate), halves(g_up), g_down]
    dh2, ffn_sib = _mm([(dgate, wg4), (dup, wu4)], name="dh2", tb=True, b_chips=True, out_dtype=BF16,
                       rider=_Swap(ffn_g))
    ffn16, ffn_own = _sum_pairs(ffn_g, ffn_sib, place, "ffn")
    dx2, gs["ffn_norm_w"], dx2h = _norm_bwd(dh2, x2, small["ffn_norm_w"], dx3, "norm2_bwd", True)
    dmix = _mm([(dx2h, w_out)], name="dmix", tb=True, out_dtype=BF16)
    g_out = _rows_to_chips(_mm([(mixed, dx2h)], name="dw_out", ta=True, out_dtype=BF16))
    d_ogdn, dproj, d_omla, gs["gdn_norm_w"], gs["mla_out_norm_w"], delta, d_omla_t = _mix_bwd(
        dmix, o_gdn, proj, o_mla, small["gdn_norm_w"], small["mla_out_norm_w"], nh, z_col)
    dqc, dkc, dvv, ffn_recv = _mla_bwd(qc, qt, kc, kt, vv, d_omla, d_omla_t, lse, delta, nh, _Exchange(ffn16))
    ffn_tot = _reduce_end(ffn_own, ffn_recv, "ffn")
    dqraw, dkvraw, dkr = _mla_rope_bwd(dqc, dkc, dvv, pos, nh)
    dcqn = _mm([(dqraw, wuq_p)], name="dcqn", tb=True)
    dckvn = _mm([(dkvraw, wukv_p)], name="dckvn", tb=True)
    g_uq = _cols_to_chips(_unpermute_w_uq(_mm([(cqn, dqraw)], name="dw_uq", ta=True, out_dtype=BF16), nh))
    g_ukv = _cols_to_chips(_unpermute_w_ukv(_mm([(ckvn, dkvraw)], name="dw_ukv", ta=True, out_dtype=BF16), nh))
    dproj, gs["q_norm_w"], gs["kv_norm_w"] = _mla_norm_bwd(
        proj, small["q_norm_w"], small["kv_norm_w"], dcqn, dckvn, dproj, col_q, col_kv)
    (dgq, dgk, dgv, dgb), ffn_shared = _gdn_bwd(gq, gk, gv, gb, gbt, states, d_ogdn, nh, _Share(ffn_tot))
    dconv, dproj, gs["conv_w"], dal, ddb = _gdn_prep_bwd(
        proj, conv_w, alog_l, dtb_l, dgq, dgk, dgv, dgb, dkr, dproj, nh, misc_col)
    gs["a_log"] = dal[:, A_LANE:A_LANE + nh]
    gs["dt_bias"] = ddb[:, A_LANE:A_LANE + nh]
    dproj = _conv_bwd_input(dconv, conv_w, dproj)
    g_in = _cols_to_chips(_unpermute_w_in(_mm([(h1, dproj)], name="dw_in", ta=True, out_dtype=BF16), nh, lr))
    att_g = [g_in, g_uq, g_ukv, g_out]
    att16, att_own = _sum_pairs(att_g, _comm(_Swap(att_g), "swap_att"), place, "att")
    dh1, att_recv = _mm([(dproj, win_p)], name="dh1", tb=True, out_dtype=BF16, rider=_Exchange(att16))
    att_tot = _reduce_end(att_own, att_recv, "att")
    att_shared = _comm(_Share(att_tot), "share_att")
    grad_x, gs["attn_norm_w"] = _norm_bwd(dh1, x, small["attn_norm_w"], dx2, "norm1_bwd", False)
    return loss, grad_x, att_tot + ffn_tot, list(att_shared) + list(ffn_shared), gs


SMALL = ("attn_norm_w", "ffn_norm_w", "final_norm_w", "q_norm_w", "kv_norm_w", "gdn_norm_w",
         "mla_out_norm_w", "a_log", "dt_bias")
WEIGHTS = ("attn_norm_w", "w_in", "conv_w", "a_log", "dt_bias", "gdn_norm_w", "q_norm_w", "w_uq",
           "kv_norm_w", "w_ukv", "mla_out_norm_w", "w_out", "ffn_norm_w", "w_gate", "w_up", "w_down",
           "final_norm_w")


def _pack_small(vecs):
    flat = jnp.concatenate([v.astype(F32).reshape(-1) for v in vecs])
    pad = (-flat.shape[0]) % (8 * LANE)
    return jnp.concatenate([flat, jnp.zeros((pad,), F32)]).reshape(-1, LANE)


def kernel(x, positions, attn_norm_w, w_in, conv_w, a_log, dt_bias, gdn_norm_w, q_norm_w, w_uq, kv_norm_w, w_ukv, mla_out_norm_w, w_out, ffn_norm_w, w_gate, w_up, w_down, final_norm_w, loss_target, m_attn_norm_w, m_w_in, m_conv_w, m_a_log, m_dt_bias, m_gdn_norm_w, m_q_norm_w, m_w_uq, m_kv_norm_w, m_w_ukv, m_mla_out_norm_w, m_w_out, m_ffn_norm_w, m_w_gate, m_w_up, m_w_down, m_final_norm_w, v_attn_norm_w, v_w_in, v_conv_w, v_a_log, v_dt_bias, v_gdn_norm_w, v_q_norm_w, v_w_uq, v_kv_norm_w, v_w_ukv, v_mla_out_norm_w, v_w_out, v_ffn_norm_w, v_w_gate, v_w_up, v_w_down, v_final_norm_w):
    args = dict(locals())
    xi, yi, ci = lax.axis_index("x"), lax.axis_index("y"), lax.axis_index("c")
    chip = 2 * xi + yi

    def two_d(a):
        return a.reshape(a.shape[-2:]) if a.ndim >= 2 else a.reshape(1, -1)

    wloc = {n: two_d(args[n]) for n in WEIGHTS}
    mloc = {n: two_d(args["m_" + n]) for n in WEIGHTS}
    vloc = {n: two_d(args["v_" + n]) for n in WEIGHTS}

    cw = wloc["conv_w"]
    cshard = cw.shape[1]
    cfull = jnp.zeros((CONV, 4 * cshard), F32)
    cfull = lax.dynamic_update_slice(cfull, jnp.where(ci == 0, cw, 0.0), (0, chip * cshard))
    conv_full = _small_allreduce(_pack_small([cfull]), "gather_conv_w").reshape(-1)[:CONV * 4 * cshard]
    conv_full = conv_full.reshape(CONV, 4 * cshard)

    small = {n: wloc[n] for n in SMALL}
    small["conv_w"] = conv_full

    pos = positions.reshape(-1, 1).astype(F32)
    place = jnp.stack([ci, chip]).astype(jnp.int32)
    loss, grad_x, totals, from_sib, gs = _step(
        two_d(x), pos, two_d(loss_target), wloc["w_in"].astype(BF16), [wloc[n].astype(BF16) for n in ATTN_W[1:]],
        [wloc[n].astype(BF16) for n in FFN_W], small, place)

    small_names = SMALL + ("conv_w",)
    pk = _pack_small([gs[n] for n in small_names] + [loss])
    red = _small_allreduce(pk, "reduce_small").reshape(-1)
    gsm, off = {}, 0
    for n in small_names:
        shp = gs[n].shape
        gsm[n] = red[off:off + shp[0] * shp[1]].reshape(shp)
        off += shp[0] * shp[1]
    loss_out = red[off]
    gsm["conv_w"] = lax.dynamic_slice(gsm["conv_w"], (0, chip * cshard), (CONV, cshard))

    grads, deltas, new_m, new_v = {}, {}, {}, {}
    for n, mine, theirs in zip(BIG, totals, from_sib):
        grads[n], deltas[n], new_m[n], new_v[n] = _adamw_halves(wloc[n], mine, theirs, mloc[n], vloc[n], place,
                                                                "adamw_" + n)
    grads["conv_w"] = gsm["conv_w"]
    deltas["conv_w"], new_m["conv_w"], new_v["conv_w"] = _adamw(wloc["conv_w"], gsm["conv_w"], mloc["conv_w"],
                                                                vloc["conv_w"], "adamw_conv_w")
    sm_shapes = [wloc[n].shape for n in SMALL]
    pd, pm, pv = _adamw(_pack_small([wloc[n] for n in SMALL]), _pack_small([gsm[n] for n in SMALL]),
                        _pack_small([mloc[n] for n in SMALL]), _pack_small([vloc[n] for n in SMALL]),
                        "adamw_small")
    for dst, packed in ((deltas, pd), (new_m, pm), (new_v, pv)):
        flat, off = packed.reshape(-1), 0
        for n, shp in zip(SMALL, sm_shapes):
            dst[n] = flat[off:off + shp[0] * shp[1]].reshape(shp)
            off += shp[0] * shp[1]
    for n in SMALL:
        grads[n] = gsm[n]

    def like(n, a):
        return a.reshape(args[n].shape)

    outs = [loss_out.reshape(()), grad_x.reshape(x.shape)]
    for group in (grads, deltas, new_m, new_v):
        outs += [like(n, group[n]) for n in WEIGHTS]
    return tuple(outs)
```

```python
import jax
import jax.numpy as jnp
from jax import lax
from jax.experimental import pallas as pl
from jax.experimental.pallas import tpu as pltpu

F32, BF16 = jnp.float32, jnp.bfloat16
SDS = jax.ShapeDtypeStruct
MESH = pl.DeviceIdType.MESH

HEAD = 128
ROPE = 64
CHUNK = 64
PAIR = 2 * CHUNK
CONV = 4
EPS = 1e-6
ROPE_THETA = 10000.0
LANE = 128
B_LANE = 64
A_LANE = 72
VMEM_LIMIT = 48 * 1024 * 1024
VMEM_LIMIT_WIDE = 56 * 1024 * 1024
MLA_BLOCK = 512
LOG2E = 1.4426950408889634
LN2 = 0.6931471805599453
SM_SCALE = (HEAD + ROPE) ** -0.5

ADAM_LR = 0.001
ADAM_B1 = 0.9
ADAM_B2 = 0.999
ADAM_EPS = 1e-08
ADAM_WD = 0.01
ADAM_STEP = 10


def _tile(n, pref, mult=LANE):
    if n <= pref:
        return n
    t = (pref // mult) * mult
    while t >= mult:
        if n % t == 0:
            return t
        t -= mult
    return n


def _pcall(body, *, name, grid, in_specs, out_specs, out_shape, scratch=(), vmem=VMEM_LIMIT, aliases=None):
    return pl.pallas_call(
        body, name=name, grid=grid, in_specs=in_specs, out_specs=out_specs,
        out_shape=out_shape, scratch_shapes=list(scratch), input_output_aliases=aliases or {},
        compiler_params=pltpu.CompilerParams(
            dimension_semantics=("arbitrary",) * len(grid), vmem_limit_bytes=vmem))


def _pcall_riding(core, rider, *, name, grid, in_specs, out_specs, out_shape, args, scratch=()):
    n_in, n_out, n_scr = len(in_specs), len(out_specs), len(scratch)
    r_in, r_out = len(rider.arrays), len(rider.out_shapes)

    def body(*refs):
        ins, refs = refs[:n_in], refs[n_in:]
        r_ins, refs = refs[:r_in], refs[r_in:]
        outs, refs = refs[:n_out], refs[n_out:]
        r_outs, refs = refs[:r_out], refs[r_out:]
        scr, sems = refs[:n_scr], refs[n_scr:]
        r_refs = (r_ins, r_outs, sems)
        _ride_begin(rider, r_refs, pl.program_id(0))
        core(*ins, *outs, *scr)
        _ride_end(rider, r_refs, pl.program_id(0), grid[0])

    res = _pcall(body, name=name, grid=grid, in_specs=list(in_specs) + [_ANY] * r_in,
                 out_specs=list(out_specs) + [_ANY] * r_out, out_shape=list(out_shape) + rider.out_shapes,
                 scratch=list(scratch) + rider.scratch)(*args, *rider.arrays)
    return res[:n_out], res[n_out:]


def _rows(ts, width, col=0):
    return pl.BlockSpec((ts, width), lambda i: (i, col))


def _full(shape):
    nd = len(shape)
    return pl.BlockSpec(shape, lambda i: (0,) * nd)


def _dot(a, b):
    return jnp.dot(a.astype(BF16), b.astype(BF16), preferred_element_type=F32)


def _dot_nt(a, b):
    return lax.dot_general(a.astype(BF16), b.astype(BF16), (((1,), (1,)), ((), ())),
                           preferred_element_type=F32)


def _dot_tn(a, b):
    return lax.dot_general(a.astype(BF16), b.astype(BF16), (((0,), (0,)), ((), ())),
                           preferred_element_type=F32)


def _sigmoid(x):
    return 1.0 / (1.0 + jnp.exp(-x))


def _silu(x):
    return x * _sigmoid(x)


def _dsilu(x):
    s = _sigmoid(x)
    return s * (1.0 + x * (1.0 - s))


def _lane_iota(shape):
    return lax.broadcasted_iota(jnp.int32, shape, len(shape) - 1)


def _col(block, idx):
    return jnp.sum(jnp.where(_lane_iota(block.shape) == idx, block, 0.0), axis=-1, keepdims=True)


def _mm(pairs, *, name, ta=False, tb=False, out_dtype=F32, res=None, tm=1024, tn=1024, tk=2048,
        b_chips=False, out_chips=False, rider=None):
    a0, b0 = pairs[0]
    if ta:
        kdim, m = a0.shape
    else:
        m, kdim = a0.shape
    if b_chips and tb:
        n, tk = b0.shape[1], b0.shape[2]
        assert kdim == 4 * tk
    elif b_chips:
        n, tn = 4 * b0.shape[2], b0.shape[2]
        assert kdim == b0.shape[1]
    else:
        n = b0.shape[0] if tb else b0.shape[1]
    if out_chips:
        tn = n // 4
    tm = _tile(m, tm)
    tn = tn if (out_chips or (b_chips and not tb)) else _tile(n, tn)
    tk = tk if (b_chips and tb) else _tile(kdim, tk)
    assert m % tm == 0 and n % tn == 0 and kdim % tk == 0
    nk, npair = kdim // tk, len(pairs)
    grid = (m // tm, n // tn, nk)
    dims = (((0 if ta else 1,), (1 if tb else 0,)), ((), ()))
    n_in = 2 * npair + (res is not None)
    r_in, r_out = (len(rider.arrays), len(rider.out_shapes)) if rider else (0, 0)

    def body(*refs):
        o_ref = refs[n_in + r_in]
        acc = refs[n_in + r_in + 1 + r_out]
        k = pl.program_id(2)
        if rider:
            r_refs = (refs[n_in:n_in + r_in], refs[n_in + r_in + 1:n_in + r_in + 1 + r_out],
                      refs[n_in + r_in + 2 + r_out:])
            step = (pl.program_id(0) * grid[1] + pl.program_id(1)) * nk + k
            _ride_begin(rider, r_refs, step)

        @pl.when(k == 0)
        def _():
            acc[...] = jnp.zeros_like(acc)

        tot = None
        for p in range(npair):
            d = lax.dot_general(refs[2 * p][...].astype(BF16), refs[2 * p + 1][...].astype(BF16),
                                dims, preferred_element_type=F32)
            tot = d if tot is None else tot + d
        acc[...] += tot

        @pl.when(k == nk - 1)
        def _():
            r = acc[...]
            if res is not None:
                r = r + refs[2 * npair][...]
            o_ref[...] = r.astype(out_dtype)

        if rider:
            _ride_end(rider, r_refs, step, grid[0] * grid[1] * nk)

    if ta:
        a_spec = pl.BlockSpec((tk, tm), lambda i, j, k: (k, i))
    else:
        a_spec = pl.BlockSpec((tm, tk), lambda i, j, k: (i, k))
    if b_chips and tb:
        b_spec = pl.BlockSpec((None, tn, tk), lambda i, j, k: (k, j, 0))
    elif b_chips:
        b_spec = pl.BlockSpec((None, tk, tn), lambda i, j, k: (j, k, 0))
    elif tb:
        b_spec = pl.BlockSpec((tn, tk), lambda i, j, k: (j, k))
    else:
        b_spec = pl.BlockSpec((tk, tn), lambda i, j, k: (k, j))
    if out_chips:
        o_spec = pl.BlockSpec((None, tm, tn), lambda i, j, k: (j, i, 0))
        o_shape = SDS((4, m, tn), out_dtype)
    else:
        o_spec = pl.BlockSpec((tm, tn), lambda i, j, k: (i, j))
        o_shape = SDS((m, n), out_dtype)
    in_specs, args = [], []
    for a, b in pairs:
        in_specs += [a_spec, b_spec]
        args += [a, b]
    if res is not None:
        in_specs.append(o_spec)
        args.append(res)
    out_specs, out_shapes, scratch = [o_spec], [o_shape], [pltpu.VMEM((tm, tn), F32)]
    if rider:
        in_specs += [_ANY] * r_in
        args += rider.arrays
        out_specs += [_ANY] * r_out
        out_shapes += rider.out_shapes
        scratch += rider.scratch
    outs = _pcall(body, name=name, grid=grid, in_specs=in_specs, out_specs=out_specs, out_shape=out_shapes,
                  scratch=scratch)(*args)
    return (outs[0], outs[1:]) if rider else outs[0]


def _norm_fwd(x, w, name, rider=None):
    s, d = x.shape
    ts = _tile(s, 512, 8)

    def body(x_ref, w_ref, h_ref, ht_ref):
        xv = x_ref[...]
        r = lax.rsqrt(jnp.mean(xv * xv, axis=-1, keepdims=True) + EPS)
        h = xv * r * w_ref[...]
        h_ref[...] = h.astype(BF16)
        ht_ref[...] = h.T.astype(BF16)

    spec = dict(name=name, grid=(s // ts,), in_specs=[_rows(ts, d), _full((1, d))],
                out_specs=[_rows(ts, d), pl.BlockSpec((d, ts), lambda i: (0, i))],
                out_shape=[SDS((s, d), BF16), SDS((d, s), BF16)])
    if rider is None:
        return _pcall(body, **spec)(x, w)
    outs, r_outs = _pcall_riding(body, rider, args=(x, w), **spec)
    return outs[0], outs[1], r_outs


def _norm_bwd(dh, x, w, dres, name, with_bf16):
    s, d = x.shape
    ts = _tile(s, 256, 8)

    def body(dh_ref, x_ref, w_ref, dres_ref, dx_ref, dw_ref, *dx16_ref):
        @pl.when(pl.program_id(0) == 0)
        def _():
            dw_ref[...] = jnp.zeros_like(dw_ref)

        xv, dhv = x_ref[...], dh_ref[...]
        r = lax.rsqrt(jnp.mean(xv * xv, axis=-1, keepdims=True) + EPS)
        xh = xv * r
        dw_ref[...] += jnp.sum(dhv * xh, axis=0, keepdims=True)
        dxh = dhv * w_ref[...]
        dx = dres_ref[...] + r * (dxh - xh * jnp.mean(dxh * xh, axis=-1, keepdims=True))
        dx_ref[...] = dx
        for ref in dx16_ref:
            ref[...] = dx.astype(BF16)

    extra = 1 if with_bf16 else 0
    return _pcall(body, name=name, grid=(s // ts,),
                  in_specs=[_rows(ts, d), _rows(ts, d), _full((1, d)), _rows(ts, d)],
                  out_specs=[_rows(ts, d), _full((1, d))] + [_rows(ts, d)] * extra,
                  out_shape=[SDS((s, d), F32), SDS((1, d), F32)] + [SDS((s, d), BF16)] * extra)(
                      dh, x, w, dres)


def _final_loss(x3, tgt, w):
    s, d = x3.shape
    ts = _tile(s, 256, 8)

    def body(x_ref, t_ref, w_ref, dx_ref, dw_ref, loss_ref, dx16_ref):
        @pl.when(pl.program_id(0) == 0)
        def _():
            dw_ref[...] = jnp.zeros_like(dw_ref)
            loss_ref[...] = jnp.zeros_like(loss_ref)

        xv, wv = x_ref[...], w_ref[...]
        r = lax.rsqrt(jnp.mean(xv * xv, axis=-1, keepdims=True) + EPS)
        xh = xv * r
        err = xh * wv - t_ref[...]
        row = jnp.mean(err * err, axis=-1, keepdims=True)
        loss_ref[...] += 0.5 * jnp.sum(row, axis=0, keepdims=True)
        dy = err * (1.0 / d)
        dw_ref[...] += jnp.sum(dy * xh, axis=0, keepdims=True)
        dxh = dy * wv
        dx = r * (dxh - xh * jnp.mean(dxh * xh, axis=-1, keepdims=True))
        dx_ref[...] = dx
        dx16_ref[...] = dx.astype(BF16)

    return _pcall(body, name="final_loss", grid=(s // ts,),
                  in_specs=[_rows(ts, d), _rows(ts, d), _full((1, d))],
                  out_specs=[_rows(ts, d), _full((1, d)), _full((1, 1)), _rows(ts, d)],
                  out_shape=[SDS((s, d), F32), SDS((1, d), F32), SDS((1, 1), F32), SDS((s, d), BF16)])(
                      x3, tgt, w)


def _shift_down(cur, halo, s):
    if s == 0:
        return cur
    row8 = lax.broadcasted_iota(jnp.int32, halo.shape, 0)
    r = pltpu.roll(cur, s, 0)
    top = jnp.where(row8 < s, pltpu.roll(halo, s, 0), r[0:8])
    return jnp.concatenate([top, r[8:]], axis=0)


def _shift_up(cur, halo, s):
    if s == 0:
        return cur
    ts = cur.shape[0]
    row8 = lax.broadcasted_iota(jnp.int32, halo.shape, 0)
    r = pltpu.roll(cur, ts - s, 0)
    bot = jnp.where(row8 >= 8 - s, pltpu.roll(halo, 8 - s, 0), r[ts - 8:ts])
    return jnp.concatenate([r[:ts - 8], bot], axis=0)


def _chunk_tri(ts, upper):
    i = lax.broadcasted_iota(jnp.int32, (ts, ts), 0)
    j = lax.broadcasted_iota(jnp.int32, (ts, ts), 1)
    same = jnp.right_shift(i, 6) == jnp.right_shift(j, 6)
    return jnp.where(same & ((j >= i) if upper else (j <= i)), 1.0, 0.0).astype(F32)


def _gate_values(m, alog, dtb):
    lane = _lane_iota(m.shape)
    beta = _sigmoid(m)
    xg = m + dtb
    sp = jnp.maximum(xg, 0.0) + jnp.log(1.0 + jnp.exp(-jnp.abs(xg)))
    ga = (lane >= A_LANE) & (lane < A_LANE + 8)
    g = jnp.where(ga, -jnp.exp(alog) * sp, 0.0)
    return beta, g, xg, ga


def _l2_heads(a, nh, scale):
    outs, rs = [], []
    for h in range(nh):
        ah = a[:, HEAD * h:HEAD * (h + 1)]
        r = lax.rsqrt(jnp.sum(ah * ah, axis=-1, keepdims=True) + EPS)
        outs.append(ah * (r * scale))
        rs.append(r)
    return jnp.concatenate(outs, axis=-1), rs


def _gdn_prep(proj, conv_w, alog_l, dtb_l, nh, misc_col):
    s = proj.shape[0]
    w = nh * HEAD
    ts = _tile(s, 256, PAIR)
    hb = ts // 8

    def body(cur_ref, halo_ref, misc_ref, cw_ref, al_ref, db_ref, q_ref, k_ref, v_ref, gb_ref, gbt_ref):
        first = pl.program_id(0) == 0
        outs = (q_ref, k_ref, v_ref)
        for sec in range(3):
            cs = slice(sec * w, (sec + 1) * w)
            cur = cur_ref[:, cs]
            halo = jnp.where(first, 0.0, halo_ref[:, cs])
            pre = None
            for j in range(CONV):
                term = cw_ref[j:j + 1, cs] * _shift_down(cur, halo, CONV - 1 - j)
                pre = term if pre is None else pre + term
            act = _silu(pre)
            if sec == 0:
                act, _ = _l2_heads(act, nh, HEAD ** -0.5)
            elif sec == 1:
                act, _ = _l2_heads(act, nh, 1.0)
            outs[sec][...] = act
        m = misc_ref[...]
        lane = _lane_iota(m.shape)
        beta, g, _, ga = _gate_values(m, al_ref[...], db_ref[...])
        gcc = jnp.dot(_chunk_tri(ts, False), g, precision=lax.Precision.HIGHEST,
                      preferred_element_type=F32)
        gb = jnp.where((lane >= B_LANE) & (lane < B_LANE + 8), beta, jnp.where(ga, gcc, 0.0))
        gb_ref[...] = gb
        gbt_ref[...] = gb.T

    return _pcall(
        body, name="gdn_prep", grid=(s // ts,),
        in_specs=[_rows(ts, 3 * w),
                  pl.BlockSpec((8, 3 * w), lambda i: (jnp.maximum(i * hb - 1, 0), 0)),
                  _rows(ts, LANE, misc_col), _full((CONV, 3 * w)), _full((1, LANE)), _full((1, LANE))],
        out_specs=[_rows(ts, w), _rows(ts, w), _rows(ts, w), _rows(ts, LANE),
                   pl.BlockSpec((LANE, ts), lambda i: (0, i))],
        out_shape=[SDS((s, w), F32), SDS((s, w), F32), SDS((s, w), F32), SDS((s, LANE), F32),
                   SDS((LANE, s), F32)])(proj, proj, proj, conv_w, alog_l, dtb_l)


def _gdn_prep_bwd(proj, conv_w, alog_l, dtb_l, dq, dk, dv, dgb, dkr, dproj, nh, misc_col):
    s = proj.shape[0]
    w = nh * HEAD
    ts = _tile(s, 256, PAIR)
    hb = ts // 8
    assert misc_col % 2 == 0

    def body(cur_ref, halo_ref, misc_ref, cw_ref, al_ref, db_ref, dq_ref, dk_ref, dv_ref, dgb_ref,
             dkr_ref, _, dc_ref, dm_ref, dcw_ref, dal_ref, ddb_ref):
        first = pl.program_id(0) == 0

        @pl.when(first)
        def _():
            dcw_ref[...] = jnp.zeros_like(dcw_ref)
            dal_ref[...] = jnp.zeros_like(dal_ref)
            ddb_ref[...] = jnp.zeros_like(ddb_ref)

        dins = (dq_ref, dk_ref, dv_ref)
        for sec in range(3):
            cs = slice(sec * w, (sec + 1) * w)
            cur = cur_ref[:, cs]
            halo = jnp.where(first, 0.0, halo_ref[:, cs])
            us = [_shift_down(cur, halo, CONV - 1 - j) for j in range(CONV)]
            pre = None
            for j in range(CONV):
                term = cw_ref[j:j + 1, cs] * us[j]
                pre = term if pre is None else pre + term
            act = _silu(pre)
            dout = dins[sec][...]
            if sec < 2:
                scale = HEAD ** -0.5 if sec == 0 else 1.0
                parts = []
                for h in range(nh):
                    hs = slice(HEAD * h, HEAD * (h + 1))
                    ah = act[:, hs]
                    r = lax.rsqrt(jnp.sum(ah * ah, axis=-1, keepdims=True) + EPS)
                    ahat = ah * r
                    dy = dout[:, hs]
                    parts.append((scale * r) * (dy - ahat * jnp.sum(dy * ahat, axis=-1, keepdims=True)))
                dact = jnp.concatenate(parts, axis=-1)
            else:
                dact = dout
            dconv = dact * _dsilu(pre)
            dc_ref[:, cs] = dconv
            for j in range(CONV):
                dcw_ref[j:j + 1, cs] += jnp.sum(dconv * us[j], axis=0, keepdims=True)
        m = misc_ref[...]
        lane = _lane_iota(m.shape)
        al = al_ref[...]
        beta, g, xg, ga = _gate_values(m, al, db_ref[...])
        dgbv = dgb_ref[...]
        dg = jnp.dot(_chunk_tri(ts, True), jnp.where(ga, dgbv, 0.0), precision=lax.Precision.HIGHEST,
                     preferred_element_type=F32)
        da_raw = jnp.where(ga, dg * (-jnp.exp(al)) * _sigmoid(xg), 0.0)
        db_raw = jnp.where((lane >= B_LANE) & (lane < B_LANE + 8), dgbv * beta * (1.0 - beta), 0.0)
        dal_ref[...] += jnp.sum(dg * g, axis=0, keepdims=True)
        ddb_ref[...] += jnp.sum(da_raw, axis=0, keepdims=True)
        dm_ref[:, :LANE] = (dkr_ref[...] + da_raw + db_raw).astype(BF16)
        dm_ref[:, LANE:] = jnp.zeros((ts, LANE), BF16)

    return _pcall(
        body, name="gdn_prep_bwd", grid=(s // ts,),
        in_specs=[_rows(ts, 3 * w),
                  pl.BlockSpec((8, 3 * w), lambda i: (jnp.maximum(i * hb - 1, 0), 0)),
                  _rows(ts, LANE, misc_col), _full((CONV, 3 * w)), _full((1, LANE)), _full((1, LANE)),
                  _rows(ts, w), _rows(ts, w), _rows(ts, w), _rows(ts, LANE), _rows(ts, LANE), _ANY],
        out_specs=[_rows(ts, 3 * w), _rows(ts, 2 * LANE, misc_col // 2), _full((CONV, 3 * w)), _full((1, LANE)),
                   _full((1, LANE))],
        out_shape=[SDS((s, 3 * w), F32), SDS(dproj.shape, BF16), SDS((CONV, 3 * w), F32),
                   SDS((1, LANE), F32), SDS((1, LANE), F32)], aliases={11: 1})(
                       proj, proj, proj, conv_w, alog_l, dtb_l, dq, dk, dv, dgb, dkr, dproj)


def _conv_bwd_input(dconv, conv_w, dproj):
    s, c = dconv.shape
    ts = _tile(s, 256, 8)
    hb = ts // 8
    nblk8 = s // 8
    nt = s // ts

    def body(cur_ref, nxt_ref, cw_ref, _, o_ref):
        last = pl.program_id(0) == nt - 1
        cur = cur_ref[...]
        halo = jnp.where(last, 0.0, nxt_ref[...])
        acc = None
        for j in range(CONV):
            term = cw_ref[j:j + 1, :] * _shift_up(cur, halo, CONV - 1 - j)
            acc = term if acc is None else acc + term
        o_ref[...] = acc.astype(BF16)

    return _pcall(
        body, name="conv_bwd_input", grid=(nt,),
        in_specs=[_rows(ts, c),
                  pl.BlockSpec((8, c), lambda i: (jnp.minimum((i + 1) * hb, nblk8 - 1), 0)),
                  _full((CONV, c)), _ANY],
        out_specs=_rows(ts, c), out_shape=SDS(dproj.shape, BF16), aliases={3: 0})(
            dconv, dconv, conv_w, dproj)


def _inv_unit_lower(a):
    n = a[0].shape[0]
    i = lax.broadcasted_iota(jnp.int32, (n, n), 0)
    j = lax.broadcasted_iota(jnp.int32, (n, n), 1)
    eye = jnp.where(i == j, 1.0, 0.0)
    t = [eye - ah for ah in a]
    x = a
    for _ in range(5):
        x = [_dot(xh, xh) for xh in x]
        t = [th + _dot(th, xh) for th, xh in zip(t, x)]
    return t


def _pair_common(q, k, gcol, grow, bcol):
    i = lax.broadcasted_iota(jnp.int32, (PAIR, PAIR), 0)
    j = lax.broadcasted_iota(jnp.int32, (PAIR, PAIR), 1)
    same = jnp.right_shift(i, 6) == jnp.right_shift(j, 6)
    tril = same & (i >= j)
    strict = same & (i > j)
    dec = [jnp.where(tril, jnp.exp(jnp.minimum(gc - gr, 0.0)), 0.0) for gc, gr in zip(gcol, grow)]
    kk = [_dot_nt(kh, kh) for kh in k]
    qk = [_dot_nt(qh, kh) for qh, kh in zip(q, k)]
    a = [jnp.where(strict, b * kkh * d, 0.0) for b, kkh, d in zip(bcol, kk, dec)]
    t = _inv_unit_lower(a)
    p = [qkh * d for qkh, d in zip(qk, dec)]
    return dec, kk, a, t, p, tril, strict


def _ext(v, a):
    z = jnp.zeros_like(v)
    return jnp.concatenate([v, z] if a == 0 else [z, v], axis=0)


def _gdn_fwd(q, k, v, gb, gbt, nh, rider):
    s = q.shape[0]
    w = nh * HEAD
    npair = s // PAIR

    def body(q_ref, k_ref, v_ref, gb_ref, gbt_ref, o_ref, st_ref, s_ref):
        @pl.when(pl.program_id(0) == 0)
        def _():
            s_ref[...] = jnp.zeros_like(s_ref)

        heads = range(nh)
        hs = [slice(HEAD * h, HEAD * (h + 1)) for h in heads]
        gbv = gb_ref[...]
        q, k, v = [q_ref[:, s_] for s_ in hs], [k_ref[:, s_] for s_ in hs], [v_ref[:, s_] for s_ in hs]
        gcol = [_col(gbv, A_LANE + h) for h in heads]
        bcol = [_col(gbv, B_LANE + h) for h in heads]
        grow = [gbt_ref[A_LANE + h:A_LANE + h + 1, :] for h in heads]
        _, _, _, t, p, _, _ = _pair_common(q, k, gcol, grow, bcol)
        eg = [jnp.exp(gc) for gc in gcol]
        qg = [x * e for x, e in zip(q, eg)]
        kg = [x * e for x, e in zip(k, eg)]
        outs = []
        for a in range(2):
            sl = slice(CHUNK * a, CHUNK * (a + 1))
            st = [s_ref[h] for h in heads]
            for h in heads:
                st_ref[a, h] = st[h]
            r = [v[h][sl] - _dot(kg[h][sl], st[h]) for h in heads]
            vn = [_dot(t[h][sl], _ext(bcol[h][sl] * r[h], a)) for h in heads]
            outs.append([_dot(qg[h][sl], st[h]) + _dot(p[h][sl], _ext(vn[h], a)) for h in heads])
            gl = [_col(gr, CHUNK * (a + 1) - 1) for gr in grow]
            kd = [k[h][sl] * jnp.exp(gl[h] - gcol[h][sl]) for h in heads]
            upd = [_dot_tn(kd[h], vn[h]) for h in heads]
            for h in heads:
                s_ref[h] = jnp.exp(gl[h]) * st[h] + upd[h]
        for h in heads:
            o_ref[:, hs[h]] = jnp.concatenate([outs[0][h], outs[1][h]], axis=0)

    return _pcall_riding(
        body, rider, name="gdn_fwd", grid=(npair,),
        in_specs=[_rows(PAIR, w), _rows(PAIR, w), _rows(PAIR, w), _rows(PAIR, LANE),
                  pl.BlockSpec((LANE, PAIR), lambda i: (0, i))],
        out_specs=[_rows(PAIR, w), pl.BlockSpec((2, nh, HEAD, HEAD), lambda i: (i, 0, 0, 0))],
        out_shape=[SDS((s, w), F32), SDS((2 * npair, nh, HEAD, HEAD), F32)],
        scratch=[pltpu.VMEM((nh, HEAD, HEAD), F32)], args=(q, k, v, gb, gbt))


def _gdn_bwd(q, k, v, gb, gbt, states, do, nh, rider):
    s = q.shape[0]
    w = nh * HEAD
    npair = s // PAIR
    rev = lambda i: (npair - 1 - i, 0)

    def body(q_ref, k_ref, v_ref, gb_ref, gbt_ref, st_ref, do_ref, dq_ref, dk_ref, dv_ref, dgb_ref,
             ds_ref):
        @pl.when(pl.program_id(0) == 0)
        def _():
            ds_ref[...] = jnp.zeros_like(ds_ref)

        lane = _lane_iota((PAIR, LANE))
        row = lax.broadcasted_iota(jnp.int32, (CHUNK, 1), 0)
        heads = range(nh)
        hs = [slice(HEAD * h, HEAD * (h + 1)) for h in heads]
        gbv = gb_ref[...]
        q, k, v = [q_ref[:, s_] for s_ in hs], [k_ref[:, s_] for s_ in hs], [v_ref[:, s_] for s_ in hs]
        do = [do_ref[:, s_] for s_ in hs]
        gcol = [_col(gbv, A_LANE + h) for h in heads]
        bcol = [_col(gbv, B_LANE + h) for h in heads]
        grow = [gbt_ref[A_LANE + h:A_LANE + h + 1, :] for h in heads]
        dec, kk, amat, t, p, tril, strict = _pair_common(q, k, gcol, grow, bcol)
        tt, pt = [x.T for x in t], [x.T for x in p]
        eg = [jnp.exp(gc) for gc in gcol]
        qg = [x * e for x, e in zip(q, eg)]
        kg = [x * e for x, e in zip(k, eg)]
        sums = lambda x: jnp.sum(x, axis=-1, keepdims=True)
        rs, vns = [None, None], [None, None]
        for a in range(2):
            sl = slice(CHUNK * a, CHUNK * (a + 1))
            rs[a] = [v[h][sl] - _dot(kg[h][sl], st_ref[a, h]) for h in heads]
            vns[a] = [_dot(t[h][sl], _ext(bcol[h][sl] * rs[a][h], a)) for h in heads]
        dsn = [ds_ref[h] for h in heads]
        dqs, dks, dvs, dgcs, dbs, drbs = ([None, None] for _ in range(6))
        for a in (1, 0):
            sl = slice(CHUNK * a, CHUNK * (a + 1))
            st = [st_ref[a, h] for h in heads]
            gl = [_col(gr, CHUNK * (a + 1) - 1) for gr in grow]
            egl = [jnp.exp(x) for x in gl]
            dk_dec = [jnp.exp(gl[h] - gcol[h][sl]) for h in heads]
            kd = [k[h][sl] * dk_dec[h] for h in heads]
            d_vn = [_dot(pt[h][sl], _ext(do[h][sl], a)) + _dot(kd[h], dsn[h]) for h in heads]
            d_qg = [_dot_nt(do[h][sl], st[h]) for h in heads]
            d_rb = [_dot(tt[h][sl], _ext(d_vn[h], a)) for h in heads]
            d_r = [bcol[h][sl] * d_rb[h] for h in heads]
            d_kg = [-_dot_nt(d_r[h], st[h]) for h in heads]
            d_kd = [_dot_nt(vns[a][h], dsn[h]) for h in heads]
            dsn_new = [_dot_tn(qg[h][sl], do[h][sl]) - _dot_tn(kg[h][sl], d_r[h]) for h in heads]
            dbs[a] = [sums(d_rb[h] * rs[a][h]) for h in heads]
            dgl = [egl[h] * jnp.sum(dsn[h] * st[h], keepdims=True) + jnp.sum(d_kd[h] * kd[h], keepdims=True)
                   for h in heads]
            dgcs[a] = [sums(d_qg[h] * qg[h][sl]) + sums(d_kg[h] * kg[h][sl]) - sums(d_kd[h] * kd[h])
                       + jnp.where(row == CHUNK - 1, dgl[h], 0.0) for h in heads]
            dqs[a] = [d_qg[h] * eg[h][sl] for h in heads]
            dks[a] = [d_kg[h] * eg[h][sl] + d_kd[h] * dk_dec[h] for h in heads]
            dvs[a] = d_r
            drbs[a] = d_rb
            dsn = [dsn_new[h] + egl[h] * dsn[h] for h in heads]
        for h in heads:
            ds_ref[h] = dsn[h]
        cat = lambda xs, h: jnp.concatenate([xs[0][h], xs[1][h]], axis=0)
        vn = [cat(vns, h) for h in heads]
        d_rb = [cat(drbs, h) for h in heads]
        dp = [jnp.where(tril, _dot_nt(do[h], vn[h]), 0.0) for h in heads]
        dam = [jnp.where(strict, -_dot_nt(d_rb[h], vn[h]), 0.0) for h in heads]
        g_p = [dp[h] * dec[h] for h in heads]
        g_a = [dam[h] * dec[h] for h in heads]
        gbk = [bcol[h] * g_a[h] for h in heads]
        dq2 = [_dot(g_p[h], k[h]) for h in heads]
        dk2 = [_dot_tn(g_p[h], q[h]) + _dot(gbk[h], k[h]) + _dot_tn(gbk[h], k[h]) for h in heads]
        dgb = jnp.zeros((PAIR, LANE), F32)
        for h in heads:
            dq_ref[:, hs[h]] = cat(dqs, h) + dq2[h]
            dk_ref[:, hs[h]] = cat(dks, h) + dk2[h]
            dv_ref[:, hs[h]] = cat(dvs, h)
            dbeta = cat(dbs, h) + sums(g_a[h] * kk[h])
            mm = dp[h] * p[h] + dam[h] * amat[h]
            dgc = cat(dgcs, h) + sums(mm) - sums(mm.T)
            dgb = dgb + jnp.where(lane == A_LANE + h, dgc, 0.0) + jnp.where(lane == B_LANE + h, dbeta, 0.0)
        dgb_ref[...] = dgb

    return _pcall_riding(
        body, rider, name="gdn_bwd", grid=(npair,),
        in_specs=[pl.BlockSpec((PAIR, w), rev), pl.BlockSpec((PAIR, w), rev), pl.BlockSpec((PAIR, w), rev),
                  pl.BlockSpec((PAIR, LANE), rev),
                  pl.BlockSpec((LANE, PAIR), lambda i: (0, npair - 1 - i)),
                  pl.BlockSpec((2, nh, HEAD, HEAD), lambda i: (npair - 1 - i, 0, 0, 0)),
                  pl.BlockSpec((PAIR, w), rev)],
        out_specs=[pl.BlockSpec((PAIR, w), rev), pl.BlockSpec((PAIR, w), rev), pl.BlockSpec((PAIR, w), rev),
                   pl.BlockSpec((PAIR, LANE), rev)],
        out_shape=[SDS((s, w), F32), SDS((s, w), F32), SDS((s, w), F32), SDS((s, LANE), F32)],
        scratch=[pltpu.VMEM((nh, HEAD, HEAD), F32)], args=(q, k, v, gb, gbt, states, do))


def _mla_norm(proj, qw, kvw, col_q, col_kv):
    s = proj.shape[0]
    lr = qw.shape[1]
    ts = _tile(s, 512, 8)

    def body(cq_ref, ckv_ref, qw_ref, kvw_ref, oq_ref, okv_ref):
        for x_ref, w_ref, o_ref in ((cq_ref, qw_ref, oq_ref), (ckv_ref, kvw_ref, okv_ref)):
            xv = x_ref[...]
            r = lax.rsqrt(jnp.mean(xv * xv, axis=-1, keepdims=True) + EPS)
            o_ref[...] = (xv * r * w_ref[...]).astype(BF16)

    return _pcall(body, name="mla_norm", grid=(s // ts,),
                  in_specs=[_rows(ts, lr, col_q), _rows(ts, lr, col_kv), _full((1, lr)), _full((1, lr))],
                  out_specs=[_rows(ts, lr), _rows(ts, lr)],
                  out_shape=[SDS((s, lr), BF16), SDS((s, lr), BF16)])(proj, proj, qw, kvw)


def _mla_norm_bwd(proj, qw, kvw, dq, dkv, dproj, col_q, col_kv):
    s = proj.shape[0]
    lr = qw.shape[1]
    ts = _tile(s, 512, 8)

    assert col_kv == col_q + 1 and col_q % 2 == 0

    def body(cq_ref, ckv_ref, qw_ref, kvw_ref, dq_ref, dkv_ref, _, o_ref, dqw_ref, dkvw_ref):
        @pl.when(pl.program_id(0) == 0)
        def _():
            dqw_ref[...] = jnp.zeros_like(dqw_ref)
            dkvw_ref[...] = jnp.zeros_like(dkvw_ref)

        for k, (x_ref, w_ref, d_ref, dw_ref) in enumerate(((cq_ref, qw_ref, dq_ref, dqw_ref),
                                                           (ckv_ref, kvw_ref, dkv_ref, dkvw_ref))):
            xv, dh = x_ref[...], d_ref[...]
            r = lax.rsqrt(jnp.mean(xv * xv, axis=-1, keepdims=True) + EPS)
            xh = xv * r
            dw_ref[...] += jnp.sum(dh * xh, axis=0, keepdims=True)
            dxh = dh * w_ref[...]
            o_ref[:, lr * k:lr * (k + 1)] = (
                r * (dxh - xh * jnp.mean(dxh * xh, axis=-1, keepdims=True))).astype(BF16)

    return _pcall(body, name="mla_norm_bwd", grid=(s // ts,),
                  in_specs=[_rows(ts, lr, col_q), _rows(ts, lr, col_kv), _full((1, lr)), _full((1, lr)),
                            _rows(ts, lr), _rows(ts, lr), _ANY],
                  out_specs=[_rows(ts, 2 * lr, col_q // 2), _full((1, lr)), _full((1, lr))],
                  out_shape=[SDS(dproj.shape, BF16), SDS((1, lr), F32), SDS((1, lr), F32)],
                  aliases={6: 0})(proj, proj, qw, kvw, dq, dkv, dproj)


def _rope_tables(pos, invf, sgn):
    ang = pos * invf
    return jnp.cos(ang), jnp.sin(ang) * sgn


def _swap_halves_lanes(y):
    lane = _lane_iota(y.shape)
    return jnp.where(lane < ROPE // 2, pltpu.roll(y, LANE - ROPE // 2, 1), pltpu.roll(y, ROPE // 2, 1))


def _rope_consts():
    half = ROPE // 2
    inv = ROPE_THETA ** (-jnp.arange(half, dtype=F32) / half)
    invf = jnp.concatenate([inv, inv, jnp.zeros((LANE - ROPE,), F32)])[None, :]
    sgn = jnp.concatenate([-jnp.ones((half,), F32), jnp.ones((half,), F32),
                           jnp.zeros((LANE - ROPE,), F32)])[None, :]
    return invf, sgn


def _mla_rope(qraw, kvraw, proj, pos, nh, misc_col):
    s = qraw.shape[0]
    ts = _tile(s, MLA_BLOCK)
    wq = nh * 2 * HEAD
    invf, sgn = _rope_consts()

    def body(q_ref, kv_ref, misc_ref, pos_ref, if_ref, sg_ref, qc_ref, kc_ref, v_ref, qt_ref, vt_ref, kt_ref):
        c, sn = _rope_tables(pos_ref[...], if_ref[...], sg_ref[...])
        lane = _lane_iota(c.shape)
        rot = lambda xb: xb * c + _swap_halves_lanes(xb) * sn
        qs = SM_SCALE * LOG2E
        krot32 = jnp.where(lane < ROPE, rot(misc_ref[...]), 0.0)
        krot, krot_t = krot32.astype(BF16), krot32.T.astype(BF16)
        for h in range(nh):
            b0 = 2 * HEAD * h
            qn = q_ref[:, b0:b0 + HEAD].astype(F32) * qs
            qr = rot(q_ref[:, b0 + HEAD:b0 + 2 * HEAD].astype(F32)) * qs
            qc_ref[:, b0:b0 + HEAD] = qn.astype(BF16)
            qc_ref[:, b0 + HEAD:b0 + 2 * HEAD] = qr.astype(BF16)
            qt_ref[b0:b0 + HEAD, :] = qn.T.astype(BF16)
            qt_ref[b0 + HEAD:b0 + 2 * HEAD, :] = qr.T.astype(BF16)
            kn = kv_ref[:, b0:b0 + HEAD]
            kc_ref[:, b0:b0 + HEAD] = kn.astype(BF16)
            kc_ref[:, b0 + HEAD:b0 + 2 * HEAD] = krot
            kt_ref[b0:b0 + HEAD, :] = kn.astype(F32).T.astype(BF16)
            kt_ref[b0 + HEAD:b0 + 2 * HEAD, :] = krot_t
            vh = kv_ref[:, wq + HEAD * h:wq + HEAD * (h + 1)]
            v_ref[:, HEAD * h:HEAD * (h + 1)] = vh.astype(BF16)
            vt_ref[HEAD * h:HEAD * (h + 1), :] = vh.astype(F32).T.astype(BF16)

    return _pcall(body, name="mla_rope", grid=(s // ts,),
                  in_specs=[_rows(ts, wq), _rows(ts, wq + nh * HEAD), _rows(ts, LANE, misc_col),
                            _rows(ts, 1), _full((1, LANE)), _full((1, LANE))],
                  out_specs=[_rows(ts, wq), _rows(ts, wq), _rows(ts, nh * HEAD),
                             pl.BlockSpec((None, wq, ts), lambda i: (i, 0, 0)),
                             pl.BlockSpec((None, nh * HEAD, ts), lambda i: (i, 0, 0)),
                             pl.BlockSpec((None, wq, ts), lambda i: (i, 0, 0))],
                  out_shape=[SDS((s, wq), BF16), SDS((s, wq), BF16), SDS((s, nh * HEAD), BF16),
                             SDS((s // ts, wq, ts), BF16), SDS((s // ts, nh * HEAD, ts), BF16),
                             SDS((s // ts, wq, ts), BF16)])(
                      qraw, kvraw, proj, pos, invf, sgn)


def _mla_rope_bwd(dqt, dkc, dv, pos, nh):
    nb, wq, ts = dqt.shape
    s = nb * ts
    invf, sgn = _rope_consts()

    def body(dq_ref, dk_ref, dv_ref, pos_ref, if_ref, sg_ref, oq_ref, okv_ref, okr_ref):
        c, sn = _rope_tables(pos_ref[...], if_ref[...], sg_ref[...])
        lane = _lane_iota(c.shape)
        unrot = lambda d: d * c + _swap_halves_lanes(d * sn)
        dkr = jnp.zeros(c.shape, F32)
        for h in range(nh):
            b0 = 2 * HEAD * h
            oq_ref[:, b0:b0 + HEAD] = (dq_ref[b0:b0 + HEAD, :].T * SM_SCALE).astype(BF16)
            oq_ref[:, b0 + HEAD:b0 + 2 * HEAD] = (
                unrot(dq_ref[b0 + HEAD:b0 + 2 * HEAD, :].T) * SM_SCALE).astype(BF16)
            okv_ref[:, b0:b0 + HEAD] = (dk_ref[:, b0:b0 + HEAD] * LN2).astype(BF16)
            okv_ref[:, b0 + HEAD:b0 + 2 * HEAD] = jnp.zeros((ts, HEAD), BF16)
            dkr = dkr + dk_ref[:, b0 + HEAD:b0 + 2 * HEAD]
        okv_ref[:, wq:] = dv_ref[...].astype(BF16)
        okr_ref[...] = jnp.where(lane < ROPE, unrot(jnp.where(lane < ROPE, dkr * LN2, 0.0)), 0.0)

    return _pcall(body, name="mla_rope_bwd", grid=(s // ts,),
                  in_specs=[pl.BlockSpec((None, wq, ts), lambda i: (i, 0, 0)), _rows(ts, wq), _rows(ts, nh * HEAD),
                            _rows(ts, 1), _full((1, LANE)), _full((1, LANE))],
                  out_specs=[_rows(ts, wq), _rows(ts, wq + nh * HEAD), _rows(ts, LANE)],
                  out_shape=[SDS((s, wq), BF16), SDS((s, wq + nh * HEAD), BF16), SDS((s, LANE), F32)])(
                      dqt, dkc, dv, pos, invf, sgn)


MLA_HP = 2
MLA_FWD_HP = 4


def _mla_fwd(qt, kc, vt, nh, rider):
    nb, _, blk = qt.shape
    s = nb * blk
    hp = MLA_FWD_HP if nh % MLA_FWD_HP == 0 else MLA_HP
    assert nh % hp == 0 and hp % MLA_HP == 0
    once = pl.Buffered(1)
    r_in, r_out = len(rider.arrays), len(rider.out_shapes)

    def body(*refs):
        qt_ref, k_ref, vt_ref = refs[:3]
        o_ref, lse_ref = refs[3 + r_in:5 + r_in]
        m_sc, l_sc, acc = refs[5 + r_in + r_out:8 + r_in + r_out]
        r_refs = (refs[3:3 + r_in], refs[5 + r_in:5 + r_in + r_out], refs[8 + r_in + r_out:])
        i = pl.program_id(1)
        grid_step = pl.program_id(0) * nb + i
        _ride_begin(rider, r_refs, grid_step)
        m_sc[...] = jnp.full_like(m_sc, -1e30)
        l_sc[...] = jnp.zeros_like(l_sc)
        acc[...] = jnp.zeros_like(acc)
        es = range(hp)

        def step(j, masked):
            rows = pl.ds(pl.multiple_of(j * blk, blk), blk)
            sc = [_dot(k_ref[rows, 2 * HEAD * e:2 * HEAD * (e + 1)], qt_ref[2 * HEAD * e:2 * HEAD * (e + 1), :])
                  for e in es]
            if masked:
                key = lax.broadcasted_iota(jnp.int32, (blk, blk), 0)
                qry = lax.broadcasted_iota(jnp.int32, (blk, blk), 1)
                sc = [jnp.where(key <= qry, x, -1e30) for x in sc]
            m_prev = [m_sc[e] for e in es]
            m_new = [jnp.maximum(m_prev[e], jnp.max(sc[e], axis=0, keepdims=True)) for e in es]
            p = [jnp.exp2(sc[e] - m_new[e]) for e in es]
            alpha = [jnp.exp2(m_prev[e] - m_new[e]) for e in es]
            pv = [_dot(vt_ref[j, HEAD * e:HEAD * (e + 1), :], p[e]) for e in es]
            for e in es:
                l_sc[e] = alpha[e] * l_sc[e] + jnp.sum(p[e], axis=0, keepdims=True)
                acc[e] = alpha[e] * acc[e] + pv[e]
                m_sc[e] = m_new[e]

        def loop_body(j, carry):
            step(j, False)
            return carry

        lax.fori_loop(0, i, loop_body, 0)
        step(i, True)
        for e in es:
            o_ref[:, HEAD * e:HEAD * (e + 1)] = (acc[e] / l_sc[e]).T
            lse_ref[e] = jnp.broadcast_to(m_sc[e] + jnp.log(l_sc[e]) * LOG2E, (8, blk))
        _ride_end(rider, r_refs, grid_step, (nh // hp) * nb)

    outs = _pcall(
        body, name="mla_fwd", grid=(nh // hp, nb),
        in_specs=[pl.BlockSpec((None, hp * 2 * HEAD, blk), lambda g, i: (i, g, 0)),
                  pl.BlockSpec((s, hp * 2 * HEAD), lambda g, i: (0, g), pipeline_mode=once),
                  pl.BlockSpec((nb, hp * HEAD, blk), lambda g, i: (0, g, 0), pipeline_mode=once)]
        + [_ANY] * r_in,
        out_specs=[pl.BlockSpec((blk, hp * HEAD), lambda g, i: (i, g)),
                   pl.BlockSpec((hp, None, 8, blk), lambda g, i: (g, i, 0, 0))] + [_ANY] * r_out,
        out_shape=[SDS((s, nh * HEAD), F32), SDS((nh, nb, 8, blk), F32)] + rider.out_shapes,
        scratch=[pltpu.VMEM((hp, 1, blk), F32), pltpu.VMEM((hp, 1, blk), F32),
                 pltpu.VMEM((hp, HEAD, blk), F32)] + rider.scratch)(qt, kc, vt, *rider.arrays)
    return outs[0], outs[1], outs[2:]


def _mla_bwd(qc, qt, kc, kt, v, do, dot, lse, delta, nh, rider):
    nb, _, blk = qt.shape
    s = nb * blk
    hp = MLA_HP
    once = pl.Buffered(1)
    r_in, r_out = len(rider.arrays), len(rider.out_shapes)
    qs = [slice(2 * HEAD * e, 2 * HEAD * (e + 1)) for e in range(hp)]
    vs = [slice(HEAD * e, HEAD * (e + 1)) for e in range(hp)]

    def body(*refs):
        q_ref, qt_ref, do_ref, dot_ref, lse_ref, dl_ref, k_ref, kt_ref, v_ref = refs[:9]
        dqt_ref, dk_ref, dv_ref = refs[9 + r_in:12 + r_in]
        dk_acc, dv_acc = refs[12 + r_in + r_out:14 + r_in + r_out]
        r_refs = (refs[9:9 + r_in], refs[12 + r_in:12 + r_in + r_out], refs[14 + r_in + r_out:])
        j = pl.program_id(1)
        grid_step = pl.program_id(0) * nb + j
        _ride_begin(rider, r_refs, grid_step)

        @pl.when(j == 0)
        def _():
            dqt_ref[...] = jnp.zeros_like(dqt_ref)

        dk_acc[...] = jnp.zeros_like(dk_acc)
        dv_acc[...] = jnp.zeros_like(dv_acc)
        es = range(hp)
        kj = [k_ref[:, qs[e]] for e in es]
        ktj = [kt_ref[qs[e], :] for e in es]
        vj = [v_ref[:, vs[e]] for e in es]

        def step(i, masked):
            rows = pl.ds(pl.multiple_of(i * blk, blk), blk)
            sc = [_dot(kj[e], qt_ref[i, qs[e], :]) for e in es]
            dp = [_dot(vj[e], dot_ref[i, vs[e], :]) for e in es]
            if masked:
                key = lax.broadcasted_iota(jnp.int32, (blk, blk), 0)
                qry = lax.broadcasted_iota(jnp.int32, (blk, blk), 1)
                sc = [jnp.where(key <= qry, x, -1e30) for x in sc]
            p = [jnp.exp2(sc[e] - lse_ref[e, i, 0:1, :]) for e in es]
            ds = [p[e] * (dp[e] - dl_ref[e, i, 0:1, :]) for e in es]
            dv = [_dot(p[e], do_ref[rows, vs[e]]) for e in es]
            dk = [_dot(ds[e], q_ref[rows, qs[e]]) for e in es]
            dq = [_dot(ktj[e], ds[e]) for e in es]
            for e in es:
                dv_acc[:, vs[e]] += dv[e]
                dk_acc[:, qs[e]] += dk[e]
                dqt_ref[i, qs[e], :] += dq[e]

        def loop_body(i, carry):
            step(i, False)
            return carry

        step(j, True)
        lax.fori_loop(j + 1, nb, loop_body, 0)
        dk_ref[...] = dk_acc[...]
        dv_ref[...] = dv_acc[...]
        _ride_end(rider, r_refs, grid_step, (nh // hp) * nb)

    rows_spec = pl.BlockSpec((hp, nb, 8, blk), lambda g, j: (g, 0, 0, 0), pipeline_mode=once)
    outs = _pcall(
        body, name="mla_bwd", grid=(nh // hp, nb),
        in_specs=[pl.BlockSpec((s, hp * 2 * HEAD), lambda g, j: (0, g), pipeline_mode=once),
                  pl.BlockSpec((nb, hp * 2 * HEAD, blk), lambda g, j: (0, g, 0), pipeline_mode=once),
                  pl.BlockSpec((s, hp * HEAD), lambda g, j: (0, g), pipeline_mode=once),
                  pl.BlockSpec((nb, hp * HEAD, blk), lambda g, j: (0, g, 0), pipeline_mode=once),
                  rows_spec, rows_spec,
                  pl.BlockSpec((blk, hp * 2 * HEAD), lambda g, j: (j, g)),
                  pl.BlockSpec((None, hp * 2 * HEAD, blk), lambda g, j: (j, g, 0)),
                  pl.BlockSpec((blk, hp * HEAD), lambda g, j: (j, g))] + [_ANY] * r_in,
        out_specs=[pl.BlockSpec((nb, hp * 2 * HEAD, blk), lambda g, j: (0, g, 0), pipeline_mode=once),
                   pl.BlockSpec((blk, hp * 2 * HEAD), lambda g, j: (j, g)),
                   pl.BlockSpec((blk, hp * HEAD), lambda g, j: (j, g))] + [_ANY] * r_out,
        out_shape=[SDS((nb, nh * 2 * HEAD, blk), F32), SDS((s, nh * 2 * HEAD), F32),
                   SDS((s, nh * HEAD), F32)] + rider.out_shapes,
        scratch=[pltpu.VMEM((blk, hp * 2 * HEAD), F32), pltpu.VMEM((blk, hp * HEAD), F32)] + rider.scratch,
        vmem=VMEM_LIMIT_WIDE)(qc, qt, do, dot, lse, delta, kc, kt, v, *rider.arrays)
    return outs[0], outs[1], outs[2], outs[3:]


def _mix_fwd(og, proj, om, gw, mw, nh, z_col):
    s = og.shape[0]
    w = nh * HEAD
    ts = _tile(s, 256, 8)

    def body(og_ref, z_ref, om_ref, gw_ref, mw_ref, o_ref):
        for h in range(nh):
            hs = slice(HEAD * h, HEAD * (h + 1))
            a = og_ref[:, hs]
            r = lax.rsqrt(jnp.mean(a * a, axis=-1, keepdims=True) + EPS)
            o_ref[:, hs] = (a * r * gw_ref[...] * _silu(z_ref[:, hs])).astype(BF16)
            b = om_ref[:, hs]
            r = lax.rsqrt(jnp.mean(b * b, axis=-1, keepdims=True) + EPS)
            o_ref[:, w + HEAD * h:w + HEAD * (h + 1)] = (b * r * mw_ref[...]).astype(BF16)

    return _pcall(body, name="mix_fwd", grid=(s // ts,),
                  in_specs=[_rows(ts, w), _rows(ts, w, z_col), _rows(ts, w), _full((1, HEAD)),
                            _full((1, HEAD))],
                  out_specs=_rows(ts, 2 * w), out_shape=SDS((s, 2 * w), BF16))(og, proj, om, gw, mw)


def _mix_bwd(dmix, og, proj, om, gw, mw, nh, z_col):
    s = og.shape[0]
    w = nh * HEAD
    ts = _tile(s, MLA_BLOCK)

    def body(d_ref, og_ref, z_ref, om_ref, gw_ref, mw_ref, dog_ref, dz_ref, dom_ref, dgw_ref, dmw_ref,
             dl_ref, domt_ref):
        @pl.when(pl.program_id(0) == 0)
        def _():
            dgw_ref[...] = jnp.zeros_like(dgw_ref)
            dmw_ref[...] = jnp.zeros_like(dmw_ref)

        dgw = jnp.zeros((1, HEAD), F32)
        dmw = jnp.zeros((1, HEAD), F32)
        for h in range(nh):
            hs = slice(HEAD * h, HEAD * (h + 1))
            a, z, dy = og_ref[:, hs], z_ref[:, hs], d_ref[:, hs]
            r = lax.rsqrt(jnp.mean(a * a, axis=-1, keepdims=True) + EPS)
            ah = a * r
            sz = _silu(z)
            dz_ref[:, hs] = (dy * (ah * gw_ref[...]) * _dsilu(z)).astype(BF16)
            dn = dy * sz
            dgw = dgw + jnp.sum(dn * ah, axis=0, keepdims=True)
            dah = dn * gw_ref[...]
            dog_ref[:, hs] = r * (dah - ah * jnp.mean(dah * ah, axis=-1, keepdims=True))
            b, dyb = om_ref[:, hs], d_ref[:, w + HEAD * h:w + HEAD * (h + 1)]
            r = lax.rsqrt(jnp.mean(b * b, axis=-1, keepdims=True) + EPS)
            bh = b * r
            dmw = dmw + jnp.sum(dyb * bh, axis=0, keepdims=True)
            dbh = dyb * mw_ref[...]
            dom = r * (dbh - bh * jnp.mean(dbh * bh, axis=-1, keepdims=True))
            dom_ref[:, hs] = dom.astype(BF16)
            domt_ref[hs, :] = dom.T.astype(BF16)
            delta = jnp.broadcast_to(jnp.sum(dom * b, axis=-1, keepdims=True), (ts, LANE))
            dl_ref[h] = delta.T[0:8, :]
        dgw_ref[...] += dgw
        dmw_ref[...] += dmw

    return _pcall(body, name="mix_bwd", grid=(s // ts,),
                  in_specs=[_rows(ts, 2 * w), _rows(ts, w), _rows(ts, w, z_col), _rows(ts, w),
                            _full((1, HEAD)), _full((1, HEAD))],
                  out_specs=[_rows(ts, w), _rows(ts, w, z_col), _rows(ts, w), _full((1, HEAD)), _full((1, HEAD)),
                             pl.BlockSpec((nh, None, 8, ts), lambda i: (0, i, 0, 0)),
                             pl.BlockSpec((None, w, ts), lambda i: (i, 0, 0))],
                  out_shape=[SDS((s, w), F32), SDS((s, proj.shape[1]), BF16), SDS((s, w), BF16),
                             SDS((1, HEAD), F32), SDS((1, HEAD), F32),
                             SDS((nh, s // ts, 8, ts), F32), SDS((s // ts, w, ts), BF16)])(
                                 dmix, og, proj, om, gw, mw)


def _swiglu_fwd(h2, wg, wu):
    m, kdim = h2.shape
    tn = wg.shape[2]
    n = 4 * tn
    tm, tk = _tile(m, 512), _tile(kdim, 2048)
    nk = kdim // tk

    def body(a_ref, g_ref, u_ref, act_ref, go_ref, uo_ref, gacc, uacc):
        k = pl.program_id(2)

        @pl.when(k == 0)
        def _():
            gacc[...] = jnp.zeros_like(gacc)
            uacc[...] = jnp.zeros_like(uacc)

        a = a_ref[...]
        gacc[...] += _dot(a, g_ref[...])
        uacc[...] += _dot(a, u_ref[...])

        @pl.when(k == nk - 1)
        def _():
            g, u = gacc[...], uacc[...]
            act_ref[...] = (_silu(g) * u).astype(BF16)
            go_ref[...] = g.astype(BF16)
            uo_ref[...] = u.astype(BF16)

    a_spec = pl.BlockSpec((tm, tk), lambda i, j, k: (i, k))
    b_spec = pl.BlockSpec((None, tk, tn), lambda i, j, k: (j, k, 0))
    o_spec = pl.BlockSpec((tm, tn), lambda i, j, k: (i, j))
    return _pcall(body, name="swiglu_fwd", grid=(m // tm, n // tn, nk),
                  in_specs=[a_spec, b_spec, b_spec], out_specs=[o_spec] * 3,
                  out_shape=[SDS((m, n), BF16)] * 3,
                  scratch=[pltpu.VMEM((tm, tn), F32), pltpu.VMEM((tm, tn), F32)])(h2, wg, wu)


def _swiglu_bwd(dx3, wd, g, u):
    m, kdim = dx3.shape
    n = wd.shape[0]
    tm, tn = _tile(m, 1024), _tile(n, 512)

    def body(a_ref, b_ref, g_ref, u_ref, dg_ref, du_ref):
        da = _dot_nt(a_ref[...], b_ref[...])
        gv, uv = g_ref[...].astype(F32), u_ref[...].astype(F32)
        dg_ref[...] = (da * uv * _dsilu(gv)).astype(BF16)
        du_ref[...] = (da * _silu(gv)).astype(BF16)

    a_spec = pl.BlockSpec((tm, kdim), lambda i, j: (i, 0))
    b_spec = pl.BlockSpec((tn, kdim), lambda i, j: (j, 0))
    o_spec = pl.BlockSpec((tm, tn), lambda i, j: (i, j))
    return _pcall(body, name="swiglu_bwd", grid=(m // tm, n // tn),
                  in_specs=[a_spec, b_spec, o_spec, o_spec], out_specs=[o_spec] * 2,
                  out_shape=[SDS((m, n), BF16)] * 2)(dx3, wd, g, u)


def _sum_pair(g, recv, place, name):
    _, _, rh, c = g.shape
    tr = _tile(rh, 256, 16)

    def body(pl_ref, g_ref, r_ref, o16_ref, own_ref):
        sm = g_ref[...].astype(F32) + r_ref[...].astype(F32)
        o16_ref[...] = sm.astype(BF16)

        @pl.when(pl.program_id(1) == pl_ref[1])
        def _():
            own_ref[...] = sm

    grid_spec = pltpu.PrefetchScalarGridSpec(
        num_scalar_prefetch=1, grid=(rh // tr, 4),
        in_specs=[pl.BlockSpec((None, None, tr, c), lambda i, t, p: (t, p[0], i, 0)),
                  pl.BlockSpec((None, tr, c), lambda i, t, p: (t, i, 0))],
        out_specs=[pl.BlockSpec((None, tr, c), lambda i, t, p: (t, i, 0)),
                   pl.BlockSpec((tr, c), lambda i, t, p: (i, 0))])
    return pl.pallas_call(
        body, name=name, grid_spec=grid_spec,
        out_shape=[SDS((4, rh, c), BF16), SDS((rh, c), F32)],
        compiler_params=pltpu.CompilerParams(dimension_semantics=("arbitrary",) * 2,
                                             vmem_limit_bytes=VMEM_LIMIT))(place, g, recv)


def _sum_chips(own, recv, name):
    rh, c = own.shape
    tr = _tile(rh, 256, 16)

    def body(o_ref, r_ref, out_ref):
        acc = o_ref[...]
        for j in range(3):
            acc = acc + r_ref[j].astype(F32)
        out_ref[...] = acc

    return _pcall(body, name=name, grid=(rh // tr,),
                  in_specs=[_rows(tr, c), pl.BlockSpec((3, tr, c), lambda i: (0, i, 0))],
                  out_specs=_rows(tr, c), out_shape=SDS(own.shape, F32))(own, recv)


def _adamw_update(wv, gv, mv, vv):
    mn = ADAM_B1 * mv + (1.0 - ADAM_B1) * gv
    vn = ADAM_B2 * vv + (1.0 - ADAM_B2) * (gv * gv)
    m_hat = mn / (1.0 - ADAM_B1 ** ADAM_STEP)
    v_hat = vn / (1.0 - ADAM_B2 ** ADAM_STEP)
    return -ADAM_LR * (m_hat / (jnp.sqrt(v_hat) + ADAM_EPS) + ADAM_WD * wv), mn, vn


def _adamw(w, g, m, v, name):
    r, c = w.shape
    tr = _tile(r, 256, 8)

    def body(w_ref, g_ref, m_ref, v_ref, d_ref, mo_ref, vo_ref):
        d_ref[...], mo_ref[...], vo_ref[...] = _adamw_update(w_ref[...], g_ref[...], m_ref[...], v_ref[...])

    spec = _rows(tr, c)
    return _pcall(body, name=name, grid=(r // tr,), in_specs=[spec] * 4, out_specs=[spec] * 3,
                  out_shape=[SDS(w.shape, F32)] * 3)(w, g, m, v)


def _adamw_halves(w, mine, theirs, m, v, place, name):
    r, c = w.shape
    rh = r // 2
    tr = _tile(rh, 256, 8)
    nt = rh // tr

    def body(p_ref, w_ref, a_ref, b_ref, m_ref, v_ref, g_ref, d_ref, mo_ref, vo_ref):
        gv = jnp.where(pl.program_id(0) // nt == p_ref[0], a_ref[...], b_ref[...])
        g_ref[...] = gv
        d_ref[...], mo_ref[...], vo_ref[...] = _adamw_update(w_ref[...], gv, m_ref[...], v_ref[...])

    full = pl.BlockSpec((tr, c), lambda i, p: (i, 0))
    half = pl.BlockSpec((tr, c), lambda i, p: (i % nt, 0))
    grid_spec = pltpu.PrefetchScalarGridSpec(num_scalar_prefetch=1, grid=(2 * nt,),
                                             in_specs=[full, half, half, full, full], out_specs=[full] * 4)
    return pl.pallas_call(
        body, name=name, grid_spec=grid_spec, out_shape=[SDS(w.shape, F32)] * 4,
        compiler_params=pltpu.CompilerParams(dimension_semantics=("arbitrary",),
                                             vmem_limit_bytes=VMEM_LIMIT))(place, w, mine, theirs, m, v)


def _place():
    x, y, c = lax.axis_index("x"), lax.axis_index("y"), lax.axis_index("c")
    chips = [(1 - x, y), (x, 1 - y), (1 - x, 1 - y)]
    return x, y, c, chips


_ANY = pl.BlockSpec(memory_space=pl.ANY)


def _remote(src, dst, sems, k, to):
    return pltpu.make_async_remote_copy(src_ref=src, dst_ref=dst, send_sem=sems[0].at[k], recv_sem=sems[1].at[k],
                                        device_id=to, device_id_type=MESH)


class _Gather:
    def __init__(self, shards):
        n = len(shards)
        self.arrays = list(shards)
        self.out_shapes = [SDS((4,) + a.shape, a.dtype) for a in shards]
        self.scratch = [pltpu.SemaphoreType.DMA((7 * n,)), pltpu.SemaphoreType.DMA((7 * n,))]

    def _plan(self, ins, outs, sems):
        x, y, c, chips = _place()
        own, sib = 2 * x + y, (x, y, 1 - c)
        plan = []
        for wi, (w, o) in enumerate(zip(ins, outs)):
            rh = w.shape[0] // 2
            mine, theirs = pl.ds(c * rh, rh), pl.ds((1 - c) * rh, rh)
            whole = _remote(w, o.at[own], sems, 7 * wi + 6, sib)
            ici, d2d, d2d_in = [], [], []
            for j, (tx, ty) in enumerate(chips):
                t = 2 * tx + ty
                ici.append(_remote(w.at[mine], o.at[own, mine], sems, 7 * wi + j, (tx, ty, c)))
                d2d.append(_remote(o.at[t, mine], o.at[t, mine], sems, 7 * wi + 3 + j, sib))
                d2d_in.append(_remote(o.at[t, theirs], o.at[t, theirs], sems, 7 * wi + 3 + j, sib))
            plan.append((whole, ici, d2d, d2d_in))
        return plan

    def begin(self, ins, outs, sems):
        for whole, ici, _, _ in self._plan(ins, outs, sems):
            whole.start()
            for cp in ici:
                cp.start()

    def middle(self, ins, outs, sems):
        for _, ici, d2d, _ in self._plan(ins, outs, sems):
            for cp_in, cp_on in zip(ici, d2d):
                cp_in.wait_recv()
                cp_on.start()

    def finish(self, ins, outs, sems):
        for whole, ici, d2d, d2d_in in self._plan(ins, outs, sems):
            for cp in d2d_in:
                cp.wait_recv()
            for cp in ici + d2d:
                cp.wait_send()
            whole.wait()


class _Swap:
    def __init__(self, grads):
        n = len(grads)
        self.arrays = list(grads)
        self.out_shapes = [SDS((4,) + g.shape[2:], g.dtype) for g in grads]
        self.scratch = [pltpu.SemaphoreType.DMA((4 * n,)), pltpu.SemaphoreType.DMA((4 * n,))]

    def _plan(self, ins, outs, sems):
        x, y, c, _ = _place()
        return [_remote(g.at[t, 1 - c], o.at[t], sems, 4 * wi + t, (x, y, 1 - c))
                for wi, (g, o) in enumerate(zip(ins, outs)) for t in range(4)]

    def begin(self, ins, outs, sems):
        for cp in self._plan(ins, outs, sems):
            cp.start()

    def middle(self, ins, outs, sems):
        pass

    def finish(self, ins, outs, sems):
        for cp in self._plan(ins, outs, sems):
            cp.wait()


class _Exchange:
    def __init__(self, pieces):
        n = len(pieces)
        self.arrays = list(pieces)
        self.out_shapes = [SDS((3,) + p.shape[1:], p.dtype) for p in pieces]
        self.scratch = [pltpu.SemaphoreType.DMA((3 * n,)), pltpu.SemaphoreType.DMA((3 * n,))]

    def _plan(self, ins, outs, sems):
        x, y, c, chips = _place()
        return [_remote(g.at[2 * tx + ty], o.at[j], sems, 3 * wi + j, (tx, ty, c))
                for wi, (g, o) in enumerate(zip(ins, outs)) for j, (tx, ty) in enumerate(chips)]

    def begin(self, ins, outs, sems):
        for cp in self._plan(ins, outs, sems):
            cp.start()

    def middle(self, ins, outs, sems):
        pass

    def finish(self, ins, outs, sems):
        for cp in self._plan(ins, outs, sems):
            cp.wait()


class _Share:
    def __init__(self, totals):
        n = len(totals)
        self.arrays = list(totals)
        self.out_shapes = [SDS(t.shape, t.dtype) for t in totals]
        self.scratch = [pltpu.SemaphoreType.DMA((n,)), pltpu.SemaphoreType.DMA((n,))]

    def _plan(self, ins, outs, sems):
        x, y, c, _ = _place()
        return [_remote(t, o, sems, wi, (x, y, 1 - c)) for wi, (t, o) in enumerate(zip(ins, outs))]

    def begin(self, ins, outs, sems):
        for cp in self._plan(ins, outs, sems):
            cp.start()

    def middle(self, ins, outs, sems):
        pass

    def finish(self, ins, outs, sems):
        for cp in self._plan(ins, outs, sems):
            cp.wait()


def _ride_begin(rider, r_refs, step):
    @pl.when(step == 0)
    def _():
        rider.begin(*r_refs)


def _ride_end(rider, r_refs, step, nsteps):
    @pl.when(step == min(3 * nsteps // 4, nsteps - 1))
    def _():
        rider.middle(*r_refs)

    @pl.when(step == nsteps - 1)
    def _():
        rider.finish(*r_refs)


def _comm(rider, name):
    n_in, n_out = len(rider.arrays), len(rider.out_shapes)

    def body(*refs):
        r_refs = (refs[:n_in], refs[n_in:n_in + n_out], refs[n_in + n_out:])
        rider.begin(*r_refs)
        rider.middle(*r_refs)
        rider.finish(*r_refs)

    return pl.pallas_call(body, name=name, out_shape=rider.out_shapes, in_specs=[_ANY] * n_in,
                          out_specs=[_ANY] * n_out, scratch_shapes=rider.scratch)(*rider.arrays)


def _small_allreduce(pk, name):
    r = pk.shape[0]
    rels = [(dx, dy, dc) for dx in (0, 1) for dy in (0, 1) for dc in (0, 1) if dx or dy or dc]

    def body(p_ref, o_ref, buf, send_sems, recv_sems):
        x, y, c, _ = _place()
        me = 4 * x + 2 * y + c
        buf[me] = p_ref[...]
        cps = []
        for k, (dx, dy, dc) in enumerate(rels):
            to = (1 - x if dx else x, 1 - y if dy else y, 1 - c if dc else c)
            cps.append(pltpu.make_async_remote_copy(src_ref=p_ref, dst_ref=buf.at[me], send_sem=send_sems.at[k],
                                                    recv_sem=recv_sems.at[k], device_id=to,
                                                    device_id_type=MESH))
        for cpy in cps:
            cpy.start()
        for cpy in cps:
            cpy.wait()
        acc = buf[0]
        for d in range(1, 8):
            acc = acc + buf[d]
        o_ref[...] = acc

    vm = pl.BlockSpec(memory_space=pltpu.VMEM)
    return pl.pallas_call(body, name=name, out_shape=SDS(pk.shape, F32), in_specs=[vm], out_specs=vm,
                          scratch_shapes=[pltpu.VMEM((8, r, LANE), F32), pltpu.SemaphoreType.DMA((7,)),
                                          pltpu.SemaphoreType.DMA((7,))])(pk)


ATTN_W = ("w_in", "w_uq", "w_ukv", "w_out")
FFN_W = ("w_gate", "w_up", "w_down")
BIG = ATTN_W + FFN_W


def _cols_from_chips(g):
    return jnp.concatenate([g[t] for t in range(4)], axis=1)


def _cols_to_chips(full):
    r, n = full.shape
    return full.reshape(r, 4, n // 4).transpose(1, 0, 2).reshape(4, 2, r // 2, n // 4)


def _rows_to_chips(full):
    n, c = full.shape
    return full.reshape(4, 2, n // 8, c)


def _permute_w_in(w, nh):
    d = w.shape[0]
    g = 4 * nh * HEAD
    lr = (w.shape[1] - g - 2 * nh - ROPE) // 2
    o = g + 2 * nh
    pad = jnp.zeros((d, LANE - ROPE - 8 - nh), w.dtype)
    pad8 = jnp.zeros((d, 8 - nh), w.dtype)
    return jnp.concatenate([w[:, :g], w[:, o:o + 2 * lr], w[:, o + 2 * lr:], w[:, g:g + nh], pad8,
                            w[:, g + nh:g + 2 * nh], pad, jnp.zeros((d, LANE), w.dtype)], axis=1)


def _unpermute_w_in(wp, nh, lr):
    g = 4 * nh * HEAD
    mc = g + 2 * lr
    return jnp.concatenate([wp[:, :g], wp[:, mc + B_LANE:mc + B_LANE + nh], wp[:, mc + A_LANE:mc + A_LANE + nh],
                            wp[:, g:g + 2 * lr], wp[:, mc:mc + ROPE]], axis=1)


def _permute_w_uq(w, nh):
    lr = w.shape[0]
    w3 = w.reshape(lr, nh, HEAD + ROPE)
    return jnp.concatenate([w3, jnp.zeros((lr, nh, HEAD - ROPE), w.dtype)], axis=2).reshape(lr, nh * 2 * HEAD)


def _unpermute_w_uq(wp, nh):
    lr = wp.shape[0]
    return wp.reshape(lr, nh, 2 * HEAD)[:, :, :HEAD + ROPE].reshape(lr, nh * (HEAD + ROPE))


def _permute_w_ukv(w, nh):
    lr = w.shape[0]
    w3 = w.reshape(lr, nh, 2 * HEAD)
    kp = jnp.concatenate([w3[:, :, :HEAD], jnp.zeros((lr, nh, HEAD), w.dtype)], axis=2)
    return jnp.concatenate([kp.reshape(lr, nh * 2 * HEAD), w3[:, :, HEAD:].reshape(lr, nh * HEAD)], axis=1)


def _unpermute_w_ukv(wp, nh):
    lr = wp.shape[0]
    kp = wp[:, :nh * 2 * HEAD].reshape(lr, nh, 2 * HEAD)[:, :, :HEAD]
    vp = wp[:, nh * 2 * HEAD:].reshape(lr, nh, HEAD)
    return jnp.concatenate([kp, vp], axis=2).reshape(lr, nh * 2 * HEAD)


def _sum_pairs(grads, recv, place, tag):
    sums = [_sum_pair(g, r, place, "sum_pair_%s%d" % (tag, k)) for k, (g, r) in enumerate(zip(grads, recv))]
    return [s[0] for s in sums], [s[1] for s in sums]


def _reduce_end(own, recv, tag):
    return [_sum_chips(o, r, "sum_chips_%s%d" % (tag, k)) for k, (o, r) in enumerate(zip(own, recv))]


def _step(x, pos, tgt, w_in, attn_shards, ffn_shards, small, place):
    nh = small["a_log"].shape[1]
    lr = small["q_norm_w"].shape[1]
    w = nh * HEAD
    z_col, col_q, col_kv = 3, 4 * w // lr, 4 * w // lr + 1
    misc_c = 4 * w + 2 * lr
    misc_col = misc_c // LANE
    assert (4 * w) % lr == 0 and small["kv_norm_w"].shape[1] == lr

    zl = jnp.zeros((1, LANE), F32)
    alog_l = zl.at[:, A_LANE:A_LANE + nh].set(small["a_log"])
    dtb_l = zl.at[:, A_LANE:A_LANE + nh].set(small["dt_bias"])
    conv_w = small["conv_w"]

    h1, h1t, (in4,) = _norm_fwd(x, small["attn_norm_w"], "norm1", rider=_Gather([w_in]))
    win_p = _permute_w_in(_cols_from_chips(in4), nh)
    proj, (uq4, ukv4, out4) = _mm([(h1, win_p)], name="proj_in", rider=_Gather(attn_shards))
    wuq_p = _permute_w_uq(_cols_from_chips(uq4), nh)
    wukv_p = _permute_w_ukv(_cols_from_chips(ukv4), nh)
    w_out = out4.reshape(-1, out4.shape[2])
    gq, gk, gv, gb, gbt = _gdn_prep(proj, conv_w, alog_l, dtb_l, nh, misc_col)
    (o_gdn, states), (wg4,) = _gdn_fwd(gq, gk, gv, gb, gbt, nh, _Gather(ffn_shards[:1]))
    cqn, ckvn = _mla_norm(proj, small["q_norm_w"], small["kv_norm_w"], col_q, col_kv)
    qraw = _mm([(cqn, wuq_p)], name="proj_uq", out_dtype=BF16)
    kvraw = _mm([(ckvn, wukv_p)], name="proj_ukv", out_dtype=BF16)
    qc, kc, vv, qt, vt, kt = _mla_rope(qraw, kvraw, proj, pos, nh, misc_col)
    o_mla, lse, (wu4, wd4) = _mla_fwd(qt, kc, vt, nh, _Gather(ffn_shards[1:]))
    w_down = wd4.reshape(-1, wd4.shape[2])
    mixed = _mix_fwd(o_gdn, proj, o_mla, small["gdn_norm_w"], small["mla_out_norm_w"], nh, z_col)
    x2 = _mm([(mixed, w_out)], name="proj_out", res=x)
    h2, h2t = _norm_fwd(x2, small["ffn_norm_w"], "norm2")
    act, gpre, upre = _swiglu_fwd(h2, wg4, wu4)
    x3 = _mm([(act, w_down)], name="proj_down", res=x2, tk=2816)
    dx3, d_final, loss, dx3h = _final_loss(x3, tgt, small["final_norm_w"])

    gs = {"final_norm_w": d_final}
    dgate, dup = _swiglu_bwd(dx3h, w_down, gpre, upre)
    g_down = _rows_to_chips(_mm([(act, dx3h)], name="dw_down", ta=True, out_dtype=BF16))
    g_gate = _mm([(h2t, dgate)], name="dw_gate", out_dtype=BF16, out_chips=True)
    g_up = _mm([(h2t, dup)], name="dw_up", out_dtype=BF16, out_chips=True)
    halves = lambda g: g.reshape(4, 2, g.shape[1] // 2, g.shape[2])
    ffn_g = [halves(g_gate), halves(g_up), g_down]
    dh2, ffn_sib = _mm([(dgate, wg4), (dup, wu4)], name="dh2", tb=True, b_chips=True, out_dtype=BF16,
                       rider=_Swap(ffn_g))
    ffn16, ffn_own = _sum_pairs(ffn_g, ffn_sib, place, "ffn")
    dx2, gs["ffn_norm_w"], dx2h = _norm_bwd(dh2, x2, small["ffn_norm_w"], dx3, "norm2_bwd", True)
    dmix = _mm([(dx2h, w_out)], name="dmix", tb=True, out_dtype=BF16)
    g_out = _rows_to_chips(_mm([(mixed, dx2h)], name="dw_out", ta=True, out_dtype=BF16))
    d_ogdn, dproj, d_omla, gs["gdn_norm_w"], gs["mla_out_norm_w"], delta, d_omla_t = _mix_bwd(
        dmix, o_gdn, proj, o_mla, small["gdn_norm_w"], small["mla_out_norm_w"], nh, z_col)
    dqc, dkc, dvv, ffn_recv = _mla_bwd(qc, qt, kc, kt, vv, d_omla, d_omla_t, lse, delta, nh, _Exchange(ffn16))
    ffn_tot = _reduce_end(ffn_own, ffn_recv, "ffn")
    dqraw, dkvraw, dkr = _mla_rope_bwd(dqc, dkc, dvv, pos, nh)
    dcqn = _mm([(dqraw, wuq_p)], name="dcqn", tb=True)
    dckvn = _mm([(dkvraw, wukv_p)], name="dckvn", tb=True)
    g_uq = _cols_to_chips(_unpermute_w_uq(_mm([(cqn, dqraw)], name="dw_uq", ta=True, out_dtype=BF16), nh))
    g_ukv = _cols_to_chips(_unpermute_w_ukv(_mm([(ckvn, dkvraw)], name="dw_ukv", ta=True, out_dtype=BF16), nh))
    dproj, gs["q_norm_w"], gs["kv_norm_w"] = _mla_norm_bwd(
        proj, small["q_norm_w"], small["kv_norm_w"], dcqn, dckvn, dproj, col_q, col_kv)
    (dgq, dgk, dgv, dgb), ffn_shared = _gdn_bwd(gq, gk, gv, gb, gbt, states, d_ogdn, nh, _Share(ffn_tot))
    dconv, dproj, gs["conv_w"], dal, ddb = _gdn_prep_bwd(
        proj, conv_w, alog_l, dtb_l, dgq, dgk, dgv, dgb, dkr, dproj, nh, misc_col)
    gs["a_log"] = dal[:, A_LANE:A_LANE + nh]
    gs["dt_bias"] = ddb[:, A_LANE:A_LANE + nh]
    dproj = _conv_bwd_input(dconv, conv_w, dproj)
    g_in = _cols_to_chips(_unpermute_w_in(_mm([(h1t, dproj)], name="dw_in", out_dtype=BF16), nh, lr))
    att_g = [g_in, g_uq, g_ukv, g_out]
    att16, att_own = _sum_pairs(att_g, _comm(_Swap(att_g), "swap_att"), place, "att")
    dh1, att_recv = _mm([(dproj, win_p)], name="dh1", tb=True, out_dtype=BF16, rider=_Exchange(att16))
    att_tot = _reduce_end(att_own, att_recv, "att")
    att_shared = _comm(_Share(att_tot), "share_att")
    grad_x, gs["attn_norm_w"] = _norm_bwd(dh1, x, small["attn_norm_w"], dx2, "norm1_bwd", False)
    return loss, grad_x, att_tot + ffn_tot, list(att_shared) + list(ffn_shared), gs


SMALL = ("attn_norm_w", "ffn_norm_w", "final_norm_w", "q_norm_w", "kv_norm_w", "gdn_norm_w",
         "mla_out_norm_w", "a_log", "dt_bias")
WEIGHTS = ("attn_norm_w", "w_in", "conv_w", "a_log", "dt_bias", "gdn_norm_w", "q_norm_w", "w_uq",
           "kv_norm_w", "w_ukv", "mla_out_norm_w", "w_out", "ffn_norm_w", "w_gate", "w_up", "w_down",
           "final_norm_w")


def _pack_small(vecs):
    flat = jnp.concatenate([v.astype(F32).reshape(-1) for v in vecs])
    pad = (-flat.shape[0]) % (8 * LANE)
    return jnp.concatenate([flat, jnp.zeros((pad,), F32)]).reshape(-1, LANE)


def kernel(x, positions, attn_norm_w, w_in, conv_w, a_log, dt_bias, gdn_norm_w, q_norm_w, w_uq, kv_norm_w, w_ukv, mla_out_norm_w, w_out, ffn_norm_w, w_gate, w_up, w_down, final_norm_w, loss_target, m_attn_norm_w, m_w_in, m_conv_w, m_a_log, m_dt_bias, m_gdn_norm_w, m_q_norm_w, m_w_uq, m_kv_norm_w, m_w_ukv, m_mla_out_norm_w, m_w_out, m_ffn_norm_w, m_w_gate, m_w_up, m_w_down, m_final_norm_w, v_attn_norm_w, v_w_in, v_conv_w, v_a_log, v_dt_bias, v_gdn_norm_w, v_q_norm_w, v_w_uq, v_kv_norm_w, v_w_ukv, v_mla_out_norm_w, v_w_out, v_ffn_norm_w, v_w_gate, v_w_up, v_w_down, v_final_norm_w):
    args = dict(locals())
    xi, yi, ci = lax.axis_index("x"), lax.axis_index("y"), lax.axis_index("c")
    chip = 2 * xi + yi

    def two_d(a):
        return a.reshape(a.shape[-2:]) if a.ndim >= 2 else a.reshape(1, -1)

    wloc = {n: two_d(args[n]) for n in WEIGHTS}
    mloc = {n: two_d(args["m_" + n]) for n in WEIGHTS}
    vloc = {n: two_d(args["v_" + n]) for n in WEIGHTS}

    cw = wloc["conv_w"]
    cshard = cw.shape[1]
    cfull = jnp.zeros((CONV, 4 * cshard), F32)
    cfull = lax.dynamic_update_slice(cfull, jnp.where(ci == 0, cw, 0.0), (0, chip * cshard))
    conv_full = _small_allreduce(_pack_small([cfull]), "gather_conv_w").reshape(-1)[:CONV * 4 * cshard]
    conv_full = conv_full.reshape(CONV, 4 * cshard)

    small = {n: wloc[n] for n in SMALL}
    small["conv_w"] = conv_full

    pos = positions.reshape(-1, 1).astype(F32)
    place = jnp.stack([ci, chip]).astype(jnp.int32)
    loss, grad_x, totals, from_sib, gs = _step(
        two_d(x), pos, two_d(loss_target), wloc["w_in"].astype(BF16), [wloc[n].astype(BF16) for n in ATTN_W[1:]],
        [wloc[n].astype(BF16) for n in FFN_W], small, place)

    small_names = SMALL + ("conv_w",)
    pk = _pack_small([gs[n] for n in small_names] + [loss])
    red = _small_allreduce(pk, "reduce_small").reshape(-1)
    gsm, off = {}, 0
    for n in small_names:
        shp = gs[n].shape
        gsm[n] = red[off:off + shp[0] * shp[1]].reshape(shp)
        off += shp[0] * shp[1]
    loss_out = red[off]
    gsm["conv_w"] = lax.dynamic_slice(gsm["conv_w"], (0, chip * cshard), (CONV, cshard))

    grads, deltas, new_m, new_v = {}, {}, {}, {}
    for n, mine, theirs in zip(BIG, totals, from_sib):
        grads[n], deltas[n], new_m[n], new_v[n] = _adamw_halves(wloc[n], mine, theirs, mloc[n], vloc[n], place,
                                                                "adamw_" + n)
    grads["conv_w"] = gsm["conv_w"]
    deltas["conv_w"], new_m["conv_w"], new_v["conv_w"] = _adamw(wloc["conv_w"], gsm["conv_w"], mloc["conv_w"],
                                                                vloc["conv_w"], "adamw_conv_w")
    sm_shapes = [wloc[n].shape for n in SMALL]
    pd, pm, pv = _adamw(_pack_small([wloc[n] for n in SMALL]), _pack_small([gsm[n] for n in SMALL]),
                        _pack_small([mloc[n] for n in SMALL]), _pack_small([vloc[n] for n in SMALL]),
                        "adamw_small")
    for dst, packed in ((deltas, pd), (new_m, pm), (new_v, pv)):
        flat, off = packed.reshape(-1), 0
        for n, shp in zip(SMALL, sm_shapes):
            dst[n] = flat[off:off + shp[0] * shp[1]].reshape(shp)
            off += shp[0] * shp[1]
    for n in SMALL:
        grads[n] = gsm[n]

    def like(n, a):
        return a.reshape(args[n].shape)

    outs = [loss_out.reshape(()), grad_x.reshape(x.shape)]
    for group in (grads, deltas, new_m, new_v):
        outs += [like(n, group[n]) for n in WEIGHTS]
    return tuple(outs)
```

```python
import jax
import jax.numpy as jnp
from jax import lax
from jax.experimental import pallas as pl
from jax.experimental.pallas import tpu as pltpu

F32, BF16 = jnp.float32, jnp.bfloat16
SDS = jax.ShapeDtypeStruct
MESH = pl.DeviceIdType.MESH

HEAD = 128
ROPE = 64
CHUNK = 64
PAIR = 2 * CHUNK
CONV = 4
EPS = 1e-6
ROPE_THETA = 10000.0
LANE = 128
B_LANE = 64
A_LANE = 72
VMEM_LIMIT = 48 * 1024 * 1024
VMEM_LIMIT_WIDE = 56 * 1024 * 1024
MLA_BLOCK = 512
LOG2E = 1.4426950408889634
LN2 = 0.6931471805599453
SM_SCALE = (HEAD + ROPE) ** -0.5

ADAM_LR = 0.001
ADAM_B1 = 0.9
ADAM_B2 = 0.999
ADAM_EPS = 1e-08
ADAM_WD = 0.01
ADAM_STEP = 10


def _tile(n, pref, mult=LANE):
    if n <= pref:
        return n
    t = (pref // mult) * mult
    while t >= mult:
        if n % t == 0:
            return t
        t -= mult
    return n


def _pcall(body, *, name, grid, in_specs, out_specs, out_shape, scratch=(), vmem=VMEM_LIMIT, aliases=None):
    return pl.pallas_call(
        body, name=name, grid=grid, in_specs=in_specs, out_specs=out_specs,
        out_shape=out_shape, scratch_shapes=list(scratch), input_output_aliases=aliases or {},
        compiler_params=pltpu.CompilerParams(
            dimension_semantics=("arbitrary",) * len(grid), vmem_limit_bytes=vmem))


def _pcall_riding(core, rider, *, name, grid, in_specs, out_specs, out_shape, args, scratch=()):
    n_in, n_out, n_scr = len(in_specs), len(out_specs), len(scratch)
    r_in, r_out = len(rider.arrays), len(rider.out_shapes)

    def body(*refs):
        ins, refs = refs[:n_in], refs[n_in:]
        r_ins, refs = refs[:r_in], refs[r_in:]
        outs, refs = refs[:n_out], refs[n_out:]
        r_outs, refs = refs[:r_out], refs[r_out:]
        scr, sems = refs[:n_scr], refs[n_scr:]
        r_refs = (r_ins, r_outs, sems)
        _ride_begin(rider, r_refs, pl.program_id(0))
        core(*ins, *outs, *scr)
        _ride_end(rider, r_refs, pl.program_id(0), grid[0])

    res = _pcall(body, name=name, grid=grid, in_specs=list(in_specs) + [_ANY] * r_in,
                 out_specs=list(out_specs) + [_ANY] * r_out, out_shape=list(out_shape) + rider.out_shapes,
                 scratch=list(scratch) + rider.scratch)(*args, *rider.arrays)
    return res[:n_out], res[n_out:]


def _rows(ts, width, col=0):
    return pl.BlockSpec((ts, width), lambda i: (i, col))


def _full(shape):
    nd = len(shape)
    return pl.BlockSpec(shape, lambda i: (0,) * nd)


def _dot(a, b):
    return jnp.dot(a.astype(BF16), b.astype(BF16), preferred_element_type=F32)


def _dot_nt(a, b):
    return lax.dot_general(a.astype(BF16), b.astype(BF16), (((1,), (1,)), ((), ())),
                           preferred_element_type=F32)


def _dot_tn(a, b):
    return lax.dot_general(a.astype(BF16), b.astype(BF16), (((0,), (0,)), ((), ())),
                           preferred_element_type=F32)


def _sigmoid(x):
    return 1.0 / (1.0 + jnp.exp(-x))


def _silu(x):
    return x * _sigmoid(x)


def _dsilu(x):
    s = _sigmoid(x)
    return s * (1.0 + x * (1.0 - s))


def _lane_iota(shape):
    return lax.broadcasted_iota(jnp.int32, shape, len(shape) - 1)


def _col(block, idx):
    return jnp.sum(jnp.where(_lane_iota(block.shape) == idx, block, 0.0), axis=-1, keepdims=True)


def _mm(pairs, *, name, ta=False, tb=False, out_dtype=F32, res=None, tm=1024, tn=1024, tk=2048,
        b_chips=False, out_chips=False, rider=None):
    a0, b0 = pairs[0]
    if ta:
        kdim, m = a0.shape
    else:
        m, kdim = a0.shape
    if b_chips and tb:
        n, tk = b0.shape[1], b0.shape[2]
        assert kdim == 4 * tk
    elif b_chips:
        n, tn = 4 * b0.shape[2], b0.shape[2]
        assert kdim == b0.shape[1]
    else:
        n = b0.shape[0] if tb else b0.shape[1]
    if out_chips:
        tn = n // 4
    tm = _tile(m, tm)
    tn = tn if (out_chips or (b_chips and not tb)) else _tile(n, tn)
    tk = tk if (b_chips and tb) else _tile(kdim, tk)
    assert m % tm == 0 and n % tn == 0 and kdim % tk == 0
    nk, npair = kdim // tk, len(pairs)
    grid = (m // tm, n // tn, nk)
    dims = (((0 if ta else 1,), (1 if tb else 0,)), ((), ()))
    n_in = 2 * npair + (res is not None)
    r_in, r_out = (len(rider.arrays), len(rider.out_shapes)) if rider else (0, 0)

    def body(*refs):
        o_ref = refs[n_in + r_in]
        acc = refs[n_in + r_in + 1 + r_out]
        k = pl.program_id(2)
        if rider:
            r_refs = (refs[n_in:n_in + r_in], refs[n_in + r_in + 1:n_in + r_in + 1 + r_out],
                      refs[n_in + r_in + 2 + r_out:])
            step = (pl.program_id(0) * grid[1] + pl.program_id(1)) * nk + k
            _ride_begin(rider, r_refs, step)

        @pl.when(k == 0)
        def _():
            acc[...] = jnp.zeros_like(acc)

        tot = None
        for p in range(npair):
            d = lax.dot_general(refs[2 * p][...].astype(BF16), refs[2 * p + 1][...].astype(BF16),
                                dims, preferred_element_type=F32)
            tot = d if tot is None else tot + d
        acc[...] += tot

        @pl.when(k == nk - 1)
        def _():
            r = acc[...]
            if res is not None:
                r = r + refs[2 * npair][...]
            o_ref[...] = r.astype(out_dtype)

        if rider:
            _ride_end(rider, r_refs, step, grid[0] * grid[1] * nk)

    if ta:
        a_spec = pl.BlockSpec((tk, tm), lambda i, j, k: (k, i))
    else:
        a_spec = pl.BlockSpec((tm, tk), lambda i, j, k: (i, k))
    if b_chips and tb:
        b_spec = pl.BlockSpec((None, tn, tk), lambda i, j, k: (k, j, 0))
    elif b_chips:
        b_spec = pl.BlockSpec((None, tk, tn), lambda i, j, k: (j, k, 0))
    elif tb:
        b_spec = pl.BlockSpec((tn, tk), lambda i, j, k: (j, k))
    else:
        b_spec = pl.BlockSpec((tk, tn), lambda i, j, k: (k, j))
    if out_chips:
        o_spec = pl.BlockSpec((None, tm, tn), lambda i, j, k: (j, i, 0))
        o_shape = SDS((4, m, tn), out_dtype)
    else:
        o_spec = pl.BlockSpec((tm, tn), lambda i, j, k: (i, j))
        o_shape = SDS((m, n), out_dtype)
    in_specs, args = [], []
    for a, b in pairs:
        in_specs += [a_spec, b_spec]
        args += [a, b]
    if res is not None:
        in_specs.append(o_spec)
        args.append(res)
    out_specs, out_shapes, scratch = [o_spec], [o_shape], [pltpu.VMEM((tm, tn), F32)]
    if rider:
        in_specs += [_ANY] * r_in
        args += rider.arrays
        out_specs += [_ANY] * r_out
        out_shapes += rider.out_shapes
        scratch += rider.scratch
    outs = _pcall(body, name=name, grid=grid, in_specs=in_specs, out_specs=out_specs, out_shape=out_shapes,
                  scratch=scratch)(*args)
    return (outs[0], outs[1:]) if rider else outs[0]


def _norm_fwd(x, w, name, rider=None):
    s, d = x.shape
    ts = _tile(s, 512, 8)

    def body(x_ref, w_ref, h_ref):
        xv = x_ref[...]
        r = lax.rsqrt(jnp.mean(xv * xv, axis=-1, keepdims=True) + EPS)
        h_ref[...] = (xv * r * w_ref[...]).astype(BF16)

    spec = dict(name=name, grid=(s // ts,), in_specs=[_rows(ts, d), _full((1, d))])
    if rider is None:
        return _pcall(body, out_specs=_rows(ts, d), out_shape=SDS((s, d), BF16), **spec)(x, w)
    outs, r_outs = _pcall_riding(body, rider, out_specs=[_rows(ts, d)], out_shape=[SDS((s, d), BF16)],
                                 args=(x, w), **spec)
    return outs[0], r_outs


def _proj_norm(a, b, res, w, name):
    m, kdim = a.shape
    n = b.shape[1]
    tm = _tile(m, 512, 8)

    def body(a_ref, b_ref, r_ref, w_ref, x_ref, h_ref):
        xv = r_ref[...] + _dot(a_ref[...], b_ref[...])
        x_ref[...] = xv
        r = lax.rsqrt(jnp.mean(xv * xv, axis=-1, keepdims=True) + EPS)
        h_ref[...] = (xv * r * w_ref[...]).astype(BF16)

    return _pcall(body, name=name, grid=(m // tm,),
                  in_specs=[_rows(tm, kdim), pl.BlockSpec((kdim, n), lambda i: (0, 0), pipeline_mode=pl.Buffered(1)),
                            _rows(tm, n), _full((1, n))],
                  out_specs=[_rows(tm, n), _rows(tm, n)],
                  out_shape=[SDS((m, n), F32), SDS((m, n), BF16)])(a, b, res, w)


def _norm_bwd(dh, x, w, dres, name, with_bf16):
    s, d = x.shape
    ts = _tile(s, 256, 8)

    def body(dh_ref, x_ref, w_ref, dres_ref, dx_ref, dw_ref, *dx16_ref):
        @pl.when(pl.program_id(0) == 0)
        def _():
            dw_ref[...] = jnp.zeros_like(dw_ref)

        xv, dhv = x_ref[...], dh_ref[...]
        r = lax.rsqrt(jnp.mean(xv * xv, axis=-1, keepdims=True) + EPS)
        xh = xv * r
        dw_ref[...] += jnp.sum(dhv * xh, axis=0, keepdims=True)
        dxh = dhv * w_ref[...]
        dx = dres_ref[...] + r * (dxh - xh * jnp.mean(dxh * xh, axis=-1, keepdims=True))
        dx_ref[...] = dx
        for ref in dx16_ref:
            ref[...] = dx.astype(BF16)

    extra = 1 if with_bf16 else 0
    return _pcall(body, name=name, grid=(s // ts,),
                  in_specs=[_rows(ts, d), _rows(ts, d), _full((1, d)), _rows(ts, d)],
                  out_specs=[_rows(ts, d), _full((1, d))] + [_rows(ts, d)] * extra,
                  out_shape=[SDS((s, d), F32), SDS((1, d), F32)] + [SDS((s, d), BF16)] * extra)(
                      dh, x, w, dres)


def _final_loss(x3, tgt, w):
    s, d = x3.shape
    ts = _tile(s, 256, 8)

    def body(x_ref, t_ref, w_ref, dx_ref, dw_ref, loss_ref, dx16_ref):
        @pl.when(pl.program_id(0) == 0)
        def _():
            dw_ref[...] = jnp.zeros_like(dw_ref)
            loss_ref[...] = jnp.zeros_like(loss_ref)

        xv, wv = x_ref[...], w_ref[...]
        r = lax.rsqrt(jnp.mean(xv * xv, axis=-1, keepdims=True) + EPS)
        xh = xv * r
        err = xh * wv - t_ref[...]
        row = jnp.mean(err * err, axis=-1, keepdims=True)
        loss_ref[...] += 0.5 * jnp.sum(row, axis=0, keepdims=True)
        dy = err * (1.0 / d)
        dw_ref[...] += jnp.sum(dy * xh, axis=0, keepdims=True)
        dxh = dy * wv
        dx = r * (dxh - xh * jnp.mean(dxh * xh, axis=-1, keepdims=True))
        dx_ref[...] = dx
        dx16_ref[...] = dx.astype(BF16)

    return _pcall(body, name="final_loss", grid=(s // ts,),
                  in_specs=[_rows(ts, d), _rows(ts, d), _full((1, d))],
                  out_specs=[_rows(ts, d), _full((1, d)), _full((1, 1)), _rows(ts, d)],
                  out_shape=[SDS((s, d), F32), SDS((1, d), F32), SDS((1, 1), F32), SDS((s, d), BF16)])(
                      x3, tgt, w)


def _shift_down(cur, halo, s):
    if s == 0:
        return cur
    row8 = lax.broadcasted_iota(jnp.int32, halo.shape, 0)
    r = pltpu.roll(cur, s, 0)
    top = jnp.where(row8 < s, pltpu.roll(halo, s, 0), r[0:8])
    return jnp.concatenate([top, r[8:]], axis=0)


def _shift_up(cur, halo, s):
    if s == 0:
        return cur
    ts = cur.shape[0]
    row8 = lax.broadcasted_iota(jnp.int32, halo.shape, 0)
    r = pltpu.roll(cur, ts - s, 0)
    bot = jnp.where(row8 >= 8 - s, pltpu.roll(halo, 8 - s, 0), r[ts - 8:ts])
    return jnp.concatenate([r[:ts - 8], bot], axis=0)


def _chunk_tri(ts, upper):
    i = lax.broadcasted_iota(jnp.int32, (ts, ts), 0)
    j = lax.broadcasted_iota(jnp.int32, (ts, ts), 1)
    same = jnp.right_shift(i, 6) == jnp.right_shift(j, 6)
    return jnp.where(same & ((j >= i) if upper else (j <= i)), 1.0, 0.0).astype(F32)


def _gate_values(m, alog, dtb):
    lane = _lane_iota(m.shape)
    beta = _sigmoid(m)
    xg = m + dtb
    sp = jnp.maximum(xg, 0.0) + jnp.log(1.0 + jnp.exp(-jnp.abs(xg)))
    ga = (lane >= A_LANE) & (lane < A_LANE + 8)
    g = jnp.where(ga, -jnp.exp(alog) * sp, 0.0)
    return beta, g, xg, ga


def _l2_heads(a, nh, scale):
    outs, rs = [], []
    for h in range(nh):
        ah = a[:, HEAD * h:HEAD * (h + 1)]
        r = lax.rsqrt(jnp.sum(ah * ah, axis=-1, keepdims=True) + EPS)
        outs.append(ah * (r * scale))
        rs.append(r)
    return jnp.concatenate(outs, axis=-1), rs


def _gdn_prep(proj, conv_w, alog_l, dtb_l, nh, misc_col):
    s = proj.shape[0]
    w = nh * HEAD
    ts = _tile(s, 256, PAIR)
    hb = ts // 8

    def body(cur_ref, halo_ref, misc_ref, cw_ref, al_ref, db_ref, q_ref, k_ref, v_ref, gb_ref, gbt_ref):
        first = pl.program_id(0) == 0
        outs = (q_ref, k_ref, v_ref)
        for sec in range(3):
            cs = slice(sec * w, (sec + 1) * w)
            cur = cur_ref[:, cs]
            halo = jnp.where(first, 0.0, halo_ref[:, cs])
            pre = None
            for j in range(CONV):
                term = cw_ref[j:j + 1, cs] * _shift_down(cur, halo, CONV - 1 - j)
                pre = term if pre is None else pre + term
            act = _silu(pre)
            if sec == 0:
                act, _ = _l2_heads(act, nh, HEAD ** -0.5)
            elif sec == 1:
                act, _ = _l2_heads(act, nh, 1.0)
            outs[sec][...] = act
        m = misc_ref[...]
        lane = _lane_iota(m.shape)
        beta, g, _, ga = _gate_values(m, al_ref[...], db_ref[...])
        gcc = jnp.dot(_chunk_tri(ts, False), g, precision=lax.Precision.HIGHEST,
                      preferred_element_type=F32)
        gb = jnp.where((lane >= B_LANE) & (lane < B_LANE + 8), beta, jnp.where(ga, gcc, 0.0))
        gb_ref[...] = gb
        gbt_ref[...] = gb.T

    return _pcall(
        body, name="gdn_prep", grid=(s // ts,),
        in_specs=[_rows(ts, 3 * w),
                  pl.BlockSpec((8, 3 * w), lambda i: (jnp.maximum(i * hb - 1, 0), 0)),
                  _rows(ts, LANE, misc_col), _full((CONV, 3 * w)), _full((1, LANE)), _full((1, LANE))],
        out_specs=[_rows(ts, w), _rows(ts, w), _rows(ts, w), _rows(ts, LANE),
                   pl.BlockSpec((LANE, ts), lambda i: (0, i))],
        out_shape=[SDS((s, w), F32), SDS((s, w), F32), SDS((s, w), F32), SDS((s, LANE), F32),
                   SDS((LANE, s), F32)])(proj, proj, proj, conv_w, alog_l, dtb_l)


def _gdn_prep_bwd(proj, conv_w, alog_l, dtb_l, dq, dk, dv, dgb, dkr, dproj, nh, misc_col):
    s = proj.shape[0]
    w = nh * HEAD
    ts = _tile(s, 256, PAIR)
    hb = ts // 8
    assert misc_col % 2 == 0

    def body(cur_ref, halo_ref, misc_ref, cw_ref, al_ref, db_ref, dq_ref, dk_ref, dv_ref, dgb_ref,
             dkr_ref, _, dc_ref, dm_ref, dcw_ref, dal_ref, ddb_ref):
        first = pl.program_id(0) == 0

        @pl.when(first)
        def _():
            dcw_ref[...] = jnp.zeros_like(dcw_ref)
            dal_ref[...] = jnp.zeros_like(dal_ref)
            ddb_ref[...] = jnp.zeros_like(ddb_ref)

        dins = (dq_ref, dk_ref, dv_ref)
        for sec in range(3):
            cs = slice(sec * w, (sec + 1) * w)
            cur = cur_ref[:, cs]
            halo = jnp.where(first, 0.0, halo_ref[:, cs])
            us = [_shift_down(cur, halo, CONV - 1 - j) for j in range(CONV)]
            pre = None
            for j in range(CONV):
                term = cw_ref[j:j + 1, cs] * us[j]
                pre = term if pre is None else pre + term
            act = _silu(pre)
            dout = dins[sec][...]
            if sec < 2:
                scale = HEAD ** -0.5 if sec == 0 else 1.0
                parts = []
                for h in range(nh):
                    hs = slice(HEAD * h, HEAD * (h + 1))
                    ah = act[:, hs]
                    r = lax.rsqrt(jnp.sum(ah * ah, axis=-1, keepdims=True) + EPS)
                    ahat = ah * r
                    dy = dout[:, hs]
                    parts.append((scale * r) * (dy - ahat * jnp.sum(dy * ahat, axis=-1, keepdims=True)))
                dact = jnp.concatenate(parts, axis=-1)
            else:
                dact = dout
            dconv = dact * _dsilu(pre)
            dc_ref[:, cs] = dconv
            for j in range(CONV):
                dcw_ref[j:j + 1, cs] += jnp.sum(dconv * us[j], axis=0, keepdims=True)
        m = misc_ref[...]
        lane = _lane_iota(m.shape)
        al = al_ref[...]
        beta, g, xg, ga = _gate_values(m, al, db_ref[...])
        dgbv = dgb_ref[...]
        dg = jnp.dot(_chunk_tri(ts, True), jnp.where(ga, dgbv, 0.0), precision=lax.Precision.HIGHEST,
                     preferred_element_type=F32)
        da_raw = jnp.where(ga, dg * (-jnp.exp(al)) * _sigmoid(xg), 0.0)
        db_raw = jnp.where((lane >= B_LANE) & (lane < B_LANE + 8), dgbv * beta * (1.0 - beta), 0.0)
        dal_ref[...] += jnp.sum(dg * g, axis=0, keepdims=True)
        ddb_ref[...] += jnp.sum(da_raw, axis=0, keepdims=True)
        dm_ref[:, :LANE] = (dkr_ref[...] + da_raw + db_raw).astype(BF16)
        dm_ref[:, LANE:] = jnp.zeros((ts, LANE), BF16)

    return _pcall(
        body, name="gdn_prep_bwd", grid=(s // ts,),
        in_specs=[_rows(ts, 3 * w),
                  pl.BlockSpec((8, 3 * w), lambda i: (jnp.maximum(i * hb - 1, 0), 0)),
                  _rows(ts, LANE, misc_col), _full((CONV, 3 * w)), _full((1, LANE)), _full((1, LANE)),
                  _rows(ts, w), _rows(ts, w), _rows(ts, w), _rows(ts, LANE), _rows(ts, LANE), _ANY],
        out_specs=[_rows(ts, 3 * w), _rows(ts, 2 * LANE, misc_col // 2), _full((CONV, 3 * w)), _full((1, LANE)),
                   _full((1, LANE))],
        out_shape=[SDS((s, 3 * w), F32), SDS(dproj.shape, BF16), SDS((CONV, 3 * w), F32),
                   SDS((1, LANE), F32), SDS((1, LANE), F32)], aliases={11: 1})(
                       proj, proj, proj, conv_w, alog_l, dtb_l, dq, dk, dv, dgb, dkr, dproj)


def _conv_bwd_input(dconv, conv_w, dproj):
    s, c = dconv.shape
    ts = _tile(s, 256, 8)
    hb = ts // 8
    nblk8 = s // 8
    nt = s // ts

    def body(cur_ref, nxt_ref, cw_ref, _, o_ref):
        last = pl.program_id(0) == nt - 1
        cur = cur_ref[...]
        halo = jnp.where(last, 0.0, nxt_ref[...])
        acc = None
        for j in range(CONV):
            term = cw_ref[j:j + 1, :] * _shift_up(cur, halo, CONV - 1 - j)
            acc = term if acc is None else acc + term
        o_ref[...] = acc.astype(BF16)

    return _pcall(
        body, name="conv_bwd_input", grid=(nt,),
        in_specs=[_rows(ts, c),
                  pl.BlockSpec((8, c), lambda i: (jnp.minimum((i + 1) * hb, nblk8 - 1), 0)),
                  _full((CONV, c)), _ANY],
        out_specs=_rows(ts, c), out_shape=SDS(dproj.shape, BF16), aliases={3: 0})(
            dconv, dconv, conv_w, dproj)


def _inv_unit_lower(a):
    n = a[0].shape[0]
    i = lax.broadcasted_iota(jnp.int32, (n, n), 0)
    j = lax.broadcasted_iota(jnp.int32, (n, n), 1)
    eye = jnp.where(i == j, 1.0, 0.0)
    t = [eye - ah for ah in a]
    x = a
    for _ in range(5):
        x = [_dot(xh, xh) for xh in x]
        t = [th + _dot(th, xh) for th, xh in zip(t, x)]
    return t


def _pair_common(q, k, gcol, grow, bcol):
    i = lax.broadcasted_iota(jnp.int32, (PAIR, PAIR), 0)
    j = lax.broadcasted_iota(jnp.int32, (PAIR, PAIR), 1)
    same = jnp.right_shift(i, 6) == jnp.right_shift(j, 6)
    tril = same & (i >= j)
    strict = same & (i > j)
    dec = [jnp.where(tril, jnp.exp(jnp.minimum(gc - gr, 0.0)), 0.0) for gc, gr in zip(gcol, grow)]
    kk = [_dot_nt(kh, kh) for kh in k]
    qk = [_dot_nt(qh, kh) for qh, kh in zip(q, k)]
    a = [jnp.where(strict, b * kkh * d, 0.0) for b, kkh, d in zip(bcol, kk, dec)]
    t = _inv_unit_lower(a)
    p = [qkh * d for qkh, d in zip(qk, dec)]
    return dec, kk, a, t, p, tril, strict


def _ext(v, a):
    z = jnp.zeros_like(v)
    return jnp.concatenate([v, z] if a == 0 else [z, v], axis=0)


def _gdn_fwd(q, k, v, gb, gbt, nh, rider):
    s = q.shape[0]
    w = nh * HEAD
    npair = s // PAIR

    def body(q_ref, k_ref, v_ref, gb_ref, gbt_ref, o_ref, st_ref, s_ref):
        @pl.when(pl.program_id(0) == 0)
        def _():
            s_ref[...] = jnp.zeros_like(s_ref)

        heads = range(nh)
        hs = [slice(HEAD * h, HEAD * (h + 1)) for h in heads]
        gbv = gb_ref[...]
        q, k, v = [q_ref[:, s_] for s_ in hs], [k_ref[:, s_] for s_ in hs], [v_ref[:, s_] for s_ in hs]
        gcol = [_col(gbv, A_LANE + h) for h in heads]
        bcol = [_col(gbv, B_LANE + h) for h in heads]
        grow = [gbt_ref[A_LANE + h:A_LANE + h + 1, :] for h in heads]
        _, _, _, t, p, _, _ = _pair_common(q, k, gcol, grow, bcol)
        eg = [jnp.exp(gc) for gc in gcol]
        qg = [x * e for x, e in zip(q, eg)]
        kg = [x * e for x, e in zip(k, eg)]
        outs = []
        for a in range(2):
            sl = slice(CHUNK * a, CHUNK * (a + 1))
            st = [s_ref[h] for h in heads]
            for h in heads:
                st_ref[a, h] = st[h]
            r = [v[h][sl] - _dot(kg[h][sl], st[h]) for h in heads]
            vn = [_dot(t[h][sl], _ext(bcol[h][sl] * r[h], a)) for h in heads]
            outs.append([_dot(qg[h][sl], st[h]) + _dot(p[h][sl], _ext(vn[h], a)) for h in heads])
            gl = [_col(gr, CHUNK * (a + 1) - 1) for gr in grow]
            kd = [k[h][sl] * jnp.exp(gl[h] - gcol[h][sl]) for h in heads]
            upd = [_dot_tn(kd[h], vn[h]) for h in heads]
            for h in heads:
                s_ref[h] = jnp.exp(gl[h]) * st[h] + upd[h]
        for h in heads:
            o_ref[:, hs[h]] = jnp.concatenate([outs[0][h], outs[1][h]], axis=0)

    return _pcall_riding(
        body, rider, name="gdn_fwd", grid=(npair,),
        in_specs=[_rows(PAIR, w), _rows(PAIR, w), _rows(PAIR, w), _rows(PAIR, LANE),
                  pl.BlockSpec((LANE, PAIR), lambda i: (0, i))],
        out_specs=[_rows(PAIR, w), pl.BlockSpec((2, nh, HEAD, HEAD), lambda i: (i, 0, 0, 0))],
        out_shape=[SDS((s, w), F32), SDS((2 * npair, nh, HEAD, HEAD), F32)],
        scratch=[pltpu.VMEM((nh, HEAD, HEAD), F32)], args=(q, k, v, gb, gbt))


def _gdn_bwd(q, k, v, gb, gbt, states, do, nh, rider):
    s = q.shape[0]
    w = nh * HEAD
    npair = s // PAIR
    rev = lambda i: (npair - 1 - i, 0)

    def body(q_ref, k_ref, v_ref, gb_ref, gbt_ref, st_ref, do_ref, dq_ref, dk_ref, dv_ref, dgb_ref,
             ds_ref):
        @pl.when(pl.program_id(0) == 0)
        def _():
            ds_ref[...] = jnp.zeros_like(ds_ref)

        lane = _lane_iota((PAIR, LANE))
        row = lax.broadcasted_iota(jnp.int32, (CHUNK, 1), 0)
        heads = range(nh)
        hs = [slice(HEAD * h, HEAD * (h + 1)) for h in heads]
        gbv = gb_ref[...]
        q, k, v = [q_ref[:, s_] for s_ in hs], [k_ref[:, s_] for s_ in hs], [v_ref[:, s_] for s_ in hs]
        do = [do_ref[:, s_] for s_ in hs]
        gcol = [_col(gbv, A_LANE + h) for h in heads]
        bcol = [_col(gbv, B_LANE + h) for h in heads]
        grow = [gbt_ref[A_LANE + h:A_LANE + h + 1, :] for h in heads]
        dec, kk, amat, t, p, tril, strict = _pair_common(q, k, gcol, grow, bcol)
        tt, pt = [x.T for x in t], [x.T for x in p]
        eg = [jnp.exp(gc) for gc in gcol]
        qg = [x * e for x, e in zip(q, eg)]
        kg = [x * e for x, e in zip(k, eg)]
        sums = lambda x: jnp.sum(x, axis=-1, keepdims=True)
        rs, vns = [None, None], [None, None]
        for a in range(2):
            sl = slice(CHUNK * a, CHUNK * (a + 1))
            rs[a] = [v[h][sl] - _dot(kg[h][sl], st_ref[a, h]) for h in heads]
            vns[a] = [_dot(t[h][sl], _ext(bcol[h][sl] * rs[a][h], a)) for h in heads]
        dsn = [ds_ref[h] for h in heads]
        dqs, dks, dvs, dgcs, dbs, drbs = ([None, None] for _ in range(6))
        for a in (1, 0):
            sl = slice(CHUNK * a, CHUNK * (a + 1))
            st = [st_ref[a, h] for h in heads]
            gl = [_col(gr, CHUNK * (a + 1) - 1) for gr in grow]
            egl = [jnp.exp(x) for x in gl]
            dk_dec = [jnp.exp(gl[h] - gcol[h][sl]) for h in heads]
            kd = [k[h][sl] * dk_dec[h] for h in heads]
            d_vn = [_dot(pt[h][sl], _ext(do[h][sl], a)) + _dot(kd[h], dsn[h]) for h in heads]
            d_qg = [_dot_nt(do[h][sl], st[h]) for h in heads]
            d_rb = [_dot(tt[h][sl], _ext(d_vn[h], a)) for h in heads]
            d_r = [bcol[h][sl] * d_rb[h] for h in heads]
            d_kg = [-_dot_nt(d_r[h], st[h]) for h in heads]
            d_kd = [_dot_nt(vns[a][h], dsn[h]) for h in heads]
            dsn_new = [_dot_tn(qg[h][sl], do[h][sl]) - _dot_tn(kg[h][sl], d_r[h]) for h in heads]
            dbs[a] = [sums(d_rb[h] * rs[a][h]) for h in heads]
            dgl = [egl[h] * jnp.sum(dsn[h] * st[h], keepdims=True) + jnp.sum(d_kd[h] * kd[h], keepdims=True)
                   for h in heads]
            dgcs[a] = [sums(d_qg[h] * qg[h][sl]) + sums(d_kg[h] * kg[h][sl]) - sums(d_kd[h] * kd[h])
                       + jnp.where(row == CHUNK - 1, dgl[h], 0.0) for h in heads]
            dqs[a] = [d_qg[h] * eg[h][sl] for h in heads]
            dks[a] = [d_kg[h] * eg[h][sl] + d_kd[h] * dk_dec[h] for h in heads]
            dvs[a] = d_r
            drbs[a] = d_rb
            dsn = [dsn_new[h] + egl[h] * dsn[h] for h in heads]
        for h in heads:
            ds_ref[h] = dsn[h]
        cat = lambda xs, h: jnp.concatenate([xs[0][h], xs[1][h]], axis=0)
        vn = [cat(vns, h) for h in heads]
        d_rb = [cat(drbs, h) for h in heads]
        dp = [jnp.where(tril, _dot_nt(do[h], vn[h]), 0.0) for h in heads]
        dam = [jnp.where(strict, -_dot_nt(d_rb[h], vn[h]), 0.0) for h in heads]
        g_p = [dp[h] * dec[h] for h in heads]
        g_a = [dam[h] * dec[h] for h in heads]
        gbk = [bcol[h] * g_a[h] for h in heads]
        dq2 = [_dot(g_p[h], k[h]) for h in heads]
        dk2 = [_dot_tn(g_p[h], q[h]) + _dot(gbk[h], k[h]) + _dot_tn(gbk[h], k[h]) for h in heads]
        dgb = jnp.zeros((PAIR, LANE), F32)
        for h in heads:
            dq_ref[:, hs[h]] = cat(dqs, h) + dq2[h]
            dk_ref[:, hs[h]] = cat(dks, h) + dk2[h]
            dv_ref[:, hs[h]] = cat(dvs, h)
            dbeta = cat(dbs, h) + sums(g_a[h] * kk[h])
            mm = dp[h] * p[h] + dam[h] * amat[h]
            dgc = cat(dgcs, h) + sums(mm) - sums(mm.T)
            dgb = dgb + jnp.where(lane == A_LANE + h, dgc, 0.0) + jnp.where(lane == B_LANE + h, dbeta, 0.0)
        dgb_ref[...] = dgb

    return _pcall_riding(
        body, rider, name="gdn_bwd", grid=(npair,),
        in_specs=[pl.BlockSpec((PAIR, w), rev), pl.BlockSpec((PAIR, w), rev), pl.BlockSpec((PAIR, w), rev),
                  pl.BlockSpec((PAIR, LANE), rev),
                  pl.BlockSpec((LANE, PAIR), lambda i: (0, npair - 1 - i)),
                  pl.BlockSpec((2, nh, HEAD, HEAD), lambda i: (npair - 1 - i, 0, 0, 0)),
                  pl.BlockSpec((PAIR, w), rev)],
        out_specs=[pl.BlockSpec((PAIR, w), rev), pl.BlockSpec((PAIR, w), rev), pl.BlockSpec((PAIR, w), rev),
                   pl.BlockSpec((PAIR, LANE), rev)],
        out_shape=[SDS((s, w), F32), SDS((s, w), F32), SDS((s, w), F32), SDS((s, LANE), F32)],
        scratch=[pltpu.VMEM((nh, HEAD, HEAD), F32)], args=(q, k, v, gb, gbt, states, do))


def _mla_norm(proj, qw, kvw, col_q, col_kv):
    s = proj.shape[0]
    lr = qw.shape[1]
    ts = _tile(s, 512, 8)

    def body(cq_ref, ckv_ref, qw_ref, kvw_ref, oq_ref, okv_ref):
        for x_ref, w_ref, o_ref in ((cq_ref, qw_ref, oq_ref), (ckv_ref, kvw_ref, okv_ref)):
            xv = x_ref[...]
            r = lax.rsqrt(jnp.mean(xv * xv, axis=-1, keepdims=True) + EPS)
            o_ref[...] = (xv * r * w_ref[...]).astype(BF16)

    return _pcall(body, name="mla_norm", grid=(s // ts,),
                  in_specs=[_rows(ts, lr, col_q), _rows(ts, lr, col_kv), _full((1, lr)), _full((1, lr))],
                  out_specs=[_rows(ts, lr), _rows(ts, lr)],
                  out_shape=[SDS((s, lr), BF16), SDS((s, lr), BF16)])(proj, proj, qw, kvw)


def _mla_norm_bwd(proj, qw, kvw, dq, dkv, dproj, col_q, col_kv):
    s = proj.shape[0]
    lr = qw.shape[1]
    ts = _tile(s, 512, 8)

    assert col_kv == col_q + 1 and col_q % 2 == 0

    def body(cq_ref, ckv_ref, qw_ref, kvw_ref, dq_ref, dkv_ref, _, o_ref, dqw_ref, dkvw_ref):
        @pl.when(pl.program_id(0) == 0)
        def _():
            dqw_ref[...] = jnp.zeros_like(dqw_ref)
            dkvw_ref[...] = jnp.zeros_like(dkvw_ref)

        for k, (x_ref, w_ref, d_ref, dw_ref) in enumerate(((cq_ref, qw_ref, dq_ref, dqw_ref),
                                                           (ckv_ref, kvw_ref, dkv_ref, dkvw_ref))):
            xv, dh = x_ref[...], d_ref[...]
            r = lax.rsqrt(jnp.mean(xv * xv, axis=-1, keepdims=True) + EPS)
            xh = xv * r
            dw_ref[...] += jnp.sum(dh * xh, axis=0, keepdims=True)
            dxh = dh * w_ref[...]
            o_ref[:, lr * k:lr * (k + 1)] = (
                r * (dxh - xh * jnp.mean(dxh * xh, axis=-1, keepdims=True))).astype(BF16)

    return _pcall(body, name="mla_norm_bwd", grid=(s // ts,),
                  in_specs=[_rows(ts, lr, col_q), _rows(ts, lr, col_kv), _full((1, lr)), _full((1, lr)),
                            _rows(ts, lr), _rows(ts, lr), _ANY],
                  out_specs=[_rows(ts, 2 * lr, col_q // 2), _full((1, lr)), _full((1, lr))],
                  out_shape=[SDS(dproj.shape, BF16), SDS((1, lr), F32), SDS((1, lr), F32)],
                  aliases={6: 0})(proj, proj, qw, kvw, dq, dkv, dproj)


def _rope_tables(pos, invf, sgn):
    ang = pos * invf
    return jnp.cos(ang), jnp.sin(ang) * sgn


def _swap_halves_lanes(y):
    lane = _lane_iota(y.shape)
    return jnp.where(lane < ROPE // 2, pltpu.roll(y, LANE - ROPE // 2, 1), pltpu.roll(y, ROPE // 2, 1))


def _rope_consts():
    half = ROPE // 2
    inv = ROPE_THETA ** (-jnp.arange(half, dtype=F32) / half)
    invf = jnp.concatenate([inv, inv, jnp.zeros((LANE - ROPE,), F32)])[None, :]
    sgn = jnp.concatenate([-jnp.ones((half,), F32), jnp.ones((half,), F32),
                           jnp.zeros((LANE - ROPE,), F32)])[None, :]
    return invf, sgn


def _mla_rope(qraw, kvraw, proj, pos, nh, misc_col):
    s = qraw.shape[0]
    ts = _tile(s, MLA_BLOCK)
    wq = nh * 2 * HEAD
    invf, sgn = _rope_consts()

    def body(q_ref, kv_ref, misc_ref, pos_ref, if_ref, sg_ref, qc_ref, kc_ref, v_ref, qt_ref, vt_ref, kt_ref):
        c, sn = _rope_tables(pos_ref[...], if_ref[...], sg_ref[...])
        lane = _lane_iota(c.shape)
        rot = lambda xb: xb * c + _swap_halves_lanes(xb) * sn
        qs = SM_SCALE * LOG2E
        krot32 = jnp.where(lane < ROPE, rot(misc_ref[...]), 0.0)
        krot, krot_t = krot32.astype(BF16), krot32.T.astype(BF16)
        for h in range(nh):
            b0 = 2 * HEAD * h
            qn = q_ref[:, b0:b0 + HEAD].astype(F32) * qs
            qr = rot(q_ref[:, b0 + HEAD:b0 + 2 * HEAD].astype(F32)) * qs
            qc_ref[:, b0:b0 + HEAD] = qn.astype(BF16)
            qc_ref[:, b0 + HEAD:b0 + 2 * HEAD] = qr.astype(BF16)
            qt_ref[b0:b0 + HEAD, :] = qn.T.astype(BF16)
            qt_ref[b0 + HEAD:b0 + 2 * HEAD, :] = qr.T.astype(BF16)
            kn = kv_ref[:, b0:b0 + HEAD]
            kc_ref[:, b0:b0 + HEAD] = kn.astype(BF16)
            kc_ref[:, b0 + HEAD:b0 + 2 * HEAD] = krot
            kt_ref[b0:b0 + HEAD, :] = kn.astype(F32).T.astype(BF16)
            kt_ref[b0 + HEAD:b0 + 2 * HEAD, :] = krot_t
            vh = kv_ref[:, wq + HEAD * h:wq + HEAD * (h + 1)]
            v_ref[:, HEAD * h:HEAD * (h + 1)] = vh.astype(BF16)
            vt_ref[HEAD * h:HEAD * (h + 1), :] = vh.astype(F32).T.astype(BF16)

    return _pcall(body, name="mla_rope", grid=(s // ts,),
                  in_specs=[_rows(ts, wq), _rows(ts, wq + nh * HEAD), _rows(ts, LANE, misc_col),
                            _rows(ts, 1), _full((1, LANE)), _full((1, LANE))],
                  out_specs=[_rows(ts, wq), _rows(ts, wq), _rows(ts, nh * HEAD),
                             pl.BlockSpec((None, wq, ts), lambda i: (i, 0, 0)),
                             pl.BlockSpec((None, nh * HEAD, ts), lambda i: (i, 0, 0)),
                             pl.BlockSpec((None, wq, ts), lambda i: (i, 0, 0))],
                  out_shape=[SDS((s, wq), BF16), SDS((s, wq), BF16), SDS((s, nh * HEAD), BF16),
                             SDS((s // ts, wq, ts), BF16), SDS((s // ts, nh * HEAD, ts), BF16),
                             SDS((s // ts, wq, ts), BF16)])(
                      qraw, kvraw, proj, pos, invf, sgn)


def _mla_rope_bwd(dqt, dkc, dv, pos, nh):
    nb, wq, ts = dqt.shape
    s = nb * ts
    invf, sgn = _rope_consts()

    def body(dq_ref, dk_ref, dv_ref, pos_ref, if_ref, sg_ref, oq_ref, okv_ref, okr_ref):
        c, sn = _rope_tables(pos_ref[...], if_ref[...], sg_ref[...])
        lane = _lane_iota(c.shape)
        unrot = lambda d: d * c + _swap_halves_lanes(d * sn)
        dkr = jnp.zeros(c.shape, F32)
        for h in range(nh):
            b0 = 2 * HEAD * h
            oq_ref[:, b0:b0 + HEAD] = (dq_ref[b0:b0 + HEAD, :].T * SM_SCALE).astype(BF16)
            oq_ref[:, b0 + HEAD:b0 + 2 * HEAD] = (
                unrot(dq_ref[b0 + HEAD:b0 + 2 * HEAD, :].T) * SM_SCALE).astype(BF16)
            okv_ref[:, b0:b0 + HEAD] = (dk_ref[:, b0:b0 + HEAD] * LN2).astype(BF16)
            okv_ref[:, b0 + HEAD:b0 + 2 * HEAD] = jnp.zeros((ts, HEAD), BF16)
            dkr = dkr + dk_ref[:, b0 + HEAD:b0 + 2 * HEAD]
        okv_ref[:, wq:] = dv_ref[...].astype(BF16)
        okr_ref[...] = jnp.where(lane < ROPE, unrot(jnp.where(lane < ROPE, dkr * LN2, 0.0)), 0.0)

    return _pcall(body, name="mla_rope_bwd", grid=(s // ts,),
                  in_specs=[pl.BlockSpec((None, wq, ts), lambda i: (i, 0, 0)), _rows(ts, wq), _rows(ts, nh * HEAD),
                            _rows(ts, 1), _full((1, LANE)), _full((1, LANE))],
                  out_specs=[_rows(ts, wq), _rows(ts, wq + nh * HEAD), _rows(ts, LANE)],
                  out_shape=[SDS((s, wq), BF16), SDS((s, wq + nh * HEAD), BF16), SDS((s, LANE), F32)])(
                      dqt, dkc, dv, pos, invf, sgn)


MLA_HP = 2
MLA_FWD_HP = 4


def _mla_fwd(qt, kc, vt, nh, rider):
    nb, _, blk = qt.shape
    s = nb * blk
    hp = MLA_FWD_HP if nh % MLA_FWD_HP == 0 else MLA_HP
    assert nh % hp == 0 and hp % MLA_HP == 0
    once = pl.Buffered(1)
    r_in, r_out = len(rider.arrays), len(rider.out_shapes)

    def body(*refs):
        qt_ref, k_ref, vt_ref = refs[:3]
        o_ref, lse_ref = refs[3 + r_in:5 + r_in]
        m_sc, l_sc, acc = refs[5 + r_in + r_out:8 + r_in + r_out]
        r_refs = (refs[3:3 + r_in], refs[5 + r_in:5 + r_in + r_out], refs[8 + r_in + r_out:])
        i = pl.program_id(1)
        grid_step = pl.program_id(0) * nb + i
        _ride_begin(rider, r_refs, grid_step)
        m_sc[...] = jnp.full_like(m_sc, -1e30)
        l_sc[...] = jnp.zeros_like(l_sc)
        acc[...] = jnp.zeros_like(acc)
        es = range(hp)

        def step(j, masked):
            rows = pl.ds(pl.multiple_of(j * blk, blk), blk)
            sc = [_dot(k_ref[rows, 2 * HEAD * e:2 * HEAD * (e + 1)], qt_ref[2 * HEAD * e:2 * HEAD * (e + 1), :])
                  for e in es]
            if masked:
                key = lax.broadcasted_iota(jnp.int32, (blk, blk), 0)
                qry = lax.broadcasted_iota(jnp.int32, (blk, blk), 1)
                sc = [jnp.where(key <= qry, x, -1e30) for x in sc]
            m_prev = [m_sc[e] for e in es]
            m_new = [jnp.maximum(m_prev[e], jnp.max(sc[e], axis=0, keepdims=True)) for e in es]
            p = [jnp.exp2(sc[e] - m_new[e]) for e in es]
            alpha = [jnp.exp2(m_prev[e] - m_new[e]) for e in es]
            pv = [_dot(vt_ref[j, HEAD * e:HEAD * (e + 1), :], p[e]) for e in es]
            for e in es:
                l_sc[e] = alpha[e] * l_sc[e] + jnp.sum(p[e], axis=0, keepdims=True)
                acc[e] = alpha[e] * acc[e] + pv[e]
                m_sc[e] = m_new[e]

        def loop_body(j, carry):
            step(j, False)
            return carry

        lax.fori_loop(0, i, loop_body, 0)
        step(i, True)
        for e in es:
            o_ref[:, HEAD * e:HEAD * (e + 1)] = (acc[e] / l_sc[e]).T
            lse_ref[e] = jnp.broadcast_to(m_sc[e] + jnp.log(l_sc[e]) * LOG2E, (8, blk))
        _ride_end(rider, r_refs, grid_step, (nh // hp) * nb)

    outs = _pcall(
        body, name="mla_fwd", grid=(nh // hp, nb),
        in_specs=[pl.BlockSpec((None, hp * 2 * HEAD, blk), lambda g, i: (i, g, 0)),
                  pl.BlockSpec((s, hp * 2 * HEAD), lambda g, i: (0, g), pipeline_mode=once),
                  pl.BlockSpec((nb, hp * HEAD, blk), lambda g, i: (0, g, 0), pipeline_mode=once)]
        + [_ANY] * r_in,
        out_specs=[pl.BlockSpec((blk, hp * HEAD), lambda g, i: (i, g)),
                   pl.BlockSpec((hp, None, 8, blk), lambda g, i: (g, i, 0, 0))] + [_ANY] * r_out,
        out_shape=[SDS((s, nh * HEAD), F32), SDS((nh, nb, 8, blk), F32)] + rider.out_shapes,
        scratch=[pltpu.VMEM((hp, 1, blk), F32), pltpu.VMEM((hp, 1, blk), F32),
                 pltpu.VMEM((hp, HEAD, blk), F32)] + rider.scratch)(qt, kc, vt, *rider.arrays)
    return outs[0], outs[1], outs[2:]


def _mla_bwd(qc, qt, kc, kt, v, do, dot, lse, delta, nh, rider):
    nb, _, blk = qt.shape
    s = nb * blk
    hp = MLA_HP
    once = pl.Buffered(1)
    r_in, r_out = len(rider.arrays), len(rider.out_shapes)
    qs = [slice(2 * HEAD * e, 2 * HEAD * (e + 1)) for e in range(hp)]
    vs = [slice(HEAD * e, HEAD * (e + 1)) for e in range(hp)]

    def body(*refs):
        q_ref, qt_ref, do_ref, dot_ref, lse_ref, dl_ref, k_ref, kt_ref, v_ref = refs[:9]
        dqt_ref, dk_ref, dv_ref = refs[9 + r_in:12 + r_in]
        dk_acc, dv_acc = refs[12 + r_in + r_out:14 + r_in + r_out]
        r_refs = (refs[9:9 + r_in], refs[12 + r_in:12 + r_in + r_out], refs[14 + r_in + r_out:])
        j = pl.program_id(1)
        grid_step = pl.program_id(0) * nb + j
        _ride_begin(rider, r_refs, grid_step)

        @pl.when(j == 0)
        def _():
            dqt_ref[...] = jnp.zeros_like(dqt_ref)

        dk_acc[...] = jnp.zeros_like(dk_acc)
        dv_acc[...] = jnp.zeros_like(dv_acc)
        es = range(hp)
        kj = [k_ref[:, qs[e]] for e in es]
        ktj = [kt_ref[qs[e], :] for e in es]
        vj = [v_ref[:, vs[e]] for e in es]

        def step(i, masked):
            rows = pl.ds(pl.multiple_of(i * blk, blk), blk)
            sc = [_dot(kj[e], qt_ref[i, qs[e], :]) for e in es]
            dp = [_dot(vj[e], dot_ref[i, vs[e], :]) for e in es]
            if masked:
                key = lax.broadcasted_iota(jnp.int32, (blk, blk), 0)
                qry = lax.broadcasted_iota(jnp.int32, (blk, blk), 1)
                sc = [jnp.where(key <= qry, x, -1e30) for x in sc]
            p = [jnp.exp2(sc[e] - lse_ref[e, i, 0:1, :]) for e in es]
            ds = [p[e] * (dp[e] - dl_ref[e, i, 0:1, :]) for e in es]
            dv = [_dot(p[e], do_ref[rows, vs[e]]) for e in es]
            dk = [_dot(ds[e], q_ref[rows, qs[e]]) for e in es]
            dq = [_dot(ktj[e], ds[e]) for e in es]
            for e in es:
                dv_acc[:, vs[e]] += dv[e]
                dk_acc[:, qs[e]] += dk[e]
                dqt_ref[i, qs[e], :] += dq[e]

        def loop_body(i, carry):
            step(i, False)
            return carry

        step(j, True)
        lax.fori_loop(j + 1, nb, loop_body, 0)
        dk_ref[...] = dk_acc[...]
        dv_ref[...] = dv_acc[...]
        _ride_end(rider, r_refs, grid_step, (nh // hp) * nb)

    rows_spec = pl.BlockSpec((hp, nb, 8, blk), lambda g, j: (g, 0, 0, 0), pipeline_mode=once)
    outs = _pcall(
        body, name="mla_bwd", grid=(nh // hp, nb),
        in_specs=[pl.BlockSpec((s, hp * 2 * HEAD), lambda g, j: (0, g), pipeline_mode=once),
                  pl.BlockSpec((nb, hp * 2 * HEAD, blk), lambda g, j: (0, g, 0), pipeline_mode=once),
                  pl.BlockSpec((s, hp * HEAD), lambda g, j: (0, g), pipeline_mode=once),
                  pl.BlockSpec((nb, hp * HEAD, blk), lambda g, j: (0, g, 0), pipeline_mode=once),
                  rows_spec, rows_spec,
                  pl.BlockSpec((blk, hp * 2 * HEAD), lambda g, j: (j, g)),
                  pl.BlockSpec((None, hp * 2 * HEAD, blk), lambda g, j: (j, g, 0)),
                  pl.BlockSpec((blk, hp * HEAD), lambda g, j: (j, g))] + [_ANY] * r_in,
        out_specs=[pl.BlockSpec((nb, hp * 2 * HEAD, blk), lambda g, j: (0, g, 0), pipeline_mode=once),
                   pl.BlockSpec((blk, hp * 2 * HEAD), lambda g, j: (j, g)),
                   pl.BlockSpec((blk, hp * HEAD), lambda g, j: (j, g))] + [_ANY] * r_out,
        out_shape=[SDS((nb, nh * 2 * HEAD, blk), F32), SDS((s, nh * 2 * HEAD), F32),
                   SDS((s, nh * HEAD), F32)] + rider.out_shapes,
        scratch=[pltpu.VMEM((blk, hp * 2 * HEAD), F32), pltpu.VMEM((blk, hp * HEAD), F32)] + rider.scratch,
        vmem=VMEM_LIMIT_WIDE)(qc, qt, do, dot, lse, delta, kc, kt, v, *rider.arrays)
    return outs[0], outs[1], outs[2], outs[3:]


def _mix_fwd(og, proj, om, gw, mw, nh, z_col):
    s = og.shape[0]
    w = nh * HEAD
    ts = _tile(s, 256, 8)

    def body(og_ref, z_ref, om_ref, gw_ref, mw_ref, o_ref):
        for h in range(nh):
            hs = slice(HEAD * h, HEAD * (h + 1))
            a = og_ref[:, hs]
            r = lax.rsqrt(jnp.mean(a * a, axis=-1, keepdims=True) + EPS)
            o_ref[:, hs] = (a * r * gw_ref[...] * _silu(z_ref[:, hs])).astype(BF16)
            b = om_ref[:, hs]
            r = lax.rsqrt(jnp.mean(b * b, axis=-1, keepdims=True) + EPS)
            o_ref[:, w + HEAD * h:w + HEAD * (h + 1)] = (b * r * mw_ref[...]).astype(BF16)

    return _pcall(body, name="mix_fwd", grid=(s // ts,),
                  in_specs=[_rows(ts, w), _rows(ts, w, z_col), _rows(ts, w), _full((1, HEAD)),
                            _full((1, HEAD))],
                  out_specs=_rows(ts, 2 * w), out_shape=SDS((s, 2 * w), BF16))(og, proj, om, gw, mw)


def _mix_bwd(dmix, og, proj, om, gw, mw, nh, z_col):
    s = og.shape[0]
    w = nh * HEAD
    ts = _tile(s, MLA_BLOCK)

    def body(d_ref, og_ref, z_ref, om_ref, gw_ref, mw_ref, dog_ref, dz_ref, dom_ref, dgw_ref, dmw_ref,
             dl_ref, domt_ref):
        @pl.when(pl.program_id(0) == 0)
        def _():
            dgw_ref[...] = jnp.zeros_like(dgw_ref)
            dmw_ref[...] = jnp.zeros_like(dmw_ref)

        dgw = jnp.zeros((1, HEAD), F32)
        dmw = jnp.zeros((1, HEAD), F32)
        for h in range(nh):
            hs = slice(HEAD * h, HEAD * (h + 1))
            a, z, dy = og_ref[:, hs], z_ref[:, hs], d_ref[:, hs]
            r = lax.rsqrt(jnp.mean(a * a, axis=-1, keepdims=True) + EPS)
            ah = a * r
            sz = _silu(z)
            dz_ref[:, hs] = (dy * (ah * gw_ref[...]) * _dsilu(z)).astype(BF16)
            dn = dy * sz
            dgw = dgw + jnp.sum(dn * ah, axis=0, keepdims=True)
            dah = dn * gw_ref[...]
            dog_ref[:, hs] = r * (dah - ah * jnp.mean(dah * ah, axis=-1, keepdims=True))
            b, dyb = om_ref[:, hs], d_ref[:, w + HEAD * h:w + HEAD * (h + 1)]
            r = lax.rsqrt(jnp.mean(b * b, axis=-1, keepdims=True) + EPS)
            bh = b * r
            dmw = dmw + jnp.sum(dyb * bh, axis=0, keepdims=True)
            dbh = dyb * mw_ref[...]
            dom = r * (dbh - bh * jnp.mean(dbh * bh, axis=-1, keepdims=True))
            dom_ref[:, hs] = dom.astype(BF16)
            domt_ref[hs, :] = dom.T.astype(BF16)
            delta = jnp.broadcast_to(jnp.sum(dom * b, axis=-1, keepdims=True), (ts, LANE))
            dl_ref[h] = delta.T[0:8, :]
        dgw_ref[...] += dgw
        dmw_ref[...] += dmw

    return _pcall(body, name="mix_bwd", grid=(s // ts,),
                  in_specs=[_rows(ts, 2 * w), _rows(ts, w), _rows(ts, w, z_col), _rows(ts, w),
                            _full((1, HEAD)), _full((1, HEAD))],
                  out_specs=[_rows(ts, w), _rows(ts, w, z_col), _rows(ts, w), _full((1, HEAD)), _full((1, HEAD)),
                             pl.BlockSpec((nh, None, 8, ts), lambda i: (0, i, 0, 0)),
                             pl.BlockSpec((None, w, ts), lambda i: (i, 0, 0))],
                  out_shape=[SDS((s, w), F32), SDS((s, proj.shape[1]), BF16), SDS((s, w), BF16),
                             SDS((1, HEAD), F32), SDS((1, HEAD), F32),
                             SDS((nh, s // ts, 8, ts), F32), SDS((s // ts, w, ts), BF16)])(
                                 dmix, og, proj, om, gw, mw)


def _swiglu_fwd(h2, wg, wu):
    m, kdim = h2.shape
    tn = wg.shape[2]
    n = 4 * tn
    tm, tk = _tile(m, 512), _tile(kdim, 2048)
    nk = kdim // tk

    def body(a_ref, g_ref, u_ref, act_ref, go_ref, uo_ref, gacc, uacc):
        k = pl.program_id(2)

        @pl.when(k == 0)
        def _():
            gacc[...] = jnp.zeros_like(gacc)
            uacc[...] = jnp.zeros_like(uacc)

        a = a_ref[...]
        gacc[...] += _dot(a, g_ref[...])
        uacc[...] += _dot(a, u_ref[...])

        @pl.when(k == nk - 1)
        def _():
            g, u = gacc[...], uacc[...]
            act_ref[...] = (_silu(g) * u).astype(BF16)
            go_ref[...] = g.astype(BF16)
            uo_ref[...] = u.astype(BF16)

    a_spec = pl.BlockSpec((tm, tk), lambda i, j, k: (i, k))
    b_spec = pl.BlockSpec((None, tk, tn), lambda i, j, k: (j, k, 0))
    o_spec = pl.BlockSpec((tm, tn), lambda i, j, k: (i, j))
    return _pcall(body, name="swiglu_fwd", grid=(m // tm, n // tn, nk),
                  in_specs=[a_spec, b_spec, b_spec], out_specs=[o_spec] * 3,
                  out_shape=[SDS((m, n), BF16)] * 3,
                  scratch=[pltpu.VMEM((tm, tn), F32), pltpu.VMEM((tm, tn), F32)])(h2, wg, wu)


def _swiglu_bwd(dx3, wd, g, u):
    m, kdim = dx3.shape
    n = wd.shape[0]
    tm, tn = _tile(m, 1024), _tile(n, 512)

    def body(a_ref, b_ref, g_ref, u_ref, dg_ref, du_ref):
        da = _dot_nt(a_ref[...], b_ref[...])
        gv, uv = g_ref[...].astype(F32), u_ref[...].astype(F32)
        dg_ref[...] = (da * uv * _dsilu(gv)).astype(BF16)
        du_ref[...] = (da * _silu(gv)).astype(BF16)

    a_spec = pl.BlockSpec((tm, kdim), lambda i, j: (i, 0))
    b_spec = pl.BlockSpec((tn, kdim), lambda i, j: (j, 0))
    o_spec = pl.BlockSpec((tm, tn), lambda i, j: (i, j))
    return _pcall(body, name="swiglu_bwd", grid=(m // tm, n // tn),
                  in_specs=[a_spec, b_spec, o_spec, o_spec], out_specs=[o_spec] * 2,
                  out_shape=[SDS((m, n), BF16)] * 2)(dx3, wd, g, u)


def _sum_pair(g, recv, place, name):
    _, _, rh, c = g.shape
    tr = _tile(rh, 256, 16)

    def body(pl_ref, g_ref, r_ref, o16_ref, own_ref):
        sm = g_ref[...].astype(F32) + r_ref[...].astype(F32)
        o16_ref[...] = sm.astype(BF16)

        @pl.when(pl.program_id(1) == pl_ref[1])
        def _():
            own_ref[...] = sm

    grid_spec = pltpu.PrefetchScalarGridSpec(
        num_scalar_prefetch=1, grid=(rh // tr, 4),
        in_specs=[pl.BlockSpec((None, None, tr, c), lambda i, t, p: (t, p[0], i, 0)),
                  pl.BlockSpec((None, tr, c), lambda i, t, p: (t, i, 0))],
        out_specs=[pl.BlockSpec((None, tr, c), lambda i, t, p: (t, i, 0)),
                   pl.BlockSpec((tr, c), lambda i, t, p: (i, 0))])
    return pl.pallas_call(
        body, name=name, grid_spec=grid_spec,
        out_shape=[SDS((4, rh, c), BF16), SDS((rh, c), F32)],
        compiler_params=pltpu.CompilerParams(dimension_semantics=("arbitrary",) * 2,
                                             vmem_limit_bytes=VMEM_LIMIT))(place, g, recv)


def _sum_chips(own, recv, name):
    rh, c = own.shape
    tr = _tile(rh, 256, 16)

    def body(o_ref, r_ref, out_ref):
        acc = o_ref[...]
        for j in range(3):
            acc = acc + r_ref[j].astype(F32)
        out_ref[...] = acc

    return _pcall(body, name=name, grid=(rh // tr,),
                  in_specs=[_rows(tr, c), pl.BlockSpec((3, tr, c), lambda i: (0, i, 0))],
                  out_specs=_rows(tr, c), out_shape=SDS(own.shape, F32))(own, recv)


def _adamw_update(wv, gv, mv, vv):
    mn = ADAM_B1 * mv + (1.0 - ADAM_B1) * gv
    vn = ADAM_B2 * vv + (1.0 - ADAM_B2) * (gv * gv)
    m_hat = mn / (1.0 - ADAM_B1 ** ADAM_STEP)
    v_hat = vn / (1.0 - ADAM_B2 ** ADAM_STEP)
    return -ADAM_LR * (m_hat / (jnp.sqrt(v_hat) + ADAM_EPS) + ADAM_WD * wv), mn, vn


def _adamw(w, g, m, v, name):
    r, c = w.shape
    tr = _tile(r, 256, 8)

    def body(w_ref, g_ref, m_ref, v_ref, d_ref, mo_ref, vo_ref):
        d_ref[...], mo_ref[...], vo_ref[...] = _adamw_update(w_ref[...], g_ref[...], m_ref[...], v_ref[...])

    spec = _rows(tr, c)
    return _pcall(body, name=name, grid=(r // tr,), in_specs=[spec] * 4, out_specs=[spec] * 3,
                  out_shape=[SDS(w.shape, F32)] * 3)(w, g, m, v)


def _adamw_halves(w, mine, theirs, m, v, place, name):
    r, c = w.shape
    rh = r // 2
    tr = _tile(rh, 256, 8)
    nt = rh // tr

    def body(p_ref, w_ref, a_ref, b_ref, m_ref, v_ref, g_ref, d_ref, mo_ref, vo_ref):
        gv = jnp.where(pl.program_id(0) // nt == p_ref[0], a_ref[...], b_ref[...])
        g_ref[...] = gv
        d_ref[...], mo_ref[...], vo_ref[...] = _adamw_update(w_ref[...], gv, m_ref[...], v_ref[...])

    full = pl.BlockSpec((tr, c), lambda i, p: (i, 0))
    half = pl.BlockSpec((tr, c), lambda i, p: (i % nt, 0))
    grid_spec = pltpu.PrefetchScalarGridSpec(num_scalar_prefetch=1, grid=(2 * nt,),
                                             in_specs=[full, half, half, full, full], out_specs=[full] * 4)
    return pl.pallas_call(
        body, name=name, grid_spec=grid_spec, out_shape=[SDS(w.shape, F32)] * 4,
        compiler_params=pltpu.CompilerParams(dimension_semantics=("arbitrary",),
                                             vmem_limit_bytes=VMEM_LIMIT))(place, w, mine, theirs, m, v)


def _place():
    x, y, c = lax.axis_index("x"), lax.axis_index("y"), lax.axis_index("c")
    chips = [(1 - x, y), (x, 1 - y), (1 - x, 1 - y)]
    return x, y, c, chips


_ANY = pl.BlockSpec(memory_space=pl.ANY)


def _remote(src, dst, sems, k, to):
    return pltpu.make_async_remote_copy(src_ref=src, dst_ref=dst, send_sem=sems[0].at[k], recv_sem=sems[1].at[k],
                                        device_id=to, device_id_type=MESH)


class _Gather:
    def __init__(self, shards):
        n = len(shards)
        self.arrays = list(shards)
        self.out_shapes = [SDS((4,) + a.shape, a.dtype) for a in shards]
        self.scratch = [pltpu.SemaphoreType.DMA((7 * n,)), pltpu.SemaphoreType.DMA((7 * n,))]

    def _plan(self, ins, outs, sems):
        x, y, c, chips = _place()
        own, sib = 2 * x + y, (x, y, 1 - c)
        plan = []
        for wi, (w, o) in enumerate(zip(ins, outs)):
            rh = w.shape[0] // 2
            mine, theirs = pl.ds(c * rh, rh), pl.ds((1 - c) * rh, rh)
            whole = _remote(w, o.at[own], sems, 7 * wi + 6, sib)
            ici, d2d, d2d_in = [], [], []
            for j, (tx, ty) in enumerate(chips):
                t = 2 * tx + ty
                ici.append(_remote(w.at[mine], o.at[own, mine], sems, 7 * wi + j, (tx, ty, c)))
                d2d.append(_remote(o.at[t, mine], o.at[t, mine], sems, 7 * wi + 3 + j, sib))
                d2d_in.append(_remote(o.at[t, theirs], o.at[t, theirs], sems, 7 * wi + 3 + j, sib))
            plan.append((whole, ici, d2d, d2d_in))
        return plan

    def begin(self, ins, outs, sems):
        for whole, ici, _, _ in self._plan(ins, outs, sems):
            whole.start()
            for cp in ici:
                cp.start()

    def middle(self, ins, outs, sems):
        for _, ici, d2d, _ in self._plan(ins, outs, sems):
            for cp_in, cp_on in zip(ici, d2d):
                cp_in.wait_recv()
                cp_on.start()

    def finish(self, ins, outs, sems):
        for whole, ici, d2d, d2d_in in self._plan(ins, outs, sems):
            for cp in d2d_in:
                cp.wait_recv()
            for cp in ici + d2d:
                cp.wait_send()
            whole.wait()


class _Swap:
    def __init__(self, grads):
        n = len(grads)
        self.arrays = list(grads)
        self.out_shapes = [SDS((4,) + g.shape[2:], g.dtype) for g in grads]
        self.scratch = [pltpu.SemaphoreType.DMA((4 * n,)), pltpu.SemaphoreType.DMA((4 * n,))]

    def _plan(self, ins, outs, sems):
        x, y, c, _ = _place()
        return [_remote(g.at[t, 1 - c], o.at[t], sems, 4 * wi + t, (x, y, 1 - c))
                for wi, (g, o) in enumerate(zip(ins, outs)) for t in range(4)]

    def begin(self, ins, outs, sems):
        for cp in self._plan(ins, outs, sems):
            cp.start()

    def middle(self, ins, outs, sems):
        pass

    def finish(self, ins, outs, sems):
        for cp in self._plan(ins, outs, sems):
            cp.wait()


class _Exchange:
    def __init__(self, pieces):
        n = len(pieces)
        self.arrays = list(pieces)
        self.out_shapes = [SDS((3,) + p.shape[1:], p.dtype) for p in pieces]
        self.scratch = [pltpu.SemaphoreType.DMA((3 * n,)), pltpu.SemaphoreType.DMA((3 * n,))]

    def _plan(self, ins, outs, sems):
        x, y, c, chips = _place()
        return [_remote(g.at[2 * tx + ty], o.at[j], sems, 3 * wi + j, (tx, ty, c))
                for wi, (g, o) in enumerate(zip(ins, outs)) for j, (tx, ty) in enumerate(chips)]

    def begin(self, ins, outs, sems):
        for cp in self._plan(ins, outs, sems):
            cp.start()

    def middle(self, ins, outs, sems):
        pass

    def finish(self, ins, outs, sems):
        for cp in self._plan(ins, outs, sems):
            cp.wait()


class _Share:
    def __init__(self, totals):
        n = len(totals)
        self.arrays = list(totals)
        self.out_shapes = [SDS(t.shape, t.dtype) for t in totals]
        self.scratch = [pltpu.SemaphoreType.DMA((n,)), pltpu.SemaphoreType.DMA((n,))]

    def _plan(self, ins, outs, sems):
        x, y, c, _ = _place()
        return [_remote(t, o, sems, wi, (x, y, 1 - c)) for wi, (t, o) in enumerate(zip(ins, outs))]

    def begin(self, ins, outs, sems):
        for cp in self._plan(ins, outs, sems):
            cp.start()

    def middle(self, ins, outs, sems):
        pass

    def finish(self, ins, outs, sems):
        for cp in self._plan(ins, outs, sems):
            cp.wait()


def _ride_begin(rider, r_refs, step):
    @pl.when(step == 0)
    def _():
        rider.begin(*r_refs)


def _ride_end(rider, r_refs, step, nsteps):
    @pl.when(step == min(3 * nsteps // 4, nsteps - 1))
    def _():
        rider.middle(*r_refs)

    @pl.when(step == nsteps - 1)
    def _():
        rider.finish(*r_refs)


def _comm(rider, name):
    n_in, n_out = len(rider.arrays), len(rider.out_shapes)

    def body(*refs):
        r_refs = (refs[:n_in], refs[n_in:n_in + n_out], refs[n_in + n_out:])
        rider.begin(*r_refs)
        rider.middle(*r_refs)
        rider.finish(*r_refs)

    return pl.pallas_call(body, name=name, out_shape=rider.out_shapes, in_specs=[_ANY] * n_in,
                          out_specs=[_ANY] * n_out, scratch_shapes=rider.scratch)(*rider.arrays)


def _small_allreduce(pk, name):
    r = pk.shape[0]
    rels = [(dx, dy, dc) for dx in (0, 1) for dy in (0, 1) for dc in (0, 1) if dx or dy or dc]

    def body(p_ref, o_ref, buf, send_sems, recv_sems):
        x, y, c, _ = _place()
        me = 4 * x + 2 * y + c
        buf[me] = p_ref[...]
        cps = []
        for k, (dx, dy, dc) in enumerate(rels):
            to = (1 - x if dx else x, 1 - y if dy else y, 1 - c if dc else c)
            cps.append(pltpu.make_async_remote_copy(src_ref=p_ref, dst_ref=buf.at[me], send_sem=send_sems.at[k],
                                                    recv_sem=recv_sems.at[k], device_id=to,
                                                    device_id_type=MESH))
        for cpy in cps:
            cpy.start()
        for cpy in cps:
            cpy.wait()
        acc = buf[0]
        for d in range(1, 8):
            acc = acc + buf[d]
        o_ref[...] = acc

    vm = pl.BlockSpec(memory_space=pltpu.VMEM)
    return pl.pallas_call(body, name=name, out_shape=SDS(pk.shape, F32), in_specs=[vm], out_specs=vm,
                          scratch_shapes=[pltpu.VMEM((8, r, LANE), F32), pltpu.SemaphoreType.DMA((7,)),
                                          pltpu.SemaphoreType.DMA((7,))])(pk)


ATTN_W = ("w_in", "w_uq", "w_ukv", "w_out")
FFN_W = ("w_gate", "w_up", "w_down")
BIG = ATTN_W + FFN_W


def _cols_from_chips(g):
    return jnp.concatenate([g[t] for t in range(4)], axis=1)


def _cols_to_chips(full):
    r, n = full.shape
    return full.reshape(r, 4, n // 4).transpose(1, 0, 2).reshape(4, 2, r // 2, n // 4)


def _rows_to_chips(full):
    n, c = full.shape
    return full.reshape(4, 2, n // 8, c)


def _permute_w_in(w, nh):
    d = w.shape[0]
    g = 4 * nh * HEAD
    lr = (w.shape[1] - g - 2 * nh - ROPE) // 2
    o = g + 2 * nh
    pad = jnp.zeros((d, LANE - ROPE - 8 - nh), w.dtype)
    pad8 = jnp.zeros((d, 8 - nh), w.dtype)
    return jnp.concatenate([w[:, :g], w[:, o:o + 2 * lr], w[:, o + 2 * lr:], w[:, g:g + nh], pad8,
                            w[:, g + nh:g + 2 * nh], pad, jnp.zeros((d, LANE), w.dtype)], axis=1)


def _unpermute_w_in(wp, nh, lr):
    g = 4 * nh * HEAD
    mc = g + 2 * lr
    return jnp.concatenate([wp[:, :g], wp[:, mc + B_LANE:mc + B_LANE + nh], wp[:, mc + A_LANE:mc + A_LANE + nh],
                            wp[:, g:g + 2 * lr], wp[:, mc:mc + ROPE]], axis=1)


def _permute_w_uq(w, nh):
    lr = w.shape[0]
    w3 = w.reshape(lr, nh, HEAD + ROPE)
    return jnp.concatenate([w3, jnp.zeros((lr, nh, HEAD - ROPE), w.dtype)], axis=2).reshape(lr, nh * 2 * HEAD)


def _unpermute_w_uq(wp, nh):
    lr = wp.shape[0]
    return wp.reshape(lr, nh, 2 * HEAD)[:, :, :HEAD + ROPE].reshape(lr, nh * (HEAD + ROPE))


def _permute_w_ukv(w, nh):
    lr = w.shape[0]
    w3 = w.reshape(lr, nh, 2 * HEAD)
    kp = jnp.concatenate([w3[:, :, :HEAD], jnp.zeros((lr, nh, HEAD), w.dtype)], axis=2)
    return jnp.concatenate([kp.reshape(lr, nh * 2 * HEAD), w3[:, :, HEAD:].reshape(lr, nh * HEAD)], axis=1)


def _unpermute_w_ukv(wp, nh):
    lr = wp.shape[0]
    kp = wp[:, :nh * 2 * HEAD].reshape(lr, nh, 2 * HEAD)[:, :, :HEAD]
    vp = wp[:, nh * 2 * HEAD:].reshape(lr, nh, HEAD)
    return jnp.concatenate([kp, vp], axis=2).reshape(lr, nh * 2 * HEAD)


def _sum_pairs(grads, recv, place, tag):
    sums = [_sum_pair(g, r, place, "sum_pair_%s%d" % (tag, k)) for k, (g, r) in enumerate(zip(grads, recv))]
    return [s[0] for s in sums], [s[1] for s in sums]


def _reduce_end(own, recv, tag):
    return [_sum_chips(o, r, "sum_chips_%s%d" % (tag, k)) for k, (o, r) in enumerate(zip(own, recv))]


def _step(x, pos, tgt, w_in, attn_shards, ffn_shards, small, place):
    nh = small["a_log"].shape[1]
    lr = small["q_norm_w"].shape[1]
    w = nh * HEAD
    z_col, col_q, col_kv = 3, 4 * w // lr, 4 * w // lr + 1
    misc_c = 4 * w + 2 * lr
    misc_col = misc_c // LANE
    assert (4 * w) % lr == 0 and small["kv_norm_w"].shape[1] == lr

    zl = jnp.zeros((1, LANE), F32)
    alog_l = zl.at[:, A_LANE:A_LANE + nh].set(small["a_log"])
    dtb_l = zl.at[:, A_LANE:A_LANE + nh].set(small["dt_bias"])
    conv_w = small["conv_w"]

    h1, (in4,) = _norm_fwd(x, small["attn_norm_w"], "norm1", rider=_Gather([w_in]))
    win_p = _permute_w_in(_cols_from_chips(in4), nh)
    proj, (uq4, ukv4, out4) = _mm([(h1, win_p)], name="proj_in", rider=_Gather(attn_shards))
    wuq_p = _permute_w_uq(_cols_from_chips(uq4), nh)
    wukv_p = _permute_w_ukv(_cols_from_chips(ukv4), nh)
    w_out = out4.reshape(-1, out4.shape[2])
    gq, gk, gv, gb, gbt = _gdn_prep(proj, conv_w, alog_l, dtb_l, nh, misc_col)
    (o_gdn, states), (wg4,) = _gdn_fwd(gq, gk, gv, gb, gbt, nh, _Gather(ffn_shards[:1]))
    cqn, ckvn = _mla_norm(proj, small["q_norm_w"], small["kv_norm_w"], col_q, col_kv)
    qraw = _mm([(cqn, wuq_p)], name="proj_uq", out_dtype=BF16)
    kvraw = _mm([(ckvn, wukv_p)], name="proj_ukv", out_dtype=BF16)
    qc, kc, vv, qt, vt, kt = _mla_rope(qraw, kvraw, proj, pos, nh, misc_col)
    o_mla, lse, (wu4, wd4) = _mla_fwd(qt, kc, vt, nh, _Gather(ffn_shards[1:]))
    w_down = wd4.reshape(-1, wd4.shape[2])
    mixed = _mix_fwd(o_gdn, proj, o_mla, small["gdn_norm_w"], small["mla_out_norm_w"], nh, z_col)
    x2, h2 = _proj_norm(mixed, w_out, x, small["ffn_norm_w"], "proj_out_norm2")
    act, gpre, upre = _swiglu_fwd(h2, wg4, wu4)
    x3 = _mm([(act, w_down)], name="proj_down", res=x2, tk=2816)
    dx3, d_final, loss, dx3h = _final_loss(x3, tgt, small["final_norm_w"])

    gs = {"final_norm_w": d_final}
    dgate, dup = _swiglu_bwd(dx3h, w_down, gpre, upre)
    g_down = _rows_to_chips(_mm([(act, dx3h)], name="dw_down", ta=True, out_dtype=BF16))
    g_gate = _mm([(h2, dgate)], name="dw_gate", ta=True, out_dtype=BF16, out_chips=True)
    g_up = _mm([(h2, dup)], name="dw_up", ta=True, out_dtype=BF16, out_chips=True)
    halves = lambda g: g.reshape(4, 2, g.shape[1] // 2, g.shape[2])
    ffn_g = [halves(g_gate), halves(g_up), g_down]
    dh2, ffn_sib = _mm([(dgate, wg4), (dup, wu4)], name="dh2", tb=True, b_chips=True, out_dtype=BF16,
                       rider=_Swap(ffn_g))
    ffn16, ffn_own = _sum_pairs(ffn_g, ffn_sib, place, "ffn")
    dx2, gs["ffn_norm_w"], dx2h = _norm_bwd(dh2, x2, small["ffn_norm_w"], dx3, "norm2_bwd", True)
    dmix = _mm([(dx2h, w_out)], name="dmix", tb=True, out_dtype=BF16)
    g_out = _rows_to_chips(_mm([(mixed, dx2h)], name="dw_out", ta=True, out_dtype=BF16))
    d_ogdn, dproj, d_omla, gs["gdn_norm_w"], gs["mla_out_norm_w"], delta, d_omla_t = _mix_bwd(
        dmix, o_gdn, proj, o_mla, small["gdn_norm_w"], small["mla_out_norm_w"], nh, z_col)
    dqc, dkc, dvv, ffn_recv = _mla_bwd(qc, qt, kc, kt, vv, d_omla, d_omla_t, lse, delta, nh, _Exchange(ffn16))
    ffn_tot = _reduce_end(ffn_own, ffn_recv, "ffn")
    dqraw, dkvraw, dkr = _mla_rope_bwd(dqc, dkc, dvv, pos, nh)
    dcqn = _mm([(dqraw, wuq_p)], name="dcqn", tb=True)
    dckvn = _mm([(dkvraw, wukv_p)], name="dckvn", tb=True)
    g_uq = _cols_to_chips(_unpermute_w_uq(_mm([(cqn, dqraw)], name="dw_uq", ta=True, out_dtype=BF16), nh))
    g_ukv = _cols_to_chips(_unpermute_w_ukv(_mm([(ckvn, dkvraw)], name="dw_ukv", ta=True, out_dtype=BF16), nh))
    dproj, gs["q_norm_w"], gs["kv_norm_w"] = _mla_norm_bwd(
        proj, small["q_norm_w"], small["kv_norm_w"], dcqn, dckvn, dproj, col_q, col_kv)
    (dgq, dgk, dgv, dgb), ffn_shared = _gdn_bwd(gq, gk, gv, gb, gbt, states, d_ogdn, nh, _Share(ffn_tot))
    dconv, dproj, gs["conv_w"], dal, ddb = _gdn_prep_bwd(
        proj, conv_w, alog_l, dtb_l, dgq, dgk, dgv, dgb, dkr, dproj, nh, misc_col)
    gs["a_log"] = dal[:, A_LANE:A_LANE + nh]
    gs["dt_bias"] = ddb[:, A_LANE:A_LANE + nh]
    dproj = _conv_bwd_input(dconv, conv_w, dproj)
    g_in = _cols_to_chips(_unpermute_w_in(_mm([(h1, dproj)], name="dw_in", ta=True, out_dtype=BF16), nh, lr))
    att_g = [g_in, g_uq, g_ukv, g_out]
    att16, att_own = _sum_pairs(att_g, _comm(_Swap(att_g), "swap_att"), place, "att")
    dh1, att_recv = _mm([(dproj, win_p)], name="dh1", tb=True, out_dtype=BF16, rider=_Exchange(att16))
    att_tot = _reduce_end(att_own, att_recv, "att")
    att_shared = _comm(_Share(att_tot), "share_att")
    grad_x, gs["attn_norm_w"] = _norm_bwd(dh1, x, small["attn_norm_w"], dx2, "norm1_bwd", False)
    return loss, grad_x, att_tot + ffn_tot, list(att_shared) + list(ffn_shared), gs


SMALL = ("attn_norm_w", "ffn_norm_w", "final_norm_w", "q_norm_w", "kv_norm_w", "gdn_norm_w",
         "mla_out_norm_w", "a_log", "dt_bias")
WEIGHTS = ("attn_norm_w", "w_in", "conv_w", "a_log", "dt_bias", "gdn_norm_w", "q_norm_w", "w_uq",
           "kv_norm_w", "w_ukv", "mla_out_norm_w", "w_out", "ffn_norm_w", "w_gate", "w_up", "w_down",
           "final_norm_w")


def _pack_small(vecs):
    flat = jnp.concatenate([v.astype(F32).reshape(-1) for v in vecs])
    pad = (-flat.shape[0]) % (8 * LANE)
    return jnp.concatenate([flat, jnp.zeros((pad,), F32)]).reshape(-1, LANE)


def kernel(x, positions, attn_norm_w, w_in, conv_w, a_log, dt_bias, gdn_norm_w, q_norm_w, w_uq, kv_norm_w, w_ukv, mla_out_norm_w, w_out, ffn_norm_w, w_gate, w_up, w_down, final_norm_w, loss_target, m_attn_norm_w, m_w_in, m_conv_w, m_a_log, m_dt_bias, m_gdn_norm_w, m_q_norm_w, m_w_uq, m_kv_norm_w, m_w_ukv, m_mla_out_norm_w, m_w_out, m_ffn_norm_w, m_w_gate, m_w_up, m_w_down, m_final_norm_w, v_attn_norm_w, v_w_in, v_conv_w, v_a_log, v_dt_bias, v_gdn_norm_w, v_q_norm_w, v_w_uq, v_kv_norm_w, v_w_ukv, v_mla_out_norm_w, v_w_out, v_ffn_norm_w, v_w_gate, v_w_up, v_w_down, v_final_norm_w):
    args = dict(locals())
    xi, yi, ci = lax.axis_index("x"), lax.axis_index("y"), lax.axis_index("c")
    chip = 2 * xi + yi

    def two_d(a):
        return a.reshape(a.shape[-2:]) if a.ndim >= 2 else a.reshape(1, -1)

    wloc = {n: two_d(args[n]) for n in WEIGHTS}
    mloc = {n: two_d(args["m_" + n]) for n in WEIGHTS}
    vloc = {n: two_d(args["v_" + n]) for n in WEIGHTS}

    cw = wloc["conv_w"]
    cshard = cw.shape[1]
    cfull = jnp.zeros((CONV, 4 * cshard), F32)
    cfull = lax.dynamic_update_slice(cfull, jnp.where(ci == 0, cw, 0.0), (0, chip * cshard))
    conv_full = _small_allreduce(_pack_small([cfull]), "gather_conv_w").reshape(-1)[:CONV * 4 * cshard]
    conv_full = conv_full.reshape(CONV, 4 * cshard)

    small = {n: wloc[n] for n in SMALL}
    small["conv_w"] = conv_full

    pos = positions.reshape(-1, 1).astype(F32)
    place = jnp.stack([ci, chip]).astype(jnp.int32)
    loss, grad_x, totals, from_sib, gs = _step(
        two_d(x), pos, two_d(loss_target), wloc["w_in"].astype(BF16), [wloc[n].astype(BF16) for n in ATTN_W[1:]],
        [wloc[n].astype(BF16) for n in FFN_W], small, place)

    small_names = SMALL + ("conv_w",)
    pk = _pack_small([gs[n] for n in small_names] + [loss])
    red = _small_allreduce(pk, "reduce_small").reshape(-1)
    gsm, off = {}, 0
    for n in small_names:
        shp = gs[n].shape
        gsm[n] = red[off:off + shp[0] * shp[1]].reshape(shp)
        off += shp[0] * shp[1]
    loss_out = red[off]
    gsm["conv_w"] = lax.dynamic_slice(gsm["conv_w"], (0, chip * cshard), (CONV, cshard))

    grads, deltas, new_m, new_v = {}, {}, {}, {}
    for n, mine, theirs in zip(BIG, totals, from_sib):
        grads[n], deltas[n], new_m[n], new_v[n] = _adamw_halves(wloc[n], mine, theirs, mloc[n], vloc[n], place,
                                                                "adamw_" + n)
    grads["conv_w"] = gsm["conv_w"]
    deltas["conv_w"], new_m["conv_w"], new_v["conv_w"] = _adamw(wloc["conv_w"], gsm["conv_w"], mloc["conv_w"],
                                                                vloc["conv_w"], "adamw_conv_w")
    sm_shapes = [wloc[n].shape for n in SMALL]
    pd, pm, pv = _adamw(_pack_small([wloc[n] for n in SMALL]), _pack_small([gsm[n] for n in SMALL]),
                        _pack_small([mloc[n] for n in SMALL]), _pack_small([vloc[n] for n in SMALL]),
                        "adamw_small")
    for dst, packed in ((deltas, pd), (new_m, pm), (new_v, pv)):
        flat, off = packed.reshape(-1), 0
        for n, shp in zip(SMALL, sm_shapes):
            dst[n] = flat[off:off + shp[0] * shp[1]].reshape(shp)
            off += shp[0] * shp[1]
    for n in SMALL:
        grads[n] = gsm[n]

    def like(n, a):
        return a.reshape(args[n].shape)

    outs = [loss_out.reshape(()), grad_x.reshape(x.shape)]
    for group in (grads, deltas, new_m, new_v):
        outs += [like(n, group[n]) for n in WEIGHTS]
    return tuple(outs)
```

```python
import jax
import jax.numpy as jnp
from jax import lax
from jax.experimental import pallas as pl
from jax.experimental.pallas import tpu as pltpu

F32, BF16 = jnp.float32, jnp.bfloat16
SDS = jax.ShapeDtypeStruct
MESH = pl.DeviceIdType.MESH

HEAD = 128
ROPE = 64
CHUNK = 64
PAIR = 2 * CHUNK
CONV = 4
EPS = 1e-6
ROPE_THETA = 10000.0
LANE = 128
B_LANE = 64
A_LANE = 72
VMEM_LIMIT = 48 * 1024 * 1024
VMEM_LIMIT_WIDE = 56 * 1024 * 1024
MLA_BLOCK = 512
LOG2E = 1.4426950408889634
LN2 = 0.6931471805599453
SM_SCALE = (HEAD + ROPE) ** -0.5

ADAM_LR = 0.001
ADAM_B1 = 0.9
ADAM_B2 = 0.999
ADAM_EPS = 1e-08
ADAM_WD = 0.01
ADAM_STEP = 10


def _tile(n, pref, mult=LANE):
    if n <= pref:
        return n
    t = (pref // mult) * mult
    while t >= mult:
        if n % t == 0:
            return t
        t -= mult
    return n


def _pcall(body, *, name, grid, in_specs, out_specs, out_shape, scratch=(), vmem=VMEM_LIMIT, aliases=None):
    return pl.pallas_call(
        body, name=name, grid=grid, in_specs=in_specs, out_specs=out_specs,
        out_shape=out_shape, scratch_shapes=list(scratch), input_output_aliases=aliases or {},
        compiler_params=pltpu.CompilerParams(
            dimension_semantics=("arbitrary",) * len(grid), vmem_limit_bytes=vmem))


def _pcall_riding(core, rider, *, name, grid, in_specs, out_specs, out_shape, args, scratch=()):
    n_in, n_out, n_scr = len(in_specs), len(out_specs), len(scratch)
    r_in, r_out = len(rider.arrays), len(rider.out_shapes)

    def body(*refs):
        ins, refs = refs[:n_in], refs[n_in:]
        r_ins, refs = refs[:r_in], refs[r_in:]
        outs, refs = refs[:n_out], refs[n_out:]
        r_outs, refs = refs[:r_out], refs[r_out:]
        scr, sems = refs[:n_scr], refs[n_scr:]
        r_refs = (r_ins, r_outs, sems)
        _ride_begin(rider, r_refs, pl.program_id(0))
        core(*ins, *outs, *scr)
        _ride_end(rider, r_refs, pl.program_id(0), grid[0])

    res = _pcall(body, name=name, grid=grid, in_specs=list(in_specs) + [_ANY] * r_in,
                 out_specs=list(out_specs) + [_ANY] * r_out, out_shape=list(out_shape) + rider.out_shapes,
                 scratch=list(scratch) + rider.scratch)(*args, *rider.arrays)
    return res[:n_out], res[n_out:]


def _rows(ts, width, col=0):
    return pl.BlockSpec((ts, width), lambda i: (i, col))


def _full(shape):
    nd = len(shape)
    return pl.BlockSpec(shape, lambda i: (0,) * nd)


def _dot(a, b):
    return jnp.dot(a.astype(BF16), b.astype(BF16), preferred_element_type=F32)


def _dot_nt(a, b):
    return lax.dot_general(a.astype(BF16), b.astype(BF16), (((1,), (1,)), ((), ())),
                           preferred_element_type=F32)


def _dot_tn(a, b):
    return lax.dot_general(a.astype(BF16), b.astype(BF16), (((0,), (0,)), ((), ())),
                           preferred_element_type=F32)


def _sigmoid(x):
    return 1.0 / (1.0 + jnp.exp(-x))


def _silu(x):
    return x * _sigmoid(x)


def _dsilu(x):
    s = _sigmoid(x)
    return s * (1.0 + x * (1.0 - s))


def _lane_iota(shape):
    return lax.broadcasted_iota(jnp.int32, shape, len(shape) - 1)


def _col(block, idx):
    return jnp.sum(jnp.where(_lane_iota(block.shape) == idx, block, 0.0), axis=-1, keepdims=True)


def _mm(pairs, *, name, ta=False, tb=False, out_dtype=F32, res=None, tm=1024, tn=1024, tk=2048,
        b_chips=False, out_chips=False, rider=None):
    a0, b0 = pairs[0]
    if ta:
        kdim, m = a0.shape
    else:
        m, kdim = a0.shape
    if b_chips and tb:
        n, tk = b0.shape[1], b0.shape[2]
        assert kdim == 4 * tk
    elif b_chips:
        n, tn = 4 * b0.shape[2], b0.shape[2]
        assert kdim == b0.shape[1]
    else:
        n = b0.shape[0] if tb else b0.shape[1]
    if out_chips:
        tn = n // 4
    tm = _tile(m, tm)
    tn = tn if (out_chips or (b_chips and not tb)) else _tile(n, tn)
    tk = tk if (b_chips and tb) else _tile(kdim, tk)
    assert m % tm == 0 and n % tn == 0 and kdim % tk == 0
    nk, npair = kdim // tk, len(pairs)
    grid = (m // tm, n // tn, nk)
    dims = (((0 if ta else 1,), (1 if tb else 0,)), ((), ()))
    n_in = 2 * npair + (res is not None)
    r_in, r_out = (len(rider.arrays), len(rider.out_shapes)) if rider else (0, 0)

    def body(*refs):
        o_ref = refs[n_in + r_in]
        acc = refs[n_in + r_in + 1 + r_out]
        k = pl.program_id(2)
        if rider:
            r_refs = (refs[n_in:n_in + r_in], refs[n_in + r_in + 1:n_in + r_in + 1 + r_out],
                      refs[n_in + r_in + 2 + r_out:])
            step = (pl.program_id(0) * grid[1] + pl.program_id(1)) * nk + k
            _ride_begin(rider, r_refs, step)

        @pl.when(k == 0)
        def _():
            acc[...] = jnp.zeros_like(acc)

        tot = None
        for p in range(npair):
            d = lax.dot_general(refs[2 * p][...].astype(BF16), refs[2 * p + 1][...].astype(BF16),
                                dims, preferred_element_type=F32)
            tot = d if tot is None else tot + d
        acc[...] += tot

        @pl.when(k == nk - 1)
        def _():
            r = acc[...]
            if res is not None:
                r = r + refs[2 * npair][...]
            o_ref[...] = r.astype(out_dtype)

        if rider:
            _ride_end(rider, r_refs, step, grid[0] * grid[1] * nk)

    if ta:
        a_spec = pl.BlockSpec((tk, tm), lambda i, j, k: (k, i))
    else:
        a_spec = pl.BlockSpec((tm, tk), lambda i, j, k: (i, k))
    if b_chips and tb:
        b_spec = pl.BlockSpec((None, tn, tk), lambda i, j, k: (k, j, 0))
    elif b_chips:
        b_spec = pl.BlockSpec((None, tk, tn), lambda i, j, k: (j, k, 0))
    elif tb:
        b_spec = pl.BlockSpec((tn, tk), lambda i, j, k: (j, k))
    else:
        b_spec = pl.BlockSpec((tk, tn), lambda i, j, k: (k, j))
    if out_chips:
        o_spec = pl.BlockSpec((None, tm, tn), lambda i, j, k: (j, i, 0))
        o_shape = SDS((4, m, tn), out_dtype)
    else:
        o_spec = pl.BlockSpec((tm, tn), lambda i, j, k: (i, j))
        o_shape = SDS((m, n), out_dtype)
    in_specs, args = [], []
    for a, b in pairs:
        in_specs += [a_spec, b_spec]
        args += [a, b]
    if res is not None:
        in_specs.append(o_spec)
        args.append(res)
    out_specs, out_shapes, scratch = [o_spec], [o_shape], [pltpu.VMEM((tm, tn), F32)]
    if rider:
        in_specs += [_ANY] * r_in
        args += rider.arrays
        out_specs += [_ANY] * r_out
        out_shapes += rider.out_shapes
        scratch += rider.scratch
    outs = _pcall(body, name=name, grid=grid, in_specs=in_specs, out_specs=out_specs, out_shape=out_shapes,
                  scratch=scratch)(*args)
    return (outs[0], outs[1:]) if rider else outs[0]


def _norm_fwd(x, w, name, rider=None):
    s, d = x.shape
    ts = _tile(s, 512, 8)

    def body(x_ref, w_ref, h_ref):
        xv = x_ref[...]
        r = lax.rsqrt(jnp.mean(xv * xv, axis=-1, keepdims=True) + EPS)
        h_ref[...] = (xv * r * w_ref[...]).astype(BF16)

    spec = dict(name=name, grid=(s // ts,), in_specs=[_rows(ts, d), _full((1, d))])
    if rider is None:
        return _pcall(body, out_specs=_rows(ts, d), out_shape=SDS((s, d), BF16), **spec)(x, w)
    outs, r_outs = _pcall_riding(body, rider, out_specs=[_rows(ts, d)], out_shape=[SDS((s, d), BF16)],
                                 args=(x, w), **spec)
    return outs[0], r_outs


def _proj_norm(a, b, res, w, name):
    m, kdim = a.shape
    n = b.shape[1]
    tm = _tile(m, 512, 8)

    def body(a_ref, b_ref, r_ref, w_ref, x_ref, h_ref):
        xv = r_ref[...] + _dot(a_ref[...], b_ref[...])
        x_ref[...] = xv
        r = lax.rsqrt(jnp.mean(xv * xv, axis=-1, keepdims=True) + EPS)
        h_ref[...] = (xv * r * w_ref[...]).astype(BF16)

    return _pcall(body, name=name, grid=(m // tm,),
                  in_specs=[_rows(tm, kdim), pl.BlockSpec((kdim, n), lambda i: (0, 0), pipeline_mode=pl.Buffered(1)),
                            _rows(tm, n), _full((1, n))],
                  out_specs=[_rows(tm, n), _rows(tm, n)],
                  out_shape=[SDS((m, n), F32), SDS((m, n), BF16)])(a, b, res, w)


def _norm_bwd(dh, x, w, dres, name, with_bf16):
    s, d = x.shape
    ts = _tile(s, 256, 8)

    def body(dh_ref, x_ref, w_ref, dres_ref, dx_ref, dw_ref, *dx16_ref):
        @pl.when(pl.program_id(0) == 0)
        def _():
            dw_ref[...] = jnp.zeros_like(dw_ref)

        xv, dhv = x_ref[...], dh_ref[...]
        r = lax.rsqrt(jnp.mean(xv * xv, axis=-1, keepdims=True) + EPS)
        xh = xv * r
        dw_ref[...] += jnp.sum(dhv * xh, axis=0, keepdims=True)
        dxh = dhv * w_ref[...]
        dx = dres_ref[...] + r * (dxh - xh * jnp.mean(dxh * xh, axis=-1, keepdims=True))
        dx_ref[...] = dx
        for ref in dx16_ref:
            ref[...] = dx.astype(BF16)

    extra = 1 if with_bf16 else 0
    return _pcall(body, name=name, grid=(s // ts,),
                  in_specs=[_rows(ts, d), _rows(ts, d), _full((1, d)), _rows(ts, d)],
                  out_specs=[_rows(ts, d), _full((1, d))] + [_rows(ts, d)] * extra,
                  out_shape=[SDS((s, d), F32), SDS((1, d), F32)] + [SDS((s, d), BF16)] * extra)(
                      dh, x, w, dres)


def _final_loss(x3, tgt, w):
    s, d = x3.shape
    ts = _tile(s, 256, 8)

    def body(x_ref, t_ref, w_ref, dx_ref, dw_ref, loss_ref, dx16_ref):
        @pl.when(pl.program_id(0) == 0)
        def _():
            dw_ref[...] = jnp.zeros_like(dw_ref)
            loss_ref[...] = jnp.zeros_like(loss_ref)

        xv, wv = x_ref[...], w_ref[...]
        r = lax.rsqrt(jnp.mean(xv * xv, axis=-1, keepdims=True) + EPS)
        xh = xv * r
        err = xh * wv - t_ref[...]
        row = jnp.mean(err * err, axis=-1, keepdims=True)
        loss_ref[...] += 0.5 * jnp.sum(row, axis=0, keepdims=True)
        dy = err * (1.0 / d)
        dw_ref[...] += jnp.sum(dy * xh, axis=0, keepdims=True)
        dxh = dy * wv
        dx = r * (dxh - xh * jnp.mean(dxh * xh, axis=-1, keepdims=True))
        dx_ref[...] = dx
        dx16_ref[...] = dx.astype(BF16)

    return _pcall(body, name="final_loss", grid=(s // ts,),
                  in_specs=[_rows(ts, d), _rows(ts, d), _full((1, d))],
                  out_specs=[_rows(ts, d), _full((1, d)), _full((1, 1)), _rows(ts, d)],
                  out_shape=[SDS((s, d), F32), SDS((1, d), F32), SDS((1, 1), F32), SDS((s, d), BF16)])(
                      x3, tgt, w)


def _shift_down(cur, halo, s):
    if s == 0:
        return cur
    row8 = lax.broadcasted_iota(jnp.int32, halo.shape, 0)
    r = pltpu.roll(cur, s, 0)
    top = jnp.where(row8 < s, pltpu.roll(halo, s, 0), r[0:8])
    return jnp.concatenate([top, r[8:]], axis=0)


def _shift_up(cur, halo, s):
    if s == 0:
        return cur
    ts = cur.shape[0]
    row8 = lax.broadcasted_iota(jnp.int32, halo.shape, 0)
    r = pltpu.roll(cur, ts - s, 0)
    bot = jnp.where(row8 >= 8 - s, pltpu.roll(halo, 8 - s, 0), r[ts - 8:ts])
    return jnp.concatenate([r[:ts - 8], bot], axis=0)


def _chunk_tri(ts, upper):
    i = lax.broadcasted_iota(jnp.int32, (ts, ts), 0)
    j = lax.broadcasted_iota(jnp.int32, (ts, ts), 1)
    same = jnp.right_shift(i, 6) == jnp.right_shift(j, 6)
    return jnp.where(same & ((j >= i) if upper else (j <= i)), 1.0, 0.0).astype(F32)


def _gate_values(m, alog, dtb):
    lane = _lane_iota(m.shape)
    beta = _sigmoid(m)
    xg = m + dtb
    sp = jnp.maximum(xg, 0.0) + jnp.log(1.0 + jnp.exp(-jnp.abs(xg)))
    ga = (lane >= A_LANE) & (lane < A_LANE + 8)
    g = jnp.where(ga, -jnp.exp(alog) * sp, 0.0)
    return beta, g, xg, ga


def _l2_heads(a, nh, scale):
    outs, rs = [], []
    for h in range(nh):
        ah = a[:, HEAD * h:HEAD * (h + 1)]
        r = lax.rsqrt(jnp.sum(ah * ah, axis=-1, keepdims=True) + EPS)
        outs.append(ah * (r * scale))
        rs.append(r)
    return jnp.concatenate(outs, axis=-1), rs


def _gdn_prep(proj, conv_w, alog_l, dtb_l, nh, misc_col):
    s = proj.shape[0]
    w = nh * HEAD
    ts = _tile(s, 256, PAIR)
    hb = ts // 8

    def body(cur_ref, halo_ref, misc_ref, cw_ref, al_ref, db_ref, q_ref, k_ref, v_ref, gb_ref, gbt_ref):
        first = pl.program_id(0) == 0
        outs = (q_ref, k_ref, v_ref)
        for sec in range(3):
            cs = slice(sec * w, (sec + 1) * w)
            cur = cur_ref[:, cs]
            halo = jnp.where(first, 0.0, halo_ref[:, cs])
            pre = None
            for j in range(CONV):
                term = cw_ref[j:j + 1, cs] * _shift_down(cur, halo, CONV - 1 - j)
                pre = term if pre is None else pre + term
            act = _silu(pre)
            if sec == 0:
                act, _ = _l2_heads(act, nh, HEAD ** -0.5)
            elif sec == 1:
                act, _ = _l2_heads(act, nh, 1.0)
            outs[sec][...] = act
        m = misc_ref[...]
        lane = _lane_iota(m.shape)
        beta, g, _, ga = _gate_values(m, al_ref[...], db_ref[...])
        gcc = jnp.dot(_chunk_tri(ts, False), g, precision=lax.Precision.HIGHEST,
                      preferred_element_type=F32)
        gb = jnp.where((lane >= B_LANE) & (lane < B_LANE + 8), beta, jnp.where(ga, gcc, 0.0))
        gb_ref[...] = gb
        gbt_ref[...] = gb.T

    return _pcall(
        body, name="gdn_prep", grid=(s // ts,),
        in_specs=[_rows(ts, 3 * w),
                  pl.BlockSpec((8, 3 * w), lambda i: (jnp.maximum(i * hb - 1, 0), 0)),
                  _rows(ts, LANE, misc_col), _full((CONV, 3 * w)), _full((1, LANE)), _full((1, LANE))],
        out_specs=[_rows(ts, w), _rows(ts, w), _rows(ts, w), _rows(ts, LANE),
                   pl.BlockSpec((LANE, ts), lambda i: (0, i))],
        out_shape=[SDS((s, w), F32), SDS((s, w), F32), SDS((s, w), F32), SDS((s, LANE), F32),
                   SDS((LANE, s), F32)])(proj, proj, proj, conv_w, alog_l, dtb_l)


def _gdn_prep_bwd(proj, conv_w, alog_l, dtb_l, dq, dk, dv, dgb, dkr, dproj, nh, misc_col):
    s = proj.shape[0]
    w = nh * HEAD
    ts = _tile(s, 256, PAIR)
    hb = ts // 8
    assert misc_col % 2 == 0

    def body(cur_ref, halo_ref, misc_ref, cw_ref, al_ref, db_ref, dq_ref, dk_ref, dv_ref, dgb_ref,
             dkr_ref, _, dc_ref, dm_ref, dcw_ref, dal_ref, ddb_ref):
        first = pl.program_id(0) == 0

        @pl.when(first)
        def _():
            dcw_ref[...] = jnp.zeros_like(dcw_ref)
            dal_ref[...] = jnp.zeros_like(dal_ref)
            ddb_ref[...] = jnp.zeros_like(ddb_ref)

        dins = (dq_ref, dk_ref, dv_ref)
        for sec in range(3):
            cs = slice(sec * w, (sec + 1) * w)
            cur = cur_ref[:, cs]
            halo = jnp.where(first, 0.0, halo_ref[:, cs])
            us = [_shift_down(cur, halo, CONV - 1 - j) for j in range(CONV)]
            pre = None
            for j in range(CONV):
                term = cw_ref[j:j + 1, cs] * us[j]
                pre = term if pre is None else pre + term
            act = _silu(pre)
            dout = dins[sec][...]
            if sec < 2:
                scale = HEAD ** -0.5 if sec == 0 else 1.0
                parts = []
                for h in range(nh):
                    hs = slice(HEAD * h, HEAD * (h + 1))
                    ah = act[:, hs]
                    r = lax.rsqrt(jnp.sum(ah * ah, axis=-1, keepdims=True) + EPS)
                    ahat = ah * r
                    dy = dout[:, hs]
                    parts.append((scale * r) * (dy - ahat * jnp.sum(dy * ahat, axis=-1, keepdims=True)))
                dact = jnp.concatenate(parts, axis=-1)
            else:
                dact = dout
            dconv = dact * _dsilu(pre)
            dc_ref[:, cs] = dconv
            for j in range(CONV):
                dcw_ref[j:j + 1, cs] += jnp.sum(dconv * us[j], axis=0, keepdims=True)
        m = misc_ref[...]
        lane = _lane_iota(m.shape)
        al = al_ref[...]
        beta, g, xg, ga = _gate_values(m, al, db_ref[...])
        dgbv = dgb_ref[...]
        dg = jnp.dot(_chunk_tri(ts, True), jnp.where(ga, dgbv, 0.0), precision=lax.Precision.HIGHEST,
                     preferred_element_type=F32)
        da_raw = jnp.where(ga, dg * (-jnp.exp(al)) * _sigmoid(xg), 0.0)
        db_raw = jnp.where((lane >= B_LANE) & (lane < B_LANE + 8), dgbv * beta * (1.0 - beta), 0.0)
        dal_ref[...] += jnp.sum(dg * g, axis=0, keepdims=True)
        ddb_ref[...] += jnp.sum(da_raw, axis=0, keepdims=True)
        dm_ref[:, :LANE] = (dkr_ref[...] + da_raw + db_raw).astype(BF16)
        dm_ref[:, LANE:] = jnp.zeros((ts, LANE), BF16)

    return _pcall(
        body, name="gdn_prep_bwd", grid=(s // ts,),
        in_specs=[_rows(ts, 3 * w),
                  pl.BlockSpec((8, 3 * w), lambda i: (jnp.maximum(i * hb - 1, 0), 0)),
                  _rows(ts, LANE, misc_col), _full((CONV, 3 * w)), _full((1, LANE)), _full((1, LANE)),
                  _rows(ts, w), _rows(ts, w), _rows(ts, w), _rows(ts, LANE), _rows(ts, LANE), _ANY],
        out_specs=[_rows(ts, 3 * w), _rows(ts, 2 * LANE, misc_col // 2), _full((CONV, 3 * w)), _full((1, LANE)),
                   _full((1, LANE))],
        out_shape=[SDS((s, 3 * w), F32), SDS(dproj.shape, BF16), SDS((CONV, 3 * w), F32),
                   SDS((1, LANE), F32), SDS((1, LANE), F32)], aliases={11: 1})(
                       proj, proj, proj, conv_w, alog_l, dtb_l, dq, dk, dv, dgb, dkr, dproj)


def _conv_bwd_input(dconv, conv_w, dproj):
    s, c = dconv.shape
    ts = _tile(s, 256, 8)
    hb = ts // 8
    nblk8 = s // 8
    nt = s // ts

    def body(cur_ref, nxt_ref, cw_ref, _, o_ref):
        last = pl.program_id(0) == nt - 1
        cur = cur_ref[...]
        halo = jnp.where(last, 0.0, nxt_ref[...])
        acc = None
        for j in range(CONV):
            term = cw_ref[j:j + 1, :] * _shift_up(cur, halo, CONV - 1 - j)
            acc = term if acc is None else acc + term
        o_ref[...] = acc.astype(BF16)

    return _pcall(
        body, name="conv_bwd_input", grid=(nt,),
        in_specs=[_rows(ts, c),
                  pl.BlockSpec((8, c), lambda i: (jnp.minimum((i + 1) * hb, nblk8 - 1), 0)),
                  _full((CONV, c)), _ANY],
        out_specs=_rows(ts, c), out_shape=SDS(dproj.shape, BF16), aliases={3: 0})(
            dconv, dconv, conv_w, dproj)


def _inv_unit_lower(a):
    n = a[0].shape[0]
    i = lax.broadcasted_iota(jnp.int32, (n, n), 0)
    j = lax.broadcasted_iota(jnp.int32, (n, n), 1)
    eye = jnp.where(i == j, 1.0, 0.0)
    t = [eye - ah for ah in a]
    x = a
    for _ in range(5):
        x = [_dot(xh, xh) for xh in x]
        t = [th + _dot(th, xh) for th, xh in zip(t, x)]
    return t


def _pair_common(q, k, gcol, grow, bcol):
    i = lax.broadcasted_iota(jnp.int32, (PAIR, PAIR), 0)
    j = lax.broadcasted_iota(jnp.int32, (PAIR, PAIR), 1)
    same = jnp.right_shift(i, 6) == jnp.right_shift(j, 6)
    tril = same & (i >= j)
    strict = same & (i > j)
    dec = [jnp.where(tril, jnp.exp(jnp.minimum(gc - gr, 0.0)), 0.0) for gc, gr in zip(gcol, grow)]
    kk = [_dot_nt(kh, kh) for kh in k]
    qk = [_dot_nt(qh, kh) for qh, kh in zip(q, k)]
    a = [jnp.where(strict, b * kkh * d, 0.0) for b, kkh, d in zip(bcol, kk, dec)]
    t = _inv_unit_lower(a)
    p = [qkh * d for qkh, d in zip(qk, dec)]
    return dec, kk, a, t, p, tril, strict


def _ext(v, a):
    z = jnp.zeros_like(v)
    return jnp.concatenate([v, z] if a == 0 else [z, v], axis=0)


def _gdn_fwd(q, k, v, gb, gbt, nh, rider):
    s = q.shape[0]
    w = nh * HEAD
    npair = s // PAIR

    def body(q_ref, k_ref, v_ref, gb_ref, gbt_ref, o_ref, st_ref, s_ref):
        @pl.when(pl.program_id(0) == 0)
        def _():
            s_ref[...] = jnp.zeros_like(s_ref)

        heads = range(nh)
        hs = [slice(HEAD * h, HEAD * (h + 1)) for h in heads]
        gbv = gb_ref[...]
        q, k, v = [q_ref[:, s_] for s_ in hs], [k_ref[:, s_] for s_ in hs], [v_ref[:, s_] for s_ in hs]
        gcol = [_col(gbv, A_LANE + h) for h in heads]
        bcol = [_col(gbv, B_LANE + h) for h in heads]
        grow = [gbt_ref[A_LANE + h:A_LANE + h + 1, :] for h in heads]
        _, _, _, t, p, _, _ = _pair_common(q, k, gcol, grow, bcol)
        eg = [jnp.exp(gc) for gc in gcol]
        qg = [x * e for x, e in zip(q, eg)]
        kg = [x * e for x, e in zip(k, eg)]
        outs = []
        for a in range(2):
            sl = slice(CHUNK * a, CHUNK * (a + 1))
            st = [s_ref[h] for h in heads]
            for h in heads:
                st_ref[a, h] = st[h]
            r = [v[h][sl] - _dot(kg[h][sl], st[h]) for h in heads]
            vn = [_dot(t[h][sl], _ext(bcol[h][sl] * r[h], a)) for h in heads]
            outs.append([_dot(qg[h][sl], st[h]) + _dot(p[h][sl], _ext(vn[h], a)) for h in heads])
            gl = [_col(gr, CHUNK * (a + 1) - 1) for gr in grow]
            kd = [k[h][sl] * jnp.exp(gl[h] - gcol[h][sl]) for h in heads]
            upd = [_dot_tn(kd[h], vn[h]) for h in heads]
            for h in heads:
                s_ref[h] = jnp.exp(gl[h]) * st[h] + upd[h]
        for h in heads:
            o_ref[:, hs[h]] = jnp.concatenate([outs[0][h], outs[1][h]], axis=0)

    return _pcall_riding(
        body, rider, name="gdn_fwd", grid=(npair,),
        in_specs=[_rows(PAIR, w), _rows(PAIR, w), _rows(PAIR, w), _rows(PAIR, LANE),
                  pl.BlockSpec((LANE, PAIR), lambda i: (0, i))],
        out_specs=[_rows(PAIR, w), pl.BlockSpec((2, nh, HEAD, HEAD), lambda i: (i, 0, 0, 0))],
        out_shape=[SDS((s, w), F32), SDS((2 * npair, nh, HEAD, HEAD), F32)],
        scratch=[pltpu.VMEM((nh, HEAD, HEAD), F32)], args=(q, k, v, gb, gbt))


def _gdn_bwd(q, k, v, gb, gbt, states, do, nh, rider):
    s = q.shape[0]
    w = nh * HEAD
    npair = s // PAIR
    rev = lambda i: (npair - 1 - i, 0)

    def body(q_ref, k_ref, v_ref, gb_ref, gbt_ref, st_ref, do_ref, dq_ref, dk_ref, dv_ref, dgb_ref,
             ds_ref):
        @pl.when(pl.program_id(0) == 0)
        def _():
            ds_ref[...] = jnp.zeros_like(ds_ref)

        lane = _lane_iota((PAIR, LANE))
        row = lax.broadcasted_iota(jnp.int32, (CHUNK, 1), 0)
        heads = range(nh)
        hs = [slice(HEAD * h, HEAD * (h + 1)) for h in heads]
        gbv = gb_ref[...]
        q, k, v = [q_ref[:, s_] for s_ in hs], [k_ref[:, s_] for s_ in hs], [v_ref[:, s_] for s_ in hs]
        do = [do_ref[:, s_] for s_ in hs]
        gcol = [_col(gbv, A_LANE + h) for h in heads]
        bcol = [_col(gbv, B_LANE + h) for h in heads]
        grow = [gbt_ref[A_LANE + h:A_LANE + h + 1, :] for h in heads]
        dec, kk, amat, t, p, tril, strict = _pair_common(q, k, gcol, grow, bcol)
        tt, pt = [x.T for x in t], [x.T for x in p]
        eg = [jnp.exp(gc) for gc in gcol]
        qg = [x * e for x, e in zip(q, eg)]
        kg = [x * e for x, e in zip(k, eg)]
        sums = lambda x: jnp.sum(x, axis=-1, keepdims=True)
        rs, vns = [None, None], [None, None]
        for a in range(2):
            sl = slice(CHUNK * a, CHUNK * (a + 1))
            rs[a] = [v[h][sl] - _dot(kg[h][sl], st_ref[a, h]) for h in heads]
            vns[a] = [_dot(t[h][sl], _ext(bcol[h][sl] * rs[a][h], a)) for h in heads]
        dsn = [ds_ref[h] for h in heads]
        dqs, dks, dvs, dgcs, dbs, drbs = ([None, None] for _ in range(6))
        for a in (1, 0):
            sl = slice(CHUNK * a, CHUNK * (a + 1))
            st = [st_ref[a, h] for h in heads]
            gl = [_col(gr, CHUNK * (a + 1) - 1) for gr in grow]
            egl = [jnp.exp(x) for x in gl]
            dk_dec = [jnp.exp(gl[h] - gcol[h][sl]) for h in heads]
            kd = [k[h][sl] * dk_dec[h] for h in heads]
            d_vn = [_dot(pt[h][sl], _ext(do[h][sl], a)) + _dot(kd[h], dsn[h]) for h in heads]
            d_qg = [_dot_nt(do[h][sl], st[h]) for h in heads]
            d_rb = [_dot(tt[h][sl], _ext(d_vn[h], a)) for h in heads]
            d_r = [bcol[h][sl] * d_rb[h] for h in heads]
            d_kg = [-_dot_nt(d_r[h], st[h]) for h in heads]
            d_kd = [_dot_nt(vns[a][h], dsn[h]) for h in heads]
            dsn_new = [_dot_tn(qg[h][sl], do[h][sl]) - _dot_tn(kg[h][sl], d_r[h]) for h in heads]
            dbs[a] = [sums(d_rb[h] * rs[a][h]) for h in heads]
            dgl = [egl[h] * jnp.sum(dsn[h] * st[h], keepdims=True) + jnp.sum(d_kd[h] * kd[h], keepdims=True)
                   for h in heads]
            dgcs[a] = [sums(d_qg[h] * qg[h][sl]) + sums(d_kg[h] * kg[h][sl]) - sums(d_kd[h] * kd[h])
                       + jnp.where(row == CHUNK - 1, dgl[h], 0.0) for h in heads]
            dqs[a] = [d_qg[h] * eg[h][sl] for h in heads]
            dks[a] = [d_kg[h] * eg[h][sl] + d_kd[h] * dk_dec[h] for h in heads]
            dvs[a] = d_r
            drbs[a] = d_rb
            dsn = [dsn_new[h] + egl[h] * dsn[h] for h in heads]
        for h in heads:
            ds_ref[h] = dsn[h]
        cat = lambda xs, h: jnp.concatenate([xs[0][h], xs[1][h]], axis=0)
        vn = [cat(vns, h) for h in heads]
        d_rb = [cat(drbs, h) for h in heads]
        dp = [jnp.where(tril, _dot_nt(do[h], vn[h]), 0.0) for h in heads]
        dam = [jnp.where(strict, -_dot_nt(d_rb[h], vn[h]), 0.0) for h in heads]
        g_p = [dp[h] * dec[h] for h in heads]
        g_a = [dam[h] * dec[h] for h in heads]
        gbk = [bcol[h] * g_a[h] for h in heads]
        dq2 = [_dot(g_p[h], k[h]) for h in heads]
        dk2 = [_dot_tn(g_p[h], q[h]) + _dot(gbk[h], k[h]) + _dot_tn(gbk[h], k[h]) for h in heads]
        dgb = jnp.zeros((PAIR, LANE), F32)
        for h in heads:
            dq_ref[:, hs[h]] = cat(dqs, h) + dq2[h]
            dk_ref[:, hs[h]] = cat(dks, h) + dk2[h]
            dv_ref[:, hs[h]] = cat(dvs, h)
            dbeta = cat(dbs, h) + sums(g_a[h] * kk[h])
            mm = dp[h] * p[h] + dam[h] * amat[h]
            dgc = cat(dgcs, h) + sums(mm) - sums(mm.T)
            dgb = dgb + jnp.where(lane == A_LANE + h, dgc, 0.0) + jnp.where(lane == B_LANE + h, dbeta, 0.0)
        dgb_ref[...] = dgb

    return _pcall_riding(
        body, rider, name="gdn_bwd", grid=(npair,),
        in_specs=[pl.BlockSpec((PAIR, w), rev), pl.BlockSpec((PAIR, w), rev), pl.BlockSpec((PAIR, w), rev),
                  pl.BlockSpec((PAIR, LANE), rev),
                  pl.BlockSpec((LANE, PAIR), lambda i: (0, npair - 1 - i)),
                  pl.BlockSpec((2, nh, HEAD, HEAD), lambda i: (npair - 1 - i, 0, 0, 0)),
                  pl.BlockSpec((PAIR, w), rev)],
        out_specs=[pl.BlockSpec((PAIR, w), rev), pl.BlockSpec((PAIR, w), rev), pl.BlockSpec((PAIR, w), rev),
                   pl.BlockSpec((PAIR, LANE), rev)],
        out_shape=[SDS((s, w), F32), SDS((s, w), F32), SDS((s, w), F32), SDS((s, LANE), F32)],
        scratch=[pltpu.VMEM((nh, HEAD, HEAD), F32)], args=(q, k, v, gb, gbt, states, do))


def _mla_norm(proj, qw, kvw, col_q, col_kv):
    s = proj.shape[0]
    lr = qw.shape[1]
    ts = _tile(s, 512, 8)

    def body(cq_ref, ckv_ref, qw_ref, kvw_ref, oq_ref, okv_ref):
        for x_ref, w_ref, o_ref in ((cq_ref, qw_ref, oq_ref), (ckv_ref, kvw_ref, okv_ref)):
            xv = x_ref[...]
            r = lax.rsqrt(jnp.mean(xv * xv, axis=-1, keepdims=True) + EPS)
            o_ref[...] = (xv * r * w_ref[...]).astype(BF16)

    return _pcall(body, name="mla_norm", grid=(s // ts,),
                  in_specs=[_rows(ts, lr, col_q), _rows(ts, lr, col_kv), _full((1, lr)), _full((1, lr))],
                  out_specs=[_rows(ts, lr), _rows(ts, lr)],
                  out_shape=[SDS((s, lr), BF16), SDS((s, lr), BF16)])(proj, proj, qw, kvw)


def _mla_norm_bwd(proj, qw, kvw, dq, dkv, dproj, col_q, col_kv):
    s = proj.shape[0]
    lr = qw.shape[1]
    ts = _tile(s, 512, 8)

    assert col_kv == col_q + 1 and col_q % 2 == 0

    def body(cq_ref, ckv_ref, qw_ref, kvw_ref, dq_ref, dkv_ref, _, o_ref, dqw_ref, dkvw_ref):
        @pl.when(pl.program_id(0) == 0)
        def _():
            dqw_ref[...] = jnp.zeros_like(dqw_ref)
            dkvw_ref[...] = jnp.zeros_like(dkvw_ref)

        for k, (x_ref, w_ref, d_ref, dw_ref) in enumerate(((cq_ref, qw_ref, dq_ref, dqw_ref),
                                                           (ckv_ref, kvw_ref, dkv_ref, dkvw_ref))):
            xv, dh = x_ref[...], d_ref[...]
            r = lax.rsqrt(jnp.mean(xv * xv, axis=-1, keepdims=True) + EPS)
            xh = xv * r
            dw_ref[...] += jnp.sum(dh * xh, axis=0, keepdims=True)
            dxh = dh * w_ref[...]
            o_ref[:, lr * k:lr * (k + 1)] = (
                r * (dxh - xh * jnp.mean(dxh * xh, axis=-1, keepdims=True))).astype(BF16)

    return _pcall(body, name="mla_norm_bwd", grid=(s // ts,),
                  in_specs=[_rows(ts, lr, col_q), _rows(ts, lr, col_kv), _full((1, lr)), _full((1, lr)),
                            _rows(ts, lr), _rows(ts, lr), _ANY],
                  out_specs=[_rows(ts, 2 * lr, col_q // 2), _full((1, lr)), _full((1, lr))],
                  out_shape=[SDS(dproj.shape, BF16), SDS((1, lr), F32), SDS((1, lr), F32)],
                  aliases={6: 0})(proj, proj, qw, kvw, dq, dkv, dproj)


def _rope_tables(pos, invf, sgn):
    ang = pos * invf
    return jnp.cos(ang), jnp.sin(ang) * sgn


def _swap_halves_lanes(y):
    lane = _lane_iota(y.shape)
    return jnp.where(lane < ROPE // 2, pltpu.roll(y, LANE - ROPE // 2, 1), pltpu.roll(y, ROPE // 2, 1))


def _rope_consts():
    half = ROPE // 2
    inv = ROPE_THETA ** (-jnp.arange(half, dtype=F32) / half)
    invf = jnp.concatenate([inv, inv, jnp.zeros((LANE - ROPE,), F32)])[None, :]
    sgn = jnp.concatenate([-jnp.ones((half,), F32), jnp.ones((half,), F32),
                           jnp.zeros((LANE - ROPE,), F32)])[None, :]
    return invf, sgn


def _mla_rope(qraw, kvraw, proj, pos, nh, misc_col):
    s = qraw.shape[0]
    ts = _tile(s, MLA_BLOCK)
    wq = nh * 2 * HEAD
    invf, sgn = _rope_consts()

    def body(q_ref, kv_ref, misc_ref, pos_ref, if_ref, sg_ref, qc_ref, kc_ref, v_ref, qt_ref, vt_ref, kt_ref):
        c, sn = _rope_tables(pos_ref[...], if_ref[...], sg_ref[...])
        lane = _lane_iota(c.shape)
        rot = lambda xb: xb * c + _swap_halves_lanes(xb) * sn
        qs = SM_SCALE * LOG2E
        krot32 = jnp.where(lane < ROPE, rot(misc_ref[...]), 0.0)
        krot, krot_t = krot32.astype(BF16), krot32.T.astype(BF16)
        for h in range(nh):
            b0 = 2 * HEAD * h
            qn = q_ref[:, b0:b0 + HEAD].astype(F32) * qs
            qr = rot(q_ref[:, b0 + HEAD:b0 + 2 * HEAD].astype(F32)) * qs
            qc_ref[:, b0:b0 + HEAD] = qn.astype(BF16)
            qc_ref[:, b0 + HEAD:b0 + 2 * HEAD] = qr.astype(BF16)
            qt_ref[b0:b0 + HEAD, :] = qn.T.astype(BF16)
            qt_ref[b0 + HEAD:b0 + 2 * HEAD, :] = qr.T.astype(BF16)
            kn = kv_ref[:, b0:b0 + HEAD]
            kc_ref[:, b0:b0 + HEAD] = kn.astype(BF16)
            kc_ref[:, b0 + HEAD:b0 + 2 * HEAD] = krot
            kt_ref[b0:b0 + HEAD, :] = kn.astype(F32).T.astype(BF16)
            kt_ref[b0 + HEAD:b0 + 2 * HEAD, :] = krot_t
            vh = kv_ref[:, wq + HEAD * h:wq + HEAD * (h + 1)]
            v_ref[:, HEAD * h:HEAD * (h + 1)] = vh.astype(BF16)
            vt_ref[HEAD * h:HEAD * (h + 1), :] = vh.astype(F32).T.astype(BF16)

    return _pcall(body, name="mla_rope", grid=(s // ts,),
                  in_specs=[_rows(ts, wq), _rows(ts, wq + nh * HEAD), _rows(ts, LANE, misc_col),
                            _rows(ts, 1), _full((1, LANE)), _full((1, LANE))],
                  out_specs=[_rows(ts, wq), _rows(ts, wq), _rows(ts, nh * HEAD),
                             pl.BlockSpec((None, wq, ts), lambda i: (i, 0, 0)),
                             pl.BlockSpec((None, nh * HEAD, ts), lambda i: (i, 0, 0)),
                             pl.BlockSpec((None, wq, ts), lambda i: (i, 0, 0))],
                  out_shape=[SDS((s, wq), BF16), SDS((s, wq), BF16), SDS((s, nh * HEAD), BF16),
                             SDS((s // ts, wq, ts), BF16), SDS((s // ts, nh * HEAD, ts), BF16),
                             SDS((s // ts, wq, ts), BF16)])(
                      qraw, kvraw, proj, pos, invf, sgn)


def _mla_rope_bwd(dqt, dkc, dv, pos, nh):
    nb, wq, ts = dqt.shape
    s = nb * ts
    invf, sgn = _rope_consts()

    def body(dq_ref, dk_ref, dv_ref, pos_ref, if_ref, sg_ref, oq_ref, okv_ref, okr_ref):
        c, sn = _rope_tables(pos_ref[...], if_ref[...], sg_ref[...])
        lane = _lane_iota(c.shape)
        unrot = lambda d: d * c + _swap_halves_lanes(d * sn)
        dkr = jnp.zeros(c.shape, F32)
        for h in range(nh):
            b0 = 2 * HEAD * h
            oq_ref[:, b0:b0 + HEAD] = (dq_ref[b0:b0 + HEAD, :].T * SM_SCALE).astype(BF16)
            oq_ref[:, b0 + HEAD:b0 + 2 * HEAD] = (
                unrot(dq_ref[b0 + HEAD:b0 + 2 * HEAD, :].T) * SM_SCALE).astype(BF16)
            okv_ref[:, b0:b0 + HEAD] = (dk_ref[:, b0:b0 + HEAD] * LN2).astype(BF16)
            okv_ref[:, b0 + HEAD:b0 + 2 * HEAD] = jnp.zeros((ts, HEAD), BF16)
            dkr = dkr + dk_ref[:, b0 + HEAD:b0 + 2 * HEAD]
        okv_ref[:, wq:] = dv_ref[...].astype(BF16)
        okr_ref[...] = jnp.where(lane < ROPE, unrot(jnp.where(lane < ROPE, dkr * LN2, 0.0)), 0.0)

    return _pcall(body, name="mla_rope_bwd", grid=(s // ts,),
                  in_specs=[pl.BlockSpec((None, wq, ts), lambda i: (i, 0, 0)), _rows(ts, wq), _rows(ts, nh * HEAD),
                            _rows(ts, 1), _full((1, LANE)), _full((1, LANE))],
                  out_specs=[_rows(ts, wq), _rows(ts, wq + nh * HEAD), _rows(ts, LANE)],
                  out_shape=[SDS((s, wq), BF16), SDS((s, wq + nh * HEAD), BF16), SDS((s, LANE), F32)])(
                      dqt, dkc, dv, pos, invf, sgn)


MLA_HP = 2
MLA_FWD_HP = 4


def _mla_fwd(qt, kc, vt, nh, rider):
    nb, _, blk = qt.shape
    s = nb * blk
    hp = MLA_FWD_HP if nh % MLA_FWD_HP == 0 else MLA_HP
    assert nh % hp == 0 and hp % MLA_HP == 0
    once = pl.Buffered(1)
    r_in, r_out = len(rider.arrays), len(rider.out_shapes)

    def body(*refs):
        qt_ref, k_ref, vt_ref = refs[:3]
        o_ref, lse_ref = refs[3 + r_in:5 + r_in]
        m_sc, l_sc, acc = refs[5 + r_in + r_out:8 + r_in + r_out]
        r_refs = (refs[3:3 + r_in], refs[5 + r_in:5 + r_in + r_out], refs[8 + r_in + r_out:])
        i = pl.program_id(1)
        grid_step = pl.program_id(0) * nb + i
        _ride_begin(rider, r_refs, grid_step)
        m_sc[...] = jnp.full_like(m_sc, -1e30)
        l_sc[...] = jnp.zeros_like(l_sc)
        acc[...] = jnp.zeros_like(acc)
        es = range(hp)

        def step(j, masked):
            rows = pl.ds(pl.multiple_of(j * blk, blk), blk)
            sc = [_dot(k_ref[rows, 2 * HEAD * e:2 * HEAD * (e + 1)], qt_ref[2 * HEAD * e:2 * HEAD * (e + 1), :])
                  for e in es]
            if masked:
                key = lax.broadcasted_iota(jnp.int32, (blk, blk), 0)
                qry = lax.broadcasted_iota(jnp.int32, (blk, blk), 1)
                sc = [jnp.where(key <= qry, x, -1e30) for x in sc]
            m_prev = [m_sc[e] for e in es]
            m_new = [jnp.maximum(m_prev[e], jnp.max(sc[e], axis=0, keepdims=True)) for e in es]
            p = [jnp.exp2(sc[e] - m_new[e]) for e in es]
            alpha = [jnp.exp2(m_prev[e] - m_new[e]) for e in es]
            pv = [_dot(vt_ref[j, HEAD * e:HEAD * (e + 1), :], p[e]) for e in es]
            for e in es:
                l_sc[e] = alpha[e] * l_sc[e] + jnp.sum(p[e], axis=0, keepdims=True)
                acc[e] = alpha[e] * acc[e] + pv[e]
                m_sc[e] = m_new[e]

        def loop_body(j, carry):
            step(j, False)
            return carry

        lax.fori_loop(0, i, loop_body, 0)
        step(i, True)
        for e in es:
            o_ref[:, HEAD * e:HEAD * (e + 1)] = (acc[e] / l_sc[e]).T
            lse_ref[e] = jnp.broadcast_to(m_sc[e] + jnp.log(l_sc[e]) * LOG2E, (8, blk))
        _ride_end(rider, r_refs, grid_step, (nh // hp) * nb)

    outs = _pcall(
        body, name="mla_fwd", grid=(nh // hp, nb),
        in_specs=[pl.BlockSpec((None, hp * 2 * HEAD, blk), lambda g, i: (i, g, 0)),
                  pl.BlockSpec((s, hp * 2 * HEAD), lambda g, i: (0, g), pipeline_mode=once),
                  pl.BlockSpec((nb, hp * HEAD, blk), lambda g, i: (0, g, 0), pipeline_mode=once)]
        + [_ANY] * r_in,
        out_specs=[pl.BlockSpec((blk, hp * HEAD), lambda g, i: (i, g)),
                   pl.BlockSpec((hp, None, 8, blk), lambda g, i: (g, i, 0, 0))] + [_ANY] * r_out,
        out_shape=[SDS((s, nh * HEAD), F32), SDS((nh, nb, 8, blk), F32)] + rider.out_shapes,
        scratch=[pltpu.VMEM((hp, 1, blk), F32), pltpu.VMEM((hp, 1, blk), F32),
                 pltpu.VMEM((hp, HEAD, blk), F32)] + rider.scratch)(qt, kc, vt, *rider.arrays)
    return outs[0], outs[1], outs[2:]


def _mla_bwd(qc, qt, kc, kt, v, do, dot, lse, delta, nh, rider):
    nb, _, blk = qt.shape
    s = nb * blk
    hp = MLA_HP
    once = pl.Buffered(1)
    r_in, r_out = len(rider.arrays), len(rider.out_shapes)
    qs = [slice(2 * HEAD * e, 2 * HEAD * (e + 1)) for e in range(hp)]
    vs = [slice(HEAD * e, HEAD * (e + 1)) for e in range(hp)]

    def body(*refs):
        q_ref, qt_ref, do_ref, dot_ref, lse_ref, dl_ref, k_ref, kt_ref, v_ref = refs[:9]
        dqt_ref, dk_ref, dv_ref = refs[9 + r_in:12 + r_in]
        dk_acc, dv_acc = refs[12 + r_in + r_out:14 + r_in + r_out]
        r_refs = (refs[9:9 + r_in], refs[12 + r_in:12 + r_in + r_out], refs[14 + r_in + r_out:])
        j = pl.program_id(1)
        grid_step = pl.program_id(0) * nb + j
        _ride_begin(rider, r_refs, grid_step)

        @pl.when(j == 0)
        def _():
            dqt_ref[...] = jnp.zeros_like(dqt_ref)

        dk_acc[...] = jnp.zeros_like(dk_acc)
        dv_acc[...] = jnp.zeros_like(dv_acc)
        es = range(hp)
        kj = [k_ref[:, qs[e]] for e in es]
        ktj = [kt_ref[qs[e], :] for e in es]
        vj = [v_ref[:, vs[e]] for e in es]

        def step(i, masked):
            rows = pl.ds(pl.multiple_of(i * blk, blk), blk)
            sc = [_dot(kj[e], qt_ref[i, qs[e], :]) for e in es]
            dp = [_dot(vj[e], dot_ref[i, vs[e], :]) for e in es]
            if masked:
                key = lax.broadcasted_iota(jnp.int32, (blk, blk), 0)
                qry = lax.broadcasted_iota(jnp.int32, (blk, blk), 1)
                sc = [jnp.where(key <= qry, x, -1e30) for x in sc]
            p = [jnp.exp2(sc[e] - lse_ref[e, i, 0:1, :]) for e in es]
            ds = [p[e] * (dp[e] - dl_ref[e, i, 0:1, :]) for e in es]
            dv = [_dot(p[e], do_ref[rows, vs[e]]) for e in es]
            dk = [_dot(ds[e], q_ref[rows, qs[e]]) for e in es]
            dq = [_dot(ktj[e], ds[e]) for e in es]
            for e in es:
                dv_acc[:, vs[e]] += dv[e]
                dk_acc[:, qs[e]] += dk[e]
                dqt_ref[i, qs[e], :] += dq[e]

        def loop_body(i, carry):
            step(i, False)
            return carry

        step(j, True)
        lax.fori_loop(j + 1, nb, loop_body, 0)
        dk_ref[...] = dk_acc[...]
        dv_ref[...] = dv_acc[...]
        _ride_end(rider, r_refs, grid_step, (nh // hp) * nb)

    rows_spec = pl.BlockSpec((hp, nb, 8, blk), lambda g, j: (g, 0, 0, 0), pipeline_mode=once)
    outs = _pcall(
        body, name="mla_bwd", grid=(nh // hp, nb),
        in_specs=[pl.BlockSpec((s, hp * 2 * HEAD), lambda g, j: (0, g), pipeline_mode=once),
                  pl.BlockSpec((nb, hp * 2 * HEAD, blk), lambda g, j: (0, g, 0), pipeline_mode=once),
                  pl.BlockSpec((s, hp * HEAD), lambda g, j: (0, g), pipeline_mode=once),
                  pl.BlockSpec((nb, hp * HEAD, blk), lambda g, j: (0, g, 0), pipeline_mode=once),
                  rows_spec, rows_spec,
                  pl.BlockSpec((blk, hp * 2 * HEAD), lambda g, j: (j, g)),
                  pl.BlockSpec((None, hp * 2 * HEAD, blk), lambda g, j: (j, g, 0)),
                  pl.BlockSpec((blk, hp * HEAD), lambda g, j: (j, g))] + [_ANY] * r_in,
        out_specs=[pl.BlockSpec((nb, hp * 2 * HEAD, blk), lambda g, j: (0, g, 0), pipeline_mode=once),
                   pl.BlockSpec((blk, hp * 2 * HEAD), lambda g, j: (j, g)),
                   pl.BlockSpec((blk, hp * HEAD), lambda g, j: (j, g))] + [_ANY] * r_out,
        out_shape=[SDS((nb, nh * 2 * HEAD, blk), F32), SDS((s, nh * 2 * HEAD), F32),
                   SDS((s, nh * HEAD), F32)] + rider.out_shapes,
        scratch=[pltpu.VMEM((blk, hp * 2 * HEAD), F32), pltpu.VMEM((blk, hp * HEAD), F32)] + rider.scratch,
        vmem=VMEM_LIMIT_WIDE)(qc, qt, do, dot, lse, delta, kc, kt, v, *rider.arrays)
    return outs[0], outs[1], outs[2], outs[3:]


def _mix_fwd(og, proj, om, gw, mw, nh, z_col):
    s = og.shape[0]
    w = nh * HEAD
    ts = _tile(s, 256, 8)

    def body(og_ref, z_ref, om_ref, gw_ref, mw_ref, o_ref):
        for h in range(nh):
            hs = slice(HEAD * h, HEAD * (h + 1))
            a = og_ref[:, hs]
            r = lax.rsqrt(jnp.mean(a * a, axis=-1, keepdims=True) + EPS)
            o_ref[:, hs] = (a * r * gw_ref[...] * _silu(z_ref[:, hs])).astype(BF16)
            b = om_ref[:, hs]
            r = lax.rsqrt(jnp.mean(b * b, axis=-1, keepdims=True) + EPS)
            o_ref[:, w + HEAD * h:w + HEAD * (h + 1)] = (b * r * mw_ref[...]).astype(BF16)

    return _pcall(body, name="mix_fwd", grid=(s // ts,),
                  in_specs=[_rows(ts, w), _rows(ts, w, z_col), _rows(ts, w), _full((1, HEAD)),
                            _full((1, HEAD))],
                  out_specs=_rows(ts, 2 * w), out_shape=SDS((s, 2 * w), BF16))(og, proj, om, gw, mw)


def _mix_bwd(dmix, og, proj, om, gw, mw, nh, z_col):
    s = og.shape[0]
    w = nh * HEAD
    ts = _tile(s, MLA_BLOCK)

    def body(d_ref, og_ref, z_ref, om_ref, gw_ref, mw_ref, dog_ref, dz_ref, dom_ref, dgw_ref, dmw_ref,
             dl_ref, domt_ref):
        @pl.when(pl.program_id(0) == 0)
        def _():
            dgw_ref[...] = jnp.zeros_like(dgw_ref)
            dmw_ref[...] = jnp.zeros_like(dmw_ref)

        dgw = jnp.zeros((1, HEAD), F32)
        dmw = jnp.zeros((1, HEAD), F32)
        for h in range(nh):
            hs = slice(HEAD * h, HEAD * (h + 1))
            a, z, dy = og_ref[:, hs], z_ref[:, hs], d_ref[:, hs]
            r = lax.rsqrt(jnp.mean(a * a, axis=-1, keepdims=True) + EPS)
            ah = a * r
            sz = _silu(z)
            dz_ref[:, hs] = (dy * (ah * gw_ref[...]) * _dsilu(z)).astype(BF16)
            dn = dy * sz
            dgw = dgw + jnp.sum(dn * ah, axis=0, keepdims=True)
            dah = dn * gw_ref[...]
            dog_ref[:, hs] = r * (dah - ah * jnp.mean(dah * ah, axis=-1, keepdims=True))
            b, dyb = om_ref[:, hs], d_ref[:, w + HEAD * h:w + HEAD * (h + 1)]
            r = lax.rsqrt(jnp.mean(b * b, axis=-1, keepdims=True) + EPS)
            bh = b * r
            dmw = dmw + jnp.sum(dyb * bh, axis=0, keepdims=True)
            dbh = dyb * mw_ref[...]
            dom = r * (dbh - bh * jnp.mean(dbh * bh, axis=-1, keepdims=True))
            dom_ref[:, hs] = dom.astype(BF16)
            domt_ref[hs, :] = dom.T.astype(BF16)
            delta = jnp.broadcast_to(jnp.sum(dom * b, axis=-1, keepdims=True), (ts, LANE))
            dl_ref[h] = delta.T[0:8, :]
        dgw_ref[...] += dgw
        dmw_ref[...] += dmw

    return _pcall(body, name="mix_bwd", grid=(s // ts,),
                  in_specs=[_rows(ts, 2 * w), _rows(ts, w), _rows(ts, w, z_col), _rows(ts, w),
                            _full((1, HEAD)), _full((1, HEAD))],
                  out_specs=[_rows(ts, w), _rows(ts, w, z_col), _rows(ts, w), _full((1, HEAD)), _full((1, HEAD)),
                             pl.BlockSpec((nh, None, 8, ts), lambda i: (0, i, 0, 0)),
                             pl.BlockSpec((None, w, ts), lambda i: (i, 0, 0))],
                  out_shape=[SDS((s, w), F32), SDS((s, proj.shape[1]), BF16), SDS((s, w), BF16),
                             SDS((1, HEAD), F32), SDS((1, HEAD), F32),
                             SDS((nh, s // ts, 8, ts), F32), SDS((s // ts, w, ts), BF16)])(
                                 dmix, og, proj, om, gw, mw)


def _swiglu_fwd(h2, wg, wu):
    m, kdim = h2.shape
    tn = wg.shape[2]
    n = 4 * tn
    tm, tk = _tile(m, 512), _tile(kdim, 2048)
    nk = kdim // tk

    def body(a_ref, g_ref, u_ref, act_ref, go_ref, uo_ref, gacc, uacc):
        k = pl.program_id(2)

        @pl.when(k == 0)
        def _():
            gacc[...] = jnp.zeros_like(gacc)
            uacc[...] = jnp.zeros_like(uacc)

        a = a_ref[...]
        gacc[...] += _dot(a, g_ref[...])
        uacc[...] += _dot(a, u_ref[...])

        @pl.when(k == nk - 1)
        def _():
            g, u = gacc[...], uacc[...]
            act_ref[...] = (_silu(g) * u).astype(BF16)
            go_ref[...] = g.astype(BF16)
            uo_ref[...] = u.astype(BF16)

    a_spec = pl.BlockSpec((tm, tk), lambda i, j, k: (i, k))
    b_spec = pl.BlockSpec((None, tk, tn), lambda i, j, k: (j, k, 0))
    o_spec = pl.BlockSpec((tm, tn), lambda i, j, k: (i, j))
    return _pcall(body, name="swiglu_fwd", grid=(m // tm, n // tn, nk),
                  in_specs=[a_spec, b_spec, b_spec], out_specs=[o_spec] * 3,
                  out_shape=[SDS((m, n), BF16)] * 3,
                  scratch=[pltpu.VMEM((tm, tn), F32), pltpu.VMEM((tm, tn), F32)])(h2, wg, wu)


def _swiglu_bwd(dx3, wd, g, u):
    m, kdim = dx3.shape
    n = wd.shape[0]
    tm, tn = _tile(m, 1024), _tile(n, 512)

    def body(a_ref, b_ref, g_ref, u_ref, dg_ref, du_ref):
        da = _dot_nt(a_ref[...], b_ref[...])
        gv, uv = g_ref[...].astype(F32), u_ref[...].astype(F32)
        dg_ref[...] = (da * uv * _dsilu(gv)).astype(BF16)
        du_ref[...] = (da * _silu(gv)).astype(BF16)

    a_spec = pl.BlockSpec((tm, kdim), lambda i, j: (i, 0))
    b_spec = pl.BlockSpec((tn, kdim), lambda i, j: (j, 0))
    o_spec = pl.BlockSpec((tm, tn), lambda i, j: (i, j))
    return _pcall(body, name="swiglu_bwd", grid=(m // tm, n // tn),
                  in_specs=[a_spec, b_spec, o_spec, o_spec], out_specs=[o_spec] * 2,
                  out_shape=[SDS((m, n), BF16)] * 2)(dx3, wd, g, u)


def _sum_pair(g, recv, place, name):
    _, _, rh, c = g.shape
    tr = _tile(rh, 256, 16)

    def body(pl_ref, g_ref, r_ref, o16_ref, own_ref):
        sm = g_ref[...].astype(F32) + r_ref[...].astype(F32)
        o16_ref[...] = sm.astype(BF16)

        @pl.when(pl.program_id(1) == pl_ref[1])
        def _():
            own_ref[...] = sm

    grid_spec = pltpu.PrefetchScalarGridSpec(
        num_scalar_prefetch=1, grid=(rh // tr, 4),
        in_specs=[pl.BlockSpec((None, None, tr, c), lambda i, t, p: (t, p[0], i, 0)),
                  pl.BlockSpec((None, tr, c), lambda i, t, p: (t, i, 0))],
        out_specs=[pl.BlockSpec((None, tr, c), lambda i, t, p: (t, i, 0)),
                   pl.BlockSpec((tr, c), lambda i, t, p: (i, 0))])
    return pl.pallas_call(
        body, name=name, grid_spec=grid_spec,
        out_shape=[SDS((4, rh, c), BF16), SDS((rh, c), F32)],
        compiler_params=pltpu.CompilerParams(dimension_semantics=("arbitrary",) * 2,
                                             vmem_limit_bytes=VMEM_LIMIT))(place, g, recv)


def _sum_chips(own, recv, name):
    rh, c = own.shape
    tr = _tile(rh, 256, 16)

    def body(o_ref, r_ref, out_ref):
        acc = o_ref[...]
        for j in range(3):
            acc = acc + r_ref[j].astype(F32)
        out_ref[...] = acc

    return _pcall(body, name=name, grid=(rh // tr,),
                  in_specs=[_rows(tr, c), pl.BlockSpec((3, tr, c), lambda i: (0, i, 0))],
                  out_specs=_rows(tr, c), out_shape=SDS(own.shape, F32))(own, recv)


def _adamw_update(wv, gv, mv, vv):
    mn = ADAM_B1 * mv + (1.0 - ADAM_B1) * gv
    vn = ADAM_B2 * vv + (1.0 - ADAM_B2) * (gv * gv)
    m_hat = mn / (1.0 - ADAM_B1 ** ADAM_STEP)
    v_hat = vn / (1.0 - ADAM_B2 ** ADAM_STEP)
    return -ADAM_LR * (m_hat / (jnp.sqrt(v_hat) + ADAM_EPS) + ADAM_WD * wv), mn, vn


def _adamw(w, g, m, v, name):
    r, c = w.shape
    tr = _tile(r, 256, 8)

    def body(w_ref, g_ref, m_ref, v_ref, d_ref, mo_ref, vo_ref):
        d_ref[...], mo_ref[...], vo_ref[...] = _adamw_update(w_ref[...], g_ref[...], m_ref[...], v_ref[...])

    spec = _rows(tr, c)
    return _pcall(body, name=name, grid=(r // tr,), in_specs=[spec] * 4, out_specs=[spec] * 3,
                  out_shape=[SDS(w.shape, F32)] * 3)(w, g, m, v)


def _adamw_halves(w, mine, theirs, m, v, place, name):
    r, c = w.shape
    rh = r // 2
    tr = _tile(rh, 256, 8)
    nt = rh // tr

    def body(p_ref, w_ref, a_ref, b_ref, m_ref, v_ref, g_ref, d_ref, mo_ref, vo_ref):
        gv = jnp.where(pl.program_id(0) // nt == p_ref[0], a_ref[...], b_ref[...])
        g_ref[...] = gv
        d_ref[...], mo_ref[...], vo_ref[...] = _adamw_update(w_ref[...], gv, m_ref[...], v_ref[...])

    full = pl.BlockSpec((tr, c), lambda i, p: (i, 0))
    half = pl.BlockSpec((tr, c), lambda i, p: (i % nt, 0))
    grid_spec = pltpu.PrefetchScalarGridSpec(num_scalar_prefetch=1, grid=(2 * nt,),
                                             in_specs=[full, half, half, full, full], out_specs=[full] * 4)
    return pl.pallas_call(
        body, name=name, grid_spec=grid_spec, out_shape=[SDS(w.shape, F32)] * 4,
        compiler_params=pltpu.CompilerParams(dimension_semantics=("arbitrary",),
                                             vmem_limit_bytes=VMEM_LIMIT))(place, w, mine, theirs, m, v)


def _place():
    x, y, c = lax.axis_index("x"), lax.axis_index("y"), lax.axis_index("c")
    chips = [(1 - x, y), (x, 1 - y), (1 - x, 1 - y)]
    return x, y, c, chips


_ANY = pl.BlockSpec(memory_space=pl.ANY)


def _remote(src, dst, sems, k, to):
    return pltpu.make_async_remote_copy(src_ref=src, dst_ref=dst, send_sem=sems[0].at[k], recv_sem=sems[1].at[k],
                                        device_id=to, device_id_type=MESH)


class _Gather:
    def __init__(self, shards):
        n = len(shards)
        self.arrays = list(shards)
        self.out_shapes = [SDS((4,) + a.shape, a.dtype) for a in shards]
        self.scratch = [pltpu.SemaphoreType.DMA((7 * n,)), pltpu.SemaphoreType.DMA((7 * n,))]

    def _plan(self, ins, outs, sems):
        x, y, c, chips = _place()
        own, sib = 2 * x + y, (x, y, 1 - c)
        plan = []
        for wi, (w, o) in enumerate(zip(ins, outs)):
            rh = w.shape[0] // 2
            mine, theirs = pl.ds(c * rh, rh), pl.ds((1 - c) * rh, rh)
            whole = _remote(w, o.at[own], sems, 7 * wi + 6, sib)
            ici, d2d, d2d_in = [], [], []
            for j, (tx, ty) in enumerate(chips):
                t = 2 * tx + ty
                ici.append(_remote(w.at[mine], o.at[own, mine], sems, 7 * wi + j, (tx, ty, c)))
                d2d.append(_remote(o.at[t, mine], o.at[t, mine], sems, 7 * wi + 3 + j, sib))
                d2d_in.append(_remote(o.at[t, theirs], o.at[t, theirs], sems, 7 * wi + 3 + j, sib))
            plan.append((whole, ici, d2d, d2d_in))
        return plan

    def begin(self, ins, outs, sems):
        for whole, ici, _, _ in self._plan(ins, outs, sems):
            whole.start()
            for cp in ici:
                cp.start()

    def middle(self, ins, outs, sems):
        for _, ici, d2d, _ in self._plan(ins, outs, sems):
            for cp_in, cp_on in zip(ici, d2d):
                cp_in.wait_recv()
                cp_on.start()

    def finish(self, ins, outs, sems):
        for whole, ici, d2d, d2d_in in self._plan(ins, outs, sems):
            for cp in d2d_in:
                cp.wait_recv()
            for cp in ici + d2d:
                cp.wait_send()
            whole.wait()


class _Swap:
    def __init__(self, grads):
        n = len(grads)
        self.arrays = list(grads)
        self.out_shapes = [SDS((4,) + g.shape[2:], g.dtype) for g in grads]
        self.scratch = [pltpu.SemaphoreType.DMA((4 * n,)), pltpu.SemaphoreType.DMA((4 * n,))]

    def _plan(self, ins, outs, sems):
        x, y, c, _ = _place()
        return [_remote(g.at[t, 1 - c], o.at[t], sems, 4 * wi + t, (x, y, 1 - c))
                for wi, (g, o) in enumerate(zip(ins, outs)) for t in range(4)]

    def begin(self, ins, outs, sems):
        for cp in self._plan(ins, outs, sems):
            cp.start()

    def middle(self, ins, outs, sems):
        pass

    def finish(self, ins, outs, sems):
        for cp in self._plan(ins, outs, sems):
            cp.wait()


class _Exchange:
    def __init__(self, pieces):
        n = len(pieces)
        self.arrays = list(pieces)
        self.out_shapes = [SDS((3,) + p.shape[1:], p.dtype) for p in pieces]
        self.scratch = [pltpu.SemaphoreType.DMA((3 * n,)), pltpu.SemaphoreType.DMA((3 * n,))]

    def _plan(self, ins, outs, sems):
        x, y, c, chips = _place()
        return [_remote(g.at[2 * tx + ty], o.at[j], sems, 3 * wi + j, (tx, ty, c))
                for wi, (g, o) in enumerate(zip(ins, outs)) for j, (tx, ty) in enumerate(chips)]

    def begin(self, ins, outs, sems):
        for cp in self._plan(ins, outs, sems):
            cp.start()

    def middle(self, ins, outs, sems):
        pass

    def finish(self, ins, outs, sems):
        for cp in self._plan(ins, outs, sems):
            cp.wait()


class _Share:
    def __init__(self, totals):
        n = len(totals)
        self.arrays = list(totals)
        self.out_shapes = [SDS(t.shape, t.dtype) for t in totals]
        self.scratch = [pltpu.SemaphoreType.DMA((n,)), pltpu.SemaphoreType.DMA((n,))]

    def _plan(self, ins, outs, sems):
        x, y, c, _ = _place()
        return [_remote(t, o, sems, wi, (x, y, 1 - c)) for wi, (t, o) in enumerate(zip(ins, outs))]

    def begin(self, ins, outs, sems):
        for cp in self._plan(ins, outs, sems):
            cp.start()

    def middle(self, ins, outs, sems):
        pass

    def finish(self, ins, outs, sems):
        for cp in self._plan(ins, outs, sems):
            cp.wait()


def _ride_begin(rider, r_refs, step):
    @pl.when(step == 0)
    def _():
        rider.begin(*r_refs)


def _ride_end(rider, r_refs, step, nsteps):
    @pl.when(step == min(3 * nsteps // 4, nsteps - 1))
    def _():
        rider.middle(*r_refs)

    @pl.when(step == nsteps - 1)
    def _():
        rider.finish(*r_refs)


def _comm(rider, name):
    n_in, n_out = len(rider.arrays), len(rider.out_shapes)

    def body(*refs):
        r_refs = (refs[:n_in], refs[n_in:n_in + n_out], refs[n_in + n_out:])
        rider.begin(*r_refs)
        rider.middle(*r_refs)
        rider.finish(*r_refs)

    return pl.pallas_call(body, name=name, out_shape=rider.out_shapes, in_specs=[_ANY] * n_in,
                          out_specs=[_ANY] * n_out, scratch_shapes=rider.scratch)(*rider.arrays)


def _small_allreduce(pk, name):
    r = pk.shape[0]
    rels = [(dx, dy, dc) for dx in (0, 1) for dy in (0, 1) for dc in (0, 1) if dx or dy or dc]

    def body(p_ref, o_ref, buf, send_sems, recv_sems):
        x, y, c, _ = _place()
        me = 4 * x + 2 * y + c
        buf[me] = p_ref[...]
        cps = []
        for k, (dx, dy, dc) in enumerate(rels):
            to = (1 - x if dx else x, 1 - y if dy else y, 1 - c if dc else c)
            cps.append(pltpu.make_async_remote_copy(src_ref=p_ref, dst_ref=buf.at[me], send_sem=send_sems.at[k],
                                                    recv_sem=recv_sems.at[k], device_id=to,
                                                    device_id_type=MESH))
        for cpy in cps:
            cpy.start()
        for cpy in cps:
            cpy.wait()
        acc = buf[0]
        for d in range(1, 8):
            acc = acc + buf[d]
        o_ref[...] = acc

    vm = pl.BlockSpec(memory_space=pltpu.VMEM)
    return pl.pallas_call(body, name=name, out_shape=SDS(pk.shape, F32), in_specs=[vm], out_specs=vm,
                          scratch_shapes=[pltpu.VMEM((8, r, LANE), F32), pltpu.SemaphoreType.DMA((7,)),
                                          pltpu.SemaphoreType.DMA((7,))])(pk)


ATTN_W = ("w_in", "w_uq", "w_ukv", "w_out")
FFN_W = ("w_gate", "w_up", "w_down")
BIG = ATTN_W + FFN_W


def _cols_from_chips(g):
    return jnp.concatenate([g[t] for t in range(4)], axis=1)


def _cols_to_chips(full):
    r, n = full.shape
    return full.reshape(r, 4, n // 4).transpose(1, 0, 2).reshape(4, 2, r // 2, n // 4)


def _rows_to_chips(full):
    n, c = full.shape
    return full.reshape(4, 2, n // 8, c)


def _permute_w_in(w, nh):
    d = w.shape[0]
    g = 4 * nh * HEAD
    lr = (w.shape[1] - g - 2 * nh - ROPE) // 2
    o = g + 2 * nh
    pad = jnp.zeros((d, LANE - ROPE - 8 - nh), w.dtype)
    pad8 = jnp.zeros((d, 8 - nh), w.dtype)
    return jnp.concatenate([w[:, :g], w[:, o:o + 2 * lr], w[:, o + 2 * lr:], w[:, g:g + nh], pad8,
                            w[:, g + nh:g + 2 * nh], pad, jnp.zeros((d, LANE), w.dtype)], axis=1)


def _unpermute_w_in(wp, nh, lr):
    g = 4 * nh * HEAD
    mc = g + 2 * lr
    return jnp.concatenate([wp[:, :g], wp[:, mc + B_LANE:mc + B_LANE + nh], wp[:, mc + A_LANE:mc + A_LANE + nh],
                            wp[:, g:g + 2 * lr], wp[:, mc:mc + ROPE]], axis=1)


def _permute_w_uq(w, nh):
    lr = w.shape[0]
    w3 = w.reshape(lr, nh, HEAD + ROPE)
    return jnp.concatenate([w3, jnp.zeros((lr, nh, HEAD - ROPE), w.dtype)], axis=2).reshape(lr, nh * 2 * HEAD)


def _unpermute_w_uq(wp, nh):
    lr = wp.shape[0]
    return wp.reshape(lr, nh, 2 * HEAD)[:, :, :HEAD + ROPE].reshape(lr, nh * (HEAD + ROPE))


def _permute_w_ukv(w, nh):
    lr = w.shape[0]
    w3 = w.reshape(lr, nh, 2 * HEAD)
    kp = jnp.concatenate([w3[:, :, :HEAD], jnp.zeros((lr, nh, HEAD), w.dtype)], axis=2)
    return jnp.concatenate([kp.reshape(lr, nh * 2 * HEAD), w3[:, :, HEAD:].reshape(lr, nh * HEAD)], axis=1)


def _unpermute_w_ukv(wp, nh):
    lr = wp.shape[0]
    kp = wp[:, :nh * 2 * HEAD].reshape(lr, nh, 2 * HEAD)[:, :, :HEAD]
    vp = wp[:, nh * 2 * HEAD:].reshape(lr, nh, HEAD)
    return jnp.concatenate([kp, vp], axis=2).reshape(lr, nh * 2 * HEAD)


def _sum_pairs(grads, recv, place, tag):
    sums = [_sum_pair(g, r, place, "sum_pair_%s%d" % (tag, k)) for k, (g, r) in enumerate(zip(grads, recv))]
    return [s[0] for s in sums], [s[1] for s in sums]


def _reduce_end(own, recv, tag):
    return [_sum_chips(o, r, "sum_chips_%s%d" % (tag, k)) for k, (o, r) in enumerate(zip(own, recv))]


def _step(x, pos, tgt, w_in, attn_shards, ffn_shards, small, place):
    nh = small["a_log"].shape[1]
    lr = small["q_norm_w"].shape[1]
    w = nh * HEAD
    z_col, col_q, col_kv = 3, 4 * w // lr, 4 * w // lr + 1
    misc_c = 4 * w + 2 * lr
    misc_col = misc_c // LANE
    assert (4 * w) % lr == 0 and small["kv_norm_w"].shape[1] == lr

    zl = jnp.zeros((1, LANE), F32)
    alog_l = zl.at[:, A_LANE:A_LANE + nh].set(small["a_log"])
    dtb_l = zl.at[:, A_LANE:A_LANE + nh].set(small["dt_bias"])
    conv_w = small["conv_w"]

    h1, (in4,) = _norm_fwd(x, small["attn_norm_w"], "norm1", rider=_Gather([w_in]))
    win_p = _permute_w_in(_cols_from_chips(in4), nh)
    proj, (uq4, ukv4, out4) = _mm([(h1, win_p)], name="proj_in", rider=_Gather(attn_shards))
    wuq_p = _permute_w_uq(_cols_from_chips(uq4), nh)
    wukv_p = _permute_w_ukv(_cols_from_chips(ukv4), nh)
    w_out = out4.reshape(-1, out4.shape[2])
    gq, gk, gv, gb, gbt = _gdn_prep(proj, conv_w, alog_l, dtb_l, nh, misc_col)
    (o_gdn, states), (wg4,) = _gdn_fwd(gq, gk, gv, gb, gbt, nh, _Gather(ffn_shards[:1]))
    cqn, ckvn = _mla_norm(proj, small["q_norm_w"], small["kv_norm_w"], col_q, col_kv)
    qraw = _mm([(cqn, wuq_p)], name="proj_uq", out_dtype=BF16)
    kvraw = _mm([(ckvn, wukv_p)], name="proj_ukv", out_dtype=BF16)
    qc, kc, vv, qt, vt, kt = _mla_rope(qraw, kvraw, proj, pos, nh, misc_col)
    o_mla, lse, (wu4, wd4) = _mla_fwd(qt, kc, vt, nh, _Gather(ffn_shards[1:]))
    w_down = wd4.reshape(-1, wd4.shape[2])
    mixed = _mix_fwd(o_gdn, proj, o_mla, small["gdn_norm_w"], small["mla_out_norm_w"], nh, z_col)
    x2, h2 = _proj_norm(mixed, w_out, x, small["ffn_norm_w"], "proj_out_norm2")
    act, gpre, upre = _swiglu_fwd(h2, wg4, wu4)
    x3 = _mm([(act, w_down)], name="proj_down", res=x2, tk=2816)
    dx3, d_final, loss, dx3h = _final_loss(x3, tgt, small["final_norm_w"])

    gs = {"final_norm_w": d_final}
    dgate, dup = _swiglu_bwd(dx3h, w_down, gpre, upre)
    g_down = _rows_to_chips(_mm([(act, dx3h)], name="dw_down", ta=True, out_dtype=BF16, tm=1408))
    g_gate = _mm([(h2, dgate)], name="dw_gate", ta=True, out_dtype=BF16, out_chips=True)
    g_up = _mm([(h2, dup)], name="dw_up", ta=True, out_dtype=BF16, out_chips=True)
    halves = lambda g: g.reshape(4, 2, g.shape[1] // 2, g.shape[2])
    ffn_g = [halves(g_gate), halves(g_up), g_down]
    dh2, ffn_sib = _mm([(dgate, wg4), (dup, wu4)], name="dh2", tb=True, b_chips=True, out_dtype=BF16,
                       rider=_Swap(ffn_g))
    ffn16, ffn_own = _sum_pairs(ffn_g, ffn_sib, place, "ffn")
    dx2, gs["ffn_norm_w"], dx2h = _norm_bwd(dh2, x2, small["ffn_norm_w"], dx3, "norm2_bwd", True)
    dmix = _mm([(dx2h, w_out)], name="dmix", tb=True, out_dtype=BF16)
    g_out = _rows_to_chips(_mm([(mixed, dx2h)], name="dw_out", ta=True, out_dtype=BF16))
    d_ogdn, dproj, d_omla, gs["gdn_norm_w"], gs["mla_out_norm_w"], delta, d_omla_t = _mix_bwd(
        dmix, o_gdn, proj, o_mla, small["gdn_norm_w"], small["mla_out_norm_w"], nh, z_col)
    dqc, dkc, dvv, ffn_recv = _mla_bwd(qc, qt, kc, kt, vv, d_omla, d_omla_t, lse, delta, nh, _Exchange(ffn16))
    ffn_tot = _reduce_end(ffn_own, ffn_recv, "ffn")
    dqraw, dkvraw, dkr = _mla_rope_bwd(dqc, dkc, dvv, pos, nh)
    dcqn = _mm([(dqraw, wuq_p)], name="dcqn", tb=True)
    dckvn = _mm([(dkvraw, wukv_p)], name="dckvn", tb=True)
    g_uq = _cols_to_chips(_unpermute_w_uq(_mm([(cqn, dqraw)], name="dw_uq", ta=True, out_dtype=BF16), nh))
    g_ukv = _cols_to_chips(_unpermute_w_ukv(_mm([(ckvn, dkvraw)], name="dw_ukv", ta=True, out_dtype=BF16), nh))
    dproj, gs["q_norm_w"], gs["kv_norm_w"] = _mla_norm_bwd(
        proj, small["q_norm_w"], small["kv_norm_w"], dcqn, dckvn, dproj, col_q, col_kv)
    (dgq, dgk, dgv, dgb), ffn_shared = _gdn_bwd(gq, gk, gv, gb, gbt, states, d_ogdn, nh, _Share(ffn_tot))
    dconv, dproj, gs["conv_w"], dal, ddb = _gdn_prep_bwd(
        proj, conv_w, alog_l, dtb_l, dgq, dgk, dgv, dgb, dkr, dproj, nh, misc_col)
    gs["a_log"] = dal[:, A_LANE:A_LANE + nh]
    gs["dt_bias"] = ddb[:, A_LANE:A_LANE + nh]
    dproj = _conv_bwd_input(dconv, conv_w, dproj)
    g_in = _cols_to_chips(_unpermute_w_in(_mm([(h1, dproj)], name="dw_in", ta=True, out_dtype=BF16), nh, lr))
    att_g = [g_in, g_uq, g_ukv, g_out]
    att16, att_own = _sum_pairs(att_g, _comm(_Swap(att_g), "swap_att"), place, "att")
    dh1, att_recv = _mm([(dproj, win_p)], name="dh1", tb=True, out_dtype=BF16, rider=_Exchange(att16))
    att_tot = _reduce_end(att_own, att_recv, "att")
    att_shared = _comm(_Share(att_tot), "share_att")
    grad_x, gs["attn_norm_w"] = _norm_bwd(dh1, x, small["attn_norm_w"], dx2, "norm1_bwd", False)
    return loss, grad_x, att_tot + ffn_tot, list(att_shared) + list(ffn_shared), gs


SMALL = ("attn_norm_w", "ffn_norm_w", "final_norm_w", "q_norm_w", "kv_norm_w", "gdn_norm_w",
         "mla_out_norm_w", "a_log", "dt_bias")
WEIGHTS = ("attn_norm_w", "w_in", "conv_w", "a_log", "dt_bias", "gdn_norm_w", "q_norm_w", "w_uq",
           "kv_norm_w", "w_ukv", "mla_out_norm_w", "w_out", "ffn_norm_w", "w_gate", "w_up", "w_down",
           "final_norm_w")


def _pack_small(vecs):
    flat = jnp.concatenate([v.astype(F32).reshape(-1) for v in vecs])
    pad = (-flat.shape[0]) % (8 * LANE)
    return jnp.concatenate([flat, jnp.zeros((pad,), F32)]).reshape(-1, LANE)


def kernel(x, positions, attn_norm_w, w_in, conv_w, a_log, dt_bias, gdn_norm_w, q_norm_w, w_uq, kv_norm_w, w_ukv, mla_out_norm_w, w_out, ffn_norm_w, w_gate, w_up, w_down, final_norm_w, loss_target, m_attn_norm_w, m_w_in, m_conv_w, m_a_log, m_dt_bias, m_gdn_norm_w, m_q_norm_w, m_w_uq, m_kv_norm_w, m_w_ukv, m_mla_out_norm_w, m_w_out, m_ffn_norm_w, m_w_gate, m_w_up, m_w_down, m_final_norm_w, v_attn_norm_w, v_w_in, v_conv_w, v_a_log, v_dt_bias, v_gdn_norm_w, v_q_norm_w, v_w_uq, v_kv_norm_w, v_w_ukv, v_mla_out_norm_w, v_w_out, v_ffn_norm_w, v_w_gate, v_w_up, v_w_down, v_final_norm_w):
    args = dict(locals())
    xi, yi, ci = lax.axis_index("x"), lax.axis_index("y"), lax.axis_index("c")
    chip = 2 * xi + yi

    def two_d(a):
        return a.reshape(a.shape[-2:]) if a.ndim >= 2 else a.reshape(1, -1)

    wloc = {n: two_d(args[n]) for n in WEIGHTS}
    mloc = {n: two_d(args["m_" + n]) for n in WEIGHTS}
    vloc = {n: two_d(args["v_" + n]) for n in WEIGHTS}

    cw = wloc["conv_w"]
    cshard = cw.shape[1]
    cfull = jnp.zeros((CONV, 4 * cshard), F32)
    cfull = lax.dynamic_update_slice(cfull, jnp.where(ci == 0, cw, 0.0), (0, chip * cshard))
    conv_full = _small_allreduce(_pack_small([cfull]), "gather_conv_w").reshape(-1)[:CONV * 4 * cshard]
    conv_full = conv_full.reshape(CONV, 4 * cshard)

    small = {n: wloc[n] for n in SMALL}
    small["conv_w"] = conv_full

    pos = positions.reshape(-1, 1).astype(F32)
    place = jnp.stack([ci, chip]).astype(jnp.int32)
    loss, grad_x, totals, from_sib, gs = _step(
        two_d(x), pos, two_d(loss_target), wloc["w_in"].astype(BF16), [wloc[n].astype(BF16) for n in ATTN_W[1:]],
        [wloc[n].astype(BF16) for n in FFN_W], small, place)

    small_names = SMALL + ("conv_w",)
    pk = _pack_small([gs[n] for n in small_names] + [loss])
    red = _small_allreduce(pk, "reduce_small").reshape(-1)
    gsm, off = {}, 0
    for n in small_names:
        shp = gs[n].shape
        gsm[n] = red[off:off + shp[0] * shp[1]].reshape(shp)
        off += shp[0] * shp[1]
    loss_out = red[off]
    gsm["conv_w"] = lax.dynamic_slice(gsm["conv_w"], (0, chip * cshard), (CONV, cshard))

    grads, deltas, new_m, new_v = {}, {}, {}, {}
    for n, mine, theirs in zip(BIG, totals, from_sib):
        grads[n], deltas[n], new_m[n], new_v[n] = _adamw_halves(wloc[n], mine, theirs, mloc[n], vloc[n], place,
                                                                "adamw_" + n)
    grads["conv_w"] = gsm["conv_w"]
    deltas["conv_w"], new_m["conv_w"], new_v["conv_w"] = _adamw(wloc["conv_w"], gsm["conv_w"], mloc["conv_w"],
                                                                vloc["conv_w"], "adamw_conv_w")
    sm_shapes = [wloc[n].shape for n in SMALL]
    pd, pm, pv = _adamw(_pack_small([wloc[n] for n in SMALL]), _pack_small([gsm[n] for n in SMALL]),
                        _pack_small([mloc[n] for n in SMALL]), _pack_small([vloc[n] for n in SMALL]),
                        "adamw_small")
    for dst, packed in ((deltas, pd), (new_m, pm), (new_v, pv)):
        flat, off = packed.reshape(-1), 0
        for n, shp in zip(SMALL, sm_shapes):
            dst[n] = flat[off:off + shp[0] * shp[1]].reshape(shp)
            off += shp[0] * shp[1]
    for n in SMALL:
        grads[n] = gsm[n]

    def like(n, a):
        return a.reshape(args[n].shape)

    outs = [loss_out.reshape(()), grad_x.reshape(x.shape)]
    for group in (grads, deltas, new_m, new_v):
        outs += [like(n, group[n]) for n in WEIGHTS]
    return tuple(outs)
```
